```python
import math
import jax, jax.numpy as jnp
from jax import lax
import numpy as np

D_MODEL = 1024
BATCH = 8
SEQ = 8192
DEPTH = 2

MEM_LEN = 256
N_MIXERS = 4
D_GROUP = D_MODEL // N_MIXERS
N_IN_COLS = 6 * D_GROUP
S5_CH_PER_GROUP = 16
S5_GROUPS = D_GROUP // S5_CH_PER_GROUP
S5_STATE = 64
S5_DT_MIN = 1e-3
S5_DT_MAX = 1e-1
CONV_WIDTH = 31
CONV_GN_GROUPS = 4
LRU_HEADS = 4
LRU_HEAD_DIM = D_GROUP // LRU_HEADS
LRU_CONV_WIDTH = 4
LRU_C = 8.0
ATTN_HEADS = 4
ATTN_HEAD_DIM = D_GROUP // ATTN_HEADS
D_FF = ((8 * D_MODEL // 3 + 127) // 128) * 128
FFN_CONV_WIDTH = 3
DEEPNORM_ALPHA = (2 * DEPTH) ** 0.25
DEEPNORM_BETA = (8 * DEPTH) ** -0.25
LN_EPS = 1e-5

kernel_name = "hymba_style_s5_conformer_rglru_memxattn_deepnorm"

F32 = jnp.float32


def layer_norm(x, g, b):
    xf = x.astype(F32)
    mu = jnp.mean(xf, -1, keepdims=True)
    var = jnp.mean(jnp.square(xf - mu), -1, keepdims=True)
    return ((xf - mu) * lax.rsqrt(var + LN_EPS) * g.astype(F32) + b.astype(F32)).astype(x.dtype)


def group_norm(x, g, b, groups):
    lead = x.shape[:-1]
    c = x.shape[-1]
    xf = x.astype(F32).reshape(lead + (groups, c // groups))
    mu = jnp.mean(xf, -1, keepdims=True)
    var = jnp.mean(jnp.square(xf - mu), -1, keepdims=True)
    xn = ((xf - mu) * lax.rsqrt(var + LN_EPS)).reshape(lead + (c,))
    return xn * g.astype(F32) + b.astype(F32)


def causal_dwconv(x, w, b):
    k, c = w.shape
    y = lax.conv_general_dilated(
        x, w.astype(x.dtype)[:, None, :], window_strides=(1,), padding=[(k - 1, 0)],
        dimension_numbers=("NWC", "WIO", "NWC"), feature_group_count=c)
    return y + b.astype(x.dtype)


def linear_scan(a, b):
    def combine(l, r):
        a1, b1 = l
        a2, b2 = r
        return a1 * a2, a2 * b1 + b2
    _, h = lax.associative_scan(combine, (a, b), axis=1)
    return h


def s5_mixer(u, lam_re, lam_im, log_dt, b_re, b_im, c_re, c_im, d_skip, w_glu, b_glu):
    lead = u.shape[:-1]
    uf = u.astype(F32).reshape(lead + (S5_GROUPS, S5_CH_PER_GROUP))
    lam = lax.complex(lam_re.astype(F32), lam_im.astype(F32))
    dt = jnp.exp(log_dt.astype(F32))[:, None]
    lam_bar = jnp.exp(lam * dt)
    bmat = lax.complex(b_re.astype(F32), b_im.astype(F32))
    b_bar = ((lam_bar - 1.0) / lam)[..., None] * bmat
    bu = jnp.einsum("bsgc,gpc->bsgp", uf, b_bar)
    states = linear_scan(jnp.broadcast_to(lam_bar, bu.shape), bu)
    cmat = lax.complex(c_re.astype(F32), c_im.astype(F32))
    y = jnp.einsum("bsgp,gcp->bsgc", states, cmat).real
    y = y + d_skip.astype(F32).reshape(S5_GROUPS, S5_CH_PER_GROUP) * uf
    y = jax.nn.gelu(y.reshape(lead + (D_GROUP,)))
    return y * jax.nn.sigmoid(y @ w_glu.astype(F32) + b_glu.astype(F32))


def conformer_conv(v, g, conv_w, conv_b, gn_g, gn_b, w_pw, b_pw):
    h = v * jax.nn.sigmoid(g)
    h = causal_dwconv(h, conv_w, conv_b)
    h = jax.nn.silu(group_norm(h, gn_g, gn_b, CONV_GN_GROUPS))
    return h @ w_pw.astype(F32) + b_pw.astype(F32)


def rglru_branch(xg, xr, conv_w, conv_b, w_r, b_r, w_i, b_i, lam):
    gate = jax.nn.gelu(xg.astype(F32))
    xc = causal_dwconv(xr, conv_w, conv_b).astype(F32)
    lead = xc.shape[:-1]
    xh = xc.reshape(lead + (LRU_HEADS, LRU_HEAD_DIM))
    r = jax.nn.sigmoid(jnp.einsum("bshi,hij->bshj", xh, w_r.astype(F32)).reshape(lead + (D_GROUP,)) + b_r.astype(F32))
    i = jax.nn.sigmoid(jnp.einsum("bshi,hij->bshj", xh, w_i.astype(F32)).reshape(lead + (D_GROUP,)) + b_i.astype(F32))
    log_a = -LRU_C * r * jax.nn.softplus(-lam.astype(F32))
    a = jnp.exp(log_a)
    bvals = jnp.sqrt(-jnp.expm1(2.0 * log_a)) * (i * xc)
    h = linear_scan(a, bvals)
    return h * gate


def memory_cross_attention(q, mem, w_kv):
    lead = q.shape[:-1]
    qh = q.astype(F32).reshape(lead + (ATTN_HEADS, ATTN_HEAD_DIM))
    kv = mem.astype(F32) @ w_kv.astype(F32)
    k, v = jnp.split(kv, 2, axis=-1)
    k = k.reshape(k.shape[:-1] + (ATTN_HEADS, ATTN_HEAD_DIM))
    v = v.reshape(v.shape[:-1] + (ATTN_HEADS, ATTN_HEAD_DIM))
    s = jnp.einsum("bshd,bmhd->bhsm", qh, k) * (ATTN_HEAD_DIM ** -0.5)
    p = jax.nn.softmax(s, axis=-1)
    o = jnp.einsum("bhsm,bmhd->bshd", p, v)
    return o.reshape(lead + (D_GROUP,))


def conv_ffn(x, w_up, conv_w, conv_b, w_down):
    u = x @ w_up
    u = causal_dwconv(u, conv_w, conv_b)
    val, gt = jnp.split(u, 2, axis=-1)
    h = val.astype(F32) * jax.nn.gelu(gt.astype(F32))
    return h @ w_down.astype(F32)


def _fwd_setup_inputs(seed: int = 0) -> dict:
    key = jax.random.key(seed)
    keys = iter(jax.random.split(key, 64))
    L = DEPTH

    def nrm(shape, scale):
        return jax.random.normal(next(keys), shape, F32) * scale

    def gain(shape):
        return 1.0 + nrm(shape, 0.02)

    d = {}
    d["x"] = nrm((BATCH, SEQ, D_MODEL), 1.0)
    d["mem"] = nrm((BATCH, MEM_LEN, D_MODEL), 1.0)
    d["ln_in_g"] = gain((D_MODEL,))
    d["ln_in_b"] = nrm((D_MODEL,), 0.02)
    d["w_in"] = nrm((L, D_MODEL, N_IN_COLS), D_MODEL ** -0.5)
    d["b_in"] = nrm((L, N_IN_COLS), 0.01)
    d["s5_lam_re"] = -0.5 + nrm((L, S5_GROUPS, S5_STATE), 0.01)
    d["s5_lam_im"] = math.pi * jnp.arange(S5_STATE, dtype=F32) + nrm((L, S5_GROUPS, S5_STATE), 0.01)
    d["s5_log_dt"] = jax.random.uniform(next(keys), (L, S5_GROUPS), F32, math.log(S5_DT_MIN), math.log(S5_DT_MAX))
    d["s5_b_re"] = nrm((L, S5_GROUPS, S5_STATE, S5_CH_PER_GROUP), (2 * S5_CH_PER_GROUP) ** -0.5)
    d["s5_b_im"] = nrm((L, S5_GROUPS, S5_STATE, S5_CH_PER_GROUP), (2 * S5_CH_PER_GROUP) ** -0.5)
    d["s5_c_re"] = nrm((L, S5_GROUPS, S5_CH_PER_GROUP, S5_STATE), (2 * S5_STATE) ** -0.5)
    d["s5_c_im"] = nrm((L, S5_GROUPS, S5_CH_PER_GROUP, S5_STATE), (2 * S5_STATE) ** -0.5)
    d["s5_d"] = nrm((L, D_GROUP), 1.0)
    d["s5_w_glu"] = nrm((L, D_GROUP, D_GROUP), D_GROUP ** -0.5)
    d["s5_b_glu"] = nrm((L, D_GROUP), 0.01)
    d["cv_w"] = nrm((L, CONV_WIDTH, D_GROUP), CONV_WIDTH ** -0.5)
    d["cv_b"] = nrm((L, D_GROUP), 0.01)
    d["cv_gn_g"] = gain((L, D_GROUP))
    d["cv_gn_b"] = nrm((L, D_GROUP), 0.02)
    d["cv_w_pw"] = nrm((L, D_GROUP, D_GROUP), D_GROUP ** -0.5)
    d["cv_b_pw"] = nrm((L, D_GROUP), 0.01)
    d["lru_conv_w"] = nrm((L, LRU_CONV_WIDTH, D_GROUP), LRU_CONV_WIDTH ** -0.5)
    d["lru_conv_b"] = nrm((L, D_GROUP), 0.01)
    d["lru_w_r"] = nrm((L, LRU_HEADS, LRU_HEAD_DIM, LRU_HEAD_DIM), LRU_HEAD_DIM ** -0.5)
    d["lru_b_r"] = nrm((L, D_GROUP), 0.01)
    d["lru_w_i"] = nrm((L, LRU_HEADS, LRU_HEAD_DIM, LRU_HEAD_DIM), LRU_HEAD_DIM ** -0.5)
    d["lru_b_i"] = nrm((L, D_GROUP), 0.01)
    a_c = jax.random.uniform(next(keys), (L, D_GROUP), F32, 0.9, 0.999)
    a0 = a_c ** (1.0 / LRU_C)
    d["lru_lam"] = jnp.log(a0) - jnp.log1p(-a0)
    d["attn_w_kv"] = nrm((L, D_MODEL, 2 * D_GROUP), D_MODEL ** -0.5)
    d["w_out"] = nrm((L, D_MODEL, D_MODEL), D_MODEL ** -0.5 * DEEPNORM_BETA)
    d["b_out"] = nrm((L, D_MODEL), 0.01)
    d["ln1_g"] = gain((L, D_MODEL))
    d["ln1_b"] = nrm((L, D_MODEL), 0.02)
    d["ffn_w_up"] = nrm((L, D_MODEL, 2 * D_FF), D_MODEL ** -0.5)
    d["ffn_conv_w"] = nrm((L, FFN_CONV_WIDTH, 2 * D_FF), FFN_CONV_WIDTH ** -0.5)
    d["ffn_conv_b"] = nrm((L, 2 * D_FF), 0.01)
    d["ffn_w_down"] = nrm((L, D_FF, D_MODEL), D_FF ** -0.5 * DEEPNORM_BETA)
    d["ln2_g"] = gain((L, D_MODEL))
    d["ln2_b"] = nrm((L, D_MODEL), 0.02)
    return d


def _fwd_reference(x, mem, ln_in_g, ln_in_b, w_in, b_in,
              s5_lam_re, s5_lam_im, s5_log_dt, s5_b_re, s5_b_im, s5_c_re, s5_c_im, s5_d, s5_w_glu, s5_b_glu,
              cv_w, cv_b, cv_gn_g, cv_gn_b, cv_w_pw, cv_b_pw,
              lru_conv_w, lru_conv_b, lru_w_r, lru_b_r, lru_w_i, lru_b_i, lru_lam,
              attn_w_kv, w_out, b_out, ln1_g, ln1_b,
              ffn_w_up, ffn_conv_w, ffn_conv_b, ffn_w_down, ln2_g, ln2_b):
    x = layer_norm(x, ln_in_g, ln_in_b)
    for l in range(DEPTH):
        h = x @ w_in[l] + b_in[l]
        s5_u, cv_v, cv_g, lru_g, lru_x, q = jnp.split(h, 6, axis=-1)
        y_s5 = s5_mixer(s5_u, s5_lam_re[l], s5_lam_im[l], s5_log_dt[l], s5_b_re[l], s5_b_im[l],
                        s5_c_re[l], s5_c_im[l], s5_d[l], s5_w_glu[l], s5_b_glu[l])
        y_cv = conformer_conv(cv_v, cv_g, cv_w[l], cv_b[l], cv_gn_g[l], cv_gn_b[l], cv_w_pw[l], cv_b_pw[l])
        y_lru = rglru_branch(lru_g, lru_x, lru_conv_w[l], lru_conv_b[l], lru_w_r[l], lru_b_r[l],
                             lru_w_i[l], lru_b_i[l], lru_lam[l])
        y_mem = memory_cross_attention(q, mem, attn_w_kv[l])
        mix = jnp.concatenate([y_s5, y_cv, y_lru, y_mem], axis=-1)
        mix = mix @ w_out[l].astype(F32) + b_out[l].astype(F32)
        x = layer_norm(DEEPNORM_ALPHA * x + mix, ln1_g[l], ln1_b[l])
        f = conv_ffn(x, ffn_w_up[l], ffn_conv_w[l], ffn_conv_b[l], ffn_w_down[l])
        x = layer_norm(DEEPNORM_ALPHA * x + f, ln2_g[l], ln2_b[l])
    return x


import jax as _jax
import jax.numpy as _jnp

TWIN_FORMAT = 'train_step'
FWD_PARAMS = ['x', 'mem', 'ln_in_g', 'ln_in_b', 'w_in', 'b_in', 's5_lam_re', 's5_lam_im', 's5_log_dt', 's5_b_re', 's5_b_im', 's5_c_re', 's5_c_im', 's5_d', 's5_w_glu', 's5_b_glu', 'cv_w', 'cv_b', 'cv_gn_g', 'cv_gn_b', 'cv_w_pw', 'cv_b_pw', 'lru_conv_w', 'lru_conv_b', 'lru_w_r', 'lru_b_r', 'lru_w_i', 'lru_b_i', 'lru_lam', 'attn_w_kv', 'w_out', 'b_out', 'ln1_g', 'ln1_b', 'ffn_w_up', 'ffn_conv_w', 'ffn_conv_b', 'ffn_w_down', 'ln2_g', 'ln2_b']
TWIN_WEIGHTS = ['ln_in_g', 'ln_in_b', 'w_in', 'b_in', 's5_lam_re', 's5_lam_im', 's5_log_dt', 's5_b_re', 's5_b_im', 's5_c_re', 's5_c_im', 's5_d', 's5_w_glu', 's5_b_glu', 'cv_w', 'cv_b', 'cv_gn_g', 'cv_gn_b', 'cv_w_pw', 'cv_b_pw', 'lru_conv_w', 'lru_conv_b', 'lru_w_r', 'lru_b_r', 'lru_w_i', 'lru_b_i', 'lru_lam', 'attn_w_kv', 'w_out', 'b_out', 'ln1_g', 'ln1_b', 'ffn_w_up', 'ffn_conv_w', 'ffn_conv_b', 'ffn_w_down', 'ln2_g', 'ln2_b']
TWIN_DIFF_INPUT = 'x'
TWIN_INPUTS = ['x', 'mem', 'ln_in_g', 'ln_in_b', 'w_in', 'b_in', 's5_lam_re', 's5_lam_im', 's5_log_dt', 's5_b_re', 's5_b_im', 's5_c_re', 's5_c_im', 's5_d', 's5_w_glu', 's5_b_glu', 'cv_w', 'cv_b', 'cv_gn_g', 'cv_gn_b', 'cv_w_pw', 'cv_b_pw', 'lru_conv_w', 'lru_conv_b', 'lru_w_r', 'lru_b_r', 'lru_w_i', 'lru_b_i', 'lru_lam', 'attn_w_kv', 'w_out', 'b_out', 'ln1_g', 'ln1_b', 'ffn_w_up', 'ffn_conv_w', 'ffn_conv_b', 'ffn_w_down', 'ln2_g', 'ln2_b', 'loss_target', 'm_ln_in_g', 'm_ln_in_b', 'm_w_in', 'm_b_in', 'm_s5_lam_re', 'm_s5_lam_im', 'm_s5_log_dt', 'm_s5_b_re', 'm_s5_b_im', 'm_s5_c_re', 'm_s5_c_im', 'm_s5_d', 'm_s5_w_glu', 'm_s5_b_glu', 'm_cv_w', 'm_cv_b', 'm_cv_gn_g', 'm_cv_gn_b', 'm_cv_w_pw', 'm_cv_b_pw', 'm_lru_conv_w', 'm_lru_conv_b', 'm_lru_w_r', 'm_lru_b_r', 'm_lru_w_i', 'm_lru_b_i', 'm_lru_lam', 'm_attn_w_kv', 'm_w_out', 'm_b_out', 'm_ln1_g', 'm_ln1_b', 'm_ffn_w_up', 'm_ffn_conv_w', 'm_ffn_conv_b', 'm_ffn_w_down', 'm_ln2_g', 'm_ln2_b', 'v_ln_in_g', 'v_ln_in_b', 'v_w_in', 'v_b_in', 'v_s5_lam_re', 'v_s5_lam_im', 'v_s5_log_dt', 'v_s5_b_re', 'v_s5_b_im', 'v_s5_c_re', 'v_s5_c_im', 'v_s5_d', 'v_s5_w_glu', 'v_s5_b_glu', 'v_cv_w', 'v_cv_b', 'v_cv_gn_g', 'v_cv_gn_b', 'v_cv_w_pw', 'v_cv_b_pw', 'v_lru_conv_w', 'v_lru_conv_b', 'v_lru_w_r', 'v_lru_b_r', 'v_lru_w_i', 'v_lru_b_i', 'v_lru_lam', 'v_attn_w_kv', 'v_w_out', 'v_b_out', 'v_ln1_g', 'v_ln1_b', 'v_ffn_w_up', 'v_ffn_conv_w', 'v_ffn_conv_b', 'v_ffn_w_down', 'v_ln2_g', 'v_ln2_b']
TWIN_OUTPUTS = ['loss', 'grad_x', 'grad_ln_in_g', 'grad_ln_in_b', 'grad_w_in', 'grad_b_in', 'grad_s5_lam_re', 'grad_s5_lam_im', 'grad_s5_log_dt', 'grad_s5_b_re', 'grad_s5_b_im', 'grad_s5_c_re', 'grad_s5_c_im', 'grad_s5_d', 'grad_s5_w_glu', 'grad_s5_b_glu', 'grad_cv_w', 'grad_cv_b', 'grad_cv_gn_g', 'grad_cv_gn_b', 'grad_cv_w_pw', 'grad_cv_b_pw', 'grad_lru_conv_w', 'grad_lru_conv_b', 'grad_lru_w_r', 'grad_lru_b_r', 'grad_lru_w_i', 'grad_lru_b_i', 'grad_lru_lam', 'grad_attn_w_kv', 'grad_w_out', 'grad_b_out', 'grad_ln1_g', 'grad_ln1_b', 'grad_ffn_w_up', 'grad_ffn_conv_w', 'grad_ffn_conv_b', 'grad_ffn_w_down', 'grad_ln2_g', 'grad_ln2_b', 'delta_ln_in_g', 'delta_ln_in_b', 'delta_w_in', 'delta_b_in', 'delta_s5_lam_re', 'delta_s5_lam_im', 'delta_s5_log_dt', 'delta_s5_b_re', 'delta_s5_b_im', 'delta_s5_c_re', 'delta_s5_c_im', 'delta_s5_d', 'delta_s5_w_glu', 'delta_s5_b_glu', 'delta_cv_w', 'delta_cv_b', 'delta_cv_gn_g', 'delta_cv_gn_b', 'delta_cv_w_pw', 'delta_cv_b_pw', 'delta_lru_conv_w', 'delta_lru_conv_b', 'delta_lru_w_r', 'delta_lru_b_r', 'delta_lru_w_i', 'delta_lru_b_i', 'delta_lru_lam', 'delta_attn_w_kv', 'delta_w_out', 'delta_b_out', 'delta_ln1_g', 'delta_ln1_b', 'delta_ffn_w_up', 'delta_ffn_conv_w', 'delta_ffn_conv_b', 'delta_ffn_w_down', 'delta_ln2_g', 'delta_ln2_b', 'new_m_ln_in_g', 'new_m_ln_in_b', 'new_m_w_in', 'new_m_b_in', 'new_m_s5_lam_re', 'new_m_s5_lam_im', 'new_m_s5_log_dt', 'new_m_s5_b_re', 'new_m_s5_b_im', 'new_m_s5_c_re', 'new_m_s5_c_im', 'new_m_s5_d', 'new_m_s5_w_glu', 'new_m_s5_b_glu', 'new_m_cv_w', 'new_m_cv_b', 'new_m_cv_gn_g', 'new_m_cv_gn_b', 'new_m_cv_w_pw', 'new_m_cv_b_pw', 'new_m_lru_conv_w', 'new_m_lru_conv_b', 'new_m_lru_w_r', 'new_m_lru_b_r', 'new_m_lru_w_i', 'new_m_lru_b_i', 'new_m_lru_lam', 'new_m_attn_w_kv', 'new_m_w_out', 'new_m_b_out', 'new_m_ln1_g', 'new_m_ln1_b', 'new_m_ffn_w_up', 'new_m_ffn_conv_w', 'new_m_ffn_conv_b', 'new_m_ffn_w_down', 'new_m_ln2_g', 'new_m_ln2_b', 'new_v_ln_in_g', 'new_v_ln_in_b', 'new_v_w_in', 'new_v_b_in', 'new_v_s5_lam_re', 'new_v_s5_lam_im', 'new_v_s5_log_dt', 'new_v_s5_b_re', 'new_v_s5_b_im', 'new_v_s5_c_re', 'new_v_s5_c_im', 'new_v_s5_d', 'new_v_s5_w_glu', 'new_v_s5_b_glu', 'new_v_cv_w', 'new_v_cv_b', 'new_v_cv_gn_g', 'new_v_cv_gn_b', 'new_v_cv_w_pw', 'new_v_cv_b_pw', 'new_v_lru_conv_w', 'new_v_lru_conv_b', 'new_v_lru_w_r', 'new_v_lru_b_r', 'new_v_lru_w_i', 'new_v_lru_b_i', 'new_v_lru_lam', 'new_v_attn_w_kv', 'new_v_w_out', 'new_v_b_out', 'new_v_ln1_g', 'new_v_ln1_b', 'new_v_ffn_w_up', 'new_v_ffn_conv_w', 'new_v_ffn_conv_b', 'new_v_ffn_w_down', 'new_v_ln2_g', 'new_v_ln2_b']
TWIN_LEAF_KINDS = {'loss': 'loss', 'grad_x': 'grad_x', 'grad_ln_in_g': 'grad_w', 'grad_ln_in_b': 'grad_w', 'grad_w_in': 'grad_w', 'grad_b_in': 'grad_w', 'grad_s5_lam_re': 'grad_w', 'grad_s5_lam_im': 'grad_w', 'grad_s5_log_dt': 'grad_w', 'grad_s5_b_re': 'grad_w', 'grad_s5_b_im': 'grad_w', 'grad_s5_c_re': 'grad_w', 'grad_s5_c_im': 'grad_w', 'grad_s5_d': 'grad_w', 'grad_s5_w_glu': 'grad_w', 'grad_s5_b_glu': 'grad_w', 'grad_cv_w': 'grad_w', 'grad_cv_b': 'grad_w', 'grad_cv_gn_g': 'grad_w', 'grad_cv_gn_b': 'grad_w', 'grad_cv_w_pw': 'grad_w', 'grad_cv_b_pw': 'grad_w', 'grad_lru_conv_w': 'grad_w', 'grad_lru_conv_b': 'grad_w', 'grad_lru_w_r': 'grad_w', 'grad_lru_b_r': 'grad_w', 'grad_lru_w_i': 'grad_w', 'grad_lru_b_i': 'grad_w', 'grad_lru_lam': 'grad_w', 'grad_attn_w_kv': 'grad_w', 'grad_w_out': 'grad_w', 'grad_b_out': 'grad_w', 'grad_ln1_g': 'grad_w', 'grad_ln1_b': 'grad_w', 'grad_ffn_w_up': 'grad_w', 'grad_ffn_conv_w': 'grad_w', 'grad_ffn_conv_b': 'grad_w', 'grad_ffn_w_down': 'grad_w', 'grad_ln2_g': 'grad_w', 'grad_ln2_b': 'grad_w', 'delta_ln_in_g': 'delta_w', 'delta_ln_in_b': 'delta_w', 'delta_w_in': 'delta_w', 'delta_b_in': 'delta_w', 'delta_s5_lam_re': 'delta_w', 'delta_s5_lam_im': 'delta_w', 'delta_s5_log_dt': 'delta_w', 'delta_s5_b_re': 'delta_w', 'delta_s5_b_im': 'delta_w', 'delta_s5_c_re': 'delta_w', 'delta_s5_c_im': 'delta_w', 'delta_s5_d': 'delta_w', 'delta_s5_w_glu': 'delta_w', 'delta_s5_b_glu': 'delta_w', 'delta_cv_w': 'delta_w', 'delta_cv_b': 'delta_w', 'delta_cv_gn_g': 'delta_w', 'delta_cv_gn_b': 'delta_w', 'delta_cv_w_pw': 'delta_w', 'delta_cv_b_pw': 'delta_w', 'delta_lru_conv_w': 'delta_w', 'delta_lru_conv_b': 'delta_w', 'delta_lru_w_r': 'delta_w', 'delta_lru_b_r': 'delta_w', 'delta_lru_w_i': 'delta_w', 'delta_lru_b_i': 'delta_w', 'delta_lru_lam': 'delta_w', 'delta_attn_w_kv': 'delta_w', 'delta_w_out': 'delta_w', 'delta_b_out': 'delta_w', 'delta_ln1_g': 'delta_w', 'delta_ln1_b': 'delta_w', 'delta_ffn_w_up': 'delta_w', 'delta_ffn_conv_w': 'delta_w', 'delta_ffn_conv_b': 'delta_w', 'delta_ffn_w_down': 'delta_w', 'delta_ln2_g': 'delta_w', 'delta_ln2_b': 'delta_w', 'new_m_ln_in_g': 'new_m', 'new_m_ln_in_b': 'new_m', 'new_m_w_in': 'new_m', 'new_m_b_in': 'new_m', 'new_m_s5_lam_re': 'new_m', 'new_m_s5_lam_im': 'new_m', 'new_m_s5_log_dt': 'new_m', 'new_m_s5_b_re': 'new_m', 'new_m_s5_b_im': 'new_m', 'new_m_s5_c_re': 'new_m', 'new_m_s5_c_im': 'new_m', 'new_m_s5_d': 'new_m', 'new_m_s5_w_glu': 'new_m', 'new_m_s5_b_glu': 'new_m', 'new_m_cv_w': 'new_m', 'new_m_cv_b': 'new_m', 'new_m_cv_gn_g': 'new_m', 'new_m_cv_gn_b': 'new_m', 'new_m_cv_w_pw': 'new_m', 'new_m_cv_b_pw': 'new_m', 'new_m_lru_conv_w': 'new_m', 'new_m_lru_conv_b': 'new_m', 'new_m_lru_w_r': 'new_m', 'new_m_lru_b_r': 'new_m', 'new_m_lru_w_i': 'new_m', 'new_m_lru_b_i': 'new_m', 'new_m_lru_lam': 'new_m', 'new_m_attn_w_kv': 'new_m', 'new_m_w_out': 'new_m', 'new_m_b_out': 'new_m', 'new_m_ln1_g': 'new_m', 'new_m_ln1_b': 'new_m', 'new_m_ffn_w_up': 'new_m', 'new_m_ffn_conv_w': 'new_m', 'new_m_ffn_conv_b': 'new_m', 'new_m_ffn_w_down': 'new_m', 'new_m_ln2_g': 'new_m', 'new_m_ln2_b': 'new_m', 'new_v_ln_in_g': 'new_v', 'new_v_ln_in_b': 'new_v', 'new_v_w_in': 'new_v', 'new_v_b_in': 'new_v', 'new_v_s5_lam_re': 'new_v', 'new_v_s5_lam_im': 'new_v', 'new_v_s5_log_dt': 'new_v', 'new_v_s5_b_re': 'new_v', 'new_v_s5_b_im': 'new_v', 'new_v_s5_c_re': 'new_v', 'new_v_s5_c_im': 'new_v', 'new_v_s5_d': 'new_v', 'new_v_s5_w_glu': 'new_v', 'new_v_s5_b_glu': 'new_v', 'new_v_cv_w': 'new_v', 'new_v_cv_b': 'new_v', 'new_v_cv_gn_g': 'new_v', 'new_v_cv_gn_b': 'new_v', 'new_v_cv_w_pw': 'new_v', 'new_v_cv_b_pw': 'new_v', 'new_v_lru_conv_w': 'new_v', 'new_v_lru_conv_b': 'new_v', 'new_v_lru_w_r': 'new_v', 'new_v_lru_b_r': 'new_v', 'new_v_lru_w_i': 'new_v', 'new_v_lru_b_i': 'new_v', 'new_v_lru_lam': 'new_v', 'new_v_attn_w_kv': 'new_v', 'new_v_w_out': 'new_v', 'new_v_b_out': 'new_v', 'new_v_ln1_g': 'new_v', 'new_v_ln1_b': 'new_v', 'new_v_ffn_w_up': 'new_v', 'new_v_ffn_conv_w': 'new_v', 'new_v_ffn_conv_b': 'new_v', 'new_v_ffn_w_down': 'new_v', 'new_v_ln2_g': 'new_v', 'new_v_ln2_b': 'new_v'}


def _forward(args):
    return _fwd_reference(*[args[k] for k in FWD_PARAMS])


def _output_shape():
    def fwd():
        inp = _fwd_setup_inputs(0)
        return _fwd_reference(*[inp[k] for k in FWD_PARAMS])
    out = _jax.eval_shape(fwd)
    return out.shape, out.dtype

N_MICROBATCH = 1
ADAM_LR = 0.001
ADAM_B1 = 0.9
ADAM_B2 = 0.999
ADAM_EPS = 1e-08
ADAM_WD = 0.01
ADAM_STEP = 10
PER_EXAMPLE_BATCH_AXIS = {'x': 0, 'mem': 0, 'loss_target': 0}
SHARED_INPUTS = []
_WEIGHT_DTYPES = {'ln_in_g': _jnp.float32, 'ln_in_b': _jnp.float32, 'w_in': _jnp.float32, 'b_in': _jnp.float32, 's5_lam_re': _jnp.float32, 's5_lam_im': _jnp.float32, 's5_log_dt': _jnp.float32, 's5_b_re': _jnp.float32, 's5_b_im': _jnp.float32, 's5_c_re': _jnp.float32, 's5_c_im': _jnp.float32, 's5_d': _jnp.float32, 's5_w_glu': _jnp.float32, 's5_b_glu': _jnp.float32, 'cv_w': _jnp.float32, 'cv_b': _jnp.float32, 'cv_gn_g': _jnp.float32, 'cv_gn_b': _jnp.float32, 'cv_w_pw': _jnp.float32, 'cv_b_pw': _jnp.float32, 'lru_conv_w': _jnp.float32, 'lru_conv_b': _jnp.float32, 'lru_w_r': _jnp.float32, 'lru_b_r': _jnp.float32, 'lru_w_i': _jnp.float32, 'lru_b_i': _jnp.float32, 'lru_lam': _jnp.float32, 'attn_w_kv': _jnp.float32, 'w_out': _jnp.float32, 'b_out': _jnp.float32, 'ln1_g': _jnp.float32, 'ln1_b': _jnp.float32, 'ffn_w_up': _jnp.float32, 'ffn_conv_w': _jnp.float32, 'ffn_conv_b': _jnp.float32, 'ffn_w_down': _jnp.float32, 'ln2_g': _jnp.float32, 'ln2_b': _jnp.float32}
MOMENT_SCALE = {'ln_in_g': 1.920074e+00, 'ln_in_b': 1.059121e+00, 'w_in': 3.976989e-02, 'b_in': 2.891262e-01, 's5_lam_re': 1.637942e-03, 's5_lam_im': 2.021671e-03, 's5_log_dt': 7.782686e-01, 's5_b_re': 1.139072e-03, 's5_b_im': 1.100510e-03, 's5_c_re': 2.213806e-03, 's5_c_im': 2.192344e-03, 's5_d': 6.510064e-02, 's5_w_glu': 1.149018e-02, 's5_b_glu': 2.313895e-02, 'cv_w': 6.000643e-02, 'cv_b': 3.058440e-01, 'cv_gn_g': 1.362937e-01, 'cv_gn_b': 1.869904e-01, 'cv_w_pw': 7.707696e-02, 'cv_b_pw': 3.497615e-01, 'lru_conv_w': 5.433980e-02, 'lru_conv_b': 7.590582e-01, 'lru_w_r': 2.692948e-02, 'lru_b_r': 2.000360e-02, 'lru_w_i': 5.033531e-02, 'lru_b_i': 1.863571e-02, 'lru_lam': 3.493094e-02, 'attn_w_kv': 1.062562e-02, 'w_out': 1.143838e-01, 'b_out': 7.002783e-01, 'ln1_g': 2.070616e+00, 'ln1_b': 1.003067e+00, 'ffn_w_up': 3.512826e-02, 'ffn_conv_w': 3.518715e-02, 'ffn_conv_b': 4.776717e-02, 'ffn_w_down': 1.153116e-01, 'ln2_g': 4.534528e+01, 'ln2_b': 3.528738e+00}


def _to_microbatches(a, axis):
    t = _jnp.moveaxis(a, axis, 0)
    t = t.reshape((N_MICROBATCH, t.shape[0] // N_MICROBATCH) + t.shape[1:])
    return _jnp.moveaxis(t, 1, axis + 1)


def setup_inputs(seed: int = 0) -> dict:
    inp = _fwd_setup_inputs(seed)
    key = _jax.random.fold_in(_jax.random.key(seed), 7919)
    shape, _ = _output_shape()
    out = dict(inp)
    out["loss_target"] = _jax.random.normal(_jax.random.fold_in(key, 0), shape, _jnp.float32)
    for i, name in enumerate(TWIN_WEIGHTS):
        w = inp[name].astype(_jnp.float32)
        if MOMENT_SCALE is None:
            s = _jnp.sqrt(_jnp.mean(_jnp.square(w)) + 1e-30)
        else:
            s = MOMENT_SCALE[name]
        km, kv = _jax.random.split(_jax.random.fold_in(key, i + 1))
        out[name] = w
        out["m_" + name] = s * _jax.random.normal(km, w.shape, _jnp.float32)
        out["v_" + name] = (s * s) * _jax.random.uniform(kv, w.shape, _jnp.float32, 0.5, 1.5)
    if N_MICROBATCH > 1:
        for name, axis in PER_EXAMPLE_BATCH_AXIS.items():
            out[name] = _to_microbatches(out[name], axis)
    return {'x': out['x'], 'mem': out['mem'], 'ln_in_g': out['ln_in_g'], 'ln_in_b': out['ln_in_b'], 'w_in': out['w_in'], 'b_in': out['b_in'], 's5_lam_re': out['s5_lam_re'], 's5_lam_im': out['s5_lam_im'], 's5_log_dt': out['s5_log_dt'], 's5_b_re': out['s5_b_re'], 's5_b_im': out['s5_b_im'], 's5_c_re': out['s5_c_re'], 's5_c_im': out['s5_c_im'], 's5_d': out['s5_d'], 's5_w_glu': out['s5_w_glu'], 's5_b_glu': out['s5_b_glu'], 'cv_w': out['cv_w'], 'cv_b': out['cv_b'], 'cv_gn_g': out['cv_gn_g'], 'cv_gn_b': out['cv_gn_b'], 'cv_w_pw': out['cv_w_pw'], 'cv_b_pw': out['cv_b_pw'], 'lru_conv_w': out['lru_conv_w'], 'lru_conv_b': out['lru_conv_b'], 'lru_w_r': out['lru_w_r'], 'lru_b_r': out['lru_b_r'], 'lru_w_i': out['lru_w_i'], 'lru_b_i': out['lru_b_i'], 'lru_lam': out['lru_lam'], 'attn_w_kv': out['attn_w_kv'], 'w_out': out['w_out'], 'b_out': out['b_out'], 'ln1_g': out['ln1_g'], 'ln1_b': out['ln1_b'], 'ffn_w_up': out['ffn_w_up'], 'ffn_conv_w': out['ffn_conv_w'], 'ffn_conv_b': out['ffn_conv_b'], 'ffn_w_down': out['ffn_w_down'], 'ln2_g': out['ln2_g'], 'ln2_b': out['ln2_b'], 'loss_target': out['loss_target'], 'm_ln_in_g': out['m_ln_in_g'], 'm_ln_in_b': out['m_ln_in_b'], 'm_w_in': out['m_w_in'], 'm_b_in': out['m_b_in'], 'm_s5_lam_re': out['m_s5_lam_re'], 'm_s5_lam_im': out['m_s5_lam_im'], 'm_s5_log_dt': out['m_s5_log_dt'], 'm_s5_b_re': out['m_s5_b_re'], 'm_s5_b_im': out['m_s5_b_im'], 'm_s5_c_re': out['m_s5_c_re'], 'm_s5_c_im': out['m_s5_c_im'], 'm_s5_d': out['m_s5_d'], 'm_s5_w_glu': out['m_s5_w_glu'], 'm_s5_b_glu': out['m_s5_b_glu'], 'm_cv_w': out['m_cv_w'], 'm_cv_b': out['m_cv_b'], 'm_cv_gn_g': out['m_cv_gn_g'], 'm_cv_gn_b': out['m_cv_gn_b'], 'm_cv_w_pw': out['m_cv_w_pw'], 'm_cv_b_pw': out['m_cv_b_pw'], 'm_lru_conv_w': out['m_lru_conv_w'], 'm_lru_conv_b': out['m_lru_conv_b'], 'm_lru_w_r': out['m_lru_w_r'], 'm_lru_b_r': out['m_lru_b_r'], 'm_lru_w_i': out['m_lru_w_i'], 'm_lru_b_i': out['m_lru_b_i'], 'm_lru_lam': out['m_lru_lam'], 'm_attn_w_kv': out['m_attn_w_kv'], 'm_w_out': out['m_w_out'], 'm_b_out': out['m_b_out'], 'm_ln1_g': out['m_ln1_g'], 'm_ln1_b': out['m_ln1_b'], 'm_ffn_w_up': out['m_ffn_w_up'], 'm_ffn_conv_w': out['m_ffn_conv_w'], 'm_ffn_conv_b': out['m_ffn_conv_b'], 'm_ffn_w_down': out['m_ffn_w_down'], 'm_ln2_g': out['m_ln2_g'], 'm_ln2_b': out['m_ln2_b'], 'v_ln_in_g': out['v_ln_in_g'], 'v_ln_in_b': out['v_ln_in_b'], 'v_w_in': out['v_w_in'], 'v_b_in': out['v_b_in'], 'v_s5_lam_re': out['v_s5_lam_re'], 'v_s5_lam_im': out['v_s5_lam_im'], 'v_s5_log_dt': out['v_s5_log_dt'], 'v_s5_b_re': out['v_s5_b_re'], 'v_s5_b_im': out['v_s5_b_im'], 'v_s5_c_re': out['v_s5_c_re'], 'v_s5_c_im': out['v_s5_c_im'], 'v_s5_d': out['v_s5_d'], 'v_s5_w_glu': out['v_s5_w_glu'], 'v_s5_b_glu': out['v_s5_b_glu'], 'v_cv_w': out['v_cv_w'], 'v_cv_b': out['v_cv_b'], 'v_cv_gn_g': out['v_cv_gn_g'], 'v_cv_gn_b': out['v_cv_gn_b'], 'v_cv_w_pw': out['v_cv_w_pw'], 'v_cv_b_pw': out['v_cv_b_pw'], 'v_lru_conv_w': out['v_lru_conv_w'], 'v_lru_conv_b': out['v_lru_conv_b'], 'v_lru_w_r': out['v_lru_w_r'], 'v_lru_b_r': out['v_lru_b_r'], 'v_lru_w_i': out['v_lru_w_i'], 'v_lru_b_i': out['v_lru_b_i'], 'v_lru_lam': out['v_lru_lam'], 'v_attn_w_kv': out['v_attn_w_kv'], 'v_w_out': out['v_w_out'], 'v_b_out': out['v_b_out'], 'v_ln1_g': out['v_ln1_g'], 'v_ln1_b': out['v_ln1_b'], 'v_ffn_w_up': out['v_ffn_w_up'], 'v_ffn_conv_w': out['v_ffn_conv_w'], 'v_ffn_conv_b': out['v_ffn_conv_b'], 'v_ffn_w_down': out['v_ffn_w_down'], 'v_ln2_g': out['v_ln2_g'], 'v_ln2_b': out['v_ln2_b']}


def _loss(weights, diff, rest, loss_target):
    with _jax.named_scope("forward"):
        args = {**rest, TWIN_DIFF_INPUT: diff, **{k: w.astype(_WEIGHT_DTYPES[k]) for k, w in weights.items()}}
        y = _forward(args)
    with _jax.named_scope("loss_head"):
        err = _jnp.square(y.astype(_jnp.float32) - loss_target)
        return 0.5 * _jnp.sum(_jnp.mean(err, axis=-1)) if err.ndim else 0.5 * err


def _adamw(w, g, m, v):
    m = ADAM_B1 * m + (1.0 - ADAM_B1) * g
    v = ADAM_B2 * v + (1.0 - ADAM_B2) * _jnp.square(g)
    m_hat = m / (1.0 - ADAM_B1 ** ADAM_STEP)
    v_hat = v / (1.0 - ADAM_B2 ** ADAM_STEP)
    delta = -ADAM_LR * (m_hat / (_jnp.sqrt(v_hat) + ADAM_EPS) + ADAM_WD * w)
    return delta, m, v


def reference(x, mem, ln_in_g, ln_in_b, w_in, b_in, s5_lam_re, s5_lam_im, s5_log_dt, s5_b_re, s5_b_im, s5_c_re, s5_c_im, s5_d, s5_w_glu, s5_b_glu, cv_w, cv_b, cv_gn_g, cv_gn_b, cv_w_pw, cv_b_pw, lru_conv_w, lru_conv_b, lru_w_r, lru_b_r, lru_w_i, lru_b_i, lru_lam, attn_w_kv, w_out, b_out, ln1_g, ln1_b, ffn_w_up, ffn_conv_w, ffn_conv_b, ffn_w_down, ln2_g, ln2_b, loss_target, m_ln_in_g, m_ln_in_b, m_w_in, m_b_in, m_s5_lam_re, m_s5_lam_im, m_s5_log_dt, m_s5_b_re, m_s5_b_im, m_s5_c_re, m_s5_c_im, m_s5_d, m_s5_w_glu, m_s5_b_glu, m_cv_w, m_cv_b, m_cv_gn_g, m_cv_gn_b, m_cv_w_pw, m_cv_b_pw, m_lru_conv_w, m_lru_conv_b, m_lru_w_r, m_lru_b_r, m_lru_w_i, m_lru_b_i, m_lru_lam, m_attn_w_kv, m_w_out, m_b_out, m_ln1_g, m_ln1_b, m_ffn_w_up, m_ffn_conv_w, m_ffn_conv_b, m_ffn_w_down, m_ln2_g, m_ln2_b, v_ln_in_g, v_ln_in_b, v_w_in, v_b_in, v_s5_lam_re, v_s5_lam_im, v_s5_log_dt, v_s5_b_re, v_s5_b_im, v_s5_c_re, v_s5_c_im, v_s5_d, v_s5_w_glu, v_s5_b_glu, v_cv_w, v_cv_b, v_cv_gn_g, v_cv_gn_b, v_cv_w_pw, v_cv_b_pw, v_lru_conv_w, v_lru_conv_b, v_lru_w_r, v_lru_b_r, v_lru_w_i, v_lru_b_i, v_lru_lam, v_attn_w_kv, v_w_out, v_b_out, v_ln1_g, v_ln1_b, v_ffn_w_up, v_ffn_conv_w, v_ffn_conv_b, v_ffn_w_down, v_ln2_g, v_ln2_b):
    given = dict(x=x, mem=mem, ln_in_g=ln_in_g, ln_in_b=ln_in_b, w_in=w_in, b_in=b_in, s5_lam_re=s5_lam_re, s5_lam_im=s5_lam_im, s5_log_dt=s5_log_dt, s5_b_re=s5_b_re, s5_b_im=s5_b_im, s5_c_re=s5_c_re, s5_c_im=s5_c_im, s5_d=s5_d, s5_w_glu=s5_w_glu, s5_b_glu=s5_b_glu, cv_w=cv_w, cv_b=cv_b, cv_gn_g=cv_gn_g, cv_gn_b=cv_gn_b, cv_w_pw=cv_w_pw, cv_b_pw=cv_b_pw, lru_conv_w=lru_conv_w, lru_conv_b=lru_conv_b, lru_w_r=lru_w_r, lru_b_r=lru_b_r, lru_w_i=lru_w_i, lru_b_i=lru_b_i, lru_lam=lru_lam, attn_w_kv=attn_w_kv, w_out=w_out, b_out=b_out, ln1_g=ln1_g, ln1_b=ln1_b, ffn_w_up=ffn_w_up, ffn_conv_w=ffn_conv_w, ffn_conv_b=ffn_conv_b, ffn_w_down=ffn_w_down, ln2_g=ln2_g, ln2_b=ln2_b, loss_target=loss_target, m_ln_in_g=m_ln_in_g, m_ln_in_b=m_ln_in_b, m_w_in=m_w_in, m_b_in=m_b_in, m_s5_lam_re=m_s5_lam_re, m_s5_lam_im=m_s5_lam_im, m_s5_log_dt=m_s5_log_dt, m_s5_b_re=m_s5_b_re, m_s5_b_im=m_s5_b_im, m_s5_c_re=m_s5_c_re, m_s5_c_im=m_s5_c_im, m_s5_d=m_s5_d, m_s5_w_glu=m_s5_w_glu, m_s5_b_glu=m_s5_b_glu, m_cv_w=m_cv_w, m_cv_b=m_cv_b, m_cv_gn_g=m_cv_gn_g, m_cv_gn_b=m_cv_gn_b, m_cv_w_pw=m_cv_w_pw, m_cv_b_pw=m_cv_b_pw, m_lru_conv_w=m_lru_conv_w, m_lru_conv_b=m_lru_conv_b, m_lru_w_r=m_lru_w_r, m_lru_b_r=m_lru_b_r, m_lru_w_i=m_lru_w_i, m_lru_b_i=m_lru_b_i, m_lru_lam=m_lru_lam, m_attn_w_kv=m_attn_w_kv, m_w_out=m_w_out, m_b_out=m_b_out, m_ln1_g=m_ln1_g, m_ln1_b=m_ln1_b, m_ffn_w_up=m_ffn_w_up, m_ffn_conv_w=m_ffn_conv_w, m_ffn_conv_b=m_ffn_conv_b, m_ffn_w_down=m_ffn_w_down, m_ln2_g=m_ln2_g, m_ln2_b=m_ln2_b, v_ln_in_g=v_ln_in_g, v_ln_in_b=v_ln_in_b, v_w_in=v_w_in, v_b_in=v_b_in, v_s5_lam_re=v_s5_lam_re, v_s5_lam_im=v_s5_lam_im, v_s5_log_dt=v_s5_log_dt, v_s5_b_re=v_s5_b_re, v_s5_b_im=v_s5_b_im, v_s5_c_re=v_s5_c_re, v_s5_c_im=v_s5_c_im, v_s5_d=v_s5_d, v_s5_w_glu=v_s5_w_glu, v_s5_b_glu=v_s5_b_glu, v_cv_w=v_cv_w, v_cv_b=v_cv_b, v_cv_gn_g=v_cv_gn_g, v_cv_gn_b=v_cv_gn_b, v_cv_w_pw=v_cv_w_pw, v_cv_b_pw=v_cv_b_pw, v_lru_conv_w=v_lru_conv_w, v_lru_conv_b=v_lru_conv_b, v_lru_w_r=v_lru_w_r, v_lru_b_r=v_lru_b_r, v_lru_w_i=v_lru_w_i, v_lru_b_i=v_lru_b_i, v_lru_lam=v_lru_lam, v_attn_w_kv=v_attn_w_kv, v_w_out=v_w_out, v_b_out=v_b_out, v_ln1_g=v_ln1_g, v_ln1_b=v_ln1_b, v_ffn_w_up=v_ffn_w_up, v_ffn_conv_w=v_ffn_conv_w, v_ffn_conv_b=v_ffn_conv_b, v_ffn_w_down=v_ffn_w_down, v_ln2_g=v_ln2_g, v_ln2_b=v_ln2_b)
    weights = {n: given[n] for n in TWIN_WEIGHTS}
    shared = {n: given[n] for n in SHARED_INPUTS}
    per_example = {n: given[n] for n in ['x', 'mem']}
    grad_fn = _jax.value_and_grad(_loss, argnums=(0, 1))

    def one_microbatch(ex, loss_target):
        ex = dict(ex)
        diff = ex.pop(TWIN_DIFF_INPUT)
        return grad_fn(weights, diff, {**shared, **ex}, loss_target)

    if N_MICROBATCH == 1:
        loss, (grad_w, grad_x) = one_microbatch(per_example, given["loss_target"])
    else:
        def body(carry, xs):
            loss_sum, grad_sum = carry
            l_k, (gw_k, gx_k) = one_microbatch(xs[0], xs[1])
            with _jax.named_scope("update"):
                return (loss_sum + l_k, _jax.tree.map(_jnp.add, grad_sum, gw_k)), gx_k

        init = (_jnp.zeros((), _jnp.float32), _jax.tree.map(_jnp.zeros_like, weights))
        (loss, grad_w), grad_x = _jax.lax.scan(body, init, (per_example, given["loss_target"]))
    with _jax.named_scope("update"):
        delta_w, new_m, new_v = {}, {}, {}
        for n in TWIN_WEIGHTS:
            delta_w[n], new_m[n], new_v[n] = _adamw(weights[n], grad_w[n], given["m_" + n], given["v_" + n])
    return (loss, grad_x, *[grad_w[n] for n in TWIN_WEIGHTS], *[delta_w[n] for n in TWIN_WEIGHTS],
            *[new_m[n] for n in TWIN_WEIGHTS], *[new_v[n] for n in TWIN_WEIGHTS])
```

```python
import functools
import math

import jax
import jax.numpy as jnp
from jax import lax
from jax.experimental import pallas as pl
from jax.experimental.pallas import tpu as pltpu

F32 = jnp.float32
BF16 = jnp.bfloat16
MESH = pl.DeviceIdType.MESH
ANY = pl.BlockSpec(memory_space=pl.ANY)

DEPTH = 2
D_MODEL = 1024
D_GROUP = 256
N_IN = 6 * D_GROUP
D_FF = 2816
N_STATE = 1024
CONV_WIDTH = 31
LRU_CONV_WIDTH = 4
FFN_CONV_WIDTH = 3
LRU_C = 8.0
ALPHA = (2 * DEPTH) ** 0.25
LN_EPS = 1e-5
N_CHIPS = 4
MEM_ROWS = 256
LANES = 128
SUBLANES = 8
VMEM_LIMIT = 56 * 1024 * 1024

ADAM_LR, ADAM_B1, ADAM_B2, ADAM_EPS, ADAM_WD, ADAM_STEP = 0.001, 0.9, 0.999, 1e-08, 0.01, 10

WEIGHTS = ['ln_in_g', 'ln_in_b', 'w_in', 'b_in', 's5_lam_re', 's5_lam_im', 's5_log_dt', 's5_b_re', 's5_b_im',
           's5_c_re', 's5_c_im', 's5_d', 's5_w_glu', 's5_b_glu', 'cv_w', 'cv_b', 'cv_gn_g', 'cv_gn_b', 'cv_w_pw',
           'cv_b_pw', 'lru_conv_w', 'lru_conv_b', 'lru_w_r', 'lru_b_r', 'lru_w_i', 'lru_b_i', 'lru_lam',
           'attn_w_kv', 'w_out', 'b_out', 'ln1_g', 'ln1_b', 'ffn_w_up', 'ffn_conv_w', 'ffn_conv_b', 'ffn_w_down',
           'ln2_g', 'ln2_b']
BIG = ['w_in', 'attn_w_kv', 'w_out', 'ffn_w_up', 'ffn_w_down']
SMALL_SHARDED = {'s5_w_glu': 1, 'cv_w': 2, 'cv_w_pw': 1, 'lru_conv_w': 2, 'ffn_conv_w': 2}


def _cparams(n_axes):
    return pltpu.CompilerParams(dimension_semantics=("arbitrary",) * n_axes, vmem_limit_bytes=VMEM_LIMIT)


def _dot(a, b):
    return jnp.dot(a, b, preferred_element_type=F32)


def _dot_nt(a, b):
    return lax.dot_general(a, b, (((1,), (1,)), ((), ())), preferred_element_type=F32)


def _dot_tn(a, b):
    return lax.dot_general(a, b, (((0,), (0,)), ((), ())), preferred_element_type=F32)


def _bf(v):
    return v.astype(BF16)


def _colsum(v):
    return jnp.sum(v, axis=0, keepdims=True)


def _dot3(v, p):
    hi = _bf(v)
    r1 = v - hi.astype(F32)
    mid = _bf(r1)
    lo = _bf(r1 - mid.astype(F32))
    return _dot(hi, p) + _dot(mid, p) + _dot(lo, p)


_GELU_C = math.sqrt(2.0 / math.pi)


def _gelu_parts(v):
    t = jnp.tanh(_GELU_C * (v + 0.044715 * v * v * v))
    return 0.5 * v * (1.0 + t), t


def _gelu(v):
    return _gelu_parts(v)[0]


def _gelu_grad(v, t):
    return 0.5 * (1.0 + t) + 0.5 * v * (1.0 - t * t) * _GELU_C * (1.0 + 3.0 * 0.044715 * v * v)


def _sigmoid(v):
    return 1.0 / (1.0 + jnp.exp(-v))


def _acc(ref, val, first):
    @pl.when(first)
    def _():
        ref[...] = val

    @pl.when(jnp.logical_not(first))
    def _():
        ref[...] += val


def _rows(shape):
    return lax.broadcasted_iota(jnp.int32, shape, 0)


def _ln_rows(r):
    mu = jnp.mean(r, -1, keepdims=True)
    rc = r - mu
    var = jnp.mean(rc * rc, -1, keepdims=True)
    rs = lax.rsqrt(var + LN_EPS)
    return rc * rs, rs


def ln_fwd(x, g, b, name):
    s = x.shape[0]
    tm = min(512, s)

    def body(x_ref, g_ref, b_ref, xh_ref, rs_ref, xb_ref):
        xh, rs = _ln_rows(x_ref[...])
        xh_ref[...] = xh
        rs_ref[...] = rs
        xb_ref[...] = _bf(xh * g_ref[...] + b_ref[...])

    row = pl.BlockSpec((tm, D_MODEL), lambda i: (i, 0))
    vec = pl.BlockSpec((1, D_MODEL), lambda i: (0, 0))
    return pl.pallas_call(
        body, name=name, grid=(s // tm,),
        in_specs=[row, vec, vec],
        out_specs=[row, pl.BlockSpec((tm, 1), lambda i: (i, 0)), row],
        out_shape=[jax.ShapeDtypeStruct((s, D_MODEL), F32), jax.ShapeDtypeStruct((s, 1), F32),
                   jax.ShapeDtypeStruct((s, D_MODEL), BF16)],
        compiler_params=_cparams(1),
    )(x, g, b)


def proj_ln(a, w, layer, bias, xh_prev, g_prev, b_prev, g, b, name):
    s, k = a.shape
    tm = min(512, s)

    def body(a_ref, w_ref, bias_ref, xp_ref, gp_ref, bp_ref, g_ref, b_ref, xh_ref, rs_ref, xb_ref):
        acc = _dot(a_ref[...], w_ref[...]) + bias_ref[...]
        r = ALPHA * (xp_ref[...] * gp_ref[...] + bp_ref[...]) + acc
        xh, rs = _ln_rows(r)
        xh_ref[...] = xh
        rs_ref[...] = rs
        xb_ref[...] = _bf(xh * g_ref[...] + b_ref[...])

    row = pl.BlockSpec((tm, D_MODEL), lambda i: (i, 0))
    vec = pl.BlockSpec((1, D_MODEL), lambda i: (0, 0))
    return pl.pallas_call(
        body, name=name, grid=(s // tm,),
        in_specs=[pl.BlockSpec((tm, k), lambda i: (i, 0)),
                  pl.BlockSpec((None, k, D_MODEL), lambda i: (layer, 0, 0)),
                  vec, row, vec, vec, vec, vec],
        out_specs=[row, pl.BlockSpec((tm, 1), lambda i: (i, 0)), row],
        out_shape=[jax.ShapeDtypeStruct((s, D_MODEL), F32), jax.ShapeDtypeStruct((s, 1), F32),
                   jax.ShapeDtypeStruct((s, D_MODEL), BF16)],
        compiler_params=_cparams(1),
    )(a, w, bias, xh_prev, g_prev, b_prev, g, b)


def ln_bwd(dy, xh, rs, g, name):
    s = xh.shape[0]
    tm = min(512, s)

    def body(dy_ref, xh_ref, rs_ref, g_ref, dr_ref, dg_ref, db_ref, cs_ref):
        first = pl.program_id(0) == 0
        dyv = dy_ref[...]
        xhv = xh_ref[...]
        dxh = dyv * g_ref[...]
        dr = rs_ref[...] * (dxh - jnp.mean(dxh, -1, keepdims=True) - xhv * jnp.mean(dxh * xhv, -1, keepdims=True))
        dr_ref[...] = dr
        _acc(dg_ref, _colsum(dyv * xhv), first)
        _acc(db_ref, _colsum(dyv), first)
        _acc(cs_ref, _colsum(dr), first)

    row = pl.BlockSpec((tm, D_MODEL), lambda i: (i, 0))
    vec = pl.BlockSpec((1, D_MODEL), lambda i: (0, 0))
    vshape = jax.ShapeDtypeStruct((1, D_MODEL), F32)
    return pl.pallas_call(
        body, name=name, grid=(s // tm,),
        in_specs=[row, row, pl.BlockSpec((tm, 1), lambda i: (i, 0)), vec],
        out_specs=[row, vec, vec, vec],
        out_shape=[jax.ShapeDtypeStruct((s, D_MODEL), F32), vshape, vshape, vshape],
        compiler_params=_cparams(1),
    )(dy, xh, rs, g)


def loss_ln_bwd(target, xh, rs, g, b, name):
    s = xh.shape[0]
    tm = min(512, s)

    def body(t_ref, xh_ref, rs_ref, g_ref, b_ref, dr_ref, dg_ref, db_ref, sq_ref):
        first = pl.program_id(0) == 0
        xhv = xh_ref[...]
        err = xhv * g_ref[...] + b_ref[...] - t_ref[...]
        dyv = err * (1.0 / D_MODEL)
        dxh = dyv * g_ref[...]
        dr = rs_ref[...] * (dxh - jnp.mean(dxh, -1, keepdims=True) - xhv * jnp.mean(dxh * xhv, -1, keepdims=True))
        dr_ref[...] = dr
        _acc(dg_ref, _colsum(dyv * xhv), first)
        _acc(db_ref, _colsum(dyv), first)
        _acc(sq_ref, _colsum(err * err), first)

    row = pl.BlockSpec((tm, D_MODEL), lambda i: (i, 0))
    vec = pl.BlockSpec((1, D_MODEL), lambda i: (0, 0))
    vshape = jax.ShapeDtypeStruct((1, D_MODEL), F32)
    return pl.pallas_call(
        body, name=name, grid=(s // tm,),
        in_specs=[row, row, pl.BlockSpec((tm, 1), lambda i: (i, 0)), vec, vec],
        out_specs=[row, vec, vec, vec],
        out_shape=[jax.ShapeDtypeStruct((s, D_MODEL), F32), vshape, vshape, vshape],
        compiler_params=_cparams(1),
    )(target, xh, rs, g, b)


def mm_nn(a, w, layer, bias, out_dtype, tm, name):
    m, k = a.shape
    _, nj, _, n = w.shape
    tm = min(tm, m)

    def body(a_ref, w_ref, b_ref, o_ref):
        o_ref[...] = (_dot(_bf(a_ref[...]), w_ref[...]) + b_ref[...]).astype(out_dtype)

    return pl.pallas_call(
        body, name=name, grid=(m // tm, nj),
        in_specs=[pl.BlockSpec((tm, k), lambda i, j: (i, 0)),
                  pl.BlockSpec((None, None, k, n), lambda i, j: (layer, j, 0, 0)),
                  pl.BlockSpec((1, n), lambda i, j: (0, j))],
        out_specs=pl.BlockSpec((tm, n), lambda i, j: (i, j)),
        out_shape=jax.ShapeDtypeStruct((m, nj * n), out_dtype),
        compiler_params=_cparams(2),
    )(a, w, bias)


def mm_nt(a, a_block, a_map, w, layer, tr, add, out_dtype, tm, m, name):
    _, nj, r, n = w.shape
    tm = min(tm, m)
    has_add = add is not None

    def body(*refs):
        if has_add:
            a_ref, w_ref, add_ref, o_ref, acc_ref = refs
        else:
            a_ref, w_ref, o_ref, acc_ref = refs
        j = pl.program_id(2)
        part = _dot_nt(_bf(a_ref[...]), w_ref[...])

        @pl.when(j == 0)
        def _():
            acc_ref[...] = part

        @pl.when(j > 0)
        def _():
            acc_ref[...] += part

        @pl.when(j == nj - 1)
        def _():
            res = acc_ref[...]
            if has_add:
                res = res + ALPHA * add_ref[...]
            o_ref[...] = res.astype(out_dtype)

    in_specs = [pl.BlockSpec(a_block, a_map),
                pl.BlockSpec((None, None, tr, n), lambda i, rt, j: (layer, j, rt, 0))]
    ops = [a, w]
    if has_add:
        in_specs.append(pl.BlockSpec((tm, tr), lambda i, rt, j: (i, rt)))
        ops.append(add)
    return pl.pallas_call(
        body, name=name, grid=(m // tm, r // tr, nj),
        in_specs=in_specs,
        out_specs=pl.BlockSpec((tm, tr), lambda i, rt, j: (i, rt)),
        out_shape=jax.ShapeDtypeStruct((m, r), out_dtype),
        scratch_shapes=[pltpu.VMEM((tm, tr), F32)],
        compiler_params=_cparams(3),
    )(*ops)


def mm_tn(a, b, b_block, b_map, nj, n, tk, ts, s, name):
    kx = a.shape[1]
    ts = min(ts, s)

    def body(a_ref, b_ref, o_ref):
        part = _dot_tn(_bf(a_ref[...]), _bf(b_ref[...]))
        _acc(o_ref, part, pl.program_id(2) == 0)

    return pl.pallas_call(
        body, name=name, grid=(kx // tk, nj, s // ts),
        in_specs=[pl.BlockSpec((ts, tk), lambda kt, j, st: (st, kt)),
                  pl.BlockSpec(b_block, b_map)],
        out_specs=pl.BlockSpec((None, tk, n), lambda kt, j, st: (j, kt, 0)),
        out_shape=jax.ShapeDtypeStruct((nj, kx, n), F32),
        compiler_params=_cparams(3),
    )(a, b)


def _cscan(br, bi, ar, ai, tb, reverse):
    row = _rows(br.shape)
    pr, pi = ar, ai
    sh = 1
    while sh < tb:
        if reverse:
            amt, mask = tb - sh, row < tb - sh
        else:
            amt, mask = sh, row >= sh
        sr = jnp.where(mask, pltpu.roll(br, amt, 0), 0.0)
        si = jnp.where(mask, pltpu.roll(bi, amt, 0), 0.0)
        br, bi = br + pr * sr - pi * si, bi + pr * si + pi * sr
        pr, pi = pr * pr - pi * pi, 2.0 * pr * pi
        sh *= 2
    return br, bi


def _power_table(ar, ai, tb, reverse):
    row = _rows((tb, N_STATE))
    seed_row = tb - 1 if reverse else 0
    er = jnp.where(row == seed_row, ar, 0.0)
    ei = jnp.where(row == seed_row, ai, 0.0)
    return _cscan(er, ei, ar, ai, tb, reverse)


def s5_fwd(h_in, a2, bexp, cexp, dskip, wglu, layer, bglu, tb, name):
    s = h_in.shape[0]
    n = N_STATE

    def body(u_ref, a_ref, b_ref, c_ref, d_ref, w_ref, bg_ref, y_ref, h_ref, y0_ref, carry, ptab):
        ar, ai = a_ref[0:1, 0:n], a_ref[0:1, n:2 * n]

        @pl.when(pl.program_id(0) == 0)
        def _():
            carry[...] = jnp.zeros_like(carry)
            pr0, pi0 = _power_table(ar, ai, tb, False)
            ptab[:, 0:n] = pr0
            ptab[:, n:2 * n] = pi0

        u = u_ref[...]
        bu = _dot(_bf(u), b_ref[...])
        lr, li = _cscan(bu[:, 0:n], bu[:, n:2 * n], ar, ai, tb, False)
        cr, ci = carry[7:8, 0:n], carry[7:8, n:2 * n]
        pr, pi = ptab[:, 0:n], ptab[:, n:2 * n]
        hr = lr + pr * cr - pi * ci
        hi = li + pr * ci + pi * cr
        h_ref[:, 0:n] = hr
        h_ref[:, n:2 * n] = hi
        carry[:, 0:n] = hr[tb - 8:tb]
        carry[:, n:2 * n] = hi[tb - 8:tb]
        y0 = _dot(_bf(hr), c_ref[0:n, :]) + _dot(_bf(hi), c_ref[n:2 * n, :]) + d_ref[...] * u
        y0_ref[...] = y0
        yg = _gelu(y0)
        z = _dot(_bf(yg), w_ref[...]) + bg_ref[...]
        y_ref[...] = _bf(yg * _sigmoid(z))

    vec = pl.BlockSpec((1, D_GROUP), lambda i: (0, 0))
    return pl.pallas_call(
        body, name=name, grid=(s // tb,),
        in_specs=[pl.BlockSpec((tb, D_GROUP), lambda i: (i, 0)),
                  pl.BlockSpec((1, 2 * n), lambda i: (0, 0)),
                  pl.BlockSpec((D_GROUP, 2 * n), lambda i: (0, 0)),
                  pl.BlockSpec((2 * n, D_GROUP), lambda i: (0, 0)),
                  vec,
                  pl.BlockSpec((None, D_GROUP, D_GROUP), lambda i: (layer, 0, 0)),
                  vec],
        out_specs=[pl.BlockSpec((tb, D_GROUP), lambda i: (i, 0)),
                   pl.BlockSpec((tb, 2 * n), lambda i: (i, 0)),
                   pl.BlockSpec((tb, D_GROUP), lambda i: (i, 0))],
        out_shape=[jax.ShapeDtypeStruct((s, D_GROUP), BF16), jax.ShapeDtypeStruct((s, 2 * n), F32),
                   jax.ShapeDtypeStruct((s, D_GROUP), F32)],
        scratch_shapes=[pltpu.VMEM((8, 2 * n), F32), pltpu.VMEM((tb, 2 * n), F32)],
        compiler_params=_cparams(1),
    )(h_in, a2, bexp, cexp, dskip, wglu, bglu)


def s5_bwd(dmix, h_in, y0, hst, a2, bexp, cexp, dskip, wglu, layer, bglu, tb, name):
    s = h_in.shape[0]
    n = N_STATE
    nb = s // tb
    halo = tb // 8

    def body(dy_ref, u_ref, y0_ref, h_ref, hp_ref, a_ref, b_ref, c_ref, d_ref, w_ref, bg_ref,
             du_ref, cs_ref, db_ref, dc_ref, dd_ref, dw_ref, dbg_ref, da_ref, carry, qtab):
        i = pl.program_id(0)
        first = i == 0
        ar, ai = a_ref[0:1, 0:n], -a_ref[0:1, n:2 * n]

        @pl.when(first)
        def _():
            carry[...] = jnp.zeros_like(carry)
            qr0, qi0 = _power_table(ar, ai, tb, True)
            qtab[:, 0:n] = qr0
            qtab[:, n:2 * n] = qi0

        dy = dy_ref[...]
        u = u_ref[...]
        y0v = y0_ref[...]
        yg, t = _gelu_parts(y0v)
        z = _dot(_bf(yg), w_ref[...]) + bg_ref[...]
        sg = _sigmoid(z)
        dz = dy * yg * sg * (1.0 - sg)
        dyg = dy * sg + _dot_nt(_bf(dz), w_ref[...])
        _acc(dw_ref, _dot_tn(_bf(yg), _bf(dz)), first)
        _acc(dbg_ref, _colsum(dz), first)
        dy0 = dyg * _gelu_grad(y0v, t)
        _acc(dd_ref, _colsum(dy0 * u), first)
        hr, hi = h_ref[:, 0:n], h_ref[:, n:2 * n]
        dy0b = _bf(dy0)
        _acc(dc_ref.at[0:n, :], _dot_tn(_bf(hr), dy0b), first)
        _acc(dc_ref.at[n:2 * n, :], _dot_tn(_bf(hi), dy0b), first)
        g = _dot_nt(dy0b, c_ref[...])
        lr, li = _cscan(g[:, 0:n], g[:, n:2 * n], ar, ai, tb, True)
        cr, ci = carry[0:1, 0:n], carry[0:1, n:2 * n]
        qr, qi = qtab[:, 0:n], qtab[:, n:2 * n]
        gr = lr + qr * cr - qi * ci
        gi = li + qr * ci + qi * cr
        carry[:, 0:n] = gr[0:8]
        carry[:, n:2 * n] = gi[0:8]
        keep = jnp.where(i == nb - 1, 0.0, 1.0)
        row = _rows((tb, n))
        pr = jnp.where(row == 0, hp_ref[7:8, 0:n] * keep, pltpu.roll(hr, 1, 0))
        pi = jnp.where(row == 0, hp_ref[7:8, n:2 * n] * keep, pltpu.roll(hi, 1, 0))
        _acc(da_ref.at[:, 0:n], _colsum(gr * pr + gi * pi), first)
        _acc(da_ref.at[:, n:2 * n], _colsum(gi * pr - gr * pi), first)
        grb, gib = _bf(gr), _bf(gi)
        du = d_ref[...] * dy0 + _dot_nt(grb, b_ref[:, 0:n]) + _dot_nt(gib, b_ref[:, n:2 * n])
        ub = _bf(u)
        _acc(db_ref.at[:, 0:n], _dot_tn(ub, grb), first)
        _acc(db_ref.at[:, n:2 * n], _dot_tn(ub, gib), first)
        du_ref[...] = _bf(du)
        _acc(cs_ref, _colsum(du), first)

    rev = lambda i: (nb - 1 - i, 0)
    vec = pl.BlockSpec((1, D_GROUP), lambda i: (0, 0))
    vshape = jax.ShapeDtypeStruct((1, D_GROUP), F32)
    return pl.pallas_call(
        body, name=name, grid=(nb,),
        in_specs=[pl.BlockSpec((tb, D_GROUP), rev),
                  pl.BlockSpec((tb, D_GROUP), rev),
                  pl.BlockSpec((tb, D_GROUP), rev),
                  pl.BlockSpec((tb, 2 * n), rev),
                  pl.BlockSpec((8, 2 * n), lambda i: (jnp.maximum((nb - 1 - i) * halo - 1, 0), 0)),
                  pl.BlockSpec((1, 2 * n), lambda i: (0, 0)),
                  pl.BlockSpec((D_GROUP, 2 * n), lambda i: (0, 0)),
                  pl.BlockSpec((2 * n, D_GROUP), lambda i: (0, 0)),
                  vec,
                  pl.BlockSpec((None, D_GROUP, D_GROUP), lambda i: (layer, 0, 0)),
                  vec],
        out_specs=[pl.BlockSpec((tb, D_GROUP), rev), vec,
                   pl.BlockSpec((D_GROUP, 2 * n), lambda i: (0, 0)),
                   pl.BlockSpec((2 * n, D_GROUP), lambda i: (0, 0)),
                   vec,
                   pl.BlockSpec((D_GROUP, D_GROUP), lambda i: (0, 0)),
                   vec,
                   pl.BlockSpec((1, 2 * n), lambda i: (0, 0))],
        out_shape=[jax.ShapeDtypeStruct((s, D_GROUP), BF16), vshape,
                   jax.ShapeDtypeStruct((D_GROUP, 2 * n), F32), jax.ShapeDtypeStruct((2 * n, D_GROUP), F32),
                   vshape, jax.ShapeDtypeStruct((D_GROUP, D_GROUP), F32), vshape,
                   jax.ShapeDtypeStruct((1, 2 * n), F32)],
        scratch_shapes=[pltpu.VMEM((8, 2 * n), F32), pltpu.VMEM((tb, 2 * n), F32)],
        compiler_params=_cparams(1),
    )(dmix, h_in, y0, hst, hst, a2, bexp, cexp, dskip, wglu, bglu)


CV_HALO = 32


def _gn_stats(hc, pmat):
    mu = _dot3(hc, pmat)
    xc = hc - mu
    var = _dot3(xc * xc, pmat)
    rstd = lax.rsqrt(var + LN_EPS)
    return xc * rstd, rstd


def cv_fwd(h_in, cw, cb, gg, gb, pmat, wpw, layer, bpw, tb, name):
    s = h_in.shape[0]
    hl = CV_HALO

    def body(v_ref, g_ref, cw_ref, cb_ref, gg_ref, gb_ref, p_ref, w_ref, bw_ref, y_ref, hc_ref, ext):
        @pl.when(pl.program_id(0) == 0)
        def _():
            ext[0:hl, :] = jnp.zeros((hl, D_GROUP), F32)

        ext[hl:hl + tb, :] = v_ref[...] * _sigmoid(g_ref[...])
        acc = jnp.zeros((tb, D_GROUP), F32) + cb_ref[...]
        for k in range(CONV_WIDTH):
            off = hl - (CONV_WIDTH - 1) + k
            acc = acc + cw_ref[k:k + 1, :] * ext[off:off + tb, :]
        hc_ref[...] = acc
        ext[0:hl, :] = ext[tb:tb + hl, :]
        xn, _ = _gn_stats(acc, p_ref[...])
        hn = xn * gg_ref[...] + gb_ref[...]
        hs = hn * _sigmoid(hn)
        y_ref[...] = _bf(_dot(_bf(hs), w_ref[...]) + bw_ref[...])

    vec = pl.BlockSpec((1, D_GROUP), lambda i: (0, 0))
    sq = pl.BlockSpec((D_GROUP, D_GROUP), lambda i: (0, 0))
    return pl.pallas_call(
        body, name=name, grid=(s // tb,),
        in_specs=[pl.BlockSpec((tb, D_GROUP), lambda i: (i, 1)),
                  pl.BlockSpec((tb, D_GROUP), lambda i: (i, 2)),
                  pl.BlockSpec((hl, D_GROUP), lambda i: (0, 0)),
                  vec, vec, vec, sq,
                  pl.BlockSpec((None, D_GROUP, D_GROUP), lambda i: (layer, 0, 0)),
                  vec],
        out_specs=[pl.BlockSpec((tb, D_GROUP), lambda i: (i, 0)), pl.BlockSpec((tb, D_GROUP), lambda i: (i, 0))],
        out_shape=[jax.ShapeDtypeStruct((s, D_GROUP), BF16), jax.ShapeDtypeStruct((s, D_GROUP), F32)],
        scratch_shapes=[pltpu.VMEM((hl + tb, D_GROUP), F32)],
        compiler_params=_cparams(1),
    )(h_in, h_in, cw, cb, gg, gb, pmat, wpw, bpw)


def cv_bwd(dmix, h_in, hc, cw, gg, gb, pmat, wpw, layer, tb, name):
    s = h_in.shape[0]
    hl = CV_HALO
    nb = s // tb
    per = tb // hl

    def body(dy_ref, v_ref, g_ref, vh_ref, gh_ref, hc_ref, cw_ref, gg_ref, gb_ref, p_ref, w_ref,
             dvg_ref, cs_ref, dcw_ref, dcb_ref, dgg_ref, dgb_ref, dw_ref, dbw_ref, ext, dext, head):
        i = pl.program_id(0)
        first = i == 0

        @pl.when(first)
        def _():
            head[...] = jnp.zeros_like(head)

        dy = dy_ref[...]
        pm = p_ref[...]
        xn, rstd = _gn_stats(hc_ref[...], pm)
        hn = xn * gg_ref[...] + gb_ref[...]
        sg = _sigmoid(hn)
        hs = hn * sg
        dyb = _bf(dy)
        _acc(dbw_ref, _colsum(dy), first)
        _acc(dw_ref, _dot_tn(_bf(hs), dyb), first)
        dhs = _dot_nt(dyb, w_ref[...])
        dhn = dhs * sg * (1.0 + hn * (1.0 - sg))
        _acc(dgg_ref, _colsum(dhn * xn), first)
        _acc(dgb_ref, _colsum(dhn), first)
        dxn = dhn * gg_ref[...]
        dhc = rstd * (dxn - _dot3(dxn, pm) - xn * _dot3(dxn * xn, pm))
        _acc(dcb_ref, _colsum(dhc), first)
        v = v_ref[...]
        sgg = _sigmoid(g_ref[...])
        keep = jnp.where(i == nb - 1, 0.0, 1.0)
        ext[0:hl, :] = vh_ref[...] * _sigmoid(gh_ref[...]) * keep
        ext[hl:hl + tb, :] = v * sgg
        dext[0:tb, :] = dhc
        dext[tb:tb + hl, :] = head[...]
        head[...] = dhc[0:hl]
        dhg = jnp.zeros((tb, D_GROUP), F32)
        for k in range(CONV_WIDTH):
            off = hl - (CONV_WIDTH - 1) + k
            wk = _colsum(dhc * ext[off:off + tb, :])
            _acc(dcw_ref.at[k:k + 1, :], wk, first)
            back = CONV_WIDTH - 1 - k
            dhg = dhg + cw_ref[k:k + 1, :] * dext[back:back + tb, :]

        @pl.when(first)
        def _():
            dcw_ref[CONV_WIDTH:hl, :] = jnp.zeros((hl - CONV_WIDTH, D_GROUP), F32)

        dv = dhg * sgg
        dg = dhg * v * sgg * (1.0 - sgg)
        dvg_ref[:, 0:D_GROUP] = _bf(dv)
        dvg_ref[:, D_GROUP:2 * D_GROUP] = _bf(dg)
        _acc(cs_ref.at[:, 0:D_GROUP], _colsum(dv), first)
        _acc(cs_ref.at[:, D_GROUP:2 * D_GROUP], _colsum(dg), first)

    vec = pl.BlockSpec((1, D_GROUP), lambda i: (0, 0))
    sq = pl.BlockSpec((D_GROUP, D_GROUP), lambda i: (0, 0))
    tap = pl.BlockSpec((hl, D_GROUP), lambda i: (0, 0))
    vshape = jax.ShapeDtypeStruct((1, D_GROUP), F32)

    def blk(col):
        return pl.BlockSpec((tb, D_GROUP), lambda i: (nb - 1 - i, col))

    def halo_blk(col):
        return pl.BlockSpec((hl, D_GROUP), lambda i: (jnp.maximum((nb - 1 - i) * per - 1, 0), col))

    return pl.pallas_call(
        body, name=name, grid=(nb,),
        in_specs=[blk(1), blk(1), blk(2), halo_blk(1), halo_blk(2),
                  pl.BlockSpec((tb, D_GROUP), lambda i: (nb - 1 - i, 0)),
                  tap, vec, vec, sq,
                  pl.BlockSpec((None, D_GROUP, D_GROUP), lambda i: (layer, 0, 0))],
        out_specs=[pl.BlockSpec((tb, 2 * D_GROUP), lambda i: (nb - 1 - i, 0)),
                   pl.BlockSpec((1, 2 * D_GROUP), lambda i: (0, 0)),
                   tap, vec, vec, vec, sq, vec],
        out_shape=[jax.ShapeDtypeStruct((s, 2 * D_GROUP), BF16), jax.ShapeDtypeStruct((1, 2 * D_GROUP), F32),
                   jax.ShapeDtypeStruct((hl, D_GROUP), F32), vshape, vshape, vshape,
                   jax.ShapeDtypeStruct((D_GROUP, D_GROUP), F32), vshape],
        scratch_shapes=[pltpu.VMEM((hl + tb, D_GROUP), F32), pltpu.VMEM((tb + hl, D_GROUP), F32),
                        pltpu.VMEM((hl, D_GROUP), F32)],
        compiler_params=_cparams(1),
    )(dmix, h_in, h_in, h_in, h_in, hc, cw, gg, gb, pmat, wpw)


LRU_HALO = 8


def _lru_gates(xc, wr_ref, br_ref, wi_ref, bi_ref, sp_ref):
    xcb = _bf(xc)
    r = _sigmoid(_dot(xcb, wr_ref[...]) + br_ref[...])
    gi = _sigmoid(_dot(xcb, wi_ref[...]) + bi_ref[...])
    la = -LRU_C * r * sp_ref[...]
    a = jnp.exp(la)
    e2 = a * a
    sq = jnp.sqrt(-jnp.tanh(la) * (e2 + 1.0))
    return r, gi, a, e2, sq


def _rscan(a, b, tb, reverse):
    row = _rows(a.shape)
    sh = 1
    while sh < tb:
        if reverse:
            amt, mask = tb - sh, row < tb - sh
        else:
            amt, mask = sh, row >= sh
        a_s = jnp.where(mask, pltpu.roll(a, amt, 0), 1.0)
        b_s = jnp.where(mask, pltpu.roll(b, amt, 0), 0.0)
        b = b + a * b_s
        a = a * a_s
        sh *= 2
    return a, b


def lru_fwd(h_in, cw, cb, wr, br, wi, bi, sp, tb, name):
    s = h_in.shape[0]
    hl = LRU_HALO

    def body(xg_ref, xr_ref, cw_ref, cb_ref, wr_ref, br_ref, wi_ref, bi_ref, sp_ref, y_ref, xc_ref, h_ref, ext, carry):
        @pl.when(pl.program_id(0) == 0)
        def _():
            ext[0:hl, :] = jnp.zeros((hl, D_GROUP), F32)
            carry[...] = jnp.zeros_like(carry)

        ext[hl:hl + tb, :] = xr_ref[...]
        xc = jnp.zeros((tb, D_GROUP), F32) + cb_ref[...]
        for k in range(LRU_CONV_WIDTH):
            off = hl - (LRU_CONV_WIDTH - 1) + k
            xc = xc + cw_ref[k:k + 1, :] * ext[off:off + tb, :]
        xc_ref[...] = xc
        ext[0:hl, :] = ext[tb:tb + hl, :]
        r, gi, a, e2, sq = _lru_gates(xc, wr_ref, br_ref, wi_ref, bi_ref, sp_ref)
        pa, hloc = _rscan(a, sq * (gi * xc), tb, False)
        h = hloc + pa * carry[7:8, :]
        h_ref[...] = h
        carry[...] = h[tb - 8:tb]
        y_ref[...] = _bf(h * _gelu(xg_ref[...]))

    vec = pl.BlockSpec((1, D_GROUP), lambda i: (0, 0))
    sq_spec = pl.BlockSpec((D_GROUP, D_GROUP), lambda i: (0, 0))
    blk = pl.BlockSpec((tb, D_GROUP), lambda i: (i, 0))
    return pl.pallas_call(
        body, name=name, grid=(s // tb,),
        in_specs=[pl.BlockSpec((tb, D_GROUP), lambda i: (i, 3)),
                  pl.BlockSpec((tb, D_GROUP), lambda i: (i, 4)),
                  pl.BlockSpec((hl, D_GROUP), lambda i: (0, 0)),
                  vec, sq_spec, vec, sq_spec, vec, vec],
        out_specs=[blk, blk, blk],
        out_shape=[jax.ShapeDtypeStruct((s, D_GROUP), BF16), jax.ShapeDtypeStruct((s, D_GROUP), F32),
                   jax.ShapeDtypeStruct((s, D_GROUP), F32)],
        scratch_shapes=[pltpu.VMEM((hl + tb, D_GROUP), F32), pltpu.VMEM((8, D_GROUP), F32)],
        compiler_params=_cparams(1),
    )(h_in, h_in, cw, cb, wr, br, wi, bi, sp)


def lru_bwd(dmix, h_in, xcs, hs, cw, wr, br, wi, bi, sp, tb, name):
    s = h_in.shape[0]
    hl = LRU_HALO
    nb = s // tb
    per = tb // hl

    def body(dy_ref, xg_ref, xr_ref, xrh_ref, xc_ref, h_ref, hp_ref, cw_ref, wr_ref, br_ref, wi_ref, bi_ref, sp_ref,
             dx_ref, cs_ref, dcw_ref, dcb_ref, dwr_ref, dbr_ref, dwi_ref, dbi_ref, dsp_ref,
             ext, dext, head, anext, gnext):
        i = pl.program_id(0)
        first = i == 0

        @pl.when(first)
        def _():
            head[...] = jnp.zeros_like(head)
            anext[...] = jnp.zeros_like(anext)
            gnext[...] = jnp.zeros_like(gnext)

        dy = dy_ref[...]
        xg = xg_ref[...]
        xc = xc_ref[...]
        h = h_ref[...]
        r, gi, a, e2, sq = _lru_gates(xc, wr_ref, br_ref, wi_ref, bi_ref, sp_ref)
        gate, t = _gelu_parts(xg)
        dh = dy * gate
        dxg = dy * h * _gelu_grad(xg, t)
        row = _rows((tb, D_GROUP))
        coef = jnp.where(row == tb - 1, anext[0:1, :], pltpu.roll(a, tb - 1, 0))
        pc, gloc = _rscan(coef, dh, tb, True)
        gfull = gloc + pc * gnext[0:1, :]
        anext[...] = a[0:8]
        gnext[...] = gfull[0:8]
        keep = jnp.where(i == nb - 1, 0.0, 1.0)
        hprev = jnp.where(row == 0, hp_ref[7:8, :] * keep, pltpu.roll(h, 1, 0))
        da = gfull * hprev
        uu = gi * xc
        dsq = gfull * uu
        duu = gfull * sq
        dla = da * a - dsq * e2 / sq
        sp = sp_ref[...]
        dr = dla * (-LRU_C) * sp
        _acc(dsp_ref, _colsum(dla * (-LRU_C) * r), first)
        dzr = dr * r * (1.0 - r)
        dzi = duu * xc * gi * (1.0 - gi)
        dzrb, dzib = _bf(dzr), _bf(dzi)
        dxc = duu * gi + _dot_nt(dzrb, wr_ref[...]) + _dot_nt(dzib, wi_ref[...])
        xcb = _bf(xc)
        _acc(dwr_ref, _dot_tn(xcb, dzrb), first)
        _acc(dwi_ref, _dot_tn(xcb, dzib), first)
        _acc(dbr_ref, _colsum(dzr), first)
        _acc(dbi_ref, _colsum(dzi), first)
        _acc(dcb_ref, _colsum(dxc), first)
        ext[0:hl, :] = xrh_ref[...] * keep
        ext[hl:hl + tb, :] = xr_ref[...]
        dext[0:tb, :] = dxc
        dext[tb:tb + hl, :] = head[...]
        head[...] = dxc[0:hl]
        dxr = jnp.zeros((tb, D_GROUP), F32)
        for k in range(LRU_CONV_WIDTH):
            off = hl - (LRU_CONV_WIDTH - 1) + k
            _acc(dcw_ref.at[k:k + 1, :], _colsum(dxc * ext[off:off + tb, :]), first)
            back = LRU_CONV_WIDTH - 1 - k
            dxr = dxr + cw_ref[k:k + 1, :] * dext[back:back + tb, :]

        @pl.when(first)
        def _():
            dcw_ref[LRU_CONV_WIDTH:hl, :] = jnp.zeros((hl - LRU_CONV_WIDTH, D_GROUP), F32)

        dx_ref[:, 0:D_GROUP] = _bf(dxg)
        dx_ref[:, D_GROUP:2 * D_GROUP] = _bf(dxr)
        _acc(cs_ref.at[:, 0:D_GROUP], _colsum(dxg), first)
        _acc(cs_ref.at[:, D_GROUP:2 * D_GROUP], _colsum(dxr), first)

    vec = pl.BlockSpec((1, D_GROUP), lambda i: (0, 0))
    sq_spec = pl.BlockSpec((D_GROUP, D_GROUP), lambda i: (0, 0))
    tap = pl.BlockSpec((hl, D_GROUP), lambda i: (0, 0))
    vshape = jax.ShapeDtypeStruct((1, D_GROUP), F32)
    sshape = jax.ShapeDtypeStruct((D_GROUP, D_GROUP), F32)

    def blk(col):
        return pl.BlockSpec((tb, D_GROUP), lambda i: (nb - 1 - i, col))

    def halo_blk(col):
        return pl.BlockSpec((hl, D_GROUP), lambda i: (jnp.maximum((nb - 1 - i) * per - 1, 0), col))

    return pl.pallas_call(
        body, name=name, grid=(nb,),
        in_specs=[blk(2), blk(3), blk(4), halo_blk(4), blk(0), blk(0), halo_blk(0),
                  tap, sq_spec, vec, sq_spec, vec, vec],
        out_specs=[pl.BlockSpec((tb, 2 * D_GROUP), lambda i: (nb - 1 - i, 0)),
                   pl.BlockSpec((1, 2 * D_GROUP), lambda i: (0, 0)),
                   tap, vec, sq_spec, vec, sq_spec, vec, vec],
        out_shape=[jax.ShapeDtypeStruct((s, 2 * D_GROUP), BF16), jax.ShapeDtypeStruct((1, 2 * D_GROUP), F32),
                   jax.ShapeDtypeStruct((hl, D_GROUP), F32), vshape, sshape, vshape, sshape, vshape, vshape],
        scratch_shapes=[pltpu.VMEM((hl + tb, D_GROUP), F32), pltpu.VMEM((tb + hl, D_GROUP), F32),
                        pltpu.VMEM((hl, D_GROUP), F32), pltpu.VMEM((8, D_GROUP), F32), pltpu.VMEM((8, D_GROUP), F32)],
        compiler_params=_cparams(1),
    )(dmix, h_in, h_in, h_in, xcs, hs, hs, cw, wr, br, wi, bi, sp)


ATTN_HEADS = 4
ATTN_HEAD_DIM = 64
ATTN_SCALE = ATTN_HEAD_DIM ** -0.5


def _head_mask(h):
    lane = lax.broadcasted_iota(jnp.int32, (1, D_GROUP), 1)
    return jnp.where((lane >= h * ATTN_HEAD_DIM) & (lane < (h + 1) * ATTN_HEAD_DIM), 1.0, 0.0)


def _softmax_rows(sc):
    e = jnp.exp(sc - jnp.max(sc, -1, keepdims=True))
    return e / jnp.sum(e, -1, keepdims=True)


def attn_fwd(h_in, kv, tb, name):
    s = h_in.shape[0]

    def body(q_ref, kv_ref, y_ref):
        q = q_ref[...]
        kb = _bf(kv_ref[:, 0:D_GROUP])
        vb = _bf(kv_ref[:, D_GROUP:2 * D_GROUP])
        out = jnp.zeros((tb, D_GROUP), F32)
        for h in range(ATTN_HEADS):
            mask = _head_mask(h)
            p = _softmax_rows(_dot_nt(_bf(q * mask), kb) * ATTN_SCALE)
            out = out + _dot(_bf(p), vb) * mask
        y_ref[...] = _bf(out)

    return pl.pallas_call(
        body, name=name, grid=(s // tb,),
        in_specs=[pl.BlockSpec((tb, D_GROUP), lambda i: (i, 5)),
                  pl.BlockSpec((D_GROUP, 2 * D_GROUP), lambda i: (0, 0))],
        out_specs=pl.BlockSpec((tb, D_GROUP), lambda i: (i, 0)),
        out_shape=jax.ShapeDtypeStruct((s, D_GROUP), BF16),
        compiler_params=_cparams(1),
    )(h_in, kv)


def attn_bwd(dmix, h_in, kv, tb, name):
    s = h_in.shape[0]

    def body(do_ref, q_ref, kv_ref, dq_ref, cs_ref, dkv_ref):
        first = pl.program_id(0) == 0
        q = q_ref[...]
        do = do_ref[...]
        kb = _bf(kv_ref[:, 0:D_GROUP])
        vb = _bf(kv_ref[:, D_GROUP:2 * D_GROUP])
        dq = jnp.zeros((tb, D_GROUP), F32)
        dk = jnp.zeros((D_GROUP, D_GROUP), F32)
        dv = jnp.zeros((D_GROUP, D_GROUP), F32)
        for h in range(ATTN_HEADS):
            mask = _head_mask(h)
            qm = _bf(q * mask)
            p = _softmax_rows(_dot_nt(qm, kb) * ATTN_SCALE)
            dom = _bf(do * mask)
            dp = _dot_nt(dom, vb)
            dv = dv + _dot_tn(_bf(p), dom)
            ds = _bf(p * (dp - jnp.sum(dp * p, -1, keepdims=True)) * ATTN_SCALE)
            dq = dq + _dot(ds, kb) * mask
            dk = dk + _dot_tn(ds, qm)
        dq_ref[...] = _bf(dq)
        _acc(cs_ref, _colsum(dq), first)
        _acc(dkv_ref.at[:, 0:D_GROUP], dk, first)
        _acc(dkv_ref.at[:, D_GROUP:2 * D_GROUP], dv, first)

    return pl.pallas_call(
        body, name=name, grid=(s // tb,),
        in_specs=[pl.BlockSpec((tb, D_GROUP), lambda i: (i, 3)),
                  pl.BlockSpec((tb, D_GROUP), lambda i: (i, 5)),
                  pl.BlockSpec((D_GROUP, 2 * D_GROUP), lambda i: (0, 0))],
        out_specs=[pl.BlockSpec((tb, D_GROUP), lambda i: (i, 0)),
                   pl.BlockSpec((1, D_GROUP), lambda i: (0, 0)),
                   pl.BlockSpec((D_GROUP, 2 * D_GROUP), lambda i: (0, 0))],
        out_shape=[jax.ShapeDtypeStruct((s, D_GROUP), BF16), jax.ShapeDtypeStruct((1, D_GROUP), F32),
                   jax.ShapeDtypeStruct((D_GROUP, 2 * D_GROUP), F32)],
        compiler_params=_cparams(1),
    )(dmix, h_in, kv)


FFN_HALO = 8
FFN_TN = D_FF // 2


def _conv3(ext, w_ref, b_ref, tb):
    acc = jnp.zeros((tb, FFN_TN), F32) + b_ref[...]
    for k in range(FFN_CONV_WIDTH):
        off = FFN_HALO - (FFN_CONV_WIDTH - 1) + k
        acc = acc + w_ref[k:k + 1, :] * ext[off:off + tb, :]
    return acc


def ffn_act_fwd(u, cw, cb, tb, name):
    s = u.shape[0]
    hl = FFN_HALO
    nct = D_FF // FFN_TN

    def body(uv_ref, ug_ref, wv_ref, wg_ref, bv_ref, bg_ref, hf_ref, extv, extg):
        @pl.when(pl.program_id(1) == 0)
        def _():
            extv[0:hl, :] = jnp.zeros((hl, FFN_TN), F32)
            extg[0:hl, :] = jnp.zeros((hl, FFN_TN), F32)

        extv[hl:hl + tb, :] = uv_ref[...]
        extg[hl:hl + tb, :] = ug_ref[...]
        vc = _conv3(extv, wv_ref, bv_ref, tb)
        gc = _conv3(extg, wg_ref, bg_ref, tb)
        hf_ref[...] = _bf(vc * _gelu(gc))
        extv[0:hl, :] = extv[tb:tb + hl, :]
        extg[0:hl, :] = extg[tb:tb + hl, :]

    return pl.pallas_call(
        body, name=name, grid=(nct, s // tb),
        in_specs=[pl.BlockSpec((tb, FFN_TN), lambda c, i: (i, c)),
                  pl.BlockSpec((tb, FFN_TN), lambda c, i: (i, c + nct)),
                  pl.BlockSpec((hl, FFN_TN), lambda c, i: (0, c)),
                  pl.BlockSpec((hl, FFN_TN), lambda c, i: (0, c + nct)),
                  pl.BlockSpec((1, FFN_TN), lambda c, i: (0, c)),
                  pl.BlockSpec((1, FFN_TN), lambda c, i: (0, c + nct))],
        out_specs=pl.BlockSpec((tb, FFN_TN), lambda c, i: (i, c)),
        out_shape=jax.ShapeDtypeStruct((s, D_FF), BF16),
        scratch_shapes=[pltpu.VMEM((hl + tb, FFN_TN), F32), pltpu.VMEM((hl + tb, FFN_TN), F32)],
        compiler_params=_cparams(2),
    )(u, u, cw, cw, cb, cb)


def ffn_act_bwd(dhf, u, cw, cb, tb, name):
    s = u.shape[0]
    hl = FFN_HALO
    nct = D_FF // FFN_TN
    nb = s // tb
    per = tb // hl

    def half(d_c, ext, dext, head, w_ref, dw_ref, first, i):
        dext[0:tb, :] = d_c
        dext[tb:tb + hl, :] = head[...]
        head[...] = d_c[0:hl]
        du = jnp.zeros((tb, FFN_TN), F32)
        for k in range(FFN_CONV_WIDTH):
            off = hl - (FFN_CONV_WIDTH - 1) + k
            _acc(dw_ref.at[k:k + 1, :], _colsum(d_c * ext[off:off + tb, :]), first)
            back = FFN_CONV_WIDTH - 1 - k
            du = du + w_ref[k:k + 1, :] * dext[back:back + tb, :]
        _acc(dw_ref.at[FFN_CONV_WIDTH:FFN_CONV_WIDTH + 1, :], _colsum(d_c), first)

        @pl.when(first)
        def _():
            dw_ref[FFN_CONV_WIDTH + 1:hl, :] = jnp.zeros((hl - FFN_CONV_WIDTH - 1, FFN_TN), F32)

        return du

    def body(dh_ref, uv_ref, ug_ref, uvh_ref, ugh_ref, wv_ref, wg_ref, bv_ref, bg_ref,
             du_ref, dwv_ref, dwg_ref, extv, extg, dextv, dextg, headv, headg):
        i = pl.program_id(1)
        first = i == 0

        @pl.when(first)
        def _():
            headv[...] = jnp.zeros_like(headv)
            headg[...] = jnp.zeros_like(headg)

        keep = jnp.where(i == nb - 1, 0.0, 1.0)
        extv[0:hl, :] = uvh_ref[...] * keep
        extg[0:hl, :] = ugh_ref[...] * keep
        extv[hl:hl + tb, :] = uv_ref[...]
        extg[hl:hl + tb, :] = ug_ref[...]
        vc = _conv3(extv, wv_ref, bv_ref, tb)
        gc = _conv3(extg, wg_ref, bg_ref, tb)
        ge, t = _gelu_parts(gc)
        dh = dh_ref[...]
        dvc = dh * ge
        dgc = dh * vc * _gelu_grad(gc, t)
        du_ref[0] = _bf(half(dvc, extv, dextv, headv, wv_ref, dwv_ref, first, i))
        du_ref[1] = _bf(half(dgc, extg, dextg, headg, wg_ref, dwg_ref, first, i))

    def blk(shift):
        return pl.BlockSpec((tb, FFN_TN), lambda c, i: (nb - 1 - i, c + shift))

    def halo_blk(shift):
        return pl.BlockSpec((hl, FFN_TN), lambda c, i: (jnp.maximum((nb - 1 - i) * per - 1, 0), c + shift))

    tapv = pl.BlockSpec((hl, FFN_TN), lambda c, i: (0, c))
    tapg = pl.BlockSpec((hl, FFN_TN), lambda c, i: (0, c + nct))
    return pl.pallas_call(
        body, name=name, grid=(nct, nb),
        in_specs=[blk(0), blk(0), blk(nct), halo_blk(0), halo_blk(nct), tapv, tapg,
                  pl.BlockSpec((1, FFN_TN), lambda c, i: (0, c)),
                  pl.BlockSpec((1, FFN_TN), lambda c, i: (0, c + nct))],
        out_specs=[pl.BlockSpec((2, tb, FFN_TN), lambda c, i: (0, nb - 1 - i, c)), tapv, tapv],
        out_shape=[jax.ShapeDtypeStruct((2, s, D_FF), BF16), jax.ShapeDtypeStruct((hl, D_FF), F32),
                   jax.ShapeDtypeStruct((hl, D_FF), F32)],
        scratch_shapes=[pltpu.VMEM((hl + tb, FFN_TN), F32), pltpu.VMEM((hl + tb, FFN_TN), F32),
                        pltpu.VMEM((tb + hl, FFN_TN), F32), pltpu.VMEM((tb + hl, FFN_TN), F32),
                        pltpu.VMEM((hl, FFN_TN), F32), pltpu.VMEM((hl, FFN_TN), F32)],
        compiler_params=_cparams(2),
    )(dhf, u, u, u, u, cw, cw, cb, cb)


def _place():
    x, y, c = lax.axis_index("x"), lax.axis_index("y"), lax.axis_index("c")
    return x, y, c, 2 * x + y


def _chip_peer(x, y, d):
    return jnp.bitwise_xor(x, d >> 1), jnp.bitwise_xor(y, d & 1)


def gather_weights(shards):
    n = len(shards)

    def body(*refs):
        ins, outs = refs[:n], refs[n:2 * n]
        lsem, ssem, rsem, fsem, gsem = refs[2 * n:]
        x, y, c, j = _place()
        sib = (x, y, 1 - c)
        started = []
        for w in range(n):
            for l in range(DEPTH):
                cp = pltpu.make_async_copy(ins[w].at[l], outs[w].at[l, j], lsem.at[w, l])
                cp.start()
                started.append(cp)
        ici = {}
        for w in range(n):
            for d in (1, 2, 3):
                px, py = _chip_peer(x, y, d)
                cp = pltpu.make_async_remote_copy(
                    src_ref=ins[w].at[c], dst_ref=outs[w].at[c, j], send_sem=ssem.at[w, d - 1],
                    recv_sem=rsem.at[w, d - 1], device_id=(px, py, c), device_id_type=MESH)
                cp.start()
                ici[w, d] = cp
        fwd = {}
        for d in (1, 2, 3):
            jd = jnp.bitwise_xor(j, d)
            for w in range(n):
                ici[w, d].wait_recv()
                cp = pltpu.make_async_remote_copy(
                    src_ref=outs[w].at[c, jd], dst_ref=outs[w].at[c, jd], send_sem=fsem.at[w, d - 1],
                    recv_sem=gsem.at[w, d - 1], device_id=sib, device_id_type=MESH)
                cp.start()
                fwd[w, d] = cp
        for w in range(n):
            for d in (1, 2, 3):
                fwd[w, d].wait_recv()
                fwd[w, d].wait_send()
                ici[w, d].wait_send()
        for cp in started:
            cp.wait()

    out_shape = [jax.ShapeDtypeStruct((DEPTH, N_CHIPS) + a.shape[1:], a.dtype) for a in shards]
    return pl.pallas_call(
        body, name="gather_weights", in_specs=[ANY] * n, out_specs=[ANY] * n, out_shape=out_shape,
        scratch_shapes=[pltpu.SemaphoreType.DMA((n, DEPTH)), pltpu.SemaphoreType.DMA((n, 3)),
                        pltpu.SemaphoreType.DMA((n, 3)), pltpu.SemaphoreType.DMA((n, 3)),
                        pltpu.SemaphoreType.DMA((n, 3))],
    )(*shards)


def exchange_layers(gbig, small):
    def body(g_ref, s_ref, got_ref, all_ref, dsem, esem, lsem, ssem, rsem, fsem, hsem):
        x, y, c, j = _place()
        sib = (x, y, 1 - c)
        me = 4 * x + 2 * y + c
        big = pltpu.make_async_remote_copy(src_ref=g_ref.at[1 - c], dst_ref=got_ref, send_sem=dsem, recv_sem=esem,
                                           device_id=sib, device_id_type=MESH)
        big.start()
        mine = pltpu.make_async_copy(s_ref, all_ref.at[me], lsem)
        mine.start()

        def small_copy(k, block, to, sems, from_input):
            return pltpu.make_async_remote_copy(
                src_ref=s_ref if from_input else all_ref.at[block], dst_ref=all_ref.at[block],
                send_sem=sems[0].at[k], recv_sem=sems[1].at[k], device_id=to, device_id_type=MESH)

        first = [small_copy(0, me, sib, (ssem, rsem), True)]
        for d in (1, 2, 3):
            px, py = _chip_peer(x, y, d)
            first.append(small_copy(d, me, (px, py, c), (ssem, rsem), True))
        for cp in first:
            cp.start()
        passed = []
        for d in (1, 2, 3):
            px, py = _chip_peer(x, y, d)
            src_block = 4 * px + 2 * py + c
            small_copy(d, src_block, sib, (ssem, rsem), False).wait_recv()
            cp = small_copy(d - 1, src_block, sib, (fsem, hsem), False)
            cp.start()
            passed.append(cp)
        small_copy(0, me, sib, (ssem, rsem), False).wait_recv()
        for cp in passed:
            cp.wait_recv()
        for cp in first + passed:
            cp.wait_send()
        mine.wait()
        big.wait()

    r2 = small.shape[0]
    return pl.pallas_call(
        body, name="exchange_layers", in_specs=[ANY, ANY], out_specs=[ANY, ANY],
        out_shape=[jax.ShapeDtypeStruct(gbig.shape[1:], F32), jax.ShapeDtypeStruct((8, r2, LANES), F32)],
        scratch_shapes=[pltpu.SemaphoreType.DMA, pltpu.SemaphoreType.DMA, pltpu.SemaphoreType.DMA,
                        pltpu.SemaphoreType.DMA((4,)), pltpu.SemaphoreType.DMA((4,)),
                        pltpu.SemaphoreType.DMA((3,)), pltpu.SemaphoreType.DMA((3,))],
    )(gbig, small)


def scatter_shards(s1):
    def body(s_ref, got_ref, ssem, rsem):
        x, y, c, j = _place()
        cps = []
        for d in (1, 2, 3):
            px, py = _chip_peer(x, y, d)
            cp = pltpu.make_async_remote_copy(
                src_ref=s_ref.at[jnp.bitwise_xor(j, d)], dst_ref=got_ref.at[d - 1], send_sem=ssem.at[d - 1],
                recv_sem=rsem.at[d - 1], device_id=(px, py, c), device_id_type=MESH)
            cp.start()
            cps.append(cp)
        for cp in cps:
            cp.wait()

    return pl.pallas_call(
        body, name="scatter_shards", in_specs=[ANY], out_specs=ANY,
        out_shape=jax.ShapeDtypeStruct((3,) + s1.shape[1:], F32),
        scratch_shapes=[pltpu.SemaphoreType.DMA((3,)), pltpu.SemaphoreType.DMA((3,))],
    )(s1)


def swap_with_sibling(r):
    def body(r_ref, got_ref, ssem, rsem):
        x, y, c, j = _place()
        cp = pltpu.make_async_remote_copy(src_ref=r_ref, dst_ref=got_ref, send_sem=ssem, recv_sem=rsem,
                                          device_id=(x, y, 1 - c), device_id_type=MESH)
        cp.start()
        cp.wait()

    return pl.pallas_call(
        body, name="swap_with_sibling", in_specs=[ANY], out_specs=ANY,
        out_shape=jax.ShapeDtypeStruct(r.shape, F32),
        scratch_shapes=[pltpu.SemaphoreType.DMA, pltpu.SemaphoreType.DMA],
    )(r)


def add_layer_halves(gbig, got, cidx):
    _, nch, r, cdim = gbig.shape
    tr = r // 8

    def body(c_ref, a_ref, b_ref, o_ref):
        o_ref[...] = a_ref[...] + b_ref[...]

    grid_spec = pltpu.PrefetchScalarGridSpec(
        num_scalar_prefetch=1, grid=(nch, r // tr),
        in_specs=[pl.BlockSpec((None, None, tr, cdim), lambda jj, i, c_ref: (c_ref[0], jj, i, 0)),
                  pl.BlockSpec((None, tr, cdim), lambda jj, i, c_ref: (jj, i, 0))],
        out_specs=pl.BlockSpec((None, tr, cdim), lambda jj, i, c_ref: (jj, i, 0)))
    return pl.pallas_call(
        body, name="add_layer_halves", grid_spec=grid_spec,
        out_shape=jax.ShapeDtypeStruct((nch, r, cdim), F32), compiler_params=_cparams(2),
    )(cidx, gbig, got)


def add_chip_parts(s1, got, jidx):
    _, r, cdim = s1.shape
    tr = r // 8

    def body(j_ref, a_ref, g0_ref, g1_ref, g2_ref, o_ref):
        o_ref[...] = ((a_ref[...] + g0_ref[...]) + g1_ref[...]) + g2_ref[...]

    def slot(k):
        return pl.BlockSpec((None, tr, cdim), lambda i, j_ref: (k, i, 0))

    grid_spec = pltpu.PrefetchScalarGridSpec(
        num_scalar_prefetch=1, grid=(r // tr,),
        in_specs=[pl.BlockSpec((None, tr, cdim), lambda i, j_ref: (j_ref[0], i, 0)), slot(0), slot(1), slot(2)],
        out_specs=pl.BlockSpec((tr, cdim), lambda i, j_ref: (i, 0)))
    return pl.pallas_call(
        body, name="add_chip_parts", grid_spec=grid_spec,
        out_shape=jax.ShapeDtypeStruct((r, cdim), F32), compiler_params=_cparams(1),
    )(jidx, s1, got, got, got)


def sum_devices(allp):
    _, r, _ = allp.shape

    def body(a_ref, o_ref):
        tot = a_ref[0]
        for k in range(1, 8):
            tot = tot + a_ref[k]
        o_ref[...] = tot

    tr = r // 2 if r % 16 == 0 else r
    return pl.pallas_call(
        body, name="sum_devices", grid=(r // tr,),
        in_specs=[pl.BlockSpec((8, tr, LANES), lambda i: (0, i, 0))],
        out_specs=pl.BlockSpec((tr, LANES), lambda i: (i, 0)),
        out_shape=jax.ShapeDtypeStruct((r, LANES), F32), compiler_params=_cparams(1),
    )(allp)


def _row_tile(r, cdim, limit_bytes=1 << 20):
    best = None
    for tr in range(8, r + 1, 8):
        if r % tr == 0 and tr * cdim * 4 <= limit_bytes:
            best = tr
    return best if best is not None else r


def adamw(w, g, m, v, name):
    r, cdim = w.shape
    tr = _row_tile(r, cdim)
    bc1 = 1.0 - ADAM_B1 ** ADAM_STEP
    bc2 = 1.0 - ADAM_B2 ** ADAM_STEP

    def body(w_ref, g_ref, m_ref, v_ref, d_ref, nm_ref, nv_ref):
        gv = g_ref[...]
        nm = ADAM_B1 * m_ref[...] + (1.0 - ADAM_B1) * gv
        nv = ADAM_B2 * v_ref[...] + (1.0 - ADAM_B2) * (gv * gv)
        d_ref[...] = -ADAM_LR * ((nm / bc1) / (jnp.sqrt(nv / bc2) + ADAM_EPS) + ADAM_WD * w_ref[...])
        nm_ref[...] = nm
        nv_ref[...] = nv

    blk = pl.BlockSpec((tr, cdim), lambda i: (i, 0))
    shape = jax.ShapeDtypeStruct((r, cdim), F32)
    return pl.pallas_call(
        body, name=name, grid=(r // tr,), in_specs=[blk] * 4, out_specs=[blk] * 3, out_shape=[shape] * 3,
        compiler_params=_cparams(1),
    )(w, g, m, v)


def _s5_prepare(lam_re, lam_im, log_dt, b_re, b_im, c_re, c_im):
    groups, ch = 16, 16
    dt = jnp.exp(log_dt)[:, None]
    mag = jnp.exp(lam_re * dt)
    a_r, a_i = mag * jnp.cos(lam_im * dt), mag * jnp.sin(lam_im * dt)
    den = lam_re * lam_re + lam_im * lam_im
    q_r = ((a_r - 1.0) * lam_re + a_i * lam_im) / den
    q_i = (a_i * lam_re - (a_r - 1.0) * lam_im) / den
    bb_r = q_r[..., None] * b_re - q_i[..., None] * b_im
    bb_i = q_r[..., None] * b_im + q_i[..., None] * b_re
    eye = jnp.eye(groups, dtype=F32)

    def expand_b(bb):
        return jnp.einsum("gpc,gh->gchp", bb, eye).reshape(groups * ch, N_STATE)

    def expand_c(cc):
        return jnp.einsum("gcp,gh->hpgc", cc, eye).reshape(N_STATE, groups * ch)

    a2 = jnp.concatenate([a_r.reshape(1, N_STATE), a_i.reshape(1, N_STATE)], axis=1)
    bexp = jnp.concatenate([expand_b(bb_r), expand_b(bb_i)], axis=1)
    cexp = jnp.concatenate([expand_c(c_re), -expand_c(c_im)], axis=0)
    return a2, bexp, cexp


def _lru_prepare(w_r, w_i, lam):
    heads = 4
    eye = jnp.eye(heads, dtype=F32)

    def expand(w):
        return jnp.einsum("hij,hk->hikj", w, eye).reshape(D_GROUP, D_GROUP)

    return expand(w_r), expand(w_i), jax.nn.softplus(-lam).reshape(1, D_GROUP)


def _pad_rows(a, rows):
    return jnp.pad(a, ((0, rows - a.shape[0]), (0, 0)))


def _group_mean_matrix():
    gidx = jnp.arange(D_GROUP) // 64
    return (gidx[:, None] == gidx[None, :]).astype(BF16) * jnp.asarray(1.0 / 64.0, BF16)


def _pack_rows(arrs, width):
    parts = []
    for a in arrs:
        flat = a.reshape(-1)
        pad = (-flat.shape[0]) % width
        parts.append(jnp.pad(flat, (0, pad)) if pad else flat)
    flat = jnp.concatenate(parts)
    rows = flat.shape[0] // width
    pad_rows = (-rows) % 16
    if pad_rows:
        flat = jnp.pad(flat, (0, pad_rows * width))
    return flat.reshape(-1, width)


def _unpack_rows(packed, shapes, width):
    flat = packed.reshape(-1)
    out, off = [], 0
    for shp in shapes:
        size = math.prod(shp)
        out.append(flat[off:off + size].reshape(shp))
        off += size + ((-size) % width)
    return out


PACK_W = 512


def kernel(x, mem, ln_in_g, ln_in_b, w_in, b_in, s5_lam_re, s5_lam_im, s5_log_dt, s5_b_re, s5_b_im, s5_c_re, s5_c_im, s5_d, s5_w_glu, s5_b_glu, cv_w, cv_b, cv_gn_g, cv_gn_b, cv_w_pw, cv_b_pw, lru_conv_w, lru_conv_b, lru_w_r, lru_b_r, lru_w_i, lru_b_i, lru_lam, attn_w_kv, w_out, b_out, ln1_g, ln1_b, ffn_w_up, ffn_conv_w, ffn_conv_b, ffn_w_down, ln2_g, ln2_b, loss_target, m_ln_in_g, m_ln_in_b, m_w_in, m_b_in, m_s5_lam_re, m_s5_lam_im, m_s5_log_dt, m_s5_b_re, m_s5_b_im, m_s5_c_re, m_s5_c_im, m_s5_d, m_s5_w_glu, m_s5_b_glu, m_cv_w, m_cv_b, m_cv_gn_g, m_cv_gn_b, m_cv_w_pw, m_cv_b_pw, m_lru_conv_w, m_lru_conv_b, m_lru_w_r, m_lru_b_r, m_lru_w_i, m_lru_b_i, m_lru_lam, m_attn_w_kv, m_w_out, m_b_out, m_ln1_g, m_ln1_b, m_ffn_w_up, m_ffn_conv_w, m_ffn_conv_b, m_ffn_w_down, m_ln2_g, m_ln2_b, v_ln_in_g, v_ln_in_b, v_w_in, v_b_in, v_s5_lam_re, v_s5_lam_im, v_s5_log_dt, v_s5_b_re, v_s5_b_im, v_s5_c_re, v_s5_c_im, v_s5_d, v_s5_w_glu, v_s5_b_glu, v_cv_w, v_cv_b, v_cv_gn_g, v_cv_gn_b, v_cv_w_pw, v_cv_b_pw, v_lru_conv_w, v_lru_conv_b, v_lru_w_r, v_lru_b_r, v_lru_w_i, v_lru_b_i, v_lru_lam, v_attn_w_kv, v_w_out, v_b_out, v_ln1_g, v_ln1_b, v_ffn_w_up, v_ffn_conv_w, v_ffn_conv_b, v_ffn_w_down, v_ln2_g, v_ln2_b):
    p = dict(locals())
    xs = x[0]
    mems = mem[0]
    target = loss_target[0]
    s = xs.shape[0]
    cidx = lax.axis_index("c")
    jidx = 2 * lax.axis_index("x") + lax.axis_index("y")
    tb_scan = min(256, s)
    tb_attn = min(512, s)
    tb_ffn = min(256, s)

    small_sh_names = list(SMALL_SHARDED)
    small_sh_shapes = [p[nm].shape[1:] for nm in small_sh_names]
    small_pack = jnp.stack([_pack_rows([p[nm][l] for nm in small_sh_names], LANES) for l in range(DEPTH)])
    gathered = gather_weights([_bf(p[nm]) for nm in BIG] + [small_pack])
    g_w_in, g_w_kv, g_w_out, g_w_up, g_w_down, g_small = gathered
    g_w_kv = g_w_kv.reshape(DEPTH, 1, D_MODEL, 2 * D_GROUP)
    g_w_out = g_w_out.reshape(DEPTH, 1, D_MODEL, D_MODEL)
    g_w_down = g_w_down.reshape(DEPTH, 1, D_FF, D_MODEL)
    full_small = {nm: [] for nm in small_sh_names}
    for l in range(DEPTH):
        per_chip = [_unpack_rows(g_small[l, jj], small_sh_shapes, LANES) for jj in range(N_CHIPS)]
        for k, nm in enumerate(small_sh_names):
            full_small[nm].append(jnp.concatenate([per_chip[jj][k] for jj in range(N_CHIPS)],
                                                  axis=SMALL_SHARDED[nm] - 1))
    w_glu_bf = _bf(jnp.stack(full_small['s5_w_glu']))
    w_pw_bf = _bf(jnp.stack(full_small['cv_w_pw']))
    pmat = _group_mean_matrix()

    def vec(a):
        return a.reshape(1, -1)

    xh0, rs0, xb0 = ln_fwd(xs, vec(ln_in_g), vec(ln_in_b), "ln_in")
    saved = []
    prev = dict(xh=xh0, rs=rs0, xb=xb0, g=vec(ln_in_g), b=vec(ln_in_b))
    for l in range(DEPTH):
        sv = dict(prev=prev)
        (a2, bexp, cexp), sv['s5_vjp'] = jax.vjp(_s5_prepare, s5_lam_re[l], s5_lam_im[l], s5_log_dt[l],
                                                 s5_b_re[l], s5_b_im[l], s5_c_re[l], s5_c_im[l])
        (wr, wi, sp), sv['lru_vjp'] = jax.vjp(_lru_prepare, lru_w_r[l], lru_w_i[l], lru_lam[l])
        sv.update(a2=a2, bexp=_bf(bexp), cexp=_bf(cexp), wr=_bf(wr), wi=_bf(wi), sp=sp)
        sv['cvw'] = _pad_rows(full_small['cv_w'][l], CV_HALO)
        sv['lcw'] = _pad_rows(full_small['lru_conv_w'][l], LRU_HALO)
        sv['fcw'] = _pad_rows(full_small['ffn_conv_w'][l], FFN_HALO)
        h_in = mm_nn(prev['xb'], g_w_in, l, vec(b_in[l]), F32, 512, f"in_proj{l}")
        kv = mm_nn(mems, g_w_kv, l, jnp.zeros((1, 2 * D_GROUP), F32), F32, 256, f"kv_proj{l}")
        y_s5, hst, y0 = s5_fwd(h_in, a2, sv['bexp'], sv['cexp'], vec(s5_d[l]), w_glu_bf, l, vec(s5_b_glu[l]),
                               tb_scan, f"s5_fwd{l}")
        y_cv, hc = cv_fwd(h_in, sv['cvw'], vec(cv_b[l]), vec(cv_gn_g[l]), vec(cv_gn_b[l]), pmat, w_pw_bf, l,
                          vec(cv_b_pw[l]), tb_scan, f"cv_fwd{l}")
        y_lru, xcs, hls = lru_fwd(h_in, sv['lcw'], vec(lru_conv_b[l]), sv['wr'], vec(lru_b_r[l]), sv['wi'],
                                  vec(lru_b_i[l]), sp, tb_scan, f"lru_fwd{l}")
        y_mem = attn_fwd(h_in, kv, tb_attn, f"attn_fwd{l}")
        mix_in = jnp.concatenate([y_s5, y_cv, y_lru, y_mem], axis=1)
        xh1, rs1, xb1 = proj_ln(mix_in, g_w_out.reshape(DEPTH, D_MODEL, D_MODEL), l, vec(b_out[l]),
                                prev['xh'], prev['g'], prev['b'], vec(ln1_g[l]), vec(ln1_b[l]), f"out_proj_ln{l}")
        u = mm_nn(xb1, g_w_up, l, jnp.zeros((1, 2 * D_FF), F32), F32, 1024, f"ffn_up{l}")
        hf = ffn_act_fwd(u, sv['fcw'], vec(ffn_conv_b[l]), tb_ffn, f"ffn_act{l}")
        xh2, rs2, xb2 = proj_ln(hf, g_w_down.reshape(DEPTH, D_FF, D_MODEL), l, jnp.zeros((1, D_MODEL), F32),
                                xh1, vec(ln1_g[l]), vec(ln1_b[l]), vec(ln2_g[l]), vec(ln2_b[l]), f"ffn_down_ln{l}")
        sv.update(h_in=h_in, kv=kv, hst=hst, y0=y0, hc=hc, xcs=xcs, hls=hls, mix_in=mix_in,
                  xh1=xh1, rs1=rs1, xb1=xb1, u=u, hf=hf, xh2=xh2, rs2=rs2)
        saved.append(sv)
        prev = dict(xh=xh2, rs=rs2, xb=xb2, g=vec(ln2_g[l]), b=vec(ln2_b[l]))

    grads = {}
    per_layer = {nm: [None] * DEPTH for nm in WEIGHTS if nm not in ('ln_in_g', 'ln_in_b')}
    big_parts = [[None] * len(BIG) for _ in range(DEPTH)]
    dx = None
    for l in reversed(range(DEPTH)):
        sv = saved[l]
        pv = sv['prev']
        if l == DEPTH - 1:
            dr2, dg2, db2, sqerr = loss_ln_bwd(target, sv['xh2'], sv['rs2'], vec(ln2_g[l]), vec(ln2_b[l]), "loss_ln2_bwd")
            loss_local = 0.5 / D_MODEL * jnp.sum(sqerr)
        else:
            dr2, dg2, db2, _ = ln_bwd(dx, sv['xh2'], sv['rs2'], vec(ln2_g[l]), f"ln2_bwd{l}")
        per_layer['ln2_g'][l], per_layer['ln2_b'][l] = dg2[0], db2[0]
        tm_nt = min(512, s)
        ts_tn = min(512, s)
        dhf = mm_nt(dr2, (tm_nt, D_MODEL), lambda i, rt, j: (i, 0), g_w_down, l, FFN_TN, None, F32, 512, s, f"ffn_down_dx{l}")
        dwd = mm_tn(sv['hf'], dr2, (ts_tn, D_MODEL), lambda kt, j, st: (st, 0), 1, D_MODEL, FFN_TN, 512, s,
                    f"ffn_down_dw{l}")
        big_parts[l][4] = dwd.reshape(N_CHIPS, D_FF // N_CHIPS, D_MODEL)
        du, dcwv, dcwg = ffn_act_bwd(dhf, sv['u'], sv['fcw'], vec(ffn_conv_b[l]), tb_ffn, f"ffn_act_bwd{l}")
        dcw = jnp.concatenate([dcwv, dcwg], axis=1)
        per_layer['ffn_conv_w'][l] = dcw[0:FFN_CONV_WIDTH]
        per_layer['ffn_conv_b'][l] = dcw[FFN_CONV_WIDTH]
        dx1 = mm_nt(du, (None, tm_nt, FFN_TN), lambda i, rt, j: (j // 2, i, j % 2), g_w_up, l, D_MODEL, dr2, F32,
                    512, s, f"ffn_up_dx{l}")
        dwu = mm_tn(sv['xb1'], du, (None, ts_tn, FFN_TN), lambda kt, j, st: (j // 2, st, j % 2), N_CHIPS, FFN_TN,
                    D_MODEL, 512, s, f"ffn_up_dw{l}")
        big_parts[l][3] = dwu
        dr1, dg1, db1, cs1 = ln_bwd(dx1, sv['xh1'], sv['rs1'], vec(ln1_g[l]), f"ln1_bwd{l}")
        per_layer['ln1_g'][l], per_layer['ln1_b'][l], per_layer['b_out'][l] = dg1[0], db1[0], cs1[0]
        dmix = mm_nt(dr1, (tm_nt, D_MODEL), lambda i, rt, j: (i, 0), g_w_out, l, D_MODEL, None, F32, 512, s, f"out_proj_dx{l}")
        dwo = mm_tn(sv['mix_in'], dr1, (ts_tn, D_MODEL), lambda kt, j, st: (st, 0), 1, D_MODEL, D_MODEL, 512, s,
                    f"out_proj_dw{l}")
        big_parts[l][2] = dwo.reshape(N_CHIPS, D_MODEL // N_CHIPS, D_MODEL)
        h_in = sv['h_in']
        (d_u, cs_u, d_bexp, d_cexp, d_dd, d_wglu, d_bglu, d_a2) = s5_bwd(
            dmix, h_in, sv['y0'], sv['hst'], sv['a2'], sv['bexp'], sv['cexp'], vec(s5_d[l]), w_glu_bf, l,
            vec(s5_b_glu[l]), tb_scan, f"s5_bwd{l}")
        (d_vg, cs_vg, d_cvw, d_cvb, d_gg, d_gb, d_wpw, d_bpw) = cv_bwd(
            dmix, h_in, sv['hc'], sv['cvw'], vec(cv_gn_g[l]), vec(cv_gn_b[l]), pmat, w_pw_bf, l, tb_scan, f"cv_bwd{l}")
        (d_lx, cs_lx, d_lcw, d_lcb, d_wr, d_br, d_wi, d_bi, d_sp) = lru_bwd(
            dmix, h_in, sv['xcs'], sv['hls'], sv['lcw'], sv['wr'], vec(lru_b_r[l]), sv['wi'], vec(lru_b_i[l]),
            sv['sp'], tb_scan, f"lru_bwd{l}")
        d_q, cs_q, d_kv = attn_bwd(dmix, h_in, sv['kv'], tb_attn, f"attn_bwd{l}")
        g_s5 = sv['s5_vjp']((d_a2, d_bexp, d_cexp))
        for nm, gval in zip(['s5_lam_re', 's5_lam_im', 's5_log_dt', 's5_b_re', 's5_b_im', 's5_c_re', 's5_c_im'], g_s5):
            per_layer[nm][l] = gval
        g_lru = sv['lru_vjp']((d_wr, d_wi, d_sp))
        for nm, gval in zip(['lru_w_r', 'lru_w_i', 'lru_lam'], g_lru):
            per_layer[nm][l] = gval
        per_layer['s5_d'][l], per_layer['s5_w_glu'][l], per_layer['s5_b_glu'][l] = d_dd[0], d_wglu, d_bglu[0]
        per_layer['cv_w'][l], per_layer['cv_b'][l] = d_cvw[0:CONV_WIDTH], d_cvb[0]
        per_layer['cv_gn_g'][l], per_layer['cv_gn_b'][l] = d_gg[0], d_gb[0]
        per_layer['cv_w_pw'][l], per_layer['cv_b_pw'][l] = d_wpw, d_bpw[0]
        per_layer['lru_conv_w'][l], per_layer['lru_conv_b'][l] = d_lcw[0:LRU_CONV_WIDTH], d_lcb[0]
        per_layer['lru_b_r'][l], per_layer['lru_b_i'][l] = d_br[0], d_bi[0]
        per_layer['b_in'][l] = jnp.concatenate([cs_u, cs_vg, cs_lx, cs_q], axis=1)[0]
        dwkv = mm_tn(mems, d_kv, (MEM_ROWS, 2 * D_GROUP), lambda kt, j, st: (st, 0), 1, 2 * D_GROUP, D_MODEL, MEM_ROWS,
                     MEM_ROWS, f"kv_proj_dw{l}")
        big_parts[l][1] = dwkv.reshape(N_CHIPS, D_MODEL // N_CHIPS, 2 * D_GROUP)
        dh_in = jnp.concatenate([d_u, d_vg, d_lx, d_q], axis=1)
        n_sh = N_IN // N_CHIPS
        dxp = mm_nt(dh_in, (tm_nt, n_sh), lambda i, rt, j: (i, j), g_w_in, l, D_MODEL, dr1, F32, 512, s, f"in_proj_dx{l}")
        dwi = mm_tn(pv['xb'], dh_in, (ts_tn, n_sh), lambda kt, j, st: (st, j), N_CHIPS, n_sh, D_MODEL, 512, s,
                    f"in_proj_dw{l}")
        big_parts[l][0] = dwi
        dx = dxp
    grad_x, dg_in, db_in, _ = ln_bwd(dx, xh0, rs0, vec(ln_in_g), "ln_in_bwd")
    grads['ln_in_g'], grads['ln_in_b'] = dg_in[0], db_in[0]
    for nm, vals in per_layer.items():
        if nm not in BIG:
            grads[nm] = jnp.stack(vals)

    small_names = [nm for nm in WEIGHTS if nm not in BIG]
    small_local = _pack_rows([grads[nm] for nm in small_names], LANES)
    gbig = jnp.stack([jnp.stack([jnp.concatenate([big_parts[l][k][jj].reshape(-1, PACK_W) for k in range(len(BIG))])
                                 for jj in range(N_CHIPS)]) for l in range(DEPTH)])
    got_a, small_all = exchange_layers(gbig, small_local)
    s1 = add_layer_halves(gbig, got_a, cidx.reshape(1).astype(jnp.int32))
    got_b = scatter_shards(s1)
    mine_r = add_chip_parts(s1, got_b, jidx.reshape(1).astype(jnp.int32))
    other_r = swap_with_sibling(mine_r)
    red_big = jnp.where(cidx == 0, jnp.stack([mine_r, other_r]), jnp.stack([other_r, mine_r]))
    small_red = sum_devices(small_all)
    small_grads = dict(zip(small_names, _unpack_rows(small_red, [grads[nm].shape for nm in small_names], LANES)))

    out_g, out_d, out_m, out_v = {}, {}, {}, {}
    big_shapes = [p[nm].shape[1:] for nm in BIG]
    off = 0
    for k, nm in enumerate(BIG):
        rows = math.prod(big_shapes[k]) // PACK_W
        gk = red_big[:, off:off + rows].reshape(p[nm].shape)
        off += rows
        two_d = (-1, p[nm].shape[-1])
        dlt, nm_, nv_ = adamw(p[nm].reshape(two_d), gk.reshape(two_d), p['m_' + nm].reshape(two_d),
                              p['v_' + nm].reshape(two_d), f"adamw_{nm}")
        out_g[nm] = gk
        out_d[nm], out_m[nm], out_v[nm] = (t.reshape(p[nm].shape) for t in (dlt, nm_, nv_))
    own = {}
    for nm in small_names:
        gfull = small_grads[nm]
        if nm in SMALL_SHARDED:
            ax = SMALL_SHARDED[nm]
            width = p[nm].shape[ax]
            gfull = lax.dynamic_slice_in_dim(gfull, jidx * width, width, axis=ax)
        own[nm] = gfull
    packs = [_pack_rows([src[nm] for nm in small_names], LANES)
             for src in (dict((nm, p[nm]) for nm in small_names), own,
                         dict((nm, p['m_' + nm]) for nm in small_names), dict((nm, p['v_' + nm]) for nm in small_names))]
    dlt, nm_, nv_ = adamw(packs[0], packs[1], packs[2], packs[3], "adamw_small")
    shapes = [p[nm].shape for nm in small_names]
    for dst, packed in ((out_d, dlt), (out_m, nm_), (out_v, nv_)):
        dst.update(zip(small_names, _unpack_rows(packed, shapes, LANES)))
    out_g.update(own)

    loss = lax.psum(loss_local, ("x", "y", "c"))
    return (loss, grad_x[None], *[out_g[nm] for nm in WEIGHTS], *[out_d[nm] for nm in WEIGHTS],
            *[out_m[nm] for nm in WEIGHTS], *[out_v[nm] for nm in WEIGHTS])
```

```python
import functools
import math

import jax
import jax.numpy as jnp
from jax import lax
from jax.experimental import pallas as pl
from jax.experimental.pallas import tpu as pltpu

F32 = jnp.float32
BF16 = jnp.bfloat16
MESH = pl.DeviceIdType.MESH
ANY = pl.BlockSpec(memory_space=pl.ANY)

DEPTH = 2
D_MODEL = 1024
D_GROUP = 256
N_IN = 6 * D_GROUP
D_FF = 2816
N_STATE = 1024
CONV_WIDTH = 31
LRU_CONV_WIDTH = 4
FFN_CONV_WIDTH = 3
LRU_C = 8.0
ALPHA = (2 * DEPTH) ** 0.25
LN_EPS = 1e-5
N_CHIPS = 4
MEM_ROWS = 256
LANES = 128
SUBLANES = 8
VMEM_LIMIT = 56 * 1024 * 1024

ADAM_LR, ADAM_B1, ADAM_B2, ADAM_EPS, ADAM_WD, ADAM_STEP = 0.001, 0.9, 0.999, 1e-08, 0.01, 10

WEIGHTS = ['ln_in_g', 'ln_in_b', 'w_in', 'b_in', 's5_lam_re', 's5_lam_im', 's5_log_dt', 's5_b_re', 's5_b_im',
           's5_c_re', 's5_c_im', 's5_d', 's5_w_glu', 's5_b_glu', 'cv_w', 'cv_b', 'cv_gn_g', 'cv_gn_b', 'cv_w_pw',
           'cv_b_pw', 'lru_conv_w', 'lru_conv_b', 'lru_w_r', 'lru_b_r', 'lru_w_i', 'lru_b_i', 'lru_lam',
           'attn_w_kv', 'w_out', 'b_out', 'ln1_g', 'ln1_b', 'ffn_w_up', 'ffn_conv_w', 'ffn_conv_b', 'ffn_w_down',
           'ln2_g', 'ln2_b']
BIG = ['w_in', 'attn_w_kv', 'w_out', 'ffn_w_up', 'ffn_w_down']
SMALL_SHARDED = {'s5_w_glu': 1, 'cv_w': 2, 'cv_w_pw': 1, 'lru_conv_w': 2, 'ffn_conv_w': 2}


def _cparams(n_axes):
    return pltpu.CompilerParams(dimension_semantics=("arbitrary",) * n_axes, vmem_limit_bytes=VMEM_LIMIT)


def _dot(a, b):
    return jnp.dot(a, b, preferred_element_type=F32)


def _dot_nt(a, b):
    return lax.dot_general(a, b, (((1,), (1,)), ((), ())), preferred_element_type=F32)


def _dot_tn(a, b):
    return lax.dot_general(a, b, (((0,), (0,)), ((), ())), preferred_element_type=F32)


def _bf(v):
    return v.astype(BF16)


def _colsum(v):
    return jnp.sum(v, axis=0, keepdims=True)


def _dot3(v, p):
    hi = _bf(v)
    r1 = v - hi.astype(F32)
    mid = _bf(r1)
    lo = _bf(r1 - mid.astype(F32))
    return _dot(hi, p) + _dot(mid, p) + _dot(lo, p)


_GELU_C = math.sqrt(2.0 / math.pi)


_GELU_C3 = _GELU_C * 0.044715


def _gelu_parts(v):
    t = jnp.tanh(v * (_GELU_C + _GELU_C3 * (v * v)))
    hv = 0.5 * v
    return hv + hv * t, t


def _gelu(v):
    return _gelu_parts(v)[0]


def _gelu_grad(v, t):
    return (0.5 + 0.5 * t) + (0.5 * v) * (1.0 - t * t) * (_GELU_C + (3.0 * _GELU_C3) * (v * v))


def _sigmoid(v):
    return 1.0 / (1.0 + jnp.exp(-v))


def _acc(ref, val, first):
    @pl.when(first)
    def _():
        ref[...] = val

    @pl.when(jnp.logical_not(first))
    def _():
        ref[...] += val


def _rows(shape):
    return lax.broadcasted_iota(jnp.int32, shape, 0)


def _ln_rows(r):
    mu = jnp.mean(r, -1, keepdims=True)
    rc = r - mu
    var = jnp.mean(rc * rc, -1, keepdims=True)
    rs = lax.rsqrt(var + LN_EPS)
    return rc * rs, rs


def ln_fwd(x, g, b, name):
    s = x.shape[0]
    tm = min(512, s)

    def body(x_ref, g_ref, b_ref, xh_ref, rs_ref, xb_ref):
        xh, rs = _ln_rows(x_ref[...])
        xh_ref[...] = xh
        rs_ref[...] = rs
        xb_ref[...] = _bf(xh * g_ref[...] + b_ref[...])

    row = pl.BlockSpec((tm, D_MODEL), lambda i: (i, 0))
    vec = pl.BlockSpec((1, D_MODEL), lambda i: (0, 0))
    return pl.pallas_call(
        body, name=name, grid=(s // tm,),
        in_specs=[row, vec, vec],
        out_specs=[row, pl.BlockSpec((tm, 1), lambda i: (i, 0)), row],
        out_shape=[jax.ShapeDtypeStruct((s, D_MODEL), F32), jax.ShapeDtypeStruct((s, 1), F32),
                   jax.ShapeDtypeStruct((s, D_MODEL), BF16)],
        compiler_params=_cparams(1),
    )(x, g, b)


def proj_ln(a, w, layer, bias, xh_prev, g_prev, b_prev, g, b, name):
    s, k = a.shape
    tm = min(512, s)

    def body(a_ref, w_ref, bias_ref, xp_ref, gp_ref, bp_ref, g_ref, b_ref, xh_ref, rs_ref, xb_ref):
        acc = _dot(a_ref[...], w_ref[...]) + bias_ref[...]
        r = ALPHA * (xp_ref[...] * gp_ref[...] + bp_ref[...]) + acc
        xh, rs = _ln_rows(r)
        xh_ref[...] = xh
        rs_ref[...] = rs
        xb_ref[...] = _bf(xh * g_ref[...] + b_ref[...])

    row = pl.BlockSpec((tm, D_MODEL), lambda i: (i, 0))
    vec = pl.BlockSpec((1, D_MODEL), lambda i: (0, 0))
    return pl.pallas_call(
        body, name=name, grid=(s // tm,),
        in_specs=[pl.BlockSpec((tm, k), lambda i: (i, 0)),
                  pl.BlockSpec((None, k, D_MODEL), lambda i: (layer, 0, 0)),
                  vec, row, vec, vec, vec, vec],
        out_specs=[row, pl.BlockSpec((tm, 1), lambda i: (i, 0)), row],
        out_shape=[jax.ShapeDtypeStruct((s, D_MODEL), F32), jax.ShapeDtypeStruct((s, 1), F32),
                   jax.ShapeDtypeStruct((s, D_MODEL), BF16)],
        compiler_params=_cparams(1),
    )(a, w, bias, xh_prev, g_prev, b_prev, g, b)


def ln_bwd(dy, xh, rs, g, name):
    s = xh.shape[0]
    tm = min(512, s)

    def body(dy_ref, xh_ref, rs_ref, g_ref, dr_ref, dg_ref, db_ref, cs_ref):
        first = pl.program_id(0) == 0
        dyv = dy_ref[...]
        xhv = xh_ref[...]
        dxh = dyv * g_ref[...]
        dr = rs_ref[...] * (dxh - jnp.mean(dxh, -1, keepdims=True) - xhv * jnp.mean(dxh * xhv, -1, keepdims=True))
        dr_ref[...] = dr
        _acc(dg_ref, _colsum(dyv * xhv), first)
        _acc(db_ref, _colsum(dyv), first)
        _acc(cs_ref, _colsum(dr), first)

    row = pl.BlockSpec((tm, D_MODEL), lambda i: (i, 0))
    vec = pl.BlockSpec((1, D_MODEL), lambda i: (0, 0))
    vshape = jax.ShapeDtypeStruct((1, D_MODEL), F32)
    return pl.pallas_call(
        body, name=name, grid=(s // tm,),
        in_specs=[row, row, pl.BlockSpec((tm, 1), lambda i: (i, 0)), vec],
        out_specs=[row, vec, vec, vec],
        out_shape=[jax.ShapeDtypeStruct((s, D_MODEL), F32), vshape, vshape, vshape],
        compiler_params=_cparams(1),
    )(dy, xh, rs, g)


def loss_ln_bwd(target, xh, rs, g, b, name):
    s = xh.shape[0]
    tm = min(512, s)

    def body(t_ref, xh_ref, rs_ref, g_ref, b_ref, dr_ref, dg_ref, db_ref, sq_ref):
        first = pl.program_id(0) == 0
        xhv = xh_ref[...]
        err = xhv * g_ref[...] + b_ref[...] - t_ref[...]
        dyv = err * (1.0 / D_MODEL)
        dxh = dyv * g_ref[...]
        dr = rs_ref[...] * (dxh - jnp.mean(dxh, -1, keepdims=True) - xhv * jnp.mean(dxh * xhv, -1, keepdims=True))
        dr_ref[...] = dr
        _acc(dg_ref, _colsum(dyv * xhv), first)
        _acc(db_ref, _colsum(dyv), first)
        _acc(sq_ref, _colsum(err * err), first)

    row = pl.BlockSpec((tm, D_MODEL), lambda i: (i, 0))
    vec = pl.BlockSpec((1, D_MODEL), lambda i: (0, 0))
    vshape = jax.ShapeDtypeStruct((1, D_MODEL), F32)
    return pl.pallas_call(
        body, name=name, grid=(s // tm,),
        in_specs=[row, row, pl.BlockSpec((tm, 1), lambda i: (i, 0)), vec, vec],
        out_specs=[row, vec, vec, vec],
        out_shape=[jax.ShapeDtypeStruct((s, D_MODEL), F32), vshape, vshape, vshape],
        compiler_params=_cparams(1),
    )(target, xh, rs, g, b)


def mm_nn(a, w, layer, bias, out_dtype, tm, name):
    m, k = a.shape
    _, nj, _, n = w.shape
    tm = min(tm, m)

    def body(a_ref, w_ref, b_ref, o_ref):
        o_ref[...] = (_dot(_bf(a_ref[...]), w_ref[...]) + b_ref[...]).astype(out_dtype)

    return pl.pallas_call(
        body, name=name, grid=(m // tm, nj),
        in_specs=[pl.BlockSpec((tm, k), lambda i, j: (i, 0)),
                  pl.BlockSpec((None, None, k, n), lambda i, j: (layer, j, 0, 0)),
                  pl.BlockSpec((1, n), lambda i, j: (0, j))],
        out_specs=pl.BlockSpec((tm, n), lambda i, j: (i, j)),
        out_shape=jax.ShapeDtypeStruct((m, nj * n), out_dtype),
        compiler_params=_cparams(2),
    )(a, w, bias)


def mm_nt(a, a_block, a_map, w, layer, tr, add, out_dtype, tm, m, name):
    _, nj, r, n = w.shape
    tm = min(tm, m)
    has_add = add is not None

    def body(*refs):
        if has_add:
            a_ref, w_ref, add_ref, o_ref, acc_ref = refs
        else:
            a_ref, w_ref, o_ref, acc_ref = refs
        j = pl.program_id(2)
        part = _dot_nt(_bf(a_ref[...]), w_ref[...])

        @pl.when(j == 0)
        def _():
            acc_ref[...] = part

        @pl.when(j > 0)
        def _():
            acc_ref[...] += part

        @pl.when(j == nj - 1)
        def _():
            res = acc_ref[...]
            if has_add:
                res = res + ALPHA * add_ref[...]
            o_ref[...] = res.astype(out_dtype)

    in_specs = [pl.BlockSpec(a_block, a_map),
                pl.BlockSpec((None, None, tr, n), lambda i, rt, j: (layer, j, rt, 0))]
    ops = [a, w]
    if has_add:
        in_specs.append(pl.BlockSpec((tm, tr), lambda i, rt, j: (i, rt)))
        ops.append(add)
    return pl.pallas_call(
        body, name=name, grid=(m // tm, r // tr, nj),
        in_specs=in_specs,
        out_specs=pl.BlockSpec((tm, tr), lambda i, rt, j: (i, rt)),
        out_shape=jax.ShapeDtypeStruct((m, r), out_dtype),
        scratch_shapes=[pltpu.VMEM((tm, tr), F32)],
        compiler_params=_cparams(3),
    )(*ops)


def mm_tn(a, b, b_block, b_map, nj, n, tk, ts, s, layer, into, name):
    kx = a.shape[1]
    ts = min(ts, s)

    def body(a_ref, b_ref, *rest):
        o_ref = rest[-1]
        part = _dot_tn(_bf(a_ref[...]), _bf(b_ref[...]))
        _acc(o_ref, part, pl.program_id(2) == 0)

    in_specs = [pl.BlockSpec((ts, tk), lambda kt, j, st: (st, kt)), pl.BlockSpec(b_block, b_map)]
    ops = [a, b]
    aliases = {}
    if into is not None:
        in_specs.append(ANY)
        ops.append(into)
        aliases = {2: 0}
    return pl.pallas_call(
        body, name=name, grid=(kx // tk, nj, s // ts),
        in_specs=in_specs,
        out_specs=pl.BlockSpec((None, None, tk, n), lambda kt, j, st: (layer, j, kt, 0)),
        out_shape=jax.ShapeDtypeStruct((DEPTH, nj, kx, n), F32),
        input_output_aliases=aliases,
        compiler_params=_cparams(3),
    )(*ops)


def _cscan(br, bi, ar, ai, tb, reverse):
    row = _rows(br.shape)
    pr, pi = ar, ai
    sh = 1
    while sh < tb:
        if reverse:
            amt, mask = tb - sh, row < tb - sh
        else:
            amt, mask = sh, row >= sh
        sr = jnp.where(mask, pltpu.roll(br, amt, 0), 0.0)
        si = jnp.where(mask, pltpu.roll(bi, amt, 0), 0.0)
        br, bi = br + pr * sr - pi * si, bi + pr * si + pi * sr
        pr, pi = pr * pr - pi * pi, 2.0 * pr * pi
        sh *= 2
    return br, bi


def _power_table(ar, ai, tb, reverse):
    row = _rows((tb, N_STATE))
    seed_row = tb - 1 if reverse else 0
    er = jnp.where(row == seed_row, ar, 0.0)
    ei = jnp.where(row == seed_row, ai, 0.0)
    return _cscan(er, ei, ar, ai, tb, reverse)


def s5_fwd(h_in, a2, bexp, cexp, dskip, wglu, layer, bglu, tb, name):
    s = h_in.shape[0]
    n = N_STATE

    def body(u_ref, a_ref, b_ref, c_ref, d_ref, w_ref, bg_ref, y_ref, h_ref, y0_ref, carry, ptab):
        ar, ai = a_ref[0:1, 0:n], a_ref[0:1, n:2 * n]

        @pl.when(pl.program_id(0) == 0)
        def _():
            carry[...] = jnp.zeros_like(carry)
            pr0, pi0 = _power_table(ar, ai, tb, False)
            ptab[:, 0:n] = pr0
            ptab[:, n:2 * n] = pi0

        u = u_ref[...]
        bu = _dot(_bf(u), b_ref[...])
        lr, li = _cscan(bu[:, 0:n], bu[:, n:2 * n], ar, ai, tb, False)
        cr, ci = carry[7:8, 0:n], carry[7:8, n:2 * n]
        pr, pi = ptab[:, 0:n], ptab[:, n:2 * n]
        hr = lr + pr * cr - pi * ci
        hi = li + pr * ci + pi * cr
        h_ref[:, 0:n] = hr
        h_ref[:, n:2 * n] = hi
        carry[:, 0:n] = hr[tb - 8:tb]
        carry[:, n:2 * n] = hi[tb - 8:tb]
        y0 = _dot(_bf(hr), c_ref[0:n, :]) + _dot(_bf(hi), c_ref[n:2 * n, :]) + d_ref[...] * u
        y0_ref[...] = y0
        yg = _gelu(y0)
        z = _dot(_bf(yg), w_ref[...]) + bg_ref[...]
        y_ref[...] = _bf(yg * _sigmoid(z))

    vec = pl.BlockSpec((1, D_GROUP), lambda i: (0, 0))
    return pl.pallas_call(
        body, name=name, grid=(s // tb,),
        in_specs=[pl.BlockSpec((tb, D_GROUP), lambda i: (i, 0)),
                  pl.BlockSpec((1, 2 * n), lambda i: (0, 0)),
                  pl.BlockSpec((D_GROUP, 2 * n), lambda i: (0, 0)),
                  pl.BlockSpec((2 * n, D_GROUP), lambda i: (0, 0)),
                  vec,
                  pl.BlockSpec((None, D_GROUP, D_GROUP), lambda i: (layer, 0, 0)),
                  vec],
        out_specs=[pl.BlockSpec((tb, D_GROUP), lambda i: (i, 0)),
                   pl.BlockSpec((tb, 2 * n), lambda i: (i, 0)),
                   pl.BlockSpec((tb, D_GROUP), lambda i: (i, 0))],
        out_shape=[jax.ShapeDtypeStruct((s, D_GROUP), BF16), jax.ShapeDtypeStruct((s, 2 * n), F32),
                   jax.ShapeDtypeStruct((s, D_GROUP), F32)],
        scratch_shapes=[pltpu.VMEM((8, 2 * n), F32), pltpu.VMEM((tb, 2 * n), F32)],
        compiler_params=_cparams(1),
    )(h_in, a2, bexp, cexp, dskip, wglu, bglu)


def s5_bwd(dmix, h_in, y0, hst, a2, bexp, cexp, dskip, wglu, layer, bglu, tb, name):
    s = h_in.shape[0]
    n = N_STATE
    nb = s // tb
    halo = tb // 8

    def body(dy_ref, u_ref, y0_ref, h_ref, hp_ref, a_ref, b_ref, c_ref, d_ref, w_ref, bg_ref,
             du_ref, cs_ref, db_ref, dc_ref, dd_ref, dw_ref, dbg_ref, da_ref, carry, qtab):
        i = pl.program_id(0)
        first = i == 0
        ar, ai = a_ref[0:1, 0:n], -a_ref[0:1, n:2 * n]

        @pl.when(first)
        def _():
            carry[...] = jnp.zeros_like(carry)
            qr0, qi0 = _power_table(ar, ai, tb, True)
            qtab[:, 0:n] = qr0
            qtab[:, n:2 * n] = qi0

        dy = dy_ref[...]
        u = u_ref[...]
        y0v = y0_ref[...]
        yg, t = _gelu_parts(y0v)
        z = _dot(_bf(yg), w_ref[...]) + bg_ref[...]
        sg = _sigmoid(z)
        dz = dy * yg * sg * (1.0 - sg)
        dyg = dy * sg + _dot_nt(_bf(dz), w_ref[...])
        _acc(dw_ref, _dot_tn(_bf(yg), _bf(dz)), first)
        _acc(dbg_ref, _colsum(dz), first)
        dy0 = dyg * _gelu_grad(y0v, t)
        _acc(dd_ref, _colsum(dy0 * u), first)
        hr, hi = h_ref[:, 0:n], h_ref[:, n:2 * n]
        dy0b = _bf(dy0)
        _acc(dc_ref.at[0:n, :], _dot_tn(_bf(hr), dy0b), first)
        _acc(dc_ref.at[n:2 * n, :], _dot_tn(_bf(hi), dy0b), first)
        g = _dot_nt(dy0b, c_ref[...])
        lr, li = _cscan(g[:, 0:n], g[:, n:2 * n], ar, ai, tb, True)
        cr, ci = carry[0:1, 0:n], carry[0:1, n:2 * n]
        qr, qi = qtab[:, 0:n], qtab[:, n:2 * n]
        gr = lr + qr * cr - qi * ci
        gi = li + qr * ci + qi * cr
        carry[:, 0:n] = gr[0:8]
        carry[:, n:2 * n] = gi[0:8]
        keep = jnp.where(i == nb - 1, 0.0, 1.0)
        row = _rows((tb, n))
        pr = jnp.where(row == 0, hp_ref[7:8, 0:n] * keep, pltpu.roll(hr, 1, 0))
        pi = jnp.where(row == 0, hp_ref[7:8, n:2 * n] * keep, pltpu.roll(hi, 1, 0))
        _acc(da_ref.at[:, 0:n], _colsum(gr * pr + gi * pi), first)
        _acc(da_ref.at[:, n:2 * n], _colsum(gi * pr - gr * pi), first)
        grb, gib = _bf(gr), _bf(gi)
        du = d_ref[...] * dy0 + _dot_nt(grb, b_ref[:, 0:n]) + _dot_nt(gib, b_ref[:, n:2 * n])
        ub = _bf(u)
        _acc(db_ref.at[:, 0:n], _dot_tn(ub, grb), first)
        _acc(db_ref.at[:, n:2 * n], _dot_tn(ub, gib), first)
        du_ref[...] = _bf(du)
        _acc(cs_ref, _colsum(du), first)

    rev = lambda i: (nb - 1 - i, 0)
    vec = pl.BlockSpec((1, D_GROUP), lambda i: (0, 0))
    vshape = jax.ShapeDtypeStruct((1, D_GROUP), F32)
    return pl.pallas_call(
        body, name=name, grid=(nb,),
        in_specs=[pl.BlockSpec((tb, D_GROUP), rev),
                  pl.BlockSpec((tb, D_GROUP), rev),
                  pl.BlockSpec((tb, D_GROUP), rev),
                  pl.BlockSpec((tb, 2 * n), rev),
                  pl.BlockSpec((8, 2 * n), lambda i: (jnp.maximum((nb - 1 - i) * halo - 1, 0), 0)),
                  pl.BlockSpec((1, 2 * n), lambda i: (0, 0)),
                  pl.BlockSpec((D_GROUP, 2 * n), lambda i: (0, 0)),
                  pl.BlockSpec((2 * n, D_GROUP), lambda i: (0, 0)),
                  vec,
                  pl.BlockSpec((None, D_GROUP, D_GROUP), lambda i: (layer, 0, 0)),
                  vec],
        out_specs=[pl.BlockSpec((tb, D_GROUP), rev), vec,
                   pl.BlockSpec((D_GROUP, 2 * n), lambda i: (0, 0)),
                   pl.BlockSpec((2 * n, D_GROUP), lambda i: (0, 0)),
                   vec,
                   pl.BlockSpec((D_GROUP, D_GROUP), lambda i: (0, 0)),
                   vec,
                   pl.BlockSpec((1, 2 * n), lambda i: (0, 0))],
        out_shape=[jax.ShapeDtypeStruct((s, D_GROUP), BF16), vshape,
                   jax.ShapeDtypeStruct((D_GROUP, 2 * n), F32), jax.ShapeDtypeStruct((2 * n, D_GROUP), F32),
                   vshape, jax.ShapeDtypeStruct((D_GROUP, D_GROUP), F32), vshape,
                   jax.ShapeDtypeStruct((1, 2 * n), F32)],
        scratch_shapes=[pltpu.VMEM((8, 2 * n), F32), pltpu.VMEM((tb, 2 * n), F32)],
        compiler_params=_cparams(1),
    )(dmix, h_in, y0, hst, hst, a2, bexp, cexp, dskip, wglu, bglu)


CV_HALO = 32


def _gn_stats(hc, pmat):
    mu = _dot3(hc, pmat)
    xc = hc - mu
    var = _dot3(xc * xc, pmat)
    rstd = lax.rsqrt(var + LN_EPS)
    return xc * rstd, rstd


def cv_fwd(h_in, cw, cb, gg, gb, pmat, wpw, layer, bpw, tb, name):
    s = h_in.shape[0]
    hl = CV_HALO

    def body(v_ref, g_ref, cw_ref, cb_ref, gg_ref, gb_ref, p_ref, w_ref, bw_ref, y_ref, hc_ref, ext):
        @pl.when(pl.program_id(0) == 0)
        def _():
            ext[0:hl, :] = jnp.zeros((hl, D_GROUP), F32)

        ext[hl:hl + tb, :] = v_ref[...] * _sigmoid(g_ref[...])
        acc = jnp.zeros((tb, D_GROUP), F32) + cb_ref[...]
        for k in range(CONV_WIDTH):
            off = hl - (CONV_WIDTH - 1) + k
            acc = acc + cw_ref[k:k + 1, :] * ext[off:off + tb, :]
        hc_ref[...] = acc
        ext[0:hl, :] = ext[tb:tb + hl, :]
        xn, _ = _gn_stats(acc, p_ref[...])
        hn = xn * gg_ref[...] + gb_ref[...]
        hs = hn * _sigmoid(hn)
        y_ref[...] = _bf(_dot(_bf(hs), w_ref[...]) + bw_ref[...])

    vec = pl.BlockSpec((1, D_GROUP), lambda i: (0, 0))
    sq = pl.BlockSpec((D_GROUP, D_GROUP), lambda i: (0, 0))
    return pl.pallas_call(
        body, name=name, grid=(s // tb,),
        in_specs=[pl.BlockSpec((tb, D_GROUP), lambda i: (i, 1)),
                  pl.BlockSpec((tb, D_GROUP), lambda i: (i, 2)),
                  pl.BlockSpec((hl, D_GROUP), lambda i: (0, 0)),
                  vec, vec, vec, sq,
                  pl.BlockSpec((None, D_GROUP, D_GROUP), lambda i: (layer, 0, 0)),
                  vec],
        out_specs=[pl.BlockSpec((tb, D_GROUP), lambda i: (i, 0)), pl.BlockSpec((tb, D_GROUP), lambda i: (i, 0))],
        out_shape=[jax.ShapeDtypeStruct((s, D_GROUP), BF16), jax.ShapeDtypeStruct((s, D_GROUP), F32)],
        scratch_shapes=[pltpu.VMEM((hl + tb, D_GROUP), F32)],
        compiler_params=_cparams(1),
    )(h_in, h_in, cw, cb, gg, gb, pmat, wpw, bpw)


def cv_bwd(dmix, h_in, hc, cw, gg, gb, pmat, wpw, layer, tb, name):
    s = h_in.shape[0]
    hl = CV_HALO
    nb = s // tb
    per = tb // hl

    def body(dy_ref, v_ref, g_ref, vh_ref, gh_ref, hc_ref, cw_ref, gg_ref, gb_ref, p_ref, w_ref,
             dvg_ref, cs_ref, dcw_ref, dcb_ref, dgg_ref, dgb_ref, dw_ref, dbw_ref, ext, dext, head):
        i = pl.program_id(0)
        first = i == 0

        @pl.when(first)
        def _():
            head[...] = jnp.zeros_like(head)

        dy = dy_ref[...]
        pm = p_ref[...]
        xn, rstd = _gn_stats(hc_ref[...], pm)
        hn = xn * gg_ref[...] + gb_ref[...]
        sg = _sigmoid(hn)
        hs = hn * sg
        dyb = _bf(dy)
        _acc(dbw_ref, _colsum(dy), first)
        _acc(dw_ref, _dot_tn(_bf(hs), dyb), first)
        dhs = _dot_nt(dyb, w_ref[...])
        dhn = dhs * sg * (1.0 + hn * (1.0 - sg))
        _acc(dgg_ref, _colsum(dhn * xn), first)
        _acc(dgb_ref, _colsum(dhn), first)
        dxn = dhn * gg_ref[...]
        dhc = rstd * (dxn - _dot3(dxn, pm) - xn * _dot3(dxn * xn, pm))
        _acc(dcb_ref, _colsum(dhc), first)
        v = v_ref[...]
        sgg = _sigmoid(g_ref[...])
        keep = jnp.where(i == nb - 1, 0.0, 1.0)
        ext[0:hl, :] = vh_ref[...] * _sigmoid(gh_ref[...]) * keep
        ext[hl:hl + tb, :] = v * sgg
        dext[0:tb, :] = dhc
        dext[tb:tb + hl, :] = head[...]
        head[...] = dhc[0:hl]
        dhg = jnp.zeros((tb, D_GROUP), F32)
        for k in range(CONV_WIDTH):
            off = hl - (CONV_WIDTH - 1) + k
            wk = _colsum(dhc * ext[off:off + tb, :])
            _acc(dcw_ref.at[k:k + 1, :], wk, first)
            back = CONV_WIDTH - 1 - k
            dhg = dhg + cw_ref[k:k + 1, :] * dext[back:back + tb, :]

        @pl.when(first)
        def _():
            dcw_ref[CONV_WIDTH:hl, :] = jnp.zeros((hl - CONV_WIDTH, D_GROUP), F32)

        dv = dhg * sgg
        dg = dhg * v * sgg * (1.0 - sgg)
        dvg_ref[:, 0:D_GROUP] = _bf(dv)
        dvg_ref[:, D_GROUP:2 * D_GROUP] = _bf(dg)
        _acc(cs_ref.at[:, 0:D_GROUP], _colsum(dv), first)
        _acc(cs_ref.at[:, D_GROUP:2 * D_GROUP], _colsum(dg), first)

    vec = pl.BlockSpec((1, D_GROUP), lambda i: (0, 0))
    sq = pl.BlockSpec((D_GROUP, D_GROUP), lambda i: (0, 0))
    tap = pl.BlockSpec((hl, D_GROUP), lambda i: (0, 0))
    vshape = jax.ShapeDtypeStruct((1, D_GROUP), F32)

    def blk(col):
        return pl.BlockSpec((tb, D_GROUP), lambda i: (nb - 1 - i, col))

    def halo_blk(col):
        return pl.BlockSpec((hl, D_GROUP), lambda i: (jnp.maximum((nb - 1 - i) * per - 1, 0), col))

    return pl.pallas_call(
        body, name=name, grid=(nb,),
        in_specs=[blk(1), blk(1), blk(2), halo_blk(1), halo_blk(2),
                  pl.BlockSpec((tb, D_GROUP), lambda i: (nb - 1 - i, 0)),
                  tap, vec, vec, sq,
                  pl.BlockSpec((None, D_GROUP, D_GROUP), lambda i: (layer, 0, 0))],
        out_specs=[pl.BlockSpec((tb, 2 * D_GROUP), lambda i: (nb - 1 - i, 0)),
                   pl.BlockSpec((1, 2 * D_GROUP), lambda i: (0, 0)),
                   tap, vec, vec, vec, sq, vec],
        out_shape=[jax.ShapeDtypeStruct((s, 2 * D_GROUP), BF16), jax.ShapeDtypeStruct((1, 2 * D_GROUP), F32),
                   jax.ShapeDtypeStruct((hl, D_GROUP), F32), vshape, vshape, vshape,
                   jax.ShapeDtypeStruct((D_GROUP, D_GROUP), F32), vshape],
        scratch_shapes=[pltpu.VMEM((hl + tb, D_GROUP), F32), pltpu.VMEM((tb + hl, D_GROUP), F32),
                        pltpu.VMEM((hl, D_GROUP), F32)],
        compiler_params=_cparams(1),
    )(dmix, h_in, h_in, h_in, h_in, hc, cw, gg, gb, pmat, wpw)


LRU_HALO = 8


def _lru_gates(xc, wr_ref, br_ref, wi_ref, bi_ref, sp_ref):
    xcb = _bf(xc)
    r = _sigmoid(_dot(xcb, wr_ref[...]) + br_ref[...])
    gi = _sigmoid(_dot(xcb, wi_ref[...]) + bi_ref[...])
    la = -LRU_C * r * sp_ref[...]
    a = jnp.exp(la)
    e2 = a * a
    sq = jnp.sqrt(-jnp.tanh(la) * (e2 + 1.0))
    return r, gi, a, e2, sq


def _rscan(a, b, tb, reverse):
    row = _rows(a.shape)
    sh = 1
    while sh < tb:
        if reverse:
            amt, mask = tb - sh, row < tb - sh
        else:
            amt, mask = sh, row >= sh
        a_s = jnp.where(mask, pltpu.roll(a, amt, 0), 1.0)
        b_s = jnp.where(mask, pltpu.roll(b, amt, 0), 0.0)
        b = b + a * b_s
        a = a * a_s
        sh *= 2
    return a, b


def lru_fwd(h_in, cw, cb, wr, br, wi, bi, sp, tb, name):
    s = h_in.shape[0]
    hl = LRU_HALO

    def body(xg_ref, xr_ref, cw_ref, cb_ref, wr_ref, br_ref, wi_ref, bi_ref, sp_ref, y_ref, xc_ref, h_ref, ext, carry):
        @pl.when(pl.program_id(0) == 0)
        def _():
            ext[0:hl, :] = jnp.zeros((hl, D_GROUP), F32)
            carry[...] = jnp.zeros_like(carry)

        ext[hl:hl + tb, :] = xr_ref[...]
        xc = jnp.zeros((tb, D_GROUP), F32) + cb_ref[...]
        for k in range(LRU_CONV_WIDTH):
            off = hl - (LRU_CONV_WIDTH - 1) + k
            xc = xc + cw_ref[k:k + 1, :] * ext[off:off + tb, :]
        xc_ref[...] = xc
        ext[0:hl, :] = ext[tb:tb + hl, :]
        r, gi, a, e2, sq = _lru_gates(xc, wr_ref, br_ref, wi_ref, bi_ref, sp_ref)
        pa, hloc = _rscan(a, sq * (gi * xc), tb, False)
        h = hloc + pa * carry[7:8, :]
        h_ref[...] = h
        carry[...] = h[tb - 8:tb]
        y_ref[...] = _bf(h * _gelu(xg_ref[...]))

    vec = pl.BlockSpec((1, D_GROUP), lambda i: (0, 0))
    sq_spec = pl.BlockSpec((D_GROUP, D_GROUP), lambda i: (0, 0))
    blk = pl.BlockSpec((tb, D_GROUP), lambda i: (i, 0))
    return pl.pallas_call(
        body, name=name, grid=(s // tb,),
        in_specs=[pl.BlockSpec((tb, D_GROUP), lambda i: (i, 3)),
                  pl.BlockSpec((tb, D_GROUP), lambda i: (i, 4)),
                  pl.BlockSpec((hl, D_GROUP), lambda i: (0, 0)),
                  vec, sq_spec, vec, sq_spec, vec, vec],
        out_specs=[blk, blk, blk],
        out_shape=[jax.ShapeDtypeStruct((s, D_GROUP), BF16), jax.ShapeDtypeStruct((s, D_GROUP), F32),
                   jax.ShapeDtypeStruct((s, D_GROUP), F32)],
        scratch_shapes=[pltpu.VMEM((hl + tb, D_GROUP), F32), pltpu.VMEM((8, D_GROUP), F32)],
        compiler_params=_cparams(1),
    )(h_in, h_in, cw, cb, wr, br, wi, bi, sp)


def lru_bwd(dmix, h_in, xcs, hs, cw, wr, br, wi, bi, sp, tb, name):
    s = h_in.shape[0]
    hl = LRU_HALO
    nb = s // tb
    per = tb // hl

    def body(dy_ref, xg_ref, xr_ref, xrh_ref, xc_ref, h_ref, hp_ref, cw_ref, wr_ref, br_ref, wi_ref, bi_ref, sp_ref,
             dx_ref, cs_ref, dcw_ref, dcb_ref, dwr_ref, dbr_ref, dwi_ref, dbi_ref, dsp_ref,
             ext, dext, head, anext, gnext):
        i = pl.program_id(0)
        first = i == 0

        @pl.when(first)
        def _():
            head[...] = jnp.zeros_like(head)
            anext[...] = jnp.zeros_like(anext)
            gnext[...] = jnp.zeros_like(gnext)

        dy = dy_ref[...]
        xg = xg_ref[...]
        xc = xc_ref[...]
        h = h_ref[...]
        r, gi, a, e2, sq = _lru_gates(xc, wr_ref, br_ref, wi_ref, bi_ref, sp_ref)
        gate, t = _gelu_parts(xg)
        dh = dy * gate
        dxg = dy * h * _gelu_grad(xg, t)
        row = _rows((tb, D_GROUP))
        coef = jnp.where(row == tb - 1, anext[0:1, :], pltpu.roll(a, tb - 1, 0))
        pc, gloc = _rscan(coef, dh, tb, True)
        gfull = gloc + pc * gnext[0:1, :]
        anext[...] = a[0:8]
        gnext[...] = gfull[0:8]
        keep = jnp.where(i == nb - 1, 0.0, 1.0)
        hprev = jnp.where(row == 0, hp_ref[7:8, :] * keep, pltpu.roll(h, 1, 0))
        da = gfull * hprev
        uu = gi * xc
        dsq = gfull * uu
        duu = gfull * sq
        dla = da * a - dsq * e2 / sq
        sp = sp_ref[...]
        dr = dla * (-LRU_C) * sp
        _acc(dsp_ref, _colsum(dla * (-LRU_C) * r), first)
        dzr = dr * r * (1.0 - r)
        dzi = duu * xc * gi * (1.0 - gi)
        dzrb, dzib = _bf(dzr), _bf(dzi)
        dxc = duu * gi + _dot_nt(dzrb, wr_ref[...]) + _dot_nt(dzib, wi_ref[...])
        xcb = _bf(xc)
        _acc(dwr_ref, _dot_tn(xcb, dzrb), first)
        _acc(dwi_ref, _dot_tn(xcb, dzib), first)
        _acc(dbr_ref, _colsum(dzr), first)
        _acc(dbi_ref, _colsum(dzi), first)
        _acc(dcb_ref, _colsum(dxc), first)
        ext[0:hl, :] = xrh_ref[...] * keep
        ext[hl:hl + tb, :] = xr_ref[...]
        dext[0:tb, :] = dxc
        dext[tb:tb + hl, :] = head[...]
        head[...] = dxc[0:hl]
        dxr = jnp.zeros((tb, D_GROUP), F32)
        for k in range(LRU_CONV_WIDTH):
            off = hl - (LRU_CONV_WIDTH - 1) + k
            _acc(dcw_ref.at[k:k + 1, :], _colsum(dxc * ext[off:off + tb, :]), first)
            back = LRU_CONV_WIDTH - 1 - k
            dxr = dxr + cw_ref[k:k + 1, :] * dext[back:back + tb, :]

        @pl.when(first)
        def _():
            dcw_ref[LRU_CONV_WIDTH:hl, :] = jnp.zeros((hl - LRU_CONV_WIDTH, D_GROUP), F32)

        dx_ref[:, 0:D_GROUP] = _bf(dxg)
        dx_ref[:, D_GROUP:2 * D_GROUP] = _bf(dxr)
        _acc(cs_ref.at[:, 0:D_GROUP], _colsum(dxg), first)
        _acc(cs_ref.at[:, D_GROUP:2 * D_GROUP], _colsum(dxr), first)

    vec = pl.BlockSpec((1, D_GROUP), lambda i: (0, 0))
    sq_spec = pl.BlockSpec((D_GROUP, D_GROUP), lambda i: (0, 0))
    tap = pl.BlockSpec((hl, D_GROUP), lambda i: (0, 0))
    vshape = jax.ShapeDtypeStruct((1, D_GROUP), F32)
    sshape = jax.ShapeDtypeStruct((D_GROUP, D_GROUP), F32)

    def blk(col):
        return pl.BlockSpec((tb, D_GROUP), lambda i: (nb - 1 - i, col))

    def halo_blk(col):
        return pl.BlockSpec((hl, D_GROUP), lambda i: (jnp.maximum((nb - 1 - i) * per - 1, 0), col))

    return pl.pallas_call(
        body, name=name, grid=(nb,),
        in_specs=[blk(2), blk(3), blk(4), halo_blk(4), blk(0), blk(0), halo_blk(0),
                  tap, sq_spec, vec, sq_spec, vec, vec],
        out_specs=[pl.BlockSpec((tb, 2 * D_GROUP), lambda i: (nb - 1 - i, 0)),
                   pl.BlockSpec((1, 2 * D_GROUP), lambda i: (0, 0)),
                   tap, vec, sq_spec, vec, sq_spec, vec, vec],
        out_shape=[jax.ShapeDtypeStruct((s, 2 * D_GROUP), BF16), jax.ShapeDtypeStruct((1, 2 * D_GROUP), F32),
                   jax.ShapeDtypeStruct((hl, D_GROUP), F32), vshape, sshape, vshape, sshape, vshape, vshape],
        scratch_shapes=[pltpu.VMEM((hl + tb, D_GROUP), F32), pltpu.VMEM((tb + hl, D_GROUP), F32),
                        pltpu.VMEM((hl, D_GROUP), F32), pltpu.VMEM((8, D_GROUP), F32), pltpu.VMEM((8, D_GROUP), F32)],
        compiler_params=_cparams(1),
    )(dmix, h_in, h_in, h_in, xcs, hs, hs, cw, wr, br, wi, bi, sp)


ATTN_HEADS = 4
ATTN_HEAD_DIM = 64
ATTN_SCALE = ATTN_HEAD_DIM ** -0.5


def _head_mask(h):
    lane = lax.broadcasted_iota(jnp.int32, (1, D_GROUP), 1)
    return jnp.where((lane >= h * ATTN_HEAD_DIM) & (lane < (h + 1) * ATTN_HEAD_DIM), 1.0, 0.0)


def _softmax_rows(sc):
    e = jnp.exp(sc - jnp.max(sc, -1, keepdims=True))
    return e / jnp.sum(e, -1, keepdims=True)


def attn_fwd(h_in, kv, tb, name):
    s = h_in.shape[0]

    def body(q_ref, kv_ref, y_ref):
        q = q_ref[...]
        kb = _bf(kv_ref[:, 0:D_GROUP])
        vb = _bf(kv_ref[:, D_GROUP:2 * D_GROUP])
        out = jnp.zeros((tb, D_GROUP), F32)
        for h in range(ATTN_HEADS):
            mask = _head_mask(h)
            p = _softmax_rows(_dot_nt(_bf(q * mask), kb) * ATTN_SCALE)
            out = out + _dot(_bf(p), vb) * mask
        y_ref[...] = _bf(out)

    return pl.pallas_call(
        body, name=name, grid=(s // tb,),
        in_specs=[pl.BlockSpec((tb, D_GROUP), lambda i: (i, 5)),
                  pl.BlockSpec((D_GROUP, 2 * D_GROUP), lambda i: (0, 0))],
        out_specs=pl.BlockSpec((tb, D_GROUP), lambda i: (i, 0)),
        out_shape=jax.ShapeDtypeStruct((s, D_GROUP), BF16),
        compiler_params=_cparams(1),
    )(h_in, kv)


def attn_bwd(dmix, h_in, kv, tb, name):
    s = h_in.shape[0]

    def body(do_ref, q_ref, kv_ref, dq_ref, cs_ref, dkv_ref):
        first = pl.program_id(0) == 0
        q = q_ref[...]
        do = do_ref[...]
        kb = _bf(kv_ref[:, 0:D_GROUP])
        vb = _bf(kv_ref[:, D_GROUP:2 * D_GROUP])
        dq = jnp.zeros((tb, D_GROUP), F32)
        dk = jnp.zeros((D_GROUP, D_GROUP), F32)
        dv = jnp.zeros((D_GROUP, D_GROUP), F32)
        for h in range(ATTN_HEADS):
            mask = _head_mask(h)
            qm = _bf(q * mask)
            p = _softmax_rows(_dot_nt(qm, kb) * ATTN_SCALE)
            dom = _bf(do * mask)
            dp = _dot_nt(dom, vb)
            dv = dv + _dot_tn(_bf(p), dom)
            ds = _bf(p * (dp - jnp.sum(dp * p, -1, keepdims=True)) * ATTN_SCALE)
            dq = dq + _dot(ds, kb) * mask
            dk = dk + _dot_tn(ds, qm)
        dq_ref[...] = _bf(dq)
        _acc(cs_ref, _colsum(dq), first)
        _acc(dkv_ref.at[:, 0:D_GROUP], dk, first)
        _acc(dkv_ref.at[:, D_GROUP:2 * D_GROUP], dv, first)

    return pl.pallas_call(
        body, name=name, grid=(s // tb,),
        in_specs=[pl.BlockSpec((tb, D_GROUP), lambda i: (i, 3)),
                  pl.BlockSpec((tb, D_GROUP), lambda i: (i, 5)),
                  pl.BlockSpec((D_GROUP, 2 * D_GROUP), lambda i: (0, 0))],
        out_specs=[pl.BlockSpec((tb, D_GROUP), lambda i: (i, 0)),
                   pl.BlockSpec((1, D_GROUP), lambda i: (0, 0)),
                   pl.BlockSpec((D_GROUP, 2 * D_GROUP), lambda i: (0, 0))],
        out_shape=[jax.ShapeDtypeStruct((s, D_GROUP), BF16), jax.ShapeDtypeStruct((1, D_GROUP), F32),
                   jax.ShapeDtypeStruct((D_GROUP, 2 * D_GROUP), F32)],
        compiler_params=_cparams(1),
    )(dmix, h_in, kv)


FFN_RB = 16
FFN_UNROLL_FWD = 4
FFN_UNROLL_BWD = 2
FFN_TAP_ROWS = 8
FFN_TN = D_FF // 2


def _shift_down(cur, tail, k):
    return pltpu.roll(jnp.concatenate([tail, cur], axis=0), k, 0)[SUBLANES:]


def _shift_up(cur, head, k):
    rb = cur.shape[0]
    return pltpu.roll(jnp.concatenate([cur, head], axis=0), rb + SUBLANES - k, 0)[:rb]


def _fold8(v):
    tot = v[0:SUBLANES]
    for t in range(1, v.shape[0] // SUBLANES):
        tot = tot + v[t * SUBLANES:(t + 1) * SUBLANES]
    return tot


def _strip(r):
    return pl.ds(pl.multiple_of(r * FFN_RB, FFN_RB), FFN_RB)


def _tail_before(r):
    return pl.ds(pl.multiple_of(jnp.maximum(r * FFN_RB - SUBLANES, 0), SUBLANES), SUBLANES)


def ffn_act_fwd(u, cw, cb, tb, name):
    s = u.shape[0]
    rb = FFN_RB
    nct = D_FF // FFN_TN
    nstrip = tb // rb

    def body(uv_ref, ug_ref, wv_ref, wg_ref, bv_ref, bg_ref, hf_ref, tailv, tailg):
        @pl.when(pl.program_id(1) == 0)
        def _():
            tailv[...] = jnp.zeros_like(tailv)
            tailg[...] = jnp.zeros_like(tailg)

        for cc in range(FFN_TN // LANES):
            cols = slice(cc * LANES, (cc + 1) * LANES)
            wv = [wv_ref[k:k + 1, cols] for k in range(FFN_CONV_WIDTH)]
            wg = [wg_ref[k:k + 1, cols] for k in range(FFN_CONV_WIDTH)]
            bv, bg = bv_ref[:, cols], bg_ref[:, cols]

            def strip(r, carry):
                tail_v, tail_g = carry
                cur_v, cur_g = uv_ref[_strip(r), cols], ug_ref[_strip(r), cols]
                vc = wv[0] * _shift_down(cur_v, tail_v, 2) + wv[1] * _shift_down(cur_v, tail_v, 1) + wv[2] * cur_v + bv
                gc = wg[0] * _shift_down(cur_g, tail_g, 2) + wg[1] * _shift_down(cur_g, tail_g, 1) + wg[2] * cur_g + bg
                hf_ref[_strip(r), cols] = _bf(vc * _gelu(gc))
                return cur_v[rb - SUBLANES:], cur_g[rb - SUBLANES:]

            def strips(q, carry):
                for k in range(FFN_UNROLL_FWD):
                    carry = strip(q * FFN_UNROLL_FWD + k, carry)
                return carry

            last_v, last_g = lax.fori_loop(0, nstrip // FFN_UNROLL_FWD, strips, (tailv[:, cols], tailg[:, cols]))
            tailv[:, cols] = last_v
            tailg[:, cols] = last_g

    return pl.pallas_call(
        body, name=name, grid=(nct, s // tb),
        in_specs=[pl.BlockSpec((tb, FFN_TN), lambda c, i: (i, c)),
                  pl.BlockSpec((tb, FFN_TN), lambda c, i: (i, c + nct)),
                  pl.BlockSpec((FFN_TAP_ROWS, FFN_TN), lambda c, i: (0, c)),
                  pl.BlockSpec((FFN_TAP_ROWS, FFN_TN), lambda c, i: (0, c + nct)),
                  pl.BlockSpec((1, FFN_TN), lambda c, i: (0, c)),
                  pl.BlockSpec((1, FFN_TN), lambda c, i: (0, c + nct))],
        out_specs=pl.BlockSpec((tb, FFN_TN), lambda c, i: (i, c)),
        out_shape=jax.ShapeDtypeStruct((s, D_FF), BF16),
        scratch_shapes=[pltpu.VMEM((SUBLANES, FFN_TN), F32), pltpu.VMEM((SUBLANES, FFN_TN), F32)],
        compiler_params=_cparams(2),
    )(u, u, cw, cw, cb, cb)


def ffn_act_bwd(dhf, u, cw, cb, tb, name):
    s = u.shape[0]
    rb = FFN_RB
    nct = D_FF // FFN_TN
    nb = s // tb
    per = tb // SUBLANES
    nstrip = tb // rb
    ntap = FFN_CONV_WIDTH

    def body(dh_ref, uv_ref, ug_ref, uvh_ref, ugh_ref, wv_ref, wg_ref, bv_ref, bg_ref,
             du_ref, dwv_ref, dwg_ref, headv, headg):
        i = pl.program_id(1)
        first = i == 0

        @pl.when(first)
        def _():
            headv[...] = jnp.zeros_like(headv)
            headg[...] = jnp.zeros_like(headg)
            dwv_ref[...] = jnp.zeros_like(dwv_ref)
            dwg_ref[...] = jnp.zeros_like(dwg_ref)

        keep = jnp.where(i == nb - 1, 0.0, 1.0)
        zero = jnp.zeros((SUBLANES, LANES), F32)
        for cc in range(FFN_TN // LANES):
            cols = slice(cc * LANES, (cc + 1) * LANES)
            wv = [wv_ref[k:k + 1, cols] for k in range(ntap)]
            wg = [wg_ref[k:k + 1, cols] for k in range(ntap)]
            bv, bg = bv_ref[:, cols], bg_ref[:, cols]
            halo_v, halo_g = uvh_ref[:, cols] * keep, ugh_ref[:, cols] * keep

            def strip(ii, carry):
                head_dv, head_dg, acc_v, acc_g = carry
                r = nstrip - 1 - ii
                cur_v, cur_g = uv_ref[_strip(r), cols], ug_ref[_strip(r), cols]
                tail_v = jnp.where(r == 0, halo_v, uv_ref[_tail_before(r), cols])
                tail_g = jnp.where(r == 0, halo_g, ug_ref[_tail_before(r), cols])
                sv = [_shift_down(cur_v, tail_v, 2), _shift_down(cur_v, tail_v, 1), cur_v]
                sg = [_shift_down(cur_g, tail_g, 2), _shift_down(cur_g, tail_g, 1), cur_g]
                vc = wv[0] * sv[0] + wv[1] * sv[1] + wv[2] * sv[2] + bv
                gc = wg[0] * sg[0] + wg[1] * sg[1] + wg[2] * sg[2] + bg
                ge, t = _gelu_parts(gc)
                dh = dh_ref[_strip(r), cols]
                dvc = dh * ge
                dgc = dh * vc * _gelu_grad(gc, t)
                acc_v = tuple(acc_v[k] + _fold8(dvc * sv[k]) for k in range(ntap)) + (acc_v[ntap] + _fold8(dvc),)
                acc_g = tuple(acc_g[k] + _fold8(dgc * sg[k]) for k in range(ntap)) + (acc_g[ntap] + _fold8(dgc),)
                du_v = wv[2] * dvc + wv[1] * _shift_up(dvc, head_dv, 1) + wv[0] * _shift_up(dvc, head_dv, 2)
                du_g = wg[2] * dgc + wg[1] * _shift_up(dgc, head_dg, 1) + wg[0] * _shift_up(dgc, head_dg, 2)
                du_ref[0, _strip(r), cols] = _bf(du_v)
                du_ref[1, _strip(r), cols] = _bf(du_g)
                return dvc[0:SUBLANES], dgc[0:SUBLANES], acc_v, acc_g

            init = (headv[:, cols], headg[:, cols], (zero,) * (ntap + 1), (zero,) * (ntap + 1))
            def strips(q, carry):
                for k in range(FFN_UNROLL_BWD):
                    carry = strip(q * FFN_UNROLL_BWD + k, carry)
                return carry

            top_dv, top_dg, acc_v, acc_g = lax.fori_loop(0, nstrip // FFN_UNROLL_BWD, strips, init)
            headv[:, cols] = top_dv
            headg[:, cols] = top_dg
            for k in range(ntap + 1):
                dwv_ref[k:k + 1, cols] += _colsum(acc_v[k])
                dwg_ref[k:k + 1, cols] += _colsum(acc_g[k])

    def blk(shift):
        return pl.BlockSpec((tb, FFN_TN), lambda c, i: (nb - 1 - i, c + shift))

    def halo_blk(shift):
        return pl.BlockSpec((SUBLANES, FFN_TN), lambda c, i: (jnp.maximum((nb - 1 - i) * per - 1, 0), c + shift))

    tapv = pl.BlockSpec((FFN_TAP_ROWS, FFN_TN), lambda c, i: (0, c))
    tapg = pl.BlockSpec((FFN_TAP_ROWS, FFN_TN), lambda c, i: (0, c + nct))
    return pl.pallas_call(
        body, name=name, grid=(nct, nb),
        in_specs=[blk(0), blk(0), blk(nct), halo_blk(0), halo_blk(nct), tapv, tapg,
                  pl.BlockSpec((1, FFN_TN), lambda c, i: (0, c)),
                  pl.BlockSpec((1, FFN_TN), lambda c, i: (0, c + nct))],
        out_specs=[pl.BlockSpec((2, tb, FFN_TN), lambda c, i: (0, nb - 1 - i, c)), tapv, tapv],
        out_shape=[jax.ShapeDtypeStruct((2, s, D_FF), BF16), jax.ShapeDtypeStruct((FFN_TAP_ROWS, D_FF), F32),
                   jax.ShapeDtypeStruct((FFN_TAP_ROWS, D_FF), F32)],
        scratch_shapes=[pltpu.VMEM((SUBLANES, FFN_TN), F32), pltpu.VMEM((SUBLANES, FFN_TN), F32)],
        compiler_params=_cparams(2),
    )(dhf, u, u, u, u, cw, cw, cb, cb)


def _place():
    x, y, c = lax.axis_index("x"), lax.axis_index("y"), lax.axis_index("c")
    return x, y, c, 2 * x + y


def _chip_peer(x, y, d):
    return jnp.bitwise_xor(x, d >> 1), jnp.bitwise_xor(y, d & 1)


def gather_weights(shards):
    n = len(shards)

    def body(*refs):
        ins, outs = refs[:n], refs[n:2 * n]
        lsem, ssem, rsem, fsem, gsem = refs[2 * n:]
        x, y, c, j = _place()
        sib = (x, y, 1 - c)
        started = []
        for w in range(n):
            for l in range(DEPTH):
                cp = pltpu.make_async_copy(ins[w].at[l], outs[w].at[l, j], lsem.at[w, l])
                cp.start()
                started.append(cp)
        ici = {}
        for w in range(n):
            for d in (1, 2, 3):
                px, py = _chip_peer(x, y, d)
                cp = pltpu.make_async_remote_copy(
                    src_ref=ins[w].at[c], dst_ref=outs[w].at[c, j], send_sem=ssem.at[w, d - 1],
                    recv_sem=rsem.at[w, d - 1], device_id=(px, py, c), device_id_type=MESH)
                cp.start()
                ici[w, d] = cp
        fwd = {}
        for d in (1, 2, 3):
            jd = jnp.bitwise_xor(j, d)
            for w in range(n):
                ici[w, d].wait_recv()
                cp = pltpu.make_async_remote_copy(
                    src_ref=outs[w].at[c, jd], dst_ref=outs[w].at[c, jd], send_sem=fsem.at[w, d - 1],
                    recv_sem=gsem.at[w, d - 1], device_id=sib, device_id_type=MESH)
                cp.start()
                fwd[w, d] = cp
        for w in range(n):
            for d in (1, 2, 3):
                fwd[w, d].wait_recv()
                fwd[w, d].wait_send()
                ici[w, d].wait_send()
        for cp in started:
            cp.wait()

    out_shape = [jax.ShapeDtypeStruct((DEPTH, N_CHIPS) + a.shape[1:], a.dtype) for a in shards]
    return pl.pallas_call(
        body, name="gather_weights", in_specs=[ANY] * n, out_specs=[ANY] * n, out_shape=out_shape,
        scratch_shapes=[pltpu.SemaphoreType.DMA((n, DEPTH)), pltpu.SemaphoreType.DMA((n, 3)),
                        pltpu.SemaphoreType.DMA((n, 3)), pltpu.SemaphoreType.DMA((n, 3)),
                        pltpu.SemaphoreType.DMA((n, 3))],
    )(*shards)


def exchange_layers(gbig, small):
    n = len(gbig)

    def body(*refs):
        g_refs, s_ref = refs[:n], refs[n]
        got_refs, all_ref = refs[n + 1:2 * n + 1], refs[2 * n + 1]
        dsem, esem, lsem, ssem, rsem, fsem, hsem = refs[2 * n + 2:]
        x, y, c, j = _place()
        sib = (x, y, 1 - c)
        me = 4 * x + 2 * y + c
        big = []
        for k in range(n):
            cp = pltpu.make_async_remote_copy(src_ref=g_refs[k].at[1 - c], dst_ref=got_refs[k], send_sem=dsem.at[k],
                                              recv_sem=esem.at[k], device_id=sib, device_id_type=MESH)
            cp.start()
            big.append(cp)
        mine = pltpu.make_async_copy(s_ref, all_ref.at[me], lsem)
        mine.start()

        def small_copy(k, block, to, sems, from_input):
            return pltpu.make_async_remote_copy(
                src_ref=s_ref if from_input else all_ref.at[block], dst_ref=all_ref.at[block],
                send_sem=sems[0].at[k], recv_sem=sems[1].at[k], device_id=to, device_id_type=MESH)

        first = [small_copy(0, me, sib, (ssem, rsem), True)]
        for d in (1, 2, 3):
            px, py = _chip_peer(x, y, d)
            first.append(small_copy(d, me, (px, py, c), (ssem, rsem), True))
        for cp in first:
            cp.start()
        passed = []
        for d in (1, 2, 3):
            px, py = _chip_peer(x, y, d)
            src_block = 4 * px + 2 * py + c
            small_copy(d, src_block, sib, (ssem, rsem), False).wait_recv()
            cp = small_copy(d - 1, src_block, sib, (fsem, hsem), False)
            cp.start()
            passed.append(cp)
        small_copy(0, me, sib, (ssem, rsem), False).wait_recv()
        for cp in passed:
            cp.wait_recv()
        for cp in first + passed:
            cp.wait_send()
        mine.wait()
        for cp in big:
            cp.wait()

    r2 = small.shape[0]
    return pl.pallas_call(
        body, name="exchange_layers", in_specs=[ANY] * (n + 1), out_specs=[ANY] * (n + 1),
        out_shape=[jax.ShapeDtypeStruct(g.shape[1:], F32) for g in gbig] + [jax.ShapeDtypeStruct((8, r2, LANES), F32)],
        scratch_shapes=[pltpu.SemaphoreType.DMA((n,)), pltpu.SemaphoreType.DMA((n,)), pltpu.SemaphoreType.DMA,
                        pltpu.SemaphoreType.DMA((4,)), pltpu.SemaphoreType.DMA((4,)),
                        pltpu.SemaphoreType.DMA((3,)), pltpu.SemaphoreType.DMA((3,))],
    )(*gbig, small)


def scatter_shards(s1):
    n = len(s1)

    def body(*refs):
        s_refs, got_refs = refs[:n], refs[n:2 * n]
        ssem, rsem = refs[2 * n:]
        x, y, c, j = _place()
        cps = []
        for d in (1, 2, 3):
            px, py = _chip_peer(x, y, d)
            for k in range(n):
                cp = pltpu.make_async_remote_copy(
                    src_ref=s_refs[k].at[jnp.bitwise_xor(j, d)], dst_ref=got_refs[k].at[d - 1],
                    send_sem=ssem.at[k, d - 1], recv_sem=rsem.at[k, d - 1], device_id=(px, py, c), device_id_type=MESH)
                cp.start()
                cps.append(cp)
        for cp in cps:
            cp.wait()

    return pl.pallas_call(
        body, name="scatter_shards", in_specs=[ANY] * n, out_specs=[ANY] * n,
        out_shape=[jax.ShapeDtypeStruct((3,) + a.shape[1:], a.dtype) for a in s1],
        scratch_shapes=[pltpu.SemaphoreType.DMA((n, 3)), pltpu.SemaphoreType.DMA((n, 3))],
    )(*s1)


def share_with_sibling(parts):
    n = len(parts)

    def body(*refs):
        r_refs, out_refs = refs[:n], refs[n:2 * n]
        lsem, ssem, rsem = refs[2 * n:]
        x, y, c, j = _place()
        cps = []
        for k in range(n):
            loc = pltpu.make_async_copy(r_refs[k], out_refs[k].at[c], lsem.at[k])
            loc.start()
            cp = pltpu.make_async_remote_copy(src_ref=r_refs[k], dst_ref=out_refs[k].at[c], send_sem=ssem.at[k],
                                              recv_sem=rsem.at[k], device_id=(x, y, 1 - c), device_id_type=MESH)
            cp.start()
            cps.append((loc, cp))
        for loc, cp in cps:
            loc.wait()
            cp.wait()

    return pl.pallas_call(
        body, name="share_with_sibling", in_specs=[ANY] * n, out_specs=[ANY] * n,
        out_shape=[jax.ShapeDtypeStruct((DEPTH,) + a.shape, F32) for a in parts],
        scratch_shapes=[pltpu.SemaphoreType.DMA((n,)), pltpu.SemaphoreType.DMA((n,)), pltpu.SemaphoreType.DMA((n,))],
    )(*parts)


def add_layer_halves(g, got, cidx, name):
    _, nch, r, cdim = g.shape
    tr = _row_tile(r, cdim, mult=16)

    def body(c_ref, a_ref, b_ref, o_ref):
        o_ref[...] = _bf(a_ref[...] + b_ref[...])

    grid_spec = pltpu.PrefetchScalarGridSpec(
        num_scalar_prefetch=1, grid=(nch, r // tr),
        in_specs=[pl.BlockSpec((None, None, tr, cdim), lambda jj, i, c_ref: (c_ref[0], jj, i, 0)),
                  pl.BlockSpec((None, tr, cdim), lambda jj, i, c_ref: (jj, i, 0))],
        out_specs=pl.BlockSpec((None, tr, cdim), lambda jj, i, c_ref: (jj, i, 0)))
    return pl.pallas_call(
        body, name=name, grid_spec=grid_spec,
        out_shape=jax.ShapeDtypeStruct((nch, r, cdim), BF16), compiler_params=_cparams(2),
    )(cidx, g, got)


def add_chip_parts(s1, got, jidx, name):
    _, r, cdim = s1.shape
    tr = _row_tile(r, cdim, mult=16)

    def body(j_ref, a_ref, g0_ref, g1_ref, g2_ref, o_ref):
        o_ref[...] = ((a_ref[...].astype(F32) + g0_ref[...].astype(F32)) + g1_ref[...].astype(F32)) + g2_ref[...].astype(F32)

    def slot(k):
        return pl.BlockSpec((None, tr, cdim), lambda i, j_ref: (k, i, 0))

    grid_spec = pltpu.PrefetchScalarGridSpec(
        num_scalar_prefetch=1, grid=(r // tr,),
        in_specs=[pl.BlockSpec((None, tr, cdim), lambda i, j_ref: (j_ref[0], i, 0)), slot(0), slot(1), slot(2)],
        out_specs=pl.BlockSpec((tr, cdim), lambda i, j_ref: (i, 0)))
    return pl.pallas_call(
        body, name=name, grid_spec=grid_spec,
        out_shape=jax.ShapeDtypeStruct((r, cdim), F32), compiler_params=_cparams(1),
    )(jidx, s1, got, got, got)


def sum_devices(allp):
    _, r, _ = allp.shape

    def body(a_ref, o_ref):
        tot = a_ref[0]
        for k in range(1, 8):
            tot = tot + a_ref[k]
        o_ref[...] = tot

    tr = r // 2 if r % 16 == 0 else r
    return pl.pallas_call(
        body, name="sum_devices", grid=(r // tr,),
        in_specs=[pl.BlockSpec((8, tr, LANES), lambda i: (0, i, 0))],
        out_specs=pl.BlockSpec((tr, LANES), lambda i: (i, 0)),
        out_shape=jax.ShapeDtypeStruct((r, LANES), F32), compiler_params=_cparams(1),
    )(allp)


def _row_tile(r, cdim, limit_bytes=1 << 20, mult=8):
    best = None
    for tr in range(mult, r + 1, mult):
        if r % tr == 0 and tr * cdim * 4 <= limit_bytes:
            best = tr
    return best if best is not None else r


def adamw(w, g, m, v, name):
    r, cdim = w.shape
    tr = _row_tile(r, cdim)
    bc1 = 1.0 - ADAM_B1 ** ADAM_STEP
    bc2 = 1.0 - ADAM_B2 ** ADAM_STEP

    def body(w_ref, g_ref, m_ref, v_ref, d_ref, nm_ref, nv_ref):
        gv = g_ref[...]
        nm = ADAM_B1 * m_ref[...] + (1.0 - ADAM_B1) * gv
        nv = ADAM_B2 * v_ref[...] + (1.0 - ADAM_B2) * (gv * gv)
        d_ref[...] = -ADAM_LR * ((nm / bc1) / (jnp.sqrt(nv / bc2) + ADAM_EPS) + ADAM_WD * w_ref[...])
        nm_ref[...] = nm
        nv_ref[...] = nv

    blk = pl.BlockSpec((tr, cdim), lambda i: (i, 0))
    shape = jax.ShapeDtypeStruct((r, cdim), F32)
    return pl.pallas_call(
        body, name=name, grid=(r // tr,), in_specs=[blk] * 4, out_specs=[blk] * 3, out_shape=[shape] * 3,
        compiler_params=_cparams(1),
    )(w, g, m, v)


def _s5_prepare(lam_re, lam_im, log_dt, b_re, b_im, c_re, c_im):
    groups, ch = 16, 16
    dt = jnp.exp(log_dt)[:, None]
    mag = jnp.exp(lam_re * dt)
    a_r, a_i = mag * jnp.cos(lam_im * dt), mag * jnp.sin(lam_im * dt)
    den = lam_re * lam_re + lam_im * lam_im
    q_r = ((a_r - 1.0) * lam_re + a_i * lam_im) / den
    q_i = (a_i * lam_re - (a_r - 1.0) * lam_im) / den
    bb_r = q_r[..., None] * b_re - q_i[..., None] * b_im
    bb_i = q_r[..., None] * b_im + q_i[..., None] * b_re
    eye = jnp.eye(groups, dtype=F32)

    def expand_b(bb):
        return jnp.einsum("gpc,gh->gchp", bb, eye).reshape(groups * ch, N_STATE)

    def expand_c(cc):
        return jnp.einsum("gcp,gh->hpgc", cc, eye).reshape(N_STATE, groups * ch)

    a2 = jnp.concatenate([a_r.reshape(1, N_STATE), a_i.reshape(1, N_STATE)], axis=1)
    bexp = jnp.concatenate([expand_b(bb_r), expand_b(bb_i)], axis=1)
    cexp = jnp.concatenate([expand_c(c_re), -expand_c(c_im)], axis=0)
    return a2, bexp, cexp


def _lru_prepare(w_r, w_i, lam):
    heads = 4
    eye = jnp.eye(heads, dtype=F32)

    def expand(w):
        return jnp.einsum("hij,hk->hikj", w, eye).reshape(D_GROUP, D_GROUP)

    return expand(w_r), expand(w_i), jax.nn.softplus(-lam).reshape(1, D_GROUP)


def _pad_rows(a, rows):
    return jnp.pad(a, ((0, rows - a.shape[0]), (0, 0)))


def _group_mean_matrix():
    gidx = jnp.arange(D_GROUP) // 64
    return (gidx[:, None] == gidx[None, :]).astype(BF16) * jnp.asarray(1.0 / 64.0, BF16)


def _pack_rows(arrs, width):
    parts = []
    for a in arrs:
        flat = a.reshape(-1)
        pad = (-flat.shape[0]) % width
        parts.append(jnp.pad(flat, (0, pad)) if pad else flat)
    flat = jnp.concatenate(parts)
    rows = flat.shape[0] // width
    pad_rows = (-rows) % 16
    if pad_rows:
        flat = jnp.pad(flat, (0, pad_rows * width))
    return flat.reshape(-1, width)


def _unpack_rows(packed, shapes, width):
    flat = packed.reshape(-1)
    out, off = [], 0
    for shp in shapes:
        size = math.prod(shp)
        out.append(flat[off:off + size].reshape(shp))
        off += size + ((-size) % width)
    return out


def kernel(x, mem, ln_in_g, ln_in_b, w_in, b_in, s5_lam_re, s5_lam_im, s5_log_dt, s5_b_re, s5_b_im, s5_c_re, s5_c_im, s5_d, s5_w_glu, s5_b_glu, cv_w, cv_b, cv_gn_g, cv_gn_b, cv_w_pw, cv_b_pw, lru_conv_w, lru_conv_b, lru_w_r, lru_b_r, lru_w_i, lru_b_i, lru_lam, attn_w_kv, w_out, b_out, ln1_g, ln1_b, ffn_w_up, ffn_conv_w, ffn_conv_b, ffn_w_down, ln2_g, ln2_b, loss_target, m_ln_in_g, m_ln_in_b, m_w_in, m_b_in, m_s5_lam_re, m_s5_lam_im, m_s5_log_dt, m_s5_b_re, m_s5_b_im, m_s5_c_re, m_s5_c_im, m_s5_d, m_s5_w_glu, m_s5_b_glu, m_cv_w, m_cv_b, m_cv_gn_g, m_cv_gn_b, m_cv_w_pw, m_cv_b_pw, m_lru_conv_w, m_lru_conv_b, m_lru_w_r, m_lru_b_r, m_lru_w_i, m_lru_b_i, m_lru_lam, m_attn_w_kv, m_w_out, m_b_out, m_ln1_g, m_ln1_b, m_ffn_w_up, m_ffn_conv_w, m_ffn_conv_b, m_ffn_w_down, m_ln2_g, m_ln2_b, v_ln_in_g, v_ln_in_b, v_w_in, v_b_in, v_s5_lam_re, v_s5_lam_im, v_s5_log_dt, v_s5_b_re, v_s5_b_im, v_s5_c_re, v_s5_c_im, v_s5_d, v_s5_w_glu, v_s5_b_glu, v_cv_w, v_cv_b, v_cv_gn_g, v_cv_gn_b, v_cv_w_pw, v_cv_b_pw, v_lru_conv_w, v_lru_conv_b, v_lru_w_r, v_lru_b_r, v_lru_w_i, v_lru_b_i, v_lru_lam, v_attn_w_kv, v_w_out, v_b_out, v_ln1_g, v_ln1_b, v_ffn_w_up, v_ffn_conv_w, v_ffn_conv_b, v_ffn_w_down, v_ln2_g, v_ln2_b):
    p = dict(locals())
    xs = x[0]
    mems = mem[0]
    target = loss_target[0]
    s = xs.shape[0]
    cidx = lax.axis_index("c")
    jidx = 2 * lax.axis_index("x") + lax.axis_index("y")
    tb_scan = min(256, s)
    tb_attn = min(512, s)
    tb_ffn = min(256, s)

    small_sh_names = list(SMALL_SHARDED)
    small_sh_shapes = [p[nm].shape[1:] for nm in small_sh_names]
    small_pack = jnp.stack([_pack_rows([p[nm][l] for nm in small_sh_names], LANES) for l in range(DEPTH)])
    gathered = gather_weights([_bf(p[nm]) for nm in BIG] + [small_pack])
    g_w_in, g_w_kv, g_w_out, g_w_up, g_w_down, g_small = gathered
    g_w_kv = g_w_kv.reshape(DEPTH, 1, D_MODEL, 2 * D_GROUP)
    g_w_out = g_w_out.reshape(DEPTH, 1, D_MODEL, D_MODEL)
    g_w_down = g_w_down.reshape(DEPTH, 1, D_FF, D_MODEL)
    full_small = {nm: [] for nm in small_sh_names}
    for l in range(DEPTH):
        per_chip = [_unpack_rows(g_small[l, jj], small_sh_shapes, LANES) for jj in range(N_CHIPS)]
        for k, nm in enumerate(small_sh_names):
            full_small[nm].append(jnp.concatenate([per_chip[jj][k] for jj in range(N_CHIPS)],
                                                  axis=SMALL_SHARDED[nm] - 1))
    w_glu_bf = _bf(jnp.stack(full_small['s5_w_glu']))
    w_pw_bf = _bf(jnp.stack(full_small['cv_w_pw']))
    pmat = _group_mean_matrix()

    def vec(a):
        return a.reshape(1, -1)

    xh0, rs0, xb0 = ln_fwd(xs, vec(ln_in_g), vec(ln_in_b), "ln_in")
    saved = []
    prev = dict(xh=xh0, rs=rs0, xb=xb0, g=vec(ln_in_g), b=vec(ln_in_b))
    for l in range(DEPTH):
        sv = dict(prev=prev)
        (a2, bexp, cexp), sv['s5_vjp'] = jax.vjp(_s5_prepare, s5_lam_re[l], s5_lam_im[l], s5_log_dt[l],
                                                 s5_b_re[l], s5_b_im[l], s5_c_re[l], s5_c_im[l])
        (wr, wi, sp), sv['lru_vjp'] = jax.vjp(_lru_prepare, lru_w_r[l], lru_w_i[l], lru_lam[l])
        sv.update(a2=a2, bexp=_bf(bexp), cexp=_bf(cexp), wr=_bf(wr), wi=_bf(wi), sp=sp)
        sv['cvw'] = _pad_rows(full_small['cv_w'][l], CV_HALO)
        sv['lcw'] = _pad_rows(full_small['lru_conv_w'][l], LRU_HALO)
        sv['fcw'] = _pad_rows(full_small['ffn_conv_w'][l], FFN_TAP_ROWS)
        h_in = mm_nn(prev['xb'], g_w_in, l, vec(b_in[l]), F32, 512, f"in_proj{l}")
        kv = mm_nn(mems, g_w_kv, l, jnp.zeros((1, 2 * D_GROUP), F32), F32, 256, f"kv_proj{l}")
        y_s5, hst, y0 = s5_fwd(h_in, a2, sv['bexp'], sv['cexp'], vec(s5_d[l]), w_glu_bf, l, vec(s5_b_glu[l]),
                               tb_scan, f"s5_fwd{l}")
        y_cv, hc = cv_fwd(h_in, sv['cvw'], vec(cv_b[l]), vec(cv_gn_g[l]), vec(cv_gn_b[l]), pmat, w_pw_bf, l,
                          vec(cv_b_pw[l]), tb_scan, f"cv_fwd{l}")
        y_lru, xcs, hls = lru_fwd(h_in, sv['lcw'], vec(lru_conv_b[l]), sv['wr'], vec(lru_b_r[l]), sv['wi'],
                                  vec(lru_b_i[l]), sp, tb_scan, f"lru_fwd{l}")
        y_mem = attn_fwd(h_in, kv, tb_attn, f"attn_fwd{l}")
        mix_in = jnp.concatenate([y_s5, y_cv, y_lru, y_mem], axis=1)
        xh1, rs1, xb1 = proj_ln(mix_in, g_w_out.reshape(DEPTH, D_MODEL, D_MODEL), l, vec(b_out[l]),
                                prev['xh'], prev['g'], prev['b'], vec(ln1_g[l]), vec(ln1_b[l]), f"out_proj_ln{l}")
        u = mm_nn(xb1, g_w_up, l, jnp.zeros((1, 2 * D_FF), F32), F32, 1024, f"ffn_up{l}")
        hf = ffn_act_fwd(u, sv['fcw'], vec(ffn_conv_b[l]), tb_ffn, f"ffn_act{l}")
        xh2, rs2, xb2 = proj_ln(hf, g_w_down.reshape(DEPTH, D_FF, D_MODEL), l, jnp.zeros((1, D_MODEL), F32),
                                xh1, vec(ln1_g[l]), vec(ln1_b[l]), vec(ln2_g[l]), vec(ln2_b[l]), f"ffn_down_ln{l}")
        sv.update(h_in=h_in, kv=kv, hst=hst, y0=y0, hc=hc, xcs=xcs, hls=hls, mix_in=mix_in,
                  xh1=xh1, rs1=rs1, xb1=xb1, u=u, hf=hf, xh2=xh2, rs2=rs2)
        saved.append(sv)
        prev = dict(xh=xh2, rs=rs2, xb=xb2, g=vec(ln2_g[l]), b=vec(ln2_b[l]))

    grads = {}
    per_layer = {nm: [None] * DEPTH for nm in WEIGHTS if nm not in ('ln_in_g', 'ln_in_b')}
    gbig = [None] * len(BIG)
    dx = None
    for l in reversed(range(DEPTH)):
        sv = saved[l]
        pv = sv['prev']
        if l == DEPTH - 1:
            dr2, dg2, db2, sqerr = loss_ln_bwd(target, sv['xh2'], sv['rs2'], vec(ln2_g[l]), vec(ln2_b[l]), "loss_ln2_bwd")
            loss_local = 0.5 / D_MODEL * jnp.sum(sqerr)
        else:
            dr2, dg2, db2, _ = ln_bwd(dx, sv['xh2'], sv['rs2'], vec(ln2_g[l]), f"ln2_bwd{l}")
        per_layer['ln2_g'][l], per_layer['ln2_b'][l] = dg2[0], db2[0]
        tm_nt = min(512, s)
        ts_tn = min(512, s)
        dhf = mm_nt(dr2, (tm_nt, D_MODEL), lambda i, rt, j: (i, 0), g_w_down, l, FFN_TN, None, F32, 512, s, f"ffn_down_dx{l}")
        gbig[4] = mm_tn(sv['hf'], dr2, (ts_tn, D_MODEL), lambda kt, j, st: (st, 0), 1, D_MODEL, FFN_TN, 512, s,
                        l, gbig[4], f"ffn_down_dw{l}")
        du, dcwv, dcwg = ffn_act_bwd(dhf, sv['u'], sv['fcw'], vec(ffn_conv_b[l]), tb_ffn, f"ffn_act_bwd{l}")
        dcw = jnp.concatenate([dcwv, dcwg], axis=1)
        per_layer['ffn_conv_w'][l] = dcw[0:FFN_CONV_WIDTH]
        per_layer['ffn_conv_b'][l] = dcw[FFN_CONV_WIDTH]
        dx1 = mm_nt(du, (None, tm_nt, FFN_TN), lambda i, rt, j: (j // 2, i, j % 2), g_w_up, l, D_MODEL, dr2, F32,
                    512, s, f"ffn_up_dx{l}")
        gbig[3] = mm_tn(sv['xb1'], du, (None, ts_tn, FFN_TN), lambda kt, j, st: (j // 2, st, j % 2), N_CHIPS, FFN_TN,
                        D_MODEL, 512, s, l, gbig[3], f"ffn_up_dw{l}")
        dr1, dg1, db1, cs1 = ln_bwd(dx1, sv['xh1'], sv['rs1'], vec(ln1_g[l]), f"ln1_bwd{l}")
        per_layer['ln1_g'][l], per_layer['ln1_b'][l], per_layer['b_out'][l] = dg1[0], db1[0], cs1[0]
        dmix = mm_nt(dr1, (tm_nt, D_MODEL), lambda i, rt, j: (i, 0), g_w_out, l, D_MODEL, None, F32, 512, s, f"out_proj_dx{l}")
        gbig[2] = mm_tn(sv['mix_in'], dr1, (ts_tn, D_MODEL), lambda kt, j, st: (st, 0), 1, D_MODEL, D_MODEL, 512, s,
                        l, gbig[2], f"out_proj_dw{l}")
        h_in = sv['h_in']
        (d_u, cs_u, d_bexp, d_cexp, d_dd, d_wglu, d_bglu, d_a2) = s5_bwd(
            dmix, h_in, sv['y0'], sv['hst'], sv['a2'], sv['bexp'], sv['cexp'], vec(s5_d[l]), w_glu_bf, l,
            vec(s5_b_glu[l]), tb_scan, f"s5_bwd{l}")
        (d_vg, cs_vg, d_cvw, d_cvb, d_gg, d_gb, d_wpw, d_bpw) = cv_bwd(
            dmix, h_in, sv['hc'], sv['cvw'], vec(cv_gn_g[l]), vec(cv_gn_b[l]), pmat, w_pw_bf, l, tb_scan, f"cv_bwd{l}")
        (d_lx, cs_lx, d_lcw, d_lcb, d_wr, d_br, d_wi, d_bi, d_sp) = lru_bwd(
            dmix, h_in, sv['xcs'], sv['hls'], sv['lcw'], sv['wr'], vec(lru_b_r[l]), sv['wi'], vec(lru_b_i[l]),
            sv['sp'], tb_scan, f"lru_bwd{l}")
        d_q, cs_q, d_kv = attn_bwd(dmix, h_in, sv['kv'], tb_attn, f"attn_bwd{l}")
        g_s5 = sv['s5_vjp']((d_a2, d_bexp, d_cexp))
        for nm, gval in zip(['s5_lam_re', 's5_lam_im', 's5_log_dt', 's5_b_re', 's5_b_im', 's5_c_re', 's5_c_im'], g_s5):
            per_layer[nm][l] = gval
        g_lru = sv['lru_vjp']((d_wr, d_wi, d_sp))
        for nm, gval in zip(['lru_w_r', 'lru_w_i', 'lru_lam'], g_lru):
            per_layer[nm][l] = gval
        per_layer['s5_d'][l], per_layer['s5_w_glu'][l], per_layer['s5_b_glu'][l] = d_dd[0], d_wglu, d_bglu[0]
        per_layer['cv_w'][l], per_layer['cv_b'][l] = d_cvw[0:CONV_WIDTH], d_cvb[0]
        per_layer['cv_gn_g'][l], per_layer['cv_gn_b'][l] = d_gg[0], d_gb[0]
        per_layer['cv_w_pw'][l], per_layer['cv_b_pw'][l] = d_wpw, d_bpw[0]
        per_layer['lru_conv_w'][l], per_layer['lru_conv_b'][l] = d_lcw[0:LRU_CONV_WIDTH], d_lcb[0]
        per_layer['lru_b_r'][l], per_layer['lru_b_i'][l] = d_br[0], d_bi[0]
        per_layer['b_in'][l] = jnp.concatenate([cs_u, cs_vg, cs_lx, cs_q], axis=1)[0]
        gbig[1] = mm_tn(mems, d_kv, (MEM_ROWS, 2 * D_GROUP), lambda kt, j, st: (st, 0), 1, 2 * D_GROUP, D_MODEL, MEM_ROWS,
                        MEM_ROWS, l, gbig[1], f"kv_proj_dw{l}")
        dh_in = jnp.concatenate([d_u, d_vg, d_lx, d_q], axis=1)
        n_sh = N_IN // N_CHIPS
        dxp = mm_nt(dh_in, (tm_nt, n_sh), lambda i, rt, j: (i, j), g_w_in, l, D_MODEL, dr1, F32, 512, s, f"in_proj_dx{l}")
        gbig[0] = mm_tn(pv['xb'], dh_in, (ts_tn, n_sh), lambda kt, j, st: (st, j), N_CHIPS, n_sh, D_MODEL, 512, s,
                        l, gbig[0], f"in_proj_dw{l}")
        dx = dxp
    grad_x, dg_in, db_in, _ = ln_bwd(dx, xh0, rs0, vec(ln_in_g), "ln_in_bwd")
    grads['ln_in_g'], grads['ln_in_b'] = dg_in[0], db_in[0]
    for nm, vals in per_layer.items():
        if nm not in BIG:
            grads[nm] = jnp.stack(vals)

    small_names = [nm for nm in WEIGHTS if nm not in BIG]
    small_local = _pack_rows([grads[nm] for nm in small_names], LANES)
    gbig = [g.reshape((DEPTH, N_CHIPS) + p[nm].shape[1:]) for g, nm in zip(gbig, BIG)]
    c1 = cidx.reshape(1).astype(jnp.int32)
    j1 = jidx.reshape(1).astype(jnp.int32)
    *got_a, small_all = exchange_layers(gbig, small_local)
    s1 = [add_layer_halves(g, ga, c1, f"add_cores_{nm}") for g, ga, nm in zip(gbig, got_a, BIG)]
    got_b = scatter_shards(s1)
    mine_r = [add_chip_parts(sk, gb, j1, f"add_chips_{nm}") for sk, gb, nm in zip(s1, got_b, BIG)]
    red_big = share_with_sibling(mine_r)
    small_red = sum_devices(small_all)
    small_grads = dict(zip(small_names, _unpack_rows(small_red, [grads[nm].shape for nm in small_names], LANES)))

    out_g, out_d, out_m, out_v = {}, {}, {}, {}
    for k, nm in enumerate(BIG):
        gk = red_big[k]
        two_d = (-1, p[nm].shape[-1])
        dlt, nm_, nv_ = adamw(p[nm].reshape(two_d), gk.reshape(two_d), p['m_' + nm].reshape(two_d),
                              p['v_' + nm].reshape(two_d), f"adamw_{nm}")
        out_g[nm] = gk
        out_d[nm], out_m[nm], out_v[nm] = (t.reshape(p[nm].shape) for t in (dlt, nm_, nv_))
    own = {}
    for nm in small_names:
        gfull = small_grads[nm]
        if nm in SMALL_SHARDED:
            ax = SMALL_SHARDED[nm]
            width = p[nm].shape[ax]
            gfull = lax.dynamic_slice_in_dim(gfull, jidx * width, width, axis=ax)
        own[nm] = gfull
    packs = [_pack_rows([src[nm] for nm in small_names], LANES)
             for src in (dict((nm, p[nm]) for nm in small_names), own,
                         dict((nm, p['m_' + nm]) for nm in small_names), dict((nm, p['v_' + nm]) for nm in small_names))]
    dlt, nm_, nv_ = adamw(packs[0], packs[1], packs[2], packs[3], "adamw_small")
    shapes = [p[nm].shape for nm in small_names]
    for dst, packed in ((out_d, dlt), (out_m, nm_), (out_v, nv_)):
        dst.update(zip(small_names, _unpack_rows(packed, shapes, LANES)))
    out_g.update(own)

    loss = lax.psum(loss_local, ("x", "y", "c"))
    return (loss, grad_x[None], *[out_g[nm] for nm in WEIGHTS], *[out_d[nm] for nm in WEIGHTS],
            *[out_m[nm] for nm in WEIGHTS], *[out_v[nm] for nm in WEIGHTS])
```

```python
import functools
import math

import jax
import jax.numpy as jnp
from jax import lax
from jax.experimental import pallas as pl
from jax.experimental.pallas import tpu as pltpu

F32 = jnp.float32
BF16 = jnp.bfloat16
MESH = pl.DeviceIdType.MESH
ANY = pl.BlockSpec(memory_space=pl.ANY)

DEPTH = 2
D_MODEL = 1024
D_GROUP = 256
N_IN = 6 * D_GROUP
D_FF = 2816
N_STATE = 1024
CONV_WIDTH = 31
LRU_CONV_WIDTH = 4
FFN_CONV_WIDTH = 3
LRU_C = 8.0
ALPHA = (2 * DEPTH) ** 0.25
LN_EPS = 1e-5
N_CHIPS = 4
MEM_ROWS = 256
LANES = 128
SUBLANES = 8
VMEM_LIMIT = 56 * 1024 * 1024

ADAM_LR, ADAM_B1, ADAM_B2, ADAM_EPS, ADAM_WD, ADAM_STEP = 0.001, 0.9, 0.999, 1e-08, 0.01, 10

WEIGHTS = ['ln_in_g', 'ln_in_b', 'w_in', 'b_in', 's5_lam_re', 's5_lam_im', 's5_log_dt', 's5_b_re', 's5_b_im',
           's5_c_re', 's5_c_im', 's5_d', 's5_w_glu', 's5_b_glu', 'cv_w', 'cv_b', 'cv_gn_g', 'cv_gn_b', 'cv_w_pw',
           'cv_b_pw', 'lru_conv_w', 'lru_conv_b', 'lru_w_r', 'lru_b_r', 'lru_w_i', 'lru_b_i', 'lru_lam',
           'attn_w_kv', 'w_out', 'b_out', 'ln1_g', 'ln1_b', 'ffn_w_up', 'ffn_conv_w', 'ffn_conv_b', 'ffn_w_down',
           'ln2_g', 'ln2_b']
BIG = ['w_in', 'attn_w_kv', 'w_out', 'ffn_w_up', 'ffn_w_down']
SMALL_SHARDED = {'s5_w_glu': 1, 'cv_w': 2, 'cv_w_pw': 1, 'lru_conv_w': 2, 'ffn_conv_w': 2}


def _cparams(n_axes):
    return pltpu.CompilerParams(dimension_semantics=("arbitrary",) * n_axes, vmem_limit_bytes=VMEM_LIMIT)


def _dot(a, b):
    return jnp.dot(a, b, preferred_element_type=F32)


def _dot_nt(a, b):
    return lax.dot_general(a, b, (((1,), (1,)), ((), ())), preferred_element_type=F32)


def _dot_tn(a, b):
    return lax.dot_general(a, b, (((0,), (0,)), ((), ())), preferred_element_type=F32)


def _bf(v):
    return v.astype(BF16)


def _colsum(v):
    return jnp.sum(v, axis=0, keepdims=True)


def _dot3(v, p):
    hi = _bf(v)
    r1 = v - hi.astype(F32)
    mid = _bf(r1)
    lo = _bf(r1 - mid.astype(F32))
    return _dot(hi, p) + _dot(mid, p) + _dot(lo, p)


_GELU_C = math.sqrt(2.0 / math.pi)


_GELU_C3 = _GELU_C * 0.044715


def _gelu_parts(v):
    t = jnp.tanh(v * (_GELU_C + _GELU_C3 * (v * v)))
    hv = 0.5 * v
    return hv + hv * t, t


def _gelu(v):
    return _gelu_parts(v)[0]


def _gelu_grad(v, t):
    return (0.5 + 0.5 * t) + (0.5 * v) * (1.0 - t * t) * (_GELU_C + (3.0 * _GELU_C3) * (v * v))


def _sigmoid(v):
    return 1.0 / (1.0 + jnp.exp(-v))


def _acc(ref, val, first):
    @pl.when(first)
    def _():
        ref[...] = val

    @pl.when(jnp.logical_not(first))
    def _():
        ref[...] += val


def _rows(shape):
    return lax.broadcasted_iota(jnp.int32, shape, 0)


def _ln_rows(r):
    mu = jnp.mean(r, -1, keepdims=True)
    rc = r - mu
    var = jnp.mean(rc * rc, -1, keepdims=True)
    rs = lax.rsqrt(var + LN_EPS)
    return rc * rs, rs


def ln_fwd(x, g, b, name):
    s = x.shape[0]
    tm = min(512, s)

    def body(x_ref, g_ref, b_ref, xh_ref, rs_ref, xb_ref):
        xh, rs = _ln_rows(x_ref[...])
        xh_ref[...] = xh
        rs_ref[...] = rs
        xb_ref[...] = _bf(xh * g_ref[...] + b_ref[...])

    row = pl.BlockSpec((tm, D_MODEL), lambda i: (i, 0))
    vec = pl.BlockSpec((1, D_MODEL), lambda i: (0, 0))
    return pl.pallas_call(
        body, name=name, grid=(s // tm,),
        in_specs=[row, vec, vec],
        out_specs=[row, pl.BlockSpec((tm, 1), lambda i: (i, 0)), row],
        out_shape=[jax.ShapeDtypeStruct((s, D_MODEL), F32), jax.ShapeDtypeStruct((s, 1), F32),
                   jax.ShapeDtypeStruct((s, D_MODEL), BF16)],
        compiler_params=_cparams(1),
    )(x, g, b)


def proj_ln(a, w, layer, bias, xh_prev, g_prev, b_prev, g, b, name):
    s, k = a.shape
    tm = min(512, s)

    def body(a_ref, w_ref, bias_ref, xp_ref, gp_ref, bp_ref, g_ref, b_ref, xh_ref, rs_ref, xb_ref):
        acc = _dot(a_ref[...], w_ref[...]) + bias_ref[...]
        r = ALPHA * (xp_ref[...] * gp_ref[...] + bp_ref[...]) + acc
        xh, rs = _ln_rows(r)
        xh_ref[...] = xh
        rs_ref[...] = rs
        xb_ref[...] = _bf(xh * g_ref[...] + b_ref[...])

    row = pl.BlockSpec((tm, D_MODEL), lambda i: (i, 0))
    vec = pl.BlockSpec((1, D_MODEL), lambda i: (0, 0))
    return pl.pallas_call(
        body, name=name, grid=(s // tm,),
        in_specs=[pl.BlockSpec((tm, k), lambda i: (i, 0)),
                  pl.BlockSpec((None, k, D_MODEL), lambda i: (layer, 0, 0)),
                  vec, row, vec, vec, vec, vec],
        out_specs=[row, pl.BlockSpec((tm, 1), lambda i: (i, 0)), row],
        out_shape=[jax.ShapeDtypeStruct((s, D_MODEL), F32), jax.ShapeDtypeStruct((s, 1), F32),
                   jax.ShapeDtypeStruct((s, D_MODEL), BF16)],
        compiler_params=_cparams(1),
    )(a, w, bias, xh_prev, g_prev, b_prev, g, b)


def ln_bwd(dy, xh, rs, g, name):
    s = xh.shape[0]
    tm = min(512, s)

    def body(dy_ref, xh_ref, rs_ref, g_ref, dr_ref, dg_ref, db_ref, cs_ref):
        first = pl.program_id(0) == 0
        dyv = dy_ref[...]
        xhv = xh_ref[...]
        dxh = dyv * g_ref[...]
        dr = rs_ref[...] * (dxh - jnp.mean(dxh, -1, keepdims=True) - xhv * jnp.mean(dxh * xhv, -1, keepdims=True))
        dr_ref[...] = dr
        _acc(dg_ref, _colsum(dyv * xhv), first)
        _acc(db_ref, _colsum(dyv), first)
        _acc(cs_ref, _colsum(dr), first)

    row = pl.BlockSpec((tm, D_MODEL), lambda i: (i, 0))
    vec = pl.BlockSpec((1, D_MODEL), lambda i: (0, 0))
    vshape = jax.ShapeDtypeStruct((1, D_MODEL), F32)
    return pl.pallas_call(
        body, name=name, grid=(s // tm,),
        in_specs=[row, row, pl.BlockSpec((tm, 1), lambda i: (i, 0)), vec],
        out_specs=[row, vec, vec, vec],
        out_shape=[jax.ShapeDtypeStruct((s, D_MODEL), F32), vshape, vshape, vshape],
        compiler_params=_cparams(1),
    )(dy, xh, rs, g)


def loss_ln_bwd(target, xh, rs, g, b, name):
    s = xh.shape[0]
    tm = min(512, s)

    def body(t_ref, xh_ref, rs_ref, g_ref, b_ref, dr_ref, dg_ref, db_ref, sq_ref):
        first = pl.program_id(0) == 0
        xhv = xh_ref[...]
        err = xhv * g_ref[...] + b_ref[...] - t_ref[...]
        dyv = err * (1.0 / D_MODEL)
        dxh = dyv * g_ref[...]
        dr = rs_ref[...] * (dxh - jnp.mean(dxh, -1, keepdims=True) - xhv * jnp.mean(dxh * xhv, -1, keepdims=True))
        dr_ref[...] = dr
        _acc(dg_ref, _colsum(dyv * xhv), first)
        _acc(db_ref, _colsum(dyv), first)
        _acc(sq_ref, _colsum(err * err), first)

    row = pl.BlockSpec((tm, D_MODEL), lambda i: (i, 0))
    vec = pl.BlockSpec((1, D_MODEL), lambda i: (0, 0))
    vshape = jax.ShapeDtypeStruct((1, D_MODEL), F32)
    return pl.pallas_call(
        body, name=name, grid=(s // tm,),
        in_specs=[row, row, pl.BlockSpec((tm, 1), lambda i: (i, 0)), vec, vec],
        out_specs=[row, vec, vec, vec],
        out_shape=[jax.ShapeDtypeStruct((s, D_MODEL), F32), vshape, vshape, vshape],
        compiler_params=_cparams(1),
    )(target, xh, rs, g, b)


def mm_nn(a, w, layer, bias, out_dtype, tm, name):
    m, k = a.shape
    _, nj, _, n = w.shape
    tm = min(tm, m)

    def body(a_ref, w_ref, b_ref, o_ref):
        o_ref[...] = (_dot(_bf(a_ref[...]), w_ref[...]) + b_ref[...]).astype(out_dtype)

    return pl.pallas_call(
        body, name=name, grid=(m // tm, nj),
        in_specs=[pl.BlockSpec((tm, k), lambda i, j: (i, 0)),
                  pl.BlockSpec((None, None, k, n), lambda i, j: (layer, j, 0, 0)),
                  pl.BlockSpec((1, n), lambda i, j: (0, j))],
        out_specs=pl.BlockSpec((tm, n), lambda i, j: (i, j)),
        out_shape=jax.ShapeDtypeStruct((m, nj * n), out_dtype),
        compiler_params=_cparams(2),
    )(a, w, bias)


def mm_nt(a, a_block, a_map, w, layer, tr, add, out_dtype, tm, m, name):
    _, nj, r, n = w.shape
    tm = min(tm, m)
    has_add = add is not None

    def body(*refs):
        if has_add:
            a_ref, w_ref, add_ref, o_ref, acc_ref = refs
        else:
            a_ref, w_ref, o_ref, acc_ref = refs
        j = pl.program_id(2)
        part = _dot_nt(_bf(a_ref[...]), w_ref[...])

        @pl.when(j == 0)
        def _():
            acc_ref[...] = part

        @pl.when(j > 0)
        def _():
            acc_ref[...] += part

        @pl.when(j == nj - 1)
        def _():
            res = acc_ref[...]
            if has_add:
                res = res + ALPHA * add_ref[...]
            o_ref[...] = res.astype(out_dtype)

    in_specs = [pl.BlockSpec(a_block, a_map),
                pl.BlockSpec((None, None, tr, n), lambda i, rt, j: (layer, j, rt, 0))]
    ops = [a, w]
    if has_add:
        in_specs.append(pl.BlockSpec((tm, tr), lambda i, rt, j: (i, rt)))
        ops.append(add)
    return pl.pallas_call(
        body, name=name, grid=(m // tm, r // tr, nj),
        in_specs=in_specs,
        out_specs=pl.BlockSpec((tm, tr), lambda i, rt, j: (i, rt)),
        out_shape=jax.ShapeDtypeStruct((m, r), out_dtype),
        scratch_shapes=[pltpu.VMEM((tm, tr), F32)],
        compiler_params=_cparams(3),
    )(*ops)


def mm_tn(a, b, b_block, b_map, nj, n, tk, ts, s, layer, into, name):
    kx = a.shape[1]
    ts = min(ts, s)

    def body(a_ref, b_ref, *rest):
        o_ref = rest[-1]
        part = _dot_tn(_bf(a_ref[...]), _bf(b_ref[...]))
        _acc(o_ref, part, pl.program_id(2) == 0)

    in_specs = [pl.BlockSpec((ts, tk), lambda kt, j, st: (st, kt)), pl.BlockSpec(b_block, b_map)]
    ops = [a, b]
    aliases = {}
    if into is not None:
        in_specs.append(ANY)
        ops.append(into)
        aliases = {2: 0}
    return pl.pallas_call(
        body, name=name, grid=(kx // tk, nj, s // ts),
        in_specs=in_specs,
        out_specs=pl.BlockSpec((None, None, tk, n), lambda kt, j, st: (layer, j, kt, 0)),
        out_shape=jax.ShapeDtypeStruct((DEPTH, nj, kx, n), F32),
        input_output_aliases=aliases,
        compiler_params=_cparams(3),
    )(*ops)


def _cscan(br, bi, ar, ai, tb, reverse):
    row = _rows(br.shape)
    pr, pi = ar, ai
    sh = 1
    while sh < tb:
        if reverse:
            amt, mask = tb - sh, row < tb - sh
        else:
            amt, mask = sh, row >= sh
        sr = jnp.where(mask, pltpu.roll(br, amt, 0), 0.0)
        si = jnp.where(mask, pltpu.roll(bi, amt, 0), 0.0)
        br, bi = br + pr * sr - pi * si, bi + pr * si + pi * sr
        pr, pi = pr * pr - pi * pi, 2.0 * pr * pi
        sh *= 2
    return br, bi


def _power_table(ar, ai, tb, reverse):
    row = _rows((tb, N_STATE))
    seed_row = tb - 1 if reverse else 0
    er = jnp.where(row == seed_row, ar, 0.0)
    ei = jnp.where(row == seed_row, ai, 0.0)
    return _cscan(er, ei, ar, ai, tb, reverse)


def s5_fwd(h_in, a2, bexp, cexp, dskip, wglu, layer, bglu, tb, name):
    s = h_in.shape[0]
    n = N_STATE

    def body(u_ref, a_ref, b_ref, c_ref, d_ref, w_ref, bg_ref, y_ref, h_ref, y0_ref, carry, ptab):
        ar, ai = a_ref[0:1, 0:n], a_ref[0:1, n:2 * n]

        @pl.when(pl.program_id(0) == 0)
        def _():
            carry[...] = jnp.zeros_like(carry)
            pr0, pi0 = _power_table(ar, ai, tb, False)
            ptab[:, 0:n] = pr0
            ptab[:, n:2 * n] = pi0

        u = u_ref[...]
        bu = _dot(_bf(u), b_ref[...])
        lr, li = _cscan(bu[:, 0:n], bu[:, n:2 * n], ar, ai, tb, False)
        cr, ci = carry[7:8, 0:n], carry[7:8, n:2 * n]
        pr, pi = ptab[:, 0:n], ptab[:, n:2 * n]
        hr = lr + pr * cr - pi * ci
        hi = li + pr * ci + pi * cr
        h_ref[:, 0:n] = hr
        h_ref[:, n:2 * n] = hi
        carry[:, 0:n] = hr[tb - 8:tb]
        carry[:, n:2 * n] = hi[tb - 8:tb]
        y0 = _dot(_bf(hr), c_ref[0:n, :]) + _dot(_bf(hi), c_ref[n:2 * n, :]) + d_ref[...] * u
        y0_ref[...] = y0
        yg = _gelu(y0)
        z = _dot(_bf(yg), w_ref[...]) + bg_ref[...]
        y_ref[...] = _bf(yg * _sigmoid(z))

    vec = pl.BlockSpec((1, D_GROUP), lambda i: (0, 0))
    return pl.pallas_call(
        body, name=name, grid=(s // tb,),
        in_specs=[pl.BlockSpec((tb, D_GROUP), lambda i: (i, 0)),
                  pl.BlockSpec((1, 2 * n), lambda i: (0, 0)),
                  pl.BlockSpec((D_GROUP, 2 * n), lambda i: (0, 0)),
                  pl.BlockSpec((2 * n, D_GROUP), lambda i: (0, 0)),
                  vec,
                  pl.BlockSpec((None, D_GROUP, D_GROUP), lambda i: (layer, 0, 0)),
                  vec],
        out_specs=[pl.BlockSpec((tb, D_GROUP), lambda i: (i, 0)),
                   pl.BlockSpec((tb, 2 * n), lambda i: (i, 0)),
                   pl.BlockSpec((tb, D_GROUP), lambda i: (i, 0))],
        out_shape=[jax.ShapeDtypeStruct((s, D_GROUP), BF16), jax.ShapeDtypeStruct((s, 2 * n), F32),
                   jax.ShapeDtypeStruct((s, D_GROUP), F32)],
        scratch_shapes=[pltpu.VMEM((8, 2 * n), F32), pltpu.VMEM((tb, 2 * n), F32)],
        compiler_params=_cparams(1),
    )(h_in, a2, bexp, cexp, dskip, wglu, bglu)


def s5_bwd(dmix, h_in, y0, hst, a2, bexp, cexp, dskip, wglu, layer, bglu, tb, name):
    s = h_in.shape[0]
    n = N_STATE
    nb = s // tb
    halo = tb // 8

    def body(dy_ref, u_ref, y0_ref, h_ref, hp_ref, a_ref, b_ref, c_ref, d_ref, w_ref, bg_ref,
             du_ref, cs_ref, db_ref, dc_ref, dd_ref, dw_ref, dbg_ref, da_ref, carry, qtab):
        i = pl.program_id(0)
        first = i == 0
        ar, ai = a_ref[0:1, 0:n], -a_ref[0:1, n:2 * n]

        @pl.when(first)
        def _():
            carry[...] = jnp.zeros_like(carry)
            qr0, qi0 = _power_table(ar, ai, tb, True)
            qtab[:, 0:n] = qr0
            qtab[:, n:2 * n] = qi0

        dy = dy_ref[...]
        u = u_ref[...]
        y0v = y0_ref[...]
        yg, t = _gelu_parts(y0v)
        z = _dot(_bf(yg), w_ref[...]) + bg_ref[...]
        sg = _sigmoid(z)
        dz = dy * yg * sg * (1.0 - sg)
        dyg = dy * sg + _dot_nt(_bf(dz), w_ref[...])
        _acc(dw_ref, _dot_tn(_bf(yg), _bf(dz)), first)
        _acc(dbg_ref, _colsum(dz), first)
        dy0 = dyg * _gelu_grad(y0v, t)
        _acc(dd_ref, _colsum(dy0 * u), first)
        hr, hi = h_ref[:, 0:n], h_ref[:, n:2 * n]
        dy0b = _bf(dy0)
        _acc(dc_ref.at[0:n, :], _dot_tn(_bf(hr), dy0b), first)
        _acc(dc_ref.at[n:2 * n, :], _dot_tn(_bf(hi), dy0b), first)
        g = _dot_nt(dy0b, c_ref[...])
        lr, li = _cscan(g[:, 0:n], g[:, n:2 * n], ar, ai, tb, True)
        cr, ci = carry[0:1, 0:n], carry[0:1, n:2 * n]
        qr, qi = qtab[:, 0:n], qtab[:, n:2 * n]
        gr = lr + qr * cr - qi * ci
        gi = li + qr * ci + qi * cr
        carry[:, 0:n] = gr[0:8]
        carry[:, n:2 * n] = gi[0:8]
        keep = jnp.where(i == nb - 1, 0.0, 1.0)
        row = _rows((tb, n))
        pr = jnp.where(row == 0, hp_ref[7:8, 0:n] * keep, pltpu.roll(hr, 1, 0))
        pi = jnp.where(row == 0, hp_ref[7:8, n:2 * n] * keep, pltpu.roll(hi, 1, 0))
        _acc(da_ref.at[:, 0:n], _colsum(gr * pr + gi * pi), first)
        _acc(da_ref.at[:, n:2 * n], _colsum(gi * pr - gr * pi), first)
        grb, gib = _bf(gr), _bf(gi)
        du = d_ref[...] * dy0 + _dot_nt(grb, b_ref[:, 0:n]) + _dot_nt(gib, b_ref[:, n:2 * n])
        ub = _bf(u)
        _acc(db_ref.at[:, 0:n], _dot_tn(ub, grb), first)
        _acc(db_ref.at[:, n:2 * n], _dot_tn(ub, gib), first)
        du_ref[...] = _bf(du)
        _acc(cs_ref, _colsum(du), first)

    rev = lambda i: (nb - 1 - i, 0)
    vec = pl.BlockSpec((1, D_GROUP), lambda i: (0, 0))
    vshape = jax.ShapeDtypeStruct((1, D_GROUP), F32)
    return pl.pallas_call(
        body, name=name, grid=(nb,),
        in_specs=[pl.BlockSpec((tb, D_GROUP), rev),
                  pl.BlockSpec((tb, D_GROUP), rev),
                  pl.BlockSpec((tb, D_GROUP), rev),
                  pl.BlockSpec((tb, 2 * n), rev),
                  pl.BlockSpec((8, 2 * n), lambda i: (jnp.maximum((nb - 1 - i) * halo - 1, 0), 0)),
                  pl.BlockSpec((1, 2 * n), lambda i: (0, 0)),
                  pl.BlockSpec((D_GROUP, 2 * n), lambda i: (0, 0)),
                  pl.BlockSpec((2 * n, D_GROUP), lambda i: (0, 0)),
                  vec,
                  pl.BlockSpec((None, D_GROUP, D_GROUP), lambda i: (layer, 0, 0)),
                  vec],
        out_specs=[pl.BlockSpec((tb, D_GROUP), rev), vec,
                   pl.BlockSpec((D_GROUP, 2 * n), lambda i: (0, 0)),
                   pl.BlockSpec((2 * n, D_GROUP), lambda i: (0, 0)),
                   vec,
                   pl.BlockSpec((D_GROUP, D_GROUP), lambda i: (0, 0)),
                   vec,
                   pl.BlockSpec((1, 2 * n), lambda i: (0, 0))],
        out_shape=[jax.ShapeDtypeStruct((s, D_GROUP), BF16), vshape,
                   jax.ShapeDtypeStruct((D_GROUP, 2 * n), F32), jax.ShapeDtypeStruct((2 * n, D_GROUP), F32),
                   vshape, jax.ShapeDtypeStruct((D_GROUP, D_GROUP), F32), vshape,
                   jax.ShapeDtypeStruct((1, 2 * n), F32)],
        scratch_shapes=[pltpu.VMEM((8, 2 * n), F32), pltpu.VMEM((tb, 2 * n), F32)],
        compiler_params=_cparams(1),
    )(dmix, h_in, y0, hst, hst, a2, bexp, cexp, dskip, wglu, bglu)


CV_HALO = 32


def _gn_stats(hc, pmat):
    mu = _dot3(hc, pmat)
    xc = hc - mu
    var = _dot3(xc * xc, pmat)
    rstd = lax.rsqrt(var + LN_EPS)
    return xc * rstd, rstd


def cv_fwd(h_in, cw, cb, gg, gb, pmat, wpw, layer, bpw, tb, name):
    s = h_in.shape[0]
    hl = CV_HALO

    def body(v_ref, g_ref, cw_ref, cb_ref, gg_ref, gb_ref, p_ref, w_ref, bw_ref, y_ref, hc_ref, ext):
        @pl.when(pl.program_id(0) == 0)
        def _():
            ext[0:hl, :] = jnp.zeros((hl, D_GROUP), F32)

        ext[hl:hl + tb, :] = v_ref[...] * _sigmoid(g_ref[...])
        acc = jnp.zeros((tb, D_GROUP), F32) + cb_ref[...]
        for k in range(CONV_WIDTH):
            off = hl - (CONV_WIDTH - 1) + k
            acc = acc + cw_ref[k:k + 1, :] * ext[off:off + tb, :]
        hc_ref[...] = acc
        ext[0:hl, :] = ext[tb:tb + hl, :]
        xn, _ = _gn_stats(acc, p_ref[...])
        hn = xn * gg_ref[...] + gb_ref[...]
        hs = hn * _sigmoid(hn)
        y_ref[...] = _bf(_dot(_bf(hs), w_ref[...]) + bw_ref[...])

    vec = pl.BlockSpec((1, D_GROUP), lambda i: (0, 0))
    sq = pl.BlockSpec((D_GROUP, D_GROUP), lambda i: (0, 0))
    return pl.pallas_call(
        body, name=name, grid=(s // tb,),
        in_specs=[pl.BlockSpec((tb, D_GROUP), lambda i: (i, 1)),
                  pl.BlockSpec((tb, D_GROUP), lambda i: (i, 2)),
                  pl.BlockSpec((hl, D_GROUP), lambda i: (0, 0)),
                  vec, vec, vec, sq,
                  pl.BlockSpec((None, D_GROUP, D_GROUP), lambda i: (layer, 0, 0)),
                  vec],
        out_specs=[pl.BlockSpec((tb, D_GROUP), lambda i: (i, 0)), pl.BlockSpec((tb, D_GROUP), lambda i: (i, 0))],
        out_shape=[jax.ShapeDtypeStruct((s, D_GROUP), BF16), jax.ShapeDtypeStruct((s, D_GROUP), F32)],
        scratch_shapes=[pltpu.VMEM((hl + tb, D_GROUP), F32)],
        compiler_params=_cparams(1),
    )(h_in, h_in, cw, cb, gg, gb, pmat, wpw, bpw)


def cv_bwd(dmix, h_in, hc, cw, gg, gb, pmat, wpw, layer, tb, name):
    s = h_in.shape[0]
    hl = CV_HALO
    nb = s // tb
    per = tb // hl

    def body(dy_ref, v_ref, g_ref, vh_ref, gh_ref, hc_ref, cw_ref, gg_ref, gb_ref, p_ref, w_ref,
             dvg_ref, cs_ref, dcw_ref, dcb_ref, dgg_ref, dgb_ref, dw_ref, dbw_ref, ext, dext, head):
        i = pl.program_id(0)
        first = i == 0

        @pl.when(first)
        def _():
            head[...] = jnp.zeros_like(head)

        dy = dy_ref[...]
        pm = p_ref[...]
        xn, rstd = _gn_stats(hc_ref[...], pm)
        hn = xn * gg_ref[...] + gb_ref[...]
        sg = _sigmoid(hn)
        hs = hn * sg
        dyb = _bf(dy)
        _acc(dbw_ref, _colsum(dy), first)
        _acc(dw_ref, _dot_tn(_bf(hs), dyb), first)
        dhs = _dot_nt(dyb, w_ref[...])
        dhn = dhs * sg * (1.0 + hn * (1.0 - sg))
        _acc(dgg_ref, _colsum(dhn * xn), first)
        _acc(dgb_ref, _colsum(dhn), first)
        dxn = dhn * gg_ref[...]
        dhc = rstd * (dxn - _dot3(dxn, pm) - xn * _dot3(dxn * xn, pm))
        _acc(dcb_ref, _colsum(dhc), first)
        v = v_ref[...]
        sgg = _sigmoid(g_ref[...])
        keep = jnp.where(i == nb - 1, 0.0, 1.0)
        ext[0:hl, :] = vh_ref[...] * _sigmoid(gh_ref[...]) * keep
        ext[hl:hl + tb, :] = v * sgg
        dext[0:tb, :] = dhc
        dext[tb:tb + hl, :] = head[...]
        head[...] = dhc[0:hl]
        dhg = jnp.zeros((tb, D_GROUP), F32)
        for k in range(CONV_WIDTH):
            off = hl - (CONV_WIDTH - 1) + k
            wk = _colsum(dhc * ext[off:off + tb, :])
            _acc(dcw_ref.at[k:k + 1, :], wk, first)
            back = CONV_WIDTH - 1 - k
            dhg = dhg + cw_ref[k:k + 1, :] * dext[back:back + tb, :]

        @pl.when(first)
        def _():
            dcw_ref[CONV_WIDTH:hl, :] = jnp.zeros((hl - CONV_WIDTH, D_GROUP), F32)

        dv = dhg * sgg
        dg = dhg * v * sgg * (1.0 - sgg)
        dvg_ref[:, 0:D_GROUP] = _bf(dv)
        dvg_ref[:, D_GROUP:2 * D_GROUP] = _bf(dg)
        _acc(cs_ref.at[:, 0:D_GROUP], _colsum(dv), first)
        _acc(cs_ref.at[:, D_GROUP:2 * D_GROUP], _colsum(dg), first)

    vec = pl.BlockSpec((1, D_GROUP), lambda i: (0, 0))
    sq = pl.BlockSpec((D_GROUP, D_GROUP), lambda i: (0, 0))
    tap = pl.BlockSpec((hl, D_GROUP), lambda i: (0, 0))
    vshape = jax.ShapeDtypeStruct((1, D_GROUP), F32)

    def blk(col):
        return pl.BlockSpec((tb, D_GROUP), lambda i: (nb - 1 - i, col))

    def halo_blk(col):
        return pl.BlockSpec((hl, D_GROUP), lambda i: (jnp.maximum((nb - 1 - i) * per - 1, 0), col))

    return pl.pallas_call(
        body, name=name, grid=(nb,),
        in_specs=[blk(1), blk(1), blk(2), halo_blk(1), halo_blk(2),
                  pl.BlockSpec((tb, D_GROUP), lambda i: (nb - 1 - i, 0)),
                  tap, vec, vec, sq,
                  pl.BlockSpec((None, D_GROUP, D_GROUP), lambda i: (layer, 0, 0))],
        out_specs=[pl.BlockSpec((tb, 2 * D_GROUP), lambda i: (nb - 1 - i, 0)),
                   pl.BlockSpec((1, 2 * D_GROUP), lambda i: (0, 0)),
                   tap, vec, vec, vec, sq, vec],
        out_shape=[jax.ShapeDtypeStruct((s, 2 * D_GROUP), BF16), jax.ShapeDtypeStruct((1, 2 * D_GROUP), F32),
                   jax.ShapeDtypeStruct((hl, D_GROUP), F32), vshape, vshape, vshape,
                   jax.ShapeDtypeStruct((D_GROUP, D_GROUP), F32), vshape],
        scratch_shapes=[pltpu.VMEM((hl + tb, D_GROUP), F32), pltpu.VMEM((tb + hl, D_GROUP), F32),
                        pltpu.VMEM((hl, D_GROUP), F32)],
        compiler_params=_cparams(1),
    )(dmix, h_in, h_in, h_in, h_in, hc, cw, gg, gb, pmat, wpw)


LRU_HALO = 8


def _lru_gates(xc, wr_ref, br_ref, wi_ref, bi_ref, sp_ref):
    xcb = _bf(xc)
    r = _sigmoid(_dot(xcb, wr_ref[...]) + br_ref[...])
    gi = _sigmoid(_dot(xcb, wi_ref[...]) + bi_ref[...])
    la = -LRU_C * r * sp_ref[...]
    a = jnp.exp(la)
    e2 = a * a
    sq = jnp.sqrt(-jnp.tanh(la) * (e2 + 1.0))
    return r, gi, a, e2, sq


def _rscan(a, b, tb, reverse):
    row = _rows(a.shape)
    sh = 1
    while sh < tb:
        if reverse:
            amt, mask = tb - sh, row < tb - sh
        else:
            amt, mask = sh, row >= sh
        a_s = jnp.where(mask, pltpu.roll(a, amt, 0), 1.0)
        b_s = jnp.where(mask, pltpu.roll(b, amt, 0), 0.0)
        b = b + a * b_s
        a = a * a_s
        sh *= 2
    return a, b


def lru_fwd(h_in, cw, cb, wr, br, wi, bi, sp, tb, name):
    s = h_in.shape[0]
    hl = LRU_HALO

    def body(xg_ref, xr_ref, cw_ref, cb_ref, wr_ref, br_ref, wi_ref, bi_ref, sp_ref, y_ref, xc_ref, h_ref, ext, carry):
        @pl.when(pl.program_id(0) == 0)
        def _():
            ext[0:hl, :] = jnp.zeros((hl, D_GROUP), F32)
            carry[...] = jnp.zeros_like(carry)

        ext[hl:hl + tb, :] = xr_ref[...]
        xc = jnp.zeros((tb, D_GROUP), F32) + cb_ref[...]
        for k in range(LRU_CONV_WIDTH):
            off = hl - (LRU_CONV_WIDTH - 1) + k
            xc = xc + cw_ref[k:k + 1, :] * ext[off:off + tb, :]
        xc_ref[...] = xc
        ext[0:hl, :] = ext[tb:tb + hl, :]
        r, gi, a, e2, sq = _lru_gates(xc, wr_ref, br_ref, wi_ref, bi_ref, sp_ref)
        pa, hloc = _rscan(a, sq * (gi * xc), tb, False)
        h = hloc + pa * carry[7:8, :]
        h_ref[...] = h
        carry[...] = h[tb - 8:tb]
        y_ref[...] = _bf(h * _gelu(xg_ref[...]))

    vec = pl.BlockSpec((1, D_GROUP), lambda i: (0, 0))
    sq_spec = pl.BlockSpec((D_GROUP, D_GROUP), lambda i: (0, 0))
    blk = pl.BlockSpec((tb, D_GROUP), lambda i: (i, 0))
    return pl.pallas_call(
        body, name=name, grid=(s // tb,),
        in_specs=[pl.BlockSpec((tb, D_GROUP), lambda i: (i, 3)),
                  pl.BlockSpec((tb, D_GROUP), lambda i: (i, 4)),
                  pl.BlockSpec((hl, D_GROUP), lambda i: (0, 0)),
                  vec, sq_spec, vec, sq_spec, vec, vec],
        out_specs=[blk, blk, blk],
        out_shape=[jax.ShapeDtypeStruct((s, D_GROUP), BF16), jax.ShapeDtypeStruct((s, D_GROUP), F32),
                   jax.ShapeDtypeStruct((s, D_GROUP), F32)],
        scratch_shapes=[pltpu.VMEM((hl + tb, D_GROUP), F32), pltpu.VMEM((8, D_GROUP), F32)],
        compiler_params=_cparams(1),
    )(h_in, h_in, cw, cb, wr, br, wi, bi, sp)


def lru_bwd(dmix, h_in, xcs, hs, cw, wr, br, wi, bi, sp, tb, name):
    s = h_in.shape[0]
    hl = LRU_HALO
    nb = s // tb
    per = tb // hl

    def body(dy_ref, xg_ref, xr_ref, xrh_ref, xc_ref, h_ref, hp_ref, cw_ref, wr_ref, br_ref, wi_ref, bi_ref, sp_ref,
             dx_ref, cs_ref, dcw_ref, dcb_ref, dwr_ref, dbr_ref, dwi_ref, dbi_ref, dsp_ref,
             ext, dext, head, anext, gnext):
        i = pl.program_id(0)
        first = i == 0

        @pl.when(first)
        def _():
            head[...] = jnp.zeros_like(head)
            anext[...] = jnp.zeros_like(anext)
            gnext[...] = jnp.zeros_like(gnext)

        dy = dy_ref[...]
        xg = xg_ref[...]
        xc = xc_ref[...]
        h = h_ref[...]
        r, gi, a, e2, sq = _lru_gates(xc, wr_ref, br_ref, wi_ref, bi_ref, sp_ref)
        gate, t = _gelu_parts(xg)
        dh = dy * gate
        dxg = dy * h * _gelu_grad(xg, t)
        row = _rows((tb, D_GROUP))
        coef = jnp.where(row == tb - 1, anext[0:1, :], pltpu.roll(a, tb - 1, 0))
        pc, gloc = _rscan(coef, dh, tb, True)
        gfull = gloc + pc * gnext[0:1, :]
        anext[...] = a[0:8]
        gnext[...] = gfull[0:8]
        keep = jnp.where(i == nb - 1, 0.0, 1.0)
        hprev = jnp.where(row == 0, hp_ref[7:8, :] * keep, pltpu.roll(h, 1, 0))
        da = gfull * hprev
        uu = gi * xc
        dsq = gfull * uu
        duu = gfull * sq
        dla = da * a - dsq * e2 / sq
        sp = sp_ref[...]
        dr = dla * (-LRU_C) * sp
        _acc(dsp_ref, _colsum(dla * (-LRU_C) * r), first)
        dzr = dr * r * (1.0 - r)
        dzi = duu * xc * gi * (1.0 - gi)
        dzrb, dzib = _bf(dzr), _bf(dzi)
        dxc = duu * gi + _dot_nt(dzrb, wr_ref[...]) + _dot_nt(dzib, wi_ref[...])
        xcb = _bf(xc)
        _acc(dwr_ref, _dot_tn(xcb, dzrb), first)
        _acc(dwi_ref, _dot_tn(xcb, dzib), first)
        _acc(dbr_ref, _colsum(dzr), first)
        _acc(dbi_ref, _colsum(dzi), first)
        _acc(dcb_ref, _colsum(dxc), first)
        ext[0:hl, :] = xrh_ref[...] * keep
        ext[hl:hl + tb, :] = xr_ref[...]
        dext[0:tb, :] = dxc
        dext[tb:tb + hl, :] = head[...]
        head[...] = dxc[0:hl]
        dxr = jnp.zeros((tb, D_GROUP), F32)
        for k in range(LRU_CONV_WIDTH):
            off = hl - (LRU_CONV_WIDTH - 1) + k
            _acc(dcw_ref.at[k:k + 1, :], _colsum(dxc * ext[off:off + tb, :]), first)
            back = LRU_CONV_WIDTH - 1 - k
            dxr = dxr + cw_ref[k:k + 1, :] * dext[back:back + tb, :]

        @pl.when(first)
        def _():
            dcw_ref[LRU_CONV_WIDTH:hl, :] = jnp.zeros((hl - LRU_CONV_WIDTH, D_GROUP), F32)

        dx_ref[:, 0:D_GROUP] = _bf(dxg)
        dx_ref[:, D_GROUP:2 * D_GROUP] = _bf(dxr)
        _acc(cs_ref.at[:, 0:D_GROUP], _colsum(dxg), first)
        _acc(cs_ref.at[:, D_GROUP:2 * D_GROUP], _colsum(dxr), first)

    vec = pl.BlockSpec((1, D_GROUP), lambda i: (0, 0))
    sq_spec = pl.BlockSpec((D_GROUP, D_GROUP), lambda i: (0, 0))
    tap = pl.BlockSpec((hl, D_GROUP), lambda i: (0, 0))
    vshape = jax.ShapeDtypeStruct((1, D_GROUP), F32)
    sshape = jax.ShapeDtypeStruct((D_GROUP, D_GROUP), F32)

    def blk(col):
        return pl.BlockSpec((tb, D_GROUP), lambda i: (nb - 1 - i, col))

    def halo_blk(col):
        return pl.BlockSpec((hl, D_GROUP), lambda i: (jnp.maximum((nb - 1 - i) * per - 1, 0), col))

    return pl.pallas_call(
        body, name=name, grid=(nb,),
        in_specs=[blk(2), blk(3), blk(4), halo_blk(4), blk(0), blk(0), halo_blk(0),
                  tap, sq_spec, vec, sq_spec, vec, vec],
        out_specs=[pl.BlockSpec((tb, 2 * D_GROUP), lambda i: (nb - 1 - i, 0)),
                   pl.BlockSpec((1, 2 * D_GROUP), lambda i: (0, 0)),
                   tap, vec, sq_spec, vec, sq_spec, vec, vec],
        out_shape=[jax.ShapeDtypeStruct((s, 2 * D_GROUP), BF16), jax.ShapeDtypeStruct((1, 2 * D_GROUP), F32),
                   jax.ShapeDtypeStruct((hl, D_GROUP), F32), vshape, sshape, vshape, sshape, vshape, vshape],
        scratch_shapes=[pltpu.VMEM((hl + tb, D_GROUP), F32), pltpu.VMEM((tb + hl, D_GROUP), F32),
                        pltpu.VMEM((hl, D_GROUP), F32), pltpu.VMEM((8, D_GROUP), F32), pltpu.VMEM((8, D_GROUP), F32)],
        compiler_params=_cparams(1),
    )(dmix, h_in, h_in, h_in, xcs, hs, hs, cw, wr, br, wi, bi, sp)


ATTN_HEADS = 4
ATTN_HEAD_DIM = 64
ATTN_SCALE = ATTN_HEAD_DIM ** -0.5


def _head_mask(h):
    lane = lax.broadcasted_iota(jnp.int32, (1, D_GROUP), 1)
    return jnp.where((lane >= h * ATTN_HEAD_DIM) & (lane < (h + 1) * ATTN_HEAD_DIM), 1.0, 0.0)


def _softmax_rows(sc):
    e = jnp.exp(sc - jnp.max(sc, -1, keepdims=True))
    return e / jnp.sum(e, -1, keepdims=True)


def attn_fwd(h_in, kv, tb, name):
    s = h_in.shape[0]

    def body(q_ref, kv_ref, y_ref):
        q = q_ref[...]
        kb = _bf(kv_ref[:, 0:D_GROUP])
        vb = _bf(kv_ref[:, D_GROUP:2 * D_GROUP])
        out = jnp.zeros((tb, D_GROUP), F32)
        for h in range(ATTN_HEADS):
            mask = _head_mask(h)
            p = _softmax_rows(_dot_nt(_bf(q * mask), kb) * ATTN_SCALE)
            out = out + _dot(_bf(p), vb) * mask
        y_ref[...] = _bf(out)

    return pl.pallas_call(
        body, name=name, grid=(s // tb,),
        in_specs=[pl.BlockSpec((tb, D_GROUP), lambda i: (i, 5)),
                  pl.BlockSpec((D_GROUP, 2 * D_GROUP), lambda i: (0, 0))],
        out_specs=pl.BlockSpec((tb, D_GROUP), lambda i: (i, 0)),
        out_shape=jax.ShapeDtypeStruct((s, D_GROUP), BF16),
        compiler_params=_cparams(1),
    )(h_in, kv)


def attn_bwd(dmix, h_in, kv, tb, name):
    s = h_in.shape[0]

    def body(do_ref, q_ref, kv_ref, dq_ref, cs_ref, dkv_ref):
        first = pl.program_id(0) == 0
        q = q_ref[...]
        do = do_ref[...]
        kb = _bf(kv_ref[:, 0:D_GROUP])
        vb = _bf(kv_ref[:, D_GROUP:2 * D_GROUP])
        dq = jnp.zeros((tb, D_GROUP), F32)
        dk = jnp.zeros((D_GROUP, D_GROUP), F32)
        dv = jnp.zeros((D_GROUP, D_GROUP), F32)
        for h in range(ATTN_HEADS):
            mask = _head_mask(h)
            qm = _bf(q * mask)
            p = _softmax_rows(_dot_nt(qm, kb) * ATTN_SCALE)
            dom = _bf(do * mask)
            dp = _dot_nt(dom, vb)
            dv = dv + _dot_tn(_bf(p), dom)
            ds = _bf(p * (dp - jnp.sum(dp * p, -1, keepdims=True)) * ATTN_SCALE)
            dq = dq + _dot(ds, kb) * mask
            dk = dk + _dot_tn(ds, qm)
        dq_ref[...] = _bf(dq)
        _acc(cs_ref, _colsum(dq), first)
        _acc(dkv_ref.at[:, 0:D_GROUP], dk, first)
        _acc(dkv_ref.at[:, D_GROUP:2 * D_GROUP], dv, first)

    return pl.pallas_call(
        body, name=name, grid=(s // tb,),
        in_specs=[pl.BlockSpec((tb, D_GROUP), lambda i: (i, 3)),
                  pl.BlockSpec((tb, D_GROUP), lambda i: (i, 5)),
                  pl.BlockSpec((D_GROUP, 2 * D_GROUP), lambda i: (0, 0))],
        out_specs=[pl.BlockSpec((tb, D_GROUP), lambda i: (i, 0)),
                   pl.BlockSpec((1, D_GROUP), lambda i: (0, 0)),
                   pl.BlockSpec((D_GROUP, 2 * D_GROUP), lambda i: (0, 0))],
        out_shape=[jax.ShapeDtypeStruct((s, D_GROUP), BF16), jax.ShapeDtypeStruct((1, D_GROUP), F32),
                   jax.ShapeDtypeStruct((D_GROUP, 2 * D_GROUP), F32)],
        compiler_params=_cparams(1),
    )(dmix, h_in, kv)


FFN_RB = 16
FFN_UNROLL_FWD = 4
FFN_UNROLL_BWD = 2
FFN_TAP_ROWS = 8
FFN_TN = D_FF // 2


def _shift_down(cur, tail, k):
    return pltpu.roll(jnp.concatenate([tail, cur], axis=0), k, 0)[SUBLANES:]


def _shift_up(cur, head, k):
    rb = cur.shape[0]
    return pltpu.roll(jnp.concatenate([cur, head], axis=0), rb + SUBLANES - k, 0)[:rb]


def _fold8(v):
    tot = v[0:SUBLANES]
    for t in range(1, v.shape[0] // SUBLANES):
        tot = tot + v[t * SUBLANES:(t + 1) * SUBLANES]
    return tot


def _strip(r):
    return pl.ds(pl.multiple_of(r * FFN_RB, FFN_RB), FFN_RB)


def _tail_before(r):
    return pl.ds(pl.multiple_of(jnp.maximum(r * FFN_RB - SUBLANES, 0), SUBLANES), SUBLANES)


def ffn_act_fwd(u, cw, cb, tb, name):
    s = u.shape[0]
    rb = FFN_RB
    nct = D_FF // FFN_TN
    nstrip = tb // rb

    def body(uv_ref, ug_ref, wv_ref, wg_ref, bv_ref, bg_ref, hf_ref, tailv, tailg):
        @pl.when(pl.program_id(1) == 0)
        def _():
            tailv[...] = jnp.zeros_like(tailv)
            tailg[...] = jnp.zeros_like(tailg)

        for cc in range(FFN_TN // LANES):
            cols = slice(cc * LANES, (cc + 1) * LANES)
            wv = [wv_ref[k:k + 1, cols] for k in range(FFN_CONV_WIDTH)]
            wg = [wg_ref[k:k + 1, cols] for k in range(FFN_CONV_WIDTH)]
            bv, bg = bv_ref[:, cols], bg_ref[:, cols]

            def strip(r, carry):
                tail_v, tail_g = carry
                cur_v, cur_g = uv_ref[_strip(r), cols], ug_ref[_strip(r), cols]
                vc = wv[0] * _shift_down(cur_v, tail_v, 2) + wv[1] * _shift_down(cur_v, tail_v, 1) + wv[2] * cur_v + bv
                gc = wg[0] * _shift_down(cur_g, tail_g, 2) + wg[1] * _shift_down(cur_g, tail_g, 1) + wg[2] * cur_g + bg
                hf_ref[_strip(r), cols] = _bf(vc * _gelu(gc))
                return cur_v[rb - SUBLANES:], cur_g[rb - SUBLANES:]

            def strips(q, carry):
                for k in range(FFN_UNROLL_FWD):
                    carry = strip(q * FFN_UNROLL_FWD + k, carry)
                return carry

            last_v, last_g = lax.fori_loop(0, nstrip // FFN_UNROLL_FWD, strips, (tailv[:, cols], tailg[:, cols]))
            tailv[:, cols] = last_v
            tailg[:, cols] = last_g

    return pl.pallas_call(
        body, name=name, grid=(nct, s // tb),
        in_specs=[pl.BlockSpec((tb, FFN_TN), lambda c, i: (i, c)),
                  pl.BlockSpec((tb, FFN_TN), lambda c, i: (i, c + nct)),
                  pl.BlockSpec((FFN_TAP_ROWS, FFN_TN), lambda c, i: (0, c)),
                  pl.BlockSpec((FFN_TAP_ROWS, FFN_TN), lambda c, i: (0, c + nct)),
                  pl.BlockSpec((1, FFN_TN), lambda c, i: (0, c)),
                  pl.BlockSpec((1, FFN_TN), lambda c, i: (0, c + nct))],
        out_specs=pl.BlockSpec((tb, FFN_TN), lambda c, i: (i, c)),
        out_shape=jax.ShapeDtypeStruct((s, D_FF), BF16),
        scratch_shapes=[pltpu.VMEM((SUBLANES, FFN_TN), F32), pltpu.VMEM((SUBLANES, FFN_TN), F32)],
        compiler_params=_cparams(2),
    )(u, u, cw, cw, cb, cb)


def ffn_act_bwd(dhf, u, cw, cb, tb, name):
    s = u.shape[0]
    rb = FFN_RB
    nct = D_FF // FFN_TN
    nb = s // tb
    per = tb // SUBLANES
    nstrip = tb // rb
    ntap = FFN_CONV_WIDTH

    def body(dh_ref, uv_ref, ug_ref, uvh_ref, ugh_ref, wv_ref, wg_ref, bv_ref, bg_ref,
             du_ref, dwv_ref, dwg_ref, headv, headg):
        i = pl.program_id(1)
        first = i == 0

        @pl.when(first)
        def _():
            headv[...] = jnp.zeros_like(headv)
            headg[...] = jnp.zeros_like(headg)
            dwv_ref[...] = jnp.zeros_like(dwv_ref)
            dwg_ref[...] = jnp.zeros_like(dwg_ref)

        keep = jnp.where(i == nb - 1, 0.0, 1.0)
        zero = jnp.zeros((SUBLANES, LANES), F32)
        for cc in range(FFN_TN // LANES):
            cols = slice(cc * LANES, (cc + 1) * LANES)
            wv = [wv_ref[k:k + 1, cols] for k in range(ntap)]
            wg = [wg_ref[k:k + 1, cols] for k in range(ntap)]
            bv, bg = bv_ref[:, cols], bg_ref[:, cols]
            halo_v, halo_g = uvh_ref[:, cols] * keep, ugh_ref[:, cols] * keep

            def strip(ii, carry):
                head_dv, head_dg, acc_v, acc_g = carry
                r = nstrip - 1 - ii
                cur_v, cur_g = uv_ref[_strip(r), cols], ug_ref[_strip(r), cols]
                tail_v = jnp.where(r == 0, halo_v, uv_ref[_tail_before(r), cols])
                tail_g = jnp.where(r == 0, halo_g, ug_ref[_tail_before(r), cols])
                sv = [_shift_down(cur_v, tail_v, 2), _shift_down(cur_v, tail_v, 1), cur_v]
                sg = [_shift_down(cur_g, tail_g, 2), _shift_down(cur_g, tail_g, 1), cur_g]
                vc = wv[0] * sv[0] + wv[1] * sv[1] + wv[2] * sv[2] + bv
                gc = wg[0] * sg[0] + wg[1] * sg[1] + wg[2] * sg[2] + bg
                ge, t = _gelu_parts(gc)
                dh = dh_ref[_strip(r), cols]
                dvc = dh * ge
                dgc = dh * vc * _gelu_grad(gc, t)
                acc_v = tuple(acc_v[k] + _fold8(dvc * sv[k]) for k in range(ntap)) + (acc_v[ntap] + _fold8(dvc),)
                acc_g = tuple(acc_g[k] + _fold8(dgc * sg[k]) for k in range(ntap)) + (acc_g[ntap] + _fold8(dgc),)
                du_v = wv[2] * dvc + wv[1] * _shift_up(dvc, head_dv, 1) + wv[0] * _shift_up(dvc, head_dv, 2)
                du_g = wg[2] * dgc + wg[1] * _shift_up(dgc, head_dg, 1) + wg[0] * _shift_up(dgc, head_dg, 2)
                du_ref[0, _strip(r), cols] = _bf(du_v)
                du_ref[1, _strip(r), cols] = _bf(du_g)
                return dvc[0:SUBLANES], dgc[0:SUBLANES], acc_v, acc_g

            init = (headv[:, cols], headg[:, cols], (zero,) * (ntap + 1), (zero,) * (ntap + 1))
            def strips(q, carry):
                for k in range(FFN_UNROLL_BWD):
                    carry = strip(q * FFN_UNROLL_BWD + k, carry)
                return carry

            top_dv, top_dg, acc_v, acc_g = lax.fori_loop(0, nstrip // FFN_UNROLL_BWD, strips, init)
            headv[:, cols] = top_dv
            headg[:, cols] = top_dg
            for k in range(ntap + 1):
                dwv_ref[k:k + 1, cols] += _colsum(acc_v[k])
                dwg_ref[k:k + 1, cols] += _colsum(acc_g[k])

    def blk(shift):
        return pl.BlockSpec((tb, FFN_TN), lambda c, i: (nb - 1 - i, c + shift))

    def halo_blk(shift):
        return pl.BlockSpec((SUBLANES, FFN_TN), lambda c, i: (jnp.maximum((nb - 1 - i) * per - 1, 0), c + shift))

    tapv = pl.BlockSpec((FFN_TAP_ROWS, FFN_TN), lambda c, i: (0, c))
    tapg = pl.BlockSpec((FFN_TAP_ROWS, FFN_TN), lambda c, i: (0, c + nct))
    return pl.pallas_call(
        body, name=name, grid=(nct, nb),
        in_specs=[blk(0), blk(0), blk(nct), halo_blk(0), halo_blk(nct), tapv, tapg,
                  pl.BlockSpec((1, FFN_TN), lambda c, i: (0, c)),
                  pl.BlockSpec((1, FFN_TN), lambda c, i: (0, c + nct))],
        out_specs=[pl.BlockSpec((2, tb, FFN_TN), lambda c, i: (0, nb - 1 - i, c)), tapv, tapv],
        out_shape=[jax.ShapeDtypeStruct((2, s, D_FF), BF16), jax.ShapeDtypeStruct((FFN_TAP_ROWS, D_FF), F32),
                   jax.ShapeDtypeStruct((FFN_TAP_ROWS, D_FF), F32)],
        scratch_shapes=[pltpu.VMEM((SUBLANES, FFN_TN), F32), pltpu.VMEM((SUBLANES, FFN_TN), F32)],
        compiler_params=_cparams(2),
    )(dhf, u, u, u, u, cw, cw, cb, cb)


def _place():
    x, y, c = lax.axis_index("x"), lax.axis_index("y"), lax.axis_index("c")
    return x, y, c, 2 * x + y


def _chip_peer(x, y, d):
    return jnp.bitwise_xor(x, d >> 1), jnp.bitwise_xor(y, d & 1)


def gather_weights(slabs):
    n = len(slabs)

    def body(*refs):
        outs = refs[n:2 * n]
        ssem, rsem, fsem, gsem = refs[2 * n:]
        x, y, c, j = _place()
        sib = (x, y, 1 - c)
        ici = {}
        for w in range(n):
            for d in (1, 2, 3):
                px, py = _chip_peer(x, y, d)
                cp = pltpu.make_async_remote_copy(
                    src_ref=outs[w].at[c, j], dst_ref=outs[w].at[c, j], send_sem=ssem.at[w, d - 1],
                    recv_sem=rsem.at[w, d - 1], device_id=(px, py, c), device_id_type=MESH)
                cp.start()
                ici[w, d] = cp
        fwd = {}
        for d in (1, 2, 3):
            jd = jnp.bitwise_xor(j, d)
            for w in range(n):
                ici[w, d].wait_recv()
                cp = pltpu.make_async_remote_copy(
                    src_ref=outs[w].at[c, jd], dst_ref=outs[w].at[c, jd], send_sem=fsem.at[w, d - 1],
                    recv_sem=gsem.at[w, d - 1], device_id=sib, device_id_type=MESH)
                cp.start()
                fwd[w, d] = cp
        for w in range(n):
            for d in (1, 2, 3):
                fwd[w, d].wait_recv()
                fwd[w, d].wait_send()
                ici[w, d].wait_send()

    out_shape = [jax.ShapeDtypeStruct(a.shape, a.dtype) for a in slabs]
    return pl.pallas_call(
        body, name="gather_weights", in_specs=[ANY] * n, out_specs=[ANY] * n, out_shape=out_shape,
        input_output_aliases={w: w for w in range(n)},
        scratch_shapes=[pltpu.SemaphoreType.DMA((n, 3)), pltpu.SemaphoreType.DMA((n, 3)),
                        pltpu.SemaphoreType.DMA((n, 3)), pltpu.SemaphoreType.DMA((n, 3))],
    )(*slabs)


def _own_slab(shard, jidx):
    slab = jnp.zeros((DEPTH, N_CHIPS) + shard.shape[1:], shard.dtype)
    return lax.dynamic_update_slice_in_dim(slab, shard[:, None], jidx, axis=1)


def exchange_layers(gbig, small):
    n = len(gbig)

    def body(*refs):
        g_refs, s_ref = refs[:n], refs[n]
        got_refs, all_ref = refs[n + 1:2 * n + 1], refs[2 * n + 1]
        dsem, esem, lsem, ssem, rsem, fsem, hsem = refs[2 * n + 2:]
        x, y, c, j = _place()
        sib = (x, y, 1 - c)
        me = 4 * x + 2 * y + c
        big = []
        for k in range(n):
            cp = pltpu.make_async_remote_copy(src_ref=g_refs[k].at[1 - c], dst_ref=got_refs[k], send_sem=dsem.at[k],
                                              recv_sem=esem.at[k], device_id=sib, device_id_type=MESH)
            cp.start()
            big.append(cp)
        mine = pltpu.make_async_copy(s_ref, all_ref.at[me], lsem)
        mine.start()

        def small_copy(k, block, to, sems, from_input):
            return pltpu.make_async_remote_copy(
                src_ref=s_ref if from_input else all_ref.at[block], dst_ref=all_ref.at[block],
                send_sem=sems[0].at[k], recv_sem=sems[1].at[k], device_id=to, device_id_type=MESH)

        first = [small_copy(0, me, sib, (ssem, rsem), True)]
        for d in (1, 2, 3):
            px, py = _chip_peer(x, y, d)
            first.append(small_copy(d, me, (px, py, c), (ssem, rsem), True))
        for cp in first:
            cp.start()
        passed = []
        for d in (1, 2, 3):
            px, py = _chip_peer(x, y, d)
            src_block = 4 * px + 2 * py + c
            small_copy(d, src_block, sib, (ssem, rsem), False).wait_recv()
            cp = small_copy(d - 1, src_block, sib, (fsem, hsem), False)
            cp.start()
            passed.append(cp)
        small_copy(0, me, sib, (ssem, rsem), False).wait_recv()
        for cp in passed:
            cp.wait_recv()
        for cp in first + passed:
            cp.wait_send()
        mine.wait()
        for cp in big:
            cp.wait()

    r2 = small.shape[0]
    return pl.pallas_call(
        body, name="exchange_layers", in_specs=[ANY] * (n + 1), out_specs=[ANY] * (n + 1),
        out_shape=[jax.ShapeDtypeStruct(g.shape[1:], F32) for g in gbig] + [jax.ShapeDtypeStruct((8, r2, LANES), F32)],
        scratch_shapes=[pltpu.SemaphoreType.DMA((n,)), pltpu.SemaphoreType.DMA((n,)), pltpu.SemaphoreType.DMA,
                        pltpu.SemaphoreType.DMA((4,)), pltpu.SemaphoreType.DMA((4,)),
                        pltpu.SemaphoreType.DMA((3,)), pltpu.SemaphoreType.DMA((3,))],
    )(*gbig, small)


def scatter_shards(s1):
    n = len(s1)

    def body(*refs):
        s_refs, got_refs = refs[:n], refs[n:2 * n]
        ssem, rsem = refs[2 * n:]
        x, y, c, j = _place()
        cps = []
        for d in (1, 2, 3):
            px, py = _chip_peer(x, y, d)
            for k in range(n):
                cp = pltpu.make_async_remote_copy(
                    src_ref=s_refs[k].at[jnp.bitwise_xor(j, d)], dst_ref=got_refs[k].at[d - 1],
                    send_sem=ssem.at[k, d - 1], recv_sem=rsem.at[k, d - 1], device_id=(px, py, c), device_id_type=MESH)
                cp.start()
                cps.append(cp)
        for cp in cps:
            cp.wait()

    return pl.pallas_call(
        body, name="scatter_shards", in_specs=[ANY] * n, out_specs=[ANY] * n,
        out_shape=[jax.ShapeDtypeStruct((3,) + a.shape[1:], a.dtype) for a in s1],
        scratch_shapes=[pltpu.SemaphoreType.DMA((n, 3)), pltpu.SemaphoreType.DMA((n, 3))],
    )(*s1)


def share_with_sibling(parts):
    n = len(parts)

    def body(*refs):
        out_refs = refs[n:2 * n]
        ssem, rsem = refs[2 * n:]
        x, y, c, j = _place()
        cps = []
        for k in range(n):
            cp = pltpu.make_async_remote_copy(src_ref=out_refs[k].at[c], dst_ref=out_refs[k].at[c], send_sem=ssem.at[k],
                                              recv_sem=rsem.at[k], device_id=(x, y, 1 - c), device_id_type=MESH)
            cp.start()
            cps.append(cp)
        for cp in cps:
            cp.wait()

    return pl.pallas_call(
        body, name="share_with_sibling", in_specs=[ANY] * n, out_specs=[ANY] * n,
        out_shape=[jax.ShapeDtypeStruct(a.shape, F32) for a in parts],
        input_output_aliases={k: k for k in range(n)},
        scratch_shapes=[pltpu.SemaphoreType.DMA((n,)), pltpu.SemaphoreType.DMA((n,))],
    )(*parts)


def add_layer_halves(g, got, cidx, name):
    _, nch, r, cdim = g.shape
    tr = _row_tile(r, cdim, mult=16)

    def body(c_ref, a_ref, b_ref, o_ref):
        o_ref[...] = _bf(a_ref[...] + b_ref[...])

    grid_spec = pltpu.PrefetchScalarGridSpec(
        num_scalar_prefetch=1, grid=(nch, r // tr),
        in_specs=[pl.BlockSpec((None, None, tr, cdim), lambda jj, i, c_ref: (c_ref[0], jj, i, 0)),
                  pl.BlockSpec((None, tr, cdim), lambda jj, i, c_ref: (jj, i, 0))],
        out_specs=pl.BlockSpec((None, tr, cdim), lambda jj, i, c_ref: (jj, i, 0)))
    return pl.pallas_call(
        body, name=name, grid_spec=grid_spec,
        out_shape=jax.ShapeDtypeStruct((nch, r, cdim), BF16), compiler_params=_cparams(2),
    )(cidx, g, got)


def add_chip_parts(s1, got, jc, name):
    _, r, cdim = s1.shape
    tr = _row_tile(r, cdim, mult=16)

    def body(jc_ref, a_ref, g0_ref, g1_ref, g2_ref, o_ref):
        o_ref[...] = ((a_ref[...].astype(F32) + g0_ref[...].astype(F32)) + g1_ref[...].astype(F32)) + g2_ref[...].astype(F32)

    def slot(k):
        return pl.BlockSpec((None, tr, cdim), lambda i, jc_ref: (k, i, 0))

    grid_spec = pltpu.PrefetchScalarGridSpec(
        num_scalar_prefetch=1, grid=(r // tr,),
        in_specs=[pl.BlockSpec((None, tr, cdim), lambda i, jc_ref: (jc_ref[0], i, 0)), slot(0), slot(1), slot(2)],
        out_specs=pl.BlockSpec((None, tr, cdim), lambda i, jc_ref: (jc_ref[1], i, 0)))
    return pl.pallas_call(
        body, name=name, grid_spec=grid_spec,
        out_shape=jax.ShapeDtypeStruct((DEPTH, r, cdim), F32), compiler_params=_cparams(1),
    )(jc, s1, got, got, got)


def sum_devices(allp):
    _, r, _ = allp.shape

    def body(a_ref, o_ref):
        tot = a_ref[0]
        for k in range(1, 8):
            tot = tot + a_ref[k]
        o_ref[...] = tot

    tr = r // 2 if r % 16 == 0 else r
    return pl.pallas_call(
        body, name="sum_devices", grid=(r // tr,),
        in_specs=[pl.BlockSpec((8, tr, LANES), lambda i: (0, i, 0))],
        out_specs=pl.BlockSpec((tr, LANES), lambda i: (i, 0)),
        out_shape=jax.ShapeDtypeStruct((r, LANES), F32), compiler_params=_cparams(1),
    )(allp)


def _row_tile(r, cdim, limit_bytes=1 << 20, mult=8):
    best = None
    for tr in range(mult, r + 1, mult):
        if r % tr == 0 and tr * cdim * 4 <= limit_bytes:
            best = tr
    return best if best is not None else r


def adamw(w, g, m, v, name):
    r, cdim = w.shape
    tr = _row_tile(r, cdim)
    bc1 = 1.0 - ADAM_B1 ** ADAM_STEP
    bc2 = 1.0 - ADAM_B2 ** ADAM_STEP

    def body(w_ref, g_ref, m_ref, v_ref, d_ref, nm_ref, nv_ref, go_ref):
        gv = g_ref[...]
        nm = ADAM_B1 * m_ref[...] + (1.0 - ADAM_B1) * gv
        nv = ADAM_B2 * v_ref[...] + (1.0 - ADAM_B2) * (gv * gv)
        d_ref[...] = -ADAM_LR * ((nm / bc1) / (jnp.sqrt(nv / bc2) + ADAM_EPS) + ADAM_WD * w_ref[...])
        nm_ref[...] = nm
        nv_ref[...] = nv
        go_ref[...] = gv

    blk = pl.BlockSpec((tr, cdim), lambda i: (i, 0))
    shape = jax.ShapeDtypeStruct((r, cdim), F32)
    return pl.pallas_call(
        body, name=name, grid=(r // tr,), in_specs=[blk] * 4, out_specs=[blk] * 4, out_shape=[shape] * 4,
        compiler_params=_cparams(1),
    )(w, g, m, v)


def _s5_prepare(lam_re, lam_im, log_dt, b_re, b_im, c_re, c_im):
    groups, ch = 16, 16
    dt = jnp.exp(log_dt)[:, None]
    mag = jnp.exp(lam_re * dt)
    a_r, a_i = mag * jnp.cos(lam_im * dt), mag * jnp.sin(lam_im * dt)
    den = lam_re * lam_re + lam_im * lam_im
    q_r = ((a_r - 1.0) * lam_re + a_i * lam_im) / den
    q_i = (a_i * lam_re - (a_r - 1.0) * lam_im) / den
    bb_r = q_r[..., None] * b_re - q_i[..., None] * b_im
    bb_i = q_r[..., None] * b_im + q_i[..., None] * b_re
    eye = jnp.eye(groups, dtype=F32)

    def expand_b(bb):
        return jnp.einsum("gpc,gh->gchp", bb, eye).reshape(groups * ch, N_STATE)

    def expand_c(cc):
        return jnp.einsum("gcp,gh->hpgc", cc, eye).reshape(N_STATE, groups * ch)

    a2 = jnp.concatenate([a_r.reshape(1, N_STATE), a_i.reshape(1, N_STATE)], axis=1)
    bexp = jnp.concatenate([expand_b(bb_r), expand_b(bb_i)], axis=1)
    cexp = jnp.concatenate([expand_c(c_re), -expand_c(c_im)], axis=0)
    return a2, bexp, cexp


def _lru_prepare(w_r, w_i, lam):
    heads = 4
    eye = jnp.eye(heads, dtype=F32)

    def expand(w):
        return jnp.einsum("hij,hk->hikj", w, eye).reshape(D_GROUP, D_GROUP)

    return expand(w_r), expand(w_i), jax.nn.softplus(-lam).reshape(1, D_GROUP)


def _pad_rows(a, rows):
    return jnp.pad(a, ((0, rows - a.shape[0]), (0, 0)))


def _group_mean_matrix():
    gidx = jnp.arange(D_GROUP) // 64
    return (gidx[:, None] == gidx[None, :]).astype(BF16) * jnp.asarray(1.0 / 64.0, BF16)


def _pack_rows(arrs, width):
    parts = []
    for a in arrs:
        flat = a.reshape(-1)
        pad = (-flat.shape[0]) % width
        parts.append(jnp.pad(flat, (0, pad)) if pad else flat)
    flat = jnp.concatenate(parts)
    rows = flat.shape[0] // width
    pad_rows = (-rows) % 16
    if pad_rows:
        flat = jnp.pad(flat, (0, pad_rows * width))
    return flat.reshape(-1, width)


def _unpack_rows(packed, shapes, width):
    flat = packed.reshape(-1)
    out, off = [], 0
    for shp in shapes:
        size = math.prod(shp)
        out.append(flat[off:off + size].reshape(shp))
        off += size + ((-size) % width)
    return out


def kernel(x, mem, ln_in_g, ln_in_b, w_in, b_in, s5_lam_re, s5_lam_im, s5_log_dt, s5_b_re, s5_b_im, s5_c_re, s5_c_im, s5_d, s5_w_glu, s5_b_glu, cv_w, cv_b, cv_gn_g, cv_gn_b, cv_w_pw, cv_b_pw, lru_conv_w, lru_conv_b, lru_w_r, lru_b_r, lru_w_i, lru_b_i, lru_lam, attn_w_kv, w_out, b_out, ln1_g, ln1_b, ffn_w_up, ffn_conv_w, ffn_conv_b, ffn_w_down, ln2_g, ln2_b, loss_target, m_ln_in_g, m_ln_in_b, m_w_in, m_b_in, m_s5_lam_re, m_s5_lam_im, m_s5_log_dt, m_s5_b_re, m_s5_b_im, m_s5_c_re, m_s5_c_im, m_s5_d, m_s5_w_glu, m_s5_b_glu, m_cv_w, m_cv_b, m_cv_gn_g, m_cv_gn_b, m_cv_w_pw, m_cv_b_pw, m_lru_conv_w, m_lru_conv_b, m_lru_w_r, m_lru_b_r, m_lru_w_i, m_lru_b_i, m_lru_lam, m_attn_w_kv, m_w_out, m_b_out, m_ln1_g, m_ln1_b, m_ffn_w_up, m_ffn_conv_w, m_ffn_conv_b, m_ffn_w_down, m_ln2_g, m_ln2_b, v_ln_in_g, v_ln_in_b, v_w_in, v_b_in, v_s5_lam_re, v_s5_lam_im, v_s5_log_dt, v_s5_b_re, v_s5_b_im, v_s5_c_re, v_s5_c_im, v_s5_d, v_s5_w_glu, v_s5_b_glu, v_cv_w, v_cv_b, v_cv_gn_g, v_cv_gn_b, v_cv_w_pw, v_cv_b_pw, v_lru_conv_w, v_lru_conv_b, v_lru_w_r, v_lru_b_r, v_lru_w_i, v_lru_b_i, v_lru_lam, v_attn_w_kv, v_w_out, v_b_out, v_ln1_g, v_ln1_b, v_ffn_w_up, v_ffn_conv_w, v_ffn_conv_b, v_ffn_w_down, v_ln2_g, v_ln2_b):
    p = dict(locals())
    xs = x[0]
    mems = mem[0]
    target = loss_target[0]
    s = xs.shape[0]
    cidx = lax.axis_index("c")
    jidx = 2 * lax.axis_index("x") + lax.axis_index("y")
    tb_scan = min(256, s)
    tb_attn = min(512, s)
    tb_ffn = min(256, s)

    small_sh_names = list(SMALL_SHARDED)
    small_sh_shapes = [p[nm].shape[1:] for nm in small_sh_names]
    small_pack = jnp.stack([_pack_rows([p[nm][l] for nm in small_sh_names], LANES) for l in range(DEPTH)])
    gathered = gather_weights([_own_slab(_bf(p[nm]), jidx) for nm in BIG] + [_own_slab(small_pack, jidx)])
    g_w_in, g_w_kv, g_w_out, g_w_up, g_w_down, g_small = gathered
    g_w_kv = g_w_kv.reshape(DEPTH, 1, D_MODEL, 2 * D_GROUP)
    g_w_out = g_w_out.reshape(DEPTH, 1, D_MODEL, D_MODEL)
    g_w_down = g_w_down.reshape(DEPTH, 1, D_FF, D_MODEL)
    full_small = {nm: [] for nm in small_sh_names}
    for l in range(DEPTH):
        per_chip = [_unpack_rows(g_small[l, jj], small_sh_shapes, LANES) for jj in range(N_CHIPS)]
        for k, nm in enumerate(small_sh_names):
            full_small[nm].append(jnp.concatenate([per_chip[jj][k] for jj in range(N_CHIPS)],
                                                  axis=SMALL_SHARDED[nm] - 1))
    w_glu_bf = _bf(jnp.stack(full_small['s5_w_glu']))
    w_pw_bf = _bf(jnp.stack(full_small['cv_w_pw']))
    pmat = _group_mean_matrix()

    def vec(a):
        return a.reshape(1, -1)

    xh0, rs0, xb0 = ln_fwd(xs, vec(ln_in_g), vec(ln_in_b), "ln_in")
    saved = []
    prev = dict(xh=xh0, rs=rs0, xb=xb0, g=vec(ln_in_g), b=vec(ln_in_b))
    for l in range(DEPTH):
        sv = dict(prev=prev)
        (a2, bexp, cexp), sv['s5_vjp'] = jax.vjp(_s5_prepare, s5_lam_re[l], s5_lam_im[l], s5_log_dt[l],
                                                 s5_b_re[l], s5_b_im[l], s5_c_re[l], s5_c_im[l])
        (wr, wi, sp), sv['lru_vjp'] = jax.vjp(_lru_prepare, lru_w_r[l], lru_w_i[l], lru_lam[l])
        sv.update(a2=a2, bexp=_bf(bexp), cexp=_bf(cexp), wr=_bf(wr), wi=_bf(wi), sp=sp)
        sv['cvw'] = _pad_rows(full_small['cv_w'][l], CV_HALO)
        sv['lcw'] = _pad_rows(full_small['lru_conv_w'][l], LRU_HALO)
        sv['fcw'] = _pad_rows(full_small['ffn_conv_w'][l], FFN_TAP_ROWS)
        h_in = mm_nn(prev['xb'], g_w_in, l, vec(b_in[l]), F32, 512, f"in_proj{l}")
        kv = mm_nn(mems, g_w_kv, l, jnp.zeros((1, 2 * D_GROUP), F32), F32, 256, f"kv_proj{l}")
        y_s5, hst, y0 = s5_fwd(h_in, a2, sv['bexp'], sv['cexp'], vec(s5_d[l]), w_glu_bf, l, vec(s5_b_glu[l]),
                               tb_scan, f"s5_fwd{l}")
        y_cv, hc = cv_fwd(h_in, sv['cvw'], vec(cv_b[l]), vec(cv_gn_g[l]), vec(cv_gn_b[l]), pmat, w_pw_bf, l,
                          vec(cv_b_pw[l]), tb_scan, f"cv_fwd{l}")
        y_lru, xcs, hls = lru_fwd(h_in, sv['lcw'], vec(lru_conv_b[l]), sv['wr'], vec(lru_b_r[l]), sv['wi'],
                                  vec(lru_b_i[l]), sp, tb_scan, f"lru_fwd{l}")
        y_mem = attn_fwd(h_in, kv, tb_attn, f"attn_fwd{l}")
        mix_in = jnp.concatenate([y_s5, y_cv, y_lru, y_mem], axis=1)
        xh1, rs1, xb1 = proj_ln(mix_in, g_w_out.reshape(DEPTH, D_MODEL, D_MODEL), l, vec(b_out[l]),
                                prev['xh'], prev['g'], prev['b'], vec(ln1_g[l]), vec(ln1_b[l]), f"out_proj_ln{l}")
        u = mm_nn(xb1, g_w_up, l, jnp.zeros((1, 2 * D_FF), F32), F32, 1024, f"ffn_up{l}")
        hf = ffn_act_fwd(u, sv['fcw'], vec(ffn_conv_b[l]), tb_ffn, f"ffn_act{l}")
        xh2, rs2, xb2 = proj_ln(hf, g_w_down.reshape(DEPTH, D_FF, D_MODEL), l, jnp.zeros((1, D_MODEL), F32),
                                xh1, vec(ln1_g[l]), vec(ln1_b[l]), vec(ln2_g[l]), vec(ln2_b[l]), f"ffn_down_ln{l}")
        sv.update(h_in=h_in, kv=kv, hst=hst, y0=y0, hc=hc, xcs=xcs, hls=hls, mix_in=mix_in,
                  xh1=xh1, rs1=rs1, xb1=xb1, u=u, hf=hf, xh2=xh2, rs2=rs2)
        saved.append(sv)
        prev = dict(xh=xh2, rs=rs2, xb=xb2, g=vec(ln2_g[l]), b=vec(ln2_b[l]))

    grads = {}
    per_layer = {nm: [None] * DEPTH for nm in WEIGHTS if nm not in ('ln_in_g', 'ln_in_b')}
    gbig = [None] * len(BIG)
    dx = None
    for l in reversed(range(DEPTH)):
        sv = saved[l]
        pv = sv['prev']
        if l == DEPTH - 1:
            dr2, dg2, db2, sqerr = loss_ln_bwd(target, sv['xh2'], sv['rs2'], vec(ln2_g[l]), vec(ln2_b[l]), "loss_ln2_bwd")
            loss_local = 0.5 / D_MODEL * jnp.sum(sqerr)
        else:
            dr2, dg2, db2, _ = ln_bwd(dx, sv['xh2'], sv['rs2'], vec(ln2_g[l]), f"ln2_bwd{l}")
        per_layer['ln2_g'][l], per_layer['ln2_b'][l] = dg2[0], db2[0]
        tm_nt = min(512, s)
        ts_tn = min(512, s)
        dhf = mm_nt(dr2, (tm_nt, D_MODEL), lambda i, rt, j: (i, 0), g_w_down, l, FFN_TN, None, F32, 512, s, f"ffn_down_dx{l}")
        gbig[4] = mm_tn(sv['hf'], dr2, (ts_tn, D_MODEL), lambda kt, j, st: (st, 0), 1, D_MODEL, FFN_TN, 512, s,
                        l, gbig[4], f"ffn_down_dw{l}")
        du, dcwv, dcwg = ffn_act_bwd(dhf, sv['u'], sv['fcw'], vec(ffn_conv_b[l]), tb_ffn, f"ffn_act_bwd{l}")
        dcw = jnp.concatenate([dcwv, dcwg], axis=1)
        per_layer['ffn_conv_w'][l] = dcw[0:FFN_CONV_WIDTH]
        per_layer['ffn_conv_b'][l] = dcw[FFN_CONV_WIDTH]
        dx1 = mm_nt(du, (None, tm_nt, FFN_TN), lambda i, rt, j: (j // 2, i, j % 2), g_w_up, l, D_MODEL, dr2, F32,
                    512, s, f"ffn_up_dx{l}")
        gbig[3] = mm_tn(sv['xb1'], du, (None, ts_tn, FFN_TN), lambda kt, j, st: (j // 2, st, j % 2), N_CHIPS, FFN_TN,
                        D_MODEL, 512, s, l, gbig[3], f"ffn_up_dw{l}")
        dr1, dg1, db1, cs1 = ln_bwd(dx1, sv['xh1'], sv['rs1'], vec(ln1_g[l]), f"ln1_bwd{l}")
        per_layer['ln1_g'][l], per_layer['ln1_b'][l], per_layer['b_out'][l] = dg1[0], db1[0], cs1[0]
        dmix = mm_nt(dr1, (tm_nt, D_MODEL), lambda i, rt, j: (i, 0), g_w_out, l, D_MODEL, None, F32, 512, s, f"out_proj_dx{l}")
        gbig[2] = mm_tn(sv['mix_in'], dr1, (ts_tn, D_MODEL), lambda kt, j, st: (st, 0), 1, D_MODEL, D_MODEL, 512, s,
                        l, gbig[2], f"out_proj_dw{l}")
        h_in = sv['h_in']
        (d_u, cs_u, d_bexp, d_cexp, d_dd, d_wglu, d_bglu, d_a2) = s5_bwd(
            dmix, h_in, sv['y0'], sv['hst'], sv['a2'], sv['bexp'], sv['cexp'], vec(s5_d[l]), w_glu_bf, l,
            vec(s5_b_glu[l]), tb_scan, f"s5_bwd{l}")
        (d_vg, cs_vg, d_cvw, d_cvb, d_gg, d_gb, d_wpw, d_bpw) = cv_bwd(
            dmix, h_in, sv['hc'], sv['cvw'], vec(cv_gn_g[l]), vec(cv_gn_b[l]), pmat, w_pw_bf, l, tb_scan, f"cv_bwd{l}")
        (d_lx, cs_lx, d_lcw, d_lcb, d_wr, d_br, d_wi, d_bi, d_sp) = lru_bwd(
            dmix, h_in, sv['xcs'], sv['hls'], sv['lcw'], sv['wr'], vec(lru_b_r[l]), sv['wi'], vec(lru_b_i[l]),
            sv['sp'], tb_scan, f"lru_bwd{l}")
        d_q, cs_q, d_kv = attn_bwd(dmix, h_in, sv['kv'], tb_attn, f"attn_bwd{l}")
        g_s5 = sv['s5_vjp']((d_a2, d_bexp, d_cexp))
        for nm, gval in zip(['s5_lam_re', 's5_lam_im', 's5_log_dt', 's5_b_re', 's5_b_im', 's5_c_re', 's5_c_im'], g_s5):
            per_layer[nm][l] = gval
        g_lru = sv['lru_vjp']((d_wr, d_wi, d_sp))
        for nm, gval in zip(['lru_w_r', 'lru_w_i', 'lru_lam'], g_lru):
            per_layer[nm][l] = gval
        per_layer['s5_d'][l], per_layer['s5_w_glu'][l], per_layer['s5_b_glu'][l] = d_dd[0], d_wglu, d_bglu[0]
        per_layer['cv_w'][l], per_layer['cv_b'][l] = d_cvw[0:CONV_WIDTH], d_cvb[0]
        per_layer['cv_gn_g'][l], per_layer['cv_gn_b'][l] = d_gg[0], d_gb[0]
        per_layer['cv_w_pw'][l], per_layer['cv_b_pw'][l] = d_wpw, d_bpw[0]
        per_layer['lru_conv_w'][l], per_layer['lru_conv_b'][l] = d_lcw[0:LRU_CONV_WIDTH], d_lcb[0]
        per_layer['lru_b_r'][l], per_layer['lru_b_i'][l] = d_br[0], d_bi[0]
        per_layer['b_in'][l] = jnp.concatenate([cs_u, cs_vg, cs_lx, cs_q], axis=1)[0]
        gbig[1] = mm_tn(mems, d_kv, (MEM_ROWS, 2 * D_GROUP), lambda kt, j, st: (st, 0), 1, 2 * D_GROUP, D_MODEL, MEM_ROWS,
                        MEM_ROWS, l, gbig[1], f"kv_proj_dw{l}")
        dh_in = jnp.concatenate([d_u, d_vg, d_lx, d_q], axis=1)
        n_sh = N_IN // N_CHIPS
        dxp = mm_nt(dh_in, (tm_nt, n_sh), lambda i, rt, j: (i, j), g_w_in, l, D_MODEL, dr1, F32, 512, s, f"in_proj_dx{l}")
        gbig[0] = mm_tn(pv['xb'], dh_in, (ts_tn, n_sh), lambda kt, j, st: (st, j), N_CHIPS, n_sh, D_MODEL, 512, s,
                        l, gbig[0], f"in_proj_dw{l}")
        dx = dxp
    grad_x, dg_in, db_in, _ = ln_bwd(dx, xh0, rs0, vec(ln_in_g), "ln_in_bwd")
    grads['ln_in_g'], grads['ln_in_b'] = dg_in[0], db_in[0]
    for nm, vals in per_layer.items():
        if nm not in BIG:
            grads[nm] = jnp.stack(vals)

    small_names = [nm for nm in WEIGHTS if nm not in BIG]
    small_local = _pack_rows([grads[nm] for nm in small_names], LANES)
    gbig = [g.reshape((DEPTH, N_CHIPS) + p[nm].shape[1:]) for g, nm in zip(gbig, BIG)]
    c1 = cidx.reshape(1).astype(jnp.int32)
    jc = jnp.stack([jidx, cidx]).astype(jnp.int32)
    *got_a, small_all = exchange_layers(gbig, small_local)
    s1 = [add_layer_halves(g, ga, c1, f"add_cores_{nm}") for g, ga, nm in zip(gbig, got_a, BIG)]
    got_b = scatter_shards(s1)
    mine_r = [add_chip_parts(sk, gb, jc, f"add_chips_{nm}") for sk, gb, nm in zip(s1, got_b, BIG)]
    red_big = share_with_sibling(mine_r)
    small_red = sum_devices(small_all)
    small_grads = dict(zip(small_names, _unpack_rows(small_red, [grads[nm].shape for nm in small_names], LANES)))

    out_g, out_d, out_m, out_v = {}, {}, {}, {}
    for k, nm in enumerate(BIG):
        gk = red_big[k]
        two_d = (-1, p[nm].shape[-1])
        res = adamw(p[nm].reshape(two_d), gk.reshape(two_d), p['m_' + nm].reshape(two_d),
                    p['v_' + nm].reshape(two_d), f"adamw_{nm}")
        out_d[nm], out_m[nm], out_v[nm], out_g[nm] = (t.reshape(p[nm].shape) for t in res)
    own = {}
    for nm in small_names:
        gfull = small_grads[nm]
        if nm in SMALL_SHARDED:
            ax = SMALL_SHARDED[nm]
            width = p[nm].shape[ax]
            gfull = lax.dynamic_slice_in_dim(gfull, jidx * width, width, axis=ax)
        own[nm] = gfull
    packs = [_pack_rows([src[nm] for nm in small_names], LANES)
             for src in (dict((nm, p[nm]) for nm in small_names), own,
                         dict((nm, p['m_' + nm]) for nm in small_names), dict((nm, p['v_' + nm]) for nm in small_names))]
    dlt, nm_, nv_, _ = adamw(packs[0], packs[1], packs[2], packs[3], "adamw_small")
    shapes = [p[nm].shape for nm in small_names]
    for dst, packed in ((out_d, dlt), (out_m, nm_), (out_v, nv_)):
        dst.update(zip(small_names, _unpack_rows(packed, shapes, LANES)))
    out_g.update(own)

    loss = lax.psum(loss_local, ("x", "y", "c"))
    return (loss, grad_x[None], *[out_g[nm] for nm in WEIGHTS], *[out_d[nm] for nm in WEIGHTS],
            *[out_m[nm] for nm in WEIGHTS], *[out_v[nm] for nm in WEIGHTS])
```

```python
import functools
import math

import jax
import jax.numpy as jnp
from jax import lax
from jax.experimental import pallas as pl
from jax.experimental.pallas import tpu as pltpu

F32 = jnp.float32
BF16 = jnp.bfloat16
MESH = pl.DeviceIdType.MESH
ANY = pl.BlockSpec(memory_space=pl.ANY)

DEPTH = 2
D_MODEL = 1024
D_GROUP = 256
N_IN = 6 * D_GROUP
D_FF = 2816
N_STATE = 1024
CONV_WIDTH = 31
LRU_CONV_WIDTH = 4
FFN_CONV_WIDTH = 3
LRU_C = 8.0
ALPHA = (2 * DEPTH) ** 0.25
LN_EPS = 1e-5
N_CHIPS = 4
MEM_ROWS = 256
LANES = 128
SUBLANES = 8
VMEM_LIMIT = 56 * 1024 * 1024

ADAM_LR, ADAM_B1, ADAM_B2, ADAM_EPS, ADAM_WD, ADAM_STEP = 0.001, 0.9, 0.999, 1e-08, 0.01, 10

WEIGHTS = ['ln_in_g', 'ln_in_b', 'w_in', 'b_in', 's5_lam_re', 's5_lam_im', 's5_log_dt', 's5_b_re', 's5_b_im',
           's5_c_re', 's5_c_im', 's5_d', 's5_w_glu', 's5_b_glu', 'cv_w', 'cv_b', 'cv_gn_g', 'cv_gn_b', 'cv_w_pw',
           'cv_b_pw', 'lru_conv_w', 'lru_conv_b', 'lru_w_r', 'lru_b_r', 'lru_w_i', 'lru_b_i', 'lru_lam',
           'attn_w_kv', 'w_out', 'b_out', 'ln1_g', 'ln1_b', 'ffn_w_up', 'ffn_conv_w', 'ffn_conv_b', 'ffn_w_down',
           'ln2_g', 'ln2_b']
BIG = ['w_in', 'attn_w_kv', 'w_out', 'ffn_w_up', 'ffn_w_down']
SMALL_SHARDED = {'s5_w_glu': 1, 'cv_w': 2, 'cv_w_pw': 1, 'lru_conv_w': 2, 'ffn_conv_w': 2}


def _cparams(n_axes):
    return pltpu.CompilerParams(dimension_semantics=("arbitrary",) * n_axes, vmem_limit_bytes=VMEM_LIMIT)


def _dot(a, b):
    return jnp.dot(a, b, preferred_element_type=F32)


def _dot_nt(a, b):
    return lax.dot_general(a, b, (((1,), (1,)), ((), ())), preferred_element_type=F32)


def _dot_tn(a, b):
    return lax.dot_general(a, b, (((0,), (0,)), ((), ())), preferred_element_type=F32)


def _bf(v):
    return v.astype(BF16)


def _colsum(v):
    return jnp.sum(v, axis=0, keepdims=True)


def _dot3(v, p):
    hi = _bf(v)
    r1 = v - hi.astype(F32)
    mid = _bf(r1)
    lo = _bf(r1 - mid.astype(F32))
    return _dot(hi, p) + _dot(mid, p) + _dot(lo, p)


_GELU_C = math.sqrt(2.0 / math.pi)


_GELU_C3 = _GELU_C * 0.044715


def _gelu_parts(v):
    t = jnp.tanh(v * (_GELU_C + _GELU_C3 * (v * v)))
    hv = 0.5 * v
    return hv + hv * t, t


def _gelu(v):
    return _gelu_parts(v)[0]


def _gelu_grad(v, t):
    return (0.5 + 0.5 * t) + (0.5 * v) * (1.0 - t * t) * (_GELU_C + (3.0 * _GELU_C3) * (v * v))


def _sigmoid(v):
    return 1.0 / (1.0 + jnp.exp(-v))


def _acc(ref, val, first):
    @pl.when(first)
    def _():
        ref[...] = val

    @pl.when(jnp.logical_not(first))
    def _():
        ref[...] += val


def _rows(shape):
    return lax.broadcasted_iota(jnp.int32, shape, 0)


def _ln_rows(r):
    mu = jnp.mean(r, -1, keepdims=True)
    rc = r - mu
    var = jnp.mean(rc * rc, -1, keepdims=True)
    rs = lax.rsqrt(var + LN_EPS)
    return rc * rs, rs


def ln_fwd(x, g, b, name):
    s = x.shape[0]
    tm = min(512, s)

    def body(x_ref, g_ref, b_ref, xh_ref, rs_ref, xb_ref):
        xh, rs = _ln_rows(x_ref[...])
        xh_ref[...] = xh
        rs_ref[...] = rs
        xb_ref[...] = _bf(xh * g_ref[...] + b_ref[...])

    row = pl.BlockSpec((tm, D_MODEL), lambda i: (i, 0))
    vec = pl.BlockSpec((1, D_MODEL), lambda i: (0, 0))
    return pl.pallas_call(
        body, name=name, grid=(s // tm,),
        in_specs=[row, vec, vec],
        out_specs=[row, pl.BlockSpec((tm, 1), lambda i: (i, 0)), row],
        out_shape=[jax.ShapeDtypeStruct((s, D_MODEL), F32), jax.ShapeDtypeStruct((s, 1), F32),
                   jax.ShapeDtypeStruct((s, D_MODEL), BF16)],
        compiler_params=_cparams(1),
    )(x, g, b)


def proj_ln(a, w, layer, bias, xh_prev, g_prev, b_prev, g, b, name):
    s, k = a.shape
    tm = min(512, s)

    def body(a_ref, w_ref, bias_ref, xp_ref, gp_ref, bp_ref, g_ref, b_ref, xh_ref, rs_ref, xb_ref):
        acc = _dot(a_ref[...], w_ref[...]) + bias_ref[...]
        r = ALPHA * (xp_ref[...] * gp_ref[...] + bp_ref[...]) + acc
        xh, rs = _ln_rows(r)
        xh_ref[...] = xh
        rs_ref[...] = rs
        xb_ref[...] = _bf(xh * g_ref[...] + b_ref[...])

    row = pl.BlockSpec((tm, D_MODEL), lambda i: (i, 0))
    vec = pl.BlockSpec((1, D_MODEL), lambda i: (0, 0))
    return pl.pallas_call(
        body, name=name, grid=(s // tm,),
        in_specs=[pl.BlockSpec((tm, k), lambda i: (i, 0)),
                  pl.BlockSpec((None, k, D_MODEL), lambda i: (layer, 0, 0)),
                  vec, row, vec, vec, vec, vec],
        out_specs=[row, pl.BlockSpec((tm, 1), lambda i: (i, 0)), row],
        out_shape=[jax.ShapeDtypeStruct((s, D_MODEL), F32), jax.ShapeDtypeStruct((s, 1), F32),
                   jax.ShapeDtypeStruct((s, D_MODEL), BF16)],
        compiler_params=_cparams(1),
    )(a, w, bias, xh_prev, g_prev, b_prev, g, b)


def ln_bwd(dy, xh, rs, g, name):
    s = xh.shape[0]
    tm = min(512, s)

    def body(dy_ref, xh_ref, rs_ref, g_ref, dr_ref, dg_ref, db_ref, cs_ref):
        first = pl.program_id(0) == 0
        dyv = dy_ref[...]
        xhv = xh_ref[...]
        dxh = dyv * g_ref[...]
        dr = rs_ref[...] * (dxh - jnp.mean(dxh, -1, keepdims=True) - xhv * jnp.mean(dxh * xhv, -1, keepdims=True))
        dr_ref[...] = dr
        _acc(dg_ref, _colsum(dyv * xhv), first)
        _acc(db_ref, _colsum(dyv), first)
        _acc(cs_ref, _colsum(dr), first)

    row = pl.BlockSpec((tm, D_MODEL), lambda i: (i, 0))
    vec = pl.BlockSpec((1, D_MODEL), lambda i: (0, 0))
    vshape = jax.ShapeDtypeStruct((1, D_MODEL), F32)
    return pl.pallas_call(
        body, name=name, grid=(s // tm,),
        in_specs=[row, row, pl.BlockSpec((tm, 1), lambda i: (i, 0)), vec],
        out_specs=[row, vec, vec, vec],
        out_shape=[jax.ShapeDtypeStruct((s, D_MODEL), F32), vshape, vshape, vshape],
        compiler_params=_cparams(1),
    )(dy, xh, rs, g)


def loss_ln_bwd(target, xh, rs, g, b, name):
    s = xh.shape[0]
    tm = min(512, s)

    def body(t_ref, xh_ref, rs_ref, g_ref, b_ref, dr_ref, dg_ref, db_ref, sq_ref):
        first = pl.program_id(0) == 0
        xhv = xh_ref[...]
        err = xhv * g_ref[...] + b_ref[...] - t_ref[...]
        dyv = err * (1.0 / D_MODEL)
        dxh = dyv * g_ref[...]
        dr = rs_ref[...] * (dxh - jnp.mean(dxh, -1, keepdims=True) - xhv * jnp.mean(dxh * xhv, -1, keepdims=True))
        dr_ref[...] = dr
        _acc(dg_ref, _colsum(dyv * xhv), first)
        _acc(db_ref, _colsum(dyv), first)
        _acc(sq_ref, _colsum(err * err), first)

    row = pl.BlockSpec((tm, D_MODEL), lambda i: (i, 0))
    vec = pl.BlockSpec((1, D_MODEL), lambda i: (0, 0))
    vshape = jax.ShapeDtypeStruct((1, D_MODEL), F32)
    return pl.pallas_call(
        body, name=name, grid=(s // tm,),
        in_specs=[row, row, pl.BlockSpec((tm, 1), lambda i: (i, 0)), vec, vec],
        out_specs=[row, vec, vec, vec],
        out_shape=[jax.ShapeDtypeStruct((s, D_MODEL), F32), vshape, vshape, vshape],
        compiler_params=_cparams(1),
    )(target, xh, rs, g, b)


def mm_nn(a, w, layer, bias, out_dtype, tm, name):
    m, k = a.shape
    _, nj, _, n = w.shape
    tm = min(tm, m)

    def body(a_ref, w_ref, b_ref, o_ref):
        o_ref[...] = (_dot(_bf(a_ref[...]), w_ref[...]) + b_ref[...]).astype(out_dtype)

    return pl.pallas_call(
        body, name=name, grid=(nj, m // tm),
        in_specs=[pl.BlockSpec((tm, k), lambda j, i: (i, 0)),
                  pl.BlockSpec((None, None, k, n), lambda j, i: (layer, j, 0, 0)),
                  pl.BlockSpec((1, n), lambda j, i: (0, j))],
        out_specs=pl.BlockSpec((tm, n), lambda j, i: (i, j)),
        out_shape=jax.ShapeDtypeStruct((m, nj * n), out_dtype),
        compiler_params=_cparams(2),
    )(a, w, bias)


def mm_nt(a, a_block, a_map, pick, w, layer, add, out_dtype, tm, m, name):
    _, nj, r, n = w.shape
    tm = min(tm, m)
    has_add = add is not None

    def body(*refs):
        if has_add:
            a_ref, w_ref, add_ref, o_ref = refs
        else:
            a_ref, w_ref, o_ref = refs
        res = _dot_nt(_bf(pick(a_ref, 0)), w_ref[0])
        for j in range(1, nj):
            res = res + _dot_nt(_bf(pick(a_ref, j)), w_ref[j])
        if has_add:
            res = res + ALPHA * add_ref[...]
        o_ref[...] = res.astype(out_dtype)

    in_specs = [pl.BlockSpec(a_block, a_map),
                pl.BlockSpec((None, nj, r, n), lambda i: (layer, 0, 0, 0))]
    ops = [a, w]
    if has_add:
        in_specs.append(pl.BlockSpec((tm, r), lambda i: (i, 0)))
        ops.append(add)
    return pl.pallas_call(
        body, name=name, grid=(m // tm,),
        in_specs=in_specs,
        out_specs=pl.BlockSpec((tm, r), lambda i: (i, 0)),
        out_shape=jax.ShapeDtypeStruct((m, r), out_dtype),
        compiler_params=_cparams(1),
    )(*ops)


def mm_tn(a, b, b_block, b_map, nj, n, tk, ts, s, layer, into, name):
    kx = a.shape[1]
    ts = min(ts, s)

    def body(a_ref, b_ref, *rest):
        o_ref = rest[-1]
        part = _dot_tn(_bf(a_ref[...]), _bf(b_ref[...]))
        _acc(o_ref, part, pl.program_id(2) == 0)

    in_specs = [pl.BlockSpec((ts, tk), lambda kt, j, st: (st, kt)), pl.BlockSpec(b_block, b_map)]
    ops = [a, b]
    aliases = {}
    if into is not None:
        in_specs.append(ANY)
        ops.append(into)
        aliases = {2: 0}
    return pl.pallas_call(
        body, name=name, grid=(kx // tk, nj, s // ts),
        in_specs=in_specs,
        out_specs=pl.BlockSpec((None, None, tk, n), lambda kt, j, st: (layer, j, kt, 0)),
        out_shape=jax.ShapeDtypeStruct((DEPTH, nj, kx, n), F32),
        input_output_aliases=aliases,
        compiler_params=_cparams(3),
    )(*ops)


S5_TAB_ROWS = 8 * SUBLANES


def _s5_scan_table(tab_ref, ar, ai, reverse):
    n = N_STATE
    row = _rows((SUBLANES, n))
    edge = SUBLANES - 1 if reverse else 0
    tab_ref[0:8, :] = jnp.where(row == edge, ar, 0.0)
    tab_ref[8:16, :] = jnp.where(row == edge, ai, 0.0)
    pr, pi = ar, ai
    for step, k in enumerate((1, 2, 4)):
        mask = row < SUBLANES - k if reverse else row >= k
        tab_ref[16 + 16 * step:24 + 16 * step, :] = jnp.where(mask, pr, 0.0)
        tab_ref[24 + 16 * step:32 + 16 * step, :] = jnp.where(mask, pi, 0.0)
        pr, pi = pr * pr - pi * pi, 2.0 * pr * pi


def _s5_scan(src_ref, dst_ref, tab_ref, edge_ref, tb, reverse, per_tile=None):
    n = N_STATE
    ng = tb // SUBLANES
    nq = n // LANES
    link = SUBLANES - 1 if reverse else 1

    def tile(ii, carry):
        g = ng - 1 - ii if reverse else ii
        rows = pl.ds(pl.multiple_of(g * SUBLANES, SUBLANES), SUBLANES)
        out = []
        for q in range(nq):
            cre = slice(q * LANES, (q + 1) * LANES)
            cim = slice(n + q * LANES, n + (q + 1) * LANES)
            lr, li = src_ref[rows, cre], src_ref[rows, cim]
            tr, ti = pltpu.roll(carry[2 * q], link, 0), pltpu.roll(carry[2 * q + 1], link, 0)
            kr, ki = tab_ref[0:8, cre], tab_ref[8:16, cre]
            lr, li = lr + kr * tr - ki * ti, li + kr * ti + ki * tr
            for step, k in enumerate((1, 2, 4)):
                amt = SUBLANES - k if reverse else k
                kr, ki = tab_ref[16 + 16 * step:24 + 16 * step, cre], tab_ref[24 + 16 * step:32 + 16 * step, cre]
                sr, si = pltpu.roll(lr, amt, 0), pltpu.roll(li, amt, 0)
                lr, li = lr + kr * sr - ki * si, li + kr * si + ki * sr
            dst_ref[rows, cre] = lr
            dst_ref[rows, cim] = li
            if per_tile is not None:
                per_tile(g, q, (cre, cim), lr, li)
            out += [lr, li]
        return tuple(out)

    init = []
    for q in range(nq):
        init += [edge_ref[:, q * LANES:(q + 1) * LANES], edge_ref[:, n + q * LANES:n + (q + 1) * LANES]]
    fin = lax.fori_loop(0, ng, tile, tuple(init))
    for q in range(nq):
        edge_ref[:, q * LANES:(q + 1) * LANES] = fin[2 * q]
        edge_ref[:, n + q * LANES:n + (q + 1) * LANES] = fin[2 * q + 1]


def s5_fwd(h_in, a2, bexp, cexp, dskip, wglu, layer, bglu, tb, name):
    s = h_in.shape[0]
    n = N_STATE

    def body(u_ref, a_ref, b_ref, c_ref, d_ref, w_ref, bg_ref, y_ref, h_ref, y0_ref, edge, tab, bu_ref):
        @pl.when(pl.program_id(0) == 0)
        def _():
            edge[...] = jnp.zeros_like(edge)
            _s5_scan_table(tab, a_ref[0:1, 0:n], a_ref[0:1, n:2 * n], False)

        u = u_ref[...]
        bu_ref[...] = _dot(_bf(u), b_ref[...])
        _s5_scan(bu_ref, h_ref, tab, edge, tb, False)
        y0 = _dot(_bf(h_ref[:, 0:n]), c_ref[0:n, :]) + _dot(_bf(h_ref[:, n:2 * n]), c_ref[n:2 * n, :]) + d_ref[...] * u
        y0_ref[...] = y0
        yg = _gelu(y0)
        z = _dot(_bf(yg), w_ref[...]) + bg_ref[...]
        y_ref[...] = _bf(yg * _sigmoid(z))

    vec = pl.BlockSpec((1, D_GROUP), lambda i: (0, 0))
    return pl.pallas_call(
        body, name=name, grid=(s // tb,),
        in_specs=[pl.BlockSpec((tb, D_GROUP), lambda i: (i, 0)),
                  pl.BlockSpec((1, 2 * n), lambda i: (0, 0)),
                  pl.BlockSpec((D_GROUP, 2 * n), lambda i: (0, 0)),
                  pl.BlockSpec((2 * n, D_GROUP), lambda i: (0, 0)),
                  vec,
                  pl.BlockSpec((None, D_GROUP, D_GROUP), lambda i: (layer, 0, 0)),
                  vec],
        out_specs=[pl.BlockSpec((tb, D_GROUP), lambda i: (i, 0)),
                   pl.BlockSpec((tb, 2 * n), lambda i: (i, 0)),
                   pl.BlockSpec((tb, D_GROUP), lambda i: (i, 0))],
        out_shape=[jax.ShapeDtypeStruct((s, D_GROUP), BF16), jax.ShapeDtypeStruct((s, 2 * n), F32),
                   jax.ShapeDtypeStruct((s, D_GROUP), F32)],
        scratch_shapes=[pltpu.VMEM((SUBLANES, 2 * n), F32), pltpu.VMEM((S5_TAB_ROWS, n), F32),
                        pltpu.VMEM((tb, 2 * n), F32)],
        compiler_params=_cparams(1),
    )(h_in, a2, bexp, cexp, dskip, wglu, bglu)


def s5_bwd(dmix, h_in, y0, hst, a2, bexp, cexp, dskip, wglu, layer, bglu, tb, name):
    s = h_in.shape[0]
    n = N_STATE
    nb = s // tb
    halo = tb // 8

    def body(dy_ref, u_ref, y0_ref, h_ref, hp_ref, a_ref, b_ref, c_ref, d_ref, w_ref, bg_ref,
             du_ref, cs_ref, db_ref, dc_ref, dd_ref, dw_ref, dbg_ref, da_ref, edge, tab, g_ref, da_acc):
        i = pl.program_id(0)
        first = i == 0

        @pl.when(first)
        def _():
            edge[...] = jnp.zeros_like(edge)
            da_acc[...] = jnp.zeros_like(da_acc)
            _s5_scan_table(tab, a_ref[0:1, 0:n], -a_ref[0:1, n:2 * n], True)

        dy = dy_ref[...]
        u = u_ref[...]
        y0v = y0_ref[...]
        yg, t = _gelu_parts(y0v)
        z = _dot(_bf(yg), w_ref[...]) + bg_ref[...]
        sg = _sigmoid(z)
        dz = dy * yg * sg * (1.0 - sg)
        dyg = dy * sg + _dot_nt(_bf(dz), w_ref[...])
        _acc(dw_ref, _dot_tn(_bf(yg), _bf(dz)), first)
        _acc(dbg_ref, _colsum(dz), first)
        dy0 = dyg * _gelu_grad(y0v, t)
        _acc(dd_ref, _colsum(dy0 * u), first)
        dy0b = _bf(dy0)
        _acc(dc_ref.at[0:n, :], _dot_tn(_bf(h_ref[:, 0:n]), dy0b), first)
        _acc(dc_ref.at[n:2 * n, :], _dot_tn(_bf(h_ref[:, n:2 * n]), dy0b), first)
        g_ref[...] = _dot_nt(dy0b, c_ref[...])
        keep = jnp.where(i == nb - 1, 0.0, 1.0)
        row0 = _rows((SUBLANES, LANES)) == 0

        def grad_a(g, q, cols, gr, gi):
            cre, cim = cols
            rows = pl.ds(pl.multiple_of(g * SUBLANES, SUBLANES), SUBLANES)
            before = pl.ds(pl.multiple_of(jnp.maximum(g - 1, 0) * SUBLANES, SUBLANES), SUBLANES)
            pre = jnp.where(g == 0, hp_ref[:, cre] * keep, h_ref[before, cre])
            pim = jnp.where(g == 0, hp_ref[:, cim] * keep, h_ref[before, cim])
            pr = jnp.where(row0, pltpu.roll(pre, 1, 0), pltpu.roll(h_ref[rows, cre], 1, 0))
            pi = jnp.where(row0, pltpu.roll(pim, 1, 0), pltpu.roll(h_ref[rows, cim], 1, 0))
            da_acc[:, cre] += gr * pr + gi * pi
            da_acc[:, cim] += gi * pr - gr * pi

        _s5_scan(g_ref, g_ref, tab, edge, tb, True, grad_a)
        da_ref[...] = _colsum(da_acc[...])
        gr, gi = g_ref[:, 0:n], g_ref[:, n:2 * n]
        grb, gib = _bf(gr), _bf(gi)
        du = d_ref[...] * dy0 + _dot_nt(grb, b_ref[:, 0:n]) + _dot_nt(gib, b_ref[:, n:2 * n])
        ub = _bf(u)
        _acc(db_ref.at[:, 0:n], _dot_tn(ub, grb), first)
        _acc(db_ref.at[:, n:2 * n], _dot_tn(ub, gib), first)
        du_ref[...] = _bf(du)
        _acc(cs_ref, _colsum(du), first)

    rev = lambda i: (nb - 1 - i, 0)
    vec = pl.BlockSpec((1, D_GROUP), lambda i: (0, 0))
    vshape = jax.ShapeDtypeStruct((1, D_GROUP), F32)
    return pl.pallas_call(
        body, name=name, grid=(nb,),
        in_specs=[pl.BlockSpec((tb, D_GROUP), rev),
                  pl.BlockSpec((tb, D_GROUP), rev),
                  pl.BlockSpec((tb, D_GROUP), rev),
                  pl.BlockSpec((tb, 2 * n), rev),
                  pl.BlockSpec((8, 2 * n), lambda i: (jnp.maximum((nb - 1 - i) * halo - 1, 0), 0)),
                  pl.BlockSpec((1, 2 * n), lambda i: (0, 0)),
                  pl.BlockSpec((D_GROUP, 2 * n), lambda i: (0, 0)),
                  pl.BlockSpec((2 * n, D_GROUP), lambda i: (0, 0)),
                  vec,
                  pl.BlockSpec((None, D_GROUP, D_GROUP), lambda i: (layer, 0, 0)),
                  vec],
        out_specs=[pl.BlockSpec((tb, D_GROUP), rev), vec,
                   pl.BlockSpec((D_GROUP, 2 * n), lambda i: (0, 0)),
                   pl.BlockSpec((2 * n, D_GROUP), lambda i: (0, 0)),
                   vec,
                   pl.BlockSpec((D_GROUP, D_GROUP), lambda i: (0, 0)),
                   vec,
                   pl.BlockSpec((1, 2 * n), lambda i: (0, 0))],
        out_shape=[jax.ShapeDtypeStruct((s, D_GROUP), BF16), vshape,
                   jax.ShapeDtypeStruct((D_GROUP, 2 * n), F32), jax.ShapeDtypeStruct((2 * n, D_GROUP), F32),
                   vshape, jax.ShapeDtypeStruct((D_GROUP, D_GROUP), F32), vshape,
                   jax.ShapeDtypeStruct((1, 2 * n), F32)],
        scratch_shapes=[pltpu.VMEM((SUBLANES, 2 * n), F32), pltpu.VMEM((S5_TAB_ROWS, n), F32),
                        pltpu.VMEM((tb, 2 * n), F32), pltpu.VMEM((SUBLANES, 2 * n), F32)],
        compiler_params=_cparams(1),
    )(dmix, h_in, y0, hst, hst, a2, bexp, cexp, dskip, wglu, bglu)


CV_HALO = 32


def _gn_stats(hc, pmat):
    mu = _dot3(hc, pmat)
    xc = hc - mu
    var = _dot3(xc * xc, pmat)
    rstd = lax.rsqrt(var + LN_EPS)
    return xc * rstd, rstd


def cv_fwd(h_in, cw, cb, gg, gb, pmat, wpw, layer, bpw, tb, name):
    s = h_in.shape[0]
    hl = CV_HALO

    def body(v_ref, g_ref, cw_ref, cb_ref, gg_ref, gb_ref, p_ref, w_ref, bw_ref, y_ref, hc_ref, ext):
        @pl.when(pl.program_id(0) == 0)
        def _():
            ext[0:hl, :] = jnp.zeros((hl, D_GROUP), F32)

        ext[hl:hl + tb, :] = v_ref[...] * _sigmoid(g_ref[...])
        acc = jnp.zeros((tb, D_GROUP), F32) + cb_ref[...]
        for k in range(CONV_WIDTH):
            off = hl - (CONV_WIDTH - 1) + k
            acc = acc + cw_ref[k:k + 1, :] * ext[off:off + tb, :]
        hc_ref[...] = acc
        ext[0:hl, :] = ext[tb:tb + hl, :]
        xn, _ = _gn_stats(acc, p_ref[...])
        hn = xn * gg_ref[...] + gb_ref[...]
        hs = hn * _sigmoid(hn)
        y_ref[...] = _bf(_dot(_bf(hs), w_ref[...]) + bw_ref[...])

    vec = pl.BlockSpec((1, D_GROUP), lambda i: (0, 0))
    sq = pl.BlockSpec((D_GROUP, D_GROUP), lambda i: (0, 0))
    return pl.pallas_call(
        body, name=name, grid=(s // tb,),
        in_specs=[pl.BlockSpec((tb, D_GROUP), lambda i: (i, 1)),
                  pl.BlockSpec((tb, D_GROUP), lambda i: (i, 2)),
                  pl.BlockSpec((hl, D_GROUP), lambda i: (0, 0)),
                  vec, vec, vec, sq,
                  pl.BlockSpec((None, D_GROUP, D_GROUP), lambda i: (layer, 0, 0)),
                  vec],
        out_specs=[pl.BlockSpec((tb, D_GROUP), lambda i: (i, 0)), pl.BlockSpec((tb, D_GROUP), lambda i: (i, 0))],
        out_shape=[jax.ShapeDtypeStruct((s, D_GROUP), BF16), jax.ShapeDtypeStruct((s, D_GROUP), F32)],
        scratch_shapes=[pltpu.VMEM((hl + tb, D_GROUP), F32)],
        compiler_params=_cparams(1),
    )(h_in, h_in, cw, cb, gg, gb, pmat, wpw, bpw)


def cv_bwd(dmix, h_in, hc, cw, gg, gb, pmat, wpw, layer, tb, name):
    s = h_in.shape[0]
    hl = CV_HALO
    nb = s // tb
    per = tb // hl

    def body(dy_ref, v_ref, g_ref, vh_ref, gh_ref, hc_ref, cw_ref, gg_ref, gb_ref, p_ref, w_ref,
             dvg_ref, cs_ref, dcw_ref, dcb_ref, dgg_ref, dgb_ref, dw_ref, dbw_ref, ext, dext, head):
        i = pl.program_id(0)
        first = i == 0

        @pl.when(first)
        def _():
            head[...] = jnp.zeros_like(head)

        dy = dy_ref[...]
        pm = p_ref[...]
        xn, rstd = _gn_stats(hc_ref[...], pm)
        hn = xn * gg_ref[...] + gb_ref[...]
        sg = _sigmoid(hn)
        hs = hn * sg
        dyb = _bf(dy)
        _acc(dbw_ref, _colsum(dy), first)
        _acc(dw_ref, _dot_tn(_bf(hs), dyb), first)
        dhs = _dot_nt(dyb, w_ref[...])
        dhn = dhs * sg * (1.0 + hn * (1.0 - sg))
        _acc(dgg_ref, _colsum(dhn * xn), first)
        _acc(dgb_ref, _colsum(dhn), first)
        dxn = dhn * gg_ref[...]
        dhc = rstd * (dxn - _dot3(dxn, pm) - xn * _dot3(dxn * xn, pm))
        _acc(dcb_ref, _colsum(dhc), first)
        v = v_ref[...]
        sgg = _sigmoid(g_ref[...])
        keep = jnp.where(i == nb - 1, 0.0, 1.0)
        ext[0:hl, :] = vh_ref[...] * _sigmoid(gh_ref[...]) * keep
        ext[hl:hl + tb, :] = v * sgg
        dext[0:tb, :] = dhc
        dext[tb:tb + hl, :] = head[...]
        head[...] = dhc[0:hl]
        dhg = jnp.zeros((tb, D_GROUP), F32)
        for k in range(CONV_WIDTH):
            off = hl - (CONV_WIDTH - 1) + k
            wk = _colsum(dhc * ext[off:off + tb, :])
            _acc(dcw_ref.at[k:k + 1, :], wk, first)
            back = CONV_WIDTH - 1 - k
            dhg = dhg + cw_ref[k:k + 1, :] * dext[back:back + tb, :]

        @pl.when(first)
        def _():
            dcw_ref[CONV_WIDTH:hl, :] = jnp.zeros((hl - CONV_WIDTH, D_GROUP), F32)

        dv = dhg * sgg
        dg = dhg * v * sgg * (1.0 - sgg)
        dvg_ref[:, 0:D_GROUP] = _bf(dv)
        dvg_ref[:, D_GROUP:2 * D_GROUP] = _bf(dg)
        _acc(cs_ref.at[:, 0:D_GROUP], _colsum(dv), first)
        _acc(cs_ref.at[:, D_GROUP:2 * D_GROUP], _colsum(dg), first)

    vec = pl.BlockSpec((1, D_GROUP), lambda i: (0, 0))
    sq = pl.BlockSpec((D_GROUP, D_GROUP), lambda i: (0, 0))
    tap = pl.BlockSpec((hl, D_GROUP), lambda i: (0, 0))
    vshape = jax.ShapeDtypeStruct((1, D_GROUP), F32)

    def blk(col):
        return pl.BlockSpec((tb, D_GROUP), lambda i: (nb - 1 - i, col))

    def halo_blk(col):
        return pl.BlockSpec((hl, D_GROUP), lambda i: (jnp.maximum((nb - 1 - i) * per - 1, 0), col))

    return pl.pallas_call(
        body, name=name, grid=(nb,),
        in_specs=[blk(1), blk(1), blk(2), halo_blk(1), halo_blk(2),
                  pl.BlockSpec((tb, D_GROUP), lambda i: (nb - 1 - i, 0)),
                  tap, vec, vec, sq,
                  pl.BlockSpec((None, D_GROUP, D_GROUP), lambda i: (layer, 0, 0))],
        out_specs=[pl.BlockSpec((tb, 2 * D_GROUP), lambda i: (nb - 1 - i, 0)),
                   pl.BlockSpec((1, 2 * D_GROUP), lambda i: (0, 0)),
                   tap, vec, vec, vec, sq, vec],
        out_shape=[jax.ShapeDtypeStruct((s, 2 * D_GROUP), BF16), jax.ShapeDtypeStruct((1, 2 * D_GROUP), F32),
                   jax.ShapeDtypeStruct((hl, D_GROUP), F32), vshape, vshape, vshape,
                   jax.ShapeDtypeStruct((D_GROUP, D_GROUP), F32), vshape],
        scratch_shapes=[pltpu.VMEM((hl + tb, D_GROUP), F32), pltpu.VMEM((tb + hl, D_GROUP), F32),
                        pltpu.VMEM((hl, D_GROUP), F32)],
        compiler_params=_cparams(1),
    )(dmix, h_in, h_in, h_in, h_in, hc, cw, gg, gb, pmat, wpw)


LRU_HALO = 8


def _lru_gates(xc, wr_ref, br_ref, wi_ref, bi_ref, sp_ref):
    xcb = _bf(xc)
    r = _sigmoid(_dot(xcb, wr_ref[...]) + br_ref[...])
    gi = _sigmoid(_dot(xcb, wi_ref[...]) + bi_ref[...])
    la = -LRU_C * r * sp_ref[...]
    a = jnp.exp(la)
    e2 = a * a
    sq = jnp.sqrt(-jnp.tanh(la) * (e2 + 1.0))
    return r, gi, a, e2, sq


def _rscan(a, b, tb, reverse):
    row = _rows(a.shape)
    sh = 1
    while sh < tb:
        if reverse:
            amt, mask = tb - sh, row < tb - sh
        else:
            amt, mask = sh, row >= sh
        a_s = jnp.where(mask, pltpu.roll(a, amt, 0), 1.0)
        b_s = jnp.where(mask, pltpu.roll(b, amt, 0), 0.0)
        b = b + a * b_s
        a = a * a_s
        sh *= 2
    return a, b


def lru_fwd(h_in, cw, cb, wr, br, wi, bi, sp, tb, name):
    s = h_in.shape[0]
    hl = LRU_HALO

    def body(xg_ref, xr_ref, cw_ref, cb_ref, wr_ref, br_ref, wi_ref, bi_ref, sp_ref, y_ref, xc_ref, h_ref, ext, carry):
        @pl.when(pl.program_id(0) == 0)
        def _():
            ext[0:hl, :] = jnp.zeros((hl, D_GROUP), F32)
            carry[...] = jnp.zeros_like(carry)

        ext[hl:hl + tb, :] = xr_ref[...]
        xc = jnp.zeros((tb, D_GROUP), F32) + cb_ref[...]
        for k in range(LRU_CONV_WIDTH):
            off = hl - (LRU_CONV_WIDTH - 1) + k
            xc = xc + cw_ref[k:k + 1, :] * ext[off:off + tb, :]
        xc_ref[...] = xc
        ext[0:hl, :] = ext[tb:tb + hl, :]
        r, gi, a, e2, sq = _lru_gates(xc, wr_ref, br_ref, wi_ref, bi_ref, sp_ref)
        pa, hloc = _rscan(a, sq * (gi * xc), tb, False)
        h = hloc + pa * carry[7:8, :]
        h_ref[...] = h
        carry[...] = h[tb - 8:tb]
        y_ref[...] = _bf(h * _gelu(xg_ref[...]))

    vec = pl.BlockSpec((1, D_GROUP), lambda i: (0, 0))
    sq_spec = pl.BlockSpec((D_GROUP, D_GROUP), lambda i: (0, 0))
    blk = pl.BlockSpec((tb, D_GROUP), lambda i: (i, 0))
    return pl.pallas_call(
        body, name=name, grid=(s // tb,),
        in_specs=[pl.BlockSpec((tb, D_GROUP), lambda i: (i, 3)),
                  pl.BlockSpec((tb, D_GROUP), lambda i: (i, 4)),
                  pl.BlockSpec((hl, D_GROUP), lambda i: (0, 0)),
                  vec, sq_spec, vec, sq_spec, vec, vec],
        out_specs=[blk, blk, blk],
        out_shape=[jax.ShapeDtypeStruct((s, D_GROUP), BF16), jax.ShapeDtypeStruct((s, D_GROUP), F32),
                   jax.ShapeDtypeStruct((s, D_GROUP), F32)],
        scratch_shapes=[pltpu.VMEM((hl + tb, D_GROUP), F32), pltpu.VMEM((8, D_GROUP), F32)],
        compiler_params=_cparams(1),
    )(h_in, h_in, cw, cb, wr, br, wi, bi, sp)


def lru_bwd(dmix, h_in, xcs, hs, cw, wr, br, wi, bi, sp, tb, name):
    s = h_in.shape[0]
    hl = LRU_HALO
    nb = s // tb
    per = tb // hl

    def body(dy_ref, xg_ref, xr_ref, xrh_ref, xc_ref, h_ref, hp_ref, cw_ref, wr_ref, br_ref, wi_ref, bi_ref, sp_ref,
             dx_ref, cs_ref, dcw_ref, dcb_ref, dwr_ref, dbr_ref, dwi_ref, dbi_ref, dsp_ref,
             ext, dext, head, anext, gnext):
        i = pl.program_id(0)
        first = i == 0

        @pl.when(first)
        def _():
            head[...] = jnp.zeros_like(head)
            anext[...] = jnp.zeros_like(anext)
            gnext[...] = jnp.zeros_like(gnext)

        dy = dy_ref[...]
        xg = xg_ref[...]
        xc = xc_ref[...]
        h = h_ref[...]
        r, gi, a, e2, sq = _lru_gates(xc, wr_ref, br_ref, wi_ref, bi_ref, sp_ref)
        gate, t = _gelu_parts(xg)
        dh = dy * gate
        dxg = dy * h * _gelu_grad(xg, t)
        row = _rows((tb, D_GROUP))
        coef = jnp.where(row == tb - 1, anext[0:1, :], pltpu.roll(a, tb - 1, 0))
        pc, gloc = _rscan(coef, dh, tb, True)
        gfull = gloc + pc * gnext[0:1, :]
        anext[...] = a[0:8]
        gnext[...] = gfull[0:8]
        keep = jnp.where(i == nb - 1, 0.0, 1.0)
        hprev = jnp.where(row == 0, hp_ref[7:8, :] * keep, pltpu.roll(h, 1, 0))
        da = gfull * hprev
        uu = gi * xc
        dsq = gfull * uu
        duu = gfull * sq
        dla = da * a - dsq * e2 / sq
        sp = sp_ref[...]
        dr = dla * (-LRU_C) * sp
        _acc(dsp_ref, _colsum(dla * (-LRU_C) * r), first)
        dzr = dr * r * (1.0 - r)
        dzi = duu * xc * gi * (1.0 - gi)
        dzrb, dzib = _bf(dzr), _bf(dzi)
        dxc = duu * gi + _dot_nt(dzrb, wr_ref[...]) + _dot_nt(dzib, wi_ref[...])
        xcb = _bf(xc)
        _acc(dwr_ref, _dot_tn(xcb, dzrb), first)
        _acc(dwi_ref, _dot_tn(xcb, dzib), first)
        _acc(dbr_ref, _colsum(dzr), first)
        _acc(dbi_ref, _colsum(dzi), first)
        _acc(dcb_ref, _colsum(dxc), first)
        ext[0:hl, :] = xrh_ref[...] * keep
        ext[hl:hl + tb, :] = xr_ref[...]
        dext[0:tb, :] = dxc
        dext[tb:tb + hl, :] = head[...]
        head[...] = dxc[0:hl]
        dxr = jnp.zeros((tb, D_GROUP), F32)
        for k in range(LRU_CONV_WIDTH):
            off = hl - (LRU_CONV_WIDTH - 1) + k
            _acc(dcw_ref.at[k:k + 1, :], _colsum(dxc * ext[off:off + tb, :]), first)
            back = LRU_CONV_WIDTH - 1 - k
            dxr = dxr + cw_ref[k:k + 1, :] * dext[back:back + tb, :]

        @pl.when(first)
        def _():
            dcw_ref[LRU_CONV_WIDTH:hl, :] = jnp.zeros((hl - LRU_CONV_WIDTH, D_GROUP), F32)

        dx_ref[:, 0:D_GROUP] = _bf(dxg)
        dx_ref[:, D_GROUP:2 * D_GROUP] = _bf(dxr)
        _acc(cs_ref.at[:, 0:D_GROUP], _colsum(dxg), first)
        _acc(cs_ref.at[:, D_GROUP:2 * D_GROUP], _colsum(dxr), first)

    vec = pl.BlockSpec((1, D_GROUP), lambda i: (0, 0))
    sq_spec = pl.BlockSpec((D_GROUP, D_GROUP), lambda i: (0, 0))
    tap = pl.BlockSpec((hl, D_GROUP), lambda i: (0, 0))
    vshape = jax.ShapeDtypeStruct((1, D_GROUP), F32)
    sshape = jax.ShapeDtypeStruct((D_GROUP, D_GROUP), F32)

    def blk(col):
        return pl.BlockSpec((tb, D_GROUP), lambda i: (nb - 1 - i, col))

    def halo_blk(col):
        return pl.BlockSpec((hl, D_GROUP), lambda i: (jnp.maximum((nb - 1 - i) * per - 1, 0), col))

    return pl.pallas_call(
        body, name=name, grid=(nb,),
        in_specs=[blk(2), blk(3), blk(4), halo_blk(4), blk(0), blk(0), halo_blk(0),
                  tap, sq_spec, vec, sq_spec, vec, vec],
        out_specs=[pl.BlockSpec((tb, 2 * D_GROUP), lambda i: (nb - 1 - i, 0)),
                   pl.BlockSpec((1, 2 * D_GROUP), lambda i: (0, 0)),
                   tap, vec, sq_spec, vec, sq_spec, vec, vec],
        out_shape=[jax.ShapeDtypeStruct((s, 2 * D_GROUP), BF16), jax.ShapeDtypeStruct((1, 2 * D_GROUP), F32),
                   jax.ShapeDtypeStruct((hl, D_GROUP), F32), vshape, sshape, vshape, sshape, vshape, vshape],
        scratch_shapes=[pltpu.VMEM((hl + tb, D_GROUP), F32), pltpu.VMEM((tb + hl, D_GROUP), F32),
                        pltpu.VMEM((hl, D_GROUP), F32), pltpu.VMEM((8, D_GROUP), F32), pltpu.VMEM((8, D_GROUP), F32)],
        compiler_params=_cparams(1),
    )(dmix, h_in, h_in, h_in, xcs, hs, hs, cw, wr, br, wi, bi, sp)


ATTN_HEADS = 4
ATTN_HEAD_DIM = 64
ATTN_SCALE = ATTN_HEAD_DIM ** -0.5


def _head_mask(h):
    lane = lax.broadcasted_iota(jnp.int32, (1, D_GROUP), 1)
    return jnp.where((lane >= h * ATTN_HEAD_DIM) & (lane < (h + 1) * ATTN_HEAD_DIM), 1.0, 0.0)


def _softmax_rows(sc):
    e = jnp.exp(sc - jnp.max(sc, -1, keepdims=True))
    return e / jnp.sum(e, -1, keepdims=True)


def attn_fwd(h_in, kv, tb, name):
    s = h_in.shape[0]

    def body(q_ref, kv_ref, y_ref):
        q = q_ref[...]
        kb = _bf(kv_ref[:, 0:D_GROUP])
        vb = _bf(kv_ref[:, D_GROUP:2 * D_GROUP])
        out = jnp.zeros((tb, D_GROUP), F32)
        for h in range(ATTN_HEADS):
            mask = _head_mask(h)
            p = _softmax_rows(_dot_nt(_bf(q * mask), kb) * ATTN_SCALE)
            out = out + _dot(_bf(p), vb) * mask
        y_ref[...] = _bf(out)

    return pl.pallas_call(
        body, name=name, grid=(s // tb,),
        in_specs=[pl.BlockSpec((tb, D_GROUP), lambda i: (i, 5)),
                  pl.BlockSpec((D_GROUP, 2 * D_GROUP), lambda i: (0, 0))],
        out_specs=pl.BlockSpec((tb, D_GROUP), lambda i: (i, 0)),
        out_shape=jax.ShapeDtypeStruct((s, D_GROUP), BF16),
        compiler_params=_cparams(1),
    )(h_in, kv)


def attn_bwd(dmix, h_in, kv, tb, name):
    s = h_in.shape[0]

    def body(do_ref, q_ref, kv_ref, dq_ref, cs_ref, dkv_ref):
        first = pl.program_id(0) == 0
        q = q_ref[...]
        do = do_ref[...]
        kb = _bf(kv_ref[:, 0:D_GROUP])
        vb = _bf(kv_ref[:, D_GROUP:2 * D_GROUP])
        dq = jnp.zeros((tb, D_GROUP), F32)
        dk = jnp.zeros((D_GROUP, D_GROUP), F32)
        dv = jnp.zeros((D_GROUP, D_GROUP), F32)
        for h in range(ATTN_HEADS):
            mask = _head_mask(h)
            qm = _bf(q * mask)
            p = _softmax_rows(_dot_nt(qm, kb) * ATTN_SCALE)
            dom = _bf(do * mask)
            dp = _dot_nt(dom, vb)
            dv = dv + _dot_tn(_bf(p), dom)
            ds = _bf(p * (dp - jnp.sum(dp * p, -1, keepdims=True)) * ATTN_SCALE)
            dq = dq + _dot(ds, kb) * mask
            dk = dk + _dot_tn(ds, qm)
        dq_ref[...] = _bf(dq)
        _acc(cs_ref, _colsum(dq), first)
        _acc(dkv_ref.at[:, 0:D_GROUP], dk, first)
        _acc(dkv_ref.at[:, D_GROUP:2 * D_GROUP], dv, first)

    return pl.pallas_call(
        body, name=name, grid=(s // tb,),
        in_specs=[pl.BlockSpec((tb, D_GROUP), lambda i: (i, 3)),
                  pl.BlockSpec((tb, D_GROUP), lambda i: (i, 5)),
                  pl.BlockSpec((D_GROUP, 2 * D_GROUP), lambda i: (0, 0))],
        out_specs=[pl.BlockSpec((tb, D_GROUP), lambda i: (i, 0)),
                   pl.BlockSpec((1, D_GROUP), lambda i: (0, 0)),
                   pl.BlockSpec((D_GROUP, 2 * D_GROUP), lambda i: (0, 0))],
        out_shape=[jax.ShapeDtypeStruct((s, D_GROUP), BF16), jax.ShapeDtypeStruct((1, D_GROUP), F32),
                   jax.ShapeDtypeStruct((D_GROUP, 2 * D_GROUP), F32)],
        compiler_params=_cparams(1),
    )(dmix, h_in, kv)


FFN_RB = 16
FFN_UNROLL_FWD = 4
FFN_UNROLL_BWD = 2
FFN_TAP_ROWS = 8
FFN_TN = D_FF // 2


def _shift_down(cur, tail, k):
    return pltpu.roll(jnp.concatenate([tail, cur], axis=0), k, 0)[SUBLANES:]


def _shift_up(cur, head, k):
    rb = cur.shape[0]
    return pltpu.roll(jnp.concatenate([cur, head], axis=0), rb + SUBLANES - k, 0)[:rb]


def _fold8(v):
    tot = v[0:SUBLANES]
    for t in range(1, v.shape[0] // SUBLANES):
        tot = tot + v[t * SUBLANES:(t + 1) * SUBLANES]
    return tot


def _strip(r):
    return pl.ds(pl.multiple_of(r * FFN_RB, FFN_RB), FFN_RB)


def _tail_before(r):
    return pl.ds(pl.multiple_of(jnp.maximum(r * FFN_RB - SUBLANES, 0), SUBLANES), SUBLANES)


def ffn_act_fwd(u, cw, cb, tb, name):
    s = u.shape[0]
    rb = FFN_RB
    nct = D_FF // FFN_TN
    nstrip = tb // rb

    def body(uv_ref, ug_ref, wv_ref, wg_ref, bv_ref, bg_ref, hf_ref, tailv, tailg):
        @pl.when(pl.program_id(1) == 0)
        def _():
            tailv[...] = jnp.zeros_like(tailv)
            tailg[...] = jnp.zeros_like(tailg)

        for cc in range(FFN_TN // LANES):
            cols = slice(cc * LANES, (cc + 1) * LANES)
            wv = [wv_ref[k:k + 1, cols] for k in range(FFN_CONV_WIDTH)]
            wg = [wg_ref[k:k + 1, cols] for k in range(FFN_CONV_WIDTH)]
            bv, bg = bv_ref[:, cols], bg_ref[:, cols]

            def strip(r, carry):
                tail_v, tail_g = carry
                cur_v, cur_g = uv_ref[_strip(r), cols], ug_ref[_strip(r), cols]
                vc = wv[0] * _shift_down(cur_v, tail_v, 2) + wv[1] * _shift_down(cur_v, tail_v, 1) + wv[2] * cur_v + bv
                gc = wg[0] * _shift_down(cur_g, tail_g, 2) + wg[1] * _shift_down(cur_g, tail_g, 1) + wg[2] * cur_g + bg
                hf_ref[_strip(r), cols] = _bf(vc * _gelu(gc))
                return cur_v[rb - SUBLANES:], cur_g[rb - SUBLANES:]

            def strips(q, carry):
                for k in range(FFN_UNROLL_FWD):
                    carry = strip(q * FFN_UNROLL_FWD + k, carry)
                return carry

            last_v, last_g = lax.fori_loop(0, nstrip // FFN_UNROLL_FWD, strips, (tailv[:, cols], tailg[:, cols]))
            tailv[:, cols] = last_v
            tailg[:, cols] = last_g

    return pl.pallas_call(
        body, name=name, grid=(nct, s // tb),
        in_specs=[pl.BlockSpec((tb, FFN_TN), lambda c, i: (i, c)),
                  pl.BlockSpec((tb, FFN_TN), lambda c, i: (i, c + nct)),
                  pl.BlockSpec((FFN_TAP_ROWS, FFN_TN), lambda c, i: (0, c)),
                  pl.BlockSpec((FFN_TAP_ROWS, FFN_TN), lambda c, i: (0, c + nct)),
                  pl.BlockSpec((1, FFN_TN), lambda c, i: (0, c)),
                  pl.BlockSpec((1, FFN_TN), lambda c, i: (0, c + nct))],
        out_specs=pl.BlockSpec((tb, FFN_TN), lambda c, i: (i, c)),
        out_shape=jax.ShapeDtypeStruct((s, D_FF), BF16),
        scratch_shapes=[pltpu.VMEM((SUBLANES, FFN_TN), F32), pltpu.VMEM((SUBLANES, FFN_TN), F32)],
        compiler_params=_cparams(2),
    )(u, u, cw, cw, cb, cb)


def ffn_act_bwd(dhf, u, cw, cb, tb, name):
    s = u.shape[0]
    rb = FFN_RB
    nct = D_FF // FFN_TN
    nb = s // tb
    per = tb // SUBLANES
    nstrip = tb // rb
    ntap = FFN_CONV_WIDTH

    def body(dh_ref, uv_ref, ug_ref, uvh_ref, ugh_ref, wv_ref, wg_ref, bv_ref, bg_ref,
             du_ref, dwv_ref, dwg_ref, headv, headg):
        i = pl.program_id(1)
        first = i == 0

        @pl.when(first)
        def _():
            headv[...] = jnp.zeros_like(headv)
            headg[...] = jnp.zeros_like(headg)
            dwv_ref[...] = jnp.zeros_like(dwv_ref)
            dwg_ref[...] = jnp.zeros_like(dwg_ref)

        keep = jnp.where(i == nb - 1, 0.0, 1.0)
        zero = jnp.zeros((SUBLANES, LANES), F32)
        for cc in range(FFN_TN // LANES):
            cols = slice(cc * LANES, (cc + 1) * LANES)
            wv = [wv_ref[k:k + 1, cols] for k in range(ntap)]
            wg = [wg_ref[k:k + 1, cols] for k in range(ntap)]
            bv, bg = bv_ref[:, cols], bg_ref[:, cols]
            halo_v, halo_g = uvh_ref[:, cols] * keep, ugh_ref[:, cols] * keep

            def strip(ii, carry):
                head_dv, head_dg, acc_v, acc_g = carry
                r = nstrip - 1 - ii
                cur_v, cur_g = uv_ref[_strip(r), cols], ug_ref[_strip(r), cols]
                tail_v = jnp.where(r == 0, halo_v, uv_ref[_tail_before(r), cols])
                tail_g = jnp.where(r == 0, halo_g, ug_ref[_tail_before(r), cols])
                sv = [_shift_down(cur_v, tail_v, 2), _shift_down(cur_v, tail_v, 1), cur_v]
                sg = [_shift_down(cur_g, tail_g, 2), _shift_down(cur_g, tail_g, 1), cur_g]
                vc = wv[0] * sv[0] + wv[1] * sv[1] + wv[2] * sv[2] + bv
                gc = wg[0] * sg[0] + wg[1] * sg[1] + wg[2] * sg[2] + bg
                ge, t = _gelu_parts(gc)
                dh = dh_ref[_strip(r), cols]
                dvc = dh * ge
                dgc = dh * vc * _gelu_grad(gc, t)
                acc_v = tuple(acc_v[k] + _fold8(dvc * sv[k]) for k in range(ntap)) + (acc_v[ntap] + _fold8(dvc),)
                acc_g = tuple(acc_g[k] + _fold8(dgc * sg[k]) for k in range(ntap)) + (acc_g[ntap] + _fold8(dgc),)
                du_v = wv[2] * dvc + wv[1] * _shift_up(dvc, head_dv, 1) + wv[0] * _shift_up(dvc, head_dv, 2)
                du_g = wg[2] * dgc + wg[1] * _shift_up(dgc, head_dg, 1) + wg[0] * _shift_up(dgc, head_dg, 2)
                du_ref[0, _strip(r), cols] = _bf(du_v)
                du_ref[1, _strip(r), cols] = _bf(du_g)
                return dvc[0:SUBLANES], dgc[0:SUBLANES], acc_v, acc_g

            init = (headv[:, cols], headg[:, cols], (zero,) * (ntap + 1), (zero,) * (ntap + 1))
            def strips(q, carry):
                for k in range(FFN_UNROLL_BWD):
                    carry = strip(q * FFN_UNROLL_BWD + k, carry)
                return carry

            top_dv, top_dg, acc_v, acc_g = lax.fori_loop(0, nstrip // FFN_UNROLL_BWD, strips, init)
            headv[:, cols] = top_dv
            headg[:, cols] = top_dg
            for k in range(ntap + 1):
                dwv_ref[k:k + 1, cols] += _colsum(acc_v[k])
                dwg_ref[k:k + 1, cols] += _colsum(acc_g[k])

    def blk(shift):
        return pl.BlockSpec((tb, FFN_TN), lambda c, i: (nb - 1 - i, c + shift))

    def halo_blk(shift):
        return pl.BlockSpec((SUBLANES, FFN_TN), lambda c, i: (jnp.maximum((nb - 1 - i) * per - 1, 0), c + shift))

    tapv = pl.BlockSpec((FFN_TAP_ROWS, FFN_TN), lambda c, i: (0, c))
    tapg = pl.BlockSpec((FFN_TAP_ROWS, FFN_TN), lambda c, i: (0, c + nct))
    return pl.pallas_call(
        body, name=name, grid=(nct, nb),
        in_specs=[blk(0), blk(0), blk(nct), halo_blk(0), halo_blk(nct), tapv, tapg,
                  pl.BlockSpec((1, FFN_TN), lambda c, i: (0, c)),
                  pl.BlockSpec((1, FFN_TN), lambda c, i: (0, c + nct))],
        out_specs=[pl.BlockSpec((2, tb, FFN_TN), lambda c, i: (0, nb - 1 - i, c)), tapv, tapv],
        out_shape=[jax.ShapeDtypeStruct((2, s, D_FF), BF16), jax.ShapeDtypeStruct((FFN_TAP_ROWS, D_FF), F32),
                   jax.ShapeDtypeStruct((FFN_TAP_ROWS, D_FF), F32)],
        scratch_shapes=[pltpu.VMEM((SUBLANES, FFN_TN), F32), pltpu.VMEM((SUBLANES, FFN_TN), F32)],
        compiler_params=_cparams(2),
    )(dhf, u, u, u, u, cw, cw, cb, cb)


def _place():
    x, y, c = lax.axis_index("x"), lax.axis_index("y"), lax.axis_index("c")
    return x, y, c, 2 * x + y


def _chip_peer(x, y, d):
    return jnp.bitwise_xor(x, d >> 1), jnp.bitwise_xor(y, d & 1)


def gather_weights(slabs):
    n = len(slabs)

    def body(*refs):
        outs = refs[n:2 * n]
        ssem, rsem, fsem, gsem = refs[2 * n:]
        x, y, c, j = _place()
        sib = (x, y, 1 - c)
        ici = {}
        for w in range(n):
            for d in (1, 2, 3):
                px, py = _chip_peer(x, y, d)
                cp = pltpu.make_async_remote_copy(
                    src_ref=outs[w].at[c, j], dst_ref=outs[w].at[c, j], send_sem=ssem.at[w, d - 1],
                    recv_sem=rsem.at[w, d - 1], device_id=(px, py, c), device_id_type=MESH)
                cp.start()
                ici[w, d] = cp
        fwd = {}
        for d in (1, 2, 3):
            jd = jnp.bitwise_xor(j, d)
            for w in range(n):
                ici[w, d].wait_recv()
                cp = pltpu.make_async_remote_copy(
                    src_ref=outs[w].at[c, jd], dst_ref=outs[w].at[c, jd], send_sem=fsem.at[w, d - 1],
                    recv_sem=gsem.at[w, d - 1], device_id=sib, device_id_type=MESH)
                cp.start()
                fwd[w, d] = cp
        for w in range(n):
            for d in (1, 2, 3):
                fwd[w, d].wait_recv()
                fwd[w, d].wait_send()
                ici[w, d].wait_send()

    out_shape = [jax.ShapeDtypeStruct(a.shape, a.dtype) for a in slabs]
    return pl.pallas_call(
        body, name="gather_weights", in_specs=[ANY] * n, out_specs=[ANY] * n, out_shape=out_shape,
        input_output_aliases={w: w for w in range(n)},
        scratch_shapes=[pltpu.SemaphoreType.DMA((n, 3)), pltpu.SemaphoreType.DMA((n, 3)),
                        pltpu.SemaphoreType.DMA((n, 3)), pltpu.SemaphoreType.DMA((n, 3))],
    )(*slabs)


def _own_slab(shard, jidx):
    slab = jnp.zeros((DEPTH, N_CHIPS) + shard.shape[1:], shard.dtype)
    return lax.dynamic_update_slice_in_dim(slab, shard[:, None], jidx, axis=1)


def exchange_layers(gbig, small):
    n = len(gbig)

    def body(*refs):
        g_refs, s_ref = refs[:n], refs[n]
        got_refs, all_ref = refs[n + 1:2 * n + 1], refs[2 * n + 1]
        dsem, esem, lsem, ssem, rsem, fsem, hsem = refs[2 * n + 2:]
        x, y, c, j = _place()
        sib = (x, y, 1 - c)
        me = 4 * x + 2 * y + c
        big = []
        for k in range(n):
            cp = pltpu.make_async_remote_copy(src_ref=g_refs[k].at[1 - c], dst_ref=got_refs[k], send_sem=dsem.at[k],
                                              recv_sem=esem.at[k], device_id=sib, device_id_type=MESH)
            cp.start()
            big.append(cp)
        mine = pltpu.make_async_copy(s_ref, all_ref.at[me], lsem)
        mine.start()

        def small_copy(k, block, to, sems, from_input):
            return pltpu.make_async_remote_copy(
                src_ref=s_ref if from_input else all_ref.at[block], dst_ref=all_ref.at[block],
                send_sem=sems[0].at[k], recv_sem=sems[1].at[k], device_id=to, device_id_type=MESH)

        first = [small_copy(0, me, sib, (ssem, rsem), True)]
        for d in (1, 2, 3):
            px, py = _chip_peer(x, y, d)
            first.append(small_copy(d, me, (px, py, c), (ssem, rsem), True))
        for cp in first:
            cp.start()
        passed = []
        for d in (1, 2, 3):
            px, py = _chip_peer(x, y, d)
            src_block = 4 * px + 2 * py + c
            small_copy(d, src_block, sib, (ssem, rsem), False).wait_recv()
            cp = small_copy(d - 1, src_block, sib, (fsem, hsem), False)
            cp.start()
            passed.append(cp)
        small_copy(0, me, sib, (ssem, rsem), False).wait_recv()
        for cp in passed:
            cp.wait_recv()
        for cp in first + passed:
            cp.wait_send()
        mine.wait()
        for cp in big:
            cp.wait()

    r2 = small.shape[0]
    return pl.pallas_call(
        body, name="exchange_layers", in_specs=[ANY] * (n + 1), out_specs=[ANY] * (n + 1),
        out_shape=[jax.ShapeDtypeStruct(g.shape[1:], F32) for g in gbig] + [jax.ShapeDtypeStruct((8, r2, LANES), F32)],
        scratch_shapes=[pltpu.SemaphoreType.DMA((n,)), pltpu.SemaphoreType.DMA((n,)), pltpu.SemaphoreType.DMA,
                        pltpu.SemaphoreType.DMA((4,)), pltpu.SemaphoreType.DMA((4,)),
                        pltpu.SemaphoreType.DMA((3,)), pltpu.SemaphoreType.DMA((3,))],
    )(*gbig, small)


def scatter_shards(s1):
    n = len(s1)

    def body(*refs):
        s_refs, got_refs = refs[:n], refs[n:2 * n]
        ssem, rsem = refs[2 * n:]
        x, y, c, j = _place()
        cps = []
        for d in (1, 2, 3):
            px, py = _chip_peer(x, y, d)
            for k in range(n):
                cp = pltpu.make_async_remote_copy(
                    src_ref=s_refs[k].at[jnp.bitwise_xor(j, d)], dst_ref=got_refs[k].at[d - 1],
                    send_sem=ssem.at[k, d - 1], recv_sem=rsem.at[k, d - 1], device_id=(px, py, c), device_id_type=MESH)
                cp.start()
                cps.append(cp)
        for cp in cps:
            cp.wait()

    return pl.pallas_call(
        body, name="scatter_shards", in_specs=[ANY] * n, out_specs=[ANY] * n,
        out_shape=[jax.ShapeDtypeStruct((3,) + a.shape[1:], a.dtype) for a in s1],
        scratch_shapes=[pltpu.SemaphoreType.DMA((n, 3)), pltpu.SemaphoreType.DMA((n, 3))],
    )(*s1)


def share_with_sibling(parts):
    n = len(parts)

    def body(*refs):
        out_refs = refs[n:2 * n]
        ssem, rsem = refs[2 * n:]
        x, y, c, j = _place()
        cps = []
        for k in range(n):
            cp = pltpu.make_async_remote_copy(src_ref=out_refs[k].at[c], dst_ref=out_refs[k].at[c], send_sem=ssem.at[k],
                                              recv_sem=rsem.at[k], device_id=(x, y, 1 - c), device_id_type=MESH)
            cp.start()
            cps.append(cp)
        for cp in cps:
            cp.wait()

    return pl.pallas_call(
        body, name="share_with_sibling", in_specs=[ANY] * n, out_specs=[ANY] * n,
        out_shape=[jax.ShapeDtypeStruct(a.shape, F32) for a in parts],
        input_output_aliases={k: k for k in range(n)},
        scratch_shapes=[pltpu.SemaphoreType.DMA((n,)), pltpu.SemaphoreType.DMA((n,))],
    )(*parts)


def add_layer_halves(g, got, cidx, name):
    _, nch, r, cdim = g.shape
    tr = _row_tile(r, cdim, mult=16)

    def body(c_ref, a_ref, b_ref, o_ref):
        o_ref[...] = _bf(a_ref[...] + b_ref[...])

    grid_spec = pltpu.PrefetchScalarGridSpec(
        num_scalar_prefetch=1, grid=(nch, r // tr),
        in_specs=[pl.BlockSpec((None, None, tr, cdim), lambda jj, i, c_ref: (c_ref[0], jj, i, 0)),
                  pl.BlockSpec((None, tr, cdim), lambda jj, i, c_ref: (jj, i, 0))],
        out_specs=pl.BlockSpec((None, tr, cdim), lambda jj, i, c_ref: (jj, i, 0)))
    return pl.pallas_call(
        body, name=name, grid_spec=grid_spec,
        out_shape=jax.ShapeDtypeStruct((nch, r, cdim), BF16), compiler_params=_cparams(2),
    )(cidx, g, got)


def add_chip_parts(s1, got, jc, name):
    _, r, cdim = s1.shape
    tr = _row_tile(r, cdim, mult=16)

    def body(jc_ref, a_ref, g0_ref, g1_ref, g2_ref, o_ref):
        o_ref[...] = ((a_ref[...].astype(F32) + g0_ref[...].astype(F32)) + g1_ref[...].astype(F32)) + g2_ref[...].astype(F32)

    def slot(k):
        return pl.BlockSpec((None, tr, cdim), lambda i, jc_ref: (k, i, 0))

    grid_spec = pltpu.PrefetchScalarGridSpec(
        num_scalar_prefetch=1, grid=(r // tr,),
        in_specs=[pl.BlockSpec((None, tr, cdim), lambda i, jc_ref: (jc_ref[0], i, 0)), slot(0), slot(1), slot(2)],
        out_specs=pl.BlockSpec((None, tr, cdim), lambda i, jc_ref: (jc_ref[1], i, 0)))
    return pl.pallas_call(
        body, name=name, grid_spec=grid_spec,
        out_shape=jax.ShapeDtypeStruct((DEPTH, r, cdim), F32), compiler_params=_cparams(1),
    )(jc, s1, got, got, got)


def sum_devices(allp):
    _, r, _ = allp.shape

    def body(a_ref, o_ref):
        tot = a_ref[0]
        for k in range(1, 8):
            tot = tot + a_ref[k]
        o_ref[...] = tot

    tr = r // 2 if r % 16 == 0 else r
    return pl.pallas_call(
        body, name="sum_devices", grid=(r // tr,),
        in_specs=[pl.BlockSpec((8, tr, LANES), lambda i: (0, i, 0))],
        out_specs=pl.BlockSpec((tr, LANES), lambda i: (i, 0)),
        out_shape=jax.ShapeDtypeStruct((r, LANES), F32), compiler_params=_cparams(1),
    )(allp)


def _row_tile(r, cdim, limit_bytes=1 << 20, mult=8):
    best = None
    for tr in range(mult, r + 1, mult):
        if r % tr == 0 and tr * cdim * 4 <= limit_bytes:
            best = tr
    return best if best is not None else r


def adamw(w, g, m, v, name):
    r, cdim = w.shape
    tr = _row_tile(r, cdim)
    bc1 = 1.0 - ADAM_B1 ** ADAM_STEP
    bc2 = 1.0 - ADAM_B2 ** ADAM_STEP

    def body(w_ref, g_ref, m_ref, v_ref, d_ref, nm_ref, nv_ref, go_ref):
        gv = g_ref[...]
        nm = ADAM_B1 * m_ref[...] + (1.0 - ADAM_B1) * gv
        nv = ADAM_B2 * v_ref[...] + (1.0 - ADAM_B2) * (gv * gv)
        d_ref[...] = -ADAM_LR * ((nm / bc1) / (jnp.sqrt(nv / bc2) + ADAM_EPS) + ADAM_WD * w_ref[...])
        nm_ref[...] = nm
        nv_ref[...] = nv
        go_ref[...] = gv

    blk = pl.BlockSpec((tr, cdim), lambda i: (i, 0))
    shape = jax.ShapeDtypeStruct((r, cdim), F32)
    return pl.pallas_call(
        body, name=name, grid=(r // tr,), in_specs=[blk] * 4, out_specs=[blk] * 4, out_shape=[shape] * 4,
        compiler_params=_cparams(1),
    )(w, g, m, v)


def _s5_prepare(lam_re, lam_im, log_dt, b_re, b_im, c_re, c_im):
    groups, ch = 16, 16
    dt = jnp.exp(log_dt)[:, None]
    mag = jnp.exp(lam_re * dt)
    a_r, a_i = mag * jnp.cos(lam_im * dt), mag * jnp.sin(lam_im * dt)
    den = lam_re * lam_re + lam_im * lam_im
    q_r = ((a_r - 1.0) * lam_re + a_i * lam_im) / den
    q_i = (a_i * lam_re - (a_r - 1.0) * lam_im) / den
    bb_r = q_r[..., None] * b_re - q_i[..., None] * b_im
    bb_i = q_r[..., None] * b_im + q_i[..., None] * b_re
    eye = jnp.eye(groups, dtype=F32)

    def expand_b(bb):
        return jnp.einsum("gpc,gh->gchp", bb, eye).reshape(groups * ch, N_STATE)

    def expand_c(cc):
        return jnp.einsum("gcp,gh->hpgc", cc, eye).reshape(N_STATE, groups * ch)

    a2 = jnp.concatenate([a_r.reshape(1, N_STATE), a_i.reshape(1, N_STATE)], axis=1)
    bexp = jnp.concatenate([expand_b(bb_r), expand_b(bb_i)], axis=1)
    cexp = jnp.concatenate([expand_c(c_re), -expand_c(c_im)], axis=0)
    return a2, bexp, cexp


def _lru_prepare(w_r, w_i, lam):
    heads = 4
    eye = jnp.eye(heads, dtype=F32)

    def expand(w):
        return jnp.einsum("hij,hk->hikj", w, eye).reshape(D_GROUP, D_GROUP)

    return expand(w_r), expand(w_i), jax.nn.softplus(-lam).reshape(1, D_GROUP)


def _pad_rows(a, rows):
    return jnp.pad(a, ((0, rows - a.shape[0]), (0, 0)))


def _group_mean_matrix():
    gidx = jnp.arange(D_GROUP) // 64
    return (gidx[:, None] == gidx[None, :]).astype(BF16) * jnp.asarray(1.0 / 64.0, BF16)


def _pack_rows(arrs, width):
    parts = []
    for a in arrs:
        flat = a.reshape(-1)
        pad = (-flat.shape[0]) % width
        parts.append(jnp.pad(flat, (0, pad)) if pad else flat)
    flat = jnp.concatenate(parts)
    rows = flat.shape[0] // width
    pad_rows = (-rows) % 16
    if pad_rows:
        flat = jnp.pad(flat, (0, pad_rows * width))
    return flat.reshape(-1, width)


def _unpack_rows(packed, shapes, width):
    flat = packed.reshape(-1)
    out, off = [], 0
    for shp in shapes:
        size = math.prod(shp)
        out.append(flat[off:off + size].reshape(shp))
        off += size + ((-size) % width)
    return out


def kernel(x, mem, ln_in_g, ln_in_b, w_in, b_in, s5_lam_re, s5_lam_im, s5_log_dt, s5_b_re, s5_b_im, s5_c_re, s5_c_im, s5_d, s5_w_glu, s5_b_glu, cv_w, cv_b, cv_gn_g, cv_gn_b, cv_w_pw, cv_b_pw, lru_conv_w, lru_conv_b, lru_w_r, lru_b_r, lru_w_i, lru_b_i, lru_lam, attn_w_kv, w_out, b_out, ln1_g, ln1_b, ffn_w_up, ffn_conv_w, ffn_conv_b, ffn_w_down, ln2_g, ln2_b, loss_target, m_ln_in_g, m_ln_in_b, m_w_in, m_b_in, m_s5_lam_re, m_s5_lam_im, m_s5_log_dt, m_s5_b_re, m_s5_b_im, m_s5_c_re, m_s5_c_im, m_s5_d, m_s5_w_glu, m_s5_b_glu, m_cv_w, m_cv_b, m_cv_gn_g, m_cv_gn_b, m_cv_w_pw, m_cv_b_pw, m_lru_conv_w, m_lru_conv_b, m_lru_w_r, m_lru_b_r, m_lru_w_i, m_lru_b_i, m_lru_lam, m_attn_w_kv, m_w_out, m_b_out, m_ln1_g, m_ln1_b, m_ffn_w_up, m_ffn_conv_w, m_ffn_conv_b, m_ffn_w_down, m_ln2_g, m_ln2_b, v_ln_in_g, v_ln_in_b, v_w_in, v_b_in, v_s5_lam_re, v_s5_lam_im, v_s5_log_dt, v_s5_b_re, v_s5_b_im, v_s5_c_re, v_s5_c_im, v_s5_d, v_s5_w_glu, v_s5_b_glu, v_cv_w, v_cv_b, v_cv_gn_g, v_cv_gn_b, v_cv_w_pw, v_cv_b_pw, v_lru_conv_w, v_lru_conv_b, v_lru_w_r, v_lru_b_r, v_lru_w_i, v_lru_b_i, v_lru_lam, v_attn_w_kv, v_w_out, v_b_out, v_ln1_g, v_ln1_b, v_ffn_w_up, v_ffn_conv_w, v_ffn_conv_b, v_ffn_w_down, v_ln2_g, v_ln2_b):
    p = dict(locals())
    xs = x[0]
    mems = mem[0]
    target = loss_target[0]
    s = xs.shape[0]
    cidx = lax.axis_index("c")
    jidx = 2 * lax.axis_index("x") + lax.axis_index("y")
    tb_scan = min(256, s)
    tb_attn = min(512, s)
    tb_ffn = min(256, s)

    small_sh_names = list(SMALL_SHARDED)
    small_sh_shapes = [p[nm].shape[1:] for nm in small_sh_names]
    small_pack = jnp.stack([_pack_rows([p[nm][l] for nm in small_sh_names], LANES) for l in range(DEPTH)])
    gathered = gather_weights([_own_slab(_bf(p[nm]), jidx) for nm in BIG] + [_own_slab(small_pack, jidx)])
    g_w_in, g_w_kv, g_w_out, g_w_up, g_w_down, g_small = gathered
    g_w_kv = g_w_kv.reshape(DEPTH, 1, D_MODEL, 2 * D_GROUP)
    g_w_out = g_w_out.reshape(DEPTH, 1, D_MODEL, D_MODEL)
    g_w_down = g_w_down.reshape(DEPTH, 1, D_FF, D_MODEL)
    full_small = {nm: [] for nm in small_sh_names}
    for l in range(DEPTH):
        per_chip = [_unpack_rows(g_small[l, jj], small_sh_shapes, LANES) for jj in range(N_CHIPS)]
        for k, nm in enumerate(small_sh_names):
            full_small[nm].append(jnp.concatenate([per_chip[jj][k] for jj in range(N_CHIPS)],
                                                  axis=SMALL_SHARDED[nm] - 1))
    w_glu_bf = _bf(jnp.stack(full_small['s5_w_glu']))
    w_pw_bf = _bf(jnp.stack(full_small['cv_w_pw']))
    pmat = _group_mean_matrix()

    def vec(a):
        return a.reshape(1, -1)

    xh0, rs0, xb0 = ln_fwd(xs, vec(ln_in_g), vec(ln_in_b), "ln_in")
    saved = []
    prev = dict(xh=xh0, rs=rs0, xb=xb0, g=vec(ln_in_g), b=vec(ln_in_b))
    for l in range(DEPTH):
        sv = dict(prev=prev)
        (a2, bexp, cexp), sv['s5_vjp'] = jax.vjp(_s5_prepare, s5_lam_re[l], s5_lam_im[l], s5_log_dt[l],
                                                 s5_b_re[l], s5_b_im[l], s5_c_re[l], s5_c_im[l])
        (wr, wi, sp), sv['lru_vjp'] = jax.vjp(_lru_prepare, lru_w_r[l], lru_w_i[l], lru_lam[l])
        sv.update(a2=a2, bexp=_bf(bexp), cexp=_bf(cexp), wr=_bf(wr), wi=_bf(wi), sp=sp)
        sv['cvw'] = _pad_rows(full_small['cv_w'][l], CV_HALO)
        sv['lcw'] = _pad_rows(full_small['lru_conv_w'][l], LRU_HALO)
        sv['fcw'] = _pad_rows(full_small['ffn_conv_w'][l], FFN_TAP_ROWS)
        h_in = mm_nn(prev['xb'], g_w_in, l, vec(b_in[l]), F32, 512, f"in_proj{l}")
        kv = mm_nn(mems, g_w_kv, l, jnp.zeros((1, 2 * D_GROUP), F32), F32, 256, f"kv_proj{l}")
        y_s5, hst, y0 = s5_fwd(h_in, a2, sv['bexp'], sv['cexp'], vec(s5_d[l]), w_glu_bf, l, vec(s5_b_glu[l]),
                               tb_scan, f"s5_fwd{l}")
        y_cv, hc = cv_fwd(h_in, sv['cvw'], vec(cv_b[l]), vec(cv_gn_g[l]), vec(cv_gn_b[l]), pmat, w_pw_bf, l,
                          vec(cv_b_pw[l]), tb_scan, f"cv_fwd{l}")
        y_lru, xcs, hls = lru_fwd(h_in, sv['lcw'], vec(lru_conv_b[l]), sv['wr'], vec(lru_b_r[l]), sv['wi'],
                                  vec(lru_b_i[l]), sp, tb_scan, f"lru_fwd{l}")
        y_mem = attn_fwd(h_in, kv, tb_attn, f"attn_fwd{l}")
        mix_in = jnp.concatenate([y_s5, y_cv, y_lru, y_mem], axis=1)
        xh1, rs1, xb1 = proj_ln(mix_in, g_w_out.reshape(DEPTH, D_MODEL, D_MODEL), l, vec(b_out[l]),
                                prev['xh'], prev['g'], prev['b'], vec(ln1_g[l]), vec(ln1_b[l]), f"out_proj_ln{l}")
        u = mm_nn(xb1, g_w_up, l, jnp.zeros((1, 2 * D_FF), F32), F32, 1024, f"ffn_up{l}")
        hf = ffn_act_fwd(u, sv['fcw'], vec(ffn_conv_b[l]), tb_ffn, f"ffn_act{l}")
        xh2, rs2, xb2 = proj_ln(hf, g_w_down.reshape(DEPTH, D_FF, D_MODEL), l, jnp.zeros((1, D_MODEL), F32),
                                xh1, vec(ln1_g[l]), vec(ln1_b[l]), vec(ln2_g[l]), vec(ln2_b[l]), f"ffn_down_ln{l}")
        sv.update(h_in=h_in, kv=kv, hst=hst, y0=y0, hc=hc, xcs=xcs, hls=hls, mix_in=mix_in,
                  xh1=xh1, rs1=rs1, xb1=xb1, u=u, hf=hf, xh2=xh2, rs2=rs2)
        saved.append(sv)
        prev = dict(xh=xh2, rs=rs2, xb=xb2, g=vec(ln2_g[l]), b=vec(ln2_b[l]))

    grads = {}
    per_layer = {nm: [None] * DEPTH for nm in WEIGHTS if nm not in ('ln_in_g', 'ln_in_b')}
    gbig = [None] * len(BIG)
    dx = None
    for l in reversed(range(DEPTH)):
        sv = saved[l]
        pv = sv['prev']
        if l == DEPTH - 1:
            dr2, dg2, db2, sqerr = loss_ln_bwd(target, sv['xh2'], sv['rs2'], vec(ln2_g[l]), vec(ln2_b[l]), "loss_ln2_bwd")
            loss_local = 0.5 / D_MODEL * jnp.sum(sqerr)
        else:
            dr2, dg2, db2, _ = ln_bwd(dx, sv['xh2'], sv['rs2'], vec(ln2_g[l]), f"ln2_bwd{l}")
        per_layer['ln2_g'][l], per_layer['ln2_b'][l] = dg2[0], db2[0]
        tm_nt = min(512, s)
        ts_big = min(2048, s)
        whole = lambda a_ref, j: a_ref[...]
        dhf = mm_nt(dr2, (tm_nt, D_MODEL), lambda i: (i, 0), whole, g_w_down, l, None, F32, 512, s, f"ffn_down_dx{l}")
        gbig[4] = mm_tn(sv['hf'], dr2, (min(1024, s), D_MODEL), lambda kt, j, st: (st, 0), 1, D_MODEL, FFN_TN, 1024, s,
                        l, gbig[4], f"ffn_down_dw{l}")
        du, dcwv, dcwg = ffn_act_bwd(dhf, sv['u'], sv['fcw'], vec(ffn_conv_b[l]), tb_ffn, f"ffn_act_bwd{l}")
        dcw = jnp.concatenate([dcwv, dcwg], axis=1)
        per_layer['ffn_conv_w'][l] = dcw[0:FFN_CONV_WIDTH]
        per_layer['ffn_conv_b'][l] = dcw[FFN_CONV_WIDTH]
        dx1 = mm_nt(du, (2, tm_nt, D_FF), lambda i: (0, i, 0),
                    lambda a_ref, j: a_ref[j // 2, :, (j % 2) * FFN_TN:(j % 2 + 1) * FFN_TN], g_w_up, l, dr2, F32,
                    512, s, f"ffn_up_dx{l}")
        gbig[3] = mm_tn(sv['xb1'], du, (None, ts_big, FFN_TN), lambda kt, j, st: (j // 2, st, j % 2), N_CHIPS, FFN_TN,
                        D_MODEL, 2048, s, l, gbig[3], f"ffn_up_dw{l}")
        dr1, dg1, db1, cs1 = ln_bwd(dx1, sv['xh1'], sv['rs1'], vec(ln1_g[l]), f"ln1_bwd{l}")
        per_layer['ln1_g'][l], per_layer['ln1_b'][l], per_layer['b_out'][l] = dg1[0], db1[0], cs1[0]
        dmix = mm_nt(dr1, (tm_nt, D_MODEL), lambda i: (i, 0), whole, g_w_out, l, None, F32, 512, s, f"out_proj_dx{l}")
        gbig[2] = mm_tn(sv['mix_in'], dr1, (min(1024, s), D_MODEL), lambda kt, j, st: (st, 0), 1, D_MODEL, D_MODEL, 1024, s,
                        l, gbig[2], f"out_proj_dw{l}")
        h_in = sv['h_in']
        (d_u, cs_u, d_bexp, d_cexp, d_dd, d_wglu, d_bglu, d_a2) = s5_bwd(
            dmix, h_in, sv['y0'], sv['hst'], sv['a2'], sv['bexp'], sv['cexp'], vec(s5_d[l]), w_glu_bf, l,
            vec(s5_b_glu[l]), tb_scan, f"s5_bwd{l}")
        (d_vg, cs_vg, d_cvw, d_cvb, d_gg, d_gb, d_wpw, d_bpw) = cv_bwd(
            dmix, h_in, sv['hc'], sv['cvw'], vec(cv_gn_g[l]), vec(cv_gn_b[l]), pmat, w_pw_bf, l, tb_scan, f"cv_bwd{l}")
        (d_lx, cs_lx, d_lcw, d_lcb, d_wr, d_br, d_wi, d_bi, d_sp) = lru_bwd(
            dmix, h_in, sv['xcs'], sv['hls'], sv['lcw'], sv['wr'], vec(lru_b_r[l]), sv['wi'], vec(lru_b_i[l]),
            sv['sp'], tb_scan, f"lru_bwd{l}")
        d_q, cs_q, d_kv = attn_bwd(dmix, h_in, sv['kv'], tb_attn, f"attn_bwd{l}")
        g_s5 = sv['s5_vjp']((d_a2, d_bexp, d_cexp))
        for nm, gval in zip(['s5_lam_re', 's5_lam_im', 's5_log_dt', 's5_b_re', 's5_b_im', 's5_c_re', 's5_c_im'], g_s5):
            per_layer[nm][l] = gval
        g_lru = sv['lru_vjp']((d_wr, d_wi, d_sp))
        for nm, gval in zip(['lru_w_r', 'lru_w_i', 'lru_lam'], g_lru):
            per_layer[nm][l] = gval
        per_layer['s5_d'][l], per_layer['s5_w_glu'][l], per_layer['s5_b_glu'][l] = d_dd[0], d_wglu, d_bglu[0]
        per_layer['cv_w'][l], per_layer['cv_b'][l] = d_cvw[0:CONV_WIDTH], d_cvb[0]
        per_layer['cv_gn_g'][l], per_layer['cv_gn_b'][l] = d_gg[0], d_gb[0]
        per_layer['cv_w_pw'][l], per_layer['cv_b_pw'][l] = d_wpw, d_bpw[0]
        per_layer['lru_conv_w'][l], per_layer['lru_conv_b'][l] = d_lcw[0:LRU_CONV_WIDTH], d_lcb[0]
        per_layer['lru_b_r'][l], per_layer['lru_b_i'][l] = d_br[0], d_bi[0]
        per_layer['b_in'][l] = jnp.concatenate([cs_u, cs_vg, cs_lx, cs_q], axis=1)[0]
        gbig[1] = mm_tn(mems, d_kv, (MEM_ROWS, 2 * D_GROUP), lambda kt, j, st: (st, 0), 1, 2 * D_GROUP, D_MODEL, MEM_ROWS,
                        MEM_ROWS, l, gbig[1], f"kv_proj_dw{l}")
        dh_in = jnp.concatenate([d_u, d_vg, d_lx, d_q], axis=1)
        n_sh = N_IN // N_CHIPS
        dxp = mm_nt(dh_in, (tm_nt, N_IN), lambda i: (i, 0), lambda a_ref, j: a_ref[:, j * n_sh:(j + 1) * n_sh],
                    g_w_in, l, dr1, F32, 512, s, f"in_proj_dx{l}")
        gbig[0] = mm_tn(pv['xb'], dh_in, (ts_big, n_sh), lambda kt, j, st: (st, j), N_CHIPS, n_sh, D_MODEL, 2048, s,
                        l, gbig[0], f"in_proj_dw{l}")
        dx = dxp
    grad_x, dg_in, db_in, _ = ln_bwd(dx, xh0, rs0, vec(ln_in_g), "ln_in_bwd")
    grads['ln_in_g'], grads['ln_in_b'] = dg_in[0], db_in[0]
    for nm, vals in per_layer.items():
        if nm not in BIG:
            grads[nm] = jnp.stack(vals)

    small_names = [nm for nm in WEIGHTS if nm not in BIG]
    small_local = _pack_rows([grads[nm] for nm in small_names], LANES)
    gbig = [g.reshape((DEPTH, N_CHIPS) + p[nm].shape[1:]) for g, nm in zip(gbig, BIG)]
    c1 = cidx.reshape(1).astype(jnp.int32)
    jc = jnp.stack([jidx, cidx]).astype(jnp.int32)
    *got_a, small_all = exchange_layers(gbig, small_local)
    s1 = [add_layer_halves(g, ga, c1, f"add_cores_{nm}") for g, ga, nm in zip(gbig, got_a, BIG)]
    got_b = scatter_shards(s1)
    mine_r = [add_chip_parts(sk, gb, jc, f"add_chips_{nm}") for sk, gb, nm in zip(s1, got_b, BIG)]
    red_big = share_with_sibling(mine_r)
    small_red = sum_devices(small_all)
    small_grads = dict(zip(small_names, _unpack_rows(small_red, [grads[nm].shape for nm in small_names], LANES)))

    out_g, out_d, out_m, out_v = {}, {}, {}, {}
    for k, nm in enumerate(BIG):
        gk = red_big[k]
        two_d = (-1, p[nm].shape[-1])
        res = adamw(p[nm].reshape(two_d), gk.reshape(two_d), p['m_' + nm].reshape(two_d),
                    p['v_' + nm].reshape(two_d), f"adamw_{nm}")
        out_d[nm], out_m[nm], out_v[nm], out_g[nm] = (t.reshape(p[nm].shape) for t in res)
    own = {}
    for nm in small_names:
        gfull = small_grads[nm]
        if nm in SMALL_SHARDED:
            ax = SMALL_SHARDED[nm]
            width = p[nm].shape[ax]
            gfull = lax.dynamic_slice_in_dim(gfull, jidx * width, width, axis=ax)
        own[nm] = gfull
    packs = [_pack_rows([src[nm] for nm in small_names], LANES)
             for src in (dict((nm, p[nm]) for nm in small_names), own,
                         dict((nm, p['m_' + nm]) for nm in small_names), dict((nm, p['v_' + nm]) for nm in small_names))]
    dlt, nm_, nv_, _ = adamw(packs[0], packs[1], packs[2], packs[3], "adamw_small")
    shapes = [p[nm].shape for nm in small_names]
    for dst, packed in ((out_d, dlt), (out_m, nm_), (out_v, nv_)):
        dst.update(zip(small_names, _unpack_rows(packed, shapes, LANES)))
    out_g.update(own)

    loss = lax.psum(loss_local, ("x", "y", "c"))
    return (loss, grad_x[None], *[out_g[nm] for nm in WEIGHTS], *[out_d[nm] for nm in WEIGHTS],
            *[out_m[nm] for nm in WEIGHTS], *[out_v[nm] for nm in WEIGHTS])
```

```python
import functools
import math

import jax
import jax.numpy as jnp
from jax import lax
from jax.experimental import pallas as pl
from jax.experimental.pallas import tpu as pltpu

F32 = jnp.float32
BF16 = jnp.bfloat16
MESH = pl.DeviceIdType.MESH
ANY = pl.BlockSpec(memory_space=pl.ANY)

DEPTH = 2
D_MODEL = 1024
D_GROUP = 256
N_IN = 6 * D_GROUP
D_FF = 2816
N_STATE = 1024
CONV_WIDTH = 31
LRU_CONV_WIDTH = 4
FFN_CONV_WIDTH = 3
LRU_C = 8.0
ALPHA = (2 * DEPTH) ** 0.25
LN_EPS = 1e-5
N_CHIPS = 4
MEM_ROWS = 256
LANES = 128
SUBLANES = 8
VMEM_LIMIT = 56 * 1024 * 1024

ADAM_LR, ADAM_B1, ADAM_B2, ADAM_EPS, ADAM_WD, ADAM_STEP = 0.001, 0.9, 0.999, 1e-08, 0.01, 10

WEIGHTS = ['ln_in_g', 'ln_in_b', 'w_in', 'b_in', 's5_lam_re', 's5_lam_im', 's5_log_dt', 's5_b_re', 's5_b_im',
           's5_c_re', 's5_c_im', 's5_d', 's5_w_glu', 's5_b_glu', 'cv_w', 'cv_b', 'cv_gn_g', 'cv_gn_b', 'cv_w_pw',
           'cv_b_pw', 'lru_conv_w', 'lru_conv_b', 'lru_w_r', 'lru_b_r', 'lru_w_i', 'lru_b_i', 'lru_lam',
           'attn_w_kv', 'w_out', 'b_out', 'ln1_g', 'ln1_b', 'ffn_w_up', 'ffn_conv_w', 'ffn_conv_b', 'ffn_w_down',
           'ln2_g', 'ln2_b']
BIG = ['w_in', 'attn_w_kv', 'w_out', 'ffn_w_up', 'ffn_w_down']
SMALL_SHARDED = {'s5_w_glu': 1, 'cv_w': 2, 'cv_w_pw': 1, 'lru_conv_w': 2, 'ffn_conv_w': 2}


def _cparams(n_axes):
    return pltpu.CompilerParams(dimension_semantics=("arbitrary",) * n_axes, vmem_limit_bytes=VMEM_LIMIT)


def _dot(a, b):
    return jnp.dot(a, b, preferred_element_type=F32)


def _dot_nt(a, b):
    return lax.dot_general(a, b, (((1,), (1,)), ((), ())), preferred_element_type=F32)


def _dot_tn(a, b):
    return lax.dot_general(a, b, (((0,), (0,)), ((), ())), preferred_element_type=F32)


def _bf(v):
    return v.astype(BF16)


def _colsum(v):
    return jnp.sum(v, axis=0, keepdims=True)


def _dot3(v, p):
    hi = _bf(v)
    r1 = v - hi.astype(F32)
    mid = _bf(r1)
    lo = _bf(r1 - mid.astype(F32))
    return _dot(hi, p) + _dot(mid, p) + _dot(lo, p)


_GELU_C = math.sqrt(2.0 / math.pi)


_GELU_C3 = _GELU_C * 0.044715


def _gelu_parts(v):
    t = jnp.tanh(v * (_GELU_C + _GELU_C3 * (v * v)))
    hv = 0.5 * v
    return hv + hv * t, t


def _gelu(v):
    return _gelu_parts(v)[0]


def _gelu_grad(v, t):
    return (0.5 + 0.5 * t) + (0.5 * v) * (1.0 - t * t) * (_GELU_C + (3.0 * _GELU_C3) * (v * v))


def _sigmoid(v):
    return 1.0 / (1.0 + jnp.exp(-v))


def _acc(ref, val, first):
    @pl.when(first)
    def _():
        ref[...] = val

    @pl.when(jnp.logical_not(first))
    def _():
        ref[...] += val


def _rows(shape):
    return lax.broadcasted_iota(jnp.int32, shape, 0)


def _ln_rows(r):
    mu = jnp.mean(r, -1, keepdims=True)
    rc = r - mu
    var = jnp.mean(rc * rc, -1, keepdims=True)
    rs = lax.rsqrt(var + LN_EPS)
    return rc * rs, rs


def ln_fwd(x, g, b, name):
    s = x.shape[0]
    tm = min(512, s)

    def body(x_ref, g_ref, b_ref, xh_ref, rs_ref, xb_ref):
        xh, rs = _ln_rows(x_ref[...])
        xh_ref[...] = xh
        rs_ref[...] = rs
        xb_ref[...] = _bf(xh * g_ref[...] + b_ref[...])

    row = pl.BlockSpec((tm, D_MODEL), lambda i: (i, 0))
    vec = pl.BlockSpec((1, D_MODEL), lambda i: (0, 0))
    return pl.pallas_call(
        body, name=name, grid=(s // tm,),
        in_specs=[row, vec, vec],
        out_specs=[row, pl.BlockSpec((tm, 1), lambda i: (i, 0)), row],
        out_shape=[jax.ShapeDtypeStruct((s, D_MODEL), F32), jax.ShapeDtypeStruct((s, 1), F32),
                   jax.ShapeDtypeStruct((s, D_MODEL), BF16)],
        compiler_params=_cparams(1),
    )(x, g, b)


def proj_ln(a, w, layer, bias, xh_prev, g_prev, b_prev, g, b, name):
    s, k = a.shape
    tm = min(512, s)

    def body(a_ref, w_ref, bias_ref, xp_ref, gp_ref, bp_ref, g_ref, b_ref, xh_ref, rs_ref, xb_ref):
        acc = _dot(a_ref[...], w_ref[...]) + bias_ref[...]
        r = ALPHA * (xp_ref[...] * gp_ref[...] + bp_ref[...]) + acc
        xh, rs = _ln_rows(r)
        xh_ref[...] = xh
        rs_ref[...] = rs
        xb_ref[...] = _bf(xh * g_ref[...] + b_ref[...])

    row = pl.BlockSpec((tm, D_MODEL), lambda i: (i, 0))
    vec = pl.BlockSpec((1, D_MODEL), lambda i: (0, 0))
    return pl.pallas_call(
        body, name=name, grid=(s // tm,),
        in_specs=[pl.BlockSpec((tm, k), lambda i: (i, 0)),
                  pl.BlockSpec((None, k, D_MODEL), lambda i: (layer, 0, 0)),
                  vec, row, vec, vec, vec, vec],
        out_specs=[row, pl.BlockSpec((tm, 1), lambda i: (i, 0)), row],
        out_shape=[jax.ShapeDtypeStruct((s, D_MODEL), F32), jax.ShapeDtypeStruct((s, 1), F32),
                   jax.ShapeDtypeStruct((s, D_MODEL), BF16)],
        compiler_params=_cparams(1),
    )(a, w, bias, xh_prev, g_prev, b_prev, g, b)


def ln_bwd(dy, xh, rs, g, name):
    s = xh.shape[0]
    tm = min(512, s)

    def body(dy_ref, xh_ref, rs_ref, g_ref, dr_ref, dg_ref, db_ref, cs_ref):
        first = pl.program_id(0) == 0
        dyv = dy_ref[...]
        xhv = xh_ref[...]
        dxh = dyv * g_ref[...]
        dr = rs_ref[...] * (dxh - jnp.mean(dxh, -1, keepdims=True) - xhv * jnp.mean(dxh * xhv, -1, keepdims=True))
        dr_ref[...] = dr
        _acc(dg_ref, _colsum(dyv * xhv), first)
        _acc(db_ref, _colsum(dyv), first)
        _acc(cs_ref, _colsum(dr), first)

    row = pl.BlockSpec((tm, D_MODEL), lambda i: (i, 0))
    vec = pl.BlockSpec((1, D_MODEL), lambda i: (0, 0))
    vshape = jax.ShapeDtypeStruct((1, D_MODEL), F32)
    return pl.pallas_call(
        body, name=name, grid=(s // tm,),
        in_specs=[row, row, pl.BlockSpec((tm, 1), lambda i: (i, 0)), vec],
        out_specs=[row, vec, vec, vec],
        out_shape=[jax.ShapeDtypeStruct((s, D_MODEL), F32), vshape, vshape, vshape],
        compiler_params=_cparams(1),
    )(dy, xh, rs, g)


def loss_ln_bwd(target, xh, rs, g, b, name):
    s = xh.shape[0]
    tm = min(512, s)

    def body(t_ref, xh_ref, rs_ref, g_ref, b_ref, dr_ref, dg_ref, db_ref, sq_ref):
        first = pl.program_id(0) == 0
        xhv = xh_ref[...]
        err = xhv * g_ref[...] + b_ref[...] - t_ref[...]
        dyv = err * (1.0 / D_MODEL)
        dxh = dyv * g_ref[...]
        dr = rs_ref[...] * (dxh - jnp.mean(dxh, -1, keepdims=True) - xhv * jnp.mean(dxh * xhv, -1, keepdims=True))
        dr_ref[...] = dr
        _acc(dg_ref, _colsum(dyv * xhv), first)
        _acc(db_ref, _colsum(dyv), first)
        _acc(sq_ref, _colsum(err * err), first)

    row = pl.BlockSpec((tm, D_MODEL), lambda i: (i, 0))
    vec = pl.BlockSpec((1, D_MODEL), lambda i: (0, 0))
    vshape = jax.ShapeDtypeStruct((1, D_MODEL), F32)
    return pl.pallas_call(
        body, name=name, grid=(s // tm,),
        in_specs=[row, row, pl.BlockSpec((tm, 1), lambda i: (i, 0)), vec, vec],
        out_specs=[row, vec, vec, vec],
        out_shape=[jax.ShapeDtypeStruct((s, D_MODEL), F32), vshape, vshape, vshape],
        compiler_params=_cparams(1),
    )(target, xh, rs, g, b)


def mm_nn(a, w, layer, bias, out_dtype, tm, name):
    m, k = a.shape
    _, nj, _, n = w.shape
    tm = min(tm, m)

    def body(a_ref, w_ref, b_ref, o_ref):
        o_ref[...] = (_dot(_bf(a_ref[...]), w_ref[...]) + b_ref[...]).astype(out_dtype)

    return pl.pallas_call(
        body, name=name, grid=(nj, m // tm),
        in_specs=[pl.BlockSpec((tm, k), lambda j, i: (i, 0)),
                  pl.BlockSpec((None, None, k, n), lambda j, i: (layer, j, 0, 0)),
                  pl.BlockSpec((1, n), lambda j, i: (0, j))],
        out_specs=pl.BlockSpec((tm, n), lambda j, i: (i, j)),
        out_shape=jax.ShapeDtypeStruct((m, nj * n), out_dtype),
        compiler_params=_cparams(2),
    )(a, w, bias)


def mm_nt(a, a_block, a_map, pick, w, layer, add, out_dtype, tm, m, name):
    _, nj, r, n = w.shape
    tm = min(tm, m)
    has_add = add is not None

    def body(*refs):
        if has_add:
            a_ref, w_ref, add_ref, o_ref = refs
        else:
            a_ref, w_ref, o_ref = refs
        res = _dot_nt(_bf(pick(a_ref, 0)), w_ref[0])
        for j in range(1, nj):
            res = res + _dot_nt(_bf(pick(a_ref, j)), w_ref[j])
        if has_add:
            res = res + ALPHA * add_ref[...]
        o_ref[...] = res.astype(out_dtype)

    in_specs = [pl.BlockSpec(a_block, a_map),
                pl.BlockSpec((None, nj, r, n), lambda i: (layer, 0, 0, 0))]
    ops = [a, w]
    if has_add:
        in_specs.append(pl.BlockSpec((tm, r), lambda i: (i, 0)))
        ops.append(add)
    return pl.pallas_call(
        body, name=name, grid=(m // tm,),
        in_specs=in_specs,
        out_specs=pl.BlockSpec((tm, r), lambda i: (i, 0)),
        out_shape=jax.ShapeDtypeStruct((m, r), out_dtype),
        compiler_params=_cparams(1),
    )(*ops)


def mm_tn(a, b, b_block, b_map, nj, n, tk, ts, s, layer, into, name):
    kx = a.shape[1]
    ts = min(ts, s)

    def body(a_ref, b_ref, *rest):
        o_ref = rest[-1]
        part = _dot_tn(_bf(a_ref[...]), _bf(b_ref[...]))
        _acc(o_ref, part, pl.program_id(2) == 0)

    in_specs = [pl.BlockSpec((ts, tk), lambda kt, j, st: (st, kt)), pl.BlockSpec(b_block, b_map)]
    ops = [a, b]
    aliases = {}
    if into is not None:
        in_specs.append(ANY)
        ops.append(into)
        aliases = {2: 0}
    return pl.pallas_call(
        body, name=name, grid=(kx // tk, nj, s // ts),
        in_specs=in_specs,
        out_specs=pl.BlockSpec((None, None, tk, n), lambda kt, j, st: (layer, j, kt, 0)),
        out_shape=jax.ShapeDtypeStruct((DEPTH, nj, kx, n), F32),
        input_output_aliases=aliases,
        compiler_params=_cparams(3),
    )(*ops)


S5_TAB_ROWS = 8 * SUBLANES


def _s5_scan_table(tab_ref, ar, ai, reverse):
    n = N_STATE
    row = _rows((SUBLANES, n))
    edge = SUBLANES - 1 if reverse else 0
    tab_ref[0:8, :] = jnp.where(row == edge, ar, 0.0)
    tab_ref[8:16, :] = jnp.where(row == edge, ai, 0.0)
    pr, pi = ar, ai
    for step, k in enumerate((1, 2, 4)):
        mask = row < SUBLANES - k if reverse else row >= k
        tab_ref[16 + 16 * step:24 + 16 * step, :] = jnp.where(mask, pr, 0.0)
        tab_ref[24 + 16 * step:32 + 16 * step, :] = jnp.where(mask, pi, 0.0)
        pr, pi = pr * pr - pi * pi, 2.0 * pr * pi


def _s5_scan(src_ref, dst_ref, tab_ref, edge_ref, tb, reverse, per_tile=None):
    n = N_STATE
    ng = tb // SUBLANES
    nq = n // LANES
    link = SUBLANES - 1 if reverse else 1

    def tile(ii, carry):
        g = ng - 1 - ii if reverse else ii
        rows = pl.ds(pl.multiple_of(g * SUBLANES, SUBLANES), SUBLANES)
        out = []
        for q in range(nq):
            cre = slice(q * LANES, (q + 1) * LANES)
            cim = slice(n + q * LANES, n + (q + 1) * LANES)
            lr, li = src_ref[rows, cre], src_ref[rows, cim]
            tr, ti = pltpu.roll(carry[2 * q], link, 0), pltpu.roll(carry[2 * q + 1], link, 0)
            kr, ki = tab_ref[0:8, cre], tab_ref[8:16, cre]
            lr, li = lr + kr * tr - ki * ti, li + kr * ti + ki * tr
            for step, k in enumerate((1, 2, 4)):
                amt = SUBLANES - k if reverse else k
                kr, ki = tab_ref[16 + 16 * step:24 + 16 * step, cre], tab_ref[24 + 16 * step:32 + 16 * step, cre]
                sr, si = pltpu.roll(lr, amt, 0), pltpu.roll(li, amt, 0)
                lr, li = lr + kr * sr - ki * si, li + kr * si + ki * sr
            dst_ref[rows, cre] = lr
            dst_ref[rows, cim] = li
            if per_tile is not None:
                per_tile(g, q, (cre, cim), lr, li)
            out += [lr, li]
        return tuple(out)

    init = []
    for q in range(nq):
        init += [edge_ref[:, q * LANES:(q + 1) * LANES], edge_ref[:, n + q * LANES:n + (q + 1) * LANES]]
    fin = lax.fori_loop(0, ng, tile, tuple(init))
    for q in range(nq):
        edge_ref[:, q * LANES:(q + 1) * LANES] = fin[2 * q]
        edge_ref[:, n + q * LANES:n + (q + 1) * LANES] = fin[2 * q + 1]


def s5_fwd(h_in, a2, bexp, cexp, dskip, wglu, layer, bglu, tb, name):
    s = h_in.shape[0]
    n = N_STATE

    def body(u_ref, a_ref, b_ref, c_ref, d_ref, w_ref, bg_ref, y_ref, h_ref, y0_ref, edge, tab, bu_ref):
        @pl.when(pl.program_id(0) == 0)
        def _():
            edge[...] = jnp.zeros_like(edge)
            _s5_scan_table(tab, a_ref[0:1, 0:n], a_ref[0:1, n:2 * n], False)

        u = u_ref[...]
        bu_ref[...] = _dot(_bf(u), b_ref[...])
        _s5_scan(bu_ref, h_ref, tab, edge, tb, False)
        y0 = _dot(_bf(h_ref[:, 0:n]), c_ref[0:n, :]) + _dot(_bf(h_ref[:, n:2 * n]), c_ref[n:2 * n, :]) + d_ref[...] * u
        y0_ref[...] = y0
        yg = _gelu(y0)
        z = _dot(_bf(yg), w_ref[...]) + bg_ref[...]
        y_ref[...] = _bf(yg * _sigmoid(z))

    vec = pl.BlockSpec((1, D_GROUP), lambda i: (0, 0))
    return pl.pallas_call(
        body, name=name, grid=(s // tb,),
        in_specs=[pl.BlockSpec((tb, D_GROUP), lambda i: (i, 0)),
                  pl.BlockSpec((1, 2 * n), lambda i: (0, 0)),
                  pl.BlockSpec((D_GROUP, 2 * n), lambda i: (0, 0)),
                  pl.BlockSpec((2 * n, D_GROUP), lambda i: (0, 0)),
                  vec,
                  pl.BlockSpec((None, D_GROUP, D_GROUP), lambda i: (layer, 0, 0)),
                  vec],
        out_specs=[pl.BlockSpec((tb, D_GROUP), lambda i: (i, 0)),
                   pl.BlockSpec((tb, 2 * n), lambda i: (i, 0)),
                   pl.BlockSpec((tb, D_GROUP), lambda i: (i, 0))],
        out_shape=[jax.ShapeDtypeStruct((s, D_GROUP), BF16), jax.ShapeDtypeStruct((s, 2 * n), F32),
                   jax.ShapeDtypeStruct((s, D_GROUP), F32)],
        scratch_shapes=[pltpu.VMEM((SUBLANES, 2 * n), F32), pltpu.VMEM((S5_TAB_ROWS, n), F32),
                        pltpu.VMEM((tb, 2 * n), F32)],
        compiler_params=_cparams(1),
    )(h_in, a2, bexp, cexp, dskip, wglu, bglu)


def s5_bwd(dmix, h_in, y0, hst, a2, bexp, cexp, dskip, wglu, layer, bglu, tb, name):
    s = h_in.shape[0]
    n = N_STATE
    nb = s // tb
    halo = tb // 8

    def body(dy_ref, u_ref, y0_ref, h_ref, hp_ref, a_ref, b_ref, c_ref, d_ref, w_ref, bg_ref,
             du_ref, cs_ref, db_ref, dc_ref, dd_ref, dw_ref, dbg_ref, da_ref, edge, tab, g_ref, da_acc):
        i = pl.program_id(0)
        first = i == 0

        @pl.when(first)
        def _():
            edge[...] = jnp.zeros_like(edge)
            da_acc[...] = jnp.zeros_like(da_acc)
            _s5_scan_table(tab, a_ref[0:1, 0:n], -a_ref[0:1, n:2 * n], True)

        dy = dy_ref[...]
        u = u_ref[...]
        y0v = y0_ref[...]
        yg, t = _gelu_parts(y0v)
        z = _dot(_bf(yg), w_ref[...]) + bg_ref[...]
        sg = _sigmoid(z)
        dz = dy * yg * sg * (1.0 - sg)
        dyg = dy * sg + _dot_nt(_bf(dz), w_ref[...])
        _acc(dw_ref, _dot_tn(_bf(yg), _bf(dz)), first)
        _acc(dbg_ref, _colsum(dz), first)
        dy0 = dyg * _gelu_grad(y0v, t)
        _acc(dd_ref, _colsum(dy0 * u), first)
        dy0b = _bf(dy0)
        _acc(dc_ref.at[0:n, :], _dot_tn(_bf(h_ref[:, 0:n]), dy0b), first)
        _acc(dc_ref.at[n:2 * n, :], _dot_tn(_bf(h_ref[:, n:2 * n]), dy0b), first)
        g_ref[...] = _dot_nt(dy0b, c_ref[...])
        keep = jnp.where(i == nb - 1, 0.0, 1.0)
        row0 = _rows((SUBLANES, LANES)) == 0

        def grad_a(g, q, cols, gr, gi):
            cre, cim = cols
            rows = pl.ds(pl.multiple_of(g * SUBLANES, SUBLANES), SUBLANES)
            before = pl.ds(pl.multiple_of(jnp.maximum(g - 1, 0) * SUBLANES, SUBLANES), SUBLANES)
            pre = jnp.where(g == 0, hp_ref[:, cre] * keep, h_ref[before, cre])
            pim = jnp.where(g == 0, hp_ref[:, cim] * keep, h_ref[before, cim])
            pr = jnp.where(row0, pltpu.roll(pre, 1, 0), pltpu.roll(h_ref[rows, cre], 1, 0))
            pi = jnp.where(row0, pltpu.roll(pim, 1, 0), pltpu.roll(h_ref[rows, cim], 1, 0))
            da_acc[:, cre] += gr * pr + gi * pi
            da_acc[:, cim] += gi * pr - gr * pi

        _s5_scan(g_ref, g_ref, tab, edge, tb, True, grad_a)
        da_ref[...] = _colsum(da_acc[...])
        gr, gi = g_ref[:, 0:n], g_ref[:, n:2 * n]
        grb, gib = _bf(gr), _bf(gi)
        du = d_ref[...] * dy0 + _dot_nt(grb, b_ref[:, 0:n]) + _dot_nt(gib, b_ref[:, n:2 * n])
        ub = _bf(u)
        _acc(db_ref.at[:, 0:n], _dot_tn(ub, grb), first)
        _acc(db_ref.at[:, n:2 * n], _dot_tn(ub, gib), first)
        du_ref[...] = _bf(du)
        _acc(cs_ref, _colsum(du), first)

    rev = lambda i: (nb - 1 - i, 0)
    vec = pl.BlockSpec((1, D_GROUP), lambda i: (0, 0))
    vshape = jax.ShapeDtypeStruct((1, D_GROUP), F32)
    return pl.pallas_call(
        body, name=name, grid=(nb,),
        in_specs=[pl.BlockSpec((tb, D_GROUP), rev),
                  pl.BlockSpec((tb, D_GROUP), rev),
                  pl.BlockSpec((tb, D_GROUP), rev),
                  pl.BlockSpec((tb, 2 * n), rev),
                  pl.BlockSpec((8, 2 * n), lambda i: (jnp.maximum((nb - 1 - i) * halo - 1, 0), 0)),
                  pl.BlockSpec((1, 2 * n), lambda i: (0, 0)),
                  pl.BlockSpec((D_GROUP, 2 * n), lambda i: (0, 0)),
                  pl.BlockSpec((2 * n, D_GROUP), lambda i: (0, 0)),
                  vec,
                  pl.BlockSpec((None, D_GROUP, D_GROUP), lambda i: (layer, 0, 0)),
                  vec],
        out_specs=[pl.BlockSpec((tb, D_GROUP), rev), vec,
                   pl.BlockSpec((D_GROUP, 2 * n), lambda i: (0, 0)),
                   pl.BlockSpec((2 * n, D_GROUP), lambda i: (0, 0)),
                   vec,
                   pl.BlockSpec((D_GROUP, D_GROUP), lambda i: (0, 0)),
                   vec,
                   pl.BlockSpec((1, 2 * n), lambda i: (0, 0))],
        out_shape=[jax.ShapeDtypeStruct((s, D_GROUP), BF16), vshape,
                   jax.ShapeDtypeStruct((D_GROUP, 2 * n), F32), jax.ShapeDtypeStruct((2 * n, D_GROUP), F32),
                   vshape, jax.ShapeDtypeStruct((D_GROUP, D_GROUP), F32), vshape,
                   jax.ShapeDtypeStruct((1, 2 * n), F32)],
        scratch_shapes=[pltpu.VMEM((SUBLANES, 2 * n), F32), pltpu.VMEM((S5_TAB_ROWS, n), F32),
                        pltpu.VMEM((tb, 2 * n), F32), pltpu.VMEM((SUBLANES, 2 * n), F32)],
        compiler_params=_cparams(1),
    )(dmix, h_in, y0, hst, hst, a2, bexp, cexp, dskip, wglu, bglu)


CV_HALO = 32


def _gn_stats(hc, pmat):
    mu = _dot3(hc, pmat)
    xc = hc - mu
    var = _dot3(xc * xc, pmat)
    rstd = lax.rsqrt(var + LN_EPS)
    return xc * rstd, rstd


def cv_fwd(h_in, cw, cb, gg, gb, pmat, wpw, layer, bpw, tb, name):
    s = h_in.shape[0]
    hl = CV_HALO

    def body(v_ref, g_ref, cw_ref, cb_ref, gg_ref, gb_ref, p_ref, w_ref, bw_ref, y_ref, hc_ref, ext):
        @pl.when(pl.program_id(0) == 0)
        def _():
            ext[0:hl, :] = jnp.zeros((hl, D_GROUP), F32)

        ext[hl:hl + tb, :] = v_ref[...] * _sigmoid(g_ref[...])
        acc = jnp.zeros((tb, D_GROUP), F32) + cb_ref[...]
        for k in range(CONV_WIDTH):
            off = hl - (CONV_WIDTH - 1) + k
            acc = acc + cw_ref[k:k + 1, :] * ext[off:off + tb, :]
        hc_ref[...] = acc
        ext[0:hl, :] = ext[tb:tb + hl, :]
        xn, _ = _gn_stats(acc, p_ref[...])
        hn = xn * gg_ref[...] + gb_ref[...]
        hs = hn * _sigmoid(hn)
        y_ref[...] = _bf(_dot(_bf(hs), w_ref[...]) + bw_ref[...])

    vec = pl.BlockSpec((1, D_GROUP), lambda i: (0, 0))
    sq = pl.BlockSpec((D_GROUP, D_GROUP), lambda i: (0, 0))
    return pl.pallas_call(
        body, name=name, grid=(s // tb,),
        in_specs=[pl.BlockSpec((tb, D_GROUP), lambda i: (i, 1)),
                  pl.BlockSpec((tb, D_GROUP), lambda i: (i, 2)),
                  pl.BlockSpec((hl, D_GROUP), lambda i: (0, 0)),
                  vec, vec, vec, sq,
                  pl.BlockSpec((None, D_GROUP, D_GROUP), lambda i: (layer, 0, 0)),
                  vec],
        out_specs=[pl.BlockSpec((tb, D_GROUP), lambda i: (i, 0)), pl.BlockSpec((tb, D_GROUP), lambda i: (i, 0))],
        out_shape=[jax.ShapeDtypeStruct((s, D_GROUP), BF16), jax.ShapeDtypeStruct((s, D_GROUP), F32)],
        scratch_shapes=[pltpu.VMEM((hl + tb, D_GROUP), F32)],
        compiler_params=_cparams(1),
    )(h_in, h_in, cw, cb, gg, gb, pmat, wpw, bpw)


def cv_bwd(dmix, h_in, hc, cw, gg, gb, pmat, wpw, layer, tb, name):
    s = h_in.shape[0]
    hl = CV_HALO
    nb = s // tb
    per = tb // hl

    def body(dy_ref, v_ref, g_ref, vh_ref, gh_ref, hc_ref, cw_ref, gg_ref, gb_ref, p_ref, w_ref,
             dvg_ref, cs_ref, dcw_ref, dcb_ref, dgg_ref, dgb_ref, dw_ref, dbw_ref, ext, dext, head):
        i = pl.program_id(0)
        first = i == 0

        @pl.when(first)
        def _():
            head[...] = jnp.zeros_like(head)

        dy = dy_ref[...]
        pm = p_ref[...]
        xn, rstd = _gn_stats(hc_ref[...], pm)
        hn = xn * gg_ref[...] + gb_ref[...]
        sg = _sigmoid(hn)
        hs = hn * sg
        dyb = _bf(dy)
        _acc(dbw_ref, _colsum(dy), first)
        _acc(dw_ref, _dot_tn(_bf(hs), dyb), first)
        dhs = _dot_nt(dyb, w_ref[...])
        dhn = dhs * sg * (1.0 + hn * (1.0 - sg))
        _acc(dgg_ref, _colsum(dhn * xn), first)
        _acc(dgb_ref, _colsum(dhn), first)
        dxn = dhn * gg_ref[...]
        dhc = rstd * (dxn - _dot3(dxn, pm) - xn * _dot3(dxn * xn, pm))
        _acc(dcb_ref, _colsum(dhc), first)
        v = v_ref[...]
        sgg = _sigmoid(g_ref[...])
        keep = jnp.where(i == nb - 1, 0.0, 1.0)
        ext[0:hl, :] = vh_ref[...] * _sigmoid(gh_ref[...]) * keep
        ext[hl:hl + tb, :] = v * sgg
        dext[0:tb, :] = dhc
        dext[tb:tb + hl, :] = head[...]
        head[...] = dhc[0:hl]
        dhg = jnp.zeros((tb, D_GROUP), F32)
        for k in range(CONV_WIDTH):
            off = hl - (CONV_WIDTH - 1) + k
            wk = _colsum(dhc * ext[off:off + tb, :])
            _acc(dcw_ref.at[k:k + 1, :], wk, first)
            back = CONV_WIDTH - 1 - k
            dhg = dhg + cw_ref[k:k + 1, :] * dext[back:back + tb, :]

        @pl.when(first)
        def _():
            dcw_ref[CONV_WIDTH:hl, :] = jnp.zeros((hl - CONV_WIDTH, D_GROUP), F32)

        dv = dhg * sgg
        dg = dhg * v * sgg * (1.0 - sgg)
        dvg_ref[:, 0:D_GROUP] = _bf(dv)
        dvg_ref[:, D_GROUP:2 * D_GROUP] = _bf(dg)
        _acc(cs_ref.at[:, 0:D_GROUP], _colsum(dv), first)
        _acc(cs_ref.at[:, D_GROUP:2 * D_GROUP], _colsum(dg), first)

    vec = pl.BlockSpec((1, D_GROUP), lambda i: (0, 0))
    sq = pl.BlockSpec((D_GROUP, D_GROUP), lambda i: (0, 0))
    tap = pl.BlockSpec((hl, D_GROUP), lambda i: (0, 0))
    vshape = jax.ShapeDtypeStruct((1, D_GROUP), F32)

    def blk(col):
        return pl.BlockSpec((tb, D_GROUP), lambda i: (nb - 1 - i, col))

    def halo_blk(col):
        return pl.BlockSpec((hl, D_GROUP), lambda i: (jnp.maximum((nb - 1 - i) * per - 1, 0), col))

    return pl.pallas_call(
        body, name=name, grid=(nb,),
        in_specs=[blk(1), blk(1), blk(2), halo_blk(1), halo_blk(2),
                  pl.BlockSpec((tb, D_GROUP), lambda i: (nb - 1 - i, 0)),
                  tap, vec, vec, sq,
                  pl.BlockSpec((None, D_GROUP, D_GROUP), lambda i: (layer, 0, 0))],
        out_specs=[pl.BlockSpec((tb, 2 * D_GROUP), lambda i: (nb - 1 - i, 0)),
                   pl.BlockSpec((1, 2 * D_GROUP), lambda i: (0, 0)),
                   tap, vec, vec, vec, sq, vec],
        out_shape=[jax.ShapeDtypeStruct((s, 2 * D_GROUP), BF16), jax.ShapeDtypeStruct((1, 2 * D_GROUP), F32),
                   jax.ShapeDtypeStruct((hl, D_GROUP), F32), vshape, vshape, vshape,
                   jax.ShapeDtypeStruct((D_GROUP, D_GROUP), F32), vshape],
        scratch_shapes=[pltpu.VMEM((hl + tb, D_GROUP), F32), pltpu.VMEM((tb + hl, D_GROUP), F32),
                        pltpu.VMEM((hl, D_GROUP), F32)],
        compiler_params=_cparams(1),
    )(dmix, h_in, h_in, h_in, h_in, hc, cw, gg, gb, pmat, wpw)


LRU_HALO = 8


def _lru_gates(xc, wr_ref, br_ref, wi_ref, bi_ref, sp_ref):
    xcb = _bf(xc)
    r = _sigmoid(_dot(xcb, wr_ref[...]) + br_ref[...])
    gi = _sigmoid(_dot(xcb, wi_ref[...]) + bi_ref[...])
    la = -LRU_C * r * sp_ref[...]
    a = jnp.exp(la)
    e2 = a * a
    sq = jnp.sqrt(-jnp.tanh(la) * (e2 + 1.0))
    return r, gi, a, e2, sq


def _rscan(a, b, tb, reverse):
    row = _rows(a.shape)
    sh = 1
    while sh < tb:
        if reverse:
            amt, mask = tb - sh, row < tb - sh
        else:
            amt, mask = sh, row >= sh
        a_s = jnp.where(mask, pltpu.roll(a, amt, 0), 1.0)
        b_s = jnp.where(mask, pltpu.roll(b, amt, 0), 0.0)
        b = b + a * b_s
        a = a * a_s
        sh *= 2
    return a, b


def lru_fwd(h_in, cw, cb, wr, br, wi, bi, sp, tb, name):
    s = h_in.shape[0]
    hl = LRU_HALO

    def body(xg_ref, xr_ref, cw_ref, cb_ref, wr_ref, br_ref, wi_ref, bi_ref, sp_ref, y_ref, xc_ref, h_ref, ext, carry):
        @pl.when(pl.program_id(0) == 0)
        def _():
            ext[0:hl, :] = jnp.zeros((hl, D_GROUP), F32)
            carry[...] = jnp.zeros_like(carry)

        ext[hl:hl + tb, :] = xr_ref[...]
        xc = jnp.zeros((tb, D_GROUP), F32) + cb_ref[...]
        for k in range(LRU_CONV_WIDTH):
            off = hl - (LRU_CONV_WIDTH - 1) + k
            xc = xc + cw_ref[k:k + 1, :] * ext[off:off + tb, :]
        xc_ref[...] = xc
        ext[0:hl, :] = ext[tb:tb + hl, :]
        r, gi, a, e2, sq = _lru_gates(xc, wr_ref, br_ref, wi_ref, bi_ref, sp_ref)
        pa, hloc = _rscan(a, sq * (gi * xc), tb, False)
        h = hloc + pa * carry[7:8, :]
        h_ref[...] = h
        carry[...] = h[tb - 8:tb]
        y_ref[...] = _bf(h * _gelu(xg_ref[...]))

    vec = pl.BlockSpec((1, D_GROUP), lambda i: (0, 0))
    sq_spec = pl.BlockSpec((D_GROUP, D_GROUP), lambda i: (0, 0))
    blk = pl.BlockSpec((tb, D_GROUP), lambda i: (i, 0))
    return pl.pallas_call(
        body, name=name, grid=(s // tb,),
        in_specs=[pl.BlockSpec((tb, D_GROUP), lambda i: (i, 3)),
                  pl.BlockSpec((tb, D_GROUP), lambda i: (i, 4)),
                  pl.BlockSpec((hl, D_GROUP), lambda i: (0, 0)),
                  vec, sq_spec, vec, sq_spec, vec, vec],
        out_specs=[blk, blk, blk],
        out_shape=[jax.ShapeDtypeStruct((s, D_GROUP), BF16), jax.ShapeDtypeStruct((s, D_GROUP), F32),
                   jax.ShapeDtypeStruct((s, D_GROUP), F32)],
        scratch_shapes=[pltpu.VMEM((hl + tb, D_GROUP), F32), pltpu.VMEM((8, D_GROUP), F32)],
        compiler_params=_cparams(1),
    )(h_in, h_in, cw, cb, wr, br, wi, bi, sp)


def lru_bwd(dmix, h_in, xcs, hs, cw, wr, br, wi, bi, sp, tb, name):
    s = h_in.shape[0]
    hl = LRU_HALO
    nb = s // tb
    per = tb // hl

    def body(dy_ref, xg_ref, xr_ref, xrh_ref, xc_ref, h_ref, hp_ref, cw_ref, wr_ref, br_ref, wi_ref, bi_ref, sp_ref,
             dx_ref, cs_ref, dcw_ref, dcb_ref, dwr_ref, dbr_ref, dwi_ref, dbi_ref, dsp_ref,
             ext, dext, head, anext, gnext):
        i = pl.program_id(0)
        first = i == 0

        @pl.when(first)
        def _():
            head[...] = jnp.zeros_like(head)
            anext[...] = jnp.zeros_like(anext)
            gnext[...] = jnp.zeros_like(gnext)

        dy = dy_ref[...]
        xg = xg_ref[...]
        xc = xc_ref[...]
        h = h_ref[...]
        r, gi, a, e2, sq = _lru_gates(xc, wr_ref, br_ref, wi_ref, bi_ref, sp_ref)
        gate, t = _gelu_parts(xg)
        dh = dy * gate
        dxg = dy * h * _gelu_grad(xg, t)
        row = _rows((tb, D_GROUP))
        coef = jnp.where(row == tb - 1, anext[0:1, :], pltpu.roll(a, tb - 1, 0))
        pc, gloc = _rscan(coef, dh, tb, True)
        gfull = gloc + pc * gnext[0:1, :]
        anext[...] = a[0:8]
        gnext[...] = gfull[0:8]
        keep = jnp.where(i == nb - 1, 0.0, 1.0)
        hprev = jnp.where(row == 0, hp_ref[7:8, :] * keep, pltpu.roll(h, 1, 0))
        da = gfull * hprev
        uu = gi * xc
        dsq = gfull * uu
        duu = gfull * sq
        dla = da * a - dsq * e2 / sq
        sp = sp_ref[...]
        dr = dla * (-LRU_C) * sp
        _acc(dsp_ref, _colsum(dla * (-LRU_C) * r), first)
        dzr = dr * r * (1.0 - r)
        dzi = duu * xc * gi * (1.0 - gi)
        dzrb, dzib = _bf(dzr), _bf(dzi)
        dxc = duu * gi + _dot_nt(dzrb, wr_ref[...]) + _dot_nt(dzib, wi_ref[...])
        xcb = _bf(xc)
        _acc(dwr_ref, _dot_tn(xcb, dzrb), first)
        _acc(dwi_ref, _dot_tn(xcb, dzib), first)
        _acc(dbr_ref, _colsum(dzr), first)
        _acc(dbi_ref, _colsum(dzi), first)
        _acc(dcb_ref, _colsum(dxc), first)
        ext[0:hl, :] = xrh_ref[...] * keep
        ext[hl:hl + tb, :] = xr_ref[...]
        dext[0:tb, :] = dxc
        dext[tb:tb + hl, :] = head[...]
        head[...] = dxc[0:hl]
        dxr = jnp.zeros((tb, D_GROUP), F32)
        for k in range(LRU_CONV_WIDTH):
            off = hl - (LRU_CONV_WIDTH - 1) + k
            _acc(dcw_ref.at[k:k + 1, :], _colsum(dxc * ext[off:off + tb, :]), first)
            back = LRU_CONV_WIDTH - 1 - k
            dxr = dxr + cw_ref[k:k + 1, :] * dext[back:back + tb, :]

        @pl.when(first)
        def _():
            dcw_ref[LRU_CONV_WIDTH:hl, :] = jnp.zeros((hl - LRU_CONV_WIDTH, D_GROUP), F32)

        dx_ref[:, 0:D_GROUP] = _bf(dxg)
        dx_ref[:, D_GROUP:2 * D_GROUP] = _bf(dxr)
        _acc(cs_ref.at[:, 0:D_GROUP], _colsum(dxg), first)
        _acc(cs_ref.at[:, D_GROUP:2 * D_GROUP], _colsum(dxr), first)

    vec = pl.BlockSpec((1, D_GROUP), lambda i: (0, 0))
    sq_spec = pl.BlockSpec((D_GROUP, D_GROUP), lambda i: (0, 0))
    tap = pl.BlockSpec((hl, D_GROUP), lambda i: (0, 0))
    vshape = jax.ShapeDtypeStruct((1, D_GROUP), F32)
    sshape = jax.ShapeDtypeStruct((D_GROUP, D_GROUP), F32)

    def blk(col):
        return pl.BlockSpec((tb, D_GROUP), lambda i: (nb - 1 - i, col))

    def halo_blk(col):
        return pl.BlockSpec((hl, D_GROUP), lambda i: (jnp.maximum((nb - 1 - i) * per - 1, 0), col))

    return pl.pallas_call(
        body, name=name, grid=(nb,),
        in_specs=[blk(2), blk(3), blk(4), halo_blk(4), blk(0), blk(0), halo_blk(0),
                  tap, sq_spec, vec, sq_spec, vec, vec],
        out_specs=[pl.BlockSpec((tb, 2 * D_GROUP), lambda i: (nb - 1 - i, 0)),
                   pl.BlockSpec((1, 2 * D_GROUP), lambda i: (0, 0)),
                   tap, vec, sq_spec, vec, sq_spec, vec, vec],
        out_shape=[jax.ShapeDtypeStruct((s, 2 * D_GROUP), BF16), jax.ShapeDtypeStruct((1, 2 * D_GROUP), F32),
                   jax.ShapeDtypeStruct((hl, D_GROUP), F32), vshape, sshape, vshape, sshape, vshape, vshape],
        scratch_shapes=[pltpu.VMEM((hl + tb, D_GROUP), F32), pltpu.VMEM((tb + hl, D_GROUP), F32),
                        pltpu.VMEM((hl, D_GROUP), F32), pltpu.VMEM((8, D_GROUP), F32), pltpu.VMEM((8, D_GROUP), F32)],
        compiler_params=_cparams(1),
    )(dmix, h_in, h_in, h_in, xcs, hs, hs, cw, wr, br, wi, bi, sp)


ATTN_HEADS = 4
ATTN_HEAD_DIM = 64
ATTN_SCALE = ATTN_HEAD_DIM ** -0.5


def _head_mask(h):
    lane = lax.broadcasted_iota(jnp.int32, (1, D_GROUP), 1)
    return jnp.where((lane >= h * ATTN_HEAD_DIM) & (lane < (h + 1) * ATTN_HEAD_DIM), 1.0, 0.0)


def _softmax_rows(sc):
    e = jnp.exp(sc - jnp.max(sc, -1, keepdims=True))
    return e / jnp.sum(e, -1, keepdims=True)


def attn_fwd(h_in, kv, tb, name):
    s = h_in.shape[0]

    def body(q_ref, kv_ref, y_ref):
        q = q_ref[...]
        kb = _bf(kv_ref[:, 0:D_GROUP])
        vb = _bf(kv_ref[:, D_GROUP:2 * D_GROUP])
        out = jnp.zeros((tb, D_GROUP), F32)
        for h in range(ATTN_HEADS):
            mask = _head_mask(h)
            p = _softmax_rows(_dot_nt(_bf(q * mask), kb) * ATTN_SCALE)
            out = out + _dot(_bf(p), vb) * mask
        y_ref[...] = _bf(out)

    return pl.pallas_call(
        body, name=name, grid=(s // tb,),
        in_specs=[pl.BlockSpec((tb, D_GROUP), lambda i: (i, 5)),
                  pl.BlockSpec((D_GROUP, 2 * D_GROUP), lambda i: (0, 0))],
        out_specs=pl.BlockSpec((tb, D_GROUP), lambda i: (i, 0)),
        out_shape=jax.ShapeDtypeStruct((s, D_GROUP), BF16),
        compiler_params=_cparams(1),
    )(h_in, kv)


def attn_bwd(dmix, h_in, kv, tb, name):
    s = h_in.shape[0]

    def body(do_ref, q_ref, kv_ref, dq_ref, cs_ref, dkv_ref):
        first = pl.program_id(0) == 0
        q = q_ref[...]
        do = do_ref[...]
        kb = _bf(kv_ref[:, 0:D_GROUP])
        vb = _bf(kv_ref[:, D_GROUP:2 * D_GROUP])
        dq = jnp.zeros((tb, D_GROUP), F32)
        dk = jnp.zeros((D_GROUP, D_GROUP), F32)
        dv = jnp.zeros((D_GROUP, D_GROUP), F32)
        for h in range(ATTN_HEADS):
            mask = _head_mask(h)
            qm = _bf(q * mask)
            p = _softmax_rows(_dot_nt(qm, kb) * ATTN_SCALE)
            dom = _bf(do * mask)
            dp = _dot_nt(dom, vb)
            dv = dv + _dot_tn(_bf(p), dom)
            ds = _bf(p * (dp - jnp.sum(dp * p, -1, keepdims=True)) * ATTN_SCALE)
            dq = dq + _dot(ds, kb) * mask
            dk = dk + _dot_tn(ds, qm)
        dq_ref[...] = _bf(dq)
        _acc(cs_ref, _colsum(dq), first)
        _acc(dkv_ref.at[:, 0:D_GROUP], dk, first)
        _acc(dkv_ref.at[:, D_GROUP:2 * D_GROUP], dv, first)

    return pl.pallas_call(
        body, name=name, grid=(s // tb,),
        in_specs=[pl.BlockSpec((tb, D_GROUP), lambda i: (i, 3)),
                  pl.BlockSpec((tb, D_GROUP), lambda i: (i, 5)),
                  pl.BlockSpec((D_GROUP, 2 * D_GROUP), lambda i: (0, 0))],
        out_specs=[pl.BlockSpec((tb, D_GROUP), lambda i: (i, 0)),
                   pl.BlockSpec((1, D_GROUP), lambda i: (0, 0)),
                   pl.BlockSpec((D_GROUP, 2 * D_GROUP), lambda i: (0, 0))],
        out_shape=[jax.ShapeDtypeStruct((s, D_GROUP), BF16), jax.ShapeDtypeStruct((1, D_GROUP), F32),
                   jax.ShapeDtypeStruct((D_GROUP, 2 * D_GROUP), F32)],
        compiler_params=_cparams(1),
    )(dmix, h_in, kv)


FFN_RB = 16
FFN_UNROLL_FWD = 4
FFN_UNROLL_BWD = 2
FFN_TAP_ROWS = 8
FFN_TN = D_FF // 2


def _shift_down(cur, tail, k):
    return pltpu.roll(jnp.concatenate([tail, cur], axis=0), k, 0)[SUBLANES:]


def _shift_up(cur, head, k):
    rb = cur.shape[0]
    return pltpu.roll(jnp.concatenate([cur, head], axis=0), rb + SUBLANES - k, 0)[:rb]


def _fold8(v):
    tot = v[0:SUBLANES]
    for t in range(1, v.shape[0] // SUBLANES):
        tot = tot + v[t * SUBLANES:(t + 1) * SUBLANES]
    return tot


def _strip(r):
    return pl.ds(pl.multiple_of(r * FFN_RB, FFN_RB), FFN_RB)


def ffn_act_fwd(u, cw, cb, tb, name):
    s = u.shape[0]
    rb = FFN_RB
    nct = D_FF // FFN_TN
    nstrip = tb // rb

    def body(uv_ref, ug_ref, wv_ref, wg_ref, bv_ref, bg_ref, hf_ref, keep_ref, tailv, tailg):
        @pl.when(pl.program_id(1) == 0)
        def _():
            tailv[...] = jnp.zeros_like(tailv)
            tailg[...] = jnp.zeros_like(tailg)

        for cc in range(FFN_TN // LANES):
            cols = slice(cc * LANES, (cc + 1) * LANES)
            wv = [wv_ref[k:k + 1, cols] for k in range(FFN_CONV_WIDTH)]
            wg = [wg_ref[k:k + 1, cols] for k in range(FFN_CONV_WIDTH)]
            bv, bg = bv_ref[:, cols], bg_ref[:, cols]

            def strip(r, carry):
                tail_v, tail_g = carry
                cur_v, cur_g = uv_ref[_strip(r), cols], ug_ref[_strip(r), cols]
                vc = wv[0] * _shift_down(cur_v, tail_v, 2) + wv[1] * _shift_down(cur_v, tail_v, 1) + wv[2] * cur_v + bv
                gc = wg[0] * _shift_down(cur_g, tail_g, 2) + wg[1] * _shift_down(cur_g, tail_g, 1) + wg[2] * cur_g + bg
                ge, t = _gelu_parts(gc)
                hf_ref[_strip(r), cols] = _bf(vc * ge)
                keep_ref[0, _strip(r), cols] = _bf(vc)
                keep_ref[1, _strip(r), cols] = _bf(ge)
                keep_ref[2, _strip(r), cols] = _bf(_gelu_grad(gc, t))
                return cur_v[rb - SUBLANES:], cur_g[rb - SUBLANES:]

            def strips(q, carry):
                for k in range(FFN_UNROLL_FWD):
                    carry = strip(q * FFN_UNROLL_FWD + k, carry)
                return carry

            last_v, last_g = lax.fori_loop(0, nstrip // FFN_UNROLL_FWD, strips, (tailv[:, cols], tailg[:, cols]))
            tailv[:, cols] = last_v
            tailg[:, cols] = last_g

    return pl.pallas_call(
        body, name=name, grid=(nct, s // tb),
        in_specs=[pl.BlockSpec((tb, FFN_TN), lambda c, i: (i, c)),
                  pl.BlockSpec((tb, FFN_TN), lambda c, i: (i, c + nct)),
                  pl.BlockSpec((FFN_TAP_ROWS, FFN_TN), lambda c, i: (0, c)),
                  pl.BlockSpec((FFN_TAP_ROWS, FFN_TN), lambda c, i: (0, c + nct)),
                  pl.BlockSpec((1, FFN_TN), lambda c, i: (0, c)),
                  pl.BlockSpec((1, FFN_TN), lambda c, i: (0, c + nct))],
        out_specs=[pl.BlockSpec((tb, FFN_TN), lambda c, i: (i, c)),
                   pl.BlockSpec((3, tb, FFN_TN), lambda c, i: (0, i, c))],
        out_shape=[jax.ShapeDtypeStruct((s, D_FF), BF16), jax.ShapeDtypeStruct((3, s, D_FF), BF16)],
        scratch_shapes=[pltpu.VMEM((SUBLANES, FFN_TN), F32), pltpu.VMEM((SUBLANES, FFN_TN), F32)],
        compiler_params=_cparams(2),
    )(u, u, cw, cw, cb, cb)


def ffn_act_bwd(dhf, u, kept, cw, tb, name):
    s = u.shape[0]
    rb = FFN_RB
    nct = D_FF // FFN_TN
    nb = s // tb
    nstrip = tb // rb
    ntap = FFN_CONV_WIDTH

    def body(dh_ref, uv_ref, ug_ref, kept_ref, wv_ref, wg_ref, du_ref, dwv_ref, dwg_ref, headv, headg):
        i = pl.program_id(1)
        first = i == 0

        @pl.when(first)
        def _():
            headv[...] = jnp.zeros_like(headv)
            headg[...] = jnp.zeros_like(headg)
            dwv_ref[...] = jnp.zeros_like(dwv_ref)
            dwg_ref[...] = jnp.zeros_like(dwg_ref)

        zero = jnp.zeros((SUBLANES, LANES), F32)
        for cc in range(FFN_TN // LANES):
            cols = slice(cc * LANES, (cc + 1) * LANES)
            wv = [wv_ref[k:k + 1, cols] for k in range(ntap)]
            wg = [wg_ref[k:k + 1, cols] for k in range(ntap)]

            def strip(ii, carry):
                head_dv, head_dg, acc_v, acc_g = carry
                r = nstrip - 1 - ii
                dh = dh_ref[_strip(r), cols]
                dvc = dh * kept_ref[1, _strip(r), cols].astype(F32)
                dgc = dh * kept_ref[0, _strip(r), cols].astype(F32) * kept_ref[2, _strip(r), cols].astype(F32)
                sdv = [_shift_up(dvc, head_dv, 2), _shift_up(dvc, head_dv, 1), dvc]
                sdg = [_shift_up(dgc, head_dg, 2), _shift_up(dgc, head_dg, 1), dgc]
                cur_v, cur_g = uv_ref[_strip(r), cols], ug_ref[_strip(r), cols]
                acc_v = tuple(acc_v[k] + _fold8(cur_v * sdv[k]) for k in range(ntap)) + (acc_v[ntap] + _fold8(dvc),)
                acc_g = tuple(acc_g[k] + _fold8(cur_g * sdg[k]) for k in range(ntap)) + (acc_g[ntap] + _fold8(dgc),)
                du_v = wv[0] * sdv[0] + wv[1] * sdv[1] + wv[2] * sdv[2]
                du_g = wg[0] * sdg[0] + wg[1] * sdg[1] + wg[2] * sdg[2]
                du_ref[0, _strip(r), cols] = _bf(du_v)
                du_ref[1, _strip(r), cols] = _bf(du_g)
                return dvc[0:SUBLANES], dgc[0:SUBLANES], acc_v, acc_g

            init = (headv[:, cols], headg[:, cols], (zero,) * (ntap + 1), (zero,) * (ntap + 1))
            def strips(q, carry):
                for k in range(FFN_UNROLL_BWD):
                    carry = strip(q * FFN_UNROLL_BWD + k, carry)
                return carry

            top_dv, top_dg, acc_v, acc_g = lax.fori_loop(0, nstrip // FFN_UNROLL_BWD, strips, init)
            headv[:, cols] = top_dv
            headg[:, cols] = top_dg
            for k in range(ntap + 1):
                dwv_ref[k:k + 1, cols] += _colsum(acc_v[k])
                dwg_ref[k:k + 1, cols] += _colsum(acc_g[k])

    def blk(shift):
        return pl.BlockSpec((tb, FFN_TN), lambda c, i: (nb - 1 - i, c + shift))

    tapv = pl.BlockSpec((FFN_TAP_ROWS, FFN_TN), lambda c, i: (0, c))
    tapg = pl.BlockSpec((FFN_TAP_ROWS, FFN_TN), lambda c, i: (0, c + nct))
    return pl.pallas_call(
        body, name=name, grid=(nct, nb),
        in_specs=[blk(0), blk(0), blk(nct), pl.BlockSpec((3, tb, FFN_TN), lambda c, i: (0, nb - 1 - i, c)), tapv, tapg],
        out_specs=[pl.BlockSpec((2, tb, FFN_TN), lambda c, i: (0, nb - 1 - i, c)), tapv, tapv],
        out_shape=[jax.ShapeDtypeStruct((2, s, D_FF), BF16), jax.ShapeDtypeStruct((FFN_TAP_ROWS, D_FF), F32),
                   jax.ShapeDtypeStruct((FFN_TAP_ROWS, D_FF), F32)],
        scratch_shapes=[pltpu.VMEM((SUBLANES, FFN_TN), F32), pltpu.VMEM((SUBLANES, FFN_TN), F32)],
        compiler_params=_cparams(2),
    )(dhf, u, u, kept, cw, cw)


def _place():
    x, y, c = lax.axis_index("x"), lax.axis_index("y"), lax.axis_index("c")
    return x, y, c, 2 * x + y


def _chip_peer(x, y, d):
    return jnp.bitwise_xor(x, d >> 1), jnp.bitwise_xor(y, d & 1)


def gather_weights(slabs):
    n = len(slabs)

    def body(*refs):
        outs = refs[n:2 * n]
        ssem, rsem, fsem, gsem = refs[2 * n:]
        x, y, c, j = _place()
        sib = (x, y, 1 - c)
        ici = {}
        for w in range(n):
            for d in (1, 2, 3):
                px, py = _chip_peer(x, y, d)
                cp = pltpu.make_async_remote_copy(
                    src_ref=outs[w].at[c, j], dst_ref=outs[w].at[c, j], send_sem=ssem.at[w, d - 1],
                    recv_sem=rsem.at[w, d - 1], device_id=(px, py, c), device_id_type=MESH)
                cp.start()
                ici[w, d] = cp
        fwd = {}
        for d in (1, 2, 3):
            jd = jnp.bitwise_xor(j, d)
            for w in range(n):
                ici[w, d].wait_recv()
                cp = pltpu.make_async_remote_copy(
                    src_ref=outs[w].at[c, jd], dst_ref=outs[w].at[c, jd], send_sem=fsem.at[w, d - 1],
                    recv_sem=gsem.at[w, d - 1], device_id=sib, device_id_type=MESH)
                cp.start()
                fwd[w, d] = cp
        for w in range(n):
            for d in (1, 2, 3):
                fwd[w, d].wait_recv()
                fwd[w, d].wait_send()
                ici[w, d].wait_send()

    out_shape = [jax.ShapeDtypeStruct(a.shape, a.dtype) for a in slabs]
    return pl.pallas_call(
        body, name="gather_weights", in_specs=[ANY] * n, out_specs=[ANY] * n, out_shape=out_shape,
        input_output_aliases={w: w for w in range(n)},
        scratch_shapes=[pltpu.SemaphoreType.DMA((n, 3)), pltpu.SemaphoreType.DMA((n, 3)),
                        pltpu.SemaphoreType.DMA((n, 3)), pltpu.SemaphoreType.DMA((n, 3))],
    )(*slabs)


def _own_slab(shard, jidx):
    slab = jnp.zeros((DEPTH, N_CHIPS) + shard.shape[1:], shard.dtype)
    return lax.dynamic_update_slice_in_dim(slab, shard[:, None], jidx, axis=1)


def exchange_layers(gbig, small):
    n = len(gbig)

    def body(*refs):
        g_refs, s_ref = refs[:n], refs[n]
        got_refs, all_ref = refs[n + 1:2 * n + 1], refs[2 * n + 1]
        dsem, esem, lsem, ssem, rsem, fsem, hsem = refs[2 * n + 2:]
        x, y, c, j = _place()
        sib = (x, y, 1 - c)
        me = 4 * x + 2 * y + c
        big = []
        for k in range(n):
            cp = pltpu.make_async_remote_copy(src_ref=g_refs[k].at[1 - c], dst_ref=got_refs[k], send_sem=dsem.at[k],
                                              recv_sem=esem.at[k], device_id=sib, device_id_type=MESH)
            cp.start()
            big.append(cp)
        mine = pltpu.make_async_copy(s_ref, all_ref.at[me], lsem)
        mine.start()

        def small_copy(k, block, to, sems, from_input):
            return pltpu.make_async_remote_copy(
                src_ref=s_ref if from_input else all_ref.at[block], dst_ref=all_ref.at[block],
                send_sem=sems[0].at[k], recv_sem=sems[1].at[k], device_id=to, device_id_type=MESH)

        first = [small_copy(0, me, sib, (ssem, rsem), True)]
        for d in (1, 2, 3):
            px, py = _chip_peer(x, y, d)
            first.append(small_copy(d, me, (px, py, c), (ssem, rsem), True))
        for cp in first:
            cp.start()
        passed = []
        for d in (1, 2, 3):
            px, py = _chip_peer(x, y, d)
            src_block = 4 * px + 2 * py + c
            small_copy(d, src_block, sib, (ssem, rsem), False).wait_recv()
            cp = small_copy(d - 1, src_block, sib, (fsem, hsem), False)
            cp.start()
            passed.append(cp)
        small_copy(0, me, sib, (ssem, rsem), False).wait_recv()
        for cp in passed:
            cp.wait_recv()
        for cp in first + passed:
            cp.wait_send()
        mine.wait()
        for cp in big:
            cp.wait()

    r2 = small.shape[0]
    return pl.pallas_call(
        body, name="exchange_layers", in_specs=[ANY] * (n + 1), out_specs=[ANY] * (n + 1),
        out_shape=[jax.ShapeDtypeStruct(g.shape[1:], F32) for g in gbig] + [jax.ShapeDtypeStruct((8, r2, LANES), F32)],
        scratch_shapes=[pltpu.SemaphoreType.DMA((n,)), pltpu.SemaphoreType.DMA((n,)), pltpu.SemaphoreType.DMA,
                        pltpu.SemaphoreType.DMA((4,)), pltpu.SemaphoreType.DMA((4,)),
                        pltpu.SemaphoreType.DMA((3,)), pltpu.SemaphoreType.DMA((3,))],
    )(*gbig, small)


def scatter_shards(s1):
    n = len(s1)

    def body(*refs):
        s_refs, got_refs = refs[:n], refs[n:2 * n]
        ssem, rsem = refs[2 * n:]
        x, y, c, j = _place()
        cps = []
        for d in (1, 2, 3):
            px, py = _chip_peer(x, y, d)
            for k in range(n):
                cp = pltpu.make_async_remote_copy(
                    src_ref=s_refs[k].at[jnp.bitwise_xor(j, d)], dst_ref=got_refs[k].at[d - 1],
                    send_sem=ssem.at[k, d - 1], recv_sem=rsem.at[k, d - 1], device_id=(px, py, c), device_id_type=MESH)
                cp.start()
                cps.append(cp)
        for cp in cps:
            cp.wait()

    return pl.pallas_call(
        body, name="scatter_shards", in_specs=[ANY] * n, out_specs=[ANY] * n,
        out_shape=[jax.ShapeDtypeStruct((3,) + a.shape[1:], a.dtype) for a in s1],
        scratch_shapes=[pltpu.SemaphoreType.DMA((n, 3)), pltpu.SemaphoreType.DMA((n, 3))],
    )(*s1)


def share_with_sibling(parts):
    n = len(parts)

    def body(*refs):
        out_refs = refs[n:2 * n]
        ssem, rsem = refs[2 * n:]
        x, y, c, j = _place()
        cps = []
        for k in range(n):
            cp = pltpu.make_async_remote_copy(src_ref=out_refs[k].at[c], dst_ref=out_refs[k].at[c], send_sem=ssem.at[k],
                                              recv_sem=rsem.at[k], device_id=(x, y, 1 - c), device_id_type=MESH)
            cp.start()
            cps.append(cp)
        for cp in cps:
            cp.wait()

    return pl.pallas_call(
        body, name="share_with_sibling", in_specs=[ANY] * n, out_specs=[ANY] * n,
        out_shape=[jax.ShapeDtypeStruct(a.shape, F32) for a in parts],
        input_output_aliases={k: k for k in range(n)},
        scratch_shapes=[pltpu.SemaphoreType.DMA((n,)), pltpu.SemaphoreType.DMA((n,))],
    )(*parts)


def add_layer_halves(g, got, cidx, name):
    _, nch, r, cdim = g.shape
    tr = _row_tile(r, cdim, mult=16)

    def body(c_ref, a_ref, b_ref, o_ref):
        o_ref[...] = _bf(a_ref[...] + b_ref[...])

    grid_spec = pltpu.PrefetchScalarGridSpec(
        num_scalar_prefetch=1, grid=(nch, r // tr),
        in_specs=[pl.BlockSpec((None, None, tr, cdim), lambda jj, i, c_ref: (c_ref[0], jj, i, 0)),
                  pl.BlockSpec((None, tr, cdim), lambda jj, i, c_ref: (jj, i, 0))],
        out_specs=pl.BlockSpec((None, tr, cdim), lambda jj, i, c_ref: (jj, i, 0)))
    return pl.pallas_call(
        body, name=name, grid_spec=grid_spec,
        out_shape=jax.ShapeDtypeStruct((nch, r, cdim), BF16), compiler_params=_cparams(2),
    )(cidx, g, got)


def add_chip_parts(s1, got, jc, name):
    _, r, cdim = s1.shape
    tr = _row_tile(r, cdim, mult=16)

    def body(jc_ref, a_ref, g0_ref, g1_ref, g2_ref, o_ref):
        o_ref[...] = ((a_ref[...].astype(F32) + g0_ref[...].astype(F32)) + g1_ref[...].astype(F32)) + g2_ref[...].astype(F32)

    def slot(k):
        return pl.BlockSpec((None, tr, cdim), lambda i, jc_ref: (k, i, 0))

    grid_spec = pltpu.PrefetchScalarGridSpec(
        num_scalar_prefetch=1, grid=(r // tr,),
        in_specs=[pl.BlockSpec((None, tr, cdim), lambda i, jc_ref: (jc_ref[0], i, 0)), slot(0), slot(1), slot(2)],
        out_specs=pl.BlockSpec((None, tr, cdim), lambda i, jc_ref: (jc_ref[1], i, 0)))
    return pl.pallas_call(
        body, name=name, grid_spec=grid_spec,
        out_shape=jax.ShapeDtypeStruct((DEPTH, r, cdim), F32), compiler_params=_cparams(1),
    )(jc, s1, got, got, got)


def sum_devices(allp):
    _, r, _ = allp.shape

    def body(a_ref, o_ref):
        tot = a_ref[0]
        for k in range(1, 8):
            tot = tot + a_ref[k]
        o_ref[...] = tot

    tr = r // 2 if r % 16 == 0 else r
    return pl.pallas_call(
        body, name="sum_devices", grid=(r // tr,),
        in_specs=[pl.BlockSpec((8, tr, LANES), lambda i: (0, i, 0))],
        out_specs=pl.BlockSpec((tr, LANES), lambda i: (i, 0)),
        out_shape=jax.ShapeDtypeStruct((r, LANES), F32), compiler_params=_cparams(1),
    )(allp)


def _row_tile(r, cdim, limit_bytes=1 << 20, mult=8):
    best = None
    for tr in range(mult, r + 1, mult):
        if r % tr == 0 and tr * cdim * 4 <= limit_bytes:
            best = tr
    return best if best is not None else r


def adamw(w, g, m, v, name):
    r, cdim = w.shape
    tr = _row_tile(r, cdim)
    bc1 = 1.0 - ADAM_B1 ** ADAM_STEP
    bc2 = 1.0 - ADAM_B2 ** ADAM_STEP

    def body(w_ref, g_ref, m_ref, v_ref, d_ref, nm_ref, nv_ref, go_ref):
        gv = g_ref[...]
        nm = ADAM_B1 * m_ref[...] + (1.0 - ADAM_B1) * gv
        nv = ADAM_B2 * v_ref[...] + (1.0 - ADAM_B2) * (gv * gv)
        d_ref[...] = -ADAM_LR * ((nm / bc1) / (jnp.sqrt(nv / bc2) + ADAM_EPS) + ADAM_WD * w_ref[...])
        nm_ref[...] = nm
        nv_ref[...] = nv
        go_ref[...] = gv

    blk = pl.BlockSpec((tr, cdim), lambda i: (i, 0))
    shape = jax.ShapeDtypeStruct((r, cdim), F32)
    return pl.pallas_call(
        body, name=name, grid=(r // tr,), in_specs=[blk] * 4, out_specs=[blk] * 4, out_shape=[shape] * 4,
        compiler_params=_cparams(1),
    )(w, g, m, v)


def _s5_prepare(lam_re, lam_im, log_dt, b_re, b_im, c_re, c_im):
    groups, ch = 16, 16
    dt = jnp.exp(log_dt)[:, None]
    mag = jnp.exp(lam_re * dt)
    a_r, a_i = mag * jnp.cos(lam_im * dt), mag * jnp.sin(lam_im * dt)
    den = lam_re * lam_re + lam_im * lam_im
    q_r = ((a_r - 1.0) * lam_re + a_i * lam_im) / den
    q_i = (a_i * lam_re - (a_r - 1.0) * lam_im) / den
    bb_r = q_r[..., None] * b_re - q_i[..., None] * b_im
    bb_i = q_r[..., None] * b_im + q_i[..., None] * b_re
    eye = jnp.eye(groups, dtype=F32)

    def expand_b(bb):
        return jnp.einsum("gpc,gh->gchp", bb, eye).reshape(groups * ch, N_STATE)

    def expand_c(cc):
        return jnp.einsum("gcp,gh->hpgc", cc, eye).reshape(N_STATE, groups * ch)

    a2 = jnp.concatenate([a_r.reshape(1, N_STATE), a_i.reshape(1, N_STATE)], axis=1)
    bexp = jnp.concatenate([expand_b(bb_r), expand_b(bb_i)], axis=1)
    cexp = jnp.concatenate([expand_c(c_re), -expand_c(c_im)], axis=0)
    return a2, bexp, cexp


def _lru_prepare(w_r, w_i, lam):
    heads = 4
    eye = jnp.eye(heads, dtype=F32)

    def expand(w):
        return jnp.einsum("hij,hk->hikj", w, eye).reshape(D_GROUP, D_GROUP)

    return expand(w_r), expand(w_i), jax.nn.softplus(-lam).reshape(1, D_GROUP)


def _pad_rows(a, rows):
    return jnp.pad(a, ((0, rows - a.shape[0]), (0, 0)))


def _group_mean_matrix():
    gidx = jnp.arange(D_GROUP) // 64
    return (gidx[:, None] == gidx[None, :]).astype(BF16) * jnp.asarray(1.0 / 64.0, BF16)


def _pack_rows(arrs, width):
    parts = []
    for a in arrs:
        flat = a.reshape(-1)
        pad = (-flat.shape[0]) % width
        parts.append(jnp.pad(flat, (0, pad)) if pad else flat)
    flat = jnp.concatenate(parts)
    rows = flat.shape[0] // width
    pad_rows = (-rows) % 16
    if pad_rows:
        flat = jnp.pad(flat, (0, pad_rows * width))
    return flat.reshape(-1, width)


def _unpack_rows(packed, shapes, width):
    flat = packed.reshape(-1)
    out, off = [], 0
    for shp in shapes:
        size = math.prod(shp)
        out.append(flat[off:off + size].reshape(shp))
        off += size + ((-size) % width)
    return out


def kernel(x, mem, ln_in_g, ln_in_b, w_in, b_in, s5_lam_re, s5_lam_im, s5_log_dt, s5_b_re, s5_b_im, s5_c_re, s5_c_im, s5_d, s5_w_glu, s5_b_glu, cv_w, cv_b, cv_gn_g, cv_gn_b, cv_w_pw, cv_b_pw, lru_conv_w, lru_conv_b, lru_w_r, lru_b_r, lru_w_i, lru_b_i, lru_lam, attn_w_kv, w_out, b_out, ln1_g, ln1_b, ffn_w_up, ffn_conv_w, ffn_conv_b, ffn_w_down, ln2_g, ln2_b, loss_target, m_ln_in_g, m_ln_in_b, m_w_in, m_b_in, m_s5_lam_re, m_s5_lam_im, m_s5_log_dt, m_s5_b_re, m_s5_b_im, m_s5_c_re, m_s5_c_im, m_s5_d, m_s5_w_glu, m_s5_b_glu, m_cv_w, m_cv_b, m_cv_gn_g, m_cv_gn_b, m_cv_w_pw, m_cv_b_pw, m_lru_conv_w, m_lru_conv_b, m_lru_w_r, m_lru_b_r, m_lru_w_i, m_lru_b_i, m_lru_lam, m_attn_w_kv, m_w_out, m_b_out, m_ln1_g, m_ln1_b, m_ffn_w_up, m_ffn_conv_w, m_ffn_conv_b, m_ffn_w_down, m_ln2_g, m_ln2_b, v_ln_in_g, v_ln_in_b, v_w_in, v_b_in, v_s5_lam_re, v_s5_lam_im, v_s5_log_dt, v_s5_b_re, v_s5_b_im, v_s5_c_re, v_s5_c_im, v_s5_d, v_s5_w_glu, v_s5_b_glu, v_cv_w, v_cv_b, v_cv_gn_g, v_cv_gn_b, v_cv_w_pw, v_cv_b_pw, v_lru_conv_w, v_lru_conv_b, v_lru_w_r, v_lru_b_r, v_lru_w_i, v_lru_b_i, v_lru_lam, v_attn_w_kv, v_w_out, v_b_out, v_ln1_g, v_ln1_b, v_ffn_w_up, v_ffn_conv_w, v_ffn_conv_b, v_ffn_w_down, v_ln2_g, v_ln2_b):
    p = dict(locals())
    xs = x[0]
    mems = mem[0]
    target = loss_target[0]
    s = xs.shape[0]
    cidx = lax.axis_index("c")
    jidx = 2 * lax.axis_index("x") + lax.axis_index("y")
    tb_scan = min(256, s)
    tb_attn = min(512, s)
    tb_ffn = min(256, s)

    small_sh_names = list(SMALL_SHARDED)
    small_sh_shapes = [p[nm].shape[1:] for nm in small_sh_names]
    small_pack = jnp.stack([_pack_rows([p[nm][l] for nm in small_sh_names], LANES) for l in range(DEPTH)])
    gathered = gather_weights([_own_slab(_bf(p[nm]), jidx) for nm in BIG] + [_own_slab(small_pack, jidx)])
    g_w_in, g_w_kv, g_w_out, g_w_up, g_w_down, g_small = gathered
    g_w_kv = g_w_kv.reshape(DEPTH, 1, D_MODEL, 2 * D_GROUP)
    g_w_out = g_w_out.reshape(DEPTH, 1, D_MODEL, D_MODEL)
    g_w_down = g_w_down.reshape(DEPTH, 1, D_FF, D_MODEL)
    full_small = {nm: [] for nm in small_sh_names}
    for l in range(DEPTH):
        per_chip = [_unpack_rows(g_small[l, jj], small_sh_shapes, LANES) for jj in range(N_CHIPS)]
        for k, nm in enumerate(small_sh_names):
            full_small[nm].append(jnp.concatenate([per_chip[jj][k] for jj in range(N_CHIPS)],
                                                  axis=SMALL_SHARDED[nm] - 1))
    w_glu_bf = _bf(jnp.stack(full_small['s5_w_glu']))
    w_pw_bf = _bf(jnp.stack(full_small['cv_w_pw']))
    pmat = _group_mean_matrix()

    def vec(a):
        return a.reshape(1, -1)

    xh0, rs0, xb0 = ln_fwd(xs, vec(ln_in_g), vec(ln_in_b), "ln_in")
    saved = []
    prev = dict(xh=xh0, rs=rs0, xb=xb0, g=vec(ln_in_g), b=vec(ln_in_b))
    for l in range(DEPTH):
        sv = dict(prev=prev)
        (a2, bexp, cexp), sv['s5_vjp'] = jax.vjp(_s5_prepare, s5_lam_re[l], s5_lam_im[l], s5_log_dt[l],
                                                 s5_b_re[l], s5_b_im[l], s5_c_re[l], s5_c_im[l])
        (wr, wi, sp), sv['lru_vjp'] = jax.vjp(_lru_prepare, lru_w_r[l], lru_w_i[l], lru_lam[l])
        sv.update(a2=a2, bexp=_bf(bexp), cexp=_bf(cexp), wr=_bf(wr), wi=_bf(wi), sp=sp)
        sv['cvw'] = _pad_rows(full_small['cv_w'][l], CV_HALO)
        sv['lcw'] = _pad_rows(full_small['lru_conv_w'][l], LRU_HALO)
        sv['fcw'] = _pad_rows(full_small['ffn_conv_w'][l], FFN_TAP_ROWS)
        h_in = mm_nn(prev['xb'], g_w_in, l, vec(b_in[l]), F32, 2048, f"in_proj{l}")
        kv = mm_nn(mems, g_w_kv, l, jnp.zeros((1, 2 * D_GROUP), F32), F32, 256, f"kv_proj{l}")
        y_s5, hst, y0 = s5_fwd(h_in, a2, sv['bexp'], sv['cexp'], vec(s5_d[l]), w_glu_bf, l, vec(s5_b_glu[l]),
                               tb_scan, f"s5_fwd{l}")
        y_cv, hc = cv_fwd(h_in, sv['cvw'], vec(cv_b[l]), vec(cv_gn_g[l]), vec(cv_gn_b[l]), pmat, w_pw_bf, l,
                          vec(cv_b_pw[l]), tb_scan, f"cv_fwd{l}")
        y_lru, xcs, hls = lru_fwd(h_in, sv['lcw'], vec(lru_conv_b[l]), sv['wr'], vec(lru_b_r[l]), sv['wi'],
                                  vec(lru_b_i[l]), sp, tb_scan, f"lru_fwd{l}")
        y_mem = attn_fwd(h_in, kv, tb_attn, f"attn_fwd{l}")
        mix_in = jnp.concatenate([y_s5, y_cv, y_lru, y_mem], axis=1)
        xh1, rs1, xb1 = proj_ln(mix_in, g_w_out.reshape(DEPTH, D_MODEL, D_MODEL), l, vec(b_out[l]),
                                prev['xh'], prev['g'], prev['b'], vec(ln1_g[l]), vec(ln1_b[l]), f"out_proj_ln{l}")
        u = mm_nn(xb1, g_w_up, l, jnp.zeros((1, 2 * D_FF), F32), F32, 1024, f"ffn_up{l}")
        hf, sv['ffn_kept'] = ffn_act_fwd(u, sv['fcw'], vec(ffn_conv_b[l]), tb_ffn, f"ffn_act{l}")
        xh2, rs2, xb2 = proj_ln(hf, g_w_down.reshape(DEPTH, D_FF, D_MODEL), l, jnp.zeros((1, D_MODEL), F32),
                                xh1, vec(ln1_g[l]), vec(ln1_b[l]), vec(ln2_g[l]), vec(ln2_b[l]), f"ffn_down_ln{l}")
        sv.update(h_in=h_in, kv=kv, hst=hst, y0=y0, hc=hc, xcs=xcs, hls=hls, mix_in=mix_in,
                  xh1=xh1, rs1=rs1, xb1=xb1, u=u, hf=hf, xh2=xh2, rs2=rs2)
        saved.append(sv)
        prev = dict(xh=xh2, rs=rs2, xb=xb2, g=vec(ln2_g[l]), b=vec(ln2_b[l]))

    grads = {}
    per_layer = {nm: [None] * DEPTH for nm in WEIGHTS if nm not in ('ln_in_g', 'ln_in_b')}
    gbig = [None] * len(BIG)
    dx = None
    for l in reversed(range(DEPTH)):
        sv = saved[l]
        pv = sv['prev']
        if l == DEPTH - 1:
            dr2, dg2, db2, sqerr = loss_ln_bwd(target, sv['xh2'], sv['rs2'], vec(ln2_g[l]), vec(ln2_b[l]), "loss_ln2_bwd")
            loss_local = 0.5 / D_MODEL * jnp.sum(sqerr)
        else:
            dr2, dg2, db2, _ = ln_bwd(dx, sv['xh2'], sv['rs2'], vec(ln2_g[l]), f"ln2_bwd{l}")
        per_layer['ln2_g'][l], per_layer['ln2_b'][l] = dg2[0], db2[0]
        tm_nt = min(512, s)
        ts_big = min(2048, s)
        whole = lambda a_ref, j: a_ref[...]
        dhf = mm_nt(dr2, (tm_nt, D_MODEL), lambda i: (i, 0), whole, g_w_down, l, None, F32, 512, s, f"ffn_down_dx{l}")
        gbig[4] = mm_tn(sv['hf'], dr2, (min(1024, s), D_MODEL), lambda kt, j, st: (st, 0), 1, D_MODEL, FFN_TN, 1024, s,
                        l, gbig[4], f"ffn_down_dw{l}")
        du, dcwv, dcwg = ffn_act_bwd(dhf, sv['u'], sv['ffn_kept'], sv['fcw'], tb_ffn, f"ffn_act_bwd{l}")
        dcw = jnp.concatenate([dcwv, dcwg], axis=1)
        per_layer['ffn_conv_w'][l] = dcw[0:FFN_CONV_WIDTH]
        per_layer['ffn_conv_b'][l] = dcw[FFN_CONV_WIDTH]
        dx1 = mm_nt(du, (2, tm_nt, D_FF), lambda i: (0, i, 0),
                    lambda a_ref, j: a_ref[j // 2, :, (j % 2) * FFN_TN:(j % 2 + 1) * FFN_TN], g_w_up, l, dr2, F32,
                    512, s, f"ffn_up_dx{l}")
        gbig[3] = mm_tn(sv['xb1'], du, (None, ts_big, FFN_TN), lambda kt, j, st: (j // 2, st, j % 2), N_CHIPS, FFN_TN,
                        D_MODEL, 2048, s, l, gbig[3], f"ffn_up_dw{l}")
        dr1, dg1, db1, cs1 = ln_bwd(dx1, sv['xh1'], sv['rs1'], vec(ln1_g[l]), f"ln1_bwd{l}")
        per_layer['ln1_g'][l], per_layer['ln1_b'][l], per_layer['b_out'][l] = dg1[0], db1[0], cs1[0]
        dmix = mm_nt(dr1, (tm_nt, D_MODEL), lambda i: (i, 0), whole, g_w_out, l, None, F32, 512, s, f"out_proj_dx{l}")
        gbig[2] = mm_tn(sv['mix_in'], dr1, (min(1024, s), D_MODEL), lambda kt, j, st: (st, 0), 1, D_MODEL, D_MODEL, 1024, s,
                        l, gbig[2], f"out_proj_dw{l}")
        h_in = sv['h_in']
        (d_u, cs_u, d_bexp, d_cexp, d_dd, d_wglu, d_bglu, d_a2) = s5_bwd(
            dmix, h_in, sv['y0'], sv['hst'], sv['a2'], sv['bexp'], sv['cexp'], vec(s5_d[l]), w_glu_bf, l,
            vec(s5_b_glu[l]), tb_scan, f"s5_bwd{l}")
        (d_vg, cs_vg, d_cvw, d_cvb, d_gg, d_gb, d_wpw, d_bpw) = cv_bwd(
            dmix, h_in, sv['hc'], sv['cvw'], vec(cv_gn_g[l]), vec(cv_gn_b[l]), pmat, w_pw_bf, l, tb_scan, f"cv_bwd{l}")
        (d_lx, cs_lx, d_lcw, d_lcb, d_wr, d_br, d_wi, d_bi, d_sp) = lru_bwd(
            dmix, h_in, sv['xcs'], sv['hls'], sv['lcw'], sv['wr'], vec(lru_b_r[l]), sv['wi'], vec(lru_b_i[l]),
            sv['sp'], tb_scan, f"lru_bwd{l}")
        d_q, cs_q, d_kv = attn_bwd(dmix, h_in, sv['kv'], tb_attn, f"attn_bwd{l}")
        g_s5 = sv['s5_vjp']((d_a2, d_bexp, d_cexp))
        for nm, gval in zip(['s5_lam_re', 's5_lam_im', 's5_log_dt', 's5_b_re', 's5_b_im', 's5_c_re', 's5_c_im'], g_s5):
            per_layer[nm][l] = gval
        g_lru = sv['lru_vjp']((d_wr, d_wi, d_sp))
        for nm, gval in zip(['lru_w_r', 'lru_w_i', 'lru_lam'], g_lru):
            per_layer[nm][l] = gval
        per_layer['s5_d'][l], per_layer['s5_w_glu'][l], per_layer['s5_b_glu'][l] = d_dd[0], d_wglu, d_bglu[0]
        per_layer['cv_w'][l], per_layer['cv_b'][l] = d_cvw[0:CONV_WIDTH], d_cvb[0]
        per_layer['cv_gn_g'][l], per_layer['cv_gn_b'][l] = d_gg[0], d_gb[0]
        per_layer['cv_w_pw'][l], per_layer['cv_b_pw'][l] = d_wpw, d_bpw[0]
        per_layer['lru_conv_w'][l], per_layer['lru_conv_b'][l] = d_lcw[0:LRU_CONV_WIDTH], d_lcb[0]
        per_layer['lru_b_r'][l], per_layer['lru_b_i'][l] = d_br[0], d_bi[0]
        per_layer['b_in'][l] = jnp.concatenate([cs_u, cs_vg, cs_lx, cs_q], axis=1)[0]
        gbig[1] = mm_tn(mems, d_kv, (MEM_ROWS, 2 * D_GROUP), lambda kt, j, st: (st, 0), 1, 2 * D_GROUP, D_MODEL, MEM_ROWS,
                        MEM_ROWS, l, gbig[1], f"kv_proj_dw{l}")
        dh_in = jnp.concatenate([d_u, d_vg, d_lx, d_q], axis=1)
        n_sh = N_IN // N_CHIPS
        dxp = mm_nt(dh_in, (tm_nt, N_IN), lambda i: (i, 0), lambda a_ref, j: a_ref[:, j * n_sh:(j + 1) * n_sh],
                    g_w_in, l, dr1, F32, 512, s, f"in_proj_dx{l}")
        gbig[0] = mm_tn(pv['xb'], dh_in, (ts_big, n_sh), lambda kt, j, st: (st, j), N_CHIPS, n_sh, D_MODEL, 2048, s,
                        l, gbig[0], f"in_proj_dw{l}")
        dx = dxp
    grad_x, dg_in, db_in, _ = ln_bwd(dx, xh0, rs0, vec(ln_in_g), "ln_in_bwd")
    grads['ln_in_g'], grads['ln_in_b'] = dg_in[0], db_in[0]
    for nm, vals in per_layer.items():
        if nm not in BIG:
            grads[nm] = jnp.stack(vals)

    small_names = [nm for nm in WEIGHTS if nm not in BIG]
    small_local = _pack_rows([grads[nm] for nm in small_names], LANES)
    gbig = [g.reshape((DEPTH, N_CHIPS) + p[nm].shape[1:]) for g, nm in zip(gbig, BIG)]
    c1 = cidx.reshape(1).astype(jnp.int32)
    jc = jnp.stack([jidx, cidx]).astype(jnp.int32)
    *got_a, small_all = exchange_layers(gbig, small_local)
    s1 = [add_layer_halves(g, ga, c1, f"add_cores_{nm}") for g, ga, nm in zip(gbig, got_a, BIG)]
    got_b = scatter_shards(s1)
    mine_r = [add_chip_parts(sk, gb, jc, f"add_chips_{nm}") for sk, gb, nm in zip(s1, got_b, BIG)]
    red_big = share_with_sibling(mine_r)
    small_red = sum_devices(small_all)
    small_grads = dict(zip(small_names, _unpack_rows(small_red, [grads[nm].shape for nm in small_names], LANES)))

    out_g, out_d, out_m, out_v = {}, {}, {}, {}
    for k, nm in enumerate(BIG):
        gk = red_big[k]
        two_d = (-1, p[nm].shape[-1])
        res = adamw(p[nm].reshape(two_d), gk.reshape(two_d), p['m_' + nm].reshape(two_d),
                    p['v_' + nm].reshape(two_d), f"adamw_{nm}")
        out_d[nm], out_m[nm], out_v[nm], out_g[nm] = (t.reshape(p[nm].shape) for t in res)
    own = {}
    for nm in small_names:
        gfull = small_grads[nm]
        if nm in SMALL_SHARDED:
            ax = SMALL_SHARDED[nm]
            width = p[nm].shape[ax]
            gfull = lax.dynamic_slice_in_dim(gfull, jidx * width, width, axis=ax)
        own[nm] = gfull
    packs = [_pack_rows([src[nm] for nm in small_names], LANES)
             for src in (dict((nm, p[nm]) for nm in small_names), own,
                         dict((nm, p['m_' + nm]) for nm in small_names), dict((nm, p['v_' + nm]) for nm in small_names))]
    dlt, nm_, nv_, _ = adamw(packs[0], packs[1], packs[2], packs[3], "adamw_small")
    shapes = [p[nm].shape for nm in small_names]
    for dst, packed in ((out_d, dlt), (out_m, nm_), (out_v, nv_)):
        dst.update(zip(small_names, _unpack_rows(packed, shapes, LANES)))
    out_g.update(own)

    loss = lax.psum(loss_local, ("x", "y", "c"))
    return (loss, grad_x[None], *[out_g[nm] for nm in WEIGHTS], *[out_d[nm] for nm in WEIGHTS],
            *[out_m[nm] for nm in WEIGHTS], *[out_v[nm] for nm in WEIGHTS])
```

```python
import functools
import math

import jax
import jax.numpy as jnp
from jax import lax
from jax.experimental import pallas as pl
from jax.experimental.pallas import tpu as pltpu

F32 = jnp.float32
BF16 = jnp.bfloat16
MESH = pl.DeviceIdType.MESH
ANY = pl.BlockSpec(memory_space=pl.ANY)

DEPTH = 2
D_MODEL = 1024
D_GROUP = 256
N_IN = 6 * D_GROUP
D_FF = 2816
N_STATE = 1024
CONV_WIDTH = 31
LRU_CONV_WIDTH = 4
FFN_CONV_WIDTH = 3
LRU_C = 8.0
ALPHA = (2 * DEPTH) ** 0.25
LN_EPS = 1e-5
N_CHIPS = 4
MEM_ROWS = 256
LANES = 128
SUBLANES = 8
VMEM_LIMIT = 56 * 1024 * 1024

ADAM_LR, ADAM_B1, ADAM_B2, ADAM_EPS, ADAM_WD, ADAM_STEP = 0.001, 0.9, 0.999, 1e-08, 0.01, 10

WEIGHTS = ['ln_in_g', 'ln_in_b', 'w_in', 'b_in', 's5_lam_re', 's5_lam_im', 's5_log_dt', 's5_b_re', 's5_b_im',
           's5_c_re', 's5_c_im', 's5_d', 's5_w_glu', 's5_b_glu', 'cv_w', 'cv_b', 'cv_gn_g', 'cv_gn_b', 'cv_w_pw',
           'cv_b_pw', 'lru_conv_w', 'lru_conv_b', 'lru_w_r', 'lru_b_r', 'lru_w_i', 'lru_b_i', 'lru_lam',
           'attn_w_kv', 'w_out', 'b_out', 'ln1_g', 'ln1_b', 'ffn_w_up', 'ffn_conv_w', 'ffn_conv_b', 'ffn_w_down',
           'ln2_g', 'ln2_b']
BIG = ['w_in', 'attn_w_kv', 'w_out', 'ffn_w_up', 'ffn_w_down']
SMALL_SHARDED = {'s5_w_glu': 1, 'cv_w': 2, 'cv_w_pw': 1, 'lru_conv_w': 2, 'ffn_conv_w': 2}


def _cparams(n_axes):
    return pltpu.CompilerParams(dimension_semantics=("arbitrary",) * n_axes, vmem_limit_bytes=VMEM_LIMIT)


def _dot(a, b):
    return jnp.dot(a, b, preferred_element_type=F32)


def _dot_nt(a, b):
    return lax.dot_general(a, b, (((1,), (1,)), ((), ())), preferred_element_type=F32)


def _dot_tn(a, b):
    return lax.dot_general(a, b, (((0,), (0,)), ((), ())), preferred_element_type=F32)


def _bf(v):
    return v.astype(BF16)


def _colsum(v):
    return jnp.sum(v, axis=0, keepdims=True)


def _dot3(v, p):
    hi = _bf(v)
    r1 = v - hi.astype(F32)
    mid = _bf(r1)
    lo = _bf(r1 - mid.astype(F32))
    return _dot(hi, p) + _dot(mid, p) + _dot(lo, p)


_GELU_C = math.sqrt(2.0 / math.pi)


_GELU_C3 = _GELU_C * 0.044715


def _gelu_parts(v):
    t = jnp.tanh(v * (_GELU_C + _GELU_C3 * (v * v)))
    hv = 0.5 * v
    return hv + hv * t, t


def _gelu(v):
    return _gelu_parts(v)[0]


def _gelu_grad(v, t):
    return (0.5 + 0.5 * t) + (0.5 * v) * (1.0 - t * t) * (_GELU_C + (3.0 * _GELU_C3) * (v * v))


def _sigmoid(v):
    return 1.0 / (1.0 + jnp.exp(-v))


def _acc(ref, val, first):
    @pl.when(first)
    def _():
        ref[...] = val

    @pl.when(jnp.logical_not(first))
    def _():
        ref[...] += val


def _rows(shape):
    return lax.broadcasted_iota(jnp.int32, shape, 0)


def _ln_rows(r):
    mu = jnp.mean(r, -1, keepdims=True)
    rc = r - mu
    var = jnp.mean(rc * rc, -1, keepdims=True)
    rs = lax.rsqrt(var + LN_EPS)
    return rc * rs, rs


def ln_fwd(x, g, b, name):
    s = x.shape[0]
    tm = min(512, s)

    def body(x_ref, g_ref, b_ref, xh_ref, rs_ref, xb_ref):
        xh, rs = _ln_rows(x_ref[...])
        xh_ref[...] = xh
        rs_ref[...] = rs
        xb_ref[...] = _bf(xh * g_ref[...] + b_ref[...])

    row = pl.BlockSpec((tm, D_MODEL), lambda i: (i, 0))
    vec = pl.BlockSpec((1, D_MODEL), lambda i: (0, 0))
    return pl.pallas_call(
        body, name=name, grid=(s // tm,),
        in_specs=[row, vec, vec],
        out_specs=[row, pl.BlockSpec((tm, 1), lambda i: (i, 0)), row],
        out_shape=[jax.ShapeDtypeStruct((s, D_MODEL), F32), jax.ShapeDtypeStruct((s, 1), F32),
                   jax.ShapeDtypeStruct((s, D_MODEL), BF16)],
        compiler_params=_cparams(1),
    )(x, g, b)


def proj_ln(a, w, layer, bias, xh_prev, g_prev, b_prev, g, b, name):
    s, k = a.shape
    tm = min(512, s)

    def body(a_ref, w_ref, bias_ref, xp_ref, gp_ref, bp_ref, g_ref, b_ref, xh_ref, rs_ref, xb_ref):
        acc = _dot(a_ref[...], w_ref[...]) + bias_ref[...]
        r = ALPHA * (xp_ref[...] * gp_ref[...] + bp_ref[...]) + acc
        xh, rs = _ln_rows(r)
        xh_ref[...] = xh
        rs_ref[...] = rs
        xb_ref[...] = _bf(xh * g_ref[...] + b_ref[...])

    row = pl.BlockSpec((tm, D_MODEL), lambda i: (i, 0))
    vec = pl.BlockSpec((1, D_MODEL), lambda i: (0, 0))
    return pl.pallas_call(
        body, name=name, grid=(s // tm,),
        in_specs=[pl.BlockSpec((tm, k), lambda i: (i, 0)),
                  pl.BlockSpec((None, k, D_MODEL), lambda i: (layer, 0, 0)),
                  vec, row, vec, vec, vec, vec],
        out_specs=[row, pl.BlockSpec((tm, 1), lambda i: (i, 0)), row],
        out_shape=[jax.ShapeDtypeStruct((s, D_MODEL), F32), jax.ShapeDtypeStruct((s, 1), F32),
                   jax.ShapeDtypeStruct((s, D_MODEL), BF16)],
        compiler_params=_cparams(1),
    )(a, w, bias, xh_prev, g_prev, b_prev, g, b)


def loss_ln_bwd(target, xh, rs, g, b, name):
    s = xh.shape[0]
    tm = min(512, s)

    def body(t_ref, xh_ref, rs_ref, g_ref, b_ref, dr_ref, dg_ref, db_ref, sq_ref):
        first = pl.program_id(0) == 0
        xhv = xh_ref[...]
        err = xhv * g_ref[...] + b_ref[...] - t_ref[...]
        dyv = err * (1.0 / D_MODEL)
        dxh = dyv * g_ref[...]
        dr = rs_ref[...] * (dxh - jnp.mean(dxh, -1, keepdims=True) - xhv * jnp.mean(dxh * xhv, -1, keepdims=True))
        dr_ref[...] = dr
        _acc(dg_ref, _colsum(dyv * xhv), first)
        _acc(db_ref, _colsum(dyv), first)
        _acc(sq_ref, _colsum(err * err), first)

    row = pl.BlockSpec((tm, D_MODEL), lambda i: (i, 0))
    vec = pl.BlockSpec((1, D_MODEL), lambda i: (0, 0))
    vshape = jax.ShapeDtypeStruct((1, D_MODEL), F32)
    return pl.pallas_call(
        body, name=name, grid=(s // tm,),
        in_specs=[row, row, pl.BlockSpec((tm, 1), lambda i: (i, 0)), vec, vec],
        out_specs=[row, vec, vec, vec],
        out_shape=[jax.ShapeDtypeStruct((s, D_MODEL), F32), vshape, vshape, vshape],
        compiler_params=_cparams(1),
    )(target, xh, rs, g, b)


def mm_nn(a, w, layer, bias, out_dtype, tm, name):
    m, k = a.shape
    _, nj, _, n = w.shape
    tm = min(tm, m)

    def body(a_ref, w_ref, b_ref, o_ref):
        o_ref[...] = (_dot(_bf(a_ref[...]), w_ref[...]) + b_ref[...]).astype(out_dtype)

    return pl.pallas_call(
        body, name=name, grid=(nj, m // tm),
        in_specs=[pl.BlockSpec((tm, k), lambda j, i: (i, 0)),
                  pl.BlockSpec((None, None, k, n), lambda j, i: (layer, j, 0, 0)),
                  pl.BlockSpec((1, n), lambda j, i: (0, j))],
        out_specs=pl.BlockSpec((tm, n), lambda j, i: (i, j)),
        out_shape=jax.ShapeDtypeStruct((m, nj * n), out_dtype),
        compiler_params=_cparams(2),
    )(a, w, bias)


def mm_nt(a, a_block, a_map, pick, w, layer, add, out_dtype, tm, m, name, ln=None):
    _, nj, r, n = w.shape
    tm = min(tm, m)
    has_add = add is not None
    n_in = 2 + has_add + (3 if ln is not None else 0)

    def body(*refs):
        a_ref, w_ref = refs[0], refs[1]
        res = _dot_nt(_bf(pick(a_ref, 0)), w_ref[0])
        for j in range(1, nj):
            res = res + _dot_nt(_bf(pick(a_ref, j)), w_ref[j])
        if has_add:
            res = res + ALPHA * refs[2][...]
        if ln is None:
            refs[n_in][...] = res.astype(out_dtype)
            return
        xh_ref, rs_ref, g_ref = refs[n_in - 3:n_in]
        dr_ref, dg_ref, db_ref, cs_ref = refs[n_in:]
        first = pl.program_id(0) == 0
        xhv = xh_ref[...]
        dxh = res * g_ref[...]
        dr = rs_ref[...] * (dxh - jnp.mean(dxh, -1, keepdims=True) - xhv * jnp.mean(dxh * xhv, -1, keepdims=True))
        dr_ref[...] = dr
        _acc(dg_ref, _colsum(res * xhv), first)
        _acc(db_ref, _colsum(res), first)
        _acc(cs_ref, _colsum(dr), first)

    row = pl.BlockSpec((tm, r), lambda i: (i, 0))
    in_specs = [pl.BlockSpec(a_block, a_map),
                pl.BlockSpec((None, nj, r, n), lambda i: (layer, 0, 0, 0))]
    ops = [a, w]
    if has_add:
        in_specs.append(row)
        ops.append(add)
    if ln is None:
        out_specs, out_shape = row, jax.ShapeDtypeStruct((m, r), out_dtype)
    else:
        vec = pl.BlockSpec((1, r), lambda i: (0, 0))
        vshape = jax.ShapeDtypeStruct((1, r), F32)
        in_specs += [row, pl.BlockSpec((tm, 1), lambda i: (i, 0)), vec]
        ops += list(ln)
        out_specs, out_shape = [row, vec, vec, vec], [jax.ShapeDtypeStruct((m, r), F32), vshape, vshape, vshape]
    return pl.pallas_call(
        body, name=name, grid=(m // tm,),
        in_specs=in_specs, out_specs=out_specs, out_shape=out_shape,
        compiler_params=_cparams(1),
    )(*ops)


def mm_tn(a, b, b_block, b_map, nj, n, tk, ts, s, layer, into, name):
    kx = a.shape[1]
    ts = min(ts, s)

    def body(a_ref, b_ref, *rest):
        o_ref = rest[-1]
        part = _dot_tn(_bf(a_ref[...]), _bf(b_ref[...]))
        _acc(o_ref, part, pl.program_id(2) == 0)

    in_specs = [pl.BlockSpec((ts, tk), lambda kt, j, st: (st, kt)), pl.BlockSpec(b_block, b_map)]
    ops = [a, b]
    aliases = {}
    if into is not None:
        in_specs.append(ANY)
        ops.append(into)
        aliases = {2: 0}
    return pl.pallas_call(
        body, name=name, grid=(kx // tk, nj, s // ts),
        in_specs=in_specs,
        out_specs=pl.BlockSpec((None, None, tk, n), lambda kt, j, st: (layer, j, kt, 0)),
        out_shape=jax.ShapeDtypeStruct((DEPTH, nj, kx, n), F32),
        input_output_aliases=aliases,
        compiler_params=_cparams(3),
    )(*ops)


S5_TAB_ROWS = 8 * SUBLANES


def _s5_scan_table(tab_ref, ar, ai, reverse):
    n = N_STATE
    row = _rows((SUBLANES, n))
    edge = SUBLANES - 1 if reverse else 0
    tab_ref[0:8, :] = jnp.where(row == edge, ar, 0.0)
    tab_ref[8:16, :] = jnp.where(row == edge, ai, 0.0)
    pr, pi = ar, ai
    for step, k in enumerate((1, 2, 4)):
        mask = row < SUBLANES - k if reverse else row >= k
        tab_ref[16 + 16 * step:24 + 16 * step, :] = jnp.where(mask, pr, 0.0)
        tab_ref[24 + 16 * step:32 + 16 * step, :] = jnp.where(mask, pi, 0.0)
        pr, pi = pr * pr - pi * pi, 2.0 * pr * pi


def _s5_scan(src_ref, dst_ref, tab_ref, edge_ref, tb, reverse, per_tile=None):
    n = N_STATE
    ng = tb // SUBLANES
    nq = n // LANES
    link = SUBLANES - 1 if reverse else 1

    def tile(ii, carry):
        g = ng - 1 - ii if reverse else ii
        rows = pl.ds(pl.multiple_of(g * SUBLANES, SUBLANES), SUBLANES)
        out = []
        for q in range(nq):
            cre = slice(q * LANES, (q + 1) * LANES)
            cim = slice(n + q * LANES, n + (q + 1) * LANES)
            lr, li = src_ref[rows, cre], src_ref[rows, cim]
            tr, ti = pltpu.roll(carry[2 * q], link, 0), pltpu.roll(carry[2 * q + 1], link, 0)
            kr, ki = tab_ref[0:8, cre], tab_ref[8:16, cre]
            lr, li = lr + kr * tr - ki * ti, li + kr * ti + ki * tr
            for step, k in enumerate((1, 2, 4)):
                amt = SUBLANES - k if reverse else k
                kr, ki = tab_ref[16 + 16 * step:24 + 16 * step, cre], tab_ref[24 + 16 * step:32 + 16 * step, cre]
                sr, si = pltpu.roll(lr, amt, 0), pltpu.roll(li, amt, 0)
                lr, li = lr + kr * sr - ki * si, li + kr * si + ki * sr
            dst_ref[rows, cre] = lr
            dst_ref[rows, cim] = li
            if per_tile is not None:
                per_tile(g, q, (cre, cim), lr, li)
            out += [lr, li]
        return tuple(out)

    init = []
    for q in range(nq):
        init += [edge_ref[:, q * LANES:(q + 1) * LANES], edge_ref[:, n + q * LANES:n + (q + 1) * LANES]]
    fin = lax.fori_loop(0, ng, tile, tuple(init))
    for q in range(nq):
        edge_ref[:, q * LANES:(q + 1) * LANES] = fin[2 * q]
        edge_ref[:, n + q * LANES:n + (q + 1) * LANES] = fin[2 * q + 1]


def s5_fwd(h_in, a2, bexp, cexp, dskip, wglu, layer, bglu, tb, name):
    s = h_in.shape[0]
    n = N_STATE

    def body(u_ref, a_ref, b_ref, c_ref, d_ref, w_ref, bg_ref, y_ref, h_ref, y0_ref, edge, tab, bu_ref):
        @pl.when(pl.program_id(0) == 0)
        def _():
            edge[...] = jnp.zeros_like(edge)
            _s5_scan_table(tab, a_ref[0:1, 0:n], a_ref[0:1, n:2 * n], False)

        u = u_ref[...]
        bu_ref[...] = _dot(_bf(u), b_ref[...])
        _s5_scan(bu_ref, h_ref, tab, edge, tb, False)
        y0 = _dot(_bf(h_ref[:, 0:n]), c_ref[0:n, :]) + _dot(_bf(h_ref[:, n:2 * n]), c_ref[n:2 * n, :]) + d_ref[...] * u
        y0_ref[...] = y0
        yg = _gelu(y0)
        z = _dot(_bf(yg), w_ref[...]) + bg_ref[...]
        y_ref[...] = _bf(yg * _sigmoid(z))

    vec = pl.BlockSpec((1, D_GROUP), lambda i: (0, 0))
    return pl.pallas_call(
        body, name=name, grid=(s // tb,),
        in_specs=[pl.BlockSpec((tb, D_GROUP), lambda i: (i, 0)),
                  pl.BlockSpec((1, 2 * n), lambda i: (0, 0)),
                  pl.BlockSpec((D_GROUP, 2 * n), lambda i: (0, 0)),
                  pl.BlockSpec((2 * n, D_GROUP), lambda i: (0, 0)),
                  vec,
                  pl.BlockSpec((None, D_GROUP, D_GROUP), lambda i: (layer, 0, 0)),
                  vec],
        out_specs=[pl.BlockSpec((tb, D_GROUP), lambda i: (i, 0)),
                   pl.BlockSpec((tb, 2 * n), lambda i: (i, 0)),
                   pl.BlockSpec((tb, D_GROUP), lambda i: (i, 0))],
        out_shape=[jax.ShapeDtypeStruct((s, D_GROUP), BF16), jax.ShapeDtypeStruct((s, 2 * n), F32),
                   jax.ShapeDtypeStruct((s, D_GROUP), F32)],
        scratch_shapes=[pltpu.VMEM((SUBLANES, 2 * n), F32), pltpu.VMEM((S5_TAB_ROWS, n), F32),
                        pltpu.VMEM((tb, 2 * n), F32)],
        compiler_params=_cparams(1),
    )(h_in, a2, bexp, cexp, dskip, wglu, bglu)


def s5_bwd(dmix, h_in, y0, hst, a2, bexp, cexp, dskip, wglu, layer, bglu, tb, name):
    s = h_in.shape[0]
    n = N_STATE
    nb = s // tb
    halo = tb // 8

    def body(dy_ref, u_ref, y0_ref, h_ref, hp_ref, a_ref, b_ref, c_ref, d_ref, w_ref, bg_ref,
             du_ref, cs_ref, db_ref, dc_ref, dd_ref, dw_ref, dbg_ref, da_ref, edge, tab, g_ref, da_acc):
        i = pl.program_id(0)
        first = i == 0

        @pl.when(first)
        def _():
            edge[...] = jnp.zeros_like(edge)
            da_acc[...] = jnp.zeros_like(da_acc)
            _s5_scan_table(tab, a_ref[0:1, 0:n], -a_ref[0:1, n:2 * n], True)

        dy = dy_ref[...]
        u = u_ref[...]
        y0v = y0_ref[...]
        yg, t = _gelu_parts(y0v)
        z = _dot(_bf(yg), w_ref[...]) + bg_ref[...]
        sg = _sigmoid(z)
        dz = dy * yg * sg * (1.0 - sg)
        dyg = dy * sg + _dot_nt(_bf(dz), w_ref[...])
        _acc(dw_ref, _dot_tn(_bf(yg), _bf(dz)), first)
        _acc(dbg_ref, _colsum(dz), first)
        dy0 = dyg * _gelu_grad(y0v, t)
        _acc(dd_ref, _colsum(dy0 * u), first)
        dy0b = _bf(dy0)
        _acc(dc_ref.at[0:n, :], _dot_tn(_bf(h_ref[:, 0:n]), dy0b), first)
        _acc(dc_ref.at[n:2 * n, :], _dot_tn(_bf(h_ref[:, n:2 * n]), dy0b), first)
        g_ref[...] = _dot_nt(dy0b, c_ref[...])
        keep = jnp.where(i == nb - 1, 0.0, 1.0)
        row0 = _rows((SUBLANES, LANES)) == 0

        def grad_a(g, q, cols, gr, gi):
            cre, cim = cols
            rows = pl.ds(pl.multiple_of(g * SUBLANES, SUBLANES), SUBLANES)
            before = pl.ds(pl.multiple_of(jnp.maximum(g - 1, 0) * SUBLANES, SUBLANES), SUBLANES)
            pre = jnp.where(g == 0, hp_ref[:, cre] * keep, h_ref[before, cre])
            pim = jnp.where(g == 0, hp_ref[:, cim] * keep, h_ref[before, cim])
            pr = jnp.where(row0, pltpu.roll(pre, 1, 0), pltpu.roll(h_ref[rows, cre], 1, 0))
            pi = jnp.where(row0, pltpu.roll(pim, 1, 0), pltpu.roll(h_ref[rows, cim], 1, 0))
            da_acc[:, cre] += gr * pr + gi * pi
            da_acc[:, cim] += gi * pr - gr * pi

        _s5_scan(g_ref, g_ref, tab, edge, tb, True, grad_a)
        da_ref[...] = _colsum(da_acc[...])
        gr, gi = g_ref[:, 0:n], g_ref[:, n:2 * n]
        grb, gib = _bf(gr), _bf(gi)
        du = d_ref[...] * dy0 + _dot_nt(grb, b_ref[:, 0:n]) + _dot_nt(gib, b_ref[:, n:2 * n])
        ub = _bf(u)
        _acc(db_ref.at[:, 0:n], _dot_tn(ub, grb), first)
        _acc(db_ref.at[:, n:2 * n], _dot_tn(ub, gib), first)
        du_ref[...] = _bf(du)
        _acc(cs_ref, _colsum(du), first)

    rev = lambda i: (nb - 1 - i, 0)
    vec = pl.BlockSpec((1, D_GROUP), lambda i: (0, 0))
    vshape = jax.ShapeDtypeStruct((1, D_GROUP), F32)
    return pl.pallas_call(
        body, name=name, grid=(nb,),
        in_specs=[pl.BlockSpec((tb, D_GROUP), rev),
                  pl.BlockSpec((tb, D_GROUP), rev),
                  pl.BlockSpec((tb, D_GROUP), rev),
                  pl.BlockSpec((tb, 2 * n), rev),
                  pl.BlockSpec((8, 2 * n), lambda i: (jnp.maximum((nb - 1 - i) * halo - 1, 0), 0)),
                  pl.BlockSpec((1, 2 * n), lambda i: (0, 0)),
                  pl.BlockSpec((D_GROUP, 2 * n), lambda i: (0, 0)),
                  pl.BlockSpec((2 * n, D_GROUP), lambda i: (0, 0)),
                  vec,
                  pl.BlockSpec((None, D_GROUP, D_GROUP), lambda i: (layer, 0, 0)),
                  vec],
        out_specs=[pl.BlockSpec((tb, D_GROUP), rev), vec,
                   pl.BlockSpec((D_GROUP, 2 * n), lambda i: (0, 0)),
                   pl.BlockSpec((2 * n, D_GROUP), lambda i: (0, 0)),
                   vec,
                   pl.BlockSpec((D_GROUP, D_GROUP), lambda i: (0, 0)),
                   vec,
                   pl.BlockSpec((1, 2 * n), lambda i: (0, 0))],
        out_shape=[jax.ShapeDtypeStruct((s, D_GROUP), BF16), vshape,
                   jax.ShapeDtypeStruct((D_GROUP, 2 * n), F32), jax.ShapeDtypeStruct((2 * n, D_GROUP), F32),
                   vshape, jax.ShapeDtypeStruct((D_GROUP, D_GROUP), F32), vshape,
                   jax.ShapeDtypeStruct((1, 2 * n), F32)],
        scratch_shapes=[pltpu.VMEM((SUBLANES, 2 * n), F32), pltpu.VMEM((S5_TAB_ROWS, n), F32),
                        pltpu.VMEM((tb, 2 * n), F32), pltpu.VMEM((SUBLANES, 2 * n), F32)],
        compiler_params=_cparams(1),
    )(dmix, h_in, y0, hst, hst, a2, bexp, cexp, dskip, wglu, bglu)


CV_HALO = 32


def _gn_stats(hc, pmat):
    mu = _dot3(hc, pmat)
    xc = hc - mu
    var = _dot3(xc * xc, pmat)
    rstd = lax.rsqrt(var + LN_EPS)
    return xc * rstd, rstd


def cv_fwd(h_in, cw, cb, gg, gb, pmat, wpw, layer, bpw, tb, name):
    s = h_in.shape[0]
    hl = CV_HALO

    def body(v_ref, g_ref, cw_ref, cb_ref, gg_ref, gb_ref, p_ref, w_ref, bw_ref, y_ref, hc_ref, ext):
        @pl.when(pl.program_id(0) == 0)
        def _():
            ext[0:hl, :] = jnp.zeros((hl, D_GROUP), F32)

        ext[hl:hl + tb, :] = v_ref[...] * _sigmoid(g_ref[...])
        acc = jnp.zeros((tb, D_GROUP), F32) + cb_ref[...]
        for k in range(CONV_WIDTH):
            off = hl - (CONV_WIDTH - 1) + k
            acc = acc + cw_ref[k:k + 1, :] * ext[off:off + tb, :]
        hc_ref[...] = acc
        ext[0:hl, :] = ext[tb:tb + hl, :]
        xn, _ = _gn_stats(acc, p_ref[...])
        hn = xn * gg_ref[...] + gb_ref[...]
        hs = hn * _sigmoid(hn)
        y_ref[...] = _bf(_dot(_bf(hs), w_ref[...]) + bw_ref[...])

    vec = pl.BlockSpec((1, D_GROUP), lambda i: (0, 0))
    sq = pl.BlockSpec((D_GROUP, D_GROUP), lambda i: (0, 0))
    return pl.pallas_call(
        body, name=name, grid=(s // tb,),
        in_specs=[pl.BlockSpec((tb, D_GROUP), lambda i: (i, 1)),
                  pl.BlockSpec((tb, D_GROUP), lambda i: (i, 2)),
                  pl.BlockSpec((hl, D_GROUP), lambda i: (0, 0)),
                  vec, vec, vec, sq,
                  pl.BlockSpec((None, D_GROUP, D_GROUP), lambda i: (layer, 0, 0)),
                  vec],
        out_specs=[pl.BlockSpec((tb, D_GROUP), lambda i: (i, 0)), pl.BlockSpec((tb, D_GROUP), lambda i: (i, 0))],
        out_shape=[jax.ShapeDtypeStruct((s, D_GROUP), BF16), jax.ShapeDtypeStruct((s, D_GROUP), F32)],
        scratch_shapes=[pltpu.VMEM((hl + tb, D_GROUP), F32)],
        compiler_params=_cparams(1),
    )(h_in, h_in, cw, cb, gg, gb, pmat, wpw, bpw)


def cv_bwd(dmix, h_in, hc, cw, gg, gb, pmat, wpw, layer, tb, name):
    s = h_in.shape[0]
    hl = CV_HALO
    nb = s // tb
    per = tb // hl

    def body(dy_ref, v_ref, g_ref, vh_ref, gh_ref, hc_ref, cw_ref, gg_ref, gb_ref, p_ref, w_ref,
             dvg_ref, cs_ref, dcw_ref, dcb_ref, dgg_ref, dgb_ref, dw_ref, dbw_ref, ext, dext, head):
        i = pl.program_id(0)
        first = i == 0

        @pl.when(first)
        def _():
            head[...] = jnp.zeros_like(head)

        dy = dy_ref[...]
        pm = p_ref[...]
        xn, rstd = _gn_stats(hc_ref[...], pm)
        hn = xn * gg_ref[...] + gb_ref[...]
        sg = _sigmoid(hn)
        hs = hn * sg
        dyb = _bf(dy)
        _acc(dbw_ref, _colsum(dy), first)
        _acc(dw_ref, _dot_tn(_bf(hs), dyb), first)
        dhs = _dot_nt(dyb, w_ref[...])
        dhn = dhs * sg * (1.0 + hn * (1.0 - sg))
        _acc(dgg_ref, _colsum(dhn * xn), first)
        _acc(dgb_ref, _colsum(dhn), first)
        dxn = dhn * gg_ref[...]
        dhc = rstd * (dxn - _dot3(dxn, pm) - xn * _dot3(dxn * xn, pm))
        _acc(dcb_ref, _colsum(dhc), first)
        v = v_ref[...]
        sgg = _sigmoid(g_ref[...])
        keep = jnp.where(i == nb - 1, 0.0, 1.0)
        ext[0:hl, :] = vh_ref[...] * _sigmoid(gh_ref[...]) * keep
        ext[hl:hl + tb, :] = v * sgg
        dext[0:tb, :] = dhc
        dext[tb:tb + hl, :] = head[...]
        head[...] = dhc[0:hl]
        dhg = jnp.zeros((tb, D_GROUP), F32)
        for k in range(CONV_WIDTH):
            off = hl - (CONV_WIDTH - 1) + k
            wk = _colsum(dhc * ext[off:off + tb, :])
            _acc(dcw_ref.at[k:k + 1, :], wk, first)
            back = CONV_WIDTH - 1 - k
            dhg = dhg + cw_ref[k:k + 1, :] * dext[back:back + tb, :]

        @pl.when(first)
        def _():
            dcw_ref[CONV_WIDTH:hl, :] = jnp.zeros((hl - CONV_WIDTH, D_GROUP), F32)

        dv = dhg * sgg
        dg = dhg * v * sgg * (1.0 - sgg)
        dvg_ref[:, 0:D_GROUP] = _bf(dv)
        dvg_ref[:, D_GROUP:2 * D_GROUP] = _bf(dg)
        _acc(cs_ref.at[:, 0:D_GROUP], _colsum(dv), first)
        _acc(cs_ref.at[:, D_GROUP:2 * D_GROUP], _colsum(dg), first)

    vec = pl.BlockSpec((1, D_GROUP), lambda i: (0, 0))
    sq = pl.BlockSpec((D_GROUP, D_GROUP), lambda i: (0, 0))
    tap = pl.BlockSpec((hl, D_GROUP), lambda i: (0, 0))
    vshape = jax.ShapeDtypeStruct((1, D_GROUP), F32)

    def blk(col):
        return pl.BlockSpec((tb, D_GROUP), lambda i: (nb - 1 - i, col))

    def halo_blk(col):
        return pl.BlockSpec((hl, D_GROUP), lambda i: (jnp.maximum((nb - 1 - i) * per - 1, 0), col))

    return pl.pallas_call(
        body, name=name, grid=(nb,),
        in_specs=[blk(1), blk(1), blk(2), halo_blk(1), halo_blk(2),
                  pl.BlockSpec((tb, D_GROUP), lambda i: (nb - 1 - i, 0)),
                  tap, vec, vec, sq,
                  pl.BlockSpec((None, D_GROUP, D_GROUP), lambda i: (layer, 0, 0))],
        out_specs=[pl.BlockSpec((tb, 2 * D_GROUP), lambda i: (nb - 1 - i, 0)),
                   pl.BlockSpec((1, 2 * D_GROUP), lambda i: (0, 0)),
                   tap, vec, vec, vec, sq, vec],
        out_shape=[jax.ShapeDtypeStruct((s, 2 * D_GROUP), BF16), jax.ShapeDtypeStruct((1, 2 * D_GROUP), F32),
                   jax.ShapeDtypeStruct((hl, D_GROUP), F32), vshape, vshape, vshape,
                   jax.ShapeDtypeStruct((D_GROUP, D_GROUP), F32), vshape],
        scratch_shapes=[pltpu.VMEM((hl + tb, D_GROUP), F32), pltpu.VMEM((tb + hl, D_GROUP), F32),
                        pltpu.VMEM((hl, D_GROUP), F32)],
        compiler_params=_cparams(1),
    )(dmix, h_in, h_in, h_in, h_in, hc, cw, gg, gb, pmat, wpw)


LRU_HALO = 8


def _lru_gates(xc, wr_ref, br_ref, wi_ref, bi_ref, sp_ref):
    xcb = _bf(xc)
    r = _sigmoid(_dot(xcb, wr_ref[...]) + br_ref[...])
    gi = _sigmoid(_dot(xcb, wi_ref[...]) + bi_ref[...])
    la = -LRU_C * r * sp_ref[...]
    a = jnp.exp(la)
    e2 = a * a
    sq = jnp.sqrt(-jnp.tanh(la) * (e2 + 1.0))
    return r, gi, a, e2, sq


def _rscan(a, b, tb, reverse):
    row = _rows(a.shape)
    sh = 1
    while sh < tb:
        if reverse:
            amt, mask = tb - sh, row < tb - sh
        else:
            amt, mask = sh, row >= sh
        a_s = jnp.where(mask, pltpu.roll(a, amt, 0), 1.0)
        b_s = jnp.where(mask, pltpu.roll(b, amt, 0), 0.0)
        b = b + a * b_s
        a = a * a_s
        sh *= 2
    return a, b


def lru_fwd(h_in, cw, cb, wr, br, wi, bi, sp, tb, name):
    s = h_in.shape[0]
    hl = LRU_HALO

    def body(xg_ref, xr_ref, cw_ref, cb_ref, wr_ref, br_ref, wi_ref, bi_ref, sp_ref, y_ref, xc_ref, h_ref, ext, carry):
        @pl.when(pl.program_id(0) == 0)
        def _():
            ext[0:hl, :] = jnp.zeros((hl, D_GROUP), F32)
            carry[...] = jnp.zeros_like(carry)

        ext[hl:hl + tb, :] = xr_ref[...]
        xc = jnp.zeros((tb, D_GROUP), F32) + cb_ref[...]
        for k in range(LRU_CONV_WIDTH):
            off = hl - (LRU_CONV_WIDTH - 1) + k
            xc = xc + cw_ref[k:k + 1, :] * ext[off:off + tb, :]
        xc_ref[...] = xc
        ext[0:hl, :] = ext[tb:tb + hl, :]
        r, gi, a, e2, sq = _lru_gates(xc, wr_ref, br_ref, wi_ref, bi_ref, sp_ref)
        pa, hloc = _rscan(a, sq * (gi * xc), tb, False)
        h = hloc + pa * carry[7:8, :]
        h_ref[...] = h
        carry[...] = h[tb - 8:tb]
        y_ref[...] = _bf(h * _gelu(xg_ref[...]))

    vec = pl.BlockSpec((1, D_GROUP), lambda i: (0, 0))
    sq_spec = pl.BlockSpec((D_GROUP, D_GROUP), lambda i: (0, 0))
    blk = pl.BlockSpec((tb, D_GROUP), lambda i: (i, 0))
    return pl.pallas_call(
        body, name=name, grid=(s // tb,),
        in_specs=[pl.BlockSpec((tb, D_GROUP), lambda i: (i, 3)),
                  pl.BlockSpec((tb, D_GROUP), lambda i: (i, 4)),
                  pl.BlockSpec((hl, D_GROUP), lambda i: (0, 0)),
                  vec, sq_spec, vec, sq_spec, vec, vec],
        out_specs=[blk, blk, blk],
        out_shape=[jax.ShapeDtypeStruct((s, D_GROUP), BF16), jax.ShapeDtypeStruct((s, D_GROUP), F32),
                   jax.ShapeDtypeStruct((s, D_GROUP), F32)],
        scratch_shapes=[pltpu.VMEM((hl + tb, D_GROUP), F32), pltpu.VMEM((8, D_GROUP), F32)],
        compiler_params=_cparams(1),
    )(h_in, h_in, cw, cb, wr, br, wi, bi, sp)


def lru_bwd(dmix, h_in, xcs, hs, cw, wr, br, wi, bi, sp, tb, name):
    s = h_in.shape[0]
    hl = LRU_HALO
    nb = s // tb
    per = tb // hl

    def body(dy_ref, xg_ref, xr_ref, xrh_ref, xc_ref, h_ref, hp_ref, cw_ref, wr_ref, br_ref, wi_ref, bi_ref, sp_ref,
             dx_ref, cs_ref, dcw_ref, dcb_ref, dwr_ref, dbr_ref, dwi_ref, dbi_ref, dsp_ref,
             ext, dext, head, anext, gnext):
        i = pl.program_id(0)
        first = i == 0

        @pl.when(first)
        def _():
            head[...] = jnp.zeros_like(head)
            anext[...] = jnp.zeros_like(anext)
            gnext[...] = jnp.zeros_like(gnext)

        dy = dy_ref[...]
        xg = xg_ref[...]
        xc = xc_ref[...]
        h = h_ref[...]
        r, gi, a, e2, sq = _lru_gates(xc, wr_ref, br_ref, wi_ref, bi_ref, sp_ref)
        gate, t = _gelu_parts(xg)
        dh = dy * gate
        dxg = dy * h * _gelu_grad(xg, t)
        row = _rows((tb, D_GROUP))
        coef = jnp.where(row == tb - 1, anext[0:1, :], pltpu.roll(a, tb - 1, 0))
        pc, gloc = _rscan(coef, dh, tb, True)
        gfull = gloc + pc * gnext[0:1, :]
        anext[...] = a[0:8]
        gnext[...] = gfull[0:8]
        keep = jnp.where(i == nb - 1, 0.0, 1.0)
        hprev = jnp.where(row == 0, hp_ref[7:8, :] * keep, pltpu.roll(h, 1, 0))
        da = gfull * hprev
        uu = gi * xc
        dsq = gfull * uu
        duu = gfull * sq
        dla = da * a - dsq * e2 / sq
        sp = sp_ref[...]
        dr = dla * (-LRU_C) * sp
        _acc(dsp_ref, _colsum(dla * (-LRU_C) * r), first)
        dzr = dr * r * (1.0 - r)
        dzi = duu * xc * gi * (1.0 - gi)
        dzrb, dzib = _bf(dzr), _bf(dzi)
        dxc = duu * gi + _dot_nt(dzrb, wr_ref[...]) + _dot_nt(dzib, wi_ref[...])
        xcb = _bf(xc)
        _acc(dwr_ref, _dot_tn(xcb, dzrb), first)
        _acc(dwi_ref, _dot_tn(xcb, dzib), first)
        _acc(dbr_ref, _colsum(dzr), first)
        _acc(dbi_ref, _colsum(dzi), first)
        _acc(dcb_ref, _colsum(dxc), first)
        ext[0:hl, :] = xrh_ref[...] * keep
        ext[hl:hl + tb, :] = xr_ref[...]
        dext[0:tb, :] = dxc
        dext[tb:tb + hl, :] = head[...]
        head[...] = dxc[0:hl]
        dxr = jnp.zeros((tb, D_GROUP), F32)
        for k in range(LRU_CONV_WIDTH):
            off = hl - (LRU_CONV_WIDTH - 1) + k
            _acc(dcw_ref.at[k:k + 1, :], _colsum(dxc * ext[off:off + tb, :]), first)
            back = LRU_CONV_WIDTH - 1 - k
            dxr = dxr + cw_ref[k:k + 1, :] * dext[back:back + tb, :]

        @pl.when(first)
        def _():
            dcw_ref[LRU_CONV_WIDTH:hl, :] = jnp.zeros((hl - LRU_CONV_WIDTH, D_GROUP), F32)

        dx_ref[:, 0:D_GROUP] = _bf(dxg)
        dx_ref[:, D_GROUP:2 * D_GROUP] = _bf(dxr)
        _acc(cs_ref.at[:, 0:D_GROUP], _colsum(dxg), first)
        _acc(cs_ref.at[:, D_GROUP:2 * D_GROUP], _colsum(dxr), first)

    vec = pl.BlockSpec((1, D_GROUP), lambda i: (0, 0))
    sq_spec = pl.BlockSpec((D_GROUP, D_GROUP), lambda i: (0, 0))
    tap = pl.BlockSpec((hl, D_GROUP), lambda i: (0, 0))
    vshape = jax.ShapeDtypeStruct((1, D_GROUP), F32)
    sshape = jax.ShapeDtypeStruct((D_GROUP, D_GROUP), F32)

    def blk(col):
        return pl.BlockSpec((tb, D_GROUP), lambda i: (nb - 1 - i, col))

    def halo_blk(col):
        return pl.BlockSpec((hl, D_GROUP), lambda i: (jnp.maximum((nb - 1 - i) * per - 1, 0), col))

    return pl.pallas_call(
        body, name=name, grid=(nb,),
        in_specs=[blk(2), blk(3), blk(4), halo_blk(4), blk(0), blk(0), halo_blk(0),
                  tap, sq_spec, vec, sq_spec, vec, vec],
        out_specs=[pl.BlockSpec((tb, 2 * D_GROUP), lambda i: (nb - 1 - i, 0)),
                   pl.BlockSpec((1, 2 * D_GROUP), lambda i: (0, 0)),
                   tap, vec, sq_spec, vec, sq_spec, vec, vec],
        out_shape=[jax.ShapeDtypeStruct((s, 2 * D_GROUP), BF16), jax.ShapeDtypeStruct((1, 2 * D_GROUP), F32),
                   jax.ShapeDtypeStruct((hl, D_GROUP), F32), vshape, sshape, vshape, sshape, vshape, vshape],
        scratch_shapes=[pltpu.VMEM((hl + tb, D_GROUP), F32), pltpu.VMEM((tb + hl, D_GROUP), F32),
                        pltpu.VMEM((hl, D_GROUP), F32), pltpu.VMEM((8, D_GROUP), F32), pltpu.VMEM((8, D_GROUP), F32)],
        compiler_params=_cparams(1),
    )(dmix, h_in, h_in, h_in, xcs, hs, hs, cw, wr, br, wi, bi, sp)


ATTN_HEADS = 4
ATTN_HEAD_DIM = 64
ATTN_SCALE = ATTN_HEAD_DIM ** -0.5


def _head_mask(h):
    lane = lax.broadcasted_iota(jnp.int32, (1, D_GROUP), 1)
    return jnp.where((lane >= h * ATTN_HEAD_DIM) & (lane < (h + 1) * ATTN_HEAD_DIM), 1.0, 0.0)


def _softmax_rows(sc):
    e = jnp.exp(sc - jnp.max(sc, -1, keepdims=True))
    return e / jnp.sum(e, -1, keepdims=True)


def attn_fwd(h_in, kv, tb, name):
    s = h_in.shape[0]

    def body(q_ref, kv_ref, y_ref):
        q = q_ref[...]
        kb = _bf(kv_ref[:, 0:D_GROUP])
        vb = _bf(kv_ref[:, D_GROUP:2 * D_GROUP])
        out = jnp.zeros((tb, D_GROUP), F32)
        for h in range(ATTN_HEADS):
            mask = _head_mask(h)
            p = _softmax_rows(_dot_nt(_bf(q * mask), kb) * ATTN_SCALE)
            out = out + _dot(_bf(p), vb) * mask
        y_ref[...] = _bf(out)

    return pl.pallas_call(
        body, name=name, grid=(s // tb,),
        in_specs=[pl.BlockSpec((tb, D_GROUP), lambda i: (i, 5)),
                  pl.BlockSpec((D_GROUP, 2 * D_GROUP), lambda i: (0, 0))],
        out_specs=pl.BlockSpec((tb, D_GROUP), lambda i: (i, 0)),
        out_shape=jax.ShapeDtypeStruct((s, D_GROUP), BF16),
        compiler_params=_cparams(1),
    )(h_in, kv)


def attn_bwd(dmix, h_in, kv, tb, name):
    s = h_in.shape[0]

    def body(do_ref, q_ref, kv_ref, dq_ref, cs_ref, dkv_ref):
        first = pl.program_id(0) == 0
        q = q_ref[...]
        do = do_ref[...]
        kb = _bf(kv_ref[:, 0:D_GROUP])
        vb = _bf(kv_ref[:, D_GROUP:2 * D_GROUP])
        dq = jnp.zeros((tb, D_GROUP), F32)
        dk = jnp.zeros((D_GROUP, D_GROUP), F32)
        dv = jnp.zeros((D_GROUP, D_GROUP), F32)
        for h in range(ATTN_HEADS):
            mask = _head_mask(h)
            qm = _bf(q * mask)
            p = _softmax_rows(_dot_nt(qm, kb) * ATTN_SCALE)
            dom = _bf(do * mask)
            dp = _dot_nt(dom, vb)
            dv = dv + _dot_tn(_bf(p), dom)
            ds = _bf(p * (dp - jnp.sum(dp * p, -1, keepdims=True)) * ATTN_SCALE)
            dq = dq + _dot(ds, kb) * mask
            dk = dk + _dot_tn(ds, qm)
        dq_ref[...] = _bf(dq)
        _acc(cs_ref, _colsum(dq), first)
        _acc(dkv_ref.at[:, 0:D_GROUP], dk, first)
        _acc(dkv_ref.at[:, D_GROUP:2 * D_GROUP], dv, first)

    return pl.pallas_call(
        body, name=name, grid=(s // tb,),
        in_specs=[pl.BlockSpec((tb, D_GROUP), lambda i: (i, 3)),
                  pl.BlockSpec((tb, D_GROUP), lambda i: (i, 5)),
                  pl.BlockSpec((D_GROUP, 2 * D_GROUP), lambda i: (0, 0))],
        out_specs=[pl.BlockSpec((tb, D_GROUP), lambda i: (i, 0)),
                   pl.BlockSpec((1, D_GROUP), lambda i: (0, 0)),
                   pl.BlockSpec((D_GROUP, 2 * D_GROUP), lambda i: (0, 0))],
        out_shape=[jax.ShapeDtypeStruct((s, D_GROUP), BF16), jax.ShapeDtypeStruct((1, D_GROUP), F32),
                   jax.ShapeDtypeStruct((D_GROUP, 2 * D_GROUP), F32)],
        compiler_params=_cparams(1),
    )(dmix, h_in, kv)


FFN_RB = 16
FFN_UNROLL_FWD = 4
FFN_UNROLL_BWD = 2
FFN_TAP_ROWS = 8
FFN_TN = D_FF // 2


def _shift_down(cur, tail, k):
    return pltpu.roll(jnp.concatenate([tail, cur], axis=0), k, 0)[SUBLANES:]


def _shift_up(cur, head, k):
    rb = cur.shape[0]
    return pltpu.roll(jnp.concatenate([cur, head], axis=0), rb + SUBLANES - k, 0)[:rb]


def _fold8(v):
    tot = v[0:SUBLANES]
    for t in range(1, v.shape[0] // SUBLANES):
        tot = tot + v[t * SUBLANES:(t + 1) * SUBLANES]
    return tot


def _strip(r):
    return pl.ds(pl.multiple_of(r * FFN_RB, FFN_RB), FFN_RB)


def ffn_act_fwd(u, cw, cb, tb, name):
    s = u.shape[0]
    rb = FFN_RB
    nct = D_FF // FFN_TN
    nstrip = tb // rb

    def body(uv_ref, ug_ref, wv_ref, wg_ref, bv_ref, bg_ref, hf_ref, keep_ref, tailv, tailg):
        @pl.when(pl.program_id(1) == 0)
        def _():
            tailv[...] = jnp.zeros_like(tailv)
            tailg[...] = jnp.zeros_like(tailg)

        for cc in range(FFN_TN // LANES):
            cols = slice(cc * LANES, (cc + 1) * LANES)
            wv = [wv_ref[k:k + 1, cols] for k in range(FFN_CONV_WIDTH)]
            wg = [wg_ref[k:k + 1, cols] for k in range(FFN_CONV_WIDTH)]
            bv, bg = bv_ref[:, cols], bg_ref[:, cols]

            def strip(r, carry):
                tail_v, tail_g = carry
                cur_v, cur_g = uv_ref[_strip(r), cols], ug_ref[_strip(r), cols]
                vc = wv[0] * _shift_down(cur_v, tail_v, 2) + wv[1] * _shift_down(cur_v, tail_v, 1) + wv[2] * cur_v + bv
                gc = wg[0] * _shift_down(cur_g, tail_g, 2) + wg[1] * _shift_down(cur_g, tail_g, 1) + wg[2] * cur_g + bg
                ge, t = _gelu_parts(gc)
                hf_ref[_strip(r), cols] = _bf(vc * ge)
                keep_ref[0, _strip(r), cols] = _bf(vc)
                keep_ref[1, _strip(r), cols] = _bf(ge)
                keep_ref[2, _strip(r), cols] = _bf(_gelu_grad(gc, t))
                return cur_v[rb - SUBLANES:], cur_g[rb - SUBLANES:]

            def strips(q, carry):
                for k in range(FFN_UNROLL_FWD):
                    carry = strip(q * FFN_UNROLL_FWD + k, carry)
                return carry

            last_v, last_g = lax.fori_loop(0, nstrip // FFN_UNROLL_FWD, strips, (tailv[:, cols], tailg[:, cols]))
            tailv[:, cols] = last_v
            tailg[:, cols] = last_g

    return pl.pallas_call(
        body, name=name, grid=(nct, s // tb),
        in_specs=[pl.BlockSpec((tb, FFN_TN), lambda c, i: (i, c)),
                  pl.BlockSpec((tb, FFN_TN), lambda c, i: (i, c + nct)),
                  pl.BlockSpec((FFN_TAP_ROWS, FFN_TN), lambda c, i: (0, c)),
                  pl.BlockSpec((FFN_TAP_ROWS, FFN_TN), lambda c, i: (0, c + nct)),
                  pl.BlockSpec((1, FFN_TN), lambda c, i: (0, c)),
                  pl.BlockSpec((1, FFN_TN), lambda c, i: (0, c + nct))],
        out_specs=[pl.BlockSpec((tb, FFN_TN), lambda c, i: (i, c)),
                   pl.BlockSpec((3, tb, FFN_TN), lambda c, i: (0, i, c))],
        out_shape=[jax.ShapeDtypeStruct((s, D_FF), BF16), jax.ShapeDtypeStruct((3, s, D_FF), BF16)],
        scratch_shapes=[pltpu.VMEM((SUBLANES, FFN_TN), F32), pltpu.VMEM((SUBLANES, FFN_TN), F32)],
        compiler_params=_cparams(2),
    )(u, u, cw, cw, cb, cb)


def ffn_act_bwd(dhf, u, kept, cw, tb, name):
    s = u.shape[0]
    rb = FFN_RB
    nct = D_FF // FFN_TN
    nb = s // tb
    nstrip = tb // rb
    ntap = FFN_CONV_WIDTH

    def body(dh_ref, uv_ref, ug_ref, kept_ref, wv_ref, wg_ref, du_ref, dwv_ref, dwg_ref, headv, headg):
        i = pl.program_id(1)
        first = i == 0

        @pl.when(first)
        def _():
            headv[...] = jnp.zeros_like(headv)
            headg[...] = jnp.zeros_like(headg)
            dwv_ref[...] = jnp.zeros_like(dwv_ref)
            dwg_ref[...] = jnp.zeros_like(dwg_ref)

        zero = jnp.zeros((SUBLANES, LANES), F32)
        for cc in range(FFN_TN // LANES):
            cols = slice(cc * LANES, (cc + 1) * LANES)
            wv = [wv_ref[k:k + 1, cols] for k in range(ntap)]
            wg = [wg_ref[k:k + 1, cols] for k in range(ntap)]

            def strip(ii, carry):
                head_dv, head_dg, acc_v, acc_g = carry
                r = nstrip - 1 - ii
                dh = dh_ref[_strip(r), cols]
                dvc = dh * kept_ref[1, _strip(r), cols].astype(F32)
                dgc = dh * kept_ref[0, _strip(r), cols].astype(F32) * kept_ref[2, _strip(r), cols].astype(F32)
                sdv = [_shift_up(dvc, head_dv, 2), _shift_up(dvc, head_dv, 1), dvc]
                sdg = [_shift_up(dgc, head_dg, 2), _shift_up(dgc, head_dg, 1), dgc]
                cur_v, cur_g = uv_ref[_strip(r), cols], ug_ref[_strip(r), cols]
                acc_v = tuple(acc_v[k] + _fold8(cur_v * sdv[k]) for k in range(ntap)) + (acc_v[ntap] + _fold8(dvc),)
                acc_g = tuple(acc_g[k] + _fold8(cur_g * sdg[k]) for k in range(ntap)) + (acc_g[ntap] + _fold8(dgc),)
                du_v = wv[0] * sdv[0] + wv[1] * sdv[1] + wv[2] * sdv[2]
                du_g = wg[0] * sdg[0] + wg[1] * sdg[1] + wg[2] * sdg[2]
                du_ref[0, _strip(r), cols] = _bf(du_v)
                du_ref[1, _strip(r), cols] = _bf(du_g)
                return dvc[0:SUBLANES], dgc[0:SUBLANES], acc_v, acc_g

            init = (headv[:, cols], headg[:, cols], (zero,) * (ntap + 1), (zero,) * (ntap + 1))
            def strips(q, carry):
                for k in range(FFN_UNROLL_BWD):
                    carry = strip(q * FFN_UNROLL_BWD + k, carry)
                return carry

            top_dv, top_dg, acc_v, acc_g = lax.fori_loop(0, nstrip // FFN_UNROLL_BWD, strips, init)
            headv[:, cols] = top_dv
            headg[:, cols] = top_dg
            for k in range(ntap + 1):
                dwv_ref[k:k + 1, cols] += _colsum(acc_v[k])
                dwg_ref[k:k + 1, cols] += _colsum(acc_g[k])

    def blk(shift):
        return pl.BlockSpec((tb, FFN_TN), lambda c, i: (nb - 1 - i, c + shift))

    tapv = pl.BlockSpec((FFN_TAP_ROWS, FFN_TN), lambda c, i: (0, c))
    tapg = pl.BlockSpec((FFN_TAP_ROWS, FFN_TN), lambda c, i: (0, c + nct))
    return pl.pallas_call(
        body, name=name, grid=(nct, nb),
        in_specs=[blk(0), blk(0), blk(nct), pl.BlockSpec((3, tb, FFN_TN), lambda c, i: (0, nb - 1 - i, c)), tapv, tapg],
        out_specs=[pl.BlockSpec((2, tb, FFN_TN), lambda c, i: (0, nb - 1 - i, c)), tapv, tapv],
        out_shape=[jax.ShapeDtypeStruct((2, s, D_FF), BF16), jax.ShapeDtypeStruct((FFN_TAP_ROWS, D_FF), F32),
                   jax.ShapeDtypeStruct((FFN_TAP_ROWS, D_FF), F32)],
        scratch_shapes=[pltpu.VMEM((SUBLANES, FFN_TN), F32), pltpu.VMEM((SUBLANES, FFN_TN), F32)],
        compiler_params=_cparams(2),
    )(dhf, u, u, kept, cw, cw)


def _place():
    x, y, c = lax.axis_index("x"), lax.axis_index("y"), lax.axis_index("c")
    return x, y, c, 2 * x + y


def _chip_peer(x, y, d):
    return jnp.bitwise_xor(x, d >> 1), jnp.bitwise_xor(y, d & 1)


def gather_weights(slabs):
    n = len(slabs)

    def body(*refs):
        outs = refs[n:2 * n]
        ssem, rsem, fsem, gsem = refs[2 * n:]
        x, y, c, j = _place()
        sib = (x, y, 1 - c)
        ici = {}
        for w in range(n):
            for d in (1, 2, 3):
                px, py = _chip_peer(x, y, d)
                cp = pltpu.make_async_remote_copy(
                    src_ref=outs[w].at[c, j], dst_ref=outs[w].at[c, j], send_sem=ssem.at[w, d - 1],
                    recv_sem=rsem.at[w, d - 1], device_id=(px, py, c), device_id_type=MESH)
                cp.start()
                ici[w, d] = cp
        fwd = {}
        for d in (1, 2, 3):
            jd = jnp.bitwise_xor(j, d)
            for w in range(n):
                ici[w, d].wait_recv()
                cp = pltpu.make_async_remote_copy(
                    src_ref=outs[w].at[c, jd], dst_ref=outs[w].at[c, jd], send_sem=fsem.at[w, d - 1],
                    recv_sem=gsem.at[w, d - 1], device_id=sib, device_id_type=MESH)
                cp.start()
                fwd[w, d] = cp
        for w in range(n):
            for d in (1, 2, 3):
                fwd[w, d].wait_recv()
                fwd[w, d].wait_send()
                ici[w, d].wait_send()

    out_shape = [jax.ShapeDtypeStruct(a.shape, a.dtype) for a in slabs]
    return pl.pallas_call(
        body, name="gather_weights", in_specs=[ANY] * n, out_specs=[ANY] * n, out_shape=out_shape,
        input_output_aliases={w: w for w in range(n)},
        scratch_shapes=[pltpu.SemaphoreType.DMA((n, 3)), pltpu.SemaphoreType.DMA((n, 3)),
                        pltpu.SemaphoreType.DMA((n, 3)), pltpu.SemaphoreType.DMA((n, 3))],
    )(*slabs)


def _own_slab(shard, jidx):
    slab = jnp.zeros((DEPTH, N_CHIPS) + shard.shape[1:], shard.dtype)
    return lax.dynamic_update_slice_in_dim(slab, shard[:, None], jidx, axis=1)


def exchange_layers(gbig, small):
    n = len(gbig)

    def body(*refs):
        g_refs, s_ref = refs[:n], refs[n]
        got_refs, all_ref = refs[n + 1:2 * n + 1], refs[2 * n + 1]
        dsem, esem, lsem, ssem, rsem, fsem, hsem = refs[2 * n + 2:]
        x, y, c, j = _place()
        sib = (x, y, 1 - c)
        me = 4 * x + 2 * y + c
        big = []
        for k in range(n):
            cp = pltpu.make_async_remote_copy(src_ref=g_refs[k].at[1 - c], dst_ref=got_refs[k], send_sem=dsem.at[k],
                                              recv_sem=esem.at[k], device_id=sib, device_id_type=MESH)
            cp.start()
            big.append(cp)
        mine = pltpu.make_async_copy(s_ref, all_ref.at[me], lsem)
        mine.start()

        def small_copy(k, block, to, sems, from_input):
            return pltpu.make_async_remote_copy(
                src_ref=s_ref if from_input else all_ref.at[block], dst_ref=all_ref.at[block],
                send_sem=sems[0].at[k], recv_sem=sems[1].at[k], device_id=to, device_id_type=MESH)

        first = [small_copy(0, me, sib, (ssem, rsem), True)]
        for d in (1, 2, 3):
            px, py = _chip_peer(x, y, d)
            first.append(small_copy(d, me, (px, py, c), (ssem, rsem), True))
        for cp in first:
            cp.start()
        passed = []
        for d in (1, 2, 3):
            px, py = _chip_peer(x, y, d)
            src_block = 4 * px + 2 * py + c
            small_copy(d, src_block, sib, (ssem, rsem), False).wait_recv()
            cp = small_copy(d - 1, src_block, sib, (fsem, hsem), False)
            cp.start()
            passed.append(cp)
        small_copy(0, me, sib, (ssem, rsem), False).wait_recv()
        for cp in passed:
            cp.wait_recv()
        for cp in first + passed:
            cp.wait_send()
        mine.wait()
        for cp in big:
            cp.wait()

    r2 = small.shape[0]
    return pl.pallas_call(
        body, name="exchange_layers", in_specs=[ANY] * (n + 1), out_specs=[ANY] * (n + 1),
        out_shape=[jax.ShapeDtypeStruct(g.shape[1:], F32) for g in gbig] + [jax.ShapeDtypeStruct((8, r2, LANES), F32)],
        scratch_shapes=[pltpu.SemaphoreType.DMA((n,)), pltpu.SemaphoreType.DMA((n,)), pltpu.SemaphoreType.DMA,
                        pltpu.SemaphoreType.DMA((4,)), pltpu.SemaphoreType.DMA((4,)),
                        pltpu.SemaphoreType.DMA((3,)), pltpu.SemaphoreType.DMA((3,))],
    )(*gbig, small)


def scatter_shards(s1):
    n = len(s1)

    def body(*refs):
        s_refs, got_refs = refs[:n], refs[n:2 * n]
        ssem, rsem = refs[2 * n:]
        x, y, c, j = _place()
        cps = []
        for d in (1, 2, 3):
            px, py = _chip_peer(x, y, d)
            for k in range(n):
                cp = pltpu.make_async_remote_copy(
                    src_ref=s_refs[k].at[jnp.bitwise_xor(j, d)], dst_ref=got_refs[k].at[d - 1],
                    send_sem=ssem.at[k, d - 1], recv_sem=rsem.at[k, d - 1], device_id=(px, py, c), device_id_type=MESH)
                cp.start()
                cps.append(cp)
        for cp in cps:
            cp.wait()

    return pl.pallas_call(
        body, name="scatter_shards", in_specs=[ANY] * n, out_specs=[ANY] * n,
        out_shape=[jax.ShapeDtypeStruct((3,) + a.shape[1:], a.dtype) for a in s1],
        scratch_shapes=[pltpu.SemaphoreType.DMA((n, 3)), pltpu.SemaphoreType.DMA((n, 3))],
    )(*s1)


def share_with_sibling(parts):
    n = len(parts)

    def body(*refs):
        out_refs = refs[n:2 * n]
        ssem, rsem = refs[2 * n:]
        x, y, c, j = _place()
        cps = []
        for k in range(n):
            cp = pltpu.make_async_remote_copy(src_ref=out_refs[k].at[c], dst_ref=out_refs[k].at[c], send_sem=ssem.at[k],
                                              recv_sem=rsem.at[k], device_id=(x, y, 1 - c), device_id_type=MESH)
            cp.start()
            cps.append(cp)
        for cp in cps:
            cp.wait()

    return pl.pallas_call(
        body, name="share_with_sibling", in_specs=[ANY] * n, out_specs=[ANY] * n,
        out_shape=[jax.ShapeDtypeStruct(a.shape, F32) for a in parts],
        input_output_aliases={k: k for k in range(n)},
        scratch_shapes=[pltpu.SemaphoreType.DMA((n,)), pltpu.SemaphoreType.DMA((n,))],
    )(*parts)


def add_layer_halves(g, got, cidx, name):
    _, nch, r, cdim = g.shape
    tr = _row_tile(r, cdim, mult=16)

    def body(c_ref, a_ref, b_ref, o_ref):
        o_ref[...] = _bf(a_ref[...] + b_ref[...])

    grid_spec = pltpu.PrefetchScalarGridSpec(
        num_scalar_prefetch=1, grid=(nch, r // tr),
        in_specs=[pl.BlockSpec((None, None, tr, cdim), lambda jj, i, c_ref: (c_ref[0], jj, i, 0)),
                  pl.BlockSpec((None, tr, cdim), lambda jj, i, c_ref: (jj, i, 0))],
        out_specs=pl.BlockSpec((None, tr, cdim), lambda jj, i, c_ref: (jj, i, 0)))
    return pl.pallas_call(
        body, name=name, grid_spec=grid_spec,
        out_shape=jax.ShapeDtypeStruct((nch, r, cdim), BF16), compiler_params=_cparams(2),
    )(cidx, g, got)


def add_chip_parts(s1, got, jc, name):
    _, r, cdim = s1.shape
    tr = _row_tile(r, cdim, mult=16)

    def body(jc_ref, a_ref, g0_ref, g1_ref, g2_ref, o_ref):
        o_ref[...] = ((a_ref[...].astype(F32) + g0_ref[...].astype(F32)) + g1_ref[...].astype(F32)) + g2_ref[...].astype(F32)

    def slot(k):
        return pl.BlockSpec((None, tr, cdim), lambda i, jc_ref: (k, i, 0))

    grid_spec = pltpu.PrefetchScalarGridSpec(
        num_scalar_prefetch=1, grid=(r // tr,),
        in_specs=[pl.BlockSpec((None, tr, cdim), lambda i, jc_ref: (jc_ref[0], i, 0)), slot(0), slot(1), slot(2)],
        out_specs=pl.BlockSpec((None, tr, cdim), lambda i, jc_ref: (jc_ref[1], i, 0)))
    return pl.pallas_call(
        body, name=name, grid_spec=grid_spec,
        out_shape=jax.ShapeDtypeStruct((DEPTH, r, cdim), F32), compiler_params=_cparams(1),
    )(jc, s1, got, got, got)


def sum_devices(allp):
    _, r, _ = allp.shape

    def body(a_ref, o_ref):
        tot = a_ref[0]
        for k in range(1, 8):
            tot = tot + a_ref[k]
        o_ref[...] = tot

    tr = r // 2 if r % 16 == 0 else r
    return pl.pallas_call(
        body, name="sum_devices", grid=(r // tr,),
        in_specs=[pl.BlockSpec((8, tr, LANES), lambda i: (0, i, 0))],
        out_specs=pl.BlockSpec((tr, LANES), lambda i: (i, 0)),
        out_shape=jax.ShapeDtypeStruct((r, LANES), F32), compiler_params=_cparams(1),
    )(allp)


def _row_tile(r, cdim, limit_bytes=1 << 20, mult=8):
    best = None
    for tr in range(mult, r + 1, mult):
        if r % tr == 0 and tr * cdim * 4 <= limit_bytes:
            best = tr
    return best if best is not None else r


def adamw(w, g, m, v, name):
    r, cdim = w.shape
    tr = _row_tile(r, cdim)
    bc1 = 1.0 - ADAM_B1 ** ADAM_STEP
    bc2 = 1.0 - ADAM_B2 ** ADAM_STEP

    def body(w_ref, g_ref, m_ref, v_ref, d_ref, nm_ref, nv_ref, go_ref):
        gv = g_ref[...]
        nm = ADAM_B1 * m_ref[...] + (1.0 - ADAM_B1) * gv
        nv = ADAM_B2 * v_ref[...] + (1.0 - ADAM_B2) * (gv * gv)
        d_ref[...] = -ADAM_LR * ((nm / bc1) / (jnp.sqrt(nv / bc2) + ADAM_EPS) + ADAM_WD * w_ref[...])
        nm_ref[...] = nm
        nv_ref[...] = nv
        go_ref[...] = gv

    blk = pl.BlockSpec((tr, cdim), lambda i: (i, 0))
    shape = jax.ShapeDtypeStruct((r, cdim), F32)
    return pl.pallas_call(
        body, name=name, grid=(r // tr,), in_specs=[blk] * 4, out_specs=[blk] * 4, out_shape=[shape] * 4,
        compiler_params=_cparams(1),
    )(w, g, m, v)


def _s5_prepare(lam_re, lam_im, log_dt, b_re, b_im, c_re, c_im):
    groups, ch = 16, 16
    dt = jnp.exp(log_dt)[:, None]
    mag = jnp.exp(lam_re * dt)
    a_r, a_i = mag * jnp.cos(lam_im * dt), mag * jnp.sin(lam_im * dt)
    den = lam_re * lam_re + lam_im * lam_im
    q_r = ((a_r - 1.0) * lam_re + a_i * lam_im) / den
    q_i = (a_i * lam_re - (a_r - 1.0) * lam_im) / den
    bb_r = q_r[..., None] * b_re - q_i[..., None] * b_im
    bb_i = q_r[..., None] * b_im + q_i[..., None] * b_re
    eye = jnp.eye(groups, dtype=F32)

    def expand_b(bb):
        return jnp.einsum("gpc,gh->gchp", bb, eye).reshape(groups * ch, N_STATE)

    def expand_c(cc):
        return jnp.einsum("gcp,gh->hpgc", cc, eye).reshape(N_STATE, groups * ch)

    a2 = jnp.concatenate([a_r.reshape(1, N_STATE), a_i.reshape(1, N_STATE)], axis=1)
    bexp = jnp.concatenate([expand_b(bb_r), expand_b(bb_i)], axis=1)
    cexp = jnp.concatenate([expand_c(c_re), -expand_c(c_im)], axis=0)
    return a2, bexp, cexp


def _lru_prepare(w_r, w_i, lam):
    heads = 4
    eye = jnp.eye(heads, dtype=F32)

    def expand(w):
        return jnp.einsum("hij,hk->hikj", w, eye).reshape(D_GROUP, D_GROUP)

    return expand(w_r), expand(w_i), jax.nn.softplus(-lam).reshape(1, D_GROUP)


def _pad_rows(a, rows):
    return jnp.pad(a, ((0, rows - a.shape[0]), (0, 0)))


def _group_mean_matrix():
    gidx = jnp.arange(D_GROUP) // 64
    return (gidx[:, None] == gidx[None, :]).astype(BF16) * jnp.asarray(1.0 / 64.0, BF16)


def _pack_rows(arrs, width):
    parts = []
    for a in arrs:
        flat = a.reshape(-1)
        pad = (-flat.shape[0]) % width
        parts.append(jnp.pad(flat, (0, pad)) if pad else flat)
    flat = jnp.concatenate(parts)
    rows = flat.shape[0] // width
    pad_rows = (-rows) % 16
    if pad_rows:
        flat = jnp.pad(flat, (0, pad_rows * width))
    return flat.reshape(-1, width)


def _unpack_rows(packed, shapes, width):
    flat = packed.reshape(-1)
    out, off = [], 0
    for shp in shapes:
        size = math.prod(shp)
        out.append(flat[off:off + size].reshape(shp))
        off += size + ((-size) % width)
    return out


def kernel(x, mem, ln_in_g, ln_in_b, w_in, b_in, s5_lam_re, s5_lam_im, s5_log_dt, s5_b_re, s5_b_im, s5_c_re, s5_c_im, s5_d, s5_w_glu, s5_b_glu, cv_w, cv_b, cv_gn_g, cv_gn_b, cv_w_pw, cv_b_pw, lru_conv_w, lru_conv_b, lru_w_r, lru_b_r, lru_w_i, lru_b_i, lru_lam, attn_w_kv, w_out, b_out, ln1_g, ln1_b, ffn_w_up, ffn_conv_w, ffn_conv_b, ffn_w_down, ln2_g, ln2_b, loss_target, m_ln_in_g, m_ln_in_b, m_w_in, m_b_in, m_s5_lam_re, m_s5_lam_im, m_s5_log_dt, m_s5_b_re, m_s5_b_im, m_s5_c_re, m_s5_c_im, m_s5_d, m_s5_w_glu, m_s5_b_glu, m_cv_w, m_cv_b, m_cv_gn_g, m_cv_gn_b, m_cv_w_pw, m_cv_b_pw, m_lru_conv_w, m_lru_conv_b, m_lru_w_r, m_lru_b_r, m_lru_w_i, m_lru_b_i, m_lru_lam, m_attn_w_kv, m_w_out, m_b_out, m_ln1_g, m_ln1_b, m_ffn_w_up, m_ffn_conv_w, m_ffn_conv_b, m_ffn_w_down, m_ln2_g, m_ln2_b, v_ln_in_g, v_ln_in_b, v_w_in, v_b_in, v_s5_lam_re, v_s5_lam_im, v_s5_log_dt, v_s5_b_re, v_s5_b_im, v_s5_c_re, v_s5_c_im, v_s5_d, v_s5_w_glu, v_s5_b_glu, v_cv_w, v_cv_b, v_cv_gn_g, v_cv_gn_b, v_cv_w_pw, v_cv_b_pw, v_lru_conv_w, v_lru_conv_b, v_lru_w_r, v_lru_b_r, v_lru_w_i, v_lru_b_i, v_lru_lam, v_attn_w_kv, v_w_out, v_b_out, v_ln1_g, v_ln1_b, v_ffn_w_up, v_ffn_conv_w, v_ffn_conv_b, v_ffn_w_down, v_ln2_g, v_ln2_b):
    p = dict(locals())
    xs = x[0]
    mems = mem[0]
    target = loss_target[0]
    s = xs.shape[0]
    cidx = lax.axis_index("c")
    jidx = 2 * lax.axis_index("x") + lax.axis_index("y")
    tb_scan = min(256, s)
    tb_s5 = min(512, s)
    tb_attn = min(512, s)
    tb_ffn = min(256, s)

    small_sh_names = list(SMALL_SHARDED)
    small_sh_shapes = [p[nm].shape[1:] for nm in small_sh_names]
    small_pack = jnp.stack([_pack_rows([p[nm][l] for nm in small_sh_names], LANES) for l in range(DEPTH)])
    gathered = gather_weights([_own_slab(_bf(p[nm]), jidx) for nm in BIG] + [_own_slab(small_pack, jidx)])
    g_w_in, g_w_kv, g_w_out, g_w_up, g_w_down, g_small = gathered
    g_w_kv = g_w_kv.reshape(DEPTH, 1, D_MODEL, 2 * D_GROUP)
    g_w_out = g_w_out.reshape(DEPTH, 1, D_MODEL, D_MODEL)
    g_w_down = g_w_down.reshape(DEPTH, 1, D_FF, D_MODEL)
    full_small = {nm: [] for nm in small_sh_names}
    for l in range(DEPTH):
        per_chip = [_unpack_rows(g_small[l, jj], small_sh_shapes, LANES) for jj in range(N_CHIPS)]
        for k, nm in enumerate(small_sh_names):
            full_small[nm].append(jnp.concatenate([per_chip[jj][k] for jj in range(N_CHIPS)],
                                                  axis=SMALL_SHARDED[nm] - 1))
    w_glu_bf = _bf(jnp.stack(full_small['s5_w_glu']))
    w_pw_bf = _bf(jnp.stack(full_small['cv_w_pw']))
    pmat = _group_mean_matrix()

    def vec(a):
        return a.reshape(1, -1)

    xh0, rs0, xb0 = ln_fwd(xs, vec(ln_in_g), vec(ln_in_b), "ln_in")
    saved = []
    prev = dict(xh=xh0, rs=rs0, xb=xb0, g=vec(ln_in_g), b=vec(ln_in_b))
    for l in range(DEPTH):
        sv = dict(prev=prev)
        (a2, bexp, cexp), sv['s5_vjp'] = jax.vjp(_s5_prepare, s5_lam_re[l], s5_lam_im[l], s5_log_dt[l],
                                                 s5_b_re[l], s5_b_im[l], s5_c_re[l], s5_c_im[l])
        (wr, wi, sp), sv['lru_vjp'] = jax.vjp(_lru_prepare, lru_w_r[l], lru_w_i[l], lru_lam[l])
        sv.update(a2=a2, bexp=_bf(bexp), cexp=_bf(cexp), wr=_bf(wr), wi=_bf(wi), sp=sp)
        sv['cvw'] = _pad_rows(full_small['cv_w'][l], CV_HALO)
        sv['lcw'] = _pad_rows(full_small['lru_conv_w'][l], LRU_HALO)
        sv['fcw'] = _pad_rows(full_small['ffn_conv_w'][l], FFN_TAP_ROWS)
        h_in = mm_nn(prev['xb'], g_w_in, l, vec(b_in[l]), F32, 2048, f"in_proj{l}")
        kv = mm_nn(mems, g_w_kv, l, jnp.zeros((1, 2 * D_GROUP), F32), F32, 256, f"kv_proj{l}")
        y_s5, hst, y0 = s5_fwd(h_in, a2, sv['bexp'], sv['cexp'], vec(s5_d[l]), w_glu_bf, l, vec(s5_b_glu[l]),
                               tb_s5, f"s5_fwd{l}")
        y_cv, hc = cv_fwd(h_in, sv['cvw'], vec(cv_b[l]), vec(cv_gn_g[l]), vec(cv_gn_b[l]), pmat, w_pw_bf, l,
                          vec(cv_b_pw[l]), tb_scan, f"cv_fwd{l}")
        y_lru, xcs, hls = lru_fwd(h_in, sv['lcw'], vec(lru_conv_b[l]), sv['wr'], vec(lru_b_r[l]), sv['wi'],
                                  vec(lru_b_i[l]), sp, tb_scan, f"lru_fwd{l}")
        y_mem = attn_fwd(h_in, kv, tb_attn, f"attn_fwd{l}")
        mix_in = jnp.concatenate([y_s5, y_cv, y_lru, y_mem], axis=1)
        xh1, rs1, xb1 = proj_ln(mix_in, g_w_out.reshape(DEPTH, D_MODEL, D_MODEL), l, vec(b_out[l]),
                                prev['xh'], prev['g'], prev['b'], vec(ln1_g[l]), vec(ln1_b[l]), f"out_proj_ln{l}")
        u = mm_nn(xb1, g_w_up, l, jnp.zeros((1, 2 * D_FF), F32), F32, 1024, f"ffn_up{l}")
        hf, sv['ffn_kept'] = ffn_act_fwd(u, sv['fcw'], vec(ffn_conv_b[l]), tb_ffn, f"ffn_act{l}")
        xh2, rs2, xb2 = proj_ln(hf, g_w_down.reshape(DEPTH, D_FF, D_MODEL), l, jnp.zeros((1, D_MODEL), F32),
                                xh1, vec(ln1_g[l]), vec(ln1_b[l]), vec(ln2_g[l]), vec(ln2_b[l]), f"ffn_down_ln{l}")
        sv.update(h_in=h_in, kv=kv, hst=hst, y0=y0, hc=hc, xcs=xcs, hls=hls, mix_in=mix_in,
                  xh1=xh1, rs1=rs1, xb1=xb1, u=u, hf=hf, xh2=xh2, rs2=rs2)
        saved.append(sv)
        prev = dict(xh=xh2, rs=rs2, xb=xb2, g=vec(ln2_g[l]), b=vec(ln2_b[l]))

    grads = {}
    per_layer = {nm: [None] * DEPTH for nm in WEIGHTS if nm not in ('ln_in_g', 'ln_in_b')}
    gbig = [None] * len(BIG)
    below = None
    for l in reversed(range(DEPTH)):
        sv = saved[l]
        pv = sv['prev']
        if l == DEPTH - 1:
            dr2, dg2, db2, sqerr = loss_ln_bwd(target, sv['xh2'], sv['rs2'], vec(ln2_g[l]), vec(ln2_b[l]), "loss_ln2_bwd")
            loss_local = 0.5 / D_MODEL * jnp.sum(sqerr)
        else:
            dr2, dg2, db2 = below
        per_layer['ln2_g'][l], per_layer['ln2_b'][l] = dg2[0], db2[0]
        tm_nt = min(512, s)
        ts_big = min(2048, s)
        whole = lambda a_ref, j: a_ref[...]
        dhf = mm_nt(dr2, (tm_nt, D_MODEL), lambda i: (i, 0), whole, g_w_down, l, None, F32, 512, s, f"ffn_down_dx{l}")
        gbig[4] = mm_tn(sv['hf'], dr2, (min(1024, s), D_MODEL), lambda kt, j, st: (st, 0), 1, D_MODEL, FFN_TN, 1024, s,
                        l, gbig[4], f"ffn_down_dw{l}")
        du, dcwv, dcwg = ffn_act_bwd(dhf, sv['u'], sv['ffn_kept'], sv['fcw'], tb_ffn, f"ffn_act_bwd{l}")
        dcw = jnp.concatenate([dcwv, dcwg], axis=1)
        per_layer['ffn_conv_w'][l] = dcw[0:FFN_CONV_WIDTH]
        per_layer['ffn_conv_b'][l] = dcw[FFN_CONV_WIDTH]
        dr1, dg1, db1, cs1 = mm_nt(du, (2, tm_nt, D_FF), lambda i: (0, i, 0),
                                   lambda a_ref, j: a_ref[j // 2, :, (j % 2) * FFN_TN:(j % 2 + 1) * FFN_TN], g_w_up, l, dr2,
                                   F32, 512, s, f"ffn_up_dx_ln1_bwd{l}", ln=(sv['xh1'], sv['rs1'], vec(ln1_g[l])))
        gbig[3] = mm_tn(sv['xb1'], du, (None, ts_big, FFN_TN), lambda kt, j, st: (j // 2, st, j % 2), N_CHIPS, FFN_TN,
                        D_MODEL, 2048, s, l, gbig[3], f"ffn_up_dw{l}")
        per_layer['ln1_g'][l], per_layer['ln1_b'][l], per_layer['b_out'][l] = dg1[0], db1[0], cs1[0]
        dmix = mm_nt(dr1, (tm_nt, D_MODEL), lambda i: (i, 0), whole, g_w_out, l, None, F32, 512, s, f"out_proj_dx{l}")
        gbig[2] = mm_tn(sv['mix_in'], dr1, (min(1024, s), D_MODEL), lambda kt, j, st: (st, 0), 1, D_MODEL, D_MODEL, 1024, s,
                        l, gbig[2], f"out_proj_dw{l}")
        h_in = sv['h_in']
        (d_u, cs_u, d_bexp, d_cexp, d_dd, d_wglu, d_bglu, d_a2) = s5_bwd(
            dmix, h_in, sv['y0'], sv['hst'], sv['a2'], sv['bexp'], sv['cexp'], vec(s5_d[l]), w_glu_bf, l,
            vec(s5_b_glu[l]), tb_s5, f"s5_bwd{l}")
        (d_vg, cs_vg, d_cvw, d_cvb, d_gg, d_gb, d_wpw, d_bpw) = cv_bwd(
            dmix, h_in, sv['hc'], sv['cvw'], vec(cv_gn_g[l]), vec(cv_gn_b[l]), pmat, w_pw_bf, l, tb_scan, f"cv_bwd{l}")
        (d_lx, cs_lx, d_lcw, d_lcb, d_wr, d_br, d_wi, d_bi, d_sp) = lru_bwd(
            dmix, h_in, sv['xcs'], sv['hls'], sv['lcw'], sv['wr'], vec(lru_b_r[l]), sv['wi'], vec(lru_b_i[l]),
            sv['sp'], tb_scan, f"lru_bwd{l}")
        d_q, cs_q, d_kv = attn_bwd(dmix, h_in, sv['kv'], tb_attn, f"attn_bwd{l}")
        g_s5 = sv['s5_vjp']((d_a2, d_bexp, d_cexp))
        for nm, gval in zip(['s5_lam_re', 's5_lam_im', 's5_log_dt', 's5_b_re', 's5_b_im', 's5_c_re', 's5_c_im'], g_s5):
            per_layer[nm][l] = gval
        g_lru = sv['lru_vjp']((d_wr, d_wi, d_sp))
        for nm, gval in zip(['lru_w_r', 'lru_w_i', 'lru_lam'], g_lru):
            per_layer[nm][l] = gval
        per_layer['s5_d'][l], per_layer['s5_w_glu'][l], per_layer['s5_b_glu'][l] = d_dd[0], d_wglu, d_bglu[0]
        per_layer['cv_w'][l], per_layer['cv_b'][l] = d_cvw[0:CONV_WIDTH], d_cvb[0]
        per_layer['cv_gn_g'][l], per_layer['cv_gn_b'][l] = d_gg[0], d_gb[0]
        per_layer['cv_w_pw'][l], per_layer['cv_b_pw'][l] = d_wpw, d_bpw[0]
        per_layer['lru_conv_w'][l], per_layer['lru_conv_b'][l] = d_lcw[0:LRU_CONV_WIDTH], d_lcb[0]
        per_layer['lru_b_r'][l], per_layer['lru_b_i'][l] = d_br[0], d_bi[0]
        per_layer['b_in'][l] = jnp.concatenate([cs_u, cs_vg, cs_lx, cs_q], axis=1)[0]
        gbig[1] = mm_tn(mems, d_kv, (MEM_ROWS, 2 * D_GROUP), lambda kt, j, st: (st, 0), 1, 2 * D_GROUP, D_MODEL, MEM_ROWS,
                        MEM_ROWS, l, gbig[1], f"kv_proj_dw{l}")
        dh_in = jnp.concatenate([d_u, d_vg, d_lx, d_q], axis=1)
        n_sh = N_IN // N_CHIPS
        below = mm_nt(dh_in, (tm_nt, N_IN), lambda i: (i, 0), lambda a_ref, j: a_ref[:, j * n_sh:(j + 1) * n_sh],
                      g_w_in, l, dr1, F32, 512, s, f"in_proj_dx_ln_bwd{l}", ln=(pv['xh'], pv['rs'], pv['g']))[:3]
        gbig[0] = mm_tn(pv['xb'], dh_in, (ts_big, n_sh), lambda kt, j, st: (st, j), N_CHIPS, n_sh, D_MODEL, 2048, s,
                        l, gbig[0], f"in_proj_dw{l}")
    grad_x, dg_in, db_in = below
    grads['ln_in_g'], grads['ln_in_b'] = dg_in[0], db_in[0]
    for nm, vals in per_layer.items():
        if nm not in BIG:
            grads[nm] = jnp.stack(vals)

    small_names = [nm for nm in WEIGHTS if nm not in BIG]
    small_local = _pack_rows([grads[nm] for nm in small_names], LANES)
    gbig = [g.reshape((DEPTH, N_CHIPS) + p[nm].shape[1:]) for g, nm in zip(gbig, BIG)]
    c1 = cidx.reshape(1).astype(jnp.int32)
    jc = jnp.stack([jidx, cidx]).astype(jnp.int32)
    *got_a, small_all = exchange_layers(gbig, small_local)
    s1 = [add_layer_halves(g, ga, c1, f"add_cores_{nm}") for g, ga, nm in zip(gbig, got_a, BIG)]
    got_b = scatter_shards(s1)
    mine_r = [add_chip_parts(sk, gb, jc, f"add_chips_{nm}") for sk, gb, nm in zip(s1, got_b, BIG)]
    red_big = share_with_sibling(mine_r)
    small_red = sum_devices(small_all)
    small_grads = dict(zip(small_names, _unpack_rows(small_red, [grads[nm].shape for nm in small_names], LANES)))

    out_g, out_d, out_m, out_v = {}, {}, {}, {}
    for k, nm in enumerate(BIG):
        gk = red_big[k]
        two_d = (-1, p[nm].shape[-1])
        res = adamw(p[nm].reshape(two_d), gk.reshape(two_d), p['m_' + nm].reshape(two_d),
                    p['v_' + nm].reshape(two_d), f"adamw_{nm}")
        out_d[nm], out_m[nm], out_v[nm], out_g[nm] = (t.reshape(p[nm].shape) for t in res)
    own = {}
    for nm in small_names:
        gfull = small_grads[nm]
        if nm in SMALL_SHARDED:
            ax = SMALL_SHARDED[nm]
            width = p[nm].shape[ax]
            gfull = lax.dynamic_slice_in_dim(gfull, jidx * width, width, axis=ax)
        own[nm] = gfull
    packs = [_pack_rows([src[nm] for nm in small_names], LANES)
             for src in (dict((nm, p[nm]) for nm in small_names), own,
                         dict((nm, p['m_' + nm]) for nm in small_names), dict((nm, p['v_' + nm]) for nm in small_names))]
    dlt, nm_, nv_, _ = adamw(packs[0], packs[1], packs[2], packs[3], "adamw_small")
    shapes = [p[nm].shape for nm in small_names]
    for dst, packed in ((out_d, dlt), (out_m, nm_), (out_v, nv_)):
        dst.update(zip(small_names, _unpack_rows(packed, shapes, LANES)))
    out_g.update(own)

    loss = lax.psum(loss_local, ("x", "y", "c"))
    return (loss, grad_x[None], *[out_g[nm] for nm in WEIGHTS], *[out_d[nm] for nm in WEIGHTS],
            *[out_m[nm] for nm in WEIGHTS], *[out_v[nm] for nm in WEIGHTS])
```

```python
import functools
import math

import jax
import jax.numpy as jnp
from jax import lax
from jax.experimental import pallas as pl
from jax.experimental.pallas import tpu as pltpu

F32 = jnp.float32
BF16 = jnp.bfloat16
MESH = pl.DeviceIdType.MESH
ANY = pl.BlockSpec(memory_space=pl.ANY)

DEPTH = 2
D_MODEL = 1024
D_GROUP = 256
N_IN = 6 * D_GROUP
D_FF = 2816
N_STATE = 1024
CONV_WIDTH = 31
LRU_CONV_WIDTH = 4
FFN_CONV_WIDTH = 3
LRU_C = 8.0
ALPHA = (2 * DEPTH) ** 0.25
LN_EPS = 1e-5
N_CHIPS = 4
MEM_ROWS = 256
LANES = 128
SUBLANES = 8
VMEM_LIMIT = 56 * 1024 * 1024

ADAM_LR, ADAM_B1, ADAM_B2, ADAM_EPS, ADAM_WD, ADAM_STEP = 0.001, 0.9, 0.999, 1e-08, 0.01, 10

WEIGHTS = ['ln_in_g', 'ln_in_b', 'w_in', 'b_in', 's5_lam_re', 's5_lam_im', 's5_log_dt', 's5_b_re', 's5_b_im',
           's5_c_re', 's5_c_im', 's5_d', 's5_w_glu', 's5_b_glu', 'cv_w', 'cv_b', 'cv_gn_g', 'cv_gn_b', 'cv_w_pw',
           'cv_b_pw', 'lru_conv_w', 'lru_conv_b', 'lru_w_r', 'lru_b_r', 'lru_w_i', 'lru_b_i', 'lru_lam',
           'attn_w_kv', 'w_out', 'b_out', 'ln1_g', 'ln1_b', 'ffn_w_up', 'ffn_conv_w', 'ffn_conv_b', 'ffn_w_down',
           'ln2_g', 'ln2_b']
BIG = ['w_in', 'attn_w_kv', 'w_out', 'ffn_w_up', 'ffn_w_down']
SMALL_SHARDED = {'s5_w_glu': 1, 'cv_w': 2, 'cv_w_pw': 1, 'lru_conv_w': 2, 'ffn_conv_w': 2}


def _cparams(n_axes):
    return pltpu.CompilerParams(dimension_semantics=("arbitrary",) * n_axes, vmem_limit_bytes=VMEM_LIMIT)


def _dot(a, b):
    return jnp.dot(a, b, preferred_element_type=F32)


def _dot_nt(a, b):
    return lax.dot_general(a, b, (((1,), (1,)), ((), ())), preferred_element_type=F32)


def _dot_tn(a, b):
    return lax.dot_general(a, b, (((0,), (0,)), ((), ())), preferred_element_type=F32)


def _bf(v):
    return v.astype(BF16)


def _colsum(v):
    return jnp.sum(v, axis=0, keepdims=True)


def _dot3(v, p):
    hi = _bf(v)
    r1 = v - hi.astype(F32)
    mid = _bf(r1)
    lo = _bf(r1 - mid.astype(F32))
    return _dot(hi, p) + _dot(mid, p) + _dot(lo, p)


_GELU_C = math.sqrt(2.0 / math.pi)


_GELU_C3 = _GELU_C * 0.044715


def _gelu_parts(v):
    t = jnp.tanh(v * (_GELU_C + _GELU_C3 * (v * v)))
    hv = 0.5 * v
    return hv + hv * t, t


def _gelu(v):
    return _gelu_parts(v)[0]


def _gelu_grad(v, t):
    return (0.5 + 0.5 * t) + (0.5 * v) * (1.0 - t * t) * (_GELU_C + (3.0 * _GELU_C3) * (v * v))


def _sigmoid(v):
    return 1.0 / (1.0 + jnp.exp(-v))


def _acc(ref, val, first):
    @pl.when(first)
    def _():
        ref[...] = val

    @pl.when(jnp.logical_not(first))
    def _():
        ref[...] += val


def _rows(shape):
    return lax.broadcasted_iota(jnp.int32, shape, 0)


def _ln_rows(r):
    mu = jnp.mean(r, -1, keepdims=True)
    rc = r - mu
    var = jnp.mean(rc * rc, -1, keepdims=True)
    rs = lax.rsqrt(var + LN_EPS)
    return rc * rs, rs


def ln_fwd(x, g, b, name):
    s = x.shape[0]
    tm = min(512, s)

    def body(x_ref, g_ref, b_ref, xh_ref, rs_ref, xb_ref):
        xh, rs = _ln_rows(x_ref[...])
        xh_ref[...] = xh
        rs_ref[...] = rs
        xb_ref[...] = _bf(xh * g_ref[...] + b_ref[...])

    row = pl.BlockSpec((tm, D_MODEL), lambda i: (i, 0))
    vec = pl.BlockSpec((1, D_MODEL), lambda i: (0, 0))
    return pl.pallas_call(
        body, name=name, grid=(s // tm,),
        in_specs=[row, vec, vec],
        out_specs=[row, pl.BlockSpec((tm, 1), lambda i: (i, 0)), row],
        out_shape=[jax.ShapeDtypeStruct((s, D_MODEL), F32), jax.ShapeDtypeStruct((s, 1), F32),
                   jax.ShapeDtypeStruct((s, D_MODEL), BF16)],
        compiler_params=_cparams(1),
    )(x, g, b)


def proj_ln(a, w, layer, bias, xh_prev, g_prev, b_prev, g, b, name):
    s, k = a.shape
    tm = min(512, s)

    def body(a_ref, w_ref, bias_ref, xp_ref, gp_ref, bp_ref, g_ref, b_ref, xh_ref, rs_ref, xb_ref):
        acc = _dot(a_ref[...], w_ref[...]) + bias_ref[...]
        r = ALPHA * (xp_ref[...] * gp_ref[...] + bp_ref[...]) + acc
        xh, rs = _ln_rows(r)
        xh_ref[...] = xh
        rs_ref[...] = rs
        xb_ref[...] = _bf(xh * g_ref[...] + b_ref[...])

    row = pl.BlockSpec((tm, D_MODEL), lambda i: (i, 0))
    vec = pl.BlockSpec((1, D_MODEL), lambda i: (0, 0))
    return pl.pallas_call(
        body, name=name, grid=(s // tm,),
        in_specs=[pl.BlockSpec((tm, k), lambda i: (i, 0)),
                  pl.BlockSpec((None, k, D_MODEL), lambda i: (layer, 0, 0)),
                  vec, row, vec, vec, vec, vec],
        out_specs=[row, pl.BlockSpec((tm, 1), lambda i: (i, 0)), row],
        out_shape=[jax.ShapeDtypeStruct((s, D_MODEL), F32), jax.ShapeDtypeStruct((s, 1), F32),
                   jax.ShapeDtypeStruct((s, D_MODEL), BF16)],
        compiler_params=_cparams(1),
    )(a, w, bias, xh_prev, g_prev, b_prev, g, b)


def loss_ln_bwd(target, xh, rs, g, b, name):
    s = xh.shape[0]
    tm = min(512, s)

    def body(t_ref, xh_ref, rs_ref, g_ref, b_ref, dr_ref, dg_ref, db_ref, sq_ref):
        first = pl.program_id(0) == 0
        xhv = xh_ref[...]
        err = xhv * g_ref[...] + b_ref[...] - t_ref[...]
        dyv = err * (1.0 / D_MODEL)
        dxh = dyv * g_ref[...]
        dr = rs_ref[...] * (dxh - jnp.mean(dxh, -1, keepdims=True) - xhv * jnp.mean(dxh * xhv, -1, keepdims=True))
        dr_ref[...] = dr
        _acc(dg_ref, _colsum(dyv * xhv), first)
        _acc(db_ref, _colsum(dyv), first)
        _acc(sq_ref, _colsum(err * err), first)

    row = pl.BlockSpec((tm, D_MODEL), lambda i: (i, 0))
    vec = pl.BlockSpec((1, D_MODEL), lambda i: (0, 0))
    vshape = jax.ShapeDtypeStruct((1, D_MODEL), F32)
    return pl.pallas_call(
        body, name=name, grid=(s // tm,),
        in_specs=[row, row, pl.BlockSpec((tm, 1), lambda i: (i, 0)), vec, vec],
        out_specs=[row, vec, vec, vec],
        out_shape=[jax.ShapeDtypeStruct((s, D_MODEL), F32), vshape, vshape, vshape],
        compiler_params=_cparams(1),
    )(target, xh, rs, g, b)


def mm_nn(a, w, layer, bias, out_dtype, tm, name):
    m, k = a.shape
    _, nj, _, n = w.shape
    tm = min(tm, m)

    def body(a_ref, w_ref, b_ref, o_ref):
        o_ref[...] = (_dot(_bf(a_ref[...]), w_ref[...]) + b_ref[...]).astype(out_dtype)

    return pl.pallas_call(
        body, name=name, grid=(nj, m // tm),
        in_specs=[pl.BlockSpec((tm, k), lambda j, i: (i, 0)),
                  pl.BlockSpec((None, None, k, n), lambda j, i: (layer, j, 0, 0)),
                  pl.BlockSpec((1, n), lambda j, i: (0, j))],
        out_specs=pl.BlockSpec((tm, n), lambda j, i: (i, j)),
        out_shape=jax.ShapeDtypeStruct((m, nj * n), out_dtype),
        compiler_params=_cparams(2),
    )(a, w, bias)


def mm_nt(a, a_block, a_map, pick, w, layer, add, out_dtype, tm, m, name, ln=None):
    _, nj, r, n = w.shape
    tm = min(tm, m)
    has_add = add is not None
    n_in = 2 + has_add + (3 if ln is not None else 0)

    def body(*refs):
        a_ref, w_ref = refs[0], refs[1]
        res = _dot_nt(_bf(pick(a_ref, 0)), w_ref[0])
        for j in range(1, nj):
            res = res + _dot_nt(_bf(pick(a_ref, j)), w_ref[j])
        if has_add:
            res = res + ALPHA * refs[2][...]
        if ln is None:
            refs[n_in][...] = res.astype(out_dtype)
            return
        xh_ref, rs_ref, g_ref = refs[n_in - 3:n_in]
        dr_ref, dg_ref, db_ref, cs_ref = refs[n_in:]
        first = pl.program_id(0) == 0
        xhv = xh_ref[...]
        dxh = res * g_ref[...]
        dr = rs_ref[...] * (dxh - jnp.mean(dxh, -1, keepdims=True) - xhv * jnp.mean(dxh * xhv, -1, keepdims=True))
        dr_ref[...] = dr
        _acc(dg_ref, _colsum(res * xhv), first)
        _acc(db_ref, _colsum(res), first)
        _acc(cs_ref, _colsum(dr), first)

    row = pl.BlockSpec((tm, r), lambda i: (i, 0))
    in_specs = [pl.BlockSpec(a_block, a_map),
                pl.BlockSpec((None, nj, r, n), lambda i: (layer, 0, 0, 0))]
    ops = [a, w]
    if has_add:
        in_specs.append(row)
        ops.append(add)
    if ln is None:
        out_specs, out_shape = row, jax.ShapeDtypeStruct((m, r), out_dtype)
    else:
        vec = pl.BlockSpec((1, r), lambda i: (0, 0))
        vshape = jax.ShapeDtypeStruct((1, r), F32)
        in_specs += [row, pl.BlockSpec((tm, 1), lambda i: (i, 0)), vec]
        ops += list(ln)
        out_specs, out_shape = [row, vec, vec, vec], [jax.ShapeDtypeStruct((m, r), F32), vshape, vshape, vshape]
    return pl.pallas_call(
        body, name=name, grid=(m // tm,),
        in_specs=in_specs, out_specs=out_specs, out_shape=out_shape,
        compiler_params=_cparams(1),
    )(*ops)


def mm_tn(a, b, b_block, b_map, nj, n, tk, ts, s, name):
    kx = a.shape[1]
    ts = min(ts, s)

    def body(a_ref, b_ref, o_ref):
        part = _dot_tn(_bf(a_ref[...]), _bf(b_ref[...]))
        _acc(o_ref, part, pl.program_id(2) == 0)

    return pl.pallas_call(
        body, name=name, grid=(kx // tk, nj, s // ts),
        in_specs=[pl.BlockSpec((ts, tk), lambda kt, j, st: (st, kt)), pl.BlockSpec(b_block, b_map)],
        out_specs=pl.BlockSpec((None, tk, n), lambda kt, j, st: (j, kt, 0)),
        out_shape=jax.ShapeDtypeStruct((nj, kx, n), F32),
        compiler_params=_cparams(3),
    )(a, b)


S5_TAB_ROWS = 8 * SUBLANES


def _s5_scan_table(tab_ref, ar, ai, reverse):
    n = N_STATE
    row = _rows((SUBLANES, n))
    edge = SUBLANES - 1 if reverse else 0
    tab_ref[0:8, :] = jnp.where(row == edge, ar, 0.0)
    tab_ref[8:16, :] = jnp.where(row == edge, ai, 0.0)
    pr, pi = ar, ai
    for step, k in enumerate((1, 2, 4)):
        mask = row < SUBLANES - k if reverse else row >= k
        tab_ref[16 + 16 * step:24 + 16 * step, :] = jnp.where(mask, pr, 0.0)
        tab_ref[24 + 16 * step:32 + 16 * step, :] = jnp.where(mask, pi, 0.0)
        pr, pi = pr * pr - pi * pi, 2.0 * pr * pi


def _s5_scan(src_ref, dst_ref, tab_ref, edge_ref, tb, reverse, per_tile=None):
    n = N_STATE
    ng = tb // SUBLANES
    nq = n // LANES
    link = SUBLANES - 1 if reverse else 1

    def tile(ii, carry):
        g = ng - 1 - ii if reverse else ii
        rows = pl.ds(pl.multiple_of(g * SUBLANES, SUBLANES), SUBLANES)
        out = []
        for q in range(nq):
            cre = slice(q * LANES, (q + 1) * LANES)
            cim = slice(n + q * LANES, n + (q + 1) * LANES)
            lr, li = src_ref[rows, cre], src_ref[rows, cim]
            tr, ti = pltpu.roll(carry[2 * q], link, 0), pltpu.roll(carry[2 * q + 1], link, 0)
            kr, ki = tab_ref[0:8, cre], tab_ref[8:16, cre]
            lr, li = lr + kr * tr - ki * ti, li + kr * ti + ki * tr
            for step, k in enumerate((1, 2, 4)):
                amt = SUBLANES - k if reverse else k
                kr, ki = tab_ref[16 + 16 * step:24 + 16 * step, cre], tab_ref[24 + 16 * step:32 + 16 * step, cre]
                sr, si = pltpu.roll(lr, amt, 0), pltpu.roll(li, amt, 0)
                lr, li = lr + kr * sr - ki * si, li + kr * si + ki * sr
            dst_ref[rows, cre] = lr
            dst_ref[rows, cim] = li
            if per_tile is not None:
                per_tile(g, q, (cre, cim), lr, li)
            out += [lr, li]
        return tuple(out)

    init = []
    for q in range(nq):
        init += [edge_ref[:, q * LANES:(q + 1) * LANES], edge_ref[:, n + q * LANES:n + (q + 1) * LANES]]
    fin = lax.fori_loop(0, ng, tile, tuple(init))
    for q in range(nq):
        edge_ref[:, q * LANES:(q + 1) * LANES] = fin[2 * q]
        edge_ref[:, n + q * LANES:n + (q + 1) * LANES] = fin[2 * q + 1]


def s5_fwd(h_in, a2, bexp, cexp, dskip, wglu, layer, bglu, tb, name):
    s = h_in.shape[0]
    n = N_STATE

    def body(u_ref, a_ref, b_ref, c_ref, d_ref, w_ref, bg_ref, y_ref, h_ref, y0_ref, edge, tab, bu_ref):
        @pl.when(pl.program_id(0) == 0)
        def _():
            edge[...] = jnp.zeros_like(edge)
            _s5_scan_table(tab, a_ref[0:1, 0:n], a_ref[0:1, n:2 * n], False)

        u = u_ref[...]
        bu_ref[...] = _dot(_bf(u), b_ref[...])
        _s5_scan(bu_ref, h_ref, tab, edge, tb, False)
        y0 = _dot(_bf(h_ref[:, 0:n]), c_ref[0:n, :]) + _dot(_bf(h_ref[:, n:2 * n]), c_ref[n:2 * n, :]) + d_ref[...] * u
        y0_ref[...] = y0
        yg = _gelu(y0)
        z = _dot(_bf(yg), w_ref[...]) + bg_ref[...]
        y_ref[...] = _bf(yg * _sigmoid(z))

    vec = pl.BlockSpec((1, D_GROUP), lambda i: (0, 0))
    return pl.pallas_call(
        body, name=name, grid=(s // tb,),
        in_specs=[pl.BlockSpec((tb, D_GROUP), lambda i: (i, 0)),
                  pl.BlockSpec((1, 2 * n), lambda i: (0, 0)),
                  pl.BlockSpec((D_GROUP, 2 * n), lambda i: (0, 0)),
                  pl.BlockSpec((2 * n, D_GROUP), lambda i: (0, 0)),
                  vec,
                  pl.BlockSpec((None, D_GROUP, D_GROUP), lambda i: (layer, 0, 0)),
                  vec],
        out_specs=[pl.BlockSpec((tb, D_GROUP), lambda i: (i, 0)),
                   pl.BlockSpec((tb, 2 * n), lambda i: (i, 0)),
                   pl.BlockSpec((tb, D_GROUP), lambda i: (i, 0))],
        out_shape=[jax.ShapeDtypeStruct((s, D_GROUP), BF16), jax.ShapeDtypeStruct((s, 2 * n), F32),
                   jax.ShapeDtypeStruct((s, D_GROUP), F32)],
        scratch_shapes=[pltpu.VMEM((SUBLANES, 2 * n), F32), pltpu.VMEM((S5_TAB_ROWS, n), F32),
                        pltpu.VMEM((tb, 2 * n), F32)],
        compiler_params=_cparams(1),
    )(h_in, a2, bexp, cexp, dskip, wglu, bglu)


def s5_bwd(dmix, h_in, y0, hst, a2, bexp, cexp, dskip, wglu, layer, bglu, tb, name):
    s = h_in.shape[0]
    n = N_STATE
    nb = s // tb
    halo = tb // 8

    def body(dy_ref, u_ref, y0_ref, h_ref, hp_ref, a_ref, b_ref, c_ref, d_ref, w_ref, bg_ref,
             du_ref, cs_ref, db_ref, dc_ref, dd_ref, dw_ref, dbg_ref, da_ref, edge, tab, g_ref, da_acc):
        i = pl.program_id(0)
        first = i == 0

        @pl.when(first)
        def _():
            edge[...] = jnp.zeros_like(edge)
            da_acc[...] = jnp.zeros_like(da_acc)
            _s5_scan_table(tab, a_ref[0:1, 0:n], -a_ref[0:1, n:2 * n], True)

        dy = dy_ref[...]
        u = u_ref[...]
        y0v = y0_ref[...]
        yg, t = _gelu_parts(y0v)
        z = _dot(_bf(yg), w_ref[...]) + bg_ref[...]
        sg = _sigmoid(z)
        dz = dy * yg * sg * (1.0 - sg)
        dyg = dy * sg + _dot_nt(_bf(dz), w_ref[...])
        _acc(dw_ref, _dot_tn(_bf(yg), _bf(dz)), first)
        _acc(dbg_ref, _colsum(dz), first)
        dy0 = dyg * _gelu_grad(y0v, t)
        _acc(dd_ref, _colsum(dy0 * u), first)
        dy0b = _bf(dy0)
        _acc(dc_ref.at[0:n, :], _dot_tn(_bf(h_ref[:, 0:n]), dy0b), first)
        _acc(dc_ref.at[n:2 * n, :], _dot_tn(_bf(h_ref[:, n:2 * n]), dy0b), first)
        g_ref[...] = _dot_nt(dy0b, c_ref[...])
        keep = jnp.where(i == nb - 1, 0.0, 1.0)
        row0 = _rows((SUBLANES, LANES)) == 0

        def grad_a(g, q, cols, gr, gi):
            cre, cim = cols
            rows = pl.ds(pl.multiple_of(g * SUBLANES, SUBLANES), SUBLANES)
            before = pl.ds(pl.multiple_of(jnp.maximum(g - 1, 0) * SUBLANES, SUBLANES), SUBLANES)
            pre = jnp.where(g == 0, hp_ref[:, cre] * keep, h_ref[before, cre])
            pim = jnp.where(g == 0, hp_ref[:, cim] * keep, h_ref[before, cim])
            pr = jnp.where(row0, pltpu.roll(pre, 1, 0), pltpu.roll(h_ref[rows, cre], 1, 0))
            pi = jnp.where(row0, pltpu.roll(pim, 1, 0), pltpu.roll(h_ref[rows, cim], 1, 0))
            da_acc[:, cre] += gr * pr + gi * pi
            da_acc[:, cim] += gi * pr - gr * pi

        _s5_scan(g_ref, g_ref, tab, edge, tb, True, grad_a)
        da_ref[...] = _colsum(da_acc[...])
        gr, gi = g_ref[:, 0:n], g_ref[:, n:2 * n]
        grb, gib = _bf(gr), _bf(gi)
        du = d_ref[...] * dy0 + _dot_nt(grb, b_ref[:, 0:n]) + _dot_nt(gib, b_ref[:, n:2 * n])
        ub = _bf(u)
        _acc(db_ref.at[:, 0:n], _dot_tn(ub, grb), first)
        _acc(db_ref.at[:, n:2 * n], _dot_tn(ub, gib), first)
        du_ref[...] = _bf(du)
        _acc(cs_ref, _colsum(du), first)

    rev = lambda i: (nb - 1 - i, 0)
    vec = pl.BlockSpec((1, D_GROUP), lambda i: (0, 0))
    vshape = jax.ShapeDtypeStruct((1, D_GROUP), F32)
    return pl.pallas_call(
        body, name=name, grid=(nb,),
        in_specs=[pl.BlockSpec((tb, D_GROUP), rev),
                  pl.BlockSpec((tb, D_GROUP), rev),
                  pl.BlockSpec((tb, D_GROUP), rev),
                  pl.BlockSpec((tb, 2 * n), rev),
                  pl.BlockSpec((8, 2 * n), lambda i: (jnp.maximum((nb - 1 - i) * halo - 1, 0), 0)),
                  pl.BlockSpec((1, 2 * n), lambda i: (0, 0)),
                  pl.BlockSpec((D_GROUP, 2 * n), lambda i: (0, 0)),
                  pl.BlockSpec((2 * n, D_GROUP), lambda i: (0, 0)),
                  vec,
                  pl.BlockSpec((None, D_GROUP, D_GROUP), lambda i: (layer, 0, 0)),
                  vec],
        out_specs=[pl.BlockSpec((tb, D_GROUP), rev), vec,
                   pl.BlockSpec((D_GROUP, 2 * n), lambda i: (0, 0)),
                   pl.BlockSpec((2 * n, D_GROUP), lambda i: (0, 0)),
                   vec,
                   pl.BlockSpec((D_GROUP, D_GROUP), lambda i: (0, 0)),
                   vec,
                   pl.BlockSpec((1, 2 * n), lambda i: (0, 0))],
        out_shape=[jax.ShapeDtypeStruct((s, D_GROUP), BF16), vshape,
                   jax.ShapeDtypeStruct((D_GROUP, 2 * n), F32), jax.ShapeDtypeStruct((2 * n, D_GROUP), F32),
                   vshape, jax.ShapeDtypeStruct((D_GROUP, D_GROUP), F32), vshape,
                   jax.ShapeDtypeStruct((1, 2 * n), F32)],
        scratch_shapes=[pltpu.VMEM((SUBLANES, 2 * n), F32), pltpu.VMEM((S5_TAB_ROWS, n), F32),
                        pltpu.VMEM((tb, 2 * n), F32), pltpu.VMEM((SUBLANES, 2 * n), F32)],
        compiler_params=_cparams(1),
    )(dmix, h_in, y0, hst, hst, a2, bexp, cexp, dskip, wglu, bglu)


CV_HALO = 32


def _gn_stats(hc, pmat):
    mu = _dot3(hc, pmat)
    xc = hc - mu
    var = _dot3(xc * xc, pmat)
    rstd = lax.rsqrt(var + LN_EPS)
    return xc * rstd, rstd


def cv_fwd(h_in, cw, cb, gg, gb, pmat, wpw, layer, bpw, tb, name):
    s = h_in.shape[0]
    hl = CV_HALO

    def body(v_ref, g_ref, cw_ref, cb_ref, gg_ref, gb_ref, p_ref, w_ref, bw_ref, y_ref, hc_ref, ext):
        @pl.when(pl.program_id(0) == 0)
        def _():
            ext[0:hl, :] = jnp.zeros((hl, D_GROUP), F32)

        ext[hl:hl + tb, :] = v_ref[...] * _sigmoid(g_ref[...])
        acc = jnp.zeros((tb, D_GROUP), F32) + cb_ref[...]
        for k in range(CONV_WIDTH):
            off = hl - (CONV_WIDTH - 1) + k
            acc = acc + cw_ref[k:k + 1, :] * ext[off:off + tb, :]
        hc_ref[...] = acc
        ext[0:hl, :] = ext[tb:tb + hl, :]
        xn, _ = _gn_stats(acc, p_ref[...])
        hn = xn * gg_ref[...] + gb_ref[...]
        hs = hn * _sigmoid(hn)
        y_ref[...] = _bf(_dot(_bf(hs), w_ref[...]) + bw_ref[...])

    vec = pl.BlockSpec((1, D_GROUP), lambda i: (0, 0))
    sq = pl.BlockSpec((D_GROUP, D_GROUP), lambda i: (0, 0))
    return pl.pallas_call(
        body, name=name, grid=(s // tb,),
        in_specs=[pl.BlockSpec((tb, D_GROUP), lambda i: (i, 1)),
                  pl.BlockSpec((tb, D_GROUP), lambda i: (i, 2)),
                  pl.BlockSpec((hl, D_GROUP), lambda i: (0, 0)),
                  vec, vec, vec, sq,
                  pl.BlockSpec((None, D_GROUP, D_GROUP), lambda i: (layer, 0, 0)),
                  vec],
        out_specs=[pl.BlockSpec((tb, D_GROUP), lambda i: (i, 0)), pl.BlockSpec((tb, D_GROUP), lambda i: (i, 0))],
        out_shape=[jax.ShapeDtypeStruct((s, D_GROUP), BF16), jax.ShapeDtypeStruct((s, D_GROUP), F32)],
        scratch_shapes=[pltpu.VMEM((hl + tb, D_GROUP), F32)],
        compiler_params=_cparams(1),
    )(h_in, h_in, cw, cb, gg, gb, pmat, wpw, bpw)


def cv_bwd(dmix, h_in, hc, cw, gg, gb, pmat, wpw, layer, tb, name):
    s = h_in.shape[0]
    hl = CV_HALO
    nb = s // tb
    per = tb // hl

    def body(dy_ref, v_ref, g_ref, vh_ref, gh_ref, hc_ref, cw_ref, gg_ref, gb_ref, p_ref, w_ref,
             dvg_ref, cs_ref, dcw_ref, dcb_ref, dgg_ref, dgb_ref, dw_ref, dbw_ref, ext, dext, head):
        i = pl.program_id(0)
        first = i == 0

        @pl.when(first)
        def _():
            head[...] = jnp.zeros_like(head)

        dy = dy_ref[...]
        pm = p_ref[...]
        xn, rstd = _gn_stats(hc_ref[...], pm)
        hn = xn * gg_ref[...] + gb_ref[...]
        sg = _sigmoid(hn)
        hs = hn * sg
        dyb = _bf(dy)
        _acc(dbw_ref, _colsum(dy), first)
        _acc(dw_ref, _dot_tn(_bf(hs), dyb), first)
        dhs = _dot_nt(dyb, w_ref[...])
        dhn = dhs * sg * (1.0 + hn * (1.0 - sg))
        _acc(dgg_ref, _colsum(dhn * xn), first)
        _acc(dgb_ref, _colsum(dhn), first)
        dxn = dhn * gg_ref[...]
        dhc = rstd * (dxn - _dot3(dxn, pm) - xn * _dot3(dxn * xn, pm))
        _acc(dcb_ref, _colsum(dhc), first)
        v = v_ref[...]
        sgg = _sigmoid(g_ref[...])
        keep = jnp.where(i == nb - 1, 0.0, 1.0)
        ext[0:hl, :] = vh_ref[...] * _sigmoid(gh_ref[...]) * keep
        ext[hl:hl + tb, :] = v * sgg
        dext[0:tb, :] = dhc
        dext[tb:tb + hl, :] = head[...]
        head[...] = dhc[0:hl]
        dhg = jnp.zeros((tb, D_GROUP), F32)
        for k in range(CONV_WIDTH):
            off = hl - (CONV_WIDTH - 1) + k
            wk = _colsum(dhc * ext[off:off + tb, :])
            _acc(dcw_ref.at[k:k + 1, :], wk, first)
            back = CONV_WIDTH - 1 - k
            dhg = dhg + cw_ref[k:k + 1, :] * dext[back:back + tb, :]

        @pl.when(first)
        def _():
            dcw_ref[CONV_WIDTH:hl, :] = jnp.zeros((hl - CONV_WIDTH, D_GROUP), F32)

        dv = dhg * sgg
        dg = dhg * v * sgg * (1.0 - sgg)
        dvg_ref[:, 0:D_GROUP] = _bf(dv)
        dvg_ref[:, D_GROUP:2 * D_GROUP] = _bf(dg)
        _acc(cs_ref.at[:, 0:D_GROUP], _colsum(dv), first)
        _acc(cs_ref.at[:, D_GROUP:2 * D_GROUP], _colsum(dg), first)

    vec = pl.BlockSpec((1, D_GROUP), lambda i: (0, 0))
    sq = pl.BlockSpec((D_GROUP, D_GROUP), lambda i: (0, 0))
    tap = pl.BlockSpec((hl, D_GROUP), lambda i: (0, 0))
    vshape = jax.ShapeDtypeStruct((1, D_GROUP), F32)

    def blk(col):
        return pl.BlockSpec((tb, D_GROUP), lambda i: (nb - 1 - i, col))

    def halo_blk(col):
        return pl.BlockSpec((hl, D_GROUP), lambda i: (jnp.maximum((nb - 1 - i) * per - 1, 0), col))

    return pl.pallas_call(
        body, name=name, grid=(nb,),
        in_specs=[blk(1), blk(1), blk(2), halo_blk(1), halo_blk(2),
                  pl.BlockSpec((tb, D_GROUP), lambda i: (nb - 1 - i, 0)),
                  tap, vec, vec, sq,
                  pl.BlockSpec((None, D_GROUP, D_GROUP), lambda i: (layer, 0, 0))],
        out_specs=[pl.BlockSpec((tb, 2 * D_GROUP), lambda i: (nb - 1 - i, 0)),
                   pl.BlockSpec((1, 2 * D_GROUP), lambda i: (0, 0)),
                   tap, vec, vec, vec, sq, vec],
        out_shape=[jax.ShapeDtypeStruct((s, 2 * D_GROUP), BF16), jax.ShapeDtypeStruct((1, 2 * D_GROUP), F32),
                   jax.ShapeDtypeStruct((hl, D_GROUP), F32), vshape, vshape, vshape,
                   jax.ShapeDtypeStruct((D_GROUP, D_GROUP), F32), vshape],
        scratch_shapes=[pltpu.VMEM((hl + tb, D_GROUP), F32), pltpu.VMEM((tb + hl, D_GROUP), F32),
                        pltpu.VMEM((hl, D_GROUP), F32)],
        compiler_params=_cparams(1),
    )(dmix, h_in, h_in, h_in, h_in, hc, cw, gg, gb, pmat, wpw)


LRU_HALO = 8


def _lru_gates(xc, wr_ref, br_ref, wi_ref, bi_ref, sp_ref):
    xcb = _bf(xc)
    r = _sigmoid(_dot(xcb, wr_ref[...]) + br_ref[...])
    gi = _sigmoid(_dot(xcb, wi_ref[...]) + bi_ref[...])
    la = -LRU_C * r * sp_ref[...]
    a = jnp.exp(la)
    e2 = a * a
    sq = jnp.sqrt(-jnp.tanh(la) * (e2 + 1.0))
    return r, gi, a, e2, sq


def _rscan(a, b, tb, reverse):
    row = _rows(a.shape)
    sh = 1
    while sh < tb:
        if reverse:
            amt, mask = tb - sh, row < tb - sh
        else:
            amt, mask = sh, row >= sh
        a_s = jnp.where(mask, pltpu.roll(a, amt, 0), 1.0)
        b_s = jnp.where(mask, pltpu.roll(b, amt, 0), 0.0)
        b = b + a * b_s
        a = a * a_s
        sh *= 2
    return a, b


def lru_fwd(h_in, cw, cb, wr, br, wi, bi, sp, tb, name):
    s = h_in.shape[0]
    hl = LRU_HALO

    def body(xg_ref, xr_ref, cw_ref, cb_ref, wr_ref, br_ref, wi_ref, bi_ref, sp_ref, y_ref, xc_ref, h_ref, ext, carry):
        @pl.when(pl.program_id(0) == 0)
        def _():
            ext[0:hl, :] = jnp.zeros((hl, D_GROUP), F32)
            carry[...] = jnp.zeros_like(carry)

        ext[hl:hl + tb, :] = xr_ref[...]
        xc = jnp.zeros((tb, D_GROUP), F32) + cb_ref[...]
        for k in range(LRU_CONV_WIDTH):
            off = hl - (LRU_CONV_WIDTH - 1) + k
            xc = xc + cw_ref[k:k + 1, :] * ext[off:off + tb, :]
        xc_ref[...] = xc
        ext[0:hl, :] = ext[tb:tb + hl, :]
        r, gi, a, e2, sq = _lru_gates(xc, wr_ref, br_ref, wi_ref, bi_ref, sp_ref)
        pa, hloc = _rscan(a, sq * (gi * xc), tb, False)
        h = hloc + pa * carry[7:8, :]
        h_ref[...] = h
        carry[...] = h[tb - 8:tb]
        y_ref[...] = _bf(h * _gelu(xg_ref[...]))

    vec = pl.BlockSpec((1, D_GROUP), lambda i: (0, 0))
    sq_spec = pl.BlockSpec((D_GROUP, D_GROUP), lambda i: (0, 0))
    blk = pl.BlockSpec((tb, D_GROUP), lambda i: (i, 0))
    return pl.pallas_call(
        body, name=name, grid=(s // tb,),
        in_specs=[pl.BlockSpec((tb, D_GROUP), lambda i: (i, 3)),
                  pl.BlockSpec((tb, D_GROUP), lambda i: (i, 4)),
                  pl.BlockSpec((hl, D_GROUP), lambda i: (0, 0)),
                  vec, sq_spec, vec, sq_spec, vec, vec],
        out_specs=[blk, blk, blk],
        out_shape=[jax.ShapeDtypeStruct((s, D_GROUP), BF16), jax.ShapeDtypeStruct((s, D_GROUP), F32),
                   jax.ShapeDtypeStruct((s, D_GROUP), F32)],
        scratch_shapes=[pltpu.VMEM((hl + tb, D_GROUP), F32), pltpu.VMEM((8, D_GROUP), F32)],
        compiler_params=_cparams(1),
    )(h_in, h_in, cw, cb, wr, br, wi, bi, sp)


def lru_bwd(dmix, h_in, xcs, hs, cw, wr, br, wi, bi, sp, tb, name):
    s = h_in.shape[0]
    hl = LRU_HALO
    nb = s // tb
    per = tb // hl

    def body(dy_ref, xg_ref, xr_ref, xrh_ref, xc_ref, h_ref, hp_ref, cw_ref, wr_ref, br_ref, wi_ref, bi_ref, sp_ref,
             dx_ref, cs_ref, dcw_ref, dcb_ref, dwr_ref, dbr_ref, dwi_ref, dbi_ref, dsp_ref,
             ext, dext, head, anext, gnext):
        i = pl.program_id(0)
        first = i == 0

        @pl.when(first)
        def _():
            head[...] = jnp.zeros_like(head)
            anext[...] = jnp.zeros_like(anext)
            gnext[...] = jnp.zeros_like(gnext)

        dy = dy_ref[...]
        xg = xg_ref[...]
        xc = xc_ref[...]
        h = h_ref[...]
        r, gi, a, e2, sq = _lru_gates(xc, wr_ref, br_ref, wi_ref, bi_ref, sp_ref)
        gate, t = _gelu_parts(xg)
        dh = dy * gate
        dxg = dy * h * _gelu_grad(xg, t)
        row = _rows((tb, D_GROUP))
        coef = jnp.where(row == tb - 1, anext[0:1, :], pltpu.roll(a, tb - 1, 0))
        pc, gloc = _rscan(coef, dh, tb, True)
        gfull = gloc + pc * gnext[0:1, :]
        anext[...] = a[0:8]
        gnext[...] = gfull[0:8]
        keep = jnp.where(i == nb - 1, 0.0, 1.0)
        hprev = jnp.where(row == 0, hp_ref[7:8, :] * keep, pltpu.roll(h, 1, 0))
        da = gfull * hprev
        uu = gi * xc
        dsq = gfull * uu
        duu = gfull * sq
        dla = da * a - dsq * e2 / sq
        sp = sp_ref[...]
        dr = dla * (-LRU_C) * sp
        _acc(dsp_ref, _colsum(dla * (-LRU_C) * r), first)
        dzr = dr * r * (1.0 - r)
        dzi = duu * xc * gi * (1.0 - gi)
        dzrb, dzib = _bf(dzr), _bf(dzi)
        dxc = duu * gi + _dot_nt(dzrb, wr_ref[...]) + _dot_nt(dzib, wi_ref[...])
        xcb = _bf(xc)
        _acc(dwr_ref, _dot_tn(xcb, dzrb), first)
        _acc(dwi_ref, _dot_tn(xcb, dzib), first)
        _acc(dbr_ref, _colsum(dzr), first)
        _acc(dbi_ref, _colsum(dzi), first)
        _acc(dcb_ref, _colsum(dxc), first)
        ext[0:hl, :] = xrh_ref[...] * keep
        ext[hl:hl + tb, :] = xr_ref[...]
        dext[0:tb, :] = dxc
        dext[tb:tb + hl, :] = head[...]
        head[...] = dxc[0:hl]
        dxr = jnp.zeros((tb, D_GROUP), F32)
        for k in range(LRU_CONV_WIDTH):
            off = hl - (LRU_CONV_WIDTH - 1) + k
            _acc(dcw_ref.at[k:k + 1, :], _colsum(dxc * ext[off:off + tb, :]), first)
            back = LRU_CONV_WIDTH - 1 - k
            dxr = dxr + cw_ref[k:k + 1, :] * dext[back:back + tb, :]

        @pl.when(first)
        def _():
            dcw_ref[LRU_CONV_WIDTH:hl, :] = jnp.zeros((hl - LRU_CONV_WIDTH, D_GROUP), F32)

        dx_ref[:, 0:D_GROUP] = _bf(dxg)
        dx_ref[:, D_GROUP:2 * D_GROUP] = _bf(dxr)
        _acc(cs_ref.at[:, 0:D_GROUP], _colsum(dxg), first)
        _acc(cs_ref.at[:, D_GROUP:2 * D_GROUP], _colsum(dxr), first)

    vec = pl.BlockSpec((1, D_GROUP), lambda i: (0, 0))
    sq_spec = pl.BlockSpec((D_GROUP, D_GROUP), lambda i: (0, 0))
    tap = pl.BlockSpec((hl, D_GROUP), lambda i: (0, 0))
    vshape = jax.ShapeDtypeStruct((1, D_GROUP), F32)
    sshape = jax.ShapeDtypeStruct((D_GROUP, D_GROUP), F32)

    def blk(col):
        return pl.BlockSpec((tb, D_GROUP), lambda i: (nb - 1 - i, col))

    def halo_blk(col):
        return pl.BlockSpec((hl, D_GROUP), lambda i: (jnp.maximum((nb - 1 - i) * per - 1, 0), col))

    return pl.pallas_call(
        body, name=name, grid=(nb,),
        in_specs=[blk(2), blk(3), blk(4), halo_blk(4), blk(0), blk(0), halo_blk(0),
                  tap, sq_spec, vec, sq_spec, vec, vec],
        out_specs=[pl.BlockSpec((tb, 2 * D_GROUP), lambda i: (nb - 1 - i, 0)),
                   pl.BlockSpec((1, 2 * D_GROUP), lambda i: (0, 0)),
                   tap, vec, sq_spec, vec, sq_spec, vec, vec],
        out_shape=[jax.ShapeDtypeStruct((s, 2 * D_GROUP), BF16), jax.ShapeDtypeStruct((1, 2 * D_GROUP), F32),
                   jax.ShapeDtypeStruct((hl, D_GROUP), F32), vshape, sshape, vshape, sshape, vshape, vshape],
        scratch_shapes=[pltpu.VMEM((hl + tb, D_GROUP), F32), pltpu.VMEM((tb + hl, D_GROUP), F32),
                        pltpu.VMEM((hl, D_GROUP), F32), pltpu.VMEM((8, D_GROUP), F32), pltpu.VMEM((8, D_GROUP), F32)],
        compiler_params=_cparams(1),
    )(dmix, h_in, h_in, h_in, xcs, hs, hs, cw, wr, br, wi, bi, sp)


ATTN_HEADS = 4
ATTN_HEAD_DIM = 64
ATTN_SCALE = ATTN_HEAD_DIM ** -0.5


def _head_mask(h):
    lane = lax.broadcasted_iota(jnp.int32, (1, D_GROUP), 1)
    return jnp.where((lane >= h * ATTN_HEAD_DIM) & (lane < (h + 1) * ATTN_HEAD_DIM), 1.0, 0.0)


def _softmax_rows(sc):
    e = jnp.exp(sc - jnp.max(sc, -1, keepdims=True))
    return e / jnp.sum(e, -1, keepdims=True)


def attn_fwd(h_in, kv, tb, name):
    s = h_in.shape[0]

    def body(q_ref, kv_ref, y_ref):
        q = q_ref[...]
        kb = _bf(kv_ref[:, 0:D_GROUP])
        vb = _bf(kv_ref[:, D_GROUP:2 * D_GROUP])
        out = jnp.zeros((tb, D_GROUP), F32)
        for h in range(ATTN_HEADS):
            mask = _head_mask(h)
            p = _softmax_rows(_dot_nt(_bf(q * mask), kb) * ATTN_SCALE)
            out = out + _dot(_bf(p), vb) * mask
        y_ref[...] = _bf(out)

    return pl.pallas_call(
        body, name=name, grid=(s // tb,),
        in_specs=[pl.BlockSpec((tb, D_GROUP), lambda i: (i, 5)),
                  pl.BlockSpec((D_GROUP, 2 * D_GROUP), lambda i: (0, 0))],
        out_specs=pl.BlockSpec((tb, D_GROUP), lambda i: (i, 0)),
        out_shape=jax.ShapeDtypeStruct((s, D_GROUP), BF16),
        compiler_params=_cparams(1),
    )(h_in, kv)


def attn_bwd(dmix, h_in, kv, tb, name):
    s = h_in.shape[0]

    def body(do_ref, q_ref, kv_ref, dq_ref, cs_ref, dkv_ref):
        first = pl.program_id(0) == 0
        q = q_ref[...]
        do = do_ref[...]
        kb = _bf(kv_ref[:, 0:D_GROUP])
        vb = _bf(kv_ref[:, D_GROUP:2 * D_GROUP])
        dq = jnp.zeros((tb, D_GROUP), F32)
        dk = jnp.zeros((D_GROUP, D_GROUP), F32)
        dv = jnp.zeros((D_GROUP, D_GROUP), F32)
        for h in range(ATTN_HEADS):
            mask = _head_mask(h)
            qm = _bf(q * mask)
            p = _softmax_rows(_dot_nt(qm, kb) * ATTN_SCALE)
            dom = _bf(do * mask)
            dp = _dot_nt(dom, vb)
            dv = dv + _dot_tn(_bf(p), dom)
            ds = _bf(p * (dp - jnp.sum(dp * p, -1, keepdims=True)) * ATTN_SCALE)
            dq = dq + _dot(ds, kb) * mask
            dk = dk + _dot_tn(ds, qm)
        dq_ref[...] = _bf(dq)
        _acc(cs_ref, _colsum(dq), first)
        _acc(dkv_ref.at[:, 0:D_GROUP], dk, first)
        _acc(dkv_ref.at[:, D_GROUP:2 * D_GROUP], dv, first)

    return pl.pallas_call(
        body, name=name, grid=(s // tb,),
        in_specs=[pl.BlockSpec((tb, D_GROUP), lambda i: (i, 3)),
                  pl.BlockSpec((tb, D_GROUP), lambda i: (i, 5)),
                  pl.BlockSpec((D_GROUP, 2 * D_GROUP), lambda i: (0, 0))],
        out_specs=[pl.BlockSpec((tb, D_GROUP), lambda i: (i, 0)),
                   pl.BlockSpec((1, D_GROUP), lambda i: (0, 0)),
                   pl.BlockSpec((D_GROUP, 2 * D_GROUP), lambda i: (0, 0))],
        out_shape=[jax.ShapeDtypeStruct((s, D_GROUP), BF16), jax.ShapeDtypeStruct((1, D_GROUP), F32),
                   jax.ShapeDtypeStruct((D_GROUP, 2 * D_GROUP), F32)],
        compiler_params=_cparams(1),
    )(dmix, h_in, kv)


FFN_RB = 16
FFN_UNROLL_FWD = 4
FFN_UNROLL_BWD = 2
FFN_TAP_ROWS = 8
FFN_TN = D_FF // 2


def _shift_down(cur, tail, k):
    return pltpu.roll(jnp.concatenate([tail, cur], axis=0), k, 0)[SUBLANES:]


def _shift_up(cur, head, k):
    rb = cur.shape[0]
    return pltpu.roll(jnp.concatenate([cur, head], axis=0), rb + SUBLANES - k, 0)[:rb]


def _fold8(v):
    tot = v[0:SUBLANES]
    for t in range(1, v.shape[0] // SUBLANES):
        tot = tot + v[t * SUBLANES:(t + 1) * SUBLANES]
    return tot


def _strip(r):
    return pl.ds(pl.multiple_of(r * FFN_RB, FFN_RB), FFN_RB)


def ffn_act_fwd(u, cw, cb, tb, name, gather=()):
    s = u.shape[0]
    rb = FFN_RB
    nct = D_FF // FFN_TN
    nstrip = tb // rb
    ng = len(gather)

    def body(uv_ref, ug_ref, wv_ref, wg_ref, bv_ref, bg_ref, *rest):
        hf_ref, keep_ref = rest[ng], rest[ng + 1]
        slabs = rest[ng + 2:2 * ng + 2]
        tailv, tailg = rest[2 * ng + 2], rest[2 * ng + 3]
        sems = rest[2 * ng + 4:]
        if ng:
            @pl.when((pl.program_id(0) == 0) & (pl.program_id(1) == 0))
            def _():
                _gather_start(slabs, sems[0:2])

        @pl.when(pl.program_id(1) == 0)
        def _():
            tailv[...] = jnp.zeros_like(tailv)
            tailg[...] = jnp.zeros_like(tailg)

        for cc in range(FFN_TN // LANES):
            cols = slice(cc * LANES, (cc + 1) * LANES)
            wv = [wv_ref[k:k + 1, cols] for k in range(FFN_CONV_WIDTH)]
            wg = [wg_ref[k:k + 1, cols] for k in range(FFN_CONV_WIDTH)]
            bv, bg = bv_ref[:, cols], bg_ref[:, cols]

            def strip(r, carry):
                tail_v, tail_g = carry
                cur_v, cur_g = uv_ref[_strip(r), cols], ug_ref[_strip(r), cols]
                vc = wv[0] * _shift_down(cur_v, tail_v, 2) + wv[1] * _shift_down(cur_v, tail_v, 1) + wv[2] * cur_v + bv
                gc = wg[0] * _shift_down(cur_g, tail_g, 2) + wg[1] * _shift_down(cur_g, tail_g, 1) + wg[2] * cur_g + bg
                ge, t = _gelu_parts(gc)
                hf_ref[_strip(r), cols] = _bf(vc * ge)
                keep_ref[0, _strip(r), cols] = _bf(vc)
                keep_ref[1, _strip(r), cols] = _bf(ge)
                keep_ref[2, _strip(r), cols] = _bf(_gelu_grad(gc, t))
                return cur_v[rb - SUBLANES:], cur_g[rb - SUBLANES:]

            def strips(q, carry):
                for k in range(FFN_UNROLL_FWD):
                    carry = strip(q * FFN_UNROLL_FWD + k, carry)
                return carry

            last_v, last_g = lax.fori_loop(0, nstrip // FFN_UNROLL_FWD, strips, (tailv[:, cols], tailg[:, cols]))
            tailv[:, cols] = last_v
            tailg[:, cols] = last_g

        if ng:
            @pl.when((pl.program_id(0) == nct - 1) & (pl.program_id(1) == s // tb - 1))
            def _():
                _gather_finish(slabs, sems[0:2], sems[2:4])

    return pl.pallas_call(
        body, name=name, grid=(nct, s // tb),
        in_specs=[pl.BlockSpec((tb, FFN_TN), lambda c, i: (i, c)),
                  pl.BlockSpec((tb, FFN_TN), lambda c, i: (i, c + nct)),
                  pl.BlockSpec((FFN_TAP_ROWS, FFN_TN), lambda c, i: (0, c)),
                  pl.BlockSpec((FFN_TAP_ROWS, FFN_TN), lambda c, i: (0, c + nct)),
                  pl.BlockSpec((1, FFN_TN), lambda c, i: (0, c)),
                  pl.BlockSpec((1, FFN_TN), lambda c, i: (0, c + nct))] + [ANY] * ng,
        out_specs=[pl.BlockSpec((tb, FFN_TN), lambda c, i: (i, c)),
                   pl.BlockSpec((3, tb, FFN_TN), lambda c, i: (0, i, c))] + [ANY] * ng,
        out_shape=[jax.ShapeDtypeStruct((s, D_FF), BF16), jax.ShapeDtypeStruct((3, s, D_FF), BF16)]
        + [jax.ShapeDtypeStruct(a.shape, a.dtype) for a in gather],
        input_output_aliases={6 + k: 2 + k for k in range(ng)},
        scratch_shapes=[pltpu.VMEM((SUBLANES, FFN_TN), F32), pltpu.VMEM((SUBLANES, FFN_TN), F32)]
        + (_gather_sems(ng) if ng else []),
        compiler_params=_cparams(2),
    )(u, u, cw, cw, cb, cb, *gather)


def ffn_act_bwd(dhf, u, kept, cw, tb, name, scatter=()):
    s = u.shape[0]
    rb = FFN_RB
    nct = D_FF // FFN_TN
    nb = s // tb
    nstrip = tb // rb
    ntap = FFN_CONV_WIDTH
    ns = len(scatter)

    def body(dh_ref, uv_ref, ug_ref, kept_ref, wv_ref, wg_ref, *rest):
        s_refs = rest[:ns]
        du_ref, dwv_ref, dwg_ref = rest[ns:ns + 3]
        got_refs = rest[ns + 3:2 * ns + 3]
        headv, headg = rest[2 * ns + 3], rest[2 * ns + 4]
        sems = rest[2 * ns + 5:]
        i = pl.program_id(1)
        first = i == 0
        if ns:
            @pl.when((pl.program_id(0) == 0) & first)
            def _():
                _scatter_start(s_refs, got_refs, sems)

        @pl.when(first)
        def _():
            headv[...] = jnp.zeros_like(headv)
            headg[...] = jnp.zeros_like(headg)
            dwv_ref[...] = jnp.zeros_like(dwv_ref)
            dwg_ref[...] = jnp.zeros_like(dwg_ref)

        zero = jnp.zeros((SUBLANES, LANES), F32)
        for cc in range(FFN_TN // LANES):
            cols = slice(cc * LANES, (cc + 1) * LANES)
            wv = [wv_ref[k:k + 1, cols] for k in range(ntap)]
            wg = [wg_ref[k:k + 1, cols] for k in range(ntap)]

            def strip(ii, carry):
                head_dv, head_dg, acc_v, acc_g = carry
                r = nstrip - 1 - ii
                dh = dh_ref[_strip(r), cols]
                dvc = dh * kept_ref[1, _strip(r), cols].astype(F32)
                dgc = dh * kept_ref[0, _strip(r), cols].astype(F32) * kept_ref[2, _strip(r), cols].astype(F32)
                sdv = [_shift_up(dvc, head_dv, 2), _shift_up(dvc, head_dv, 1), dvc]
                sdg = [_shift_up(dgc, head_dg, 2), _shift_up(dgc, head_dg, 1), dgc]
                cur_v, cur_g = uv_ref[_strip(r), cols], ug_ref[_strip(r), cols]
                acc_v = tuple(acc_v[k] + _fold8(cur_v * sdv[k]) for k in range(ntap)) + (acc_v[ntap] + _fold8(dvc),)
                acc_g = tuple(acc_g[k] + _fold8(cur_g * sdg[k]) for k in range(ntap)) + (acc_g[ntap] + _fold8(dgc),)
                du_v = wv[0] * sdv[0] + wv[1] * sdv[1] + wv[2] * sdv[2]
                du_g = wg[0] * sdg[0] + wg[1] * sdg[1] + wg[2] * sdg[2]
                du_ref[0, _strip(r), cols] = _bf(du_v)
                du_ref[1, _strip(r), cols] = _bf(du_g)
                return dvc[0:SUBLANES], dgc[0:SUBLANES], acc_v, acc_g

            init = (headv[:, cols], headg[:, cols], (zero,) * (ntap + 1), (zero,) * (ntap + 1))
            def strips(q, carry):
                for k in range(FFN_UNROLL_BWD):
                    carry = strip(q * FFN_UNROLL_BWD + k, carry)
                return carry

            top_dv, top_dg, acc_v, acc_g = lax.fori_loop(0, nstrip // FFN_UNROLL_BWD, strips, init)
            headv[:, cols] = top_dv
            headg[:, cols] = top_dg
            for k in range(ntap + 1):
                dwv_ref[k:k + 1, cols] += _colsum(acc_v[k])
                dwg_ref[k:k + 1, cols] += _colsum(acc_g[k])

        if ns:
            @pl.when((pl.program_id(0) == nct - 1) & (i == nb - 1))
            def _():
                _scatter_finish(s_refs, got_refs, sems)

    def blk(shift):
        return pl.BlockSpec((tb, FFN_TN), lambda c, i: (nb - 1 - i, c + shift))

    tapv = pl.BlockSpec((FFN_TAP_ROWS, FFN_TN), lambda c, i: (0, c))
    tapg = pl.BlockSpec((FFN_TAP_ROWS, FFN_TN), lambda c, i: (0, c + nct))
    return pl.pallas_call(
        body, name=name, grid=(nct, nb),
        in_specs=[blk(0), blk(0), blk(nct), pl.BlockSpec((3, tb, FFN_TN), lambda c, i: (0, nb - 1 - i, c)), tapv, tapg]
        + [ANY] * ns,
        out_specs=[pl.BlockSpec((2, tb, FFN_TN), lambda c, i: (0, nb - 1 - i, c)), tapv, tapv] + [ANY] * ns,
        out_shape=[jax.ShapeDtypeStruct((2, s, D_FF), BF16), jax.ShapeDtypeStruct((FFN_TAP_ROWS, D_FF), F32),
                   jax.ShapeDtypeStruct((FFN_TAP_ROWS, D_FF), F32)] + _scatter_shapes(scatter),
        scratch_shapes=[pltpu.VMEM((SUBLANES, FFN_TN), F32), pltpu.VMEM((SUBLANES, FFN_TN), F32)]
        + (_scatter_sems(ns) if ns else []),
        compiler_params=_cparams(2),
    )(dhf, u, u, kept, cw, cw, *scatter)


def _place():
    x, y, c = lax.axis_index("x"), lax.axis_index("y"), lax.axis_index("c")
    return x, y, c, 2 * x + y


def _chip_peer(x, y, d):
    return jnp.bitwise_xor(x, d >> 1), jnp.bitwise_xor(y, d & 1)


def _my_half(ref_rows, c):
    half = ref_rows // 2
    return pl.ds(c * half, half)


def _gather_copy(ref, part, c, sems, k, d, to):
    rows = _my_half(ref.shape[1], c)
    return pltpu.make_async_remote_copy(src_ref=ref.at[part, rows], dst_ref=ref.at[part, rows], send_sem=sems[0].at[k, d - 1],
                                        recv_sem=sems[1].at[k, d - 1], device_id=to, device_id_type=MESH)


def _gather_start(slabs, ici_sems):
    x, y, c, j = _place()
    for k, ref in enumerate(slabs):
        for d in (1, 2, 3):
            px, py = _chip_peer(x, y, d)
            _gather_copy(ref, j, c, ici_sems, k, d, (px, py, c)).start()


def _gather_finish(slabs, ici_sems, d2d_sems):
    x, y, c, j = _place()
    sib = (x, y, 1 - c)
    passed = []
    for d in (1, 2, 3):
        jd = jnp.bitwise_xor(j, d)
        for k, ref in enumerate(slabs):
            _gather_copy(ref, jd, c, ici_sems, k, d, sib).wait_recv()
            cp = _gather_copy(ref, jd, c, d2d_sems, k, d, sib)
            cp.start()
            passed.append(cp)
    for cp in passed:
        cp.wait_recv()
        cp.wait_send()
    for k, ref in enumerate(slabs):
        for d in (1, 2, 3):
            _gather_copy(ref, j, c, ici_sems, k, d, sib).wait_send()


def _gather_sems(n):
    return [pltpu.SemaphoreType.DMA((n, 3)) for _ in range(4)]


def gather_weights(slabs):
    n = len(slabs)

    def body(*refs):
        outs = refs[n:2 * n]
        sems = refs[2 * n:]
        _gather_start(outs, sems[0:2])
        _gather_finish(outs, sems[0:2], sems[2:4])

    return pl.pallas_call(
        body, name="gather_weights", in_specs=[ANY] * n, out_specs=[ANY] * n,
        out_shape=[jax.ShapeDtypeStruct(a.shape, a.dtype) for a in slabs],
        input_output_aliases={w: w for w in range(n)}, scratch_shapes=_gather_sems(n),
    )(*slabs)


def _own_slab(part, jidx):
    slab = jnp.zeros((N_CHIPS,) + part.shape, part.dtype)
    return lax.dynamic_update_slice_in_dim(slab, part[None], jidx, axis=0)


def exchange_cores(gbig, name, small_all=None):
    n = len(gbig)
    with_small = small_all is not None

    def body(*refs):
        g_refs = refs[:n]
        got_refs = refs[n + with_small:2 * n + with_small]
        dsem, esem, ssem, rsem, fsem, hsem = refs[2 * (n + with_small):]
        x, y, c, j = _place()
        sib = (x, y, 1 - c)
        big = []
        for k in range(n):
            half = g_refs[k].shape[1] // 2
            cp = pltpu.make_async_remote_copy(
                src_ref=g_refs[k].at[:, pl.ds((1 - c) * half, half)], dst_ref=got_refs[k], send_sem=dsem.at[k],
                recv_sem=esem.at[k], device_id=sib, device_id_type=MESH)
            cp.start()
            big.append(cp)
        if with_small:
            all_ref = refs[2 * n + 1]
            me = 4 * x + 2 * y + c

            def small_copy(k, block, to, sems):
                return pltpu.make_async_remote_copy(
                    src_ref=all_ref.at[block], dst_ref=all_ref.at[block],
                    send_sem=sems[0].at[k], recv_sem=sems[1].at[k], device_id=to, device_id_type=MESH)

            first = [small_copy(0, me, sib, (ssem, rsem))]
            for d in (1, 2, 3):
                px, py = _chip_peer(x, y, d)
                first.append(small_copy(d, me, (px, py, c), (ssem, rsem)))
            for cp in first:
                cp.start()
            passed = []
            for d in (1, 2, 3):
                px, py = _chip_peer(x, y, d)
                src_block = 4 * px + 2 * py + c
                small_copy(d, src_block, sib, (ssem, rsem)).wait_recv()
                cp = small_copy(d - 1, src_block, sib, (fsem, hsem))
                cp.start()
                passed.append(cp)
            small_copy(0, me, sib, (ssem, rsem)).wait_recv()
            for cp in passed:
                cp.wait_recv()
            for cp in first + passed:
                cp.wait_send()
        for cp in big:
            cp.wait()

    ops = list(gbig) + ([small_all] if with_small else [])
    out_shape = [jax.ShapeDtypeStruct((g.shape[0], g.shape[1] // 2, g.shape[2]), F32) for g in gbig]
    aliases = {}
    if with_small:
        out_shape.append(jax.ShapeDtypeStruct(small_all.shape, F32))
        aliases = {n: n}
    return pl.pallas_call(
        body, name=name, in_specs=[ANY] * len(ops), out_specs=[ANY] * len(out_shape), out_shape=out_shape,
        input_output_aliases=aliases,
        scratch_shapes=[pltpu.SemaphoreType.DMA((n,)), pltpu.SemaphoreType.DMA((n,)),
                        pltpu.SemaphoreType.DMA((4,)), pltpu.SemaphoreType.DMA((4,)),
                        pltpu.SemaphoreType.DMA((3,)), pltpu.SemaphoreType.DMA((3,))],
    )(*ops)


def _scatter_copy(s_ref, got_ref, k, d, sems):
    x, y, c, j = _place()
    px, py = _chip_peer(x, y, d)
    return pltpu.make_async_remote_copy(
        src_ref=s_ref.at[jnp.bitwise_xor(j, d)], dst_ref=got_ref.at[d - 1], send_sem=sems[0].at[k, d - 1],
        recv_sem=sems[1].at[k, d - 1], device_id=(px, py, c), device_id_type=MESH)


def _scatter_start(s_refs, got_refs, sems):
    for d in (1, 2, 3):
        for k in range(len(s_refs)):
            _scatter_copy(s_refs[k], got_refs[k], k, d, sems).start()


def _scatter_finish(s_refs, got_refs, sems):
    for d in (1, 2, 3):
        for k in range(len(s_refs)):
            _scatter_copy(s_refs[k], got_refs[k], k, d, sems).wait()


def _scatter_sems(n):
    return [pltpu.SemaphoreType.DMA((n, 3)), pltpu.SemaphoreType.DMA((n, 3))]


def _scatter_shapes(s1):
    return [jax.ShapeDtypeStruct((3,) + a.shape[1:], a.dtype) for a in s1]


def scatter_shards(s1, name):
    n = len(s1)

    def body(*refs):
        _scatter_start(refs[:n], refs[n:2 * n], refs[2 * n:])
        _scatter_finish(refs[:n], refs[n:2 * n], refs[2 * n:])

    return pl.pallas_call(
        body, name=name, in_specs=[ANY] * n, out_specs=[ANY] * n, out_shape=_scatter_shapes(s1),
        scratch_shapes=_scatter_sems(n),
    )(*s1)


def share_with_sibling(parts):
    n = len(parts)

    def body(*refs):
        out_refs = refs[n:2 * n]
        ssem, rsem = refs[2 * n:]
        x, y, c, j = _place()
        cps = []
        for k in range(n):
            rows = _my_half(out_refs[k].shape[1], c)
            for l in range(DEPTH):
                cp = pltpu.make_async_remote_copy(
                    src_ref=out_refs[k].at[l, rows], dst_ref=out_refs[k].at[l, rows], send_sem=ssem.at[k, l],
                    recv_sem=rsem.at[k, l], device_id=(x, y, 1 - c), device_id_type=MESH)
                cp.start()
                cps.append(cp)
        for cp in cps:
            cp.wait()

    return pl.pallas_call(
        body, name="share_with_sibling", in_specs=[ANY] * n, out_specs=[ANY] * n,
        out_shape=[jax.ShapeDtypeStruct(a.shape, F32) for a in parts],
        input_output_aliases={k: k for k in range(n)},
        scratch_shapes=[pltpu.SemaphoreType.DMA((n, DEPTH)), pltpu.SemaphoreType.DMA((n, DEPTH))],
    )(*parts)


def add_core_halves(g, got, cidx, name):
    nch, r, cdim = g.shape
    half = r // 2
    tr = _row_tile(half, cdim, mult=16)
    per = half // tr

    def body(c_ref, a_ref, b_ref, o_ref):
        o_ref[...] = _bf(a_ref[...] + b_ref[...])

    grid_spec = pltpu.PrefetchScalarGridSpec(
        num_scalar_prefetch=1, grid=(nch, per),
        in_specs=[pl.BlockSpec((None, tr, cdim), lambda jj, i, c_ref: (jj, c_ref[0] * per + i, 0)),
                  pl.BlockSpec((None, tr, cdim), lambda jj, i, c_ref: (jj, i, 0))],
        out_specs=pl.BlockSpec((None, tr, cdim), lambda jj, i, c_ref: (jj, i, 0)))
    return pl.pallas_call(
        body, name=name, grid_spec=grid_spec,
        out_shape=jax.ShapeDtypeStruct((nch, half, cdim), BF16), compiler_params=_cparams(2),
    )(cidx, g, got)


def add_chip_parts(s1, got, jc, layer, into, name):
    _, half, cdim = s1.shape
    tr = _row_tile(half, cdim, mult=16)
    per = half // tr

    def body(jc_ref, a_ref, g0_ref, g1_ref, g2_ref, *rest):
        rest[-1][...] = ((a_ref[...].astype(F32) + g0_ref[...].astype(F32)) + g1_ref[...].astype(F32)) + g2_ref[...].astype(F32)

    def slot(k):
        return pl.BlockSpec((None, tr, cdim), lambda i, jc_ref: (k, i, 0))

    in_specs = [pl.BlockSpec((None, tr, cdim), lambda i, jc_ref: (jc_ref[0], i, 0)), slot(0), slot(1), slot(2)]
    ops = [jc, s1, got, got, got]
    aliases = {}
    if into is not None:
        in_specs.append(ANY)
        ops.append(into)
        aliases = {5: 0}
    grid_spec = pltpu.PrefetchScalarGridSpec(
        num_scalar_prefetch=1, grid=(per,), in_specs=in_specs,
        out_specs=pl.BlockSpec((None, tr, cdim), lambda i, jc_ref: (layer, jc_ref[1] * per + i, 0)))
    return pl.pallas_call(
        body, name=name, grid_spec=grid_spec, input_output_aliases=aliases,
        out_shape=jax.ShapeDtypeStruct((DEPTH, 2 * half, cdim), F32), compiler_params=_cparams(1),
    )(*ops)


def sum_devices(allp):
    _, r, _ = allp.shape

    def body(a_ref, o_ref):
        tot = a_ref[0]
        for k in range(1, 8):
            tot = tot + a_ref[k]
        o_ref[...] = tot

    tr = r // 2 if r % 16 == 0 else r
    return pl.pallas_call(
        body, name="sum_devices", grid=(r // tr,),
        in_specs=[pl.BlockSpec((8, tr, LANES), lambda i: (0, i, 0))],
        out_specs=pl.BlockSpec((tr, LANES), lambda i: (i, 0)),
        out_shape=jax.ShapeDtypeStruct((r, LANES), F32), compiler_params=_cparams(1),
    )(allp)


def _row_tile(r, cdim, limit_bytes=1 << 20, mult=8):
    best = None
    for tr in range(mult, r + 1, mult):
        if r % tr == 0 and tr * cdim * 4 <= limit_bytes:
            best = tr
    return best if best is not None else r


def adamw(w, g, m, v, name):
    r, cdim = w.shape
    tr = _row_tile(r, cdim)
    bc1 = 1.0 - ADAM_B1 ** ADAM_STEP
    bc2 = 1.0 - ADAM_B2 ** ADAM_STEP

    def body(w_ref, g_ref, m_ref, v_ref, d_ref, nm_ref, nv_ref, go_ref):
        gv = g_ref[...]
        nm = ADAM_B1 * m_ref[...] + (1.0 - ADAM_B1) * gv
        nv = ADAM_B2 * v_ref[...] + (1.0 - ADAM_B2) * (gv * gv)
        d_ref[...] = -ADAM_LR * ((nm / bc1) / (jnp.sqrt(nv / bc2) + ADAM_EPS) + ADAM_WD * w_ref[...])
        nm_ref[...] = nm
        nv_ref[...] = nv
        go_ref[...] = gv

    blk = pl.BlockSpec((tr, cdim), lambda i: (i, 0))
    shape = jax.ShapeDtypeStruct((r, cdim), F32)
    return pl.pallas_call(
        body, name=name, grid=(r // tr,), in_specs=[blk] * 4, out_specs=[blk] * 4, out_shape=[shape] * 4,
        compiler_params=_cparams(1),
    )(w, g, m, v)


def _s5_prepare(lam_re, lam_im, log_dt, b_re, b_im, c_re, c_im):
    groups, ch = 16, 16
    dt = jnp.exp(log_dt)[:, None]
    mag = jnp.exp(lam_re * dt)
    a_r, a_i = mag * jnp.cos(lam_im * dt), mag * jnp.sin(lam_im * dt)
    den = lam_re * lam_re + lam_im * lam_im
    q_r = ((a_r - 1.0) * lam_re + a_i * lam_im) / den
    q_i = (a_i * lam_re - (a_r - 1.0) * lam_im) / den
    bb_r = q_r[..., None] * b_re - q_i[..., None] * b_im
    bb_i = q_r[..., None] * b_im + q_i[..., None] * b_re
    eye = jnp.eye(groups, dtype=F32)

    def expand_b(bb):
        return jnp.einsum("gpc,gh->gchp", bb, eye).reshape(groups * ch, N_STATE)

    def expand_c(cc):
        return jnp.einsum("gcp,gh->hpgc", cc, eye).reshape(N_STATE, groups * ch)

    a2 = jnp.concatenate([a_r.reshape(1, N_STATE), a_i.reshape(1, N_STATE)], axis=1)
    bexp = jnp.concatenate([expand_b(bb_r), expand_b(bb_i)], axis=1)
    cexp = jnp.concatenate([expand_c(c_re), -expand_c(c_im)], axis=0)
    return a2, bexp, cexp


def _lru_prepare(w_r, w_i, lam):
    heads = 4
    eye = jnp.eye(heads, dtype=F32)

    def expand(w):
        return jnp.einsum("hij,hk->hikj", w, eye).reshape(D_GROUP, D_GROUP)

    return expand(w_r), expand(w_i), jax.nn.softplus(-lam).reshape(1, D_GROUP)


def _pad_rows(a, rows):
    return jnp.pad(a, ((0, rows - a.shape[0]), (0, 0)))


def _group_mean_matrix():
    gidx = jnp.arange(D_GROUP) // 64
    return (gidx[:, None] == gidx[None, :]).astype(BF16) * jnp.asarray(1.0 / 64.0, BF16)


def _pack_rows(arrs, width):
    parts = []
    for a in arrs:
        flat = a.reshape(-1)
        pad = (-flat.shape[0]) % width
        parts.append(jnp.pad(flat, (0, pad)) if pad else flat)
    flat = jnp.concatenate(parts)
    rows = flat.shape[0] // width
    pad_rows = (-rows) % 16
    if pad_rows:
        flat = jnp.pad(flat, (0, pad_rows * width))
    return flat.reshape(-1, width)


def _unpack_rows(packed, shapes, width):
    flat = packed.reshape(-1)
    out, off = [], 0
    for shp in shapes:
        size = math.prod(shp)
        out.append(flat[off:off + size].reshape(shp))
        off += size + ((-size) % width)
    return out


def kernel(x, mem, ln_in_g, ln_in_b, w_in, b_in, s5_lam_re, s5_lam_im, s5_log_dt, s5_b_re, s5_b_im, s5_c_re, s5_c_im, s5_d, s5_w_glu, s5_b_glu, cv_w, cv_b, cv_gn_g, cv_gn_b, cv_w_pw, cv_b_pw, lru_conv_w, lru_conv_b, lru_w_r, lru_b_r, lru_w_i, lru_b_i, lru_lam, attn_w_kv, w_out, b_out, ln1_g, ln1_b, ffn_w_up, ffn_conv_w, ffn_conv_b, ffn_w_down, ln2_g, ln2_b, loss_target, m_ln_in_g, m_ln_in_b, m_w_in, m_b_in, m_s5_lam_re, m_s5_lam_im, m_s5_log_dt, m_s5_b_re, m_s5_b_im, m_s5_c_re, m_s5_c_im, m_s5_d, m_s5_w_glu, m_s5_b_glu, m_cv_w, m_cv_b, m_cv_gn_g, m_cv_gn_b, m_cv_w_pw, m_cv_b_pw, m_lru_conv_w, m_lru_conv_b, m_lru_w_r, m_lru_b_r, m_lru_w_i, m_lru_b_i, m_lru_lam, m_attn_w_kv, m_w_out, m_b_out, m_ln1_g, m_ln1_b, m_ffn_w_up, m_ffn_conv_w, m_ffn_conv_b, m_ffn_w_down, m_ln2_g, m_ln2_b, v_ln_in_g, v_ln_in_b, v_w_in, v_b_in, v_s5_lam_re, v_s5_lam_im, v_s5_log_dt, v_s5_b_re, v_s5_b_im, v_s5_c_re, v_s5_c_im, v_s5_d, v_s5_w_glu, v_s5_b_glu, v_cv_w, v_cv_b, v_cv_gn_g, v_cv_gn_b, v_cv_w_pw, v_cv_b_pw, v_lru_conv_w, v_lru_conv_b, v_lru_w_r, v_lru_b_r, v_lru_w_i, v_lru_b_i, v_lru_lam, v_attn_w_kv, v_w_out, v_b_out, v_ln1_g, v_ln1_b, v_ffn_w_up, v_ffn_conv_w, v_ffn_conv_b, v_ffn_w_down, v_ln2_g, v_ln2_b):
    p = dict(locals())
    xs = x[0]
    mems = mem[0]
    target = loss_target[0]
    s = xs.shape[0]
    cidx = lax.axis_index("c")
    jidx = 2 * lax.axis_index("x") + lax.axis_index("y")
    tb_scan = min(256, s)
    tb_s5 = min(512, s)
    tb_attn = min(512, s)
    tb_ffn = min(256, s)

    small_sh_names = list(SMALL_SHARDED)
    small_sh_shapes = [p[nm].shape[1:] for nm in small_sh_names]
    slabs = [[_own_slab(_bf(p[nm][l]), jidx) for nm in BIG]
             + [_own_slab(_pack_rows([p[nm][l] for nm in small_sh_names], LANES), jidx)] for l in range(DEPTH)]
    gathered = [gather_weights(slabs[0])] + [None] * (DEPTH - 1)

    def weight_views(gw):
        views = dict(w_in=gw[0][None], w_kv=gw[1].reshape(1, 1, D_MODEL, 2 * D_GROUP),
                     w_out=gw[2].reshape(1, 1, D_MODEL, D_MODEL), w_up=gw[3][None],
                     w_down=gw[4].reshape(1, 1, D_FF, D_MODEL))
        per_chip = [_unpack_rows(gw[5][jj], small_sh_shapes, LANES) for jj in range(N_CHIPS)]
        for k, nm in enumerate(small_sh_names):
            views[nm] = jnp.concatenate([per_chip[jj][k] for jj in range(N_CHIPS)], axis=SMALL_SHARDED[nm] - 1)
        return views

    pmat = _group_mean_matrix()

    def vec(a):
        return a.reshape(1, -1)

    xh0, rs0, xb0 = ln_fwd(xs, vec(ln_in_g), vec(ln_in_b), "ln_in")
    saved = []
    prev = dict(xh=xh0, rs=rs0, xb=xb0, g=vec(ln_in_g), b=vec(ln_in_b))
    for l in range(DEPTH):
        sv = dict(prev=prev)
        (a2, bexp, cexp), sv['s5_vjp'] = jax.vjp(_s5_prepare, s5_lam_re[l], s5_lam_im[l], s5_log_dt[l],
                                                 s5_b_re[l], s5_b_im[l], s5_c_re[l], s5_c_im[l])
        (wr, wi, sp), sv['lru_vjp'] = jax.vjp(_lru_prepare, lru_w_r[l], lru_w_i[l], lru_lam[l])
        sv.update(a2=a2, bexp=_bf(bexp), cexp=_bf(cexp), wr=_bf(wr), wi=_bf(wi), sp=sp)
        gw = sv['gw'] = weight_views(gathered[l])
        sv['cvw'] = _pad_rows(gw['cv_w'], CV_HALO)
        sv['lcw'] = _pad_rows(gw['lru_conv_w'], LRU_HALO)
        sv['fcw'] = _pad_rows(gw['ffn_conv_w'], FFN_TAP_ROWS)
        sv['w_glu'] = _bf(gw['s5_w_glu'])[None]
        sv['w_pw'] = _bf(gw['cv_w_pw'])[None]
        h_in = mm_nn(prev['xb'], gw['w_in'], 0, vec(b_in[l]), F32, 2048, f"in_proj{l}")
        kv = mm_nn(mems, gw['w_kv'], 0, jnp.zeros((1, 2 * D_GROUP), F32), F32, 256, f"kv_proj{l}")
        y_s5, hst, y0 = s5_fwd(h_in, a2, sv['bexp'], sv['cexp'], vec(s5_d[l]), sv['w_glu'], 0, vec(s5_b_glu[l]),
                               tb_s5, f"s5_fwd{l}")
        y_cv, hc = cv_fwd(h_in, sv['cvw'], vec(cv_b[l]), vec(cv_gn_g[l]), vec(cv_gn_b[l]), pmat, sv['w_pw'], 0,
                          vec(cv_b_pw[l]), tb_scan, f"cv_fwd{l}")
        y_lru, xcs, hls = lru_fwd(h_in, sv['lcw'], vec(lru_conv_b[l]), sv['wr'], vec(lru_b_r[l]), sv['wi'],
                                  vec(lru_b_i[l]), sp, tb_scan, f"lru_fwd{l}")
        y_mem = attn_fwd(h_in, kv, tb_attn, f"attn_fwd{l}")
        mix_in = jnp.concatenate([y_s5, y_cv, y_lru, y_mem], axis=1)
        xh1, rs1, xb1 = proj_ln(mix_in, gw['w_out'].reshape(1, D_MODEL, D_MODEL), 0, vec(b_out[l]),
                                prev['xh'], prev['g'], prev['b'], vec(ln1_g[l]), vec(ln1_b[l]), f"out_proj_ln{l}")
        u = mm_nn(xb1, gw['w_up'], 0, jnp.zeros((1, 2 * D_FF), F32), F32, 1024, f"ffn_up{l}")
        nxt = slabs[l + 1] if l + 1 < DEPTH else ()
        hf, sv['ffn_kept'], *got = ffn_act_fwd(u, sv['fcw'], vec(ffn_conv_b[l]), tb_ffn, f"ffn_act{l}", gather=nxt)
        if nxt:
            gathered[l + 1] = got
        xh2, rs2, xb2 = proj_ln(hf, gw['w_down'].reshape(1, D_FF, D_MODEL), 0, jnp.zeros((1, D_MODEL), F32),
                                xh1, vec(ln1_g[l]), vec(ln1_b[l]), vec(ln2_g[l]), vec(ln2_b[l]), f"ffn_down_ln{l}")
        sv.update(h_in=h_in, kv=kv, hst=hst, y0=y0, hc=hc, xcs=xcs, hls=hls, mix_in=mix_in,
                  xh1=xh1, rs1=rs1, xb1=xb1, u=u, hf=hf, xh2=xh2, rs2=rs2)
        saved.append(sv)
        prev = dict(xh=xh2, rs=rs2, xb=xb2, g=vec(ln2_g[l]), b=vec(ln2_b[l]))

    grads = {}
    per_layer = {nm: [None] * DEPTH for nm in WEIGHTS if nm not in ('ln_in_g', 'ln_in_b')}
    c1 = cidx.reshape(1).astype(jnp.int32)
    jc = jnp.stack([jidx, cidx]).astype(jnp.int32)
    red_big = [None] * len(BIG)
    pending = None
    below = None

    def chip_parts(gl, l, small_all=None):
        gl = [g.reshape((N_CHIPS,) + p[nm].shape[1:]) for g, nm in zip(gl, BIG)]
        got = exchange_cores(gl, f"exchange_cores{l}", small_all)
        parts = [add_core_halves(g, ga, c1, f"add_cores_{nm}{l}") for g, ga, nm in zip(gl, got, BIG)]
        return parts, (got[len(BIG)] if small_all is not None else None)

    def own_sum(parts, got, l):
        for k, nm in enumerate(BIG):
            red_big[k] = add_chip_parts(parts[k], got[k], jc, l, red_big[k], f"add_chips_{nm}{l}")

    for l in reversed(range(DEPTH)):
        sv = saved[l]
        pv = sv['prev']
        gw = sv['gw']
        gl = [None] * len(BIG)
        if l == DEPTH - 1:
            dr2, dg2, db2, sqerr = loss_ln_bwd(target, sv['xh2'], sv['rs2'], vec(ln2_g[l]), vec(ln2_b[l]), "loss_ln2_bwd")
            loss_local = 0.5 / D_MODEL * jnp.sum(sqerr)
        else:
            dr2, dg2, db2 = below
        per_layer['ln2_g'][l], per_layer['ln2_b'][l] = dg2[0], db2[0]
        tm_nt = min(512, s)
        ts_big = min(2048, s)
        whole = lambda a_ref, j: a_ref[...]
        dhf = mm_nt(dr2, (tm_nt, D_MODEL), lambda i: (i, 0), whole, gw['w_down'], 0, None, F32, 512, s, f"ffn_down_dx{l}")
        gl[4] = mm_tn(sv['hf'], dr2, (min(1024, s), D_MODEL), lambda kt, j, st: (st, 0), 1, D_MODEL, FFN_TN, 1024, s,
                      f"ffn_down_dw{l}")
        waiting = pending[1] if pending is not None else ()
        du, dcwv, dcwg, *got = ffn_act_bwd(dhf, sv['u'], sv['ffn_kept'], sv['fcw'], tb_ffn, f"ffn_act_bwd{l}",
                                           scatter=waiting)
        if pending is not None:
            own_sum(pending[1], got, pending[0])
            pending = None
        dcw = jnp.concatenate([dcwv, dcwg], axis=1)
        per_layer['ffn_conv_w'][l] = dcw[0:FFN_CONV_WIDTH]
        per_layer['ffn_conv_b'][l] = dcw[FFN_CONV_WIDTH]
        dr1, dg1, db1, cs1 = mm_nt(du, (2, tm_nt, D_FF), lambda i: (0, i, 0),
                                   lambda a_ref, j: a_ref[j // 2, :, (j % 2) * FFN_TN:(j % 2 + 1) * FFN_TN], gw['w_up'], 0, dr2,
                                   F32, 512, s, f"ffn_up_dx_ln1_bwd{l}", ln=(sv['xh1'], sv['rs1'], vec(ln1_g[l])))
        gl[3] = mm_tn(sv['xb1'], du, (None, ts_big, FFN_TN), lambda kt, j, st: (j // 2, st, j % 2), N_CHIPS, FFN_TN,
                      D_MODEL, 2048, s, f"ffn_up_dw{l}")
        per_layer['ln1_g'][l], per_layer['ln1_b'][l], per_layer['b_out'][l] = dg1[0], db1[0], cs1[0]
        dmix = mm_nt(dr1, (tm_nt, D_MODEL), lambda i: (i, 0), whole, gw['w_out'], 0, None, F32, 512, s, f"out_proj_dx{l}")
        gl[2] = mm_tn(sv['mix_in'], dr1, (min(1024, s), D_MODEL), lambda kt, j, st: (st, 0), 1, D_MODEL, D_MODEL, 1024, s,
                      f"out_proj_dw{l}")
        h_in = sv['h_in']
        (d_u, cs_u, d_bexp, d_cexp, d_dd, d_wglu, d_bglu, d_a2) = s5_bwd(
            dmix, h_in, sv['y0'], sv['hst'], sv['a2'], sv['bexp'], sv['cexp'], vec(s5_d[l]), sv['w_glu'], 0,
            vec(s5_b_glu[l]), tb_s5, f"s5_bwd{l}")
        (d_vg, cs_vg, d_cvw, d_cvb, d_gg, d_gb, d_wpw, d_bpw) = cv_bwd(
            dmix, h_in, sv['hc'], sv['cvw'], vec(cv_gn_g[l]), vec(cv_gn_b[l]), pmat, sv['w_pw'], 0, tb_scan, f"cv_bwd{l}")
        (d_lx, cs_lx, d_lcw, d_lcb, d_wr, d_br, d_wi, d_bi, d_sp) = lru_bwd(
            dmix, h_in, sv['xcs'], sv['hls'], sv['lcw'], sv['wr'], vec(lru_b_r[l]), sv['wi'], vec(lru_b_i[l]),
            sv['sp'], tb_scan, f"lru_bwd{l}")
        d_q, cs_q, d_kv = attn_bwd(dmix, h_in, sv['kv'], tb_attn, f"attn_bwd{l}")
        g_s5 = sv['s5_vjp']((d_a2, d_bexp, d_cexp))
        for nm, gval in zip(['s5_lam_re', 's5_lam_im', 's5_log_dt', 's5_b_re', 's5_b_im', 's5_c_re', 's5_c_im'], g_s5):
            per_layer[nm][l] = gval
        g_lru = sv['lru_vjp']((d_wr, d_wi, d_sp))
        for nm, gval in zip(['lru_w_r', 'lru_w_i', 'lru_lam'], g_lru):
            per_layer[nm][l] = gval
        per_layer['s5_d'][l], per_layer['s5_w_glu'][l], per_layer['s5_b_glu'][l] = d_dd[0], d_wglu, d_bglu[0]
        per_layer['cv_w'][l], per_layer['cv_b'][l] = d_cvw[0:CONV_WIDTH], d_cvb[0]
        per_layer['cv_gn_g'][l], per_layer['cv_gn_b'][l] = d_gg[0], d_gb[0]
        per_layer['cv_w_pw'][l], per_layer['cv_b_pw'][l] = d_wpw, d_bpw[0]
        per_layer['lru_conv_w'][l], per_layer['lru_conv_b'][l] = d_lcw[0:LRU_CONV_WIDTH], d_lcb[0]
        per_layer['lru_b_r'][l], per_layer['lru_b_i'][l] = d_br[0], d_bi[0]
        per_layer['b_in'][l] = jnp.concatenate([cs_u, cs_vg, cs_lx, cs_q], axis=1)[0]
        gl[1] = mm_tn(mems, d_kv, (MEM_ROWS, 2 * D_GROUP), lambda kt, j, st: (st, 0), 1, 2 * D_GROUP, D_MODEL, MEM_ROWS,
                      MEM_ROWS, f"kv_proj_dw{l}")
        dh_in = jnp.concatenate([d_u, d_vg, d_lx, d_q], axis=1)
        n_sh = N_IN // N_CHIPS
        below = mm_nt(dh_in, (tm_nt, N_IN), lambda i: (i, 0), lambda a_ref, j: a_ref[:, j * n_sh:(j + 1) * n_sh],
                      gw['w_in'], 0, dr1, F32, 512, s, f"in_proj_dx_ln_bwd{l}", ln=(pv['xh'], pv['rs'], pv['g']))[:3]
        gl[0] = mm_tn(pv['xb'], dh_in, (ts_big, n_sh), lambda kt, j, st: (st, j), N_CHIPS, n_sh, D_MODEL, 2048, s,
                      f"in_proj_dw{l}")
        if l > 0:
            pending = (l, chip_parts(gl, l)[0])
    grad_x, dg_in, db_in = below
    grads['ln_in_g'], grads['ln_in_b'] = dg_in[0], db_in[0]
    for nm, vals in per_layer.items():
        if nm not in BIG:
            grads[nm] = jnp.stack(vals)

    small_names = [nm for nm in WEIGHTS if nm not in BIG]
    small_local = _pack_rows([grads[nm] for nm in small_names], LANES)
    me = 4 * lax.axis_index("x") + 2 * lax.axis_index("y") + cidx
    small_slab = lax.dynamic_update_slice_in_dim(jnp.zeros((8,) + small_local.shape, F32), small_local[None], me, axis=0)
    parts, small_all = chip_parts(gl, 0, small_slab)
    own_sum(parts, scatter_shards(parts, "scatter_shards0"), 0)
    red_big = share_with_sibling(red_big)
    small_red = sum_devices(small_all)
    small_grads = dict(zip(small_names, _unpack_rows(small_red, [grads[nm].shape for nm in small_names], LANES)))

    out_g, out_d, out_m, out_v = {}, {}, {}, {}
    for k, nm in enumerate(BIG):
        gk = red_big[k]
        two_d = (-1, p[nm].shape[-1])
        res = adamw(p[nm].reshape(two_d), gk.reshape(two_d), p['m_' + nm].reshape(two_d),
                    p['v_' + nm].reshape(two_d), f"adamw_{nm}")
        out_d[nm], out_m[nm], out_v[nm], out_g[nm] = (t.reshape(p[nm].shape) for t in res)
    own = {}
    for nm in small_names:
        gfull = small_grads[nm]
        if nm in SMALL_SHARDED:
            ax = SMALL_SHARDED[nm]
            width = p[nm].shape[ax]
            gfull = lax.dynamic_slice_in_dim(gfull, jidx * width, width, axis=ax)
        own[nm] = gfull
    packs = [_pack_rows([src[nm] for nm in small_names], LANES)
             for src in (dict((nm, p[nm]) for nm in small_names), own,
                         dict((nm, p['m_' + nm]) for nm in small_names), dict((nm, p['v_' + nm]) for nm in small_names))]
    dlt, nm_, nv_, _ = adamw(packs[0], packs[1], packs[2], packs[3], "adamw_small")
    shapes = [p[nm].shape for nm in small_names]
    for dst, packed in ((out_d, dlt), (out_m, nm_), (out_v, nv_)):
        dst.update(zip(small_names, _unpack_rows(packed, shapes, LANES)))
    out_g.update(own)

    loss = lax.psum(loss_local, ("x", "y", "c"))
    return (loss, grad_x[None], *[out_g[nm] for nm in WEIGHTS], *[out_d[nm] for nm in WEIGHTS],
            *[out_m[nm] for nm in WEIGHTS], *[out_v[nm] for nm in WEIGHTS])
```

```python
import functools
import math

import jax
import jax.numpy as jnp
from jax import lax
from jax.experimental import pallas as pl
from jax.experimental.pallas import tpu as pltpu

F32 = jnp.float32
BF16 = jnp.bfloat16
MESH = pl.DeviceIdType.MESH
ANY = pl.BlockSpec(memory_space=pl.ANY)

DEPTH = 2
D_MODEL = 1024
D_GROUP = 256
N_IN = 6 * D_GROUP
D_FF = 2816
N_STATE = 1024
CONV_WIDTH = 31
LRU_CONV_WIDTH = 4
FFN_CONV_WIDTH = 3
LRU_C = 8.0
ALPHA = (2 * DEPTH) ** 0.25
LN_EPS = 1e-5
N_CHIPS = 4
MEM_ROWS = 256
LANES = 128
SUBLANES = 8
VMEM_LIMIT = 56 * 1024 * 1024

ADAM_LR, ADAM_B1, ADAM_B2, ADAM_EPS, ADAM_WD, ADAM_STEP = 0.001, 0.9, 0.999, 1e-08, 0.01, 10

WEIGHTS = ['ln_in_g', 'ln_in_b', 'w_in', 'b_in', 's5_lam_re', 's5_lam_im', 's5_log_dt', 's5_b_re', 's5_b_im',
           's5_c_re', 's5_c_im', 's5_d', 's5_w_glu', 's5_b_glu', 'cv_w', 'cv_b', 'cv_gn_g', 'cv_gn_b', 'cv_w_pw',
           'cv_b_pw', 'lru_conv_w', 'lru_conv_b', 'lru_w_r', 'lru_b_r', 'lru_w_i', 'lru_b_i', 'lru_lam',
           'attn_w_kv', 'w_out', 'b_out', 'ln1_g', 'ln1_b', 'ffn_w_up', 'ffn_conv_w', 'ffn_conv_b', 'ffn_w_down',
           'ln2_g', 'ln2_b']
BIG = ['w_in', 'attn_w_kv', 'w_out', 'ffn_w_up', 'ffn_w_down', 's5_w_glu', 'cv_w_pw']
FIRST_NEEDED = [0, 1, 5, 6]
SMALL_SHARDED = {'cv_w': 2, 'lru_conv_w': 2, 'ffn_conv_w': 2}


def _cparams(n_axes):
    return pltpu.CompilerParams(dimension_semantics=("arbitrary",) * n_axes, vmem_limit_bytes=VMEM_LIMIT)


def _dot(a, b):
    return jnp.dot(a, b, preferred_element_type=F32)


def _dot_nt(a, b):
    return lax.dot_general(a, b, (((1,), (1,)), ((), ())), preferred_element_type=F32)


def _dot_tn(a, b):
    return lax.dot_general(a, b, (((0,), (0,)), ((), ())), preferred_element_type=F32)


def _bf(v):
    return v.astype(BF16)


def _colsum(v):
    return jnp.sum(v, axis=0, keepdims=True)


def _dot3(v, p):
    hi = _bf(v)
    r1 = v - hi.astype(F32)
    mid = _bf(r1)
    lo = _bf(r1 - mid.astype(F32))
    return _dot(hi, p) + _dot(mid, p) + _dot(lo, p)


_GELU_C = math.sqrt(2.0 / math.pi)


_GELU_C3 = _GELU_C * 0.044715


def _gelu_parts(v):
    t = jnp.tanh(v * (_GELU_C + _GELU_C3 * (v * v)))
    hv = 0.5 * v
    return hv + hv * t, t


def _gelu(v):
    return _gelu_parts(v)[0]


def _gelu_grad(v, t):
    return (0.5 + 0.5 * t) + (0.5 * v) * (1.0 - t * t) * (_GELU_C + (3.0 * _GELU_C3) * (v * v))


def _sigmoid(v):
    return 1.0 / (1.0 + jnp.exp(-v))


def _acc(ref, val, first):
    @pl.when(first)
    def _():
        ref[...] = val

    @pl.when(jnp.logical_not(first))
    def _():
        ref[...] += val


def _rows(shape):
    return lax.broadcasted_iota(jnp.int32, shape, 0)


def _ln_rows(r):
    mu = jnp.mean(r, -1, keepdims=True)
    rc = r - mu
    var = jnp.mean(rc * rc, -1, keepdims=True)
    rs = lax.rsqrt(var + LN_EPS)
    return rc * rs, rs


def ln_fwd(x, g, b, name):
    s = x.shape[0]
    tm = min(512, s)

    def body(x_ref, g_ref, b_ref, xh_ref, rs_ref, xb_ref):
        xh, rs = _ln_rows(x_ref[...])
        xh_ref[...] = xh
        rs_ref[...] = rs
        xb_ref[...] = _bf(xh * g_ref[...] + b_ref[...])

    row = pl.BlockSpec((tm, D_MODEL), lambda i: (i, 0))
    vec = pl.BlockSpec((1, D_MODEL), lambda i: (0, 0))
    return pl.pallas_call(
        body, name=name, grid=(s // tm,),
        in_specs=[row, vec, vec],
        out_specs=[row, pl.BlockSpec((tm, 1), lambda i: (i, 0)), row],
        out_shape=[jax.ShapeDtypeStruct((s, D_MODEL), F32), jax.ShapeDtypeStruct((s, 1), F32),
                   jax.ShapeDtypeStruct((s, D_MODEL), BF16)],
        compiler_params=_cparams(1),
    )(x, g, b)


def proj_ln(a, w, layer, bias, xh_prev, g_prev, b_prev, g, b, name):
    s, k = a.shape
    tm = min(512, s)

    def body(a_ref, w_ref, bias_ref, xp_ref, gp_ref, bp_ref, g_ref, b_ref, xh_ref, rs_ref, xb_ref):
        acc = _dot(a_ref[...], w_ref[...]) + bias_ref[...]
        r = ALPHA * (xp_ref[...] * gp_ref[...] + bp_ref[...]) + acc
        xh, rs = _ln_rows(r)
        xh_ref[...] = xh
        rs_ref[...] = rs
        xb_ref[...] = _bf(xh * g_ref[...] + b_ref[...])

    row = pl.BlockSpec((tm, D_MODEL), lambda i: (i, 0))
    vec = pl.BlockSpec((1, D_MODEL), lambda i: (0, 0))
    return pl.pallas_call(
        body, name=name, grid=(s // tm,),
        in_specs=[pl.BlockSpec((tm, k), lambda i: (i, 0)),
                  pl.BlockSpec((None, k, D_MODEL), lambda i: (layer, 0, 0)),
                  vec, row, vec, vec, vec, vec],
        out_specs=[row, pl.BlockSpec((tm, 1), lambda i: (i, 0)), row],
        out_shape=[jax.ShapeDtypeStruct((s, D_MODEL), F32), jax.ShapeDtypeStruct((s, 1), F32),
                   jax.ShapeDtypeStruct((s, D_MODEL), BF16)],
        compiler_params=_cparams(1),
    )(a, w, bias, xh_prev, g_prev, b_prev, g, b)


def loss_ln_bwd(target, xh, rs, g, b, name):
    s = xh.shape[0]
    tm = min(512, s)

    def body(t_ref, xh_ref, rs_ref, g_ref, b_ref, dr_ref, dg_ref, db_ref, sq_ref):
        first = pl.program_id(0) == 0
        xhv = xh_ref[...]
        err = xhv * g_ref[...] + b_ref[...] - t_ref[...]
        dyv = err * (1.0 / D_MODEL)
        dxh = dyv * g_ref[...]
        dr = rs_ref[...] * (dxh - jnp.mean(dxh, -1, keepdims=True) - xhv * jnp.mean(dxh * xhv, -1, keepdims=True))
        dr_ref[...] = dr
        _acc(dg_ref, _colsum(dyv * xhv), first)
        _acc(db_ref, _colsum(dyv), first)
        _acc(sq_ref, _colsum(err * err), first)

    row = pl.BlockSpec((tm, D_MODEL), lambda i: (i, 0))
    vec = pl.BlockSpec((1, D_MODEL), lambda i: (0, 0))
    vshape = jax.ShapeDtypeStruct((1, D_MODEL), F32)
    return pl.pallas_call(
        body, name=name, grid=(s // tm,),
        in_specs=[row, row, pl.BlockSpec((tm, 1), lambda i: (i, 0)), vec, vec],
        out_specs=[row, vec, vec, vec],
        out_shape=[jax.ShapeDtypeStruct((s, D_MODEL), F32), vshape, vshape, vshape],
        compiler_params=_cparams(1),
    )(target, xh, rs, g, b)


def mm_nn(a, w, layer, bias, out_dtype, tm, name):
    m, k = a.shape
    _, nj, _, n = w.shape
    tm = min(tm, m)

    def body(a_ref, w_ref, b_ref, o_ref):
        o_ref[...] = (_dot(_bf(a_ref[...]), w_ref[...]) + b_ref[...]).astype(out_dtype)

    return pl.pallas_call(
        body, name=name, grid=(nj, m // tm),
        in_specs=[pl.BlockSpec((tm, k), lambda j, i: (i, 0)),
                  pl.BlockSpec((None, None, k, n), lambda j, i: (layer, j, 0, 0)),
                  pl.BlockSpec((1, n), lambda j, i: (0, j))],
        out_specs=pl.BlockSpec((tm, n), lambda j, i: (i, j)),
        out_shape=jax.ShapeDtypeStruct((m, nj * n), out_dtype),
        compiler_params=_cparams(2),
    )(a, w, bias)


def mm_nt(a, a_block, a_map, pick, w, layer, add, out_dtype, tm, m, name, ln=None):
    _, nj, r, n = w.shape
    tm = min(tm, m)
    has_add = add is not None
    n_in = 2 + has_add + (3 if ln is not None else 0)

    def body(*refs):
        a_ref, w_ref = refs[0], refs[1]
        res = _dot_nt(_bf(pick(a_ref, 0)), w_ref[0])
        for j in range(1, nj):
            res = res + _dot_nt(_bf(pick(a_ref, j)), w_ref[j])
        if has_add:
            res = res + ALPHA * refs[2][...]
        if ln is None:
            refs[n_in][...] = res.astype(out_dtype)
            return
        xh_ref, rs_ref, g_ref = refs[n_in - 3:n_in]
        dr_ref, dg_ref, db_ref, cs_ref = refs[n_in:]
        first = pl.program_id(0) == 0
        xhv = xh_ref[...]
        dxh = res * g_ref[...]
        dr = rs_ref[...] * (dxh - jnp.mean(dxh, -1, keepdims=True) - xhv * jnp.mean(dxh * xhv, -1, keepdims=True))
        dr_ref[...] = dr
        _acc(dg_ref, _colsum(res * xhv), first)
        _acc(db_ref, _colsum(res), first)
        _acc(cs_ref, _colsum(dr), first)

    row = pl.BlockSpec((tm, r), lambda i: (i, 0))
    in_specs = [pl.BlockSpec(a_block, a_map),
                pl.BlockSpec((None, nj, r, n), lambda i: (layer, 0, 0, 0))]
    ops = [a, w]
    if has_add:
        in_specs.append(row)
        ops.append(add)
    if ln is None:
        out_specs, out_shape = row, jax.ShapeDtypeStruct((m, r), out_dtype)
    else:
        vec = pl.BlockSpec((1, r), lambda i: (0, 0))
        vshape = jax.ShapeDtypeStruct((1, r), F32)
        in_specs += [row, pl.BlockSpec((tm, 1), lambda i: (i, 0)), vec]
        ops += list(ln)
        out_specs, out_shape = [row, vec, vec, vec], [jax.ShapeDtypeStruct((m, r), F32), vshape, vshape, vshape]
    return pl.pallas_call(
        body, name=name, grid=(m // tm,),
        in_specs=in_specs, out_specs=out_specs, out_shape=out_shape,
        compiler_params=_cparams(1),
    )(*ops)


def mm_tn(a, b, b_block, b_map, nj, n, tk, ts, s, name):
    kx = a.shape[1]
    ts = min(ts, s)

    def body(a_ref, b_ref, o_ref):
        part = _dot_tn(_bf(a_ref[...]), _bf(b_ref[...]))
        _acc(o_ref, part, pl.program_id(2) == 0)

    return pl.pallas_call(
        body, name=name, grid=(kx // tk, nj, s // ts),
        in_specs=[pl.BlockSpec((ts, tk), lambda kt, j, st: (st, kt)), pl.BlockSpec(b_block, b_map)],
        out_specs=pl.BlockSpec((None, tk, n), lambda kt, j, st: (j, kt, 0)),
        out_shape=jax.ShapeDtypeStruct((nj, kx, n), F32),
        compiler_params=_cparams(3),
    )(a, b)


S5_TAB_ROWS = 8 * SUBLANES


def _s5_scan_table(tab_ref, ar, ai, reverse):
    n = N_STATE
    row = _rows((SUBLANES, n))
    edge = SUBLANES - 1 if reverse else 0
    tab_ref[0:8, :] = jnp.where(row == edge, ar, 0.0)
    tab_ref[8:16, :] = jnp.where(row == edge, ai, 0.0)
    pr, pi = ar, ai
    for step, k in enumerate((1, 2, 4)):
        mask = row < SUBLANES - k if reverse else row >= k
        tab_ref[16 + 16 * step:24 + 16 * step, :] = jnp.where(mask, pr, 0.0)
        tab_ref[24 + 16 * step:32 + 16 * step, :] = jnp.where(mask, pi, 0.0)
        pr, pi = pr * pr - pi * pi, 2.0 * pr * pi


def _s5_scan(src_ref, dst_ref, tab_ref, edge_ref, tb, reverse, per_tile=None):
    n = N_STATE
    ng = tb // SUBLANES
    nq = n // LANES
    link = SUBLANES - 1 if reverse else 1

    def tile(ii, carry):
        g = ng - 1 - ii if reverse else ii
        rows = pl.ds(pl.multiple_of(g * SUBLANES, SUBLANES), SUBLANES)
        out = []
        for q in range(nq):
            cre = slice(q * LANES, (q + 1) * LANES)
            cim = slice(n + q * LANES, n + (q + 1) * LANES)
            lr, li = src_ref[rows, cre], src_ref[rows, cim]
            tr, ti = pltpu.roll(carry[2 * q], link, 0), pltpu.roll(carry[2 * q + 1], link, 0)
            kr, ki = tab_ref[0:8, cre], tab_ref[8:16, cre]
            lr, li = lr + kr * tr - ki * ti, li + kr * ti + ki * tr
            for step, k in enumerate((1, 2, 4)):
                amt = SUBLANES - k if reverse else k
                kr, ki = tab_ref[16 + 16 * step:24 + 16 * step, cre], tab_ref[24 + 16 * step:32 + 16 * step, cre]
                sr, si = pltpu.roll(lr, amt, 0), pltpu.roll(li, amt, 0)
                lr, li = lr + kr * sr - ki * si, li + kr * si + ki * sr
            dst_ref[rows, cre] = lr
            dst_ref[rows, cim] = li
            if per_tile is not None:
                per_tile(g, q, (cre, cim), lr, li)
            out += [lr, li]
        return tuple(out)

    init = []
    for q in range(nq):
        init += [edge_ref[:, q * LANES:(q + 1) * LANES], edge_ref[:, n + q * LANES:n + (q + 1) * LANES]]
    fin = lax.fori_loop(0, ng, tile, tuple(init))
    for q in range(nq):
        edge_ref[:, q * LANES:(q + 1) * LANES] = fin[2 * q]
        edge_ref[:, n + q * LANES:n + (q + 1) * LANES] = fin[2 * q + 1]


def s5_fwd(h_in, a2, bexp, cexp, dskip, wglu, layer, bglu, tb, name, gather=()):
    s = h_in.shape[0]
    n = N_STATE
    ng = len(gather)

    def body(u_ref, a_ref, b_ref, c_ref, d_ref, w_ref, bg_ref, *rest):
        y_ref, h_ref, y0_ref = rest[ng:ng + 3]
        slabs = rest[ng + 3:2 * ng + 3]
        edge, tab, bu_ref = rest[2 * ng + 3:2 * ng + 6]
        sems = rest[2 * ng + 6:]

        @pl.when(pl.program_id(0) == 0)
        def _():
            if ng:
                _gather_start(slabs, sems[0:2])
            edge[...] = jnp.zeros_like(edge)
            _s5_scan_table(tab, a_ref[0:1, 0:n], a_ref[0:1, n:2 * n], False)

        u = u_ref[...]
        bu_ref[...] = _dot(_bf(u), b_ref[...])
        _s5_scan(bu_ref, h_ref, tab, edge, tb, False)
        y0 = _dot(_bf(h_ref[:, 0:n]), c_ref[0:n, :]) + _dot(_bf(h_ref[:, n:2 * n]), c_ref[n:2 * n, :]) + d_ref[...] * u
        y0_ref[...] = y0
        yg = _gelu(y0)
        z = _dot(_bf(yg), w_ref[...]) + bg_ref[...]
        y_ref[...] = _bf(yg * _sigmoid(z))
        if ng:
            @pl.when(pl.program_id(0) == s // tb - 1)
            def _():
                _gather_finish(slabs, sems[0:2], sems[2:4])

    vec = pl.BlockSpec((1, D_GROUP), lambda i: (0, 0))
    return pl.pallas_call(
        body, name=name, grid=(s // tb,),
        in_specs=[pl.BlockSpec((tb, D_GROUP), lambda i: (i, 0)),
                  pl.BlockSpec((1, 2 * n), lambda i: (0, 0)),
                  pl.BlockSpec((D_GROUP, 2 * n), lambda i: (0, 0)),
                  pl.BlockSpec((2 * n, D_GROUP), lambda i: (0, 0)),
                  vec,
                  pl.BlockSpec((None, D_GROUP, D_GROUP), lambda i: (layer, 0, 0)),
                  vec] + [ANY] * ng,
        out_specs=[pl.BlockSpec((tb, D_GROUP), lambda i: (i, 0)),
                   pl.BlockSpec((tb, 2 * n), lambda i: (i, 0)),
                   pl.BlockSpec((tb, D_GROUP), lambda i: (i, 0))] + [ANY] * ng,
        out_shape=[jax.ShapeDtypeStruct((s, D_GROUP), BF16), jax.ShapeDtypeStruct((s, 2 * n), F32),
                   jax.ShapeDtypeStruct((s, D_GROUP), F32)] + [jax.ShapeDtypeStruct(a.shape, a.dtype) for a in gather],
        input_output_aliases={7 + k: 3 + k for k in range(ng)},
        scratch_shapes=[pltpu.VMEM((SUBLANES, 2 * n), F32), pltpu.VMEM((S5_TAB_ROWS, n), F32),
                        pltpu.VMEM((tb, 2 * n), F32)] + (_gather_sems(ng) if ng else []),
        compiler_params=_cparams(1),
    )(h_in, a2, bexp, cexp, dskip, wglu, bglu, *gather)


def s5_bwd(dmix, h_in, y0, hst, a2, bexp, cexp, dskip, wglu, layer, bglu, tb, name):
    s = h_in.shape[0]
    n = N_STATE
    nb = s // tb
    halo = tb // 8

    def body(dy_ref, u_ref, y0_ref, h_ref, hp_ref, a_ref, b_ref, c_ref, d_ref, w_ref, bg_ref,
             du_ref, cs_ref, db_ref, dc_ref, dd_ref, dw_ref, dbg_ref, da_ref, edge, tab, g_ref, da_acc):
        i = pl.program_id(0)
        first = i == 0

        @pl.when(first)
        def _():
            edge[...] = jnp.zeros_like(edge)
            da_acc[...] = jnp.zeros_like(da_acc)
            _s5_scan_table(tab, a_ref[0:1, 0:n], -a_ref[0:1, n:2 * n], True)

        dy = dy_ref[...]
        u = u_ref[...]
        y0v = y0_ref[...]
        yg, t = _gelu_parts(y0v)
        z = _dot(_bf(yg), w_ref[...]) + bg_ref[...]
        sg = _sigmoid(z)
        dz = dy * yg * sg * (1.0 - sg)
        dyg = dy * sg + _dot_nt(_bf(dz), w_ref[...])
        _acc(dw_ref, _dot_tn(_bf(yg), _bf(dz)), first)
        _acc(dbg_ref, _colsum(dz), first)
        dy0 = dyg * _gelu_grad(y0v, t)
        _acc(dd_ref, _colsum(dy0 * u), first)
        dy0b = _bf(dy0)
        _acc(dc_ref.at[0:n, :], _dot_tn(_bf(h_ref[:, 0:n]), dy0b), first)
        _acc(dc_ref.at[n:2 * n, :], _dot_tn(_bf(h_ref[:, n:2 * n]), dy0b), first)
        g_ref[...] = _dot_nt(dy0b, c_ref[...])
        keep = jnp.where(i == nb - 1, 0.0, 1.0)
        row0 = _rows((SUBLANES, LANES)) == 0

        def grad_a(g, q, cols, gr, gi):
            cre, cim = cols
            rows = pl.ds(pl.multiple_of(g * SUBLANES, SUBLANES), SUBLANES)
            before = pl.ds(pl.multiple_of(jnp.maximum(g - 1, 0) * SUBLANES, SUBLANES), SUBLANES)
            pre = jnp.where(g == 0, hp_ref[:, cre] * keep, h_ref[before, cre])
            pim = jnp.where(g == 0, hp_ref[:, cim] * keep, h_ref[before, cim])
            pr = jnp.where(row0, pltpu.roll(pre, 1, 0), pltpu.roll(h_ref[rows, cre], 1, 0))
            pi = jnp.where(row0, pltpu.roll(pim, 1, 0), pltpu.roll(h_ref[rows, cim], 1, 0))
            da_acc[:, cre] += gr * pr + gi * pi
            da_acc[:, cim] += gi * pr - gr * pi

        _s5_scan(g_ref, g_ref, tab, edge, tb, True, grad_a)
        da_ref[...] = _colsum(da_acc[...])
        gr, gi = g_ref[:, 0:n], g_ref[:, n:2 * n]
        grb, gib = _bf(gr), _bf(gi)
        du = d_ref[...] * dy0 + _dot_nt(grb, b_ref[:, 0:n]) + _dot_nt(gib, b_ref[:, n:2 * n])
        ub = _bf(u)
        _acc(db_ref.at[:, 0:n], _dot_tn(ub, grb), first)
        _acc(db_ref.at[:, n:2 * n], _dot_tn(ub, gib), first)
        du_ref[...] = _bf(du)
        _acc(cs_ref, _colsum(du), first)

    rev = lambda i: (nb - 1 - i, 0)
    vec = pl.BlockSpec((1, D_GROUP), lambda i: (0, 0))
    vshape = jax.ShapeDtypeStruct((1, D_GROUP), F32)
    return pl.pallas_call(
        body, name=name, grid=(nb,),
        in_specs=[pl.BlockSpec((tb, D_GROUP), rev),
                  pl.BlockSpec((tb, D_GROUP), rev),
                  pl.BlockSpec((tb, D_GROUP), rev),
                  pl.BlockSpec((tb, 2 * n), rev),
                  pl.BlockSpec((8, 2 * n), lambda i: (jnp.maximum((nb - 1 - i) * halo - 1, 0), 0)),
                  pl.BlockSpec((1, 2 * n), lambda i: (0, 0)),
                  pl.BlockSpec((D_GROUP, 2 * n), lambda i: (0, 0)),
                  pl.BlockSpec((2 * n, D_GROUP), lambda i: (0, 0)),
                  vec,
                  pl.BlockSpec((None, D_GROUP, D_GROUP), lambda i: (layer, 0, 0)),
                  vec],
        out_specs=[pl.BlockSpec((tb, D_GROUP), rev), vec,
                   pl.BlockSpec((D_GROUP, 2 * n), lambda i: (0, 0)),
                   pl.BlockSpec((2 * n, D_GROUP), lambda i: (0, 0)),
                   vec,
                   pl.BlockSpec((D_GROUP, D_GROUP), lambda i: (0, 0)),
                   vec,
                   pl.BlockSpec((1, 2 * n), lambda i: (0, 0))],
        out_shape=[jax.ShapeDtypeStruct((s, D_GROUP), BF16), vshape,
                   jax.ShapeDtypeStruct((D_GROUP, 2 * n), F32), jax.ShapeDtypeStruct((2 * n, D_GROUP), F32),
                   vshape, jax.ShapeDtypeStruct((D_GROUP, D_GROUP), F32), vshape,
                   jax.ShapeDtypeStruct((1, 2 * n), F32)],
        scratch_shapes=[pltpu.VMEM((SUBLANES, 2 * n), F32), pltpu.VMEM((S5_TAB_ROWS, n), F32),
                        pltpu.VMEM((tb, 2 * n), F32), pltpu.VMEM((SUBLANES, 2 * n), F32)],
        compiler_params=_cparams(1),
    )(dmix, h_in, y0, hst, hst, a2, bexp, cexp, dskip, wglu, bglu)


CV_HALO = 32


def _gn_stats(hc, pmat):
    mu = _dot3(hc, pmat)
    xc = hc - mu
    var = _dot3(xc * xc, pmat)
    rstd = lax.rsqrt(var + LN_EPS)
    return xc * rstd, rstd


def cv_fwd(h_in, cw, cb, gg, gb, pmat, wpw, layer, bpw, tb, name):
    s = h_in.shape[0]
    hl = CV_HALO

    def body(v_ref, g_ref, cw_ref, cb_ref, gg_ref, gb_ref, p_ref, w_ref, bw_ref, y_ref, hc_ref, ext):
        @pl.when(pl.program_id(0) == 0)
        def _():
            ext[0:hl, :] = jnp.zeros((hl, D_GROUP), F32)

        ext[hl:hl + tb, :] = v_ref[...] * _sigmoid(g_ref[...])
        acc = jnp.zeros((tb, D_GROUP), F32) + cb_ref[...]
        for k in range(CONV_WIDTH):
            off = hl - (CONV_WIDTH - 1) + k
            acc = acc + cw_ref[k:k + 1, :] * ext[off:off + tb, :]
        hc_ref[...] = acc
        ext[0:hl, :] = ext[tb:tb + hl, :]
        xn, _ = _gn_stats(acc, p_ref[...])
        hn = xn * gg_ref[...] + gb_ref[...]
        hs = hn * _sigmoid(hn)
        y_ref[...] = _bf(_dot(_bf(hs), w_ref[...]) + bw_ref[...])

    vec = pl.BlockSpec((1, D_GROUP), lambda i: (0, 0))
    sq = pl.BlockSpec((D_GROUP, D_GROUP), lambda i: (0, 0))
    return pl.pallas_call(
        body, name=name, grid=(s // tb,),
        in_specs=[pl.BlockSpec((tb, D_GROUP), lambda i: (i, 1)),
                  pl.BlockSpec((tb, D_GROUP), lambda i: (i, 2)),
                  pl.BlockSpec((hl, D_GROUP), lambda i: (0, 0)),
                  vec, vec, vec, sq,
                  pl.BlockSpec((None, D_GROUP, D_GROUP), lambda i: (layer, 0, 0)),
                  vec],
        out_specs=[pl.BlockSpec((tb, D_GROUP), lambda i: (i, 0)), pl.BlockSpec((tb, D_GROUP), lambda i: (i, 0))],
        out_shape=[jax.ShapeDtypeStruct((s, D_GROUP), BF16), jax.ShapeDtypeStruct((s, D_GROUP), F32)],
        scratch_shapes=[pltpu.VMEM((hl + tb, D_GROUP), F32)],
        compiler_params=_cparams(1),
    )(h_in, h_in, cw, cb, gg, gb, pmat, wpw, bpw)


def cv_bwd(dmix, h_in, hc, cw, gg, gb, pmat, wpw, layer, tb, name, scatter=()):
    s = h_in.shape[0]
    hl = CV_HALO
    nb = s // tb
    per = tb // hl
    ns = len(scatter)

    def body(dy_ref, v_ref, g_ref, vh_ref, gh_ref, hc_ref, cw_ref, gg_ref, gb_ref, p_ref, w_ref, *rest):
        s_refs = rest[:ns]
        dvg_ref, cs_ref, dcw_ref, dcb_ref, dgg_ref, dgb_ref, dw_ref, dbw_ref = rest[ns:ns + 8]
        got_refs = rest[ns + 8:2 * ns + 8]
        ext, dext, head = rest[2 * ns + 8:2 * ns + 11]
        sems = rest[2 * ns + 11:]
        i = pl.program_id(0)
        first = i == 0

        @pl.when(first)
        def _():
            if ns:
                _scatter_start(s_refs, got_refs, sems)
            head[...] = jnp.zeros_like(head)

        dy = dy_ref[...]
        pm = p_ref[...]
        xn, rstd = _gn_stats(hc_ref[...], pm)
        hn = xn * gg_ref[...] + gb_ref[...]
        sg = _sigmoid(hn)
        hs = hn * sg
        dyb = _bf(dy)
        _acc(dbw_ref, _colsum(dy), first)
        _acc(dw_ref, _dot_tn(_bf(hs), dyb), first)
        dhs = _dot_nt(dyb, w_ref[...])
        dhn = dhs * sg * (1.0 + hn * (1.0 - sg))
        _acc(dgg_ref, _colsum(dhn * xn), first)
        _acc(dgb_ref, _colsum(dhn), first)
        dxn = dhn * gg_ref[...]
        dhc = rstd * (dxn - _dot3(dxn, pm) - xn * _dot3(dxn * xn, pm))
        _acc(dcb_ref, _colsum(dhc), first)
        v = v_ref[...]
        sgg = _sigmoid(g_ref[...])
        keep = jnp.where(i == nb - 1, 0.0, 1.0)
        ext[0:hl, :] = vh_ref[...] * _sigmoid(gh_ref[...]) * keep
        ext[hl:hl + tb, :] = v * sgg
        dext[0:tb, :] = dhc
        dext[tb:tb + hl, :] = head[...]
        head[...] = dhc[0:hl]
        dhg = jnp.zeros((tb, D_GROUP), F32)
        for k in range(CONV_WIDTH):
            off = hl - (CONV_WIDTH - 1) + k
            wk = _colsum(dhc * ext[off:off + tb, :])
            _acc(dcw_ref.at[k:k + 1, :], wk, first)
            back = CONV_WIDTH - 1 - k
            dhg = dhg + cw_ref[k:k + 1, :] * dext[back:back + tb, :]

        @pl.when(first)
        def _():
            dcw_ref[CONV_WIDTH:hl, :] = jnp.zeros((hl - CONV_WIDTH, D_GROUP), F32)

        dv = dhg * sgg
        dg = dhg * v * sgg * (1.0 - sgg)
        dvg_ref[:, 0:D_GROUP] = _bf(dv)
        dvg_ref[:, D_GROUP:2 * D_GROUP] = _bf(dg)
        _acc(cs_ref.at[:, 0:D_GROUP], _colsum(dv), first)
        _acc(cs_ref.at[:, D_GROUP:2 * D_GROUP], _colsum(dg), first)
        if ns:
            @pl.when(i == nb - 1)
            def _():
                _scatter_finish(s_refs, got_refs, sems)

    vec = pl.BlockSpec((1, D_GROUP), lambda i: (0, 0))
    sq = pl.BlockSpec((D_GROUP, D_GROUP), lambda i: (0, 0))
    tap = pl.BlockSpec((hl, D_GROUP), lambda i: (0, 0))
    vshape = jax.ShapeDtypeStruct((1, D_GROUP), F32)

    def blk(col):
        return pl.BlockSpec((tb, D_GROUP), lambda i: (nb - 1 - i, col))

    def halo_blk(col):
        return pl.BlockSpec((hl, D_GROUP), lambda i: (jnp.maximum((nb - 1 - i) * per - 1, 0), col))

    return pl.pallas_call(
        body, name=name, grid=(nb,),
        in_specs=[blk(1), blk(1), blk(2), halo_blk(1), halo_blk(2),
                  pl.BlockSpec((tb, D_GROUP), lambda i: (nb - 1 - i, 0)),
                  tap, vec, vec, sq,
                  pl.BlockSpec((None, D_GROUP, D_GROUP), lambda i: (layer, 0, 0))] + [ANY] * ns,
        out_specs=[pl.BlockSpec((tb, 2 * D_GROUP), lambda i: (nb - 1 - i, 0)),
                   pl.BlockSpec((1, 2 * D_GROUP), lambda i: (0, 0)),
                   tap, vec, vec, vec, sq, vec] + [ANY] * ns,
        out_shape=[jax.ShapeDtypeStruct((s, 2 * D_GROUP), BF16), jax.ShapeDtypeStruct((1, 2 * D_GROUP), F32),
                   jax.ShapeDtypeStruct((hl, D_GROUP), F32), vshape, vshape, vshape,
                   jax.ShapeDtypeStruct((D_GROUP, D_GROUP), F32), vshape] + _scatter_shapes(scatter),
        scratch_shapes=[pltpu.VMEM((hl + tb, D_GROUP), F32), pltpu.VMEM((tb + hl, D_GROUP), F32),
                        pltpu.VMEM((hl, D_GROUP), F32)] + (_scatter_sems(ns) if ns else []),
        compiler_params=_cparams(1),
    )(dmix, h_in, h_in, h_in, h_in, hc, cw, gg, gb, pmat, wpw, *scatter)


LRU_HALO = 8


def _lru_gates(xc, wr_ref, br_ref, wi_ref, bi_ref, sp_ref):
    xcb = _bf(xc)
    r = _sigmoid(_dot(xcb, wr_ref[...]) + br_ref[...])
    gi = _sigmoid(_dot(xcb, wi_ref[...]) + bi_ref[...])
    la = -LRU_C * r * sp_ref[...]
    a = jnp.exp(la)
    e2 = a * a
    sq = jnp.sqrt(-jnp.tanh(la) * (e2 + 1.0))
    return r, gi, a, e2, sq


def _rscan(a, b, tb, reverse):
    row = _rows(a.shape)
    sh = 1
    while sh < tb:
        if reverse:
            amt, mask = tb - sh, row < tb - sh
        else:
            amt, mask = sh, row >= sh
        a_s = jnp.where(mask, pltpu.roll(a, amt, 0), 1.0)
        b_s = jnp.where(mask, pltpu.roll(b, amt, 0), 0.0)
        b = b + a * b_s
        a = a * a_s
        sh *= 2
    return a, b


def lru_fwd(h_in, cw, cb, wr, br, wi, bi, sp, tb, name):
    s = h_in.shape[0]
    hl = LRU_HALO

    def body(xg_ref, xr_ref, cw_ref, cb_ref, wr_ref, br_ref, wi_ref, bi_ref, sp_ref, y_ref, xc_ref, h_ref, ext, carry):
        @pl.when(pl.program_id(0) == 0)
        def _():
            ext[0:hl, :] = jnp.zeros((hl, D_GROUP), F32)
            carry[...] = jnp.zeros_like(carry)

        ext[hl:hl + tb, :] = xr_ref[...]
        xc = jnp.zeros((tb, D_GROUP), F32) + cb_ref[...]
        for k in range(LRU_CONV_WIDTH):
            off = hl - (LRU_CONV_WIDTH - 1) + k
            xc = xc + cw_ref[k:k + 1, :] * ext[off:off + tb, :]
        xc_ref[...] = xc
        ext[0:hl, :] = ext[tb:tb + hl, :]
        r, gi, a, e2, sq = _lru_gates(xc, wr_ref, br_ref, wi_ref, bi_ref, sp_ref)
        pa, hloc = _rscan(a, sq * (gi * xc), tb, False)
        h = hloc + pa * carry[7:8, :]
        h_ref[...] = h
        carry[...] = h[tb - 8:tb]
        y_ref[...] = _bf(h * _gelu(xg_ref[...]))

    vec = pl.BlockSpec((1, D_GROUP), lambda i: (0, 0))
    sq_spec = pl.BlockSpec((D_GROUP, D_GROUP), lambda i: (0, 0))
    blk = pl.BlockSpec((tb, D_GROUP), lambda i: (i, 0))
    return pl.pallas_call(
        body, name=name, grid=(s // tb,),
        in_specs=[pl.BlockSpec((tb, D_GROUP), lambda i: (i, 3)),
                  pl.BlockSpec((tb, D_GROUP), lambda i: (i, 4)),
                  pl.BlockSpec((hl, D_GROUP), lambda i: (0, 0)),
                  vec, sq_spec, vec, sq_spec, vec, vec],
        out_specs=[blk, blk, blk],
        out_shape=[jax.ShapeDtypeStruct((s, D_GROUP), BF16), jax.ShapeDtypeStruct((s, D_GROUP), F32),
                   jax.ShapeDtypeStruct((s, D_GROUP), F32)],
        scratch_shapes=[pltpu.VMEM((hl + tb, D_GROUP), F32), pltpu.VMEM((8, D_GROUP), F32)],
        compiler_params=_cparams(1),
    )(h_in, h_in, cw, cb, wr, br, wi, bi, sp)


def lru_bwd(dmix, h_in, xcs, hs, cw, wr, br, wi, bi, sp, tb, name):
    s = h_in.shape[0]
    hl = LRU_HALO
    nb = s // tb
    per = tb // hl

    def body(dy_ref, xg_ref, xr_ref, xrh_ref, xc_ref, h_ref, hp_ref, cw_ref, wr_ref, br_ref, wi_ref, bi_ref, sp_ref,
             dx_ref, cs_ref, dcw_ref, dcb_ref, dwr_ref, dbr_ref, dwi_ref, dbi_ref, dsp_ref,
             ext, dext, head, anext, gnext):
        i = pl.program_id(0)
        first = i == 0

        @pl.when(first)
        def _():
            head[...] = jnp.zeros_like(head)
            anext[...] = jnp.zeros_like(anext)
            gnext[...] = jnp.zeros_like(gnext)

        dy = dy_ref[...]
        xg = xg_ref[...]
        xc = xc_ref[...]
        h = h_ref[...]
        r, gi, a, e2, sq = _lru_gates(xc, wr_ref, br_ref, wi_ref, bi_ref, sp_ref)
        gate, t = _gelu_parts(xg)
        dh = dy * gate
        dxg = dy * h * _gelu_grad(xg, t)
        row = _rows((tb, D_GROUP))
        coef = jnp.where(row == tb - 1, anext[0:1, :], pltpu.roll(a, tb - 1, 0))
        pc, gloc = _rscan(coef, dh, tb, True)
        gfull = gloc + pc * gnext[0:1, :]
        anext[...] = a[0:8]
        gnext[...] = gfull[0:8]
        keep = jnp.where(i == nb - 1, 0.0, 1.0)
        hprev = jnp.where(row == 0, hp_ref[7:8, :] * keep, pltpu.roll(h, 1, 0))
        da = gfull * hprev
        uu = gi * xc
        dsq = gfull * uu
        duu = gfull * sq
        dla = da * a - dsq * e2 / sq
        sp = sp_ref[...]
        dr = dla * (-LRU_C) * sp
        _acc(dsp_ref, _colsum(dla * (-LRU_C) * r), first)
        dzr = dr * r * (1.0 - r)
        dzi = duu * xc * gi * (1.0 - gi)
        dzrb, dzib = _bf(dzr), _bf(dzi)
        dxc = duu * gi + _dot_nt(dzrb, wr_ref[...]) + _dot_nt(dzib, wi_ref[...])
        xcb = _bf(xc)
        _acc(dwr_ref, _dot_tn(xcb, dzrb), first)
        _acc(dwi_ref, _dot_tn(xcb, dzib), first)
        _acc(dbr_ref, _colsum(dzr), first)
        _acc(dbi_ref, _colsum(dzi), first)
        _acc(dcb_ref, _colsum(dxc), first)
        ext[0:hl, :] = xrh_ref[...] * keep
        ext[hl:hl + tb, :] = xr_ref[...]
        dext[0:tb, :] = dxc
        dext[tb:tb + hl, :] = head[...]
        head[...] = dxc[0:hl]
        dxr = jnp.zeros((tb, D_GROUP), F32)
        for k in range(LRU_CONV_WIDTH):
            off = hl - (LRU_CONV_WIDTH - 1) + k
            _acc(dcw_ref.at[k:k + 1, :], _colsum(dxc * ext[off:off + tb, :]), first)
            back = LRU_CONV_WIDTH - 1 - k
            dxr = dxr + cw_ref[k:k + 1, :] * dext[back:back + tb, :]

        @pl.when(first)
        def _():
            dcw_ref[LRU_CONV_WIDTH:hl, :] = jnp.zeros((hl - LRU_CONV_WIDTH, D_GROUP), F32)

        dx_ref[:, 0:D_GROUP] = _bf(dxg)
        dx_ref[:, D_GROUP:2 * D_GROUP] = _bf(dxr)
        _acc(cs_ref.at[:, 0:D_GROUP], _colsum(dxg), first)
        _acc(cs_ref.at[:, D_GROUP:2 * D_GROUP], _colsum(dxr), first)

    vec = pl.BlockSpec((1, D_GROUP), lambda i: (0, 0))
    sq_spec = pl.BlockSpec((D_GROUP, D_GROUP), lambda i: (0, 0))
    tap = pl.BlockSpec((hl, D_GROUP), lambda i: (0, 0))
    vshape = jax.ShapeDtypeStruct((1, D_GROUP), F32)
    sshape = jax.ShapeDtypeStruct((D_GROUP, D_GROUP), F32)

    def blk(col):
        return pl.BlockSpec((tb, D_GROUP), lambda i: (nb - 1 - i, col))

    def halo_blk(col):
        return pl.BlockSpec((hl, D_GROUP), lambda i: (jnp.maximum((nb - 1 - i) * per - 1, 0), col))

    return pl.pallas_call(
        body, name=name, grid=(nb,),
        in_specs=[blk(2), blk(3), blk(4), halo_blk(4), blk(0), blk(0), halo_blk(0),
                  tap, sq_spec, vec, sq_spec, vec, vec],
        out_specs=[pl.BlockSpec((tb, 2 * D_GROUP), lambda i: (nb - 1 - i, 0)),
                   pl.BlockSpec((1, 2 * D_GROUP), lambda i: (0, 0)),
                   tap, vec, sq_spec, vec, sq_spec, vec, vec],
        out_shape=[jax.ShapeDtypeStruct((s, 2 * D_GROUP), BF16), jax.ShapeDtypeStruct((1, 2 * D_GROUP), F32),
                   jax.ShapeDtypeStruct((hl, D_GROUP), F32), vshape, sshape, vshape, sshape, vshape, vshape],
        scratch_shapes=[pltpu.VMEM((hl + tb, D_GROUP), F32), pltpu.VMEM((tb + hl, D_GROUP), F32),
                        pltpu.VMEM((hl, D_GROUP), F32), pltpu.VMEM((8, D_GROUP), F32), pltpu.VMEM((8, D_GROUP), F32)],
        compiler_params=_cparams(1),
    )(dmix, h_in, h_in, h_in, xcs, hs, hs, cw, wr, br, wi, bi, sp)


ATTN_HEADS = 4
ATTN_HEAD_DIM = 64
ATTN_SCALE = ATTN_HEAD_DIM ** -0.5


def _head_mask(h):
    lane = lax.broadcasted_iota(jnp.int32, (1, D_GROUP), 1)
    return jnp.where((lane >= h * ATTN_HEAD_DIM) & (lane < (h + 1) * ATTN_HEAD_DIM), 1.0, 0.0)


def _softmax_rows(sc):
    e = jnp.exp(sc - jnp.max(sc, -1, keepdims=True))
    return e / jnp.sum(e, -1, keepdims=True)


def attn_fwd(h_in, kv, tb, name):
    s = h_in.shape[0]

    def body(q_ref, kv_ref, y_ref):
        q = q_ref[...]
        kb = _bf(kv_ref[:, 0:D_GROUP])
        vb = _bf(kv_ref[:, D_GROUP:2 * D_GROUP])
        out = jnp.zeros((tb, D_GROUP), F32)
        for h in range(ATTN_HEADS):
            mask = _head_mask(h)
            p = _softmax_rows(_dot_nt(_bf(q * mask), kb) * ATTN_SCALE)
            out = out + _dot(_bf(p), vb) * mask
        y_ref[...] = _bf(out)

    return pl.pallas_call(
        body, name=name, grid=(s // tb,),
        in_specs=[pl.BlockSpec((tb, D_GROUP), lambda i: (i, 5)),
                  pl.BlockSpec((D_GROUP, 2 * D_GROUP), lambda i: (0, 0))],
        out_specs=pl.BlockSpec((tb, D_GROUP), lambda i: (i, 0)),
        out_shape=jax.ShapeDtypeStruct((s, D_GROUP), BF16),
        compiler_params=_cparams(1),
    )(h_in, kv)


def attn_bwd(dmix, h_in, kv, tb, name):
    s = h_in.shape[0]

    def body(do_ref, q_ref, kv_ref, dq_ref, cs_ref, dkv_ref):
        first = pl.program_id(0) == 0
        q = q_ref[...]
        do = do_ref[...]
        kb = _bf(kv_ref[:, 0:D_GROUP])
        vb = _bf(kv_ref[:, D_GROUP:2 * D_GROUP])
        dq = jnp.zeros((tb, D_GROUP), F32)
        dk = jnp.zeros((D_GROUP, D_GROUP), F32)
        dv = jnp.zeros((D_GROUP, D_GROUP), F32)
        for h in range(ATTN_HEADS):
            mask = _head_mask(h)
            qm = _bf(q * mask)
            p = _softmax_rows(_dot_nt(qm, kb) * ATTN_SCALE)
            dom = _bf(do * mask)
            dp = _dot_nt(dom, vb)
            dv = dv + _dot_tn(_bf(p), dom)
            ds = _bf(p * (dp - jnp.sum(dp * p, -1, keepdims=True)) * ATTN_SCALE)
            dq = dq + _dot(ds, kb) * mask
            dk = dk + _dot_tn(ds, qm)
        dq_ref[...] = _bf(dq)
        _acc(cs_ref, _colsum(dq), first)
        _acc(dkv_ref.at[:, 0:D_GROUP], dk, first)
        _acc(dkv_ref.at[:, D_GROUP:2 * D_GROUP], dv, first)

    return pl.pallas_call(
        body, name=name, grid=(s // tb,),
        in_specs=[pl.BlockSpec((tb, D_GROUP), lambda i: (i, 3)),
                  pl.BlockSpec((tb, D_GROUP), lambda i: (i, 5)),
                  pl.BlockSpec((D_GROUP, 2 * D_GROUP), lambda i: (0, 0))],
        out_specs=[pl.BlockSpec((tb, D_GROUP), lambda i: (i, 0)),
                   pl.BlockSpec((1, D_GROUP), lambda i: (0, 0)),
                   pl.BlockSpec((D_GROUP, 2 * D_GROUP), lambda i: (0, 0))],
        out_shape=[jax.ShapeDtypeStruct((s, D_GROUP), BF16), jax.ShapeDtypeStruct((1, D_GROUP), F32),
                   jax.ShapeDtypeStruct((D_GROUP, 2 * D_GROUP), F32)],
        compiler_params=_cparams(1),
    )(dmix, h_in, kv)


FFN_RB = 16
FFN_UNROLL_FWD = 4
FFN_UNROLL_BWD = 2
FFN_TAP_ROWS = 8
FFN_TN = D_FF // 2


def _shift_down(cur, tail, k):
    return pltpu.roll(jnp.concatenate([tail, cur], axis=0), k, 0)[SUBLANES:]


def _shift_up(cur, head, k):
    rb = cur.shape[0]
    return pltpu.roll(jnp.concatenate([cur, head], axis=0), rb + SUBLANES - k, 0)[:rb]


def _fold8(v):
    tot = v[0:SUBLANES]
    for t in range(1, v.shape[0] // SUBLANES):
        tot = tot + v[t * SUBLANES:(t + 1) * SUBLANES]
    return tot


def _strip(r):
    return pl.ds(pl.multiple_of(r * FFN_RB, FFN_RB), FFN_RB)


def ffn_act_fwd(u, cw, cb, tb, name, gather=()):
    s = u.shape[0]
    rb = FFN_RB
    nct = D_FF // FFN_TN
    nstrip = tb // rb
    ng = len(gather)

    def body(uv_ref, ug_ref, wv_ref, wg_ref, bv_ref, bg_ref, *rest):
        hf_ref, keep_ref = rest[ng], rest[ng + 1]
        slabs = rest[ng + 2:2 * ng + 2]
        tailv, tailg = rest[2 * ng + 2], rest[2 * ng + 3]
        sems = rest[2 * ng + 4:]
        if ng:
            @pl.when((pl.program_id(0) == 0) & (pl.program_id(1) == 0))
            def _():
                _gather_start(slabs, sems[0:2])

        @pl.when(pl.program_id(1) == 0)
        def _():
            tailv[...] = jnp.zeros_like(tailv)
            tailg[...] = jnp.zeros_like(tailg)

        for cc in range(FFN_TN // LANES):
            cols = slice(cc * LANES, (cc + 1) * LANES)
            wv = [wv_ref[k:k + 1, cols] for k in range(FFN_CONV_WIDTH)]
            wg = [wg_ref[k:k + 1, cols] for k in range(FFN_CONV_WIDTH)]
            bv, bg = bv_ref[:, cols], bg_ref[:, cols]

            def strip(r, carry):
                tail_v, tail_g = carry
                cur_v, cur_g = uv_ref[_strip(r), cols], ug_ref[_strip(r), cols]
                vc = wv[0] * _shift_down(cur_v, tail_v, 2) + wv[1] * _shift_down(cur_v, tail_v, 1) + wv[2] * cur_v + bv
                gc = wg[0] * _shift_down(cur_g, tail_g, 2) + wg[1] * _shift_down(cur_g, tail_g, 1) + wg[2] * cur_g + bg
                ge, t = _gelu_parts(gc)
                hf_ref[_strip(r), cols] = _bf(vc * ge)
                keep_ref[0, _strip(r), cols] = _bf(vc)
                keep_ref[1, _strip(r), cols] = _bf(ge)
                keep_ref[2, _strip(r), cols] = _bf(_gelu_grad(gc, t))
                return cur_v[rb - SUBLANES:], cur_g[rb - SUBLANES:]

            def strips(q, carry):
                for k in range(FFN_UNROLL_FWD):
                    carry = strip(q * FFN_UNROLL_FWD + k, carry)
                return carry

            last_v, last_g = lax.fori_loop(0, nstrip // FFN_UNROLL_FWD, strips, (tailv[:, cols], tailg[:, cols]))
            tailv[:, cols] = last_v
            tailg[:, cols] = last_g

        if ng:
            @pl.when((pl.program_id(0) == nct - 1) & (pl.program_id(1) == s // tb - 1))
            def _():
                _gather_finish(slabs, sems[0:2], sems[2:4])

    return pl.pallas_call(
        body, name=name, grid=(nct, s // tb),
        in_specs=[pl.BlockSpec((tb, FFN_TN), lambda c, i: (i, c)),
                  pl.BlockSpec((tb, FFN_TN), lambda c, i: (i, c + nct)),
                  pl.BlockSpec((FFN_TAP_ROWS, FFN_TN), lambda c, i: (0, c)),
                  pl.BlockSpec((FFN_TAP_ROWS, FFN_TN), lambda c, i: (0, c + nct)),
                  pl.BlockSpec((1, FFN_TN), lambda c, i: (0, c)),
                  pl.BlockSpec((1, FFN_TN), lambda c, i: (0, c + nct))] + [ANY] * ng,
        out_specs=[pl.BlockSpec((tb, FFN_TN), lambda c, i: (i, c)),
                   pl.BlockSpec((3, tb, FFN_TN), lambda c, i: (0, i, c))] + [ANY] * ng,
        out_shape=[jax.ShapeDtypeStruct((s, D_FF), BF16), jax.ShapeDtypeStruct((3, s, D_FF), BF16)]
        + [jax.ShapeDtypeStruct(a.shape, a.dtype) for a in gather],
        input_output_aliases={6 + k: 2 + k for k in range(ng)},
        scratch_shapes=[pltpu.VMEM((SUBLANES, FFN_TN), F32), pltpu.VMEM((SUBLANES, FFN_TN), F32)]
        + (_gather_sems(ng) if ng else []),
        compiler_params=_cparams(2),
    )(u, u, cw, cw, cb, cb, *gather)


def ffn_act_bwd(dhf, u, kept, cw, tb, name, scatter=()):
    s = u.shape[0]
    rb = FFN_RB
    nct = D_FF // FFN_TN
    nb = s // tb
    nstrip = tb // rb
    ntap = FFN_CONV_WIDTH
    ns = len(scatter)

    def body(dh_ref, uv_ref, ug_ref, kept_ref, wv_ref, wg_ref, *rest):
        s_refs = rest[:ns]
        du_ref, dwv_ref, dwg_ref = rest[ns:ns + 3]
        got_refs = rest[ns + 3:2 * ns + 3]
        headv, headg = rest[2 * ns + 3], rest[2 * ns + 4]
        sems = rest[2 * ns + 5:]
        i = pl.program_id(1)
        first = i == 0
        if ns:
            @pl.when((pl.program_id(0) == 0) & first)
            def _():
                _scatter_start(s_refs, got_refs, sems)

        @pl.when(first)
        def _():
            headv[...] = jnp.zeros_like(headv)
            headg[...] = jnp.zeros_like(headg)
            dwv_ref[...] = jnp.zeros_like(dwv_ref)
            dwg_ref[...] = jnp.zeros_like(dwg_ref)

        zero = jnp.zeros((SUBLANES, LANES), F32)
        for cc in range(FFN_TN // LANES):
            cols = slice(cc * LANES, (cc + 1) * LANES)
            wv = [wv_ref[k:k + 1, cols] for k in range(ntap)]
            wg = [wg_ref[k:k + 1, cols] for k in range(ntap)]

            def strip(ii, carry):
                head_dv, head_dg, acc_v, acc_g = carry
                r = nstrip - 1 - ii
                dh = dh_ref[_strip(r), cols]
                dvc = dh * kept_ref[1, _strip(r), cols].astype(F32)
                dgc = dh * kept_ref[0, _strip(r), cols].astype(F32) * kept_ref[2, _strip(r), cols].astype(F32)
                sdv = [_shift_up(dvc, head_dv, 2), _shift_up(dvc, head_dv, 1), dvc]
                sdg = [_shift_up(dgc, head_dg, 2), _shift_up(dgc, head_dg, 1), dgc]
                cur_v, cur_g = uv_ref[_strip(r), cols], ug_ref[_strip(r), cols]
                acc_v = tuple(acc_v[k] + _fold8(cur_v * sdv[k]) for k in range(ntap)) + (acc_v[ntap] + _fold8(dvc),)
                acc_g = tuple(acc_g[k] + _fold8(cur_g * sdg[k]) for k in range(ntap)) + (acc_g[ntap] + _fold8(dgc),)
                du_v = wv[0] * sdv[0] + wv[1] * sdv[1] + wv[2] * sdv[2]
                du_g = wg[0] * sdg[0] + wg[1] * sdg[1] + wg[2] * sdg[2]
                du_ref[0, _strip(r), cols] = _bf(du_v)
                du_ref[1, _strip(r), cols] = _bf(du_g)
                return dvc[0:SUBLANES], dgc[0:SUBLANES], acc_v, acc_g

            init = (headv[:, cols], headg[:, cols], (zero,) * (ntap + 1), (zero,) * (ntap + 1))
            def strips(q, carry):
                for k in range(FFN_UNROLL_BWD):
                    carry = strip(q * FFN_UNROLL_BWD + k, carry)
                return carry

            top_dv, top_dg, acc_v, acc_g = lax.fori_loop(0, nstrip // FFN_UNROLL_BWD, strips, init)
            headv[:, cols] = top_dv
            headg[:, cols] = top_dg
            for k in range(ntap + 1):
                dwv_ref[k:k + 1, cols] += _colsum(acc_v[k])
                dwg_ref[k:k + 1, cols] += _colsum(acc_g[k])

        if ns:
            @pl.when((pl.program_id(0) == nct - 1) & (i == nb - 1))
            def _():
                _scatter_finish(s_refs, got_refs, sems)

    def blk(shift):
        return pl.BlockSpec((tb, FFN_TN), lambda c, i: (nb - 1 - i, c + shift))

    tapv = pl.BlockSpec((FFN_TAP_ROWS, FFN_TN), lambda c, i: (0, c))
    tapg = pl.BlockSpec((FFN_TAP_ROWS, FFN_TN), lambda c, i: (0, c + nct))
    return pl.pallas_call(
        body, name=name, grid=(nct, nb),
        in_specs=[blk(0), blk(0), blk(nct), pl.BlockSpec((3, tb, FFN_TN), lambda c, i: (0, nb - 1 - i, c)), tapv, tapg]
        + [ANY] * ns,
        out_specs=[pl.BlockSpec((2, tb, FFN_TN), lambda c, i: (0, nb - 1 - i, c)), tapv, tapv] + [ANY] * ns,
        out_shape=[jax.ShapeDtypeStruct((2, s, D_FF), BF16), jax.ShapeDtypeStruct((FFN_TAP_ROWS, D_FF), F32),
                   jax.ShapeDtypeStruct((FFN_TAP_ROWS, D_FF), F32)] + _scatter_shapes(scatter),
        scratch_shapes=[pltpu.VMEM((SUBLANES, FFN_TN), F32), pltpu.VMEM((SUBLANES, FFN_TN), F32)]
        + (_scatter_sems(ns) if ns else []),
        compiler_params=_cparams(2),
    )(dhf, u, u, kept, cw, cw, *scatter)


def _place():
    x, y, c = lax.axis_index("x"), lax.axis_index("y"), lax.axis_index("c")
    return x, y, c, 2 * x + y


def _chip_peer(x, y, d):
    return jnp.bitwise_xor(x, d >> 1), jnp.bitwise_xor(y, d & 1)


def _my_half(ref_rows, c):
    half = ref_rows // 2
    return pl.ds(c * half, half)


def _gather_copy(ref, part, c, sems, k, d, to):
    rows = _my_half(ref.shape[1], c)
    return pltpu.make_async_remote_copy(src_ref=ref.at[part, rows], dst_ref=ref.at[part, rows], send_sem=sems[0].at[k, d - 1],
                                        recv_sem=sems[1].at[k, d - 1], device_id=to, device_id_type=MESH)


def _gather_start(slabs, ici_sems):
    x, y, c, j = _place()
    for k, ref in enumerate(slabs):
        for d in (1, 2, 3):
            px, py = _chip_peer(x, y, d)
            _gather_copy(ref, j, c, ici_sems, k, d, (px, py, c)).start()


def _gather_finish(slabs, ici_sems, d2d_sems):
    x, y, c, j = _place()
    sib = (x, y, 1 - c)
    passed = []
    for d in (1, 2, 3):
        jd = jnp.bitwise_xor(j, d)
        for k, ref in enumerate(slabs):
            _gather_copy(ref, jd, c, ici_sems, k, d, sib).wait_recv()
            cp = _gather_copy(ref, jd, c, d2d_sems, k, d, sib)
            cp.start()
            passed.append(cp)
    for cp in passed:
        cp.wait_recv()
        cp.wait_send()
    for k, ref in enumerate(slabs):
        for d in (1, 2, 3):
            _gather_copy(ref, j, c, ici_sems, k, d, sib).wait_send()


def _gather_sems(n):
    return [pltpu.SemaphoreType.DMA((n, 3)) for _ in range(4)]


def gather_weights(slabs):
    n = len(slabs)

    def body(*refs):
        outs = refs[n:2 * n]
        sems = refs[2 * n:]
        _gather_start(outs, sems[0:2])
        _gather_finish(outs, sems[0:2], sems[2:4])

    return pl.pallas_call(
        body, name="gather_weights", in_specs=[ANY] * n, out_specs=[ANY] * n,
        out_shape=[jax.ShapeDtypeStruct(a.shape, a.dtype) for a in slabs],
        input_output_aliases={w: w for w in range(n)}, scratch_shapes=_gather_sems(n),
    )(*slabs)


def _own_slab(part, jidx):
    slab = jnp.zeros((N_CHIPS,) + part.shape, part.dtype)
    return lax.dynamic_update_slice_in_dim(slab, part[None], jidx, axis=0)


def exchange_cores(gbig, name, small_all=None):
    n = len(gbig)
    with_small = small_all is not None

    def body(*refs):
        g_refs = refs[:n]
        got_refs = refs[n + with_small:2 * n + with_small]
        dsem, esem, ssem, rsem, fsem, hsem = refs[2 * (n + with_small):]
        x, y, c, j = _place()
        sib = (x, y, 1 - c)
        big = []
        for k in range(n):
            half = g_refs[k].shape[1] // 2
            cp = pltpu.make_async_remote_copy(
                src_ref=g_refs[k].at[:, pl.ds((1 - c) * half, half)], dst_ref=got_refs[k], send_sem=dsem.at[k],
                recv_sem=esem.at[k], device_id=sib, device_id_type=MESH)
            cp.start()
            big.append(cp)
        if with_small:
            all_ref = refs[2 * n + 1]
            me = 4 * x + 2 * y + c

            def small_copy(k, block, to, sems):
                return pltpu.make_async_remote_copy(
                    src_ref=all_ref.at[block], dst_ref=all_ref.at[block],
                    send_sem=sems[0].at[k], recv_sem=sems[1].at[k], device_id=to, device_id_type=MESH)

            first = [small_copy(0, me, sib, (ssem, rsem))]
            for d in (1, 2, 3):
                px, py = _chip_peer(x, y, d)
                first.append(small_copy(d, me, (px, py, c), (ssem, rsem)))
            for cp in first:
                cp.start()
            passed = []
            for d in (1, 2, 3):
                px, py = _chip_peer(x, y, d)
                src_block = 4 * px + 2 * py + c
                small_copy(d, src_block, sib, (ssem, rsem)).wait_recv()
                cp = small_copy(d - 1, src_block, sib, (fsem, hsem))
                cp.start()
                passed.append(cp)
            small_copy(0, me, sib, (ssem, rsem)).wait_recv()
            for cp in passed:
                cp.wait_recv()
            for cp in first + passed:
                cp.wait_send()
        for cp in big:
            cp.wait()

    ops = list(gbig) + ([small_all] if with_small else [])
    out_shape = [jax.ShapeDtypeStruct((g.shape[0], g.shape[1] // 2, g.shape[2]), F32) for g in gbig]
    aliases = {}
    if with_small:
        out_shape.append(jax.ShapeDtypeStruct(small_all.shape, F32))
        aliases = {n: n}
    return pl.pallas_call(
        body, name=name, in_specs=[ANY] * len(ops), out_specs=[ANY] * len(out_shape), out_shape=out_shape,
        input_output_aliases=aliases,
        scratch_shapes=[pltpu.SemaphoreType.DMA((n,)), pltpu.SemaphoreType.DMA((n,)),
                        pltpu.SemaphoreType.DMA((4,)), pltpu.SemaphoreType.DMA((4,)),
                        pltpu.SemaphoreType.DMA((3,)), pltpu.SemaphoreType.DMA((3,))],
    )(*ops)


def _scatter_copy(s_ref, got_ref, k, d, sems):
    x, y, c, j = _place()
    px, py = _chip_peer(x, y, d)
    return pltpu.make_async_remote_copy(
        src_ref=s_ref.at[jnp.bitwise_xor(j, d)], dst_ref=got_ref.at[d - 1], send_sem=sems[0].at[k, d - 1],
        recv_sem=sems[1].at[k, d - 1], device_id=(px, py, c), device_id_type=MESH)


def _scatter_start(s_refs, got_refs, sems):
    for d in (1, 2, 3):
        for k in range(len(s_refs)):
            _scatter_copy(s_refs[k], got_refs[k], k, d, sems).start()


def _scatter_finish(s_refs, got_refs, sems):
    for d in (1, 2, 3):
        for k in range(len(s_refs)):
            _scatter_copy(s_refs[k], got_refs[k], k, d, sems).wait()


def _scatter_sems(n):
    return [pltpu.SemaphoreType.DMA((n, 3)), pltpu.SemaphoreType.DMA((n, 3))]


def _scatter_shapes(s1):
    return [jax.ShapeDtypeStruct((3,) + a.shape[1:], a.dtype) for a in s1]


def scatter_shards(s1, name):
    n = len(s1)

    def body(*refs):
        _scatter_start(refs[:n], refs[n:2 * n], refs[2 * n:])
        _scatter_finish(refs[:n], refs[n:2 * n], refs[2 * n:])

    return pl.pallas_call(
        body, name=name, in_specs=[ANY] * n, out_specs=[ANY] * n, out_shape=_scatter_shapes(s1),
        scratch_shapes=_scatter_sems(n),
    )(*s1)


def share_with_sibling(parts):
    n = len(parts)

    def body(*refs):
        out_refs = refs[n:2 * n]
        ssem, rsem = refs[2 * n:]
        x, y, c, j = _place()
        cps = []
        for k in range(n):
            rows = _my_half(out_refs[k].shape[1], c)
            for l in range(DEPTH):
                cp = pltpu.make_async_remote_copy(
                    src_ref=out_refs[k].at[l, rows], dst_ref=out_refs[k].at[l, rows], send_sem=ssem.at[k, l],
                    recv_sem=rsem.at[k, l], device_id=(x, y, 1 - c), device_id_type=MESH)
                cp.start()
                cps.append(cp)
        for cp in cps:
            cp.wait()

    return pl.pallas_call(
        body, name="share_with_sibling", in_specs=[ANY] * n, out_specs=[ANY] * n,
        out_shape=[jax.ShapeDtypeStruct(a.shape, F32) for a in parts],
        input_output_aliases={k: k for k in range(n)},
        scratch_shapes=[pltpu.SemaphoreType.DMA((n, DEPTH)), pltpu.SemaphoreType.DMA((n, DEPTH))],
    )(*parts)


def add_core_halves(g, got, cidx, name):
    nch, r, cdim = g.shape
    half = r // 2
    tr = _row_tile(half, cdim, mult=16)
    per = half // tr

    def body(c_ref, a_ref, b_ref, o_ref):
        o_ref[...] = _bf(a_ref[...] + b_ref[...])

    grid_spec = pltpu.PrefetchScalarGridSpec(
        num_scalar_prefetch=1, grid=(nch, per),
        in_specs=[pl.BlockSpec((None, tr, cdim), lambda jj, i, c_ref: (jj, c_ref[0] * per + i, 0)),
                  pl.BlockSpec((None, tr, cdim), lambda jj, i, c_ref: (jj, i, 0))],
        out_specs=pl.BlockSpec((None, tr, cdim), lambda jj, i, c_ref: (jj, i, 0)))
    return pl.pallas_call(
        body, name=name, grid_spec=grid_spec,
        out_shape=jax.ShapeDtypeStruct((nch, half, cdim), BF16), compiler_params=_cparams(2),
    )(cidx, g, got)


def add_chip_parts(s1, got, jc, layer, into, name):
    _, half, cdim = s1.shape
    tr = _row_tile(half, cdim, mult=16)
    per = half // tr

    def body(jc_ref, a_ref, g0_ref, g1_ref, g2_ref, *rest):
        rest[-1][...] = ((a_ref[...].astype(F32) + g0_ref[...].astype(F32)) + g1_ref[...].astype(F32)) + g2_ref[...].astype(F32)

    def slot(k):
        return pl.BlockSpec((None, tr, cdim), lambda i, jc_ref: (k, i, 0))

    in_specs = [pl.BlockSpec((None, tr, cdim), lambda i, jc_ref: (jc_ref[0], i, 0)), slot(0), slot(1), slot(2)]
    ops = [jc, s1, got, got, got]
    aliases = {}
    if into is not None:
        in_specs.append(ANY)
        ops.append(into)
        aliases = {5: 0}
    grid_spec = pltpu.PrefetchScalarGridSpec(
        num_scalar_prefetch=1, grid=(per,), in_specs=in_specs,
        out_specs=pl.BlockSpec((None, tr, cdim), lambda i, jc_ref: (layer, jc_ref[1] * per + i, 0)))
    return pl.pallas_call(
        body, name=name, grid_spec=grid_spec, input_output_aliases=aliases,
        out_shape=jax.ShapeDtypeStruct((DEPTH, 2 * half, cdim), F32), compiler_params=_cparams(1),
    )(*ops)


def sum_devices(allp):
    _, r, _ = allp.shape

    def body(a_ref, o_ref):
        tot = a_ref[0]
        for k in range(1, 8):
            tot = tot + a_ref[k]
        o_ref[...] = tot

    tr = r // 2 if r % 16 == 0 else r
    return pl.pallas_call(
        body, name="sum_devices", grid=(r // tr,),
        in_specs=[pl.BlockSpec((8, tr, LANES), lambda i: (0, i, 0))],
        out_specs=pl.BlockSpec((tr, LANES), lambda i: (i, 0)),
        out_shape=jax.ShapeDtypeStruct((r, LANES), F32), compiler_params=_cparams(1),
    )(allp)


def _row_tile(r, cdim, limit_bytes=1 << 20, mult=8):
    best = None
    for tr in range(mult, r + 1, mult):
        if r % tr == 0 and tr * cdim * 4 <= limit_bytes:
            best = tr
    return best if best is not None else r


def adamw(w, g, m, v, name):
    r, cdim = w.shape
    tr = _row_tile(r, cdim)
    bc1 = 1.0 - ADAM_B1 ** ADAM_STEP
    bc2 = 1.0 - ADAM_B2 ** ADAM_STEP

    def body(w_ref, g_ref, m_ref, v_ref, d_ref, nm_ref, nv_ref, go_ref):
        gv = g_ref[...]
        nm = ADAM_B1 * m_ref[...] + (1.0 - ADAM_B1) * gv
        nv = ADAM_B2 * v_ref[...] + (1.0 - ADAM_B2) * (gv * gv)
        d_ref[...] = -ADAM_LR * ((nm / bc1) / (jnp.sqrt(nv / bc2) + ADAM_EPS) + ADAM_WD * w_ref[...])
        nm_ref[...] = nm
        nv_ref[...] = nv
        go_ref[...] = gv

    blk = pl.BlockSpec((tr, cdim), lambda i: (i, 0))
    shape = jax.ShapeDtypeStruct((r, cdim), F32)
    return pl.pallas_call(
        body, name=name, grid=(r // tr,), in_specs=[blk] * 4, out_specs=[blk] * 4, out_shape=[shape] * 4,
        compiler_params=_cparams(1),
    )(w, g, m, v)


def _s5_prepare(lam_re, lam_im, log_dt, b_re, b_im, c_re, c_im):
    groups, ch = 16, 16
    dt = jnp.exp(log_dt)[:, None]
    mag = jnp.exp(lam_re * dt)
    a_r, a_i = mag * jnp.cos(lam_im * dt), mag * jnp.sin(lam_im * dt)
    den = lam_re * lam_re + lam_im * lam_im
    q_r = ((a_r - 1.0) * lam_re + a_i * lam_im) / den
    q_i = (a_i * lam_re - (a_r - 1.0) * lam_im) / den
    bb_r = q_r[..., None] * b_re - q_i[..., None] * b_im
    bb_i = q_r[..., None] * b_im + q_i[..., None] * b_re
    eye = jnp.eye(groups, dtype=F32)

    def expand_b(bb):
        return jnp.einsum("gpc,gh->gchp", bb, eye).reshape(groups * ch, N_STATE)

    def expand_c(cc):
        return jnp.einsum("gcp,gh->hpgc", cc, eye).reshape(N_STATE, groups * ch)

    a2 = jnp.concatenate([a_r.reshape(1, N_STATE), a_i.reshape(1, N_STATE)], axis=1)
    bexp = jnp.concatenate([expand_b(bb_r), expand_b(bb_i)], axis=1)
    cexp = jnp.concatenate([expand_c(c_re), -expand_c(c_im)], axis=0)
    return a2, bexp, cexp


def _lru_prepare(w_r, w_i, lam):
    heads = 4
    eye = jnp.eye(heads, dtype=F32)

    def expand(w):
        return jnp.einsum("hij,hk->hikj", w, eye).reshape(D_GROUP, D_GROUP)

    return expand(w_r), expand(w_i), jax.nn.softplus(-lam).reshape(1, D_GROUP)


def _pad_rows(a, rows):
    return jnp.pad(a, ((0, rows - a.shape[0]), (0, 0)))


def _group_mean_matrix():
    gidx = jnp.arange(D_GROUP) // 64
    return (gidx[:, None] == gidx[None, :]).astype(BF16) * jnp.asarray(1.0 / 64.0, BF16)


def _pack_rows(arrs, width):
    parts = []
    for a in arrs:
        flat = a.reshape(-1)
        pad = (-flat.shape[0]) % width
        parts.append(jnp.pad(flat, (0, pad)) if pad else flat)
    flat = jnp.concatenate(parts)
    rows = flat.shape[0] // width
    pad_rows = (-rows) % 16
    if pad_rows:
        flat = jnp.pad(flat, (0, pad_rows * width))
    return flat.reshape(-1, width)


def _unpack_rows(packed, shapes, width):
    flat = packed.reshape(-1)
    out, off = [], 0
    for shp in shapes:
        size = math.prod(shp)
        out.append(flat[off:off + size].reshape(shp))
        off += size + ((-size) % width)
    return out


def kernel(x, mem, ln_in_g, ln_in_b, w_in, b_in, s5_lam_re, s5_lam_im, s5_log_dt, s5_b_re, s5_b_im, s5_c_re, s5_c_im, s5_d, s5_w_glu, s5_b_glu, cv_w, cv_b, cv_gn_g, cv_gn_b, cv_w_pw, cv_b_pw, lru_conv_w, lru_conv_b, lru_w_r, lru_b_r, lru_w_i, lru_b_i, lru_lam, attn_w_kv, w_out, b_out, ln1_g, ln1_b, ffn_w_up, ffn_conv_w, ffn_conv_b, ffn_w_down, ln2_g, ln2_b, loss_target, m_ln_in_g, m_ln_in_b, m_w_in, m_b_in, m_s5_lam_re, m_s5_lam_im, m_s5_log_dt, m_s5_b_re, m_s5_b_im, m_s5_c_re, m_s5_c_im, m_s5_d, m_s5_w_glu, m_s5_b_glu, m_cv_w, m_cv_b, m_cv_gn_g, m_cv_gn_b, m_cv_w_pw, m_cv_b_pw, m_lru_conv_w, m_lru_conv_b, m_lru_w_r, m_lru_b_r, m_lru_w_i, m_lru_b_i, m_lru_lam, m_attn_w_kv, m_w_out, m_b_out, m_ln1_g, m_ln1_b, m_ffn_w_up, m_ffn_conv_w, m_ffn_conv_b, m_ffn_w_down, m_ln2_g, m_ln2_b, v_ln_in_g, v_ln_in_b, v_w_in, v_b_in, v_s5_lam_re, v_s5_lam_im, v_s5_log_dt, v_s5_b_re, v_s5_b_im, v_s5_c_re, v_s5_c_im, v_s5_d, v_s5_w_glu, v_s5_b_glu, v_cv_w, v_cv_b, v_cv_gn_g, v_cv_gn_b, v_cv_w_pw, v_cv_b_pw, v_lru_conv_w, v_lru_conv_b, v_lru_w_r, v_lru_b_r, v_lru_w_i, v_lru_b_i, v_lru_lam, v_attn_w_kv, v_w_out, v_b_out, v_ln1_g, v_ln1_b, v_ffn_w_up, v_ffn_conv_w, v_ffn_conv_b, v_ffn_w_down, v_ln2_g, v_ln2_b):
    p = dict(locals())
    xs = x[0]
    mems = mem[0]
    target = loss_target[0]
    s = xs.shape[0]
    cidx = lax.axis_index("c")
    jidx = 2 * lax.axis_index("x") + lax.axis_index("y")
    tb_scan = min(256, s)
    tb_s5 = min(512, s)
    tb_attn = min(512, s)
    tb_ffn = min(256, s)

    small_sh_names = list(SMALL_SHARDED)
    small_sh_shapes = [p[nm].shape[1:] for nm in small_sh_names]
    slabs = [[_own_slab(_bf(p[nm][l]), jidx) for nm in BIG]
             + [_own_slab(_pack_rows([p[nm][l] for nm in small_sh_names], LANES), jidx)] for l in range(DEPTH)]
    n_slabs = len(BIG) + 1
    first_needed = FIRST_NEEDED + [len(BIG)]
    arrive_later = [k for k in range(n_slabs) if k not in first_needed]
    gathered = [[None] * n_slabs for _ in range(DEPTH)]
    for k, slab in zip(first_needed, gather_weights([slabs[0][k] for k in first_needed])):
        gathered[0][k] = slab

    def weight_views(gw, which):
        shapes = {0: (1, N_CHIPS, D_MODEL, N_IN // N_CHIPS), 1: (1, 1, D_MODEL, 2 * D_GROUP), 2: (1, 1, D_MODEL, D_MODEL),
                  3: (1, N_CHIPS, D_MODEL, 2 * D_FF // N_CHIPS), 4: (1, 1, D_FF, D_MODEL),
                  5: (1, D_GROUP, D_GROUP), 6: (1, D_GROUP, D_GROUP)}
        views = {BIG[k]: gw[k].reshape(shapes[k]) for k in which if k < len(BIG)}
        if len(BIG) in which:
            per_chip = [_unpack_rows(gw[len(BIG)][jj], small_sh_shapes, LANES) for jj in range(N_CHIPS)]
            for k, nm in enumerate(small_sh_names):
                views[nm] = jnp.concatenate([per_chip[jj][k] for jj in range(N_CHIPS)], axis=SMALL_SHARDED[nm] - 1)
        return views

    pmat = _group_mean_matrix()

    def vec(a):
        return a.reshape(1, -1)

    xh0, rs0, xb0 = ln_fwd(xs, vec(ln_in_g), vec(ln_in_b), "ln_in")
    saved = []
    prev = dict(xh=xh0, rs=rs0, xb=xb0, g=vec(ln_in_g), b=vec(ln_in_b))
    for l in range(DEPTH):
        sv = dict(prev=prev)
        (a2, bexp, cexp), sv['s5_vjp'] = jax.vjp(_s5_prepare, s5_lam_re[l], s5_lam_im[l], s5_log_dt[l],
                                                 s5_b_re[l], s5_b_im[l], s5_c_re[l], s5_c_im[l])
        (wr, wi, sp), sv['lru_vjp'] = jax.vjp(_lru_prepare, lru_w_r[l], lru_w_i[l], lru_lam[l])
        sv.update(a2=a2, bexp=_bf(bexp), cexp=_bf(cexp), wr=_bf(wr), wi=_bf(wi), sp=sp)
        late = arrive_later if l == 0 else []
        gw = sv['gw'] = weight_views(gathered[l], [k for k in range(n_slabs) if k not in late])
        sv['cvw'] = _pad_rows(gw['cv_w'], CV_HALO)
        sv['lcw'] = _pad_rows(gw['lru_conv_w'], LRU_HALO)
        sv['fcw'] = _pad_rows(gw['ffn_conv_w'], FFN_TAP_ROWS)
        sv['w_glu'], sv['w_pw'] = gw['s5_w_glu'], gw['cv_w_pw']
        h_in = mm_nn(prev['xb'], gw['w_in'], 0, vec(b_in[l]), F32, 2048, f"in_proj{l}")
        kv = mm_nn(mems, gw['attn_w_kv'], 0, jnp.zeros((1, 2 * D_GROUP), F32), F32, 256, f"kv_proj{l}")
        y_s5, hst, y0, *got = s5_fwd(h_in, a2, sv['bexp'], sv['cexp'], vec(s5_d[l]), sv['w_glu'], 0, vec(s5_b_glu[l]),
                                     tb_s5, f"s5_fwd{l}", gather=[slabs[l][k] for k in late])
        for k, slab in zip(late, got):
            gathered[l][k] = slab
        gw.update(weight_views(gathered[l], late))
        y_cv, hc = cv_fwd(h_in, sv['cvw'], vec(cv_b[l]), vec(cv_gn_g[l]), vec(cv_gn_b[l]), pmat, sv['w_pw'], 0,
                          vec(cv_b_pw[l]), tb_scan, f"cv_fwd{l}")
        y_lru, xcs, hls = lru_fwd(h_in, sv['lcw'], vec(lru_conv_b[l]), sv['wr'], vec(lru_b_r[l]), sv['wi'],
                                  vec(lru_b_i[l]), sp, tb_scan, f"lru_fwd{l}")
        y_mem = attn_fwd(h_in, kv, tb_attn, f"attn_fwd{l}")
        mix_in = jnp.concatenate([y_s5, y_cv, y_lru, y_mem], axis=1)
        xh1, rs1, xb1 = proj_ln(mix_in, gw['w_out'].reshape(1, D_MODEL, D_MODEL), 0, vec(b_out[l]),
                                prev['xh'], prev['g'], prev['b'], vec(ln1_g[l]), vec(ln1_b[l]), f"out_proj_ln{l}")
        u = mm_nn(xb1, gw['ffn_w_up'], 0, jnp.zeros((1, 2 * D_FF), F32), F32, 1024, f"ffn_up{l}")
        nxt = slabs[l + 1] if l + 1 < DEPTH else ()
        hf, sv['ffn_kept'], *got = ffn_act_fwd(u, sv['fcw'], vec(ffn_conv_b[l]), tb_ffn, f"ffn_act{l}", gather=nxt)
        if nxt:
            gathered[l + 1] = got
        xh2, rs2, xb2 = proj_ln(hf, gw['ffn_w_down'].reshape(1, D_FF, D_MODEL), 0, jnp.zeros((1, D_MODEL), F32),
                                xh1, vec(ln1_g[l]), vec(ln1_b[l]), vec(ln2_g[l]), vec(ln2_b[l]), f"ffn_down_ln{l}")
        sv.update(h_in=h_in, kv=kv, hst=hst, y0=y0, hc=hc, xcs=xcs, hls=hls, mix_in=mix_in,
                  xh1=xh1, rs1=rs1, xb1=xb1, u=u, hf=hf, xh2=xh2, rs2=rs2)
        saved.append(sv)
        prev = dict(xh=xh2, rs=rs2, xb=xb2, g=vec(ln2_g[l]), b=vec(ln2_b[l]))

    grads = {}
    per_layer = {nm: [None] * DEPTH for nm in WEIGHTS if nm not in ('ln_in_g', 'ln_in_b')}
    c1 = cidx.reshape(1).astype(jnp.int32)
    jc = jnp.stack([jidx, cidx]).astype(jnp.int32)
    red_big = [None] * len(BIG)
    pending = None
    below = None

    def chip_parts(gl, which, tag, small_all=None):
        gs = [gl[k].reshape((N_CHIPS,) + p[BIG[k]].shape[1:]) for k in which]
        got = exchange_cores(gs, f"exchange_cores{tag}", small_all)
        parts = [add_core_halves(g, ga, c1, f"add_cores_{BIG[k]}{tag}") for g, ga, k in zip(gs, got, which)]
        return parts, (got[len(which)] if small_all is not None else None)

    def own_sum(which, parts, got, l):
        for k, part, gk in zip(which, parts, got):
            red_big[k] = add_chip_parts(part, gk, jc, l, red_big[k], f"add_chips_{BIG[k]}{l}")

    everything = list(range(len(BIG)))
    ready_early = [2, 3, 4]
    ready_last = [k for k in everything if k not in ready_early]

    for l in reversed(range(DEPTH)):
        sv = saved[l]
        pv = sv['prev']
        gw = sv['gw']
        gl = [None] * len(BIG)
        if l == DEPTH - 1:
            dr2, dg2, db2, sqerr = loss_ln_bwd(target, sv['xh2'], sv['rs2'], vec(ln2_g[l]), vec(ln2_b[l]), "loss_ln2_bwd")
            loss_local = 0.5 / D_MODEL * jnp.sum(sqerr)
        else:
            dr2, dg2, db2 = below
        per_layer['ln2_g'][l], per_layer['ln2_b'][l] = dg2[0], db2[0]
        tm_nt = min(512, s)
        ts_big = min(2048, s)
        whole = lambda a_ref, j: a_ref[...]
        dhf = mm_nt(dr2, (tm_nt, D_MODEL), lambda i: (i, 0), whole, gw['ffn_w_down'], 0, None, F32, 512, s, f"ffn_down_dx{l}")
        gl[4] = mm_tn(sv['hf'], dr2, (min(1024, s), D_MODEL), lambda kt, j, st: (st, 0), 1, D_MODEL, FFN_TN, 1024, s,
                      f"ffn_down_dw{l}")
        waiting = pending[2] if pending is not None else ()
        du, dcwv, dcwg, *got = ffn_act_bwd(dhf, sv['u'], sv['ffn_kept'], sv['fcw'], tb_ffn, f"ffn_act_bwd{l}",
                                           scatter=waiting)
        if pending is not None:
            own_sum(pending[1], pending[2], got, pending[0])
            pending = None
        dcw = jnp.concatenate([dcwv, dcwg], axis=1)
        per_layer['ffn_conv_w'][l] = dcw[0:FFN_CONV_WIDTH]
        per_layer['ffn_conv_b'][l] = dcw[FFN_CONV_WIDTH]
        dr1, dg1, db1, cs1 = mm_nt(du, (2, tm_nt, D_FF), lambda i: (0, i, 0),
                                   lambda a_ref, j: a_ref[j // 2, :, (j % 2) * FFN_TN:(j % 2 + 1) * FFN_TN], gw['ffn_w_up'], 0, dr2,
                                   F32, 512, s, f"ffn_up_dx_ln1_bwd{l}", ln=(sv['xh1'], sv['rs1'], vec(ln1_g[l])))
        gl[3] = mm_tn(sv['xb1'], du, (None, ts_big, FFN_TN), lambda kt, j, st: (j // 2, st, j % 2), N_CHIPS, FFN_TN,
                      D_MODEL, 2048, s, f"ffn_up_dw{l}")
        per_layer['ln1_g'][l], per_layer['ln1_b'][l], per_layer['b_out'][l] = dg1[0], db1[0], cs1[0]
        dmix = mm_nt(dr1, (tm_nt, D_MODEL), lambda i: (i, 0), whole, gw['w_out'], 0, None, F32, 512, s, f"out_proj_dx{l}")
        gl[2] = mm_tn(sv['mix_in'], dr1, (min(1024, s), D_MODEL), lambda kt, j, st: (st, 0), 1, D_MODEL, D_MODEL, 1024, s,
                      f"out_proj_dw{l}")
        h_in = sv['h_in']
        (d_u, cs_u, d_bexp, d_cexp, d_dd, d_wglu, d_bglu, d_a2) = s5_bwd(
            dmix, h_in, sv['y0'], sv['hst'], sv['a2'], sv['bexp'], sv['cexp'], vec(s5_d[l]), sv['w_glu'], 0,
            vec(s5_b_glu[l]), tb_s5, f"s5_bwd{l}")
        early = chip_parts(gl, ready_early, f"{l}a")[0] if l == 0 else ()
        (d_vg, cs_vg, d_cvw, d_cvb, d_gg, d_gb, d_wpw, d_bpw, *got) = cv_bwd(
            dmix, h_in, sv['hc'], sv['cvw'], vec(cv_gn_g[l]), vec(cv_gn_b[l]), pmat, sv['w_pw'], 0, tb_scan, f"cv_bwd{l}",
            scatter=early)
        if l == 0:
            own_sum(ready_early, early, got, l)
        (d_lx, cs_lx, d_lcw, d_lcb, d_wr, d_br, d_wi, d_bi, d_sp) = lru_bwd(
            dmix, h_in, sv['xcs'], sv['hls'], sv['lcw'], sv['wr'], vec(lru_b_r[l]), sv['wi'], vec(lru_b_i[l]),
            sv['sp'], tb_scan, f"lru_bwd{l}")
        d_q, cs_q, d_kv = attn_bwd(dmix, h_in, sv['kv'], tb_attn, f"attn_bwd{l}")
        g_s5 = sv['s5_vjp']((d_a2, d_bexp, d_cexp))
        for nm, gval in zip(['s5_lam_re', 's5_lam_im', 's5_log_dt', 's5_b_re', 's5_b_im', 's5_c_re', 's5_c_im'], g_s5):
            per_layer[nm][l] = gval
        g_lru = sv['lru_vjp']((d_wr, d_wi, d_sp))
        for nm, gval in zip(['lru_w_r', 'lru_w_i', 'lru_lam'], g_lru):
            per_layer[nm][l] = gval
        per_layer['s5_d'][l], per_layer['s5_b_glu'][l] = d_dd[0], d_bglu[0]
        per_layer['cv_w'][l], per_layer['cv_b'][l] = d_cvw[0:CONV_WIDTH], d_cvb[0]
        per_layer['cv_gn_g'][l], per_layer['cv_gn_b'][l] = d_gg[0], d_gb[0]
        per_layer['cv_b_pw'][l] = d_bpw[0]
        gl[5], gl[6] = d_wglu, d_wpw
        per_layer['lru_conv_w'][l], per_layer['lru_conv_b'][l] = d_lcw[0:LRU_CONV_WIDTH], d_lcb[0]
        per_layer['lru_b_r'][l], per_layer['lru_b_i'][l] = d_br[0], d_bi[0]
        per_layer['b_in'][l] = jnp.concatenate([cs_u, cs_vg, cs_lx, cs_q], axis=1)[0]
        gl[1] = mm_tn(mems, d_kv, (MEM_ROWS, 2 * D_GROUP), lambda kt, j, st: (st, 0), 1, 2 * D_GROUP, D_MODEL, MEM_ROWS,
                      MEM_ROWS, f"kv_proj_dw{l}")
        dh_in = jnp.concatenate([d_u, d_vg, d_lx, d_q], axis=1)
        n_sh = N_IN // N_CHIPS
        below = mm_nt(dh_in, (tm_nt, N_IN), lambda i: (i, 0), lambda a_ref, j: a_ref[:, j * n_sh:(j + 1) * n_sh],
                      gw['w_in'], 0, dr1, F32, 512, s, f"in_proj_dx_ln_bwd{l}", ln=(pv['xh'], pv['rs'], pv['g']))[:3]
        gl[0] = mm_tn(pv['xb'], dh_in, (ts_big, n_sh), lambda kt, j, st: (st, j), N_CHIPS, n_sh, D_MODEL, 2048, s,
                      f"in_proj_dw{l}")
        if l > 0:
            pending = (l, everything, chip_parts(gl, everything, str(l))[0])
    grad_x, dg_in, db_in = below
    grads['ln_in_g'], grads['ln_in_b'] = dg_in[0], db_in[0]
    for nm, vals in per_layer.items():
        if nm not in BIG:
            grads[nm] = jnp.stack(vals)

    small_names = [nm for nm in WEIGHTS if nm not in BIG]
    small_local = _pack_rows([grads[nm] for nm in small_names], LANES)
    me = 4 * lax.axis_index("x") + 2 * lax.axis_index("y") + cidx
    small_slab = lax.dynamic_update_slice_in_dim(jnp.zeros((8,) + small_local.shape, F32), small_local[None], me, axis=0)
    parts, small_all = chip_parts(gl, ready_last, "0b", small_slab)
    own_sum(ready_last, parts, scatter_shards(parts, "scatter_shards0"), 0)
    red_big = share_with_sibling(red_big)
    small_red = sum_devices(small_all)
    small_grads = dict(zip(small_names, _unpack_rows(small_red, [grads[nm].shape for nm in small_names], LANES)))

    out_g, out_d, out_m, out_v = {}, {}, {}, {}
    for k, nm in enumerate(BIG):
        gk = red_big[k]
        two_d = (-1, p[nm].shape[-1])
        res = adamw(p[nm].reshape(two_d), gk.reshape(two_d), p['m_' + nm].reshape(two_d),
                    p['v_' + nm].reshape(two_d), f"adamw_{nm}")
        out_d[nm], out_m[nm], out_v[nm], out_g[nm] = (t.reshape(p[nm].shape) for t in res)
    own = {}
    for nm in small_names:
        gfull = small_grads[nm]
        if nm in SMALL_SHARDED:
            ax = SMALL_SHARDED[nm]
            width = p[nm].shape[ax]
            gfull = lax.dynamic_slice_in_dim(gfull, jidx * width, width, axis=ax)
        own[nm] = gfull
    packs = [_pack_rows([src[nm] for nm in small_names], LANES)
             for src in (dict((nm, p[nm]) for nm in small_names), own,
                         dict((nm, p['m_' + nm]) for nm in small_names), dict((nm, p['v_' + nm]) for nm in small_names))]
    dlt, nm_, nv_, _ = adamw(packs[0], packs[1], packs[2], packs[3], "adamw_small")
    shapes = [p[nm].shape for nm in small_names]
    for dst, packed in ((out_d, dlt), (out_m, nm_), (out_v, nv_)):
        dst.update(zip(small_names, _unpack_rows(packed, shapes, LANES)))
    out_g.update(own)

    loss = lax.psum(loss_local, ("x", "y", "c"))
    return (loss, grad_x[None], *[out_g[nm] for nm in WEIGHTS], *[out_d[nm] for nm in WEIGHTS],
            *[out_m[nm] for nm in WEIGHTS], *[out_v[nm] for nm in WEIGHTS])
```

```python
import functools
import math

import jax
import jax.numpy as jnp
from jax import lax
from jax.experimental import pallas as pl
from jax.experimental.pallas import tpu as pltpu

F32 = jnp.float32
BF16 = jnp.bfloat16
MESH = pl.DeviceIdType.MESH
ANY = pl.BlockSpec(memory_space=pl.ANY)

DEPTH = 2
D_MODEL = 1024
D_GROUP = 256
N_IN = 6 * D_GROUP
D_FF = 2816
N_STATE = 1024
CONV_WIDTH = 31
LRU_CONV_WIDTH = 4
FFN_CONV_WIDTH = 3
LRU_C = 8.0
ALPHA = (2 * DEPTH) ** 0.25
LN_EPS = 1e-5
N_CHIPS = 4
MEM_ROWS = 256
LANES = 128
SUBLANES = 8
VMEM_LIMIT = 56 * 1024 * 1024

ADAM_LR, ADAM_B1, ADAM_B2, ADAM_EPS, ADAM_WD, ADAM_STEP = 0.001, 0.9, 0.999, 1e-08, 0.01, 10

WEIGHTS = ['ln_in_g', 'ln_in_b', 'w_in', 'b_in', 's5_lam_re', 's5_lam_im', 's5_log_dt', 's5_b_re', 's5_b_im',
           's5_c_re', 's5_c_im', 's5_d', 's5_w_glu', 's5_b_glu', 'cv_w', 'cv_b', 'cv_gn_g', 'cv_gn_b', 'cv_w_pw',
           'cv_b_pw', 'lru_conv_w', 'lru_conv_b', 'lru_w_r', 'lru_b_r', 'lru_w_i', 'lru_b_i', 'lru_lam',
           'attn_w_kv', 'w_out', 'b_out', 'ln1_g', 'ln1_b', 'ffn_w_up', 'ffn_conv_w', 'ffn_conv_b', 'ffn_w_down',
           'ln2_g', 'ln2_b']
BIG = ['w_in', 'attn_w_kv', 'w_out', 'ffn_w_up', 'ffn_w_down', 's5_w_glu', 'cv_w_pw']
FIRST_NEEDED = [0, 1, 5, 6]
SMALL_SHARDED = {'cv_w': 2, 'lru_conv_w': 2, 'ffn_conv_w': 2}


def _cparams(n_axes):
    return pltpu.CompilerParams(dimension_semantics=("arbitrary",) * n_axes, vmem_limit_bytes=VMEM_LIMIT)


def _dot(a, b):
    return jnp.dot(a, b, preferred_element_type=F32)


def _dot_nt(a, b):
    return lax.dot_general(a, b, (((1,), (1,)), ((), ())), preferred_element_type=F32)


def _dot_tn(a, b):
    return lax.dot_general(a, b, (((0,), (0,)), ((), ())), preferred_element_type=F32)


def _bf(v):
    return v.astype(BF16)


def _colsum(v):
    return jnp.sum(v, axis=0, keepdims=True)


def _dot3(v, p):
    hi = _bf(v)
    r1 = v - hi.astype(F32)
    mid = _bf(r1)
    lo = _bf(r1 - mid.astype(F32))
    return _dot(hi, p) + _dot(mid, p) + _dot(lo, p)


_GELU_C = math.sqrt(2.0 / math.pi)


_GELU_C3 = _GELU_C * 0.044715


def _gelu_parts(v):
    t = jnp.tanh(v * (_GELU_C + _GELU_C3 * (v * v)))
    hv = 0.5 * v
    return hv + hv * t, t


def _gelu(v):
    return _gelu_parts(v)[0]


def _gelu_grad(v, t):
    return (0.5 + 0.5 * t) + (0.5 * v) * (1.0 - t * t) * (_GELU_C + (3.0 * _GELU_C3) * (v * v))


def _sigmoid(v):
    return 1.0 / (1.0 + jnp.exp(-v))


def _acc(ref, val, first):
    @pl.when(first)
    def _():
        ref[...] = val

    @pl.when(jnp.logical_not(first))
    def _():
        ref[...] += val


def _rows(shape):
    return lax.broadcasted_iota(jnp.int32, shape, 0)


def _ln_rows(r):
    mu = jnp.mean(r, -1, keepdims=True)
    rc = r - mu
    var = jnp.mean(rc * rc, -1, keepdims=True)
    rs = lax.rsqrt(var + LN_EPS)
    return rc * rs, rs


def ln_fwd(x, g, b, name):
    s = x.shape[0]
    tm = min(512, s)

    def body(x_ref, g_ref, b_ref, xh_ref, rs_ref, xb_ref):
        xh, rs = _ln_rows(x_ref[...])
        xh_ref[...] = xh
        rs_ref[...] = rs
        xb_ref[...] = _bf(xh * g_ref[...] + b_ref[...])

    row = pl.BlockSpec((tm, D_MODEL), lambda i: (i, 0))
    vec = pl.BlockSpec((1, D_MODEL), lambda i: (0, 0))
    return pl.pallas_call(
        body, name=name, grid=(s // tm,),
        in_specs=[row, vec, vec],
        out_specs=[row, pl.BlockSpec((tm, 1), lambda i: (i, 0)), row],
        out_shape=[jax.ShapeDtypeStruct((s, D_MODEL), F32), jax.ShapeDtypeStruct((s, 1), F32),
                   jax.ShapeDtypeStruct((s, D_MODEL), BF16)],
        compiler_params=_cparams(1),
    )(x, g, b)


def proj_ln(a, w, layer, bias, xh_prev, g_prev, b_prev, g, b, name):
    s, k = a.shape
    tm = min(512, s)

    def body(a_ref, w_ref, bias_ref, xp_ref, gp_ref, bp_ref, g_ref, b_ref, xh_ref, rs_ref, xb_ref):
        acc = _dot(a_ref[...], w_ref[...]) + bias_ref[...]
        r = ALPHA * (xp_ref[...] * gp_ref[...] + bp_ref[...]) + acc
        xh, rs = _ln_rows(r)
        xh_ref[...] = xh
        rs_ref[...] = rs
        xb_ref[...] = _bf(xh * g_ref[...] + b_ref[...])

    row = pl.BlockSpec((tm, D_MODEL), lambda i: (i, 0))
    vec = pl.BlockSpec((1, D_MODEL), lambda i: (0, 0))
    return pl.pallas_call(
        body, name=name, grid=(s // tm,),
        in_specs=[pl.BlockSpec((tm, k), lambda i: (i, 0)),
                  pl.BlockSpec((None, k, D_MODEL), lambda i: (layer, 0, 0)),
                  vec, row, vec, vec, vec, vec],
        out_specs=[row, pl.BlockSpec((tm, 1), lambda i: (i, 0)), row],
        out_shape=[jax.ShapeDtypeStruct((s, D_MODEL), F32), jax.ShapeDtypeStruct((s, 1), F32),
                   jax.ShapeDtypeStruct((s, D_MODEL), BF16)],
        compiler_params=_cparams(1),
    )(a, w, bias, xh_prev, g_prev, b_prev, g, b)


def loss_ln_bwd(target, xh, rs, g, b, name):
    s = xh.shape[0]
    tm = min(512, s)

    def body(t_ref, xh_ref, rs_ref, g_ref, b_ref, dr_ref, dg_ref, db_ref, sq_ref):
        first = pl.program_id(0) == 0
        xhv = xh_ref[...]
        err = xhv * g_ref[...] + b_ref[...] - t_ref[...]
        dyv = err * (1.0 / D_MODEL)
        dxh = dyv * g_ref[...]
        dr = rs_ref[...] * (dxh - jnp.mean(dxh, -1, keepdims=True) - xhv * jnp.mean(dxh * xhv, -1, keepdims=True))
        dr_ref[...] = dr
        _acc(dg_ref, _colsum(dyv * xhv), first)
        _acc(db_ref, _colsum(dyv), first)
        _acc(sq_ref, _colsum(err * err), first)

    row = pl.BlockSpec((tm, D_MODEL), lambda i: (i, 0))
    vec = pl.BlockSpec((1, D_MODEL), lambda i: (0, 0))
    vshape = jax.ShapeDtypeStruct((1, D_MODEL), F32)
    return pl.pallas_call(
        body, name=name, grid=(s // tm,),
        in_specs=[row, row, pl.BlockSpec((tm, 1), lambda i: (i, 0)), vec, vec],
        out_specs=[row, vec, vec, vec],
        out_shape=[jax.ShapeDtypeStruct((s, D_MODEL), F32), vshape, vshape, vshape],
        compiler_params=_cparams(1),
    )(target, xh, rs, g, b)


def mm_nn(a, w, layer, bias, out_dtype, tm, name):
    m, k = a.shape
    _, nj, _, n = w.shape
    tm = min(tm, m)

    def body(a_ref, w_ref, b_ref, o_ref):
        o_ref[...] = (_dot(_bf(a_ref[...]), w_ref[...]) + b_ref[...]).astype(out_dtype)

    return pl.pallas_call(
        body, name=name, grid=(nj, m // tm),
        in_specs=[pl.BlockSpec((tm, k), lambda j, i: (i, 0)),
                  pl.BlockSpec((None, None, k, n), lambda j, i: (layer, j, 0, 0)),
                  pl.BlockSpec((1, n), lambda j, i: (0, j))],
        out_specs=pl.BlockSpec((tm, n), lambda j, i: (i, j)),
        out_shape=jax.ShapeDtypeStruct((m, nj * n), out_dtype),
        compiler_params=_cparams(2),
    )(a, w, bias)


def mm_nt(a, a_block, a_map, pick, w, layer, add, out_dtype, tm, m, name, ln=None):
    _, nj, r, n = w.shape
    tm = min(tm, m)
    has_add = add is not None
    n_in = 2 + has_add + (3 if ln is not None else 0)

    def body(*refs):
        a_ref, w_ref = refs[0], refs[1]
        res = _dot_nt(_bf(pick(a_ref, 0)), w_ref[0])
        for j in range(1, nj):
            res = res + _dot_nt(_bf(pick(a_ref, j)), w_ref[j])
        if has_add:
            res = res + ALPHA * refs[2][...]
        if ln is None:
            refs[n_in][...] = res.astype(out_dtype)
            return
        xh_ref, rs_ref, g_ref = refs[n_in - 3:n_in]
        dr_ref, dg_ref, db_ref, cs_ref = refs[n_in:]
        first = pl.program_id(0) == 0
        xhv = xh_ref[...]
        dxh = res * g_ref[...]
        dr = rs_ref[...] * (dxh - jnp.mean(dxh, -1, keepdims=True) - xhv * jnp.mean(dxh * xhv, -1, keepdims=True))
        dr_ref[...] = dr
        _acc(dg_ref, _colsum(res * xhv), first)
        _acc(db_ref, _colsum(res), first)
        _acc(cs_ref, _colsum(dr), first)

    row = pl.BlockSpec((tm, r), lambda i: (i, 0))
    in_specs = [pl.BlockSpec(a_block, a_map),
                pl.BlockSpec((None, nj, r, n), lambda i: (layer, 0, 0, 0))]
    ops = [a, w]
    if has_add:
        in_specs.append(row)
        ops.append(add)
    if ln is None:
        out_specs, out_shape = row, jax.ShapeDtypeStruct((m, r), out_dtype)
    else:
        vec = pl.BlockSpec((1, r), lambda i: (0, 0))
        vshape = jax.ShapeDtypeStruct((1, r), F32)
        in_specs += [row, pl.BlockSpec((tm, 1), lambda i: (i, 0)), vec]
        ops += list(ln)
        out_specs, out_shape = [row, vec, vec, vec], [jax.ShapeDtypeStruct((m, r), F32), vshape, vshape, vshape]
    return pl.pallas_call(
        body, name=name, grid=(m // tm,),
        in_specs=in_specs, out_specs=out_specs, out_shape=out_shape,
        compiler_params=_cparams(1),
    )(*ops)


def mm_tn(a, b, b_block, b_map, nj, n, tk, ts, s, name):
    kx = a.shape[1]
    ts = min(ts, s)

    def body(a_ref, b_ref, o_ref):
        part = _dot_tn(_bf(a_ref[...]), _bf(b_ref[...]))
        _acc(o_ref, part, pl.program_id(2) == 0)

    return pl.pallas_call(
        body, name=name, grid=(kx // tk, nj, s // ts),
        in_specs=[pl.BlockSpec((ts, tk), lambda kt, j, st: (st, kt)), pl.BlockSpec(b_block, b_map)],
        out_specs=pl.BlockSpec((None, tk, n), lambda kt, j, st: (j, kt, 0)),
        out_shape=jax.ShapeDtypeStruct((nj, kx, n), F32),
        compiler_params=_cparams(3),
    )(a, b)


S5_TAB_ROWS = 8 * SUBLANES


def _s5_scan_table(tab_ref, ar, ai, reverse):
    n = N_STATE
    row = _rows((SUBLANES, n))
    edge = SUBLANES - 1 if reverse else 0
    tab_ref[0:8, :] = jnp.where(row == edge, ar, 0.0)
    tab_ref[8:16, :] = jnp.where(row == edge, ai, 0.0)
    pr, pi = ar, ai
    for step, k in enumerate((1, 2, 4)):
        mask = row < SUBLANES - k if reverse else row >= k
        tab_ref[16 + 16 * step:24 + 16 * step, :] = jnp.where(mask, pr, 0.0)
        tab_ref[24 + 16 * step:32 + 16 * step, :] = jnp.where(mask, pi, 0.0)
        pr, pi = pr * pr - pi * pi, 2.0 * pr * pi


def _s5_scan(src_ref, dst_ref, tab_ref, edge_ref, tb, reverse, per_tile=None):
    n = N_STATE
    ng = tb // SUBLANES
    nq = n // LANES
    link = SUBLANES - 1 if reverse else 1

    def tile(ii, carry):
        g = ng - 1 - ii if reverse else ii
        rows = pl.ds(pl.multiple_of(g * SUBLANES, SUBLANES), SUBLANES)
        out = []
        for q in range(nq):
            cre = slice(q * LANES, (q + 1) * LANES)
            cim = slice(n + q * LANES, n + (q + 1) * LANES)
            lr, li = src_ref[rows, cre], src_ref[rows, cim]
            tr, ti = pltpu.roll(carry[2 * q], link, 0), pltpu.roll(carry[2 * q + 1], link, 0)
            kr, ki = tab_ref[0:8, cre], tab_ref[8:16, cre]
            lr, li = lr + kr * tr - ki * ti, li + kr * ti + ki * tr
            for step, k in enumerate((1, 2, 4)):
                amt = SUBLANES - k if reverse else k
                kr, ki = tab_ref[16 + 16 * step:24 + 16 * step, cre], tab_ref[24 + 16 * step:32 + 16 * step, cre]
                sr, si = pltpu.roll(lr, amt, 0), pltpu.roll(li, amt, 0)
                lr, li = lr + kr * sr - ki * si, li + kr * si + ki * sr
            dst_ref[rows, cre] = lr
            dst_ref[rows, cim] = li
            if per_tile is not None:
                per_tile(g, q, (cre, cim), lr, li)
            out += [lr, li]
        return tuple(out)

    init = []
    for q in range(nq):
        init += [edge_ref[:, q * LANES:(q + 1) * LANES], edge_ref[:, n + q * LANES:n + (q + 1) * LANES]]
    fin = lax.fori_loop(0, ng, tile, tuple(init))
    for q in range(nq):
        edge_ref[:, q * LANES:(q + 1) * LANES] = fin[2 * q]
        edge_ref[:, n + q * LANES:n + (q + 1) * LANES] = fin[2 * q + 1]


def s5_fwd(h_in, a2, bexp, cexp, dskip, wglu, layer, bglu, tb, name, gather=()):
    s = h_in.shape[0]
    n = N_STATE
    ng = len(gather)

    def body(u_ref, a_ref, b_ref, c_ref, d_ref, w_ref, bg_ref, *rest):
        y_ref, h_ref, y0_ref = rest[ng:ng + 3]
        slabs = rest[ng + 3:2 * ng + 3]
        edge, tab, bu_ref = rest[2 * ng + 3:2 * ng + 6]
        sems = rest[2 * ng + 6:]

        @pl.when(pl.program_id(0) == 0)
        def _():
            if ng:
                _gather_start(slabs, sems[0:2])
            edge[...] = jnp.zeros_like(edge)
            _s5_scan_table(tab, a_ref[0:1, 0:n], a_ref[0:1, n:2 * n], False)

        u = u_ref[...]
        bu_ref[...] = _dot(_bf(u), b_ref[...])
        _s5_scan(bu_ref, h_ref, tab, edge, tb, False)
        y0 = _dot(_bf(h_ref[:, 0:n]), c_ref[0:n, :]) + _dot(_bf(h_ref[:, n:2 * n]), c_ref[n:2 * n, :]) + d_ref[...] * u
        y0_ref[...] = y0
        yg = _gelu(y0)
        z = _dot(_bf(yg), w_ref[...]) + bg_ref[...]
        y_ref[...] = _bf(yg * _sigmoid(z))
        if ng:
            @pl.when(pl.program_id(0) == s // tb - 1)
            def _():
                _gather_finish(slabs, sems[0:2], sems[2:4])

    vec = pl.BlockSpec((1, D_GROUP), lambda i: (0, 0))
    return pl.pallas_call(
        body, name=name, grid=(s // tb,),
        in_specs=[pl.BlockSpec((tb, D_GROUP), lambda i: (i, 0)),
                  pl.BlockSpec((1, 2 * n), lambda i: (0, 0)),
                  pl.BlockSpec((D_GROUP, 2 * n), lambda i: (0, 0)),
                  pl.BlockSpec((2 * n, D_GROUP), lambda i: (0, 0)),
                  vec,
                  pl.BlockSpec((None, D_GROUP, D_GROUP), lambda i: (layer, 0, 0)),
                  vec] + [ANY] * ng,
        out_specs=[pl.BlockSpec((tb, D_GROUP), lambda i: (i, 0)),
                   pl.BlockSpec((tb, 2 * n), lambda i: (i, 0)),
                   pl.BlockSpec((tb, D_GROUP), lambda i: (i, 0))] + [ANY] * ng,
        out_shape=[jax.ShapeDtypeStruct((s, D_GROUP), BF16), jax.ShapeDtypeStruct((s, 2 * n), F32),
                   jax.ShapeDtypeStruct((s, D_GROUP), F32)] + [jax.ShapeDtypeStruct(a.shape, a.dtype) for a in gather],
        input_output_aliases={7 + k: 3 + k for k in range(ng)},
        scratch_shapes=[pltpu.VMEM((SUBLANES, 2 * n), F32), pltpu.VMEM((S5_TAB_ROWS, n), F32),
                        pltpu.VMEM((tb, 2 * n), F32)] + (_gather_sems(ng) if ng else []),
        compiler_params=_cparams(1),
    )(h_in, a2, bexp, cexp, dskip, wglu, bglu, *gather)


def s5_bwd(dmix, h_in, y0, hst, a2, bexp, cexp, dskip, wglu, layer, bglu, tb, name):
    s = h_in.shape[0]
    n = N_STATE
    nb = s // tb
    halo = tb // 8

    def body(dy_ref, u_ref, y0_ref, h_ref, hp_ref, a_ref, b_ref, c_ref, d_ref, w_ref, bg_ref,
             du_ref, cs_ref, db_ref, dc_ref, dd_ref, dw_ref, dbg_ref, da_ref, edge, tab, g_ref, da_acc):
        i = pl.program_id(0)
        first = i == 0

        @pl.when(first)
        def _():
            edge[...] = jnp.zeros_like(edge)
            da_acc[...] = jnp.zeros_like(da_acc)
            _s5_scan_table(tab, a_ref[0:1, 0:n], -a_ref[0:1, n:2 * n], True)

        dy = dy_ref[...]
        u = u_ref[...]
        y0v = y0_ref[...]
        yg, t = _gelu_parts(y0v)
        z = _dot(_bf(yg), w_ref[...]) + bg_ref[...]
        sg = _sigmoid(z)
        dz = dy * yg * sg * (1.0 - sg)
        dyg = dy * sg + _dot_nt(_bf(dz), w_ref[...])
        _acc(dw_ref, _dot_tn(_bf(yg), _bf(dz)), first)
        _acc(dbg_ref, _colsum(dz), first)
        dy0 = dyg * _gelu_grad(y0v, t)
        _acc(dd_ref, _colsum(dy0 * u), first)
        dy0b = _bf(dy0)
        _acc(dc_ref.at[0:n, :], _dot_tn(_bf(h_ref[:, 0:n]), dy0b), first)
        _acc(dc_ref.at[n:2 * n, :], _dot_tn(_bf(h_ref[:, n:2 * n]), dy0b), first)
        g_ref[...] = _dot_nt(dy0b, c_ref[...])
        keep = jnp.where(i == nb - 1, 0.0, 1.0)
        row0 = _rows((SUBLANES, LANES)) == 0

        def grad_a(g, q, cols, gr, gi):
            cre, cim = cols
            rows = pl.ds(pl.multiple_of(g * SUBLANES, SUBLANES), SUBLANES)
            before = pl.ds(pl.multiple_of(jnp.maximum(g - 1, 0) * SUBLANES, SUBLANES), SUBLANES)
            pre = jnp.where(g == 0, hp_ref[:, cre] * keep, h_ref[before, cre])
            pim = jnp.where(g == 0, hp_ref[:, cim] * keep, h_ref[before, cim])
            pr = jnp.where(row0, pltpu.roll(pre, 1, 0), pltpu.roll(h_ref[rows, cre], 1, 0))
            pi = jnp.where(row0, pltpu.roll(pim, 1, 0), pltpu.roll(h_ref[rows, cim], 1, 0))
            da_acc[:, cre] += gr * pr + gi * pi
            da_acc[:, cim] += gi * pr - gr * pi

        _s5_scan(g_ref, g_ref, tab, edge, tb, True, grad_a)
        da_ref[...] = _colsum(da_acc[...])
        gr, gi = g_ref[:, 0:n], g_ref[:, n:2 * n]
        grb, gib = _bf(gr), _bf(gi)
        du = d_ref[...] * dy0 + _dot_nt(grb, b_ref[:, 0:n]) + _dot_nt(gib, b_ref[:, n:2 * n])
        ub = _bf(u)
        _acc(db_ref.at[:, 0:n], _dot_tn(ub, grb), first)
        _acc(db_ref.at[:, n:2 * n], _dot_tn(ub, gib), first)
        du_ref[...] = _bf(du)
        _acc(cs_ref, _colsum(du), first)

    rev = lambda i: (nb - 1 - i, 0)
    vec = pl.BlockSpec((1, D_GROUP), lambda i: (0, 0))
    vshape = jax.ShapeDtypeStruct((1, D_GROUP), F32)
    return pl.pallas_call(
        body, name=name, grid=(nb,),
        in_specs=[pl.BlockSpec((tb, D_GROUP), rev),
                  pl.BlockSpec((tb, D_GROUP), rev),
                  pl.BlockSpec((tb, D_GROUP), rev),
                  pl.BlockSpec((tb, 2 * n), rev),
                  pl.BlockSpec((8, 2 * n), lambda i: (jnp.maximum((nb - 1 - i) * halo - 1, 0), 0)),
                  pl.BlockSpec((1, 2 * n), lambda i: (0, 0)),
                  pl.BlockSpec((D_GROUP, 2 * n), lambda i: (0, 0)),
                  pl.BlockSpec((2 * n, D_GROUP), lambda i: (0, 0)),
                  vec,
                  pl.BlockSpec((None, D_GROUP, D_GROUP), lambda i: (layer, 0, 0)),
                  vec],
        out_specs=[pl.BlockSpec((tb, D_GROUP), rev), vec,
                   pl.BlockSpec((D_GROUP, 2 * n), lambda i: (0, 0)),
                   pl.BlockSpec((2 * n, D_GROUP), lambda i: (0, 0)),
                   vec,
                   pl.BlockSpec((D_GROUP, D_GROUP), lambda i: (0, 0)),
                   vec,
                   pl.BlockSpec((1, 2 * n), lambda i: (0, 0))],
        out_shape=[jax.ShapeDtypeStruct((s, D_GROUP), BF16), vshape,
                   jax.ShapeDtypeStruct((D_GROUP, 2 * n), F32), jax.ShapeDtypeStruct((2 * n, D_GROUP), F32),
                   vshape, jax.ShapeDtypeStruct((D_GROUP, D_GROUP), F32), vshape,
                   jax.ShapeDtypeStruct((1, 2 * n), F32)],
        scratch_shapes=[pltpu.VMEM((SUBLANES, 2 * n), F32), pltpu.VMEM((S5_TAB_ROWS, n), F32),
                        pltpu.VMEM((tb, 2 * n), F32), pltpu.VMEM((SUBLANES, 2 * n), F32)],
        compiler_params=_cparams(1),
    )(dmix, h_in, y0, hst, hst, a2, bexp, cexp, dskip, wglu, bglu)


CV_HALO = 32


def _gn_stats(hc, pmat):
    mu = _dot3(hc, pmat)
    xc = hc - mu
    var = _dot3(xc * xc, pmat)
    rstd = lax.rsqrt(var + LN_EPS)
    return xc * rstd, rstd


def cv_fwd(h_in, cw, cb, gg, gb, pmat, wpw, layer, bpw, tb, name):
    s = h_in.shape[0]
    hl = CV_HALO

    def body(v_ref, g_ref, cw_ref, cb_ref, gg_ref, gb_ref, p_ref, w_ref, bw_ref, y_ref, hc_ref, ext):
        @pl.when(pl.program_id(0) == 0)
        def _():
            ext[0:hl, :] = jnp.zeros((hl, D_GROUP), F32)

        ext[hl:hl + tb, :] = v_ref[...] * _sigmoid(g_ref[...])
        acc = jnp.zeros((tb, D_GROUP), F32) + cb_ref[...]
        for k in range(CONV_WIDTH):
            off = hl - (CONV_WIDTH - 1) + k
            acc = acc + cw_ref[k:k + 1, :] * ext[off:off + tb, :]
        hc_ref[...] = acc
        ext[0:hl, :] = ext[tb:tb + hl, :]
        xn, _ = _gn_stats(acc, p_ref[...])
        hn = xn * gg_ref[...] + gb_ref[...]
        hs = hn * _sigmoid(hn)
        y_ref[...] = _bf(_dot(_bf(hs), w_ref[...]) + bw_ref[...])

    vec = pl.BlockSpec((1, D_GROUP), lambda i: (0, 0))
    sq = pl.BlockSpec((D_GROUP, D_GROUP), lambda i: (0, 0))
    return pl.pallas_call(
        body, name=name, grid=(s // tb,),
        in_specs=[pl.BlockSpec((tb, D_GROUP), lambda i: (i, 1)),
                  pl.BlockSpec((tb, D_GROUP), lambda i: (i, 2)),
                  pl.BlockSpec((hl, D_GROUP), lambda i: (0, 0)),
                  vec, vec, vec, sq,
                  pl.BlockSpec((None, D_GROUP, D_GROUP), lambda i: (layer, 0, 0)),
                  vec],
        out_specs=[pl.BlockSpec((tb, D_GROUP), lambda i: (i, 0)), pl.BlockSpec((tb, D_GROUP), lambda i: (i, 0))],
        out_shape=[jax.ShapeDtypeStruct((s, D_GROUP), BF16), jax.ShapeDtypeStruct((s, D_GROUP), F32)],
        scratch_shapes=[pltpu.VMEM((hl + tb, D_GROUP), F32)],
        compiler_params=_cparams(1),
    )(h_in, h_in, cw, cb, gg, gb, pmat, wpw, bpw)


def cv_bwd(dmix, h_in, hc, cw, gg, gb, pmat, wpw, layer, tb, name, scatter=()):
    s = h_in.shape[0]
    hl = CV_HALO
    nb = s // tb
    per = tb // hl
    ns = len(scatter)

    def body(dy_ref, v_ref, g_ref, vh_ref, gh_ref, hc_ref, cw_ref, gg_ref, gb_ref, p_ref, w_ref, *rest):
        s_refs = rest[:ns]
        dvg_ref, cs_ref, dcw_ref, dcb_ref, dgg_ref, dgb_ref, dw_ref, dbw_ref = rest[ns:ns + 8]
        got_refs = rest[ns + 8:2 * ns + 8]
        ext, dext, head = rest[2 * ns + 8:2 * ns + 11]
        sems = rest[2 * ns + 11:]
        i = pl.program_id(0)
        first = i == 0

        @pl.when(first)
        def _():
            if ns:
                _scatter_start(s_refs, got_refs, sems)
            head[...] = jnp.zeros_like(head)

        dy = dy_ref[...]
        pm = p_ref[...]
        xn, rstd = _gn_stats(hc_ref[...], pm)
        hn = xn * gg_ref[...] + gb_ref[...]
        sg = _sigmoid(hn)
        hs = hn * sg
        dyb = _bf(dy)
        _acc(dbw_ref, _colsum(dy), first)
        _acc(dw_ref, _dot_tn(_bf(hs), dyb), first)
        dhs = _dot_nt(dyb, w_ref[...])
        dhn = dhs * sg * (1.0 + hn * (1.0 - sg))
        _acc(dgg_ref, _colsum(dhn * xn), first)
        _acc(dgb_ref, _colsum(dhn), first)
        dxn = dhn * gg_ref[...]
        dhc = rstd * (dxn - _dot3(dxn, pm) - xn * _dot3(dxn * xn, pm))
        _acc(dcb_ref, _colsum(dhc), first)
        v = v_ref[...]
        sgg = _sigmoid(g_ref[...])
        keep = jnp.where(i == nb - 1, 0.0, 1.0)
        ext[0:hl, :] = vh_ref[...] * _sigmoid(gh_ref[...]) * keep
        ext[hl:hl + tb, :] = v * sgg
        dext[0:tb, :] = dhc
        dext[tb:tb + hl, :] = head[...]
        head[...] = dhc[0:hl]
        dhg = jnp.zeros((tb, D_GROUP), F32)
        for k in range(CONV_WIDTH):
            off = hl - (CONV_WIDTH - 1) + k
            wk = _colsum(dhc * ext[off:off + tb, :])
            _acc(dcw_ref.at[k:k + 1, :], wk, first)
            back = CONV_WIDTH - 1 - k
            dhg = dhg + cw_ref[k:k + 1, :] * dext[back:back + tb, :]

        @pl.when(first)
        def _():
            dcw_ref[CONV_WIDTH:hl, :] = jnp.zeros((hl - CONV_WIDTH, D_GROUP), F32)

        dv = dhg * sgg
        dg = dhg * v * sgg * (1.0 - sgg)
        dvg_ref[:, 0:D_GROUP] = _bf(dv)
        dvg_ref[:, D_GROUP:2 * D_GROUP] = _bf(dg)
        _acc(cs_ref.at[:, 0:D_GROUP], _colsum(dv), first)
        _acc(cs_ref.at[:, D_GROUP:2 * D_GROUP], _colsum(dg), first)
        if ns:
            @pl.when(i == nb - 1)
            def _():
                _scatter_finish(s_refs, got_refs, sems)

    vec = pl.BlockSpec((1, D_GROUP), lambda i: (0, 0))
    sq = pl.BlockSpec((D_GROUP, D_GROUP), lambda i: (0, 0))
    tap = pl.BlockSpec((hl, D_GROUP), lambda i: (0, 0))
    vshape = jax.ShapeDtypeStruct((1, D_GROUP), F32)

    def blk(col):
        return pl.BlockSpec((tb, D_GROUP), lambda i: (nb - 1 - i, col))

    def halo_blk(col):
        return pl.BlockSpec((hl, D_GROUP), lambda i: (jnp.maximum((nb - 1 - i) * per - 1, 0), col))

    return pl.pallas_call(
        body, name=name, grid=(nb,),
        in_specs=[blk(1), blk(1), blk(2), halo_blk(1), halo_blk(2),
                  pl.BlockSpec((tb, D_GROUP), lambda i: (nb - 1 - i, 0)),
                  tap, vec, vec, sq,
                  pl.BlockSpec((None, D_GROUP, D_GROUP), lambda i: (layer, 0, 0))] + [ANY] * ns,
        out_specs=[pl.BlockSpec((tb, 2 * D_GROUP), lambda i: (nb - 1 - i, 0)),
                   pl.BlockSpec((1, 2 * D_GROUP), lambda i: (0, 0)),
                   tap, vec, vec, vec, sq, vec] + [ANY] * ns,
        out_shape=[jax.ShapeDtypeStruct((s, 2 * D_GROUP), BF16), jax.ShapeDtypeStruct((1, 2 * D_GROUP), F32),
                   jax.ShapeDtypeStruct((hl, D_GROUP), F32), vshape, vshape, vshape,
                   jax.ShapeDtypeStruct((D_GROUP, D_GROUP), F32), vshape] + _scatter_shapes(scatter),
        scratch_shapes=[pltpu.VMEM((hl + tb, D_GROUP), F32), pltpu.VMEM((tb + hl, D_GROUP), F32),
                        pltpu.VMEM((hl, D_GROUP), F32)] + (_scatter_sems(ns) if ns else []),
        compiler_params=_cparams(1),
    )(dmix, h_in, h_in, h_in, h_in, hc, cw, gg, gb, pmat, wpw, *scatter)


LRU_HALO = 8


def _lru_gates(xc, wr_ref, br_ref, wi_ref, bi_ref, sp_ref):
    xcb = _bf(xc)
    r = _sigmoid(_dot(xcb, wr_ref[...]) + br_ref[...])
    gi = _sigmoid(_dot(xcb, wi_ref[...]) + bi_ref[...])
    la = -LRU_C * r * sp_ref[...]
    a = jnp.exp(la)
    e2 = a * a
    sq = jnp.sqrt(-jnp.tanh(la) * (e2 + 1.0))
    return r, gi, a, e2, sq


def _rscan(a, b, tb, reverse):
    row = _rows(a.shape)
    sh = 1
    while sh < tb:
        if reverse:
            amt, mask = tb - sh, row < tb - sh
        else:
            amt, mask = sh, row >= sh
        a_s = jnp.where(mask, pltpu.roll(a, amt, 0), 1.0)
        b_s = jnp.where(mask, pltpu.roll(b, amt, 0), 0.0)
        b = b + a * b_s
        a = a * a_s
        sh *= 2
    return a, b


def lru_fwd(h_in, cw, cb, wr, br, wi, bi, sp, tb, name):
    s = h_in.shape[0]
    hl = LRU_HALO

    def body(xg_ref, xr_ref, cw_ref, cb_ref, wr_ref, br_ref, wi_ref, bi_ref, sp_ref, y_ref, xc_ref, h_ref, ext, carry):
        @pl.when(pl.program_id(0) == 0)
        def _():
            ext[0:hl, :] = jnp.zeros((hl, D_GROUP), F32)
            carry[...] = jnp.zeros_like(carry)

        ext[hl:hl + tb, :] = xr_ref[...]
        xc = jnp.zeros((tb, D_GROUP), F32) + cb_ref[...]
        for k in range(LRU_CONV_WIDTH):
            off = hl - (LRU_CONV_WIDTH - 1) + k
            xc = xc + cw_ref[k:k + 1, :] * ext[off:off + tb, :]
        xc_ref[...] = xc
        ext[0:hl, :] = ext[tb:tb + hl, :]
        r, gi, a, e2, sq = _lru_gates(xc, wr_ref, br_ref, wi_ref, bi_ref, sp_ref)
        pa, hloc = _rscan(a, sq * (gi * xc), tb, False)
        h = hloc + pa * carry[7:8, :]
        h_ref[...] = h
        carry[...] = h[tb - 8:tb]
        y_ref[...] = _bf(h * _gelu(xg_ref[...]))

    vec = pl.BlockSpec((1, D_GROUP), lambda i: (0, 0))
    sq_spec = pl.BlockSpec((D_GROUP, D_GROUP), lambda i: (0, 0))
    blk = pl.BlockSpec((tb, D_GROUP), lambda i: (i, 0))
    return pl.pallas_call(
        body, name=name, grid=(s // tb,),
        in_specs=[pl.BlockSpec((tb, D_GROUP), lambda i: (i, 3)),
                  pl.BlockSpec((tb, D_GROUP), lambda i: (i, 4)),
                  pl.BlockSpec((hl, D_GROUP), lambda i: (0, 0)),
                  vec, sq_spec, vec, sq_spec, vec, vec],
        out_specs=[blk, blk, blk],
        out_shape=[jax.ShapeDtypeStruct((s, D_GROUP), BF16), jax.ShapeDtypeStruct((s, D_GROUP), F32),
                   jax.ShapeDtypeStruct((s, D_GROUP), F32)],
        scratch_shapes=[pltpu.VMEM((hl + tb, D_GROUP), F32), pltpu.VMEM((8, D_GROUP), F32)],
        compiler_params=_cparams(1),
    )(h_in, h_in, cw, cb, wr, br, wi, bi, sp)


def lru_bwd(dmix, h_in, xcs, hs, cw, wr, br, wi, bi, sp, tb, name):
    s = h_in.shape[0]
    hl = LRU_HALO
    nb = s // tb
    per = tb // hl

    def body(dy_ref, xg_ref, xr_ref, xrh_ref, xc_ref, h_ref, hp_ref, cw_ref, wr_ref, br_ref, wi_ref, bi_ref, sp_ref,
             dx_ref, cs_ref, dcw_ref, dcb_ref, dwr_ref, dbr_ref, dwi_ref, dbi_ref, dsp_ref,
             ext, dext, head, anext, gnext):
        i = pl.program_id(0)
        first = i == 0

        @pl.when(first)
        def _():
            head[...] = jnp.zeros_like(head)
            anext[...] = jnp.zeros_like(anext)
            gnext[...] = jnp.zeros_like(gnext)

        dy = dy_ref[...]
        xg = xg_ref[...]
        xc = xc_ref[...]
        h = h_ref[...]
        r, gi, a, e2, sq = _lru_gates(xc, wr_ref, br_ref, wi_ref, bi_ref, sp_ref)
        gate, t = _gelu_parts(xg)
        dh = dy * gate
        dxg = dy * h * _gelu_grad(xg, t)
        row = _rows((tb, D_GROUP))
        coef = jnp.where(row == tb - 1, anext[0:1, :], pltpu.roll(a, tb - 1, 0))
        pc, gloc = _rscan(coef, dh, tb, True)
        gfull = gloc + pc * gnext[0:1, :]
        anext[...] = a[0:8]
        gnext[...] = gfull[0:8]
        keep = jnp.where(i == nb - 1, 0.0, 1.0)
        hprev = jnp.where(row == 0, hp_ref[7:8, :] * keep, pltpu.roll(h, 1, 0))
        da = gfull * hprev
        uu = gi * xc
        dsq = gfull * uu
        duu = gfull * sq
        dla = da * a - dsq * e2 / sq
        sp = sp_ref[...]
        dr = dla * (-LRU_C) * sp
        _acc(dsp_ref, _colsum(dla * (-LRU_C) * r), first)
        dzr = dr * r * (1.0 - r)
        dzi = duu * xc * gi * (1.0 - gi)
        dzrb, dzib = _bf(dzr), _bf(dzi)
        dxc = duu * gi + _dot_nt(dzrb, wr_ref[...]) + _dot_nt(dzib, wi_ref[...])
        xcb = _bf(xc)
        _acc(dwr_ref, _dot_tn(xcb, dzrb), first)
        _acc(dwi_ref, _dot_tn(xcb, dzib), first)
        _acc(dbr_ref, _colsum(dzr), first)
        _acc(dbi_ref, _colsum(dzi), first)
        _acc(dcb_ref, _colsum(dxc), first)
        ext[0:hl, :] = xrh_ref[...] * keep
        ext[hl:hl + tb, :] = xr_ref[...]
        dext[0:tb, :] = dxc
        dext[tb:tb + hl, :] = head[...]
        head[...] = dxc[0:hl]
        dxr = jnp.zeros((tb, D_GROUP), F32)
        for k in range(LRU_CONV_WIDTH):
            off = hl - (LRU_CONV_WIDTH - 1) + k
            _acc(dcw_ref.at[k:k + 1, :], _colsum(dxc * ext[off:off + tb, :]), first)
            back = LRU_CONV_WIDTH - 1 - k
            dxr = dxr + cw_ref[k:k + 1, :] * dext[back:back + tb, :]

        @pl.when(first)
        def _():
            dcw_ref[LRU_CONV_WIDTH:hl, :] = jnp.zeros((hl - LRU_CONV_WIDTH, D_GROUP), F32)

        dx_ref[:, 0:D_GROUP] = _bf(dxg)
        dx_ref[:, D_GROUP:2 * D_GROUP] = _bf(dxr)
        _acc(cs_ref.at[:, 0:D_GROUP], _colsum(dxg), first)
        _acc(cs_ref.at[:, D_GROUP:2 * D_GROUP], _colsum(dxr), first)

    vec = pl.BlockSpec((1, D_GROUP), lambda i: (0, 0))
    sq_spec = pl.BlockSpec((D_GROUP, D_GROUP), lambda i: (0, 0))
    tap = pl.BlockSpec((hl, D_GROUP), lambda i: (0, 0))
    vshape = jax.ShapeDtypeStruct((1, D_GROUP), F32)
    sshape = jax.ShapeDtypeStruct((D_GROUP, D_GROUP), F32)

    def blk(col):
        return pl.BlockSpec((tb, D_GROUP), lambda i: (nb - 1 - i, col))

    def halo_blk(col):
        return pl.BlockSpec((hl, D_GROUP), lambda i: (jnp.maximum((nb - 1 - i) * per - 1, 0), col))

    return pl.pallas_call(
        body, name=name, grid=(nb,),
        in_specs=[blk(2), blk(3), blk(4), halo_blk(4), blk(0), blk(0), halo_blk(0),
                  tap, sq_spec, vec, sq_spec, vec, vec],
        out_specs=[pl.BlockSpec((tb, 2 * D_GROUP), lambda i: (nb - 1 - i, 0)),
                   pl.BlockSpec((1, 2 * D_GROUP), lambda i: (0, 0)),
                   tap, vec, sq_spec, vec, sq_spec, vec, vec],
        out_shape=[jax.ShapeDtypeStruct((s, 2 * D_GROUP), BF16), jax.ShapeDtypeStruct((1, 2 * D_GROUP), F32),
                   jax.ShapeDtypeStruct((hl, D_GROUP), F32), vshape, sshape, vshape, sshape, vshape, vshape],
        scratch_shapes=[pltpu.VMEM((hl + tb, D_GROUP), F32), pltpu.VMEM((tb + hl, D_GROUP), F32),
                        pltpu.VMEM((hl, D_GROUP), F32), pltpu.VMEM((8, D_GROUP), F32), pltpu.VMEM((8, D_GROUP), F32)],
        compiler_params=_cparams(1),
    )(dmix, h_in, h_in, h_in, xcs, hs, hs, cw, wr, br, wi, bi, sp)


ATTN_HEADS = 4
ATTN_HEAD_DIM = 64
ATTN_SCALE = ATTN_HEAD_DIM ** -0.5


def _head_mask(h):
    lane = lax.broadcasted_iota(jnp.int32, (1, D_GROUP), 1)
    return jnp.where((lane >= h * ATTN_HEAD_DIM) & (lane < (h + 1) * ATTN_HEAD_DIM), 1.0, 0.0)


def _softmax_rows(sc):
    e = jnp.exp(sc - jnp.max(sc, -1, keepdims=True))
    return e / jnp.sum(e, -1, keepdims=True)


def attn_fwd(h_in, kv, tb, name):
    s = h_in.shape[0]

    def body(q_ref, kv_ref, y_ref):
        q = q_ref[...]
        kb = _bf(kv_ref[:, 0:D_GROUP])
        vb = _bf(kv_ref[:, D_GROUP:2 * D_GROUP])
        out = jnp.zeros((tb, D_GROUP), F32)
        for h in range(ATTN_HEADS):
            mask = _head_mask(h)
            p = _softmax_rows(_dot_nt(_bf(q * mask), kb) * ATTN_SCALE)
            out = out + _dot(_bf(p), vb) * mask
        y_ref[...] = _bf(out)

    return pl.pallas_call(
        body, name=name, grid=(s // tb,),
        in_specs=[pl.BlockSpec((tb, D_GROUP), lambda i: (i, 5)),
                  pl.BlockSpec((D_GROUP, 2 * D_GROUP), lambda i: (0, 0))],
        out_specs=pl.BlockSpec((tb, D_GROUP), lambda i: (i, 0)),
        out_shape=jax.ShapeDtypeStruct((s, D_GROUP), BF16),
        compiler_params=_cparams(1),
    )(h_in, kv)


def attn_bwd(dmix, h_in, kv, tb, name):
    s = h_in.shape[0]

    def body(do_ref, q_ref, kv_ref, dq_ref, cs_ref, dkv_ref):
        first = pl.program_id(0) == 0
        q = q_ref[...]
        do = do_ref[...]
        kb = _bf(kv_ref[:, 0:D_GROUP])
        vb = _bf(kv_ref[:, D_GROUP:2 * D_GROUP])
        dq = jnp.zeros((tb, D_GROUP), F32)
        dk = jnp.zeros((D_GROUP, D_GROUP), F32)
        dv = jnp.zeros((D_GROUP, D_GROUP), F32)
        for h in range(ATTN_HEADS):
            mask = _head_mask(h)
            qm = _bf(q * mask)
            p = _softmax_rows(_dot_nt(qm, kb) * ATTN_SCALE)
            dom = _bf(do * mask)
            dp = _dot_nt(dom, vb)
            dv = dv + _dot_tn(_bf(p), dom)
            ds = _bf(p * (dp - jnp.sum(dp * p, -1, keepdims=True)) * ATTN_SCALE)
            dq = dq + _dot(ds, kb) * mask
            dk = dk + _dot_tn(ds, qm)
        dq_ref[...] = _bf(dq)
        _acc(cs_ref, _colsum(dq), first)
        _acc(dkv_ref.at[:, 0:D_GROUP], dk, first)
        _acc(dkv_ref.at[:, D_GROUP:2 * D_GROUP], dv, first)

    return pl.pallas_call(
        body, name=name, grid=(s // tb,),
        in_specs=[pl.BlockSpec((tb, D_GROUP), lambda i: (i, 3)),
                  pl.BlockSpec((tb, D_GROUP), lambda i: (i, 5)),
                  pl.BlockSpec((D_GROUP, 2 * D_GROUP), lambda i: (0, 0))],
        out_specs=[pl.BlockSpec((tb, D_GROUP), lambda i: (i, 0)),
                   pl.BlockSpec((1, D_GROUP), lambda i: (0, 0)),
                   pl.BlockSpec((D_GROUP, 2 * D_GROUP), lambda i: (0, 0))],
        out_shape=[jax.ShapeDtypeStruct((s, D_GROUP), BF16), jax.ShapeDtypeStruct((1, D_GROUP), F32),
                   jax.ShapeDtypeStruct((D_GROUP, 2 * D_GROUP), F32)],
        compiler_params=_cparams(1),
    )(dmix, h_in, kv)


FFN_RB = 16
FFN_UNROLL_FWD = 4
FFN_UNROLL_BWD = 2
FFN_TAP_ROWS = 8
FFN_TN = D_FF // 2


def _shift_down(cur, tail, k):
    return pltpu.roll(jnp.concatenate([tail, cur], axis=0), k, 0)[SUBLANES:]


def _shift_up(cur, head, k):
    rb = cur.shape[0]
    return pltpu.roll(jnp.concatenate([cur, head], axis=0), rb + SUBLANES - k, 0)[:rb]


def _fold8(v):
    tot = v[0:SUBLANES]
    for t in range(1, v.shape[0] // SUBLANES):
        tot = tot + v[t * SUBLANES:(t + 1) * SUBLANES]
    return tot


def _strip(r):
    return pl.ds(pl.multiple_of(r * FFN_RB, FFN_RB), FFN_RB)


def ffn_act_fwd(u, cw, cb, tb, name, gather=()):
    s = u.shape[0]
    rb = FFN_RB
    nct = D_FF // FFN_TN
    nstrip = tb // rb
    ng = len(gather)

    def body(uv_ref, ug_ref, wv_ref, wg_ref, bv_ref, bg_ref, *rest):
        hf_ref, keep_ref = rest[ng], rest[ng + 1]
        slabs = rest[ng + 2:2 * ng + 2]
        tailv, tailg = rest[2 * ng + 2], rest[2 * ng + 3]
        sems = rest[2 * ng + 4:]
        if ng:
            @pl.when((pl.program_id(0) == 0) & (pl.program_id(1) == 0))
            def _():
                _gather_start(slabs, sems[0:2])

        @pl.when(pl.program_id(1) == 0)
        def _():
            tailv[...] = jnp.zeros_like(tailv)
            tailg[...] = jnp.zeros_like(tailg)

        for cc in range(FFN_TN // LANES):
            cols = slice(cc * LANES, (cc + 1) * LANES)
            wv = [wv_ref[k:k + 1, cols] for k in range(FFN_CONV_WIDTH)]
            wg = [wg_ref[k:k + 1, cols] for k in range(FFN_CONV_WIDTH)]
            bv, bg = bv_ref[:, cols], bg_ref[:, cols]

            def strip(r, carry):
                tail_v, tail_g = carry
                cur_v, cur_g = uv_ref[_strip(r), cols], ug_ref[_strip(r), cols]
                vc = wv[0] * _shift_down(cur_v, tail_v, 2) + wv[1] * _shift_down(cur_v, tail_v, 1) + wv[2] * cur_v + bv
                gc = wg[0] * _shift_down(cur_g, tail_g, 2) + wg[1] * _shift_down(cur_g, tail_g, 1) + wg[2] * cur_g + bg
                ge, t = _gelu_parts(gc)
                hf_ref[_strip(r), cols] = _bf(vc * ge)
                keep_ref[0, _strip(r), cols] = _bf(vc)
                keep_ref[1, _strip(r), cols] = _bf(ge)
                keep_ref[2, _strip(r), cols] = _bf(_gelu_grad(gc, t))
                return cur_v[rb - SUBLANES:], cur_g[rb - SUBLANES:]

            def strips(q, carry):
                for k in range(FFN_UNROLL_FWD):
                    carry = strip(q * FFN_UNROLL_FWD + k, carry)
                return carry

            last_v, last_g = lax.fori_loop(0, nstrip // FFN_UNROLL_FWD, strips, (tailv[:, cols], tailg[:, cols]))
            tailv[:, cols] = last_v
            tailg[:, cols] = last_g

        if ng:
            @pl.when((pl.program_id(0) == nct - 1) & (pl.program_id(1) == s // tb - 1))
            def _():
                _gather_finish(slabs, sems[0:2], sems[2:4])

    return pl.pallas_call(
        body, name=name, grid=(nct, s // tb),
        in_specs=[pl.BlockSpec((tb, FFN_TN), lambda c, i: (i, c)),
                  pl.BlockSpec((tb, FFN_TN), lambda c, i: (i, c + nct)),
                  pl.BlockSpec((FFN_TAP_ROWS, FFN_TN), lambda c, i: (0, c)),
                  pl.BlockSpec((FFN_TAP_ROWS, FFN_TN), lambda c, i: (0, c + nct)),
                  pl.BlockSpec((1, FFN_TN), lambda c, i: (0, c)),
                  pl.BlockSpec((1, FFN_TN), lambda c, i: (0, c + nct))] + [ANY] * ng,
        out_specs=[pl.BlockSpec((tb, FFN_TN), lambda c, i: (i, c)),
                   pl.BlockSpec((3, tb, FFN_TN), lambda c, i: (0, i, c))] + [ANY] * ng,
        out_shape=[jax.ShapeDtypeStruct((s, D_FF), BF16), jax.ShapeDtypeStruct((3, s, D_FF), BF16)]
        + [jax.ShapeDtypeStruct(a.shape, a.dtype) for a in gather],
        input_output_aliases={6 + k: 2 + k for k in range(ng)},
        scratch_shapes=[pltpu.VMEM((SUBLANES, FFN_TN), F32), pltpu.VMEM((SUBLANES, FFN_TN), F32)]
        + (_gather_sems(ng) if ng else []),
        compiler_params=_cparams(2),
    )(u, u, cw, cw, cb, cb, *gather)


def ffn_act_bwd(dhf, u, kept, cw, tb, name, scatter=()):
    s = u.shape[0]
    rb = FFN_RB
    nct = D_FF // FFN_TN
    nb = s // tb
    nstrip = tb // rb
    ntap = FFN_CONV_WIDTH
    ns = len(scatter)

    def body(dh_ref, uv_ref, ug_ref, kept_ref, wv_ref, wg_ref, *rest):
        s_refs = rest[:ns]
        du_ref, dwv_ref, dwg_ref = rest[ns:ns + 3]
        got_refs = rest[ns + 3:2 * ns + 3]
        headv, headg = rest[2 * ns + 3], rest[2 * ns + 4]
        sems = rest[2 * ns + 5:]
        i = pl.program_id(1)
        first = i == 0
        if ns:
            @pl.when((pl.program_id(0) == 0) & first)
            def _():
                _scatter_start(s_refs, got_refs, sems)

        @pl.when(first)
        def _():
            headv[...] = jnp.zeros_like(headv)
            headg[...] = jnp.zeros_like(headg)
            dwv_ref[...] = jnp.zeros_like(dwv_ref)
            dwg_ref[...] = jnp.zeros_like(dwg_ref)

        zero = jnp.zeros((SUBLANES, LANES), F32)
        for cc in range(FFN_TN // LANES):
            cols = slice(cc * LANES, (cc + 1) * LANES)
            wv = [wv_ref[k:k + 1, cols] for k in range(ntap)]
            wg = [wg_ref[k:k + 1, cols] for k in range(ntap)]

            def strip(ii, carry):
                head_dv, head_dg, acc_v, acc_g = carry
                r = nstrip - 1 - ii
                dh = dh_ref[_strip(r), cols]
                dvc = dh * kept_ref[1, _strip(r), cols].astype(F32)
                dgc = dh * kept_ref[0, _strip(r), cols].astype(F32) * kept_ref[2, _strip(r), cols].astype(F32)
                sdv = [_shift_up(dvc, head_dv, 2), _shift_up(dvc, head_dv, 1), dvc]
                sdg = [_shift_up(dgc, head_dg, 2), _shift_up(dgc, head_dg, 1), dgc]
                cur_v, cur_g = uv_ref[_strip(r), cols], ug_ref[_strip(r), cols]
                acc_v = tuple(acc_v[k] + _fold8(cur_v * sdv[k]) for k in range(ntap)) + (acc_v[ntap] + _fold8(dvc),)
                acc_g = tuple(acc_g[k] + _fold8(cur_g * sdg[k]) for k in range(ntap)) + (acc_g[ntap] + _fold8(dgc),)
                du_v = wv[0] * sdv[0] + wv[1] * sdv[1] + wv[2] * sdv[2]
                du_g = wg[0] * sdg[0] + wg[1] * sdg[1] + wg[2] * sdg[2]
                du_ref[0, _strip(r), cols] = _bf(du_v)
                du_ref[1, _strip(r), cols] = _bf(du_g)
                return dvc[0:SUBLANES], dgc[0:SUBLANES], acc_v, acc_g

            init = (headv[:, cols], headg[:, cols], (zero,) * (ntap + 1), (zero,) * (ntap + 1))
            def strips(q, carry):
                for k in range(FFN_UNROLL_BWD):
                    carry = strip(q * FFN_UNROLL_BWD + k, carry)
                return carry

            top_dv, top_dg, acc_v, acc_g = lax.fori_loop(0, nstrip // FFN_UNROLL_BWD, strips, init)
            headv[:, cols] = top_dv
            headg[:, cols] = top_dg
            for k in range(ntap + 1):
                dwv_ref[k:k + 1, cols] += _colsum(acc_v[k])
                dwg_ref[k:k + 1, cols] += _colsum(acc_g[k])

        if ns:
            @pl.when((pl.program_id(0) == nct - 1) & (i == nb - 1))
            def _():
                _scatter_finish(s_refs, got_refs, sems)

    def blk(shift):
        return pl.BlockSpec((tb, FFN_TN), lambda c, i: (nb - 1 - i, c + shift))

    tapv = pl.BlockSpec((FFN_TAP_ROWS, FFN_TN), lambda c, i: (0, c))
    tapg = pl.BlockSpec((FFN_TAP_ROWS, FFN_TN), lambda c, i: (0, c + nct))
    return pl.pallas_call(
        body, name=name, grid=(nct, nb),
        in_specs=[blk(0), blk(0), blk(nct), pl.BlockSpec((3, tb, FFN_TN), lambda c, i: (0, nb - 1 - i, c)), tapv, tapg]
        + [ANY] * ns,
        out_specs=[pl.BlockSpec((2, tb, FFN_TN), lambda c, i: (0, nb - 1 - i, c)), tapv, tapv] + [ANY] * ns,
        out_shape=[jax.ShapeDtypeStruct((2, s, D_FF), BF16), jax.ShapeDtypeStruct((FFN_TAP_ROWS, D_FF), F32),
                   jax.ShapeDtypeStruct((FFN_TAP_ROWS, D_FF), F32)] + _scatter_shapes(scatter),
        scratch_shapes=[pltpu.VMEM((SUBLANES, FFN_TN), F32), pltpu.VMEM((SUBLANES, FFN_TN), F32)]
        + (_scatter_sems(ns) if ns else []),
        compiler_params=_cparams(2),
    )(dhf, u, u, kept, cw, cw, *scatter)


def _place():
    x, y, c = lax.axis_index("x"), lax.axis_index("y"), lax.axis_index("c")
    return x, y, c, 2 * x + y


def _chip_peer(x, y, d):
    return jnp.bitwise_xor(x, d >> 1), jnp.bitwise_xor(y, d & 1)


def _my_half(ref_rows, c):
    half = ref_rows // 2
    return pl.ds(c * half, half)


def _gather_copy(ref, part, c, sems, k, d, to):
    rows = _my_half(ref.shape[1], c)
    return pltpu.make_async_remote_copy(src_ref=ref.at[part, rows], dst_ref=ref.at[part, rows], send_sem=sems[0].at[k, d - 1],
                                        recv_sem=sems[1].at[k, d - 1], device_id=to, device_id_type=MESH)


def _gather_start(slabs, ici_sems):
    x, y, c, j = _place()
    for k, ref in enumerate(slabs):
        for d in (1, 2, 3):
            px, py = _chip_peer(x, y, d)
            _gather_copy(ref, j, c, ici_sems, k, d, (px, py, c)).start()


def _gather_finish(slabs, ici_sems, d2d_sems):
    x, y, c, j = _place()
    sib = (x, y, 1 - c)
    passed = []
    for d in (1, 2, 3):
        jd = jnp.bitwise_xor(j, d)
        for k, ref in enumerate(slabs):
            _gather_copy(ref, jd, c, ici_sems, k, d, sib).wait_recv()
            cp = _gather_copy(ref, jd, c, d2d_sems, k, d, sib)
            cp.start()
            passed.append(cp)
    for cp in passed:
        cp.wait_recv()
        cp.wait_send()
    for k, ref in enumerate(slabs):
        for d in (1, 2, 3):
            _gather_copy(ref, j, c, ici_sems, k, d, sib).wait_send()


def _gather_sems(n):
    return [pltpu.SemaphoreType.DMA((n, 3)) for _ in range(4)]


def gather_weights(slabs):
    n = len(slabs)

    def body(*refs):
        outs = refs[n:2 * n]
        sems = refs[2 * n:]
        _gather_start(outs, sems[0:2])
        _gather_finish(outs, sems[0:2], sems[2:4])

    return pl.pallas_call(
        body, name="gather_weights", in_specs=[ANY] * n, out_specs=[ANY] * n,
        out_shape=[jax.ShapeDtypeStruct(a.shape, a.dtype) for a in slabs],
        input_output_aliases={w: w for w in range(n)}, scratch_shapes=_gather_sems(n),
    )(*slabs)


def _own_slab(part, jidx):
    slab = lax.empty((N_CHIPS,) + part.shape, part.dtype)
    return lax.dynamic_update_slice_in_dim(slab, part[None], jidx, axis=0)


def exchange_cores(gbig, name, small_all=None):
    n = len(gbig)
    with_small = small_all is not None

    def body(*refs):
        g_refs = refs[:n]
        got_refs = refs[n + with_small:2 * n + with_small]
        dsem, esem, ssem, rsem, fsem, hsem = refs[2 * (n + with_small):]
        x, y, c, j = _place()
        sib = (x, y, 1 - c)
        big = []
        for k in range(n):
            half = g_refs[k].shape[1] // 2
            cp = pltpu.make_async_remote_copy(
                src_ref=g_refs[k].at[:, pl.ds((1 - c) * half, half)], dst_ref=got_refs[k], send_sem=dsem.at[k],
                recv_sem=esem.at[k], device_id=sib, device_id_type=MESH)
            cp.start()
            big.append(cp)
        if with_small:
            all_ref = refs[2 * n + 1]
            me = 4 * x + 2 * y + c

            def small_copy(k, block, to, sems):
                return pltpu.make_async_remote_copy(
                    src_ref=all_ref.at[block], dst_ref=all_ref.at[block],
                    send_sem=sems[0].at[k], recv_sem=sems[1].at[k], device_id=to, device_id_type=MESH)

            first = [small_copy(0, me, sib, (ssem, rsem))]
            for d in (1, 2, 3):
                px, py = _chip_peer(x, y, d)
                first.append(small_copy(d, me, (px, py, c), (ssem, rsem)))
            for cp in first:
                cp.start()
            passed = []
            for d in (1, 2, 3):
                px, py = _chip_peer(x, y, d)
                src_block = 4 * px + 2 * py + c
                small_copy(d, src_block, sib, (ssem, rsem)).wait_recv()
                cp = small_copy(d - 1, src_block, sib, (fsem, hsem))
                cp.start()
                passed.append(cp)
            small_copy(0, me, sib, (ssem, rsem)).wait_recv()
            for cp in passed:
                cp.wait_recv()
            for cp in first + passed:
                cp.wait_send()
        for cp in big:
            cp.wait()

    ops = list(gbig) + ([small_all] if with_small else [])
    out_shape = [jax.ShapeDtypeStruct((g.shape[0], g.shape[1] // 2, g.shape[2]), F32) for g in gbig]
    aliases = {}
    if with_small:
        out_shape.append(jax.ShapeDtypeStruct(small_all.shape, F32))
        aliases = {n: n}
    return pl.pallas_call(
        body, name=name, in_specs=[ANY] * len(ops), out_specs=[ANY] * len(out_shape), out_shape=out_shape,
        input_output_aliases=aliases,
        scratch_shapes=[pltpu.SemaphoreType.DMA((n,)), pltpu.SemaphoreType.DMA((n,)),
                        pltpu.SemaphoreType.DMA((4,)), pltpu.SemaphoreType.DMA((4,)),
                        pltpu.SemaphoreType.DMA((3,)), pltpu.SemaphoreType.DMA((3,))],
    )(*ops)


def _scatter_copy(s_ref, got_ref, k, d, sems):
    x, y, c, j = _place()
    px, py = _chip_peer(x, y, d)
    return pltpu.make_async_remote_copy(
        src_ref=s_ref.at[jnp.bitwise_xor(j, d)], dst_ref=got_ref.at[d - 1], send_sem=sems[0].at[k, d - 1],
        recv_sem=sems[1].at[k, d - 1], device_id=(px, py, c), device_id_type=MESH)


def _scatter_start(s_refs, got_refs, sems):
    for d in (1, 2, 3):
        for k in range(len(s_refs)):
            _scatter_copy(s_refs[k], got_refs[k], k, d, sems).start()


def _scatter_finish(s_refs, got_refs, sems):
    for d in (1, 2, 3):
        for k in range(len(s_refs)):
            _scatter_copy(s_refs[k], got_refs[k], k, d, sems).wait()


def _scatter_sems(n):
    return [pltpu.SemaphoreType.DMA((n, 3)), pltpu.SemaphoreType.DMA((n, 3))]


def _scatter_shapes(s1):
    return [jax.ShapeDtypeStruct((3,) + a.shape[1:], a.dtype) for a in s1]


def scatter_shards(s1, name):
    n = len(s1)

    def body(*refs):
        _scatter_start(refs[:n], refs[n:2 * n], refs[2 * n:])
        _scatter_finish(refs[:n], refs[n:2 * n], refs[2 * n:])

    return pl.pallas_call(
        body, name=name, in_specs=[ANY] * n, out_specs=[ANY] * n, out_shape=_scatter_shapes(s1),
        scratch_shapes=_scatter_sems(n),
    )(*s1)


def share_with_sibling(parts):
    n = len(parts)

    def body(*refs):
        out_refs = refs[n:2 * n]
        ssem, rsem = refs[2 * n:]
        x, y, c, j = _place()
        cps = []
        for k in range(n):
            rows = _my_half(out_refs[k].shape[1], c)
            for l in range(DEPTH):
                cp = pltpu.make_async_remote_copy(
                    src_ref=out_refs[k].at[l, rows], dst_ref=out_refs[k].at[l, rows], send_sem=ssem.at[k, l],
                    recv_sem=rsem.at[k, l], device_id=(x, y, 1 - c), device_id_type=MESH)
                cp.start()
                cps.append(cp)
        for cp in cps:
            cp.wait()

    return pl.pallas_call(
        body, name="share_with_sibling", in_specs=[ANY] * n, out_specs=[ANY] * n,
        out_shape=[jax.ShapeDtypeStruct(a.shape, F32) for a in parts],
        input_output_aliases={k: k for k in range(n)},
        scratch_shapes=[pltpu.SemaphoreType.DMA((n, DEPTH)), pltpu.SemaphoreType.DMA((n, DEPTH))],
    )(*parts)


def add_core_halves(g, got, cidx, name):
    nch, r, cdim = g.shape
    half = r // 2
    tr = _row_tile(half, cdim, mult=16)
    per = half // tr

    def body(c_ref, a_ref, b_ref, o_ref):
        o_ref[...] = _bf(a_ref[...] + b_ref[...])

    grid_spec = pltpu.PrefetchScalarGridSpec(
        num_scalar_prefetch=1, grid=(nch, per),
        in_specs=[pl.BlockSpec((None, tr, cdim), lambda jj, i, c_ref: (jj, c_ref[0] * per + i, 0)),
                  pl.BlockSpec((None, tr, cdim), lambda jj, i, c_ref: (jj, i, 0))],
        out_specs=pl.BlockSpec((None, tr, cdim), lambda jj, i, c_ref: (jj, i, 0)))
    return pl.pallas_call(
        body, name=name, grid_spec=grid_spec,
        out_shape=jax.ShapeDtypeStruct((nch, half, cdim), BF16), compiler_params=_cparams(2),
    )(cidx, g, got)


def add_chip_parts(s1, got, jc, layer, into, name):
    _, half, cdim = s1.shape
    tr = _row_tile(half, cdim, mult=16)
    per = half // tr

    def body(jc_ref, a_ref, g0_ref, g1_ref, g2_ref, *rest):
        rest[-1][...] = ((a_ref[...].astype(F32) + g0_ref[...].astype(F32)) + g1_ref[...].astype(F32)) + g2_ref[...].astype(F32)

    def slot(k):
        return pl.BlockSpec((None, tr, cdim), lambda i, jc_ref: (k, i, 0))

    in_specs = [pl.BlockSpec((None, tr, cdim), lambda i, jc_ref: (jc_ref[0], i, 0)), slot(0), slot(1), slot(2)]
    ops = [jc, s1, got, got, got]
    aliases = {}
    if into is not None:
        in_specs.append(ANY)
        ops.append(into)
        aliases = {5: 0}
    grid_spec = pltpu.PrefetchScalarGridSpec(
        num_scalar_prefetch=1, grid=(per,), in_specs=in_specs,
        out_specs=pl.BlockSpec((None, tr, cdim), lambda i, jc_ref: (layer, jc_ref[1] * per + i, 0)))
    return pl.pallas_call(
        body, name=name, grid_spec=grid_spec, input_output_aliases=aliases,
        out_shape=jax.ShapeDtypeStruct((DEPTH, 2 * half, cdim), F32), compiler_params=_cparams(1),
    )(*ops)


def sum_devices(allp):
    _, r, _ = allp.shape

    def body(a_ref, o_ref):
        tot = a_ref[0]
        for k in range(1, 8):
            tot = tot + a_ref[k]
        o_ref[...] = tot

    tr = r // 2 if r % 16 == 0 else r
    return pl.pallas_call(
        body, name="sum_devices", grid=(r // tr,),
        in_specs=[pl.BlockSpec((8, tr, LANES), lambda i: (0, i, 0))],
        out_specs=pl.BlockSpec((tr, LANES), lambda i: (i, 0)),
        out_shape=jax.ShapeDtypeStruct((r, LANES), F32), compiler_params=_cparams(1),
    )(allp)


def _row_tile(r, cdim, limit_bytes=1 << 20, mult=8):
    best = None
    for tr in range(mult, r + 1, mult):
        if r % tr == 0 and tr * cdim * 4 <= limit_bytes:
            best = tr
    return best if best is not None else r


def adamw(w, g, m, v, name):
    r, cdim = w.shape
    tr = _row_tile(r, cdim)
    bc1 = 1.0 - ADAM_B1 ** ADAM_STEP
    bc2 = 1.0 - ADAM_B2 ** ADAM_STEP

    def body(w_ref, g_ref, m_ref, v_ref, d_ref, nm_ref, nv_ref, go_ref):
        gv = g_ref[...]
        nm = ADAM_B1 * m_ref[...] + (1.0 - ADAM_B1) * gv
        nv = ADAM_B2 * v_ref[...] + (1.0 - ADAM_B2) * (gv * gv)
        d_ref[...] = -ADAM_LR * ((nm / bc1) / (jnp.sqrt(nv / bc2) + ADAM_EPS) + ADAM_WD * w_ref[...])
        nm_ref[...] = nm
        nv_ref[...] = nv
        go_ref[...] = gv

    blk = pl.BlockSpec((tr, cdim), lambda i: (i, 0))
    shape = jax.ShapeDtypeStruct((r, cdim), F32)
    return pl.pallas_call(
        body, name=name, grid=(r // tr,), in_specs=[blk] * 4, out_specs=[blk] * 4, out_shape=[shape] * 4,
        compiler_params=_cparams(1),
    )(w, g, m, v)


def _s5_prepare(lam_re, lam_im, log_dt, b_re, b_im, c_re, c_im):
    groups, ch = 16, 16
    dt = jnp.exp(log_dt)[:, None]
    mag = jnp.exp(lam_re * dt)
    a_r, a_i = mag * jnp.cos(lam_im * dt), mag * jnp.sin(lam_im * dt)
    den = lam_re * lam_re + lam_im * lam_im
    q_r = ((a_r - 1.0) * lam_re + a_i * lam_im) / den
    q_i = (a_i * lam_re - (a_r - 1.0) * lam_im) / den
    bb_r = q_r[..., None] * b_re - q_i[..., None] * b_im
    bb_i = q_r[..., None] * b_im + q_i[..., None] * b_re
    eye = jnp.eye(groups, dtype=F32)

    def expand_b(bb):
        return jnp.einsum("gpc,gh->gchp", bb, eye).reshape(groups * ch, N_STATE)

    def expand_c(cc):
        return jnp.einsum("gcp,gh->hpgc", cc, eye).reshape(N_STATE, groups * ch)

    a2 = jnp.concatenate([a_r.reshape(1, N_STATE), a_i.reshape(1, N_STATE)], axis=1)
    bexp = jnp.concatenate([expand_b(bb_r), expand_b(bb_i)], axis=1)
    cexp = jnp.concatenate([expand_c(c_re), -expand_c(c_im)], axis=0)
    return a2, bexp, cexp


def _lru_prepare(w_r, w_i, lam):
    heads = 4
    eye = jnp.eye(heads, dtype=F32)

    def expand(w):
        return jnp.einsum("hij,hk->hikj", w, eye).reshape(D_GROUP, D_GROUP)

    return expand(w_r), expand(w_i), jax.nn.softplus(-lam).reshape(1, D_GROUP)


def _pad_rows(a, rows):
    return jnp.pad(a, ((0, rows - a.shape[0]), (0, 0)))


def _group_mean_matrix():
    gidx = jnp.arange(D_GROUP) // 64
    return (gidx[:, None] == gidx[None, :]).astype(BF16) * jnp.asarray(1.0 / 64.0, BF16)


def _pack_rows(arrs, width):
    parts = []
    for a in arrs:
        flat = a.reshape(-1)
        pad = (-flat.shape[0]) % width
        parts.append(jnp.pad(flat, (0, pad)) if pad else flat)
    flat = jnp.concatenate(parts)
    rows = flat.shape[0] // width
    pad_rows = (-rows) % 16
    if pad_rows:
        flat = jnp.pad(flat, (0, pad_rows * width))
    return flat.reshape(-1, width)


def _unpack_rows(packed, shapes, width):
    flat = packed.reshape(-1)
    out, off = [], 0
    for shp in shapes:
        size = math.prod(shp)
        out.append(flat[off:off + size].reshape(shp))
        off += size + ((-size) % width)
    return out


def kernel(x, mem, ln_in_g, ln_in_b, w_in, b_in, s5_lam_re, s5_lam_im, s5_log_dt, s5_b_re, s5_b_im, s5_c_re, s5_c_im, s5_d, s5_w_glu, s5_b_glu, cv_w, cv_b, cv_gn_g, cv_gn_b, cv_w_pw, cv_b_pw, lru_conv_w, lru_conv_b, lru_w_r, lru_b_r, lru_w_i, lru_b_i, lru_lam, attn_w_kv, w_out, b_out, ln1_g, ln1_b, ffn_w_up, ffn_conv_w, ffn_conv_b, ffn_w_down, ln2_g, ln2_b, loss_target, m_ln_in_g, m_ln_in_b, m_w_in, m_b_in, m_s5_lam_re, m_s5_lam_im, m_s5_log_dt, m_s5_b_re, m_s5_b_im, m_s5_c_re, m_s5_c_im, m_s5_d, m_s5_w_glu, m_s5_b_glu, m_cv_w, m_cv_b, m_cv_gn_g, m_cv_gn_b, m_cv_w_pw, m_cv_b_pw, m_lru_conv_w, m_lru_conv_b, m_lru_w_r, m_lru_b_r, m_lru_w_i, m_lru_b_i, m_lru_lam, m_attn_w_kv, m_w_out, m_b_out, m_ln1_g, m_ln1_b, m_ffn_w_up, m_ffn_conv_w, m_ffn_conv_b, m_ffn_w_down, m_ln2_g, m_ln2_b, v_ln_in_g, v_ln_in_b, v_w_in, v_b_in, v_s5_lam_re, v_s5_lam_im, v_s5_log_dt, v_s5_b_re, v_s5_b_im, v_s5_c_re, v_s5_c_im, v_s5_d, v_s5_w_glu, v_s5_b_glu, v_cv_w, v_cv_b, v_cv_gn_g, v_cv_gn_b, v_cv_w_pw, v_cv_b_pw, v_lru_conv_w, v_lru_conv_b, v_lru_w_r, v_lru_b_r, v_lru_w_i, v_lru_b_i, v_lru_lam, v_attn_w_kv, v_w_out, v_b_out, v_ln1_g, v_ln1_b, v_ffn_w_up, v_ffn_conv_w, v_ffn_conv_b, v_ffn_w_down, v_ln2_g, v_ln2_b):
    p = dict(locals())
    xs = x[0]
    mems = mem[0]
    target = loss_target[0]
    s = xs.shape[0]
    cidx = lax.axis_index("c")
    jidx = 2 * lax.axis_index("x") + lax.axis_index("y")
    tb_scan = min(256, s)
    tb_s5 = min(512, s)
    tb_attn = min(512, s)
    tb_ffn = min(256, s)

    small_sh_names = list(SMALL_SHARDED)
    small_sh_shapes = [p[nm].shape[1:] for nm in small_sh_names]
    slabs = [[_own_slab(_bf(p[nm][l]), jidx) for nm in BIG]
             + [_own_slab(_pack_rows([p[nm][l] for nm in small_sh_names], LANES), jidx)] for l in range(DEPTH)]
    n_slabs = len(BIG) + 1
    first_needed = FIRST_NEEDED + [len(BIG)]
    arrive_later = [k for k in range(n_slabs) if k not in first_needed]
    gathered = [[None] * n_slabs for _ in range(DEPTH)]
    for k, slab in zip(first_needed, gather_weights([slabs[0][k] for k in first_needed])):
        gathered[0][k] = slab

    def weight_views(gw, which):
        shapes = {0: (1, N_CHIPS, D_MODEL, N_IN // N_CHIPS), 1: (1, 1, D_MODEL, 2 * D_GROUP), 2: (1, 1, D_MODEL, D_MODEL),
                  3: (1, N_CHIPS, D_MODEL, 2 * D_FF // N_CHIPS), 4: (1, 1, D_FF, D_MODEL),
                  5: (1, D_GROUP, D_GROUP), 6: (1, D_GROUP, D_GROUP)}
        views = {BIG[k]: gw[k].reshape(shapes[k]) for k in which if k < len(BIG)}
        if len(BIG) in which:
            per_chip = [_unpack_rows(gw[len(BIG)][jj], small_sh_shapes, LANES) for jj in range(N_CHIPS)]
            for k, nm in enumerate(small_sh_names):
                views[nm] = jnp.concatenate([per_chip[jj][k] for jj in range(N_CHIPS)], axis=SMALL_SHARDED[nm] - 1)
        return views

    pmat = _group_mean_matrix()

    def vec(a):
        return a.reshape(1, -1)

    xh0, rs0, xb0 = ln_fwd(xs, vec(ln_in_g), vec(ln_in_b), "ln_in")
    saved = []
    prev = dict(xh=xh0, rs=rs0, xb=xb0, g=vec(ln_in_g), b=vec(ln_in_b))
    for l in range(DEPTH):
        sv = dict(prev=prev)
        (a2, bexp, cexp), sv['s5_vjp'] = jax.vjp(_s5_prepare, s5_lam_re[l], s5_lam_im[l], s5_log_dt[l],
                                                 s5_b_re[l], s5_b_im[l], s5_c_re[l], s5_c_im[l])
        (wr, wi, sp), sv['lru_vjp'] = jax.vjp(_lru_prepare, lru_w_r[l], lru_w_i[l], lru_lam[l])
        sv.update(a2=a2, bexp=_bf(bexp), cexp=_bf(cexp), wr=_bf(wr), wi=_bf(wi), sp=sp)
        late = arrive_later if l == 0 else []
        gw = sv['gw'] = weight_views(gathered[l], [k for k in range(n_slabs) if k not in late])
        sv['cvw'] = _pad_rows(gw['cv_w'], CV_HALO)
        sv['lcw'] = _pad_rows(gw['lru_conv_w'], LRU_HALO)
        sv['fcw'] = _pad_rows(gw['ffn_conv_w'], FFN_TAP_ROWS)
        sv['w_glu'], sv['w_pw'] = gw['s5_w_glu'], gw['cv_w_pw']
        h_in = mm_nn(prev['xb'], gw['w_in'], 0, vec(b_in[l]), F32, 2048, f"in_proj{l}")
        kv = mm_nn(mems, gw['attn_w_kv'], 0, jnp.zeros((1, 2 * D_GROUP), F32), F32, 256, f"kv_proj{l}")
        y_s5, hst, y0, *got = s5_fwd(h_in, a2, sv['bexp'], sv['cexp'], vec(s5_d[l]), sv['w_glu'], 0, vec(s5_b_glu[l]),
                                     tb_s5, f"s5_fwd{l}", gather=[slabs[l][k] for k in late])
        for k, slab in zip(late, got):
            gathered[l][k] = slab
        gw.update(weight_views(gathered[l], late))
        y_cv, hc = cv_fwd(h_in, sv['cvw'], vec(cv_b[l]), vec(cv_gn_g[l]), vec(cv_gn_b[l]), pmat, sv['w_pw'], 0,
                          vec(cv_b_pw[l]), tb_scan, f"cv_fwd{l}")
        y_lru, xcs, hls = lru_fwd(h_in, sv['lcw'], vec(lru_conv_b[l]), sv['wr'], vec(lru_b_r[l]), sv['wi'],
                                  vec(lru_b_i[l]), sp, tb_scan, f"lru_fwd{l}")
        y_mem = attn_fwd(h_in, kv, tb_attn, f"attn_fwd{l}")
        mix_in = jnp.concatenate([y_s5, y_cv, y_lru, y_mem], axis=1)
        xh1, rs1, xb1 = proj_ln(mix_in, gw['w_out'].reshape(1, D_MODEL, D_MODEL), 0, vec(b_out[l]),
                                prev['xh'], prev['g'], prev['b'], vec(ln1_g[l]), vec(ln1_b[l]), f"out_proj_ln{l}")
        u = mm_nn(xb1, gw['ffn_w_up'], 0, jnp.zeros((1, 2 * D_FF), F32), F32, 1024, f"ffn_up{l}")
        nxt = slabs[l + 1] if l + 1 < DEPTH else ()
        hf, sv['ffn_kept'], *got = ffn_act_fwd(u, sv['fcw'], vec(ffn_conv_b[l]), tb_ffn, f"ffn_act{l}", gather=nxt)
        if nxt:
            gathered[l + 1] = got
        xh2, rs2, xb2 = proj_ln(hf, gw['ffn_w_down'].reshape(1, D_FF, D_MODEL), 0, jnp.zeros((1, D_MODEL), F32),
                                xh1, vec(ln1_g[l]), vec(ln1_b[l]), vec(ln2_g[l]), vec(ln2_b[l]), f"ffn_down_ln{l}")
        sv.update(h_in=h_in, kv=kv, hst=hst, y0=y0, hc=hc, xcs=xcs, hls=hls, mix_in=mix_in,
                  xh1=xh1, rs1=rs1, xb1=xb1, u=u, hf=hf, xh2=xh2, rs2=rs2)
        saved.append(sv)
        prev = dict(xh=xh2, rs=rs2, xb=xb2, g=vec(ln2_g[l]), b=vec(ln2_b[l]))

    grads = {}
    per_layer = {nm: [None] * DEPTH for nm in WEIGHTS if nm not in ('ln_in_g', 'ln_in_b')}
    c1 = cidx.reshape(1).astype(jnp.int32)
    jc = jnp.stack([jidx, cidx]).astype(jnp.int32)
    red_big = [None] * len(BIG)
    pending = None
    below = None

    def chip_parts(gl, which, tag, small_all=None):
        gs = [gl[k].reshape((N_CHIPS,) + p[BIG[k]].shape[1:]) for k in which]
        got = exchange_cores(gs, f"exchange_cores{tag}", small_all)
        parts = [add_core_halves(g, ga, c1, f"add_cores_{BIG[k]}{tag}") for g, ga, k in zip(gs, got, which)]
        return parts, (got[len(which)] if small_all is not None else None)

    def own_sum(which, parts, got, l):
        for k, part, gk in zip(which, parts, got):
            red_big[k] = add_chip_parts(part, gk, jc, l, red_big[k], f"add_chips_{BIG[k]}{l}")

    everything = list(range(len(BIG)))
    ready_early = [2, 3, 4]
    ready_last = [k for k in everything if k not in ready_early]

    for l in reversed(range(DEPTH)):
        sv = saved[l]
        pv = sv['prev']
        gw = sv['gw']
        gl = [None] * len(BIG)
        if l == DEPTH - 1:
            dr2, dg2, db2, sqerr = loss_ln_bwd(target, sv['xh2'], sv['rs2'], vec(ln2_g[l]), vec(ln2_b[l]), "loss_ln2_bwd")
            loss_local = 0.5 / D_MODEL * jnp.sum(sqerr)
        else:
            dr2, dg2, db2 = below
        per_layer['ln2_g'][l], per_layer['ln2_b'][l] = dg2[0], db2[0]
        tm_nt = min(512, s)
        ts_big = min(2048, s)
        whole = lambda a_ref, j: a_ref[...]
        dhf = mm_nt(dr2, (tm_nt, D_MODEL), lambda i: (i, 0), whole, gw['ffn_w_down'], 0, None, F32, 512, s, f"ffn_down_dx{l}")
        gl[4] = mm_tn(sv['hf'], dr2, (min(1024, s), D_MODEL), lambda kt, j, st: (st, 0), 1, D_MODEL, FFN_TN, 1024, s,
                      f"ffn_down_dw{l}")
        waiting = pending[2] if pending is not None else ()
        du, dcwv, dcwg, *got = ffn_act_bwd(dhf, sv['u'], sv['ffn_kept'], sv['fcw'], tb_ffn, f"ffn_act_bwd{l}",
                                           scatter=waiting)
        if pending is not None:
            own_sum(pending[1], pending[2], got, pending[0])
            pending = None
        dcw = jnp.concatenate([dcwv, dcwg], axis=1)
        per_layer['ffn_conv_w'][l] = dcw[0:FFN_CONV_WIDTH]
        per_layer['ffn_conv_b'][l] = dcw[FFN_CONV_WIDTH]
        dr1, dg1, db1, cs1 = mm_nt(du, (2, tm_nt, D_FF), lambda i: (0, i, 0),
                                   lambda a_ref, j: a_ref[j // 2, :, (j % 2) * FFN_TN:(j % 2 + 1) * FFN_TN], gw['ffn_w_up'], 0, dr2,
                                   F32, 512, s, f"ffn_up_dx_ln1_bwd{l}", ln=(sv['xh1'], sv['rs1'], vec(ln1_g[l])))
        gl[3] = mm_tn(sv['xb1'], du, (None, ts_big, FFN_TN), lambda kt, j, st: (j // 2, st, j % 2), N_CHIPS, FFN_TN,
                      D_MODEL, 2048, s, f"ffn_up_dw{l}")
        per_layer['ln1_g'][l], per_layer['ln1_b'][l], per_layer['b_out'][l] = dg1[0], db1[0], cs1[0]
        dmix = mm_nt(dr1, (tm_nt, D_MODEL), lambda i: (i, 0), whole, gw['w_out'], 0, None, F32, 512, s, f"out_proj_dx{l}")
        gl[2] = mm_tn(sv['mix_in'], dr1, (min(1024, s), D_MODEL), lambda kt, j, st: (st, 0), 1, D_MODEL, D_MODEL, 1024, s,
                      f"out_proj_dw{l}")
        h_in = sv['h_in']
        (d_u, cs_u, d_bexp, d_cexp, d_dd, d_wglu, d_bglu, d_a2) = s5_bwd(
            dmix, h_in, sv['y0'], sv['hst'], sv['a2'], sv['bexp'], sv['cexp'], vec(s5_d[l]), sv['w_glu'], 0,
            vec(s5_b_glu[l]), tb_s5, f"s5_bwd{l}")
        early = chip_parts(gl, ready_early, f"{l}a")[0] if l == 0 else ()
        (d_vg, cs_vg, d_cvw, d_cvb, d_gg, d_gb, d_wpw, d_bpw, *got) = cv_bwd(
            dmix, h_in, sv['hc'], sv['cvw'], vec(cv_gn_g[l]), vec(cv_gn_b[l]), pmat, sv['w_pw'], 0, tb_scan, f"cv_bwd{l}",
            scatter=early)
        if l == 0:
            own_sum(ready_early, early, got, l)
        (d_lx, cs_lx, d_lcw, d_lcb, d_wr, d_br, d_wi, d_bi, d_sp) = lru_bwd(
            dmix, h_in, sv['xcs'], sv['hls'], sv['lcw'], sv['wr'], vec(lru_b_r[l]), sv['wi'], vec(lru_b_i[l]),
            sv['sp'], tb_scan, f"lru_bwd{l}")
        d_q, cs_q, d_kv = attn_bwd(dmix, h_in, sv['kv'], tb_attn, f"attn_bwd{l}")
        g_s5 = sv['s5_vjp']((d_a2, d_bexp, d_cexp))
        for nm, gval in zip(['s5_lam_re', 's5_lam_im', 's5_log_dt', 's5_b_re', 's5_b_im', 's5_c_re', 's5_c_im'], g_s5):
            per_layer[nm][l] = gval
        g_lru = sv['lru_vjp']((d_wr, d_wi, d_sp))
        for nm, gval in zip(['lru_w_r', 'lru_w_i', 'lru_lam'], g_lru):
            per_layer[nm][l] = gval
        per_layer['s5_d'][l], per_layer['s5_b_glu'][l] = d_dd[0], d_bglu[0]
        per_layer['cv_w'][l], per_layer['cv_b'][l] = d_cvw[0:CONV_WIDTH], d_cvb[0]
        per_layer['cv_gn_g'][l], per_layer['cv_gn_b'][l] = d_gg[0], d_gb[0]
        per_layer['cv_b_pw'][l] = d_bpw[0]
        gl[5], gl[6] = d_wglu, d_wpw
        per_layer['lru_conv_w'][l], per_layer['lru_conv_b'][l] = d_lcw[0:LRU_CONV_WIDTH], d_lcb[0]
        per_layer['lru_b_r'][l], per_layer['lru_b_i'][l] = d_br[0], d_bi[0]
        per_layer['b_in'][l] = jnp.concatenate([cs_u, cs_vg, cs_lx, cs_q], axis=1)[0]
        gl[1] = mm_tn(mems, d_kv, (MEM_ROWS, 2 * D_GROUP), lambda kt, j, st: (st, 0), 1, 2 * D_GROUP, D_MODEL, MEM_ROWS,
                      MEM_ROWS, f"kv_proj_dw{l}")
        dh_in = jnp.concatenate([d_u, d_vg, d_lx, d_q], axis=1)
        n_sh = N_IN // N_CHIPS
        below = mm_nt(dh_in, (tm_nt, N_IN), lambda i: (i, 0), lambda a_ref, j: a_ref[:, j * n_sh:(j + 1) * n_sh],
                      gw['w_in'], 0, dr1, F32, 512, s, f"in_proj_dx_ln_bwd{l}", ln=(pv['xh'], pv['rs'], pv['g']))[:3]
        gl[0] = mm_tn(pv['xb'], dh_in, (ts_big, n_sh), lambda kt, j, st: (st, j), N_CHIPS, n_sh, D_MODEL, 2048, s,
                      f"in_proj_dw{l}")
        if l > 0:
            pending = (l, everything, chip_parts(gl, everything, str(l))[0])
    grad_x, dg_in, db_in = below
    grads['ln_in_g'], grads['ln_in_b'] = dg_in[0], db_in[0]
    for nm, vals in per_layer.items():
        if nm not in BIG:
            grads[nm] = jnp.stack(vals)

    small_names = [nm for nm in WEIGHTS if nm not in BIG]
    small_local = _pack_rows([grads[nm] for nm in small_names], LANES)
    me = 4 * lax.axis_index("x") + 2 * lax.axis_index("y") + cidx
    small_slab = lax.dynamic_update_slice_in_dim(lax.empty((8,) + small_local.shape, F32), small_local[None], me, axis=0)
    parts, small_all = chip_parts(gl, ready_last, "0b", small_slab)
    own_sum(ready_last, parts, scatter_shards(parts, "scatter_shards0"), 0)
    red_big = share_with_sibling(red_big)
    small_red = sum_devices(small_all)
    small_grads = dict(zip(small_names, _unpack_rows(small_red, [grads[nm].shape for nm in small_names], LANES)))

    out_g, out_d, out_m, out_v = {}, {}, {}, {}
    for k, nm in enumerate(BIG):
        gk = red_big[k]
        two_d = (-1, p[nm].shape[-1])
        res = adamw(p[nm].reshape(two_d), gk.reshape(two_d), p['m_' + nm].reshape(two_d),
                    p['v_' + nm].reshape(two_d), f"adamw_{nm}")
        out_d[nm], out_m[nm], out_v[nm], out_g[nm] = (t.reshape(p[nm].shape) for t in res)
    own = {}
    for nm in small_names:
        gfull = small_grads[nm]
        if nm in SMALL_SHARDED:
            ax = SMALL_SHARDED[nm]
            width = p[nm].shape[ax]
            gfull = lax.dynamic_slice_in_dim(gfull, jidx * width, width, axis=ax)
        own[nm] = gfull
    packs = [_pack_rows([src[nm] for nm in small_names], LANES)
             for src in (dict((nm, p[nm]) for nm in small_names), own,
                         dict((nm, p['m_' + nm]) for nm in small_names), dict((nm, p['v_' + nm]) for nm in small_names))]
    dlt, nm_, nv_, _ = adamw(packs[0], packs[1], packs[2], packs[3], "adamw_small")
    shapes = [p[nm].shape for nm in small_names]
    for dst, packed in ((out_d, dlt), (out_m, nm_), (out_v, nv_)):
        dst.update(zip(small_names, _unpack_rows(packed, shapes, LANES)))
    out_g.update(own)

    loss = lax.psum(loss_local, ("x", "y", "c"))
    return (loss, grad_x[None], *[out_g[nm] for nm in WEIGHTS], *[out_d[nm] for nm in WEIGHTS],
            *[out_m[nm] for nm in WEIGHTS], *[out_v[nm] for nm in WEIGHTS])
```

```python
import functools
import math

import jax
import jax.numpy as jnp
from jax import lax
from jax.experimental import pallas as pl
from jax.experimental.pallas import tpu as pltpu

F32 = jnp.float32
BF16 = jnp.bfloat16
MESH = pl.DeviceIdType.MESH
ANY = pl.BlockSpec(memory_space=pl.ANY)

DEPTH = 2
D_MODEL = 1024
D_GROUP = 256
N_IN = 6 * D_GROUP
D_FF = 2816
N_STATE = 1024
CONV_WIDTH = 31
LRU_CONV_WIDTH = 4
FFN_CONV_WIDTH = 3
LRU_C = 8.0
ALPHA = (2 * DEPTH) ** 0.25
LN_EPS = 1e-5
N_CHIPS = 4
MEM_ROWS = 256
LANES = 128
SUBLANES = 8
VMEM_LIMIT = 56 * 1024 * 1024

ADAM_LR, ADAM_B1, ADAM_B2, ADAM_EPS, ADAM_WD, ADAM_STEP = 0.001, 0.9, 0.999, 1e-08, 0.01, 10

WEIGHTS = ['ln_in_g', 'ln_in_b', 'w_in', 'b_in', 's5_lam_re', 's5_lam_im', 's5_log_dt', 's5_b_re', 's5_b_im',
           's5_c_re', 's5_c_im', 's5_d', 's5_w_glu', 's5_b_glu', 'cv_w', 'cv_b', 'cv_gn_g', 'cv_gn_b', 'cv_w_pw',
           'cv_b_pw', 'lru_conv_w', 'lru_conv_b', 'lru_w_r', 'lru_b_r', 'lru_w_i', 'lru_b_i', 'lru_lam',
           'attn_w_kv', 'w_out', 'b_out', 'ln1_g', 'ln1_b', 'ffn_w_up', 'ffn_conv_w', 'ffn_conv_b', 'ffn_w_down',
           'ln2_g', 'ln2_b']
BIG = ['w_in', 'attn_w_kv', 'w_out', 'ffn_w_up', 'ffn_w_down', 's5_w_glu', 'cv_w_pw']
FIRST_NEEDED = [0, 1, 5, 6]
SMALL_SHARDED = {'cv_w': 2, 'lru_conv_w': 2, 'ffn_conv_w': 2}


def _cparams(n_axes):
    return pltpu.CompilerParams(dimension_semantics=("arbitrary",) * n_axes, vmem_limit_bytes=VMEM_LIMIT)


def _dot(a, b):
    return jnp.dot(a, b, preferred_element_type=F32)


def _dot_nt(a, b):
    return lax.dot_general(a, b, (((1,), (1,)), ((), ())), preferred_element_type=F32)


def _dot_tn(a, b):
    return lax.dot_general(a, b, (((0,), (0,)), ((), ())), preferred_element_type=F32)


def _bf(v):
    return v.astype(BF16)


def _colsum(v):
    return jnp.sum(v, axis=0, keepdims=True)


def _dot3(v, p):
    hi = _bf(v)
    r1 = v - hi.astype(F32)
    mid = _bf(r1)
    lo = _bf(r1 - mid.astype(F32))
    return _dot(hi, p) + _dot(mid, p) + _dot(lo, p)


_GELU_C = math.sqrt(2.0 / math.pi)


_GELU_C3 = _GELU_C * 0.044715


def _gelu_parts(v):
    t = jnp.tanh(v * (_GELU_C + _GELU_C3 * (v * v)))
    hv = 0.5 * v
    return hv + hv * t, t


def _gelu(v):
    return _gelu_parts(v)[0]


def _gelu_grad(v, t):
    return (0.5 + 0.5 * t) + (0.5 * v) * (1.0 - t * t) * (_GELU_C + (3.0 * _GELU_C3) * (v * v))


def _sigmoid(v):
    return 1.0 / (1.0 + jnp.exp(-v))


def _acc(ref, val, first):
    @pl.when(first)
    def _():
        ref[...] = val

    @pl.when(jnp.logical_not(first))
    def _():
        ref[...] += val


def _rows(shape):
    return lax.broadcasted_iota(jnp.int32, shape, 0)


def _ln_rows(r):
    mu = jnp.mean(r, -1, keepdims=True)
    rc = r - mu
    var = jnp.mean(rc * rc, -1, keepdims=True)
    rs = lax.rsqrt(var + LN_EPS)
    return rc * rs, rs


def ln_fwd(x, g, b, name):
    s = x.shape[0]
    tm = min(512, s)

    def body(x_ref, g_ref, b_ref, xh_ref, rs_ref, xb_ref):
        xh, rs = _ln_rows(x_ref[...])
        xh_ref[...] = xh
        rs_ref[...] = rs
        xb_ref[...] = _bf(xh * g_ref[...] + b_ref[...])

    row = pl.BlockSpec((tm, D_MODEL), lambda i: (i, 0))
    vec = pl.BlockSpec((1, D_MODEL), lambda i: (0, 0))
    return pl.pallas_call(
        body, name=name, grid=(s // tm,),
        in_specs=[row, vec, vec],
        out_specs=[row, pl.BlockSpec((tm, 1), lambda i: (i, 0)), row],
        out_shape=[jax.ShapeDtypeStruct((s, D_MODEL), F32), jax.ShapeDtypeStruct((s, 1), F32),
                   jax.ShapeDtypeStruct((s, D_MODEL), BF16)],
        compiler_params=_cparams(1),
    )(x, g, b)


def proj_ln(a, w, layer, bias, xh_prev, g_prev, b_prev, g, b, name):
    s, k = a.shape
    tm = min(512, s)

    def body(a_ref, w_ref, bias_ref, xp_ref, gp_ref, bp_ref, g_ref, b_ref, xh_ref, rs_ref, xb_ref):
        acc = _dot(a_ref[...], w_ref[...]) + bias_ref[...]
        r = ALPHA * (xp_ref[...] * gp_ref[...] + bp_ref[...]) + acc
        xh, rs = _ln_rows(r)
        xh_ref[...] = xh
        rs_ref[...] = rs
        xb_ref[...] = _bf(xh * g_ref[...] + b_ref[...])

    row = pl.BlockSpec((tm, D_MODEL), lambda i: (i, 0))
    vec = pl.BlockSpec((1, D_MODEL), lambda i: (0, 0))
    return pl.pallas_call(
        body, name=name, grid=(s // tm,),
        in_specs=[pl.BlockSpec((tm, k), lambda i: (i, 0)),
                  pl.BlockSpec((None, k, D_MODEL), lambda i: (layer, 0, 0)),
                  vec, row, vec, vec, vec, vec],
        out_specs=[row, pl.BlockSpec((tm, 1), lambda i: (i, 0)), row],
        out_shape=[jax.ShapeDtypeStruct((s, D_MODEL), F32), jax.ShapeDtypeStruct((s, 1), F32),
                   jax.ShapeDtypeStruct((s, D_MODEL), BF16)],
        compiler_params=_cparams(1),
    )(a, w, bias, xh_prev, g_prev, b_prev, g, b)


def loss_ln_bwd(target, xh, rs, g, b, name):
    s = xh.shape[0]
    tm = min(512, s)

    def body(t_ref, xh_ref, rs_ref, g_ref, b_ref, dr_ref, dg_ref, db_ref, sq_ref):
        first = pl.program_id(0) == 0
        xhv = xh_ref[...]
        err = xhv * g_ref[...] + b_ref[...] - t_ref[...]
        dyv = err * (1.0 / D_MODEL)
        dxh = dyv * g_ref[...]
        dr = rs_ref[...] * (dxh - jnp.mean(dxh, -1, keepdims=True) - xhv * jnp.mean(dxh * xhv, -1, keepdims=True))
        dr_ref[...] = dr
        _acc(dg_ref, _colsum(dyv * xhv), first)
        _acc(db_ref, _colsum(dyv), first)
        _acc(sq_ref, _colsum(err * err), first)

    row = pl.BlockSpec((tm, D_MODEL), lambda i: (i, 0))
    vec = pl.BlockSpec((1, D_MODEL), lambda i: (0, 0))
    vshape = jax.ShapeDtypeStruct((1, D_MODEL), F32)
    return pl.pallas_call(
        body, name=name, grid=(s // tm,),
        in_specs=[row, row, pl.BlockSpec((tm, 1), lambda i: (i, 0)), vec, vec],
        out_specs=[row, vec, vec, vec],
        out_shape=[jax.ShapeDtypeStruct((s, D_MODEL), F32), vshape, vshape, vshape],
        compiler_params=_cparams(1),
    )(target, xh, rs, g, b)


def mm_nn(a, w, layer, bias, out_dtype, tm, name):
    m, k = a.shape
    _, nj, _, n = w.shape
    tm = min(tm, m)

    def body(a_ref, w_ref, b_ref, o_ref):
        o_ref[...] = (_dot(_bf(a_ref[...]), w_ref[...]) + b_ref[...]).astype(out_dtype)

    return pl.pallas_call(
        body, name=name, grid=(nj, m // tm),
        in_specs=[pl.BlockSpec((tm, k), lambda j, i: (i, 0)),
                  pl.BlockSpec((None, None, k, n), lambda j, i: (layer, j, 0, 0)),
                  pl.BlockSpec((1, n), lambda j, i: (0, j))],
        out_specs=pl.BlockSpec((tm, n), lambda j, i: (i, j)),
        out_shape=jax.ShapeDtypeStruct((m, nj * n), out_dtype),
        compiler_params=_cparams(2),
    )(a, w, bias)


def mm_nt(a, a_block, a_map, pick, w, layer, add, out_dtype, tm, m, name, ln=None):
    _, nj, r, n = w.shape
    tm = min(tm, m)
    has_add = add is not None
    n_in = 2 + has_add + (3 if ln is not None else 0)

    def body(*refs):
        a_ref, w_ref = refs[0], refs[1]
        res = _dot_nt(_bf(pick(a_ref, 0)), w_ref[0])
        for j in range(1, nj):
            res = res + _dot_nt(_bf(pick(a_ref, j)), w_ref[j])
        if has_add:
            res = res + ALPHA * refs[2][...]
        if ln is None:
            refs[n_in][...] = res.astype(out_dtype)
            return
        xh_ref, rs_ref, g_ref = refs[n_in - 3:n_in]
        dr_ref, dg_ref, db_ref, cs_ref = refs[n_in:]
        first = pl.program_id(0) == 0
        xhv = xh_ref[...]
        dxh = res * g_ref[...]
        dr = rs_ref[...] * (dxh - jnp.mean(dxh, -1, keepdims=True) - xhv * jnp.mean(dxh * xhv, -1, keepdims=True))
        dr_ref[...] = dr
        _acc(dg_ref, _colsum(res * xhv), first)
        _acc(db_ref, _colsum(res), first)
        _acc(cs_ref, _colsum(dr), first)

    row = pl.BlockSpec((tm, r), lambda i: (i, 0))
    in_specs = [pl.BlockSpec(a_block, a_map),
                pl.BlockSpec((None, nj, r, n), lambda i: (layer, 0, 0, 0))]
    ops = [a, w]
    if has_add:
        in_specs.append(row)
        ops.append(add)
    if ln is None:
        out_specs, out_shape = row, jax.ShapeDtypeStruct((m, r), out_dtype)
    else:
        vec = pl.BlockSpec((1, r), lambda i: (0, 0))
        vshape = jax.ShapeDtypeStruct((1, r), F32)
        in_specs += [row, pl.BlockSpec((tm, 1), lambda i: (i, 0)), vec]
        ops += list(ln)
        out_specs, out_shape = [row, vec, vec, vec], [jax.ShapeDtypeStruct((m, r), F32), vshape, vshape, vshape]
    return pl.pallas_call(
        body, name=name, grid=(m // tm,),
        in_specs=in_specs, out_specs=out_specs, out_shape=out_shape,
        compiler_params=_cparams(1),
    )(*ops)


def mm_tn(a, b, b_block, b_map, nj, n, tk, ts, s, name, exchange=()):
    kx = a.shape[1]
    ts = min(ts, s)
    ne = len(exchange)
    grid = (kx // tk, nj, s // ts)

    def body(a_ref, b_ref, *rest):
        g_refs, o_ref, got_refs, sems = rest[:ne], rest[ne], rest[ne + 1:2 * ne + 1], rest[2 * ne + 1:]
        pids = [pl.program_id(ax) for ax in range(3)]
        if ne:
            @pl.when((pids[0] == 0) & (pids[1] == 0) & (pids[2] == 0))
            def _():
                _exchange_start(g_refs, got_refs, sems)

        part = _dot_tn(_bf(a_ref[...]), _bf(b_ref[...]))
        _acc(o_ref, part, pids[2] == 0)
        if ne:
            @pl.when((pids[0] == grid[0] - 1) & (pids[1] == grid[1] - 1) & (pids[2] == grid[2] - 1))
            def _():
                _exchange_finish(g_refs, got_refs, sems)

    res = pl.pallas_call(
        body, name=name, grid=grid,
        in_specs=[pl.BlockSpec((ts, tk), lambda kt, j, st: (st, kt)), pl.BlockSpec(b_block, b_map)] + [ANY] * ne,
        out_specs=[pl.BlockSpec((None, tk, n), lambda kt, j, st: (j, kt, 0))] + [ANY] * ne,
        out_shape=[jax.ShapeDtypeStruct((nj, kx, n), F32)] + _exchange_shapes(exchange),
        scratch_shapes=_exchange_sems(ne) if ne else [],
        compiler_params=_cparams(3),
    )(a, b, *exchange)
    return res if ne else res[0]


S5_TAB_ROWS = 8 * SUBLANES


def _s5_scan_table(tab_ref, ar, ai, reverse):
    n = N_STATE
    row = _rows((SUBLANES, n))
    edge = SUBLANES - 1 if reverse else 0
    tab_ref[0:8, :] = jnp.where(row == edge, ar, 0.0)
    tab_ref[8:16, :] = jnp.where(row == edge, ai, 0.0)
    pr, pi = ar, ai
    for step, k in enumerate((1, 2, 4)):
        mask = row < SUBLANES - k if reverse else row >= k
        tab_ref[16 + 16 * step:24 + 16 * step, :] = jnp.where(mask, pr, 0.0)
        tab_ref[24 + 16 * step:32 + 16 * step, :] = jnp.where(mask, pi, 0.0)
        pr, pi = pr * pr - pi * pi, 2.0 * pr * pi


def _s5_scan(src_ref, dst_ref, tab_ref, edge_ref, tb, reverse, per_tile=None):
    n = N_STATE
    ng = tb // SUBLANES
    nq = n // LANES
    link = SUBLANES - 1 if reverse else 1

    def tile(ii, carry):
        g = ng - 1 - ii if reverse else ii
        rows = pl.ds(pl.multiple_of(g * SUBLANES, SUBLANES), SUBLANES)
        out = []
        for q in range(nq):
            cre = slice(q * LANES, (q + 1) * LANES)
            cim = slice(n + q * LANES, n + (q + 1) * LANES)
            lr, li = src_ref[rows, cre], src_ref[rows, cim]
            tr, ti = pltpu.roll(carry[2 * q], link, 0), pltpu.roll(carry[2 * q + 1], link, 0)
            kr, ki = tab_ref[0:8, cre], tab_ref[8:16, cre]
            lr, li = lr + kr * tr - ki * ti, li + kr * ti + ki * tr
            for step, k in enumerate((1, 2, 4)):
                amt = SUBLANES - k if reverse else k
                kr, ki = tab_ref[16 + 16 * step:24 + 16 * step, cre], tab_ref[24 + 16 * step:32 + 16 * step, cre]
                sr, si = pltpu.roll(lr, amt, 0), pltpu.roll(li, amt, 0)
                lr, li = lr + kr * sr - ki * si, li + kr * si + ki * sr
            dst_ref[rows, cre] = lr
            dst_ref[rows, cim] = li
            if per_tile is not None:
                per_tile(g, q, (cre, cim), lr, li)
            out += [lr, li]
        return tuple(out)

    init = []
    for q in range(nq):
        init += [edge_ref[:, q * LANES:(q + 1) * LANES], edge_ref[:, n + q * LANES:n + (q + 1) * LANES]]
    fin = lax.fori_loop(0, ng, tile, tuple(init))
    for q in range(nq):
        edge_ref[:, q * LANES:(q + 1) * LANES] = fin[2 * q]
        edge_ref[:, n + q * LANES:n + (q + 1) * LANES] = fin[2 * q + 1]


def s5_fwd(h_in, a2, bexp, cexp, dskip, wglu, layer, bglu, tb, name, gather=()):
    s = h_in.shape[0]
    n = N_STATE
    ng = len(gather)

    def body(u_ref, a_ref, b_ref, c_ref, d_ref, w_ref, bg_ref, *rest):
        y_ref, h_ref, y0_ref = rest[ng:ng + 3]
        slabs = rest[ng + 3:2 * ng + 3]
        edge, tab, bu_ref = rest[2 * ng + 3:2 * ng + 6]
        sems = rest[2 * ng + 6:]

        @pl.when(pl.program_id(0) == 0)
        def _():
            if ng:
                _gather_start(slabs, sems[0:2])
            edge[...] = jnp.zeros_like(edge)
            _s5_scan_table(tab, a_ref[0:1, 0:n], a_ref[0:1, n:2 * n], False)

        u = u_ref[...]
        bu_ref[...] = _dot(_bf(u), b_ref[...])
        _s5_scan(bu_ref, h_ref, tab, edge, tb, False)
        y0 = _dot(_bf(h_ref[:, 0:n]), c_ref[0:n, :]) + _dot(_bf(h_ref[:, n:2 * n]), c_ref[n:2 * n, :]) + d_ref[...] * u
        y0_ref[...] = y0
        yg = _gelu(y0)
        z = _dot(_bf(yg), w_ref[...]) + bg_ref[...]
        y_ref[...] = _bf(yg * _sigmoid(z))
        if ng:
            @pl.when(pl.program_id(0) == s // tb - 1)
            def _():
                _gather_finish(slabs, sems[0:2], sems[2:4])

    vec = pl.BlockSpec((1, D_GROUP), lambda i: (0, 0))
    return pl.pallas_call(
        body, name=name, grid=(s // tb,),
        in_specs=[pl.BlockSpec((tb, D_GROUP), lambda i: (i, 0)),
                  pl.BlockSpec((1, 2 * n), lambda i: (0, 0)),
                  pl.BlockSpec((D_GROUP, 2 * n), lambda i: (0, 0)),
                  pl.BlockSpec((2 * n, D_GROUP), lambda i: (0, 0)),
                  vec,
                  pl.BlockSpec((None, D_GROUP, D_GROUP), lambda i: (layer, 0, 0)),
                  vec] + [ANY] * ng,
        out_specs=[pl.BlockSpec((tb, D_GROUP), lambda i: (i, 0)),
                   pl.BlockSpec((tb, 2 * n), lambda i: (i, 0)),
                   pl.BlockSpec((tb, D_GROUP), lambda i: (i, 0))] + [ANY] * ng,
        out_shape=[jax.ShapeDtypeStruct((s, D_GROUP), BF16), jax.ShapeDtypeStruct((s, 2 * n), F32),
                   jax.ShapeDtypeStruct((s, D_GROUP), F32)] + [jax.ShapeDtypeStruct(a.shape, a.dtype) for a in gather],
        input_output_aliases={7 + k: 3 + k for k in range(ng)},
        scratch_shapes=[pltpu.VMEM((SUBLANES, 2 * n), F32), pltpu.VMEM((S5_TAB_ROWS, n), F32),
                        pltpu.VMEM((tb, 2 * n), F32)] + (_gather_sems(ng) if ng else []),
        compiler_params=_cparams(1),
    )(h_in, a2, bexp, cexp, dskip, wglu, bglu, *gather)


def s5_bwd(dmix, h_in, y0, hst, a2, bexp, cexp, dskip, wglu, layer, bglu, tb, name):
    s = h_in.shape[0]
    n = N_STATE
    nb = s // tb
    halo = tb // 8

    def body(dy_ref, u_ref, y0_ref, h_ref, hp_ref, a_ref, b_ref, c_ref, d_ref, w_ref, bg_ref,
             du_ref, cs_ref, db_ref, dc_ref, dd_ref, dw_ref, dbg_ref, da_ref, edge, tab, g_ref, da_acc):
        i = pl.program_id(0)
        first = i == 0

        @pl.when(first)
        def _():
            edge[...] = jnp.zeros_like(edge)
            da_acc[...] = jnp.zeros_like(da_acc)
            _s5_scan_table(tab, a_ref[0:1, 0:n], -a_ref[0:1, n:2 * n], True)

        dy = dy_ref[...]
        u = u_ref[...]
        y0v = y0_ref[...]
        yg, t = _gelu_parts(y0v)
        z = _dot(_bf(yg), w_ref[...]) + bg_ref[...]
        sg = _sigmoid(z)
        dz = dy * yg * sg * (1.0 - sg)
        dyg = dy * sg + _dot_nt(_bf(dz), w_ref[...])
        _acc(dw_ref, _dot_tn(_bf(yg), _bf(dz)), first)
        _acc(dbg_ref, _colsum(dz), first)
        dy0 = dyg * _gelu_grad(y0v, t)
        _acc(dd_ref, _colsum(dy0 * u), first)
        dy0b = _bf(dy0)
        _acc(dc_ref.at[0:n, :], _dot_tn(_bf(h_ref[:, 0:n]), dy0b), first)
        _acc(dc_ref.at[n:2 * n, :], _dot_tn(_bf(h_ref[:, n:2 * n]), dy0b), first)
        g_ref[...] = _dot_nt(dy0b, c_ref[...])
        keep = jnp.where(i == nb - 1, 0.0, 1.0)
        row0 = _rows((SUBLANES, LANES)) == 0

        def grad_a(g, q, cols, gr, gi):
            cre, cim = cols
            rows = pl.ds(pl.multiple_of(g * SUBLANES, SUBLANES), SUBLANES)
            before = pl.ds(pl.multiple_of(jnp.maximum(g - 1, 0) * SUBLANES, SUBLANES), SUBLANES)
            pre = jnp.where(g == 0, hp_ref[:, cre] * keep, h_ref[before, cre])
            pim = jnp.where(g == 0, hp_ref[:, cim] * keep, h_ref[before, cim])
            pr = jnp.where(row0, pltpu.roll(pre, 1, 0), pltpu.roll(h_ref[rows, cre], 1, 0))
            pi = jnp.where(row0, pltpu.roll(pim, 1, 0), pltpu.roll(h_ref[rows, cim], 1, 0))
            da_acc[:, cre] += gr * pr + gi * pi
            da_acc[:, cim] += gi * pr - gr * pi

        _s5_scan(g_ref, g_ref, tab, edge, tb, True, grad_a)
        da_ref[...] = _colsum(da_acc[...])
        gr, gi = g_ref[:, 0:n], g_ref[:, n:2 * n]
        grb, gib = _bf(gr), _bf(gi)
        du = d_ref[...] * dy0 + _dot_nt(grb, b_ref[:, 0:n]) + _dot_nt(gib, b_ref[:, n:2 * n])
        ub = _bf(u)
        _acc(db_ref.at[:, 0:n], _dot_tn(ub, grb), first)
        _acc(db_ref.at[:, n:2 * n], _dot_tn(ub, gib), first)
        du_ref[...] = _bf(du)
        _acc(cs_ref, _colsum(du), first)

    rev = lambda i: (nb - 1 - i, 0)
    vec = pl.BlockSpec((1, D_GROUP), lambda i: (0, 0))
    vshape = jax.ShapeDtypeStruct((1, D_GROUP), F32)
    return pl.pallas_call(
        body, name=name, grid=(nb,),
        in_specs=[pl.BlockSpec((tb, D_GROUP), rev),
                  pl.BlockSpec((tb, D_GROUP), rev),
                  pl.BlockSpec((tb, D_GROUP), rev),
                  pl.BlockSpec((tb, 2 * n), rev),
                  pl.BlockSpec((8, 2 * n), lambda i: (jnp.maximum((nb - 1 - i) * halo - 1, 0), 0)),
                  pl.BlockSpec((1, 2 * n), lambda i: (0, 0)),
                  pl.BlockSpec((D_GROUP, 2 * n), lambda i: (0, 0)),
                  pl.BlockSpec((2 * n, D_GROUP), lambda i: (0, 0)),
                  vec,
                  pl.BlockSpec((None, D_GROUP, D_GROUP), lambda i: (layer, 0, 0)),
                  vec],
        out_specs=[pl.BlockSpec((tb, D_GROUP), rev), vec,
                   pl.BlockSpec((D_GROUP, 2 * n), lambda i: (0, 0)),
                   pl.BlockSpec((2 * n, D_GROUP), lambda i: (0, 0)),
                   vec,
                   pl.BlockSpec((D_GROUP, D_GROUP), lambda i: (0, 0)),
                   vec,
                   pl.BlockSpec((1, 2 * n), lambda i: (0, 0))],
        out_shape=[jax.ShapeDtypeStruct((s, D_GROUP), BF16), vshape,
                   jax.ShapeDtypeStruct((D_GROUP, 2 * n), F32), jax.ShapeDtypeStruct((2 * n, D_GROUP), F32),
                   vshape, jax.ShapeDtypeStruct((D_GROUP, D_GROUP), F32), vshape,
                   jax.ShapeDtypeStruct((1, 2 * n), F32)],
        scratch_shapes=[pltpu.VMEM((SUBLANES, 2 * n), F32), pltpu.VMEM((S5_TAB_ROWS, n), F32),
                        pltpu.VMEM((tb, 2 * n), F32), pltpu.VMEM((SUBLANES, 2 * n), F32)],
        compiler_params=_cparams(1),
    )(dmix, h_in, y0, hst, hst, a2, bexp, cexp, dskip, wglu, bglu)


CV_HALO = 32


def _gn_stats(hc, pmat):
    mu = _dot3(hc, pmat)
    xc = hc - mu
    var = _dot3(xc * xc, pmat)
    rstd = lax.rsqrt(var + LN_EPS)
    return xc * rstd, rstd


def cv_fwd(h_in, cw, cb, gg, gb, pmat, wpw, layer, bpw, tb, name):
    s = h_in.shape[0]
    hl = CV_HALO

    def body(v_ref, g_ref, cw_ref, cb_ref, gg_ref, gb_ref, p_ref, w_ref, bw_ref, y_ref, hc_ref, ext):
        @pl.when(pl.program_id(0) == 0)
        def _():
            ext[0:hl, :] = jnp.zeros((hl, D_GROUP), F32)

        ext[hl:hl + tb, :] = v_ref[...] * _sigmoid(g_ref[...])
        acc = jnp.zeros((tb, D_GROUP), F32) + cb_ref[...]
        for k in range(CONV_WIDTH):
            off = hl - (CONV_WIDTH - 1) + k
            acc = acc + cw_ref[k:k + 1, :] * ext[off:off + tb, :]
        hc_ref[...] = acc
        ext[0:hl, :] = ext[tb:tb + hl, :]
        xn, _ = _gn_stats(acc, p_ref[...])
        hn = xn * gg_ref[...] + gb_ref[...]
        hs = hn * _sigmoid(hn)
        y_ref[...] = _bf(_dot(_bf(hs), w_ref[...]) + bw_ref[...])

    vec = pl.BlockSpec((1, D_GROUP), lambda i: (0, 0))
    sq = pl.BlockSpec((D_GROUP, D_GROUP), lambda i: (0, 0))
    return pl.pallas_call(
        body, name=name, grid=(s // tb,),
        in_specs=[pl.BlockSpec((tb, D_GROUP), lambda i: (i, 1)),
                  pl.BlockSpec((tb, D_GROUP), lambda i: (i, 2)),
                  pl.BlockSpec((hl, D_GROUP), lambda i: (0, 0)),
                  vec, vec, vec, sq,
                  pl.BlockSpec((None, D_GROUP, D_GROUP), lambda i: (layer, 0, 0)),
                  vec],
        out_specs=[pl.BlockSpec((tb, D_GROUP), lambda i: (i, 0)), pl.BlockSpec((tb, D_GROUP), lambda i: (i, 0))],
        out_shape=[jax.ShapeDtypeStruct((s, D_GROUP), BF16), jax.ShapeDtypeStruct((s, D_GROUP), F32)],
        scratch_shapes=[pltpu.VMEM((hl + tb, D_GROUP), F32)],
        compiler_params=_cparams(1),
    )(h_in, h_in, cw, cb, gg, gb, pmat, wpw, bpw)


def cv_bwd(dmix, h_in, hc, cw, gg, gb, pmat, wpw, layer, tb, name, scatter=()):
    s = h_in.shape[0]
    hl = CV_HALO
    nb = s // tb
    per = tb // hl
    ns = len(scatter)

    def body(dy_ref, v_ref, g_ref, vh_ref, gh_ref, hc_ref, cw_ref, gg_ref, gb_ref, p_ref, w_ref, *rest):
        s_refs = rest[:ns]
        dvg_ref, cs_ref, dcw_ref, dcb_ref, dgg_ref, dgb_ref, dw_ref, dbw_ref = rest[ns:ns + 8]
        got_refs = rest[ns + 8:2 * ns + 8]
        ext, dext, head = rest[2 * ns + 8:2 * ns + 11]
        sems = rest[2 * ns + 11:]
        i = pl.program_id(0)
        first = i == 0

        @pl.when(first)
        def _():
            if ns:
                _scatter_start(s_refs, got_refs, sems)
            head[...] = jnp.zeros_like(head)

        dy = dy_ref[...]
        pm = p_ref[...]
        xn, rstd = _gn_stats(hc_ref[...], pm)
        hn = xn * gg_ref[...] + gb_ref[...]
        sg = _sigmoid(hn)
        hs = hn * sg
        dyb = _bf(dy)
        _acc(dbw_ref, _colsum(dy), first)
        _acc(dw_ref, _dot_tn(_bf(hs), dyb), first)
        dhs = _dot_nt(dyb, w_ref[...])
        dhn = dhs * sg * (1.0 + hn * (1.0 - sg))
        _acc(dgg_ref, _colsum(dhn * xn), first)
        _acc(dgb_ref, _colsum(dhn), first)
        dxn = dhn * gg_ref[...]
        dhc = rstd * (dxn - _dot3(dxn, pm) - xn * _dot3(dxn * xn, pm))
        _acc(dcb_ref, _colsum(dhc), first)
        v = v_ref[...]
        sgg = _sigmoid(g_ref[...])
        keep = jnp.where(i == nb - 1, 0.0, 1.0)
        ext[0:hl, :] = vh_ref[...] * _sigmoid(gh_ref[...]) * keep
        ext[hl:hl + tb, :] = v * sgg
        dext[0:tb, :] = dhc
        dext[tb:tb + hl, :] = head[...]
        head[...] = dhc[0:hl]
        dhg = jnp.zeros((tb, D_GROUP), F32)
        for k in range(CONV_WIDTH):
            off = hl - (CONV_WIDTH - 1) + k
            wk = _colsum(dhc * ext[off:off + tb, :])
            _acc(dcw_ref.at[k:k + 1, :], wk, first)
            back = CONV_WIDTH - 1 - k
            dhg = dhg + cw_ref[k:k + 1, :] * dext[back:back + tb, :]

        @pl.when(first)
        def _():
            dcw_ref[CONV_WIDTH:hl, :] = jnp.zeros((hl - CONV_WIDTH, D_GROUP), F32)

        dv = dhg * sgg
        dg = dhg * v * sgg * (1.0 - sgg)
        dvg_ref[:, 0:D_GROUP] = _bf(dv)
        dvg_ref[:, D_GROUP:2 * D_GROUP] = _bf(dg)
        _acc(cs_ref.at[:, 0:D_GROUP], _colsum(dv), first)
        _acc(cs_ref.at[:, D_GROUP:2 * D_GROUP], _colsum(dg), first)
        if ns:
            @pl.when(i == nb - 1)
            def _():
                _scatter_finish(s_refs, got_refs, sems)

    vec = pl.BlockSpec((1, D_GROUP), lambda i: (0, 0))
    sq = pl.BlockSpec((D_GROUP, D_GROUP), lambda i: (0, 0))
    tap = pl.BlockSpec((hl, D_GROUP), lambda i: (0, 0))
    vshape = jax.ShapeDtypeStruct((1, D_GROUP), F32)

    def blk(col):
        return pl.BlockSpec((tb, D_GROUP), lambda i: (nb - 1 - i, col))

    def halo_blk(col):
        return pl.BlockSpec((hl, D_GROUP), lambda i: (jnp.maximum((nb - 1 - i) * per - 1, 0), col))

    return pl.pallas_call(
        body, name=name, grid=(nb,),
        in_specs=[blk(1), blk(1), blk(2), halo_blk(1), halo_blk(2),
                  pl.BlockSpec((tb, D_GROUP), lambda i: (nb - 1 - i, 0)),
                  tap, vec, vec, sq,
                  pl.BlockSpec((None, D_GROUP, D_GROUP), lambda i: (layer, 0, 0))] + [ANY] * ns,
        out_specs=[pl.BlockSpec((tb, 2 * D_GROUP), lambda i: (nb - 1 - i, 0)),
                   pl.BlockSpec((1, 2 * D_GROUP), lambda i: (0, 0)),
                   tap, vec, vec, vec, sq, vec] + [ANY] * ns,
        out_shape=[jax.ShapeDtypeStruct((s, 2 * D_GROUP), BF16), jax.ShapeDtypeStruct((1, 2 * D_GROUP), F32),
                   jax.ShapeDtypeStruct((hl, D_GROUP), F32), vshape, vshape, vshape,
                   jax.ShapeDtypeStruct((D_GROUP, D_GROUP), F32), vshape] + _scatter_shapes(scatter),
        scratch_shapes=[pltpu.VMEM((hl + tb, D_GROUP), F32), pltpu.VMEM((tb + hl, D_GROUP), F32),
                        pltpu.VMEM((hl, D_GROUP), F32)] + (_scatter_sems(ns) if ns else []),
        compiler_params=_cparams(1),
    )(dmix, h_in, h_in, h_in, h_in, hc, cw, gg, gb, pmat, wpw, *scatter)


LRU_HALO = 8


def _lru_gates(xc, wr_ref, br_ref, wi_ref, bi_ref, sp_ref):
    xcb = _bf(xc)
    r = _sigmoid(_dot(xcb, wr_ref[...]) + br_ref[...])
    gi = _sigmoid(_dot(xcb, wi_ref[...]) + bi_ref[...])
    la = -LRU_C * r * sp_ref[...]
    a = jnp.exp(la)
    e2 = a * a
    sq = jnp.sqrt(-jnp.tanh(la) * (e2 + 1.0))
    return r, gi, a, e2, sq


def _rscan(a, b, tb, reverse):
    row = _rows(a.shape)
    sh = 1
    while sh < tb:
        if reverse:
            amt, mask = tb - sh, row < tb - sh
        else:
            amt, mask = sh, row >= sh
        a_s = jnp.where(mask, pltpu.roll(a, amt, 0), 1.0)
        b_s = jnp.where(mask, pltpu.roll(b, amt, 0), 0.0)
        b = b + a * b_s
        a = a * a_s
        sh *= 2
    return a, b


def lru_fwd(h_in, cw, cb, wr, br, wi, bi, sp, tb, name):
    s = h_in.shape[0]
    hl = LRU_HALO

    def body(xg_ref, xr_ref, cw_ref, cb_ref, wr_ref, br_ref, wi_ref, bi_ref, sp_ref, y_ref, xc_ref, h_ref, ext, carry):
        @pl.when(pl.program_id(0) == 0)
        def _():
            ext[0:hl, :] = jnp.zeros((hl, D_GROUP), F32)
            carry[...] = jnp.zeros_like(carry)

        ext[hl:hl + tb, :] = xr_ref[...]
        xc = jnp.zeros((tb, D_GROUP), F32) + cb_ref[...]
        for k in range(LRU_CONV_WIDTH):
            off = hl - (LRU_CONV_WIDTH - 1) + k
            xc = xc + cw_ref[k:k + 1, :] * ext[off:off + tb, :]
        xc_ref[...] = xc
        ext[0:hl, :] = ext[tb:tb + hl, :]
        r, gi, a, e2, sq = _lru_gates(xc, wr_ref, br_ref, wi_ref, bi_ref, sp_ref)
        pa, hloc = _rscan(a, sq * (gi * xc), tb, False)
        h = hloc + pa * carry[7:8, :]
        h_ref[...] = h
        carry[...] = h[tb - 8:tb]
        y_ref[...] = _bf(h * _gelu(xg_ref[...]))

    vec = pl.BlockSpec((1, D_GROUP), lambda i: (0, 0))
    sq_spec = pl.BlockSpec((D_GROUP, D_GROUP), lambda i: (0, 0))
    blk = pl.BlockSpec((tb, D_GROUP), lambda i: (i, 0))
    return pl.pallas_call(
        body, name=name, grid=(s // tb,),
        in_specs=[pl.BlockSpec((tb, D_GROUP), lambda i: (i, 3)),
                  pl.BlockSpec((tb, D_GROUP), lambda i: (i, 4)),
                  pl.BlockSpec((hl, D_GROUP), lambda i: (0, 0)),
                  vec, sq_spec, vec, sq_spec, vec, vec],
        out_specs=[blk, blk, blk],
        out_shape=[jax.ShapeDtypeStruct((s, D_GROUP), BF16), jax.ShapeDtypeStruct((s, D_GROUP), F32),
                   jax.ShapeDtypeStruct((s, D_GROUP), F32)],
        scratch_shapes=[pltpu.VMEM((hl + tb, D_GROUP), F32), pltpu.VMEM((8, D_GROUP), F32)],
        compiler_params=_cparams(1),
    )(h_in, h_in, cw, cb, wr, br, wi, bi, sp)


def lru_bwd(dmix, h_in, xcs, hs, cw, wr, br, wi, bi, sp, tb, name):
    s = h_in.shape[0]
    hl = LRU_HALO
    nb = s // tb
    per = tb // hl

    def body(dy_ref, xg_ref, xr_ref, xrh_ref, xc_ref, h_ref, hp_ref, cw_ref, wr_ref, br_ref, wi_ref, bi_ref, sp_ref,
             dx_ref, cs_ref, dcw_ref, dcb_ref, dwr_ref, dbr_ref, dwi_ref, dbi_ref, dsp_ref,
             ext, dext, head, anext, gnext):
        i = pl.program_id(0)
        first = i == 0

        @pl.when(first)
        def _():
            head[...] = jnp.zeros_like(head)
            anext[...] = jnp.zeros_like(anext)
            gnext[...] = jnp.zeros_like(gnext)

        dy = dy_ref[...]
        xg = xg_ref[...]
        xc = xc_ref[...]
        h = h_ref[...]
        r, gi, a, e2, sq = _lru_gates(xc, wr_ref, br_ref, wi_ref, bi_ref, sp_ref)
        gate, t = _gelu_parts(xg)
        dh = dy * gate
        dxg = dy * h * _gelu_grad(xg, t)
        row = _rows((tb, D_GROUP))
        coef = jnp.where(row == tb - 1, anext[0:1, :], pltpu.roll(a, tb - 1, 0))
        pc, gloc = _rscan(coef, dh, tb, True)
        gfull = gloc + pc * gnext[0:1, :]
        anext[...] = a[0:8]
        gnext[...] = gfull[0:8]
        keep = jnp.where(i == nb - 1, 0.0, 1.0)
        hprev = jnp.where(row == 0, hp_ref[7:8, :] * keep, pltpu.roll(h, 1, 0))
        da = gfull * hprev
        uu = gi * xc
        dsq = gfull * uu
        duu = gfull * sq
        dla = da * a - dsq * e2 / sq
        sp = sp_ref[...]
        dr = dla * (-LRU_C) * sp
        _acc(dsp_ref, _colsum(dla * (-LRU_C) * r), first)
        dzr = dr * r * (1.0 - r)
        dzi = duu * xc * gi * (1.0 - gi)
        dzrb, dzib = _bf(dzr), _bf(dzi)
        dxc = duu * gi + _dot_nt(dzrb, wr_ref[...]) + _dot_nt(dzib, wi_ref[...])
        xcb = _bf(xc)
        _acc(dwr_ref, _dot_tn(xcb, dzrb), first)
        _acc(dwi_ref, _dot_tn(xcb, dzib), first)
        _acc(dbr_ref, _colsum(dzr), first)
        _acc(dbi_ref, _colsum(dzi), first)
        _acc(dcb_ref, _colsum(dxc), first)
        ext[0:hl, :] = xrh_ref[...] * keep
        ext[hl:hl + tb, :] = xr_ref[...]
        dext[0:tb, :] = dxc
        dext[tb:tb + hl, :] = head[...]
        head[...] = dxc[0:hl]
        dxr = jnp.zeros((tb, D_GROUP), F32)
        for k in range(LRU_CONV_WIDTH):
            off = hl - (LRU_CONV_WIDTH - 1) + k
            _acc(dcw_ref.at[k:k + 1, :], _colsum(dxc * ext[off:off + tb, :]), first)
            back = LRU_CONV_WIDTH - 1 - k
            dxr = dxr + cw_ref[k:k + 1, :] * dext[back:back + tb, :]

        @pl.when(first)
        def _():
            dcw_ref[LRU_CONV_WIDTH:hl, :] = jnp.zeros((hl - LRU_CONV_WIDTH, D_GROUP), F32)

        dx_ref[:, 0:D_GROUP] = _bf(dxg)
        dx_ref[:, D_GROUP:2 * D_GROUP] = _bf(dxr)
        _acc(cs_ref.at[:, 0:D_GROUP], _colsum(dxg), first)
        _acc(cs_ref.at[:, D_GROUP:2 * D_GROUP], _colsum(dxr), first)

    vec = pl.BlockSpec((1, D_GROUP), lambda i: (0, 0))
    sq_spec = pl.BlockSpec((D_GROUP, D_GROUP), lambda i: (0, 0))
    tap = pl.BlockSpec((hl, D_GROUP), lambda i: (0, 0))
    vshape = jax.ShapeDtypeStruct((1, D_GROUP), F32)
    sshape = jax.ShapeDtypeStruct((D_GROUP, D_GROUP), F32)

    def blk(col):
        return pl.BlockSpec((tb, D_GROUP), lambda i: (nb - 1 - i, col))

    def halo_blk(col):
        return pl.BlockSpec((hl, D_GROUP), lambda i: (jnp.maximum((nb - 1 - i) * per - 1, 0), col))

    return pl.pallas_call(
        body, name=name, grid=(nb,),
        in_specs=[blk(2), blk(3), blk(4), halo_blk(4), blk(0), blk(0), halo_blk(0),
                  tap, sq_spec, vec, sq_spec, vec, vec],
        out_specs=[pl.BlockSpec((tb, 2 * D_GROUP), lambda i: (nb - 1 - i, 0)),
                   pl.BlockSpec((1, 2 * D_GROUP), lambda i: (0, 0)),
                   tap, vec, sq_spec, vec, sq_spec, vec, vec],
        out_shape=[jax.ShapeDtypeStruct((s, 2 * D_GROUP), BF16), jax.ShapeDtypeStruct((1, 2 * D_GROUP), F32),
                   jax.ShapeDtypeStruct((hl, D_GROUP), F32), vshape, sshape, vshape, sshape, vshape, vshape],
        scratch_shapes=[pltpu.VMEM((hl + tb, D_GROUP), F32), pltpu.VMEM((tb + hl, D_GROUP), F32),
                        pltpu.VMEM((hl, D_GROUP), F32), pltpu.VMEM((8, D_GROUP), F32), pltpu.VMEM((8, D_GROUP), F32)],
        compiler_params=_cparams(1),
    )(dmix, h_in, h_in, h_in, xcs, hs, hs, cw, wr, br, wi, bi, sp)


ATTN_HEADS = 4
ATTN_HEAD_DIM = 64
ATTN_SCALE = ATTN_HEAD_DIM ** -0.5


def _head_mask(h):
    lane = lax.broadcasted_iota(jnp.int32, (1, D_GROUP), 1)
    return jnp.where((lane >= h * ATTN_HEAD_DIM) & (lane < (h + 1) * ATTN_HEAD_DIM), 1.0, 0.0)


def _softmax_rows(sc):
    e = jnp.exp(sc - jnp.max(sc, -1, keepdims=True))
    return e / jnp.sum(e, -1, keepdims=True)


def attn_fwd(h_in, kv, tb, name):
    s = h_in.shape[0]

    def body(q_ref, kv_ref, y_ref):
        q = q_ref[...]
        kb = _bf(kv_ref[:, 0:D_GROUP])
        vb = _bf(kv_ref[:, D_GROUP:2 * D_GROUP])
        out = jnp.zeros((tb, D_GROUP), F32)
        for h in range(ATTN_HEADS):
            mask = _head_mask(h)
            p = _softmax_rows(_dot_nt(_bf(q * mask), kb) * ATTN_SCALE)
            out = out + _dot(_bf(p), vb) * mask
        y_ref[...] = _bf(out)

    return pl.pallas_call(
        body, name=name, grid=(s // tb,),
        in_specs=[pl.BlockSpec((tb, D_GROUP), lambda i: (i, 5)),
                  pl.BlockSpec((D_GROUP, 2 * D_GROUP), lambda i: (0, 0))],
        out_specs=pl.BlockSpec((tb, D_GROUP), lambda i: (i, 0)),
        out_shape=jax.ShapeDtypeStruct((s, D_GROUP), BF16),
        compiler_params=_cparams(1),
    )(h_in, kv)


def attn_bwd(dmix, h_in, kv, tb, name, exchange=()):
    s = h_in.shape[0]
    ne = len(exchange)

    def body(do_ref, q_ref, kv_ref, *rest):
        g_refs = rest[:ne]
        dq_ref, cs_ref, dkv_ref = rest[ne:ne + 3]
        got_refs, sems = rest[ne + 3:2 * ne + 3], rest[2 * ne + 3:]
        first = pl.program_id(0) == 0
        if ne:
            @pl.when(first)
            def _():
                _exchange_start(g_refs, got_refs, sems)

        q = q_ref[...]
        do = do_ref[...]
        kb = _bf(kv_ref[:, 0:D_GROUP])
        vb = _bf(kv_ref[:, D_GROUP:2 * D_GROUP])
        dq = jnp.zeros((tb, D_GROUP), F32)
        dk = jnp.zeros((D_GROUP, D_GROUP), F32)
        dv = jnp.zeros((D_GROUP, D_GROUP), F32)
        for h in range(ATTN_HEADS):
            mask = _head_mask(h)
            qm = _bf(q * mask)
            p = _softmax_rows(_dot_nt(qm, kb) * ATTN_SCALE)
            dom = _bf(do * mask)
            dp = _dot_nt(dom, vb)
            dv = dv + _dot_tn(_bf(p), dom)
            ds = _bf(p * (dp - jnp.sum(dp * p, -1, keepdims=True)) * ATTN_SCALE)
            dq = dq + _dot(ds, kb) * mask
            dk = dk + _dot_tn(ds, qm)
        dq_ref[...] = _bf(dq)
        _acc(cs_ref, _colsum(dq), first)
        _acc(dkv_ref.at[:, 0:D_GROUP], dk, first)
        _acc(dkv_ref.at[:, D_GROUP:2 * D_GROUP], dv, first)
        if ne:
            @pl.when(pl.program_id(0) == s // tb - 1)
            def _():
                _exchange_finish(g_refs, got_refs, sems)

    return pl.pallas_call(
        body, name=name, grid=(s // tb,),
        in_specs=[pl.BlockSpec((tb, D_GROUP), lambda i: (i, 3)),
                  pl.BlockSpec((tb, D_GROUP), lambda i: (i, 5)),
                  pl.BlockSpec((D_GROUP, 2 * D_GROUP), lambda i: (0, 0))] + [ANY] * ne,
        out_specs=[pl.BlockSpec((tb, D_GROUP), lambda i: (i, 0)),
                   pl.BlockSpec((1, D_GROUP), lambda i: (0, 0)),
                   pl.BlockSpec((D_GROUP, 2 * D_GROUP), lambda i: (0, 0))] + [ANY] * ne,
        out_shape=[jax.ShapeDtypeStruct((s, D_GROUP), BF16), jax.ShapeDtypeStruct((1, D_GROUP), F32),
                   jax.ShapeDtypeStruct((D_GROUP, 2 * D_GROUP), F32)] + _exchange_shapes(exchange),
        scratch_shapes=_exchange_sems(ne) if ne else [],
        compiler_params=_cparams(1),
    )(dmix, h_in, kv, *exchange)


FFN_RB = 16
FFN_UNROLL_FWD = 4
FFN_UNROLL_BWD = 2
FFN_TAP_ROWS = 8
FFN_TN = D_FF // 2


def _shift_down(cur, tail, k):
    return pltpu.roll(jnp.concatenate([tail, cur], axis=0), k, 0)[SUBLANES:]


def _shift_up(cur, head, k):
    rb = cur.shape[0]
    return pltpu.roll(jnp.concatenate([cur, head], axis=0), rb + SUBLANES - k, 0)[:rb]


def _fold8(v):
    tot = v[0:SUBLANES]
    for t in range(1, v.shape[0] // SUBLANES):
        tot = tot + v[t * SUBLANES:(t + 1) * SUBLANES]
    return tot


def _strip(r):
    return pl.ds(pl.multiple_of(r * FFN_RB, FFN_RB), FFN_RB)


def ffn_act_fwd(u, cw, cb, tb, name, gather=()):
    s = u.shape[0]
    rb = FFN_RB
    nct = D_FF // FFN_TN
    nstrip = tb // rb
    ng = len(gather)

    def body(uv_ref, ug_ref, wv_ref, wg_ref, bv_ref, bg_ref, *rest):
        hf_ref, keep_ref = rest[ng], rest[ng + 1]
        slabs = rest[ng + 2:2 * ng + 2]
        tailv, tailg = rest[2 * ng + 2], rest[2 * ng + 3]
        sems = rest[2 * ng + 4:]
        if ng:
            @pl.when((pl.program_id(0) == 0) & (pl.program_id(1) == 0))
            def _():
                _gather_start(slabs, sems[0:2])

        @pl.when(pl.program_id(1) == 0)
        def _():
            tailv[...] = jnp.zeros_like(tailv)
            tailg[...] = jnp.zeros_like(tailg)

        for cc in range(FFN_TN // LANES):
            cols = slice(cc * LANES, (cc + 1) * LANES)
            wv = [wv_ref[k:k + 1, cols] for k in range(FFN_CONV_WIDTH)]
            wg = [wg_ref[k:k + 1, cols] for k in range(FFN_CONV_WIDTH)]
            bv, bg = bv_ref[:, cols], bg_ref[:, cols]

            def strip(r, carry):
                tail_v, tail_g = carry
                cur_v, cur_g = uv_ref[_strip(r), cols], ug_ref[_strip(r), cols]
                vc = wv[0] * _shift_down(cur_v, tail_v, 2) + wv[1] * _shift_down(cur_v, tail_v, 1) + wv[2] * cur_v + bv
                gc = wg[0] * _shift_down(cur_g, tail_g, 2) + wg[1] * _shift_down(cur_g, tail_g, 1) + wg[2] * cur_g + bg
                ge, t = _gelu_parts(gc)
                hf_ref[_strip(r), cols] = _bf(vc * ge)
                keep_ref[0, _strip(r), cols] = _bf(vc)
                keep_ref[1, _strip(r), cols] = _bf(ge)
                keep_ref[2, _strip(r), cols] = _bf(_gelu_grad(gc, t))
                return cur_v[rb - SUBLANES:], cur_g[rb - SUBLANES:]

            def strips(q, carry):
                for k in range(FFN_UNROLL_FWD):
                    carry = strip(q * FFN_UNROLL_FWD + k, carry)
                return carry

            last_v, last_g = lax.fori_loop(0, nstrip // FFN_UNROLL_FWD, strips, (tailv[:, cols], tailg[:, cols]))
            tailv[:, cols] = last_v
            tailg[:, cols] = last_g

        if ng:
            @pl.when((pl.program_id(0) == nct - 1) & (pl.program_id(1) == s // tb - 1))
            def _():
                _gather_finish(slabs, sems[0:2], sems[2:4])

    return pl.pallas_call(
        body, name=name, grid=(nct, s // tb),
        in_specs=[pl.BlockSpec((tb, FFN_TN), lambda c, i: (i, c)),
                  pl.BlockSpec((tb, FFN_TN), lambda c, i: (i, c + nct)),
                  pl.BlockSpec((FFN_TAP_ROWS, FFN_TN), lambda c, i: (0, c)),
                  pl.BlockSpec((FFN_TAP_ROWS, FFN_TN), lambda c, i: (0, c + nct)),
                  pl.BlockSpec((1, FFN_TN), lambda c, i: (0, c)),
                  pl.BlockSpec((1, FFN_TN), lambda c, i: (0, c + nct))] + [ANY] * ng,
        out_specs=[pl.BlockSpec((tb, FFN_TN), lambda c, i: (i, c)),
                   pl.BlockSpec((3, tb, FFN_TN), lambda c, i: (0, i, c))] + [ANY] * ng,
        out_shape=[jax.ShapeDtypeStruct((s, D_FF), BF16), jax.ShapeDtypeStruct((3, s, D_FF), BF16)]
        + [jax.ShapeDtypeStruct(a.shape, a.dtype) for a in gather],
        input_output_aliases={6 + k: 2 + k for k in range(ng)},
        scratch_shapes=[pltpu.VMEM((SUBLANES, FFN_TN), F32), pltpu.VMEM((SUBLANES, FFN_TN), F32)]
        + (_gather_sems(ng) if ng else []),
        compiler_params=_cparams(2),
    )(u, u, cw, cw, cb, cb, *gather)


def ffn_act_bwd(dhf, u, kept, cw, tb, name, scatter=()):
    s = u.shape[0]
    rb = FFN_RB
    nct = D_FF // FFN_TN
    nb = s // tb
    nstrip = tb // rb
    ntap = FFN_CONV_WIDTH
    ns = len(scatter)

    def body(dh_ref, uv_ref, ug_ref, kept_ref, wv_ref, wg_ref, *rest):
        s_refs = rest[:ns]
        du_ref, dwv_ref, dwg_ref = rest[ns:ns + 3]
        got_refs = rest[ns + 3:2 * ns + 3]
        headv, headg = rest[2 * ns + 3], rest[2 * ns + 4]
        sems = rest[2 * ns + 5:]
        i = pl.program_id(1)
        first = i == 0
        if ns:
            @pl.when((pl.program_id(0) == 0) & first)
            def _():
                _scatter_start(s_refs, got_refs, sems)

        @pl.when(first)
        def _():
            headv[...] = jnp.zeros_like(headv)
            headg[...] = jnp.zeros_like(headg)
            dwv_ref[...] = jnp.zeros_like(dwv_ref)
            dwg_ref[...] = jnp.zeros_like(dwg_ref)

        zero = jnp.zeros((SUBLANES, LANES), F32)
        for cc in range(FFN_TN // LANES):
            cols = slice(cc * LANES, (cc + 1) * LANES)
            wv = [wv_ref[k:k + 1, cols] for k in range(ntap)]
            wg = [wg_ref[k:k + 1, cols] for k in range(ntap)]

            def strip(ii, carry):
                head_dv, head_dg, acc_v, acc_g = carry
                r = nstrip - 1 - ii
                dh = dh_ref[_strip(r), cols]
                dvc = dh * kept_ref[1, _strip(r), cols].astype(F32)
                dgc = dh * kept_ref[0, _strip(r), cols].astype(F32) * kept_ref[2, _strip(r), cols].astype(F32)
                sdv = [_shift_up(dvc, head_dv, 2), _shift_up(dvc, head_dv, 1), dvc]
                sdg = [_shift_up(dgc, head_dg, 2), _shift_up(dgc, head_dg, 1), dgc]
                cur_v, cur_g = uv_ref[_strip(r), cols], ug_ref[_strip(r), cols]
                acc_v = tuple(acc_v[k] + _fold8(cur_v * sdv[k]) for k in range(ntap)) + (acc_v[ntap] + _fold8(dvc),)
                acc_g = tuple(acc_g[k] + _fold8(cur_g * sdg[k]) for k in range(ntap)) + (acc_g[ntap] + _fold8(dgc),)
                du_v = wv[0] * sdv[0] + wv[1] * sdv[1] + wv[2] * sdv[2]
                du_g = wg[0] * sdg[0] + wg[1] * sdg[1] + wg[2] * sdg[2]
                du_ref[0, _strip(r), cols] = _bf(du_v)
                du_ref[1, _strip(r), cols] = _bf(du_g)
                return dvc[0:SUBLANES], dgc[0:SUBLANES], acc_v, acc_g

            init = (headv[:, cols], headg[:, cols], (zero,) * (ntap + 1), (zero,) * (ntap + 1))
            def strips(q, carry):
                for k in range(FFN_UNROLL_BWD):
                    carry = strip(q * FFN_UNROLL_BWD + k, carry)
                return carry

            top_dv, top_dg, acc_v, acc_g = lax.fori_loop(0, nstrip // FFN_UNROLL_BWD, strips, init)
            headv[:, cols] = top_dv
            headg[:, cols] = top_dg
            for k in range(ntap + 1):
                dwv_ref[k:k + 1, cols] += _colsum(acc_v[k])
                dwg_ref[k:k + 1, cols] += _colsum(acc_g[k])

        if ns:
            @pl.when((pl.program_id(0) == nct - 1) & (i == nb - 1))
            def _():
                _scatter_finish(s_refs, got_refs, sems)

    def blk(shift):
        return pl.BlockSpec((tb, FFN_TN), lambda c, i: (nb - 1 - i, c + shift))

    tapv = pl.BlockSpec((FFN_TAP_ROWS, FFN_TN), lambda c, i: (0, c))
    tapg = pl.BlockSpec((FFN_TAP_ROWS, FFN_TN), lambda c, i: (0, c + nct))
    return pl.pallas_call(
        body, name=name, grid=(nct, nb),
        in_specs=[blk(0), blk(0), blk(nct), pl.BlockSpec((3, tb, FFN_TN), lambda c, i: (0, nb - 1 - i, c)), tapv, tapg]
        + [ANY] * ns,
        out_specs=[pl.BlockSpec((2, tb, FFN_TN), lambda c, i: (0, nb - 1 - i, c)), tapv, tapv] + [ANY] * ns,
        out_shape=[jax.ShapeDtypeStruct((2, s, D_FF), BF16), jax.ShapeDtypeStruct((FFN_TAP_ROWS, D_FF), F32),
                   jax.ShapeDtypeStruct((FFN_TAP_ROWS, D_FF), F32)] + _scatter_shapes(scatter),
        scratch_shapes=[pltpu.VMEM((SUBLANES, FFN_TN), F32), pltpu.VMEM((SUBLANES, FFN_TN), F32)]
        + (_scatter_sems(ns) if ns else []),
        compiler_params=_cparams(2),
    )(dhf, u, u, kept, cw, cw, *scatter)


def _place():
    x, y, c = lax.axis_index("x"), lax.axis_index("y"), lax.axis_index("c")
    return x, y, c, 2 * x + y


def _chip_peer(x, y, d):
    return jnp.bitwise_xor(x, d >> 1), jnp.bitwise_xor(y, d & 1)


def _my_half(ref_rows, c):
    half = ref_rows // 2
    return pl.ds(c * half, half)


def _gather_copy(ref, part, c, sems, k, d, to):
    rows = _my_half(ref.shape[1], c)
    return pltpu.make_async_remote_copy(src_ref=ref.at[part, rows], dst_ref=ref.at[part, rows], send_sem=sems[0].at[k, d - 1],
                                        recv_sem=sems[1].at[k, d - 1], device_id=to, device_id_type=MESH)


def _gather_start(slabs, ici_sems):
    x, y, c, j = _place()
    for k, ref in enumerate(slabs):
        for d in (1, 2, 3):
            px, py = _chip_peer(x, y, d)
            _gather_copy(ref, j, c, ici_sems, k, d, (px, py, c)).start()


def _gather_finish(slabs, ici_sems, d2d_sems):
    x, y, c, j = _place()
    sib = (x, y, 1 - c)
    passed = []
    for d in (1, 2, 3):
        jd = jnp.bitwise_xor(j, d)
        for k, ref in enumerate(slabs):
            _gather_copy(ref, jd, c, ici_sems, k, d, sib).wait_recv()
            cp = _gather_copy(ref, jd, c, d2d_sems, k, d, sib)
            cp.start()
            passed.append(cp)
    for cp in passed:
        cp.wait_recv()
        cp.wait_send()
    for k, ref in enumerate(slabs):
        for d in (1, 2, 3):
            _gather_copy(ref, j, c, ici_sems, k, d, sib).wait_send()


def _gather_sems(n):
    return [pltpu.SemaphoreType.DMA((n, 3)) for _ in range(4)]


def gather_weights(slabs):
    n = len(slabs)

    def body(*refs):
        outs = refs[n:2 * n]
        sems = refs[2 * n:]
        _gather_start(outs, sems[0:2])
        _gather_finish(outs, sems[0:2], sems[2:4])

    return pl.pallas_call(
        body, name="gather_weights", in_specs=[ANY] * n, out_specs=[ANY] * n,
        out_shape=[jax.ShapeDtypeStruct(a.shape, a.dtype) for a in slabs],
        input_output_aliases={w: w for w in range(n)}, scratch_shapes=_gather_sems(n),
    )(*slabs)


def _own_slab(part, jidx):
    slab = lax.empty((N_CHIPS,) + part.shape, part.dtype)
    return lax.dynamic_update_slice_in_dim(slab, part[None], jidx, axis=0)


def _exchange_copy(g_ref, got_ref, k, sems):
    x, y, c, j = _place()
    half = g_ref.shape[1] // 2
    return pltpu.make_async_remote_copy(
        src_ref=g_ref.at[:, pl.ds((1 - c) * half, half)], dst_ref=got_ref, send_sem=sems[0].at[k],
        recv_sem=sems[1].at[k], device_id=(x, y, 1 - c), device_id_type=MESH)


def _exchange_start(g_refs, got_refs, sems):
    for k in range(len(g_refs)):
        _exchange_copy(g_refs[k], got_refs[k], k, sems).start()


def _exchange_finish(g_refs, got_refs, sems):
    for k in range(len(g_refs)):
        _exchange_copy(g_refs[k], got_refs[k], k, sems).wait()


def _exchange_sems(n):
    return [pltpu.SemaphoreType.DMA((n,)), pltpu.SemaphoreType.DMA((n,))]


def _exchange_shapes(gs):
    return [jax.ShapeDtypeStruct((g.shape[0], g.shape[1] // 2, g.shape[2]), F32) for g in gs]


def exchange_cores(gbig, name, small_all=None):
    n = len(gbig)
    with_small = small_all is not None

    def body(*refs):
        g_refs = refs[:n]
        got_refs = refs[n + with_small:2 * n + with_small]
        dsem, esem, ssem, rsem, fsem, hsem = refs[2 * (n + with_small):]
        x, y, c, j = _place()
        sib = (x, y, 1 - c)
        big = []
        for k in range(n):
            half = g_refs[k].shape[1] // 2
            cp = pltpu.make_async_remote_copy(
                src_ref=g_refs[k].at[:, pl.ds((1 - c) * half, half)], dst_ref=got_refs[k], send_sem=dsem.at[k],
                recv_sem=esem.at[k], device_id=sib, device_id_type=MESH)
            cp.start()
            big.append(cp)
        if with_small:
            all_ref = refs[2 * n + 1]
            me = 4 * x + 2 * y + c

            def small_copy(k, block, to, sems):
                return pltpu.make_async_remote_copy(
                    src_ref=all_ref.at[block], dst_ref=all_ref.at[block],
                    send_sem=sems[0].at[k], recv_sem=sems[1].at[k], device_id=to, device_id_type=MESH)

            first = [small_copy(0, me, sib, (ssem, rsem))]
            for d in (1, 2, 3):
                px, py = _chip_peer(x, y, d)
                first.append(small_copy(d, me, (px, py, c), (ssem, rsem)))
            for cp in first:
                cp.start()
            passed = []
            for d in (1, 2, 3):
                px, py = _chip_peer(x, y, d)
                src_block = 4 * px + 2 * py + c
                small_copy(d, src_block, sib, (ssem, rsem)).wait_recv()
                cp = small_copy(d - 1, src_block, sib, (fsem, hsem))
                cp.start()
                passed.append(cp)
            small_copy(0, me, sib, (ssem, rsem)).wait_recv()
            for cp in passed:
                cp.wait_recv()
            for cp in first + passed:
                cp.wait_send()
        for cp in big:
            cp.wait()

    ops = list(gbig) + ([small_all] if with_small else [])
    out_shape = [jax.ShapeDtypeStruct((g.shape[0], g.shape[1] // 2, g.shape[2]), F32) for g in gbig]
    aliases = {}
    if with_small:
        out_shape.append(jax.ShapeDtypeStruct(small_all.shape, F32))
        aliases = {n: n}
    return pl.pallas_call(
        body, name=name, in_specs=[ANY] * len(ops), out_specs=[ANY] * len(out_shape), out_shape=out_shape,
        input_output_aliases=aliases,
        scratch_shapes=[pltpu.SemaphoreType.DMA((n,)), pltpu.SemaphoreType.DMA((n,)),
                        pltpu.SemaphoreType.DMA((4,)), pltpu.SemaphoreType.DMA((4,)),
                        pltpu.SemaphoreType.DMA((3,)), pltpu.SemaphoreType.DMA((3,))],
    )(*ops)


def _scatter_copy(s_ref, got_ref, k, d, sems):
    x, y, c, j = _place()
    px, py = _chip_peer(x, y, d)
    return pltpu.make_async_remote_copy(
        src_ref=s_ref.at[jnp.bitwise_xor(j, d)], dst_ref=got_ref.at[d - 1], send_sem=sems[0].at[k, d - 1],
        recv_sem=sems[1].at[k, d - 1], device_id=(px, py, c), device_id_type=MESH)


def _scatter_start(s_refs, got_refs, sems):
    for d in (1, 2, 3):
        for k in range(len(s_refs)):
            _scatter_copy(s_refs[k], got_refs[k], k, d, sems).start()


def _scatter_finish(s_refs, got_refs, sems):
    for d in (1, 2, 3):
        for k in range(len(s_refs)):
            _scatter_copy(s_refs[k], got_refs[k], k, d, sems).wait()


def _scatter_sems(n):
    return [pltpu.SemaphoreType.DMA((n, 3)), pltpu.SemaphoreType.DMA((n, 3))]


def _scatter_shapes(s1):
    return [jax.ShapeDtypeStruct((3,) + a.shape[1:], a.dtype) for a in s1]


def scatter_shards(s1, name):
    n = len(s1)

    def body(*refs):
        _scatter_start(refs[:n], refs[n:2 * n], refs[2 * n:])
        _scatter_finish(refs[:n], refs[n:2 * n], refs[2 * n:])

    return pl.pallas_call(
        body, name=name, in_specs=[ANY] * n, out_specs=[ANY] * n, out_shape=_scatter_shapes(s1),
        scratch_shapes=_scatter_sems(n),
    )(*s1)


def share_with_sibling(parts):
    n = len(parts)

    def body(*refs):
        out_refs = refs[n:2 * n]
        ssem, rsem = refs[2 * n:]
        x, y, c, j = _place()
        cps = []
        for k in range(n):
            rows = _my_half(out_refs[k].shape[1], c)
            for l in range(DEPTH):
                cp = pltpu.make_async_remote_copy(
                    src_ref=out_refs[k].at[l, rows], dst_ref=out_refs[k].at[l, rows], send_sem=ssem.at[k, l],
                    recv_sem=rsem.at[k, l], device_id=(x, y, 1 - c), device_id_type=MESH)
                cp.start()
                cps.append(cp)
        for cp in cps:
            cp.wait()

    return pl.pallas_call(
        body, name="share_with_sibling", in_specs=[ANY] * n, out_specs=[ANY] * n,
        out_shape=[jax.ShapeDtypeStruct(a.shape, F32) for a in parts],
        input_output_aliases={k: k for k in range(n)},
        scratch_shapes=[pltpu.SemaphoreType.DMA((n, DEPTH)), pltpu.SemaphoreType.DMA((n, DEPTH))],
    )(*parts)


def add_core_halves(g, got, cidx, name):
    nch, r, cdim = g.shape
    half = r // 2
    tr = _row_tile(half, cdim, mult=16)
    per = half // tr

    def body(c_ref, a_ref, b_ref, o_ref):
        o_ref[...] = _bf(a_ref[...] + b_ref[...])

    grid_spec = pltpu.PrefetchScalarGridSpec(
        num_scalar_prefetch=1, grid=(nch, per),
        in_specs=[pl.BlockSpec((None, tr, cdim), lambda jj, i, c_ref: (jj, c_ref[0] * per + i, 0)),
                  pl.BlockSpec((None, tr, cdim), lambda jj, i, c_ref: (jj, i, 0))],
        out_specs=pl.BlockSpec((None, tr, cdim), lambda jj, i, c_ref: (jj, i, 0)))
    return pl.pallas_call(
        body, name=name, grid_spec=grid_spec,
        out_shape=jax.ShapeDtypeStruct((nch, half, cdim), BF16), compiler_params=_cparams(2),
    )(cidx, g, got)


def add_chip_parts(s1, got, jc, layer, into, name):
    _, half, cdim = s1.shape
    tr = _row_tile(half, cdim, mult=16)
    per = half // tr

    def body(jc_ref, a_ref, g0_ref, g1_ref, g2_ref, *rest):
        rest[-1][...] = ((a_ref[...].astype(F32) + g0_ref[...].astype(F32)) + g1_ref[...].astype(F32)) + g2_ref[...].astype(F32)

    def slot(k):
        return pl.BlockSpec((None, tr, cdim), lambda i, jc_ref: (k, i, 0))

    in_specs = [pl.BlockSpec((None, tr, cdim), lambda i, jc_ref: (jc_ref[0], i, 0)), slot(0), slot(1), slot(2)]
    ops = [jc, s1, got, got, got]
    aliases = {}
    if into is not None:
        in_specs.append(ANY)
        ops.append(into)
        aliases = {5: 0}
    grid_spec = pltpu.PrefetchScalarGridSpec(
        num_scalar_prefetch=1, grid=(per,), in_specs=in_specs,
        out_specs=pl.BlockSpec((None, tr, cdim), lambda i, jc_ref: (layer, jc_ref[1] * per + i, 0)))
    return pl.pallas_call(
        body, name=name, grid_spec=grid_spec, input_output_aliases=aliases,
        out_shape=jax.ShapeDtypeStruct((DEPTH, 2 * half, cdim), F32), compiler_params=_cparams(1),
    )(*ops)


def sum_devices(allp):
    _, r, _ = allp.shape

    def body(a_ref, o_ref):
        tot = a_ref[0]
        for k in range(1, 8):
            tot = tot + a_ref[k]
        o_ref[...] = tot

    tr = r // 2 if r % 16 == 0 else r
    return pl.pallas_call(
        body, name="sum_devices", grid=(r // tr,),
        in_specs=[pl.BlockSpec((8, tr, LANES), lambda i: (0, i, 0))],
        out_specs=pl.BlockSpec((tr, LANES), lambda i: (i, 0)),
        out_shape=jax.ShapeDtypeStruct((r, LANES), F32), compiler_params=_cparams(1),
    )(allp)


def _row_tile(r, cdim, limit_bytes=1 << 20, mult=8):
    best = None
    for tr in range(mult, r + 1, mult):
        if r % tr == 0 and tr * cdim * 4 <= limit_bytes:
            best = tr
    return best if best is not None else r


def adamw(w, g, m, v, name):
    r, cdim = w.shape
    tr = _row_tile(r, cdim)
    bc1 = 1.0 - ADAM_B1 ** ADAM_STEP
    bc2 = 1.0 - ADAM_B2 ** ADAM_STEP

    def body(w_ref, g_ref, m_ref, v_ref, d_ref, nm_ref, nv_ref, go_ref):
        gv = g_ref[...]
        nm = ADAM_B1 * m_ref[...] + (1.0 - ADAM_B1) * gv
        nv = ADAM_B2 * v_ref[...] + (1.0 - ADAM_B2) * (gv * gv)
        d_ref[...] = -ADAM_LR * ((nm / bc1) / (jnp.sqrt(nv / bc2) + ADAM_EPS) + ADAM_WD * w_ref[...])
        nm_ref[...] = nm
        nv_ref[...] = nv
        go_ref[...] = gv

    blk = pl.BlockSpec((tr, cdim), lambda i: (i, 0))
    shape = jax.ShapeDtypeStruct((r, cdim), F32)
    return pl.pallas_call(
        body, name=name, grid=(r // tr,), in_specs=[blk] * 4, out_specs=[blk] * 4, out_shape=[shape] * 4,
        compiler_params=_cparams(1),
    )(w, g, m, v)


def _s5_prepare(lam_re, lam_im, log_dt, b_re, b_im, c_re, c_im):
    groups, ch = 16, 16
    dt = jnp.exp(log_dt)[:, None]
    mag = jnp.exp(lam_re * dt)
    a_r, a_i = mag * jnp.cos(lam_im * dt), mag * jnp.sin(lam_im * dt)
    den = lam_re * lam_re + lam_im * lam_im
    q_r = ((a_r - 1.0) * lam_re + a_i * lam_im) / den
    q_i = (a_i * lam_re - (a_r - 1.0) * lam_im) / den
    bb_r = q_r[..., None] * b_re - q_i[..., None] * b_im
    bb_i = q_r[..., None] * b_im + q_i[..., None] * b_re
    eye = jnp.eye(groups, dtype=F32)

    def expand_b(bb):
        return jnp.einsum("gpc,gh->gchp", bb, eye).reshape(groups * ch, N_STATE)

    def expand_c(cc):
        return jnp.einsum("gcp,gh->hpgc", cc, eye).reshape(N_STATE, groups * ch)

    a2 = jnp.concatenate([a_r.reshape(1, N_STATE), a_i.reshape(1, N_STATE)], axis=1)
    bexp = jnp.concatenate([expand_b(bb_r), expand_b(bb_i)], axis=1)
    cexp = jnp.concatenate([expand_c(c_re), -expand_c(c_im)], axis=0)
    return a2, bexp, cexp


def _lru_prepare(w_r, w_i, lam):
    heads = 4
    eye = jnp.eye(heads, dtype=F32)

    def expand(w):
        return jnp.einsum("hij,hk->hikj", w, eye).reshape(D_GROUP, D_GROUP)

    return expand(w_r), expand(w_i), jax.nn.softplus(-lam).reshape(1, D_GROUP)


def _pad_rows(a, rows):
    return jnp.pad(a, ((0, rows - a.shape[0]), (0, 0)))


def _group_mean_matrix():
    gidx = jnp.arange(D_GROUP) // 64
    return (gidx[:, None] == gidx[None, :]).astype(BF16) * jnp.asarray(1.0 / 64.0, BF16)


def _pack_rows(arrs, width):
    parts = []
    for a in arrs:
        flat = a.reshape(-1)
        pad = (-flat.shape[0]) % width
        parts.append(jnp.pad(flat, (0, pad)) if pad else flat)
    flat = jnp.concatenate(parts)
    rows = flat.shape[0] // width
    pad_rows = (-rows) % 16
    if pad_rows:
        flat = jnp.pad(flat, (0, pad_rows * width))
    return flat.reshape(-1, width)


def _unpack_rows(packed, shapes, width):
    flat = packed.reshape(-1)
    out, off = [], 0
    for shp in shapes:
        size = math.prod(shp)
        out.append(flat[off:off + size].reshape(shp))
        off += size + ((-size) % width)
    return out


def kernel(x, mem, ln_in_g, ln_in_b, w_in, b_in, s5_lam_re, s5_lam_im, s5_log_dt, s5_b_re, s5_b_im, s5_c_re, s5_c_im, s5_d, s5_w_glu, s5_b_glu, cv_w, cv_b, cv_gn_g, cv_gn_b, cv_w_pw, cv_b_pw, lru_conv_w, lru_conv_b, lru_w_r, lru_b_r, lru_w_i, lru_b_i, lru_lam, attn_w_kv, w_out, b_out, ln1_g, ln1_b, ffn_w_up, ffn_conv_w, ffn_conv_b, ffn_w_down, ln2_g, ln2_b, loss_target, m_ln_in_g, m_ln_in_b, m_w_in, m_b_in, m_s5_lam_re, m_s5_lam_im, m_s5_log_dt, m_s5_b_re, m_s5_b_im, m_s5_c_re, m_s5_c_im, m_s5_d, m_s5_w_glu, m_s5_b_glu, m_cv_w, m_cv_b, m_cv_gn_g, m_cv_gn_b, m_cv_w_pw, m_cv_b_pw, m_lru_conv_w, m_lru_conv_b, m_lru_w_r, m_lru_b_r, m_lru_w_i, m_lru_b_i, m_lru_lam, m_attn_w_kv, m_w_out, m_b_out, m_ln1_g, m_ln1_b, m_ffn_w_up, m_ffn_conv_w, m_ffn_conv_b, m_ffn_w_down, m_ln2_g, m_ln2_b, v_ln_in_g, v_ln_in_b, v_w_in, v_b_in, v_s5_lam_re, v_s5_lam_im, v_s5_log_dt, v_s5_b_re, v_s5_b_im, v_s5_c_re, v_s5_c_im, v_s5_d, v_s5_w_glu, v_s5_b_glu, v_cv_w, v_cv_b, v_cv_gn_g, v_cv_gn_b, v_cv_w_pw, v_cv_b_pw, v_lru_conv_w, v_lru_conv_b, v_lru_w_r, v_lru_b_r, v_lru_w_i, v_lru_b_i, v_lru_lam, v_attn_w_kv, v_w_out, v_b_out, v_ln1_g, v_ln1_b, v_ffn_w_up, v_ffn_conv_w, v_ffn_conv_b, v_ffn_w_down, v_ln2_g, v_ln2_b):
    p = dict(locals())
    xs = x[0]
    mems = mem[0]
    target = loss_target[0]
    s = xs.shape[0]
    cidx = lax.axis_index("c")
    jidx = 2 * lax.axis_index("x") + lax.axis_index("y")
    tb_scan = min(256, s)
    tb_s5 = min(512, s)
    tb_attn = min(512, s)
    tb_ffn = min(256, s)

    small_sh_names = list(SMALL_SHARDED)
    small_sh_shapes = [p[nm].shape[1:] for nm in small_sh_names]
    slabs = [[_own_slab(_bf(p[nm][l]), jidx) for nm in BIG]
             + [_own_slab(_pack_rows([p[nm][l] for nm in small_sh_names], LANES), jidx)] for l in range(DEPTH)]
    n_slabs = len(BIG) + 1
    first_needed = FIRST_NEEDED + [len(BIG)]
    arrive_later = [k for k in range(n_slabs) if k not in first_needed]
    gathered = [[None] * n_slabs for _ in range(DEPTH)]
    for k, slab in zip(first_needed, gather_weights([slabs[0][k] for k in first_needed])):
        gathered[0][k] = slab

    def weight_views(gw, which):
        shapes = {0: (1, N_CHIPS, D_MODEL, N_IN // N_CHIPS), 1: (1, 1, D_MODEL, 2 * D_GROUP), 2: (1, 1, D_MODEL, D_MODEL),
                  3: (1, N_CHIPS, D_MODEL, 2 * D_FF // N_CHIPS), 4: (1, 1, D_FF, D_MODEL),
                  5: (1, D_GROUP, D_GROUP), 6: (1, D_GROUP, D_GROUP)}
        views = {BIG[k]: gw[k].reshape(shapes[k]) for k in which if k < len(BIG)}
        if len(BIG) in which:
            per_chip = [_unpack_rows(gw[len(BIG)][jj], small_sh_shapes, LANES) for jj in range(N_CHIPS)]
            for k, nm in enumerate(small_sh_names):
                views[nm] = jnp.concatenate([per_chip[jj][k] for jj in range(N_CHIPS)], axis=SMALL_SHARDED[nm] - 1)
        return views

    pmat = _group_mean_matrix()

    def vec(a):
        return a.reshape(1, -1)

    xh0, rs0, xb0 = ln_fwd(xs, vec(ln_in_g), vec(ln_in_b), "ln_in")
    saved = []
    prev = dict(xh=xh0, rs=rs0, xb=xb0, g=vec(ln_in_g), b=vec(ln_in_b))
    for l in range(DEPTH):
        sv = dict(prev=prev)
        (a2, bexp, cexp), sv['s5_vjp'] = jax.vjp(_s5_prepare, s5_lam_re[l], s5_lam_im[l], s5_log_dt[l],
                                                 s5_b_re[l], s5_b_im[l], s5_c_re[l], s5_c_im[l])
        (wr, wi, sp), sv['lru_vjp'] = jax.vjp(_lru_prepare, lru_w_r[l], lru_w_i[l], lru_lam[l])
        sv.update(a2=a2, bexp=_bf(bexp), cexp=_bf(cexp), wr=_bf(wr), wi=_bf(wi), sp=sp)
        late = arrive_later if l == 0 else []
        gw = sv['gw'] = weight_views(gathered[l], [k for k in range(n_slabs) if k not in late])
        sv['cvw'] = _pad_rows(gw['cv_w'], CV_HALO)
        sv['lcw'] = _pad_rows(gw['lru_conv_w'], LRU_HALO)
        sv['fcw'] = _pad_rows(gw['ffn_conv_w'], FFN_TAP_ROWS)
        sv['w_glu'], sv['w_pw'] = gw['s5_w_glu'], gw['cv_w_pw']
        h_in = mm_nn(prev['xb'], gw['w_in'], 0, vec(b_in[l]), F32, 2048, f"in_proj{l}")
        kv = mm_nn(mems, gw['attn_w_kv'], 0, jnp.zeros((1, 2 * D_GROUP), F32), F32, 256, f"kv_proj{l}")
        y_s5, hst, y0, *got = s5_fwd(h_in, a2, sv['bexp'], sv['cexp'], vec(s5_d[l]), sv['w_glu'], 0, vec(s5_b_glu[l]),
                                     tb_s5, f"s5_fwd{l}", gather=[slabs[l][k] for k in late])
        for k, slab in zip(late, got):
            gathered[l][k] = slab
        gw.update(weight_views(gathered[l], late))
        y_cv, hc = cv_fwd(h_in, sv['cvw'], vec(cv_b[l]), vec(cv_gn_g[l]), vec(cv_gn_b[l]), pmat, sv['w_pw'], 0,
                          vec(cv_b_pw[l]), tb_scan, f"cv_fwd{l}")
        y_lru, xcs, hls = lru_fwd(h_in, sv['lcw'], vec(lru_conv_b[l]), sv['wr'], vec(lru_b_r[l]), sv['wi'],
                                  vec(lru_b_i[l]), sp, tb_scan, f"lru_fwd{l}")
        y_mem = attn_fwd(h_in, kv, tb_attn, f"attn_fwd{l}")
        mix_in = jnp.concatenate([y_s5, y_cv, y_lru, y_mem], axis=1)
        xh1, rs1, xb1 = proj_ln(mix_in, gw['w_out'].reshape(1, D_MODEL, D_MODEL), 0, vec(b_out[l]),
                                prev['xh'], prev['g'], prev['b'], vec(ln1_g[l]), vec(ln1_b[l]), f"out_proj_ln{l}")
        u = mm_nn(xb1, gw['ffn_w_up'], 0, jnp.zeros((1, 2 * D_FF), F32), F32, 1024, f"ffn_up{l}")
        nxt = slabs[l + 1] if l + 1 < DEPTH else ()
        hf, sv['ffn_kept'], *got = ffn_act_fwd(u, sv['fcw'], vec(ffn_conv_b[l]), tb_ffn, f"ffn_act{l}", gather=nxt)
        if nxt:
            gathered[l + 1] = got
        xh2, rs2, xb2 = proj_ln(hf, gw['ffn_w_down'].reshape(1, D_FF, D_MODEL), 0, jnp.zeros((1, D_MODEL), F32),
                                xh1, vec(ln1_g[l]), vec(ln1_b[l]), vec(ln2_g[l]), vec(ln2_b[l]), f"ffn_down_ln{l}")
        sv.update(h_in=h_in, kv=kv, hst=hst, y0=y0, hc=hc, xcs=xcs, hls=hls, mix_in=mix_in,
                  xh1=xh1, rs1=rs1, xb1=xb1, u=u, hf=hf, xh2=xh2, rs2=rs2)
        saved.append(sv)
        prev = dict(xh=xh2, rs=rs2, xb=xb2, g=vec(ln2_g[l]), b=vec(ln2_b[l]))

    grads = {}
    per_layer = {nm: [None] * DEPTH for nm in WEIGHTS if nm not in ('ln_in_g', 'ln_in_b')}
    c1 = cidx.reshape(1).astype(jnp.int32)
    jc = jnp.stack([jidx, cidx]).astype(jnp.int32)
    red_big = [None] * len(BIG)
    handoff = None
    pending = None
    below = None

    def per_chip(gl, which):
        return [gl[k].reshape((N_CHIPS,) + p[BIG[k]].shape[1:]) for k in which]

    def core_sums(gs, got, which, tag):
        return [add_core_halves(g, ga, c1, f"add_cores_{BIG[k]}{tag}") for g, ga, k in zip(gs, got, which)]

    def chip_parts(gl, which, tag, small_all=None):
        gs = per_chip(gl, which)
        got = exchange_cores(gs, f"exchange_cores{tag}", small_all)
        return core_sums(gs, got, which, tag), (got[len(which)] if small_all is not None else None)

    def own_sum(which, parts, got, l):
        for k, part, gk in zip(which, parts, got):
            red_big[k] = add_chip_parts(part, gk, jc, l, red_big[k], f"add_chips_{BIG[k]}{l}")

    everything = list(range(len(BIG)))
    ready_early = [2, 3, 4]
    ready_last = [k for k in everything if k not in ready_early]

    for l in reversed(range(DEPTH)):
        sv = saved[l]
        pv = sv['prev']
        gw = sv['gw']
        gl = [None] * len(BIG)
        if l == DEPTH - 1:
            dr2, dg2, db2, sqerr = loss_ln_bwd(target, sv['xh2'], sv['rs2'], vec(ln2_g[l]), vec(ln2_b[l]), "loss_ln2_bwd")
            loss_local = 0.5 / D_MODEL * jnp.sum(sqerr)
        else:
            dr2, dg2, db2 = below
        per_layer['ln2_g'][l], per_layer['ln2_b'][l] = dg2[0], db2[0]
        tm_nt = min(512, s)
        ts_big = min(2048, s)
        whole = lambda a_ref, j: a_ref[...]
        dhf = mm_nt(dr2, (tm_nt, D_MODEL), lambda i: (i, 0), whole, gw['ffn_w_down'], 0, None, F32, 512, s, f"ffn_down_dx{l}")
        above = per_chip(*handoff) if handoff is not None else []
        res = mm_tn(sv['hf'], dr2, (min(1024, s), D_MODEL), lambda kt, j, st: (st, 0), 1, D_MODEL, FFN_TN, 1024, s,
                    f"ffn_down_dw{l}", exchange=above)
        gl[4] = res[0] if above else res
        if handoff is not None:
            pending = (l + 1, handoff[1], core_sums(above, res[1:], handoff[1], str(l + 1)))
            handoff = None
        waiting = pending[2] if pending is not None else ()
        du, dcwv, dcwg, *got = ffn_act_bwd(dhf, sv['u'], sv['ffn_kept'], sv['fcw'], tb_ffn, f"ffn_act_bwd{l}",
                                           scatter=waiting)
        if pending is not None:
            own_sum(pending[1], pending[2], got, pending[0])
            pending = None
        dcw = jnp.concatenate([dcwv, dcwg], axis=1)
        per_layer['ffn_conv_w'][l] = dcw[0:FFN_CONV_WIDTH]
        per_layer['ffn_conv_b'][l] = dcw[FFN_CONV_WIDTH]
        dr1, dg1, db1, cs1 = mm_nt(du, (2, tm_nt, D_FF), lambda i: (0, i, 0),
                                   lambda a_ref, j: a_ref[j // 2, :, (j % 2) * FFN_TN:(j % 2 + 1) * FFN_TN], gw['ffn_w_up'], 0, dr2,
                                   F32, 512, s, f"ffn_up_dx_ln1_bwd{l}", ln=(sv['xh1'], sv['rs1'], vec(ln1_g[l])))
        gl[3] = mm_tn(sv['xb1'], du, (None, ts_big, FFN_TN), lambda kt, j, st: (j // 2, st, j % 2), N_CHIPS, FFN_TN,
                      D_MODEL, 2048, s, f"ffn_up_dw{l}")
        per_layer['ln1_g'][l], per_layer['ln1_b'][l], per_layer['b_out'][l] = dg1[0], db1[0], cs1[0]
        dmix = mm_nt(dr1, (tm_nt, D_MODEL), lambda i: (i, 0), whole, gw['w_out'], 0, None, F32, 512, s, f"out_proj_dx{l}")
        gl[2] = mm_tn(sv['mix_in'], dr1, (min(1024, s), D_MODEL), lambda kt, j, st: (st, 0), 1, D_MODEL, D_MODEL, 1024, s,
                      f"out_proj_dw{l}")
        h_in = sv['h_in']
        early_gs = per_chip(gl, ready_early) if l == 0 else []
        d_q, cs_q, d_kv, *got = attn_bwd(dmix, h_in, sv['kv'], tb_attn, f"attn_bwd{l}", exchange=early_gs)
        early = core_sums(early_gs, got, ready_early, f"{l}a") if l == 0 else ()
        (d_u, cs_u, d_bexp, d_cexp, d_dd, d_wglu, d_bglu, d_a2) = s5_bwd(
            dmix, h_in, sv['y0'], sv['hst'], sv['a2'], sv['bexp'], sv['cexp'], vec(s5_d[l]), sv['w_glu'], 0,
            vec(s5_b_glu[l]), tb_s5, f"s5_bwd{l}")
        (d_vg, cs_vg, d_cvw, d_cvb, d_gg, d_gb, d_wpw, d_bpw, *got) = cv_bwd(
            dmix, h_in, sv['hc'], sv['cvw'], vec(cv_gn_g[l]), vec(cv_gn_b[l]), pmat, sv['w_pw'], 0, tb_scan, f"cv_bwd{l}",
            scatter=early)
        if l == 0:
            own_sum(ready_early, early, got, l)
        (d_lx, cs_lx, d_lcw, d_lcb, d_wr, d_br, d_wi, d_bi, d_sp) = lru_bwd(
            dmix, h_in, sv['xcs'], sv['hls'], sv['lcw'], sv['wr'], vec(lru_b_r[l]), sv['wi'], vec(lru_b_i[l]),
            sv['sp'], tb_scan, f"lru_bwd{l}")
        g_s5 = sv['s5_vjp']((d_a2, d_bexp, d_cexp))
        for nm, gval in zip(['s5_lam_re', 's5_lam_im', 's5_log_dt', 's5_b_re', 's5_b_im', 's5_c_re', 's5_c_im'], g_s5):
            per_layer[nm][l] = gval
        g_lru = sv['lru_vjp']((d_wr, d_wi, d_sp))
        for nm, gval in zip(['lru_w_r', 'lru_w_i', 'lru_lam'], g_lru):
            per_layer[nm][l] = gval
        per_layer['s5_d'][l], per_layer['s5_b_glu'][l] = d_dd[0], d_bglu[0]
        per_layer['cv_w'][l], per_layer['cv_b'][l] = d_cvw[0:CONV_WIDTH], d_cvb[0]
        per_layer['cv_gn_g'][l], per_layer['cv_gn_b'][l] = d_gg[0], d_gb[0]
        per_layer['cv_b_pw'][l] = d_bpw[0]
        gl[5], gl[6] = d_wglu, d_wpw
        per_layer['lru_conv_w'][l], per_layer['lru_conv_b'][l] = d_lcw[0:LRU_CONV_WIDTH], d_lcb[0]
        per_layer['lru_b_r'][l], per_layer['lru_b_i'][l] = d_br[0], d_bi[0]
        per_layer['b_in'][l] = jnp.concatenate([cs_u, cs_vg, cs_lx, cs_q], axis=1)[0]
        gl[1] = mm_tn(mems, d_kv, (MEM_ROWS, 2 * D_GROUP), lambda kt, j, st: (st, 0), 1, 2 * D_GROUP, D_MODEL, MEM_ROWS,
                      MEM_ROWS, f"kv_proj_dw{l}")
        dh_in = jnp.concatenate([d_u, d_vg, d_lx, d_q], axis=1)
        n_sh = N_IN // N_CHIPS
        below = mm_nt(dh_in, (tm_nt, N_IN), lambda i: (i, 0), lambda a_ref, j: a_ref[:, j * n_sh:(j + 1) * n_sh],
                      gw['w_in'], 0, dr1, F32, 512, s, f"in_proj_dx_ln_bwd{l}", ln=(pv['xh'], pv['rs'], pv['g']))[:3]
        gl[0] = mm_tn(pv['xb'], dh_in, (ts_big, n_sh), lambda kt, j, st: (st, j), N_CHIPS, n_sh, D_MODEL, 2048, s,
                      f"in_proj_dw{l}")
        if l > 0:
            handoff = (gl, everything)
    grad_x, dg_in, db_in = below
    grads['ln_in_g'], grads['ln_in_b'] = dg_in[0], db_in[0]
    for nm, vals in per_layer.items():
        if nm not in BIG:
            grads[nm] = jnp.stack(vals)

    small_names = [nm for nm in WEIGHTS if nm not in BIG]
    small_local = _pack_rows([grads[nm] for nm in small_names], LANES)
    me = 4 * lax.axis_index("x") + 2 * lax.axis_index("y") + cidx
    small_slab = lax.dynamic_update_slice_in_dim(lax.empty((8,) + small_local.shape, F32), small_local[None], me, axis=0)
    parts, small_all = chip_parts(gl, ready_last, "0b", small_slab)
    own_sum(ready_last, parts, scatter_shards(parts, "scatter_shards0"), 0)
    red_big = share_with_sibling(red_big)
    small_red = sum_devices(small_all)
    small_grads = dict(zip(small_names, _unpack_rows(small_red, [grads[nm].shape for nm in small_names], LANES)))

    out_g, out_d, out_m, out_v = {}, {}, {}, {}
    for k, nm in enumerate(BIG):
        gk = red_big[k]
        two_d = (-1, p[nm].shape[-1])
        res = adamw(p[nm].reshape(two_d), gk.reshape(two_d), p['m_' + nm].reshape(two_d),
                    p['v_' + nm].reshape(two_d), f"adamw_{nm}")
        out_d[nm], out_m[nm], out_v[nm], out_g[nm] = (t.reshape(p[nm].shape) for t in res)
    own = {}
    for nm in small_names:
        gfull = small_grads[nm]
        if nm in SMALL_SHARDED:
            ax = SMALL_SHARDED[nm]
            width = p[nm].shape[ax]
            gfull = lax.dynamic_slice_in_dim(gfull, jidx * width, width, axis=ax)
        own[nm] = gfull
    packs = [_pack_rows([src[nm] for nm in small_names], LANES)
             for src in (dict((nm, p[nm]) for nm in small_names), own,
                         dict((nm, p['m_' + nm]) for nm in small_names), dict((nm, p['v_' + nm]) for nm in small_names))]
    dlt, nm_, nv_, _ = adamw(packs[0], packs[1], packs[2], packs[3], "adamw_small")
    shapes = [p[nm].shape for nm in small_names]
    for dst, packed in ((out_d, dlt), (out_m, nm_), (out_v, nv_)):
        dst.update(zip(small_names, _unpack_rows(packed, shapes, LANES)))
    out_g.update(own)

    loss = lax.psum(loss_local, ("x", "y", "c"))
    return (loss, grad_x[None], *[out_g[nm] for nm in WEIGHTS], *[out_d[nm] for nm in WEIGHTS],
            *[out_m[nm] for nm in WEIGHTS], *[out_v[nm] for nm in WEIGHTS])
```

```python
import functools
import math

import jax
import jax.numpy as jnp
from jax import lax
from jax.experimental import pallas as pl
from jax.experimental.pallas import tpu as pltpu

F32 = jnp.float32
BF16 = jnp.bfloat16
MESH = pl.DeviceIdType.MESH
ANY = pl.BlockSpec(memory_space=pl.ANY)

DEPTH = 2
D_MODEL = 1024
D_GROUP = 256
N_IN = 6 * D_GROUP
D_FF = 2816
N_STATE = 1024
CONV_WIDTH = 31
LRU_CONV_WIDTH = 4
FFN_CONV_WIDTH = 3
LRU_C = 8.0
ALPHA = (2 * DEPTH) ** 0.25
LN_EPS = 1e-5
N_CHIPS = 4
MEM_ROWS = 256
LANES = 128
SUBLANES = 8
VMEM_LIMIT = 56 * 1024 * 1024

ADAM_LR, ADAM_B1, ADAM_B2, ADAM_EPS, ADAM_WD, ADAM_STEP = 0.001, 0.9, 0.999, 1e-08, 0.01, 10

WEIGHTS = ['ln_in_g', 'ln_in_b', 'w_in', 'b_in', 's5_lam_re', 's5_lam_im', 's5_log_dt', 's5_b_re', 's5_b_im',
           's5_c_re', 's5_c_im', 's5_d', 's5_w_glu', 's5_b_glu', 'cv_w', 'cv_b', 'cv_gn_g', 'cv_gn_b', 'cv_w_pw',
           'cv_b_pw', 'lru_conv_w', 'lru_conv_b', 'lru_w_r', 'lru_b_r', 'lru_w_i', 'lru_b_i', 'lru_lam',
           'attn_w_kv', 'w_out', 'b_out', 'ln1_g', 'ln1_b', 'ffn_w_up', 'ffn_conv_w', 'ffn_conv_b', 'ffn_w_down',
           'ln2_g', 'ln2_b']
BIG = ['w_in', 'attn_w_kv', 'w_out', 'ffn_w_up', 'ffn_w_down', 's5_w_glu', 'cv_w_pw']
FIRST_NEEDED = [0, 1, 5, 6]
SMALL_SHARDED = {'cv_w': 2, 'lru_conv_w': 2, 'ffn_conv_w': 2}


def _cparams(n_axes):
    return pltpu.CompilerParams(dimension_semantics=("arbitrary",) * n_axes, vmem_limit_bytes=VMEM_LIMIT)


def _dot(a, b):
    return jnp.dot(a, b, preferred_element_type=F32)


def _dot_nt(a, b):
    return lax.dot_general(a, b, (((1,), (1,)), ((), ())), preferred_element_type=F32)


def _dot_tn(a, b):
    return lax.dot_general(a, b, (((0,), (0,)), ((), ())), preferred_element_type=F32)


def _bf(v):
    return v.astype(BF16)


def _colsum(v):
    return jnp.sum(v, axis=0, keepdims=True)


def _dot3(v, p):
    hi = _bf(v)
    r1 = v - hi.astype(F32)
    mid = _bf(r1)
    lo = _bf(r1 - mid.astype(F32))
    return _dot(hi, p) + _dot(mid, p) + _dot(lo, p)


_GELU_C = math.sqrt(2.0 / math.pi)


_GELU_C3 = _GELU_C * 0.044715


def _gelu_parts(v):
    t = jnp.tanh(v * (_GELU_C + _GELU_C3 * (v * v)))
    hv = 0.5 * v
    return hv + hv * t, t


def _gelu(v):
    return _gelu_parts(v)[0]


def _gelu_grad(v, t):
    return (0.5 + 0.5 * t) + (0.5 * v) * (1.0 - t * t) * (_GELU_C + (3.0 * _GELU_C3) * (v * v))


def _sigmoid(v):
    return 1.0 / (1.0 + jnp.exp(-v))


def _acc(ref, val, first):
    @pl.when(first)
    def _():
        ref[...] = val

    @pl.when(jnp.logical_not(first))
    def _():
        ref[...] += val


def _rows(shape):
    return lax.broadcasted_iota(jnp.int32, shape, 0)


def _ln_rows(r):
    mu = jnp.mean(r, -1, keepdims=True)
    rc = r - mu
    var = jnp.mean(rc * rc, -1, keepdims=True)
    rs = lax.rsqrt(var + LN_EPS)
    return rc * rs, rs


def ln_fwd(x, g, b, name):
    s = x.shape[0]
    tm = min(512, s)

    def body(x_ref, g_ref, b_ref, xh_ref, rs_ref, xb_ref):
        xh, rs = _ln_rows(x_ref[...])
        xh_ref[...] = xh
        rs_ref[...] = rs
        xb_ref[...] = _bf(xh * g_ref[...] + b_ref[...])

    row = pl.BlockSpec((tm, D_MODEL), lambda i: (i, 0))
    vec = pl.BlockSpec((1, D_MODEL), lambda i: (0, 0))
    return pl.pallas_call(
        body, name=name, grid=(s // tm,),
        in_specs=[row, vec, vec],
        out_specs=[row, pl.BlockSpec((tm, 1), lambda i: (i, 0)), row],
        out_shape=[jax.ShapeDtypeStruct((s, D_MODEL), F32), jax.ShapeDtypeStruct((s, 1), F32),
                   jax.ShapeDtypeStruct((s, D_MODEL), BF16)],
        compiler_params=_cparams(1),
    )(x, g, b)


def proj_ln(a, w, layer, bias, xh_prev, g_prev, b_prev, g, b, name):
    s, k = a.shape
    tm = min(512, s)

    def body(a_ref, w_ref, bias_ref, xp_ref, gp_ref, bp_ref, g_ref, b_ref, xh_ref, rs_ref, xb_ref):
        acc = _dot(a_ref[...], w_ref[...]) + bias_ref[...]
        r = ALPHA * (xp_ref[...] * gp_ref[...] + bp_ref[...]) + acc
        xh, rs = _ln_rows(r)
        xh_ref[...] = xh
        rs_ref[...] = rs
        xb_ref[...] = _bf(xh * g_ref[...] + b_ref[...])

    row = pl.BlockSpec((tm, D_MODEL), lambda i: (i, 0))
    vec = pl.BlockSpec((1, D_MODEL), lambda i: (0, 0))
    return pl.pallas_call(
        body, name=name, grid=(s // tm,),
        in_specs=[pl.BlockSpec((tm, k), lambda i: (i, 0)),
                  pl.BlockSpec((None, k, D_MODEL), lambda i: (layer, 0, 0)),
                  vec, row, vec, vec, vec, vec],
        out_specs=[row, pl.BlockSpec((tm, 1), lambda i: (i, 0)), row],
        out_shape=[jax.ShapeDtypeStruct((s, D_MODEL), F32), jax.ShapeDtypeStruct((s, 1), F32),
                   jax.ShapeDtypeStruct((s, D_MODEL), BF16)],
        compiler_params=_cparams(1),
    )(a, w, bias, xh_prev, g_prev, b_prev, g, b)


def loss_ln_bwd(target, xh, rs, g, b, name):
    s = xh.shape[0]
    tm = min(512, s)

    def body(t_ref, xh_ref, rs_ref, g_ref, b_ref, dr_ref, dg_ref, db_ref, sq_ref):
        first = pl.program_id(0) == 0
        xhv = xh_ref[...]
        err = xhv * g_ref[...] + b_ref[...] - t_ref[...]
        dyv = err * (1.0 / D_MODEL)
        dxh = dyv * g_ref[...]
        dr = rs_ref[...] * (dxh - jnp.mean(dxh, -1, keepdims=True) - xhv * jnp.mean(dxh * xhv, -1, keepdims=True))
        dr_ref[...] = dr
        _acc(dg_ref, _colsum(dyv * xhv), first)
        _acc(db_ref, _colsum(dyv), first)
        _acc(sq_ref, _colsum(err * err), first)

    row = pl.BlockSpec((tm, D_MODEL), lambda i: (i, 0))
    vec = pl.BlockSpec((1, D_MODEL), lambda i: (0, 0))
    vshape = jax.ShapeDtypeStruct((1, D_MODEL), F32)
    return pl.pallas_call(
        body, name=name, grid=(s // tm,),
        in_specs=[row, row, pl.BlockSpec((tm, 1), lambda i: (i, 0)), vec, vec],
        out_specs=[row, vec, vec, vec],
        out_shape=[jax.ShapeDtypeStruct((s, D_MODEL), F32), vshape, vshape, vshape],
        compiler_params=_cparams(1),
    )(target, xh, rs, g, b)


def mm_nn(a, w, layer, bias, out_dtype, tm, name):
    m, k = a.shape
    _, nj, _, n = w.shape
    tm = min(tm, m)

    def body(a_ref, w_ref, b_ref, o_ref):
        o_ref[...] = (_dot(_bf(a_ref[...]), w_ref[...]) + b_ref[...]).astype(out_dtype)

    return pl.pallas_call(
        body, name=name, grid=(nj, m // tm),
        in_specs=[pl.BlockSpec((tm, k), lambda j, i: (i, 0)),
                  pl.BlockSpec((None, None, k, n), lambda j, i: (layer, j, 0, 0)),
                  pl.BlockSpec((1, n), lambda j, i: (0, j))],
        out_specs=pl.BlockSpec((tm, n), lambda j, i: (i, j)),
        out_shape=jax.ShapeDtypeStruct((m, nj * n), out_dtype),
        compiler_params=_cparams(2),
    )(a, w, bias)


def mm_nt(a, a_block, a_map, pick, w, layer, add, out_dtype, tm, m, name, ln=None):
    _, nj, r, n = w.shape
    tm = min(tm, m)
    has_add = add is not None
    n_in = 2 + has_add + (3 if ln is not None else 0)

    def body(*refs):
        a_ref, w_ref = refs[0], refs[1]
        res = _dot_nt(_bf(pick(a_ref, 0)), w_ref[0])
        for j in range(1, nj):
            res = res + _dot_nt(_bf(pick(a_ref, j)), w_ref[j])
        if has_add:
            res = res + ALPHA * refs[2][...]
        if ln is None:
            refs[n_in][...] = res.astype(out_dtype)
            return
        xh_ref, rs_ref, g_ref = refs[n_in - 3:n_in]
        dr_ref, dg_ref, db_ref, cs_ref = refs[n_in:]
        first = pl.program_id(0) == 0
        xhv = xh_ref[...]
        dxh = res * g_ref[...]
        dr = rs_ref[...] * (dxh - jnp.mean(dxh, -1, keepdims=True) - xhv * jnp.mean(dxh * xhv, -1, keepdims=True))
        dr_ref[...] = dr
        _acc(dg_ref, _colsum(res * xhv), first)
        _acc(db_ref, _colsum(res), first)
        _acc(cs_ref, _colsum(dr), first)

    row = pl.BlockSpec((tm, r), lambda i: (i, 0))
    in_specs = [pl.BlockSpec(a_block, a_map),
                pl.BlockSpec((None, nj, r, n), lambda i: (layer, 0, 0, 0))]
    ops = [a, w]
    if has_add:
        in_specs.append(row)
        ops.append(add)
    if ln is None:
        out_specs, out_shape = row, jax.ShapeDtypeStruct((m, r), out_dtype)
    else:
        vec = pl.BlockSpec((1, r), lambda i: (0, 0))
        vshape = jax.ShapeDtypeStruct((1, r), F32)
        in_specs += [row, pl.BlockSpec((tm, 1), lambda i: (i, 0)), vec]
        ops += list(ln)
        out_specs, out_shape = [row, vec, vec, vec], [jax.ShapeDtypeStruct((m, r), F32), vshape, vshape, vshape]
    return pl.pallas_call(
        body, name=name, grid=(m // tm,),
        in_specs=in_specs, out_specs=out_specs, out_shape=out_shape,
        compiler_params=_cparams(1),
    )(*ops)


def mm_tn(a, b, b_block, b_map, nj, n, tk, ts, s, name, exchange=()):
    kx = a.shape[1]
    ts = min(ts, s)
    ne = len(exchange)
    grid = (kx // tk, nj, s // ts)

    def body(a_ref, b_ref, *rest):
        g_refs, o_ref, got_refs, sems = rest[:ne], rest[ne], rest[ne + 1:2 * ne + 1], rest[2 * ne + 1:]
        pids = [pl.program_id(ax) for ax in range(3)]
        if ne:
            @pl.when((pids[0] == 0) & (pids[1] == 0) & (pids[2] == 0))
            def _():
                _exchange_start(g_refs, got_refs, sems)

        part = _dot_tn(_bf(a_ref[...]), _bf(b_ref[...]))
        _acc(o_ref, part, pids[2] == 0)
        if ne:
            @pl.when((pids[0] == grid[0] - 1) & (pids[1] == grid[1] - 1) & (pids[2] == grid[2] - 1))
            def _():
                _exchange_finish(g_refs, got_refs, sems)

    res = pl.pallas_call(
        body, name=name, grid=grid,
        in_specs=[pl.BlockSpec((ts, tk), lambda kt, j, st: (st, kt)), pl.BlockSpec(b_block, b_map)] + [ANY] * ne,
        out_specs=[pl.BlockSpec((None, tk, n), lambda kt, j, st: (j, kt, 0))] + [ANY] * ne,
        out_shape=[jax.ShapeDtypeStruct((nj, kx, n), F32)] + _exchange_shapes(exchange),
        scratch_shapes=_exchange_sems(ne) if ne else [],
        compiler_params=_cparams(3),
    )(a, b, *exchange)
    return res if ne else res[0]


S5_TAB_ROWS = 8 * SUBLANES


def _s5_scan_table(tab_ref, ar, ai, reverse):
    n = N_STATE
    row = _rows((SUBLANES, n))
    edge = SUBLANES - 1 if reverse else 0
    tab_ref[0:8, :] = jnp.where(row == edge, ar, 0.0)
    tab_ref[8:16, :] = jnp.where(row == edge, ai, 0.0)
    pr, pi = ar, ai
    for step, k in enumerate((1, 2, 4)):
        mask = row < SUBLANES - k if reverse else row >= k
        tab_ref[16 + 16 * step:24 + 16 * step, :] = jnp.where(mask, pr, 0.0)
        tab_ref[24 + 16 * step:32 + 16 * step, :] = jnp.where(mask, pi, 0.0)
        pr, pi = pr * pr - pi * pi, 2.0 * pr * pi


def _s5_scan(src_ref, dst_ref, tab_ref, edge_ref, tb, reverse, per_tile=None):
    n = N_STATE
    ng = tb // SUBLANES
    nq = n // LANES
    link = SUBLANES - 1 if reverse else 1

    def tile(ii, carry):
        g = ng - 1 - ii if reverse else ii
        rows = pl.ds(pl.multiple_of(g * SUBLANES, SUBLANES), SUBLANES)
        out = []
        for q in range(nq):
            cre = slice(q * LANES, (q + 1) * LANES)
            cim = slice(n + q * LANES, n + (q + 1) * LANES)
            lr, li = src_ref[rows, cre], src_ref[rows, cim]
            tr, ti = pltpu.roll(carry[2 * q], link, 0), pltpu.roll(carry[2 * q + 1], link, 0)
            kr, ki = tab_ref[0:8, cre], tab_ref[8:16, cre]
            lr, li = lr + kr * tr - ki * ti, li + kr * ti + ki * tr
            for step, k in enumerate((1, 2, 4)):
                amt = SUBLANES - k if reverse else k
                kr, ki = tab_ref[16 + 16 * step:24 + 16 * step, cre], tab_ref[24 + 16 * step:32 + 16 * step, cre]
                sr, si = pltpu.roll(lr, amt, 0), pltpu.roll(li, amt, 0)
                lr, li = lr + kr * sr - ki * si, li + kr * si + ki * sr
            dst_ref[rows, cre] = lr
            dst_ref[rows, cim] = li
            if per_tile is not None:
                per_tile(g, q, (cre, cim), lr, li)
            out += [lr, li]
        return tuple(out)

    init = []
    for q in range(nq):
        init += [edge_ref[:, q * LANES:(q + 1) * LANES], edge_ref[:, n + q * LANES:n + (q + 1) * LANES]]
    fin = lax.fori_loop(0, ng, tile, tuple(init))
    for q in range(nq):
        edge_ref[:, q * LANES:(q + 1) * LANES] = fin[2 * q]
        edge_ref[:, n + q * LANES:n + (q + 1) * LANES] = fin[2 * q + 1]


def s5_fwd(h_in, a2, bexp, cexp, dskip, wglu, layer, bglu, tb, name, gather=()):
    s = h_in.shape[0]
    n = N_STATE
    ng = len(gather)

    def body(u_ref, a_ref, b_ref, c_ref, d_ref, w_ref, bg_ref, *rest):
        y_ref, h_ref, y0_ref = rest[ng:ng + 3]
        slabs = rest[ng + 3:2 * ng + 3]
        edge, tab, bu_ref = rest[2 * ng + 3:2 * ng + 6]
        sems = rest[2 * ng + 6:]

        @pl.when(pl.program_id(0) == 0)
        def _():
            if ng:
                _gather_start(slabs, sems[0:2])
            edge[...] = jnp.zeros_like(edge)
            _s5_scan_table(tab, a_ref[0:1, 0:n], a_ref[0:1, n:2 * n], False)

        u = u_ref[...]
        bu_ref[...] = _dot(_bf(u), b_ref[...])
        _s5_scan(bu_ref, h_ref, tab, edge, tb, False)
        y0 = _dot(_bf(h_ref[:, 0:n]), c_ref[0:n, :]) + _dot(_bf(h_ref[:, n:2 * n]), c_ref[n:2 * n, :]) + d_ref[...] * u
        y0_ref[...] = y0
        yg = _gelu(y0)
        z = _dot(_bf(yg), w_ref[...]) + bg_ref[...]
        y_ref[...] = _bf(yg * _sigmoid(z))
        if ng:
            @pl.when(pl.program_id(0) == s // tb - 1)
            def _():
                _gather_finish(slabs, sems[0:2], sems[2:4])

    vec = pl.BlockSpec((1, D_GROUP), lambda i: (0, 0))
    return pl.pallas_call(
        body, name=name, grid=(s // tb,),
        in_specs=[pl.BlockSpec((tb, D_GROUP), lambda i: (i, 0)),
                  pl.BlockSpec((1, 2 * n), lambda i: (0, 0)),
                  pl.BlockSpec((D_GROUP, 2 * n), lambda i: (0, 0)),
                  pl.BlockSpec((2 * n, D_GROUP), lambda i: (0, 0)),
                  vec,
                  pl.BlockSpec((None, D_GROUP, D_GROUP), lambda i: (layer, 0, 0)),
                  vec] + [ANY] * ng,
        out_specs=[pl.BlockSpec((tb, D_GROUP), lambda i: (i, 0)),
                   pl.BlockSpec((tb, 2 * n), lambda i: (i, 0)),
                   pl.BlockSpec((tb, D_GROUP), lambda i: (i, 0))] + [ANY] * ng,
        out_shape=[jax.ShapeDtypeStruct((s, D_GROUP), BF16), jax.ShapeDtypeStruct((s, 2 * n), F32),
                   jax.ShapeDtypeStruct((s, D_GROUP), F32)] + [jax.ShapeDtypeStruct(a.shape, a.dtype) for a in gather],
        input_output_aliases={7 + k: 3 + k for k in range(ng)},
        scratch_shapes=[pltpu.VMEM((SUBLANES, 2 * n), F32), pltpu.VMEM((S5_TAB_ROWS, n), F32),
                        pltpu.VMEM((tb, 2 * n), F32)] + (_gather_sems(ng) if ng else []),
        compiler_params=_cparams(1),
    )(h_in, a2, bexp, cexp, dskip, wglu, bglu, *gather)


def s5_bwd(dmix, h_in, y0, hst, a2, bexp, cexp, dskip, wglu, layer, bglu, tb, name):
    s = h_in.shape[0]
    n = N_STATE
    nb = s // tb
    halo = tb // 8

    def body(dy_ref, u_ref, y0_ref, h_ref, hp_ref, a_ref, b_ref, c_ref, d_ref, w_ref, bg_ref,
             du_ref, cs_ref, db_ref, dc_ref, dd_ref, dw_ref, dbg_ref, da_ref, edge, tab, g_ref, da_acc):
        i = pl.program_id(0)
        first = i == 0

        @pl.when(first)
        def _():
            edge[...] = jnp.zeros_like(edge)
            da_acc[...] = jnp.zeros_like(da_acc)
            _s5_scan_table(tab, a_ref[0:1, 0:n], -a_ref[0:1, n:2 * n], True)

        dy = dy_ref[...]
        u = u_ref[...]
        y0v = y0_ref[...]
        yg, t = _gelu_parts(y0v)
        z = _dot(_bf(yg), w_ref[...]) + bg_ref[...]
        sg = _sigmoid(z)
        dz = dy * yg * sg * (1.0 - sg)
        dyg = dy * sg + _dot_nt(_bf(dz), w_ref[...])
        _acc(dw_ref, _dot_tn(_bf(yg), _bf(dz)), first)
        _acc(dbg_ref, _colsum(dz), first)
        dy0 = dyg * _gelu_grad(y0v, t)
        _acc(dd_ref, _colsum(dy0 * u), first)
        dy0b = _bf(dy0)
        _acc(dc_ref.at[0:n, :], _dot_tn(_bf(h_ref[:, 0:n]), dy0b), first)
        _acc(dc_ref.at[n:2 * n, :], _dot_tn(_bf(h_ref[:, n:2 * n]), dy0b), first)
        g_ref[...] = _dot_nt(dy0b, c_ref[...])
        keep = jnp.where(i == nb - 1, 0.0, 1.0)
        row0 = _rows((SUBLANES, LANES)) == 0

        def grad_a(g, q, cols, gr, gi):
            cre, cim = cols
            rows = pl.ds(pl.multiple_of(g * SUBLANES, SUBLANES), SUBLANES)
            before = pl.ds(pl.multiple_of(jnp.maximum(g - 1, 0) * SUBLANES, SUBLANES), SUBLANES)
            pre = jnp.where(g == 0, hp_ref[:, cre] * keep, h_ref[before, cre])
            pim = jnp.where(g == 0, hp_ref[:, cim] * keep, h_ref[before, cim])
            pr = jnp.where(row0, pltpu.roll(pre, 1, 0), pltpu.roll(h_ref[rows, cre], 1, 0))
            pi = jnp.where(row0, pltpu.roll(pim, 1, 0), pltpu.roll(h_ref[rows, cim], 1, 0))
            da_acc[:, cre] += gr * pr + gi * pi
            da_acc[:, cim] += gi * pr - gr * pi

        _s5_scan(g_ref, g_ref, tab, edge, tb, True, grad_a)
        da_ref[...] = _colsum(da_acc[...])
        gr, gi = g_ref[:, 0:n], g_ref[:, n:2 * n]
        grb, gib = _bf(gr), _bf(gi)
        du = d_ref[...] * dy0 + _dot_nt(grb, b_ref[:, 0:n]) + _dot_nt(gib, b_ref[:, n:2 * n])
        ub = _bf(u)
        _acc(db_ref.at[:, 0:n], _dot_tn(ub, grb), first)
        _acc(db_ref.at[:, n:2 * n], _dot_tn(ub, gib), first)
        du_ref[...] = _bf(du)
        _acc(cs_ref, _colsum(du), first)

    rev = lambda i: (nb - 1 - i, 0)
    vec = pl.BlockSpec((1, D_GROUP), lambda i: (0, 0))
    vshape = jax.ShapeDtypeStruct((1, D_GROUP), F32)
    return pl.pallas_call(
        body, name=name, grid=(nb,),
        in_specs=[pl.BlockSpec((tb, D_GROUP), rev),
                  pl.BlockSpec((tb, D_GROUP), rev),
                  pl.BlockSpec((tb, D_GROUP), rev),
                  pl.BlockSpec((tb, 2 * n), rev),
                  pl.BlockSpec((8, 2 * n), lambda i: (jnp.maximum((nb - 1 - i) * halo - 1, 0), 0)),
                  pl.BlockSpec((1, 2 * n), lambda i: (0, 0)),
                  pl.BlockSpec((D_GROUP, 2 * n), lambda i: (0, 0)),
                  pl.BlockSpec((2 * n, D_GROUP), lambda i: (0, 0)),
                  vec,
                  pl.BlockSpec((None, D_GROUP, D_GROUP), lambda i: (layer, 0, 0)),
                  vec],
        out_specs=[pl.BlockSpec((tb, D_GROUP), rev), vec,
                   pl.BlockSpec((D_GROUP, 2 * n), lambda i: (0, 0)),
                   pl.BlockSpec((2 * n, D_GROUP), lambda i: (0, 0)),
                   vec,
                   pl.BlockSpec((D_GROUP, D_GROUP), lambda i: (0, 0)),
                   vec,
                   pl.BlockSpec((1, 2 * n), lambda i: (0, 0))],
        out_shape=[jax.ShapeDtypeStruct((s, D_GROUP), BF16), vshape,
                   jax.ShapeDtypeStruct((D_GROUP, 2 * n), F32), jax.ShapeDtypeStruct((2 * n, D_GROUP), F32),
                   vshape, jax.ShapeDtypeStruct((D_GROUP, D_GROUP), F32), vshape,
                   jax.ShapeDtypeStruct((1, 2 * n), F32)],
        scratch_shapes=[pltpu.VMEM((SUBLANES, 2 * n), F32), pltpu.VMEM((S5_TAB_ROWS, n), F32),
                        pltpu.VMEM((tb, 2 * n), F32), pltpu.VMEM((SUBLANES, 2 * n), F32)],
        compiler_params=_cparams(1),
    )(dmix, h_in, y0, hst, hst, a2, bexp, cexp, dskip, wglu, bglu)


CV_HALO = 32


def _gn_stats(hc, pmat):
    mu = _dot3(hc, pmat)
    xc = hc - mu
    var = _dot3(xc * xc, pmat)
    rstd = lax.rsqrt(var + LN_EPS)
    return xc * rstd, rstd


def cv_fwd(h_in, cw, cb, gg, gb, pmat, wpw, layer, bpw, tb, name):
    s = h_in.shape[0]
    hl = CV_HALO

    def body(v_ref, g_ref, cw_ref, cb_ref, gg_ref, gb_ref, p_ref, w_ref, bw_ref, y_ref, hc_ref, ext):
        @pl.when(pl.program_id(0) == 0)
        def _():
            ext[0:hl, :] = jnp.zeros((hl, D_GROUP), F32)

        ext[hl:hl + tb, :] = v_ref[...] * _sigmoid(g_ref[...])
        acc = jnp.zeros((tb, D_GROUP), F32) + cb_ref[...]
        for k in range(CONV_WIDTH):
            off = hl - (CONV_WIDTH - 1) + k
            acc = acc + cw_ref[k:k + 1, :] * ext[off:off + tb, :]
        hc_ref[...] = acc
        ext[0:hl, :] = ext[tb:tb + hl, :]
        xn, _ = _gn_stats(acc, p_ref[...])
        hn = xn * gg_ref[...] + gb_ref[...]
        hs = hn * _sigmoid(hn)
        y_ref[...] = _bf(_dot(_bf(hs), w_ref[...]) + bw_ref[...])

    vec = pl.BlockSpec((1, D_GROUP), lambda i: (0, 0))
    sq = pl.BlockSpec((D_GROUP, D_GROUP), lambda i: (0, 0))
    return pl.pallas_call(
        body, name=name, grid=(s // tb,),
        in_specs=[pl.BlockSpec((tb, D_GROUP), lambda i: (i, 1)),
                  pl.BlockSpec((tb, D_GROUP), lambda i: (i, 2)),
                  pl.BlockSpec((hl, D_GROUP), lambda i: (0, 0)),
                  vec, vec, vec, sq,
                  pl.BlockSpec((None, D_GROUP, D_GROUP), lambda i: (layer, 0, 0)),
                  vec],
        out_specs=[pl.BlockSpec((tb, D_GROUP), lambda i: (i, 0)), pl.BlockSpec((tb, D_GROUP), lambda i: (i, 0))],
        out_shape=[jax.ShapeDtypeStruct((s, D_GROUP), BF16), jax.ShapeDtypeStruct((s, D_GROUP), F32)],
        scratch_shapes=[pltpu.VMEM((hl + tb, D_GROUP), F32)],
        compiler_params=_cparams(1),
    )(h_in, h_in, cw, cb, gg, gb, pmat, wpw, bpw)


def cv_bwd(dmix, h_in, hc, cw, gg, gb, pmat, wpw, layer, tb, name, scatter=()):
    s = h_in.shape[0]
    hl = CV_HALO
    nb = s // tb
    per = tb // hl
    ns = len(scatter)

    def body(dy_ref, v_ref, g_ref, vh_ref, gh_ref, hc_ref, cw_ref, gg_ref, gb_ref, p_ref, w_ref, *rest):
        s_refs = rest[:ns]
        dvg_ref, cs_ref, dcw_ref, dcb_ref, dgg_ref, dgb_ref, dw_ref, dbw_ref = rest[ns:ns + 8]
        got_refs = rest[ns + 8:2 * ns + 8]
        ext, dext, head = rest[2 * ns + 8:2 * ns + 11]
        sems = rest[2 * ns + 11:]
        i = pl.program_id(0)
        first = i == 0

        @pl.when(first)
        def _():
            if ns:
                _scatter_start(s_refs, got_refs, sems)
            head[...] = jnp.zeros_like(head)

        dy = dy_ref[...]
        pm = p_ref[...]
        xn, rstd = _gn_stats(hc_ref[...], pm)
        hn = xn * gg_ref[...] + gb_ref[...]
        sg = _sigmoid(hn)
        hs = hn * sg
        dyb = _bf(dy)
        _acc(dbw_ref, _colsum(dy), first)
        _acc(dw_ref, _dot_tn(_bf(hs), dyb), first)
        dhs = _dot_nt(dyb, w_ref[...])
        dhn = dhs * sg * (1.0 + hn * (1.0 - sg))
        _acc(dgg_ref, _colsum(dhn * xn), first)
        _acc(dgb_ref, _colsum(dhn), first)
        dxn = dhn * gg_ref[...]
        dhc = rstd * (dxn - _dot3(dxn, pm) - xn * _dot3(dxn * xn, pm))
        _acc(dcb_ref, _colsum(dhc), first)
        v = v_ref[...]
        sgg = _sigmoid(g_ref[...])
        keep = jnp.where(i == nb - 1, 0.0, 1.0)
        ext[0:hl, :] = vh_ref[...] * _sigmoid(gh_ref[...]) * keep
        ext[hl:hl + tb, :] = v * sgg
        dext[0:tb, :] = dhc
        dext[tb:tb + hl, :] = head[...]
        head[...] = dhc[0:hl]
        dhg = jnp.zeros((tb, D_GROUP), F32)
        for k in range(CONV_WIDTH):
            off = hl - (CONV_WIDTH - 1) + k
            wk = _colsum(dhc * ext[off:off + tb, :])
            _acc(dcw_ref.at[k:k + 1, :], wk, first)
            back = CONV_WIDTH - 1 - k
            dhg = dhg + cw_ref[k:k + 1, :] * dext[back:back + tb, :]

        @pl.when(first)
        def _():
            dcw_ref[CONV_WIDTH:hl, :] = jnp.zeros((hl - CONV_WIDTH, D_GROUP), F32)

        dv = dhg * sgg
        dg = dhg * v * sgg * (1.0 - sgg)
        dvg_ref[:, 0:D_GROUP] = _bf(dv)
        dvg_ref[:, D_GROUP:2 * D_GROUP] = _bf(dg)
        _acc(cs_ref.at[:, 0:D_GROUP], _colsum(dv), first)
        _acc(cs_ref.at[:, D_GROUP:2 * D_GROUP], _colsum(dg), first)
        if ns:
            @pl.when(i == nb - 1)
            def _():
                _scatter_finish(s_refs, got_refs, sems)

    vec = pl.BlockSpec((1, D_GROUP), lambda i: (0, 0))
    sq = pl.BlockSpec((D_GROUP, D_GROUP), lambda i: (0, 0))
    tap = pl.BlockSpec((hl, D_GROUP), lambda i: (0, 0))
    vshape = jax.ShapeDtypeStruct((1, D_GROUP), F32)

    def blk(col):
        return pl.BlockSpec((tb, D_GROUP), lambda i: (nb - 1 - i, col))

    def halo_blk(col):
        return pl.BlockSpec((hl, D_GROUP), lambda i: (jnp.maximum((nb - 1 - i) * per - 1, 0), col))

    return pl.pallas_call(
        body, name=name, grid=(nb,),
        in_specs=[blk(1), blk(1), blk(2), halo_blk(1), halo_blk(2),
                  pl.BlockSpec((tb, D_GROUP), lambda i: (nb - 1 - i, 0)),
                  tap, vec, vec, sq,
                  pl.BlockSpec((None, D_GROUP, D_GROUP), lambda i: (layer, 0, 0))] + [ANY] * ns,
        out_specs=[pl.BlockSpec((tb, 2 * D_GROUP), lambda i: (nb - 1 - i, 0)),
                   pl.BlockSpec((1, 2 * D_GROUP), lambda i: (0, 0)),
                   tap, vec, vec, vec, sq, vec] + [ANY] * ns,
        out_shape=[jax.ShapeDtypeStruct((s, 2 * D_GROUP), BF16), jax.ShapeDtypeStruct((1, 2 * D_GROUP), F32),
                   jax.ShapeDtypeStruct((hl, D_GROUP), F32), vshape, vshape, vshape,
                   jax.ShapeDtypeStruct((D_GROUP, D_GROUP), F32), vshape] + _scatter_shapes(scatter),
        scratch_shapes=[pltpu.VMEM((hl + tb, D_GROUP), F32), pltpu.VMEM((tb + hl, D_GROUP), F32),
                        pltpu.VMEM((hl, D_GROUP), F32)] + (_scatter_sems(ns) if ns else []),
        compiler_params=_cparams(1),
    )(dmix, h_in, h_in, h_in, h_in, hc, cw, gg, gb, pmat, wpw, *scatter)


LRU_HALO = 8


def _lru_gates(xc, wr_ref, br_ref, wi_ref, bi_ref, sp_ref):
    xcb = _bf(xc)
    r = _sigmoid(_dot(xcb, wr_ref[...]) + br_ref[...])
    gi = _sigmoid(_dot(xcb, wi_ref[...]) + bi_ref[...])
    la = -LRU_C * r * sp_ref[...]
    a = jnp.exp(la)
    e2 = a * a
    sq = jnp.sqrt(-jnp.tanh(la) * (e2 + 1.0))
    return r, gi, a, e2, sq


def _rscan(a, b, tb, reverse):
    row = _rows(a.shape)
    sh = 1
    while sh < tb:
        if reverse:
            amt, mask = tb - sh, row < tb - sh
        else:
            amt, mask = sh, row >= sh
        a_s = jnp.where(mask, pltpu.roll(a, amt, 0), 1.0)
        b_s = jnp.where(mask, pltpu.roll(b, amt, 0), 0.0)
        b = b + a * b_s
        a = a * a_s
        sh *= 2
    return a, b


def lru_fwd(h_in, cw, cb, wr, br, wi, bi, sp, tb, name):
    s = h_in.shape[0]
    hl = LRU_HALO

    def body(xg_ref, xr_ref, cw_ref, cb_ref, wr_ref, br_ref, wi_ref, bi_ref, sp_ref, y_ref, xc_ref, h_ref, ext, carry):
        @pl.when(pl.program_id(0) == 0)
        def _():
            ext[0:hl, :] = jnp.zeros((hl, D_GROUP), F32)
            carry[...] = jnp.zeros_like(carry)

        ext[hl:hl + tb, :] = xr_ref[...]
        xc = jnp.zeros((tb, D_GROUP), F32) + cb_ref[...]
        for k in range(LRU_CONV_WIDTH):
            off = hl - (LRU_CONV_WIDTH - 1) + k
            xc = xc + cw_ref[k:k + 1, :] * ext[off:off + tb, :]
        xc_ref[...] = xc
        ext[0:hl, :] = ext[tb:tb + hl, :]
        r, gi, a, e2, sq = _lru_gates(xc, wr_ref, br_ref, wi_ref, bi_ref, sp_ref)
        pa, hloc = _rscan(a, sq * (gi * xc), tb, False)
        h = hloc + pa * carry[7:8, :]
        h_ref[...] = h
        carry[...] = h[tb - 8:tb]
        y_ref[...] = _bf(h * _gelu(xg_ref[...]))

    vec = pl.BlockSpec((1, D_GROUP), lambda i: (0, 0))
    sq_spec = pl.BlockSpec((D_GROUP, D_GROUP), lambda i: (0, 0))
    blk = pl.BlockSpec((tb, D_GROUP), lambda i: (i, 0))
    return pl.pallas_call(
        body, name=name, grid=(s // tb,),
        in_specs=[pl.BlockSpec((tb, D_GROUP), lambda i: (i, 3)),
                  pl.BlockSpec((tb, D_GROUP), lambda i: (i, 4)),
                  pl.BlockSpec((hl, D_GROUP), lambda i: (0, 0)),
                  vec, sq_spec, vec, sq_spec, vec, vec],
        out_specs=[blk, blk, blk],
        out_shape=[jax.ShapeDtypeStruct((s, D_GROUP), BF16), jax.ShapeDtypeStruct((s, D_GROUP), F32),
                   jax.ShapeDtypeStruct((s, D_GROUP), F32)],
        scratch_shapes=[pltpu.VMEM((hl + tb, D_GROUP), F32), pltpu.VMEM((8, D_GROUP), F32)],
        compiler_params=_cparams(1),
    )(h_in, h_in, cw, cb, wr, br, wi, bi, sp)


def lru_bwd(dmix, h_in, xcs, hs, cw, wr, br, wi, bi, sp, tb, name):
    s = h_in.shape[0]
    hl = LRU_HALO
    nb = s // tb
    per = tb // hl

    def body(dy_ref, xg_ref, xr_ref, xrh_ref, xc_ref, h_ref, hp_ref, cw_ref, wr_ref, br_ref, wi_ref, bi_ref, sp_ref,
             dx_ref, cs_ref, dcw_ref, dcb_ref, dwr_ref, dbr_ref, dwi_ref, dbi_ref, dsp_ref,
             ext, dext, head, anext, gnext):
        i = pl.program_id(0)
        first = i == 0

        @pl.when(first)
        def _():
            head[...] = jnp.zeros_like(head)
            anext[...] = jnp.zeros_like(anext)
            gnext[...] = jnp.zeros_like(gnext)

        dy = dy_ref[...]
        xg = xg_ref[...]
        xc = xc_ref[...]
        h = h_ref[...]
        r, gi, a, e2, sq = _lru_gates(xc, wr_ref, br_ref, wi_ref, bi_ref, sp_ref)
        gate, t = _gelu_parts(xg)
        dh = dy * gate
        dxg = dy * h * _gelu_grad(xg, t)
        row = _rows((tb, D_GROUP))
        coef = jnp.where(row == tb - 1, anext[0:1, :], pltpu.roll(a, tb - 1, 0))
        pc, gloc = _rscan(coef, dh, tb, True)
        gfull = gloc + pc * gnext[0:1, :]
        anext[...] = a[0:8]
        gnext[...] = gfull[0:8]
        keep = jnp.where(i == nb - 1, 0.0, 1.0)
        hprev = jnp.where(row == 0, hp_ref[7:8, :] * keep, pltpu.roll(h, 1, 0))
        da = gfull * hprev
        uu = gi * xc
        dsq = gfull * uu
        duu = gfull * sq
        dla = da * a - dsq * e2 / sq
        sp = sp_ref[...]
        dr = dla * (-LRU_C) * sp
        _acc(dsp_ref, _colsum(dla * (-LRU_C) * r), first)
        dzr = dr * r * (1.0 - r)
        dzi = duu * xc * gi * (1.0 - gi)
        dzrb, dzib = _bf(dzr), _bf(dzi)
        dxc = duu * gi + _dot_nt(dzrb, wr_ref[...]) + _dot_nt(dzib, wi_ref[...])
        xcb = _bf(xc)
        _acc(dwr_ref, _dot_tn(xcb, dzrb), first)
        _acc(dwi_ref, _dot_tn(xcb, dzib), first)
        _acc(dbr_ref, _colsum(dzr), first)
        _acc(dbi_ref, _colsum(dzi), first)
        _acc(dcb_ref, _colsum(dxc), first)
        ext[0:hl, :] = xrh_ref[...] * keep
        ext[hl:hl + tb, :] = xr_ref[...]
        dext[0:tb, :] = dxc
        dext[tb:tb + hl, :] = head[...]
        head[...] = dxc[0:hl]
        dxr = jnp.zeros((tb, D_GROUP), F32)
        for k in range(LRU_CONV_WIDTH):
            off = hl - (LRU_CONV_WIDTH - 1) + k
            _acc(dcw_ref.at[k:k + 1, :], _colsum(dxc * ext[off:off + tb, :]), first)
            back = LRU_CONV_WIDTH - 1 - k
            dxr = dxr + cw_ref[k:k + 1, :] * dext[back:back + tb, :]

        @pl.when(first)
        def _():
            dcw_ref[LRU_CONV_WIDTH:hl, :] = jnp.zeros((hl - LRU_CONV_WIDTH, D_GROUP), F32)

        dx_ref[:, 0:D_GROUP] = _bf(dxg)
        dx_ref[:, D_GROUP:2 * D_GROUP] = _bf(dxr)
        _acc(cs_ref.at[:, 0:D_GROUP], _colsum(dxg), first)
        _acc(cs_ref.at[:, D_GROUP:2 * D_GROUP], _colsum(dxr), first)

    vec = pl.BlockSpec((1, D_GROUP), lambda i: (0, 0))
    sq_spec = pl.BlockSpec((D_GROUP, D_GROUP), lambda i: (0, 0))
    tap = pl.BlockSpec((hl, D_GROUP), lambda i: (0, 0))
    vshape = jax.ShapeDtypeStruct((1, D_GROUP), F32)
    sshape = jax.ShapeDtypeStruct((D_GROUP, D_GROUP), F32)

    def blk(col):
        return pl.BlockSpec((tb, D_GROUP), lambda i: (nb - 1 - i, col))

    def halo_blk(col):
        return pl.BlockSpec((hl, D_GROUP), lambda i: (jnp.maximum((nb - 1 - i) * per - 1, 0), col))

    return pl.pallas_call(
        body, name=name, grid=(nb,),
        in_specs=[blk(2), blk(3), blk(4), halo_blk(4), blk(0), blk(0), halo_blk(0),
                  tap, sq_spec, vec, sq_spec, vec, vec],
        out_specs=[pl.BlockSpec((tb, 2 * D_GROUP), lambda i: (nb - 1 - i, 0)),
                   pl.BlockSpec((1, 2 * D_GROUP), lambda i: (0, 0)),
                   tap, vec, sq_spec, vec, sq_spec, vec, vec],
        out_shape=[jax.ShapeDtypeStruct((s, 2 * D_GROUP), BF16), jax.ShapeDtypeStruct((1, 2 * D_GROUP), F32),
                   jax.ShapeDtypeStruct((hl, D_GROUP), F32), vshape, sshape, vshape, sshape, vshape, vshape],
        scratch_shapes=[pltpu.VMEM((hl + tb, D_GROUP), F32), pltpu.VMEM((tb + hl, D_GROUP), F32),
                        pltpu.VMEM((hl, D_GROUP), F32), pltpu.VMEM((8, D_GROUP), F32), pltpu.VMEM((8, D_GROUP), F32)],
        compiler_params=_cparams(1),
    )(dmix, h_in, h_in, h_in, xcs, hs, hs, cw, wr, br, wi, bi, sp)


ATTN_HEADS = 4
ATTN_HEAD_DIM = 64
ATTN_SCALE = ATTN_HEAD_DIM ** -0.5


def _head_mask(h):
    lane = lax.broadcasted_iota(jnp.int32, (1, D_GROUP), 1)
    return jnp.where((lane >= h * ATTN_HEAD_DIM) & (lane < (h + 1) * ATTN_HEAD_DIM), 1.0, 0.0)


def _softmax_rows(sc):
    e = jnp.exp(sc - jnp.max(sc, -1, keepdims=True))
    return e / jnp.sum(e, -1, keepdims=True)


def attn_fwd(h_in, kv, tb, name):
    s = h_in.shape[0]

    def body(q_ref, kv_ref, y_ref):
        q = q_ref[...]
        kb = _bf(kv_ref[:, 0:D_GROUP])
        vb = _bf(kv_ref[:, D_GROUP:2 * D_GROUP])
        out = jnp.zeros((tb, D_GROUP), F32)
        for h in range(ATTN_HEADS):
            mask = _head_mask(h)
            p = _softmax_rows(_dot_nt(_bf(q * mask), kb) * ATTN_SCALE)
            out = out + _dot(_bf(p), vb) * mask
        y_ref[...] = _bf(out)

    return pl.pallas_call(
        body, name=name, grid=(s // tb,),
        in_specs=[pl.BlockSpec((tb, D_GROUP), lambda i: (i, 5)),
                  pl.BlockSpec((D_GROUP, 2 * D_GROUP), lambda i: (0, 0))],
        out_specs=pl.BlockSpec((tb, D_GROUP), lambda i: (i, 0)),
        out_shape=jax.ShapeDtypeStruct((s, D_GROUP), BF16),
        compiler_params=_cparams(1),
    )(h_in, kv)


def attn_bwd(dmix, h_in, kv, tb, name, exchange=()):
    s = h_in.shape[0]
    ne = len(exchange)

    def body(do_ref, q_ref, kv_ref, *rest):
        g_refs = rest[:ne]
        dq_ref, cs_ref, dkv_ref = rest[ne:ne + 3]
        got_refs, sems = rest[ne + 3:2 * ne + 3], rest[2 * ne + 3:]
        first = pl.program_id(0) == 0
        if ne:
            @pl.when(first)
            def _():
                _exchange_start(g_refs, got_refs, sems)

        q = q_ref[...]
        do = do_ref[...]
        kb = _bf(kv_ref[:, 0:D_GROUP])
        vb = _bf(kv_ref[:, D_GROUP:2 * D_GROUP])
        dq = jnp.zeros((tb, D_GROUP), F32)
        dk = jnp.zeros((D_GROUP, D_GROUP), F32)
        dv = jnp.zeros((D_GROUP, D_GROUP), F32)
        for h in range(ATTN_HEADS):
            mask = _head_mask(h)
            qm = _bf(q * mask)
            p = _softmax_rows(_dot_nt(qm, kb) * ATTN_SCALE)
            dom = _bf(do * mask)
            dp = _dot_nt(dom, vb)
            dv = dv + _dot_tn(_bf(p), dom)
            ds = _bf(p * (dp - jnp.sum(dp * p, -1, keepdims=True)) * ATTN_SCALE)
            dq = dq + _dot(ds, kb) * mask
            dk = dk + _dot_tn(ds, qm)
        dq_ref[...] = _bf(dq)
        _acc(cs_ref, _colsum(dq), first)
        _acc(dkv_ref.at[:, 0:D_GROUP], dk, first)
        _acc(dkv_ref.at[:, D_GROUP:2 * D_GROUP], dv, first)
        if ne:
            @pl.when(pl.program_id(0) == s // tb - 1)
            def _():
                _exchange_finish(g_refs, got_refs, sems)

    return pl.pallas_call(
        body, name=name, grid=(s // tb,),
        in_specs=[pl.BlockSpec((tb, D_GROUP), lambda i: (i, 3)),
                  pl.BlockSpec((tb, D_GROUP), lambda i: (i, 5)),
                  pl.BlockSpec((D_GROUP, 2 * D_GROUP), lambda i: (0, 0))] + [ANY] * ne,
        out_specs=[pl.BlockSpec((tb, D_GROUP), lambda i: (i, 0)),
                   pl.BlockSpec((1, D_GROUP), lambda i: (0, 0)),
                   pl.BlockSpec((D_GROUP, 2 * D_GROUP), lambda i: (0, 0))] + [ANY] * ne,
        out_shape=[jax.ShapeDtypeStruct((s, D_GROUP), BF16), jax.ShapeDtypeStruct((1, D_GROUP), F32),
                   jax.ShapeDtypeStruct((D_GROUP, 2 * D_GROUP), F32)] + _exchange_shapes(exchange),
        scratch_shapes=_exchange_sems(ne) if ne else [],
        compiler_params=_cparams(1),
    )(dmix, h_in, kv, *exchange)


FFN_RB = 16
FFN_UNROLL_FWD = 4
FFN_UNROLL_BWD = 2
FFN_TAP_ROWS = 8
FFN_TN = D_FF // 2


def _shift_down(cur, tail, k):
    return pltpu.roll(jnp.concatenate([tail, cur], axis=0), k, 0)[SUBLANES:]


def _shift_up(cur, head, k):
    rb = cur.shape[0]
    return pltpu.roll(jnp.concatenate([cur, head], axis=0), rb + SUBLANES - k, 0)[:rb]


def _fold8(v):
    tot = v[0:SUBLANES]
    for t in range(1, v.shape[0] // SUBLANES):
        tot = tot + v[t * SUBLANES:(t + 1) * SUBLANES]
    return tot


def _strip(r):
    return pl.ds(pl.multiple_of(r * FFN_RB, FFN_RB), FFN_RB)


def ffn_act_fwd(u, cw, cb, tb, name, gather=()):
    s = u.shape[0]
    rb = FFN_RB
    nct = D_FF // FFN_TN
    nstrip = tb // rb
    ng = len(gather)

    def body(uv_ref, ug_ref, wv_ref, wg_ref, bv_ref, bg_ref, *rest):
        hf_ref, keep_ref = rest[ng], rest[ng + 1]
        slabs = rest[ng + 2:2 * ng + 2]
        tailv, tailg = rest[2 * ng + 2], rest[2 * ng + 3]
        sems = rest[2 * ng + 4:]
        if ng:
            @pl.when((pl.program_id(0) == 0) & (pl.program_id(1) == 0))
            def _():
                _gather_start(slabs, sems[0:2])

        @pl.when(pl.program_id(1) == 0)
        def _():
            tailv[...] = jnp.zeros_like(tailv)
            tailg[...] = jnp.zeros_like(tailg)

        for cc in range(FFN_TN // LANES):
            cols = slice(cc * LANES, (cc + 1) * LANES)
            wv = [wv_ref[k:k + 1, cols] for k in range(FFN_CONV_WIDTH)]
            wg = [wg_ref[k:k + 1, cols] for k in range(FFN_CONV_WIDTH)]
            bv, bg = bv_ref[:, cols], bg_ref[:, cols]

            def strip(r, carry):
                tail_v, tail_g = carry
                cur_v, cur_g = uv_ref[_strip(r), cols], ug_ref[_strip(r), cols]
                vc = wv[0] * _shift_down(cur_v, tail_v, 2) + wv[1] * _shift_down(cur_v, tail_v, 1) + wv[2] * cur_v + bv
                gc = wg[0] * _shift_down(cur_g, tail_g, 2) + wg[1] * _shift_down(cur_g, tail_g, 1) + wg[2] * cur_g + bg
                ge, t = _gelu_parts(gc)
                hf_ref[_strip(r), cols] = _bf(vc * ge)
                keep_ref[0, _strip(r), cols] = _bf(vc)
                keep_ref[1, _strip(r), cols] = _bf(ge)
                keep_ref[2, _strip(r), cols] = _bf(_gelu_grad(gc, t))
                return cur_v[rb - SUBLANES:], cur_g[rb - SUBLANES:]

            def strips(q, carry):
                for k in range(FFN_UNROLL_FWD):
                    carry = strip(q * FFN_UNROLL_FWD + k, carry)
                return carry

            last_v, last_g = lax.fori_loop(0, nstrip // FFN_UNROLL_FWD, strips, (tailv[:, cols], tailg[:, cols]))
            tailv[:, cols] = last_v
            tailg[:, cols] = last_g

        if ng:
            @pl.when((pl.program_id(0) == nct - 1) & (pl.program_id(1) == s // tb - 1))
            def _():
                _gather_finish(slabs, sems[0:2], sems[2:4])

    return pl.pallas_call(
        body, name=name, grid=(nct, s // tb),
        in_specs=[pl.BlockSpec((tb, FFN_TN), lambda c, i: (i, c)),
                  pl.BlockSpec((tb, FFN_TN), lambda c, i: (i, c + nct)),
                  pl.BlockSpec((FFN_TAP_ROWS, FFN_TN), lambda c, i: (0, c)),
                  pl.BlockSpec((FFN_TAP_ROWS, FFN_TN), lambda c, i: (0, c + nct)),
                  pl.BlockSpec((1, FFN_TN), lambda c, i: (0, c)),
                  pl.BlockSpec((1, FFN_TN), lambda c, i: (0, c + nct))] + [ANY] * ng,
        out_specs=[pl.BlockSpec((tb, FFN_TN), lambda c, i: (i, c)),
                   pl.BlockSpec((3, tb, FFN_TN), lambda c, i: (0, i, c))] + [ANY] * ng,
        out_shape=[jax.ShapeDtypeStruct((s, D_FF), BF16), jax.ShapeDtypeStruct((3, s, D_FF), BF16)]
        + [jax.ShapeDtypeStruct(a.shape, a.dtype) for a in gather],
        input_output_aliases={6 + k: 2 + k for k in range(ng)},
        scratch_shapes=[pltpu.VMEM((SUBLANES, FFN_TN), F32), pltpu.VMEM((SUBLANES, FFN_TN), F32)]
        + (_gather_sems(ng) if ng else []),
        compiler_params=_cparams(2),
    )(u, u, cw, cw, cb, cb, *gather)


def ffn_act_bwd(dhf, u, kept, cw, tb, name, scatter=()):
    s = u.shape[0]
    rb = FFN_RB
    nct = D_FF // FFN_TN
    nb = s // tb
    nstrip = tb // rb
    ntap = FFN_CONV_WIDTH
    ns = len(scatter)

    def body(dh_ref, uv_ref, ug_ref, kept_ref, wv_ref, wg_ref, *rest):
        s_refs = rest[:ns]
        du_ref, dwv_ref, dwg_ref = rest[ns:ns + 3]
        got_refs = rest[ns + 3:2 * ns + 3]
        headv, headg = rest[2 * ns + 3], rest[2 * ns + 4]
        sems = rest[2 * ns + 5:]
        i = pl.program_id(1)
        first = i == 0
        if ns:
            @pl.when((pl.program_id(0) == 0) & first)
            def _():
                _scatter_start(s_refs, got_refs, sems)

        @pl.when(first)
        def _():
            headv[...] = jnp.zeros_like(headv)
            headg[...] = jnp.zeros_like(headg)
            dwv_ref[...] = jnp.zeros_like(dwv_ref)
            dwg_ref[...] = jnp.zeros_like(dwg_ref)

        zero = jnp.zeros((SUBLANES, LANES), F32)
        for cc in range(FFN_TN // LANES):
            cols = slice(cc * LANES, (cc + 1) * LANES)
            wv = [wv_ref[k:k + 1, cols] for k in range(ntap)]
            wg = [wg_ref[k:k + 1, cols] for k in range(ntap)]

            def strip(ii, carry):
                head_dv, head_dg, acc_v, acc_g = carry
                r = nstrip - 1 - ii
                dh = dh_ref[_strip(r), cols]
                dvc = dh * kept_ref[1, _strip(r), cols].astype(F32)
                dgc = dh * kept_ref[0, _strip(r), cols].astype(F32) * kept_ref[2, _strip(r), cols].astype(F32)
                sdv = [_shift_up(dvc, head_dv, 2), _shift_up(dvc, head_dv, 1), dvc]
                sdg = [_shift_up(dgc, head_dg, 2), _shift_up(dgc, head_dg, 1), dgc]
                cur_v, cur_g = uv_ref[_strip(r), cols], ug_ref[_strip(r), cols]
                acc_v = tuple(acc_v[k] + _fold8(cur_v * sdv[k]) for k in range(ntap)) + (acc_v[ntap] + _fold8(dvc),)
                acc_g = tuple(acc_g[k] + _fold8(cur_g * sdg[k]) for k in range(ntap)) + (acc_g[ntap] + _fold8(dgc),)
                du_v = wv[0] * sdv[0] + wv[1] * sdv[1] + wv[2] * sdv[2]
                du_g = wg[0] * sdg[0] + wg[1] * sdg[1] + wg[2] * sdg[2]
                du_ref[0, _strip(r), cols] = _bf(du_v)
                du_ref[1, _strip(r), cols] = _bf(du_g)
                return dvc[0:SUBLANES], dgc[0:SUBLANES], acc_v, acc_g

            init = (headv[:, cols], headg[:, cols], (zero,) * (ntap + 1), (zero,) * (ntap + 1))
            def strips(q, carry):
                for k in range(FFN_UNROLL_BWD):
                    carry = strip(q * FFN_UNROLL_BWD + k, carry)
                return carry

            top_dv, top_dg, acc_v, acc_g = lax.fori_loop(0, nstrip // FFN_UNROLL_BWD, strips, init)
            headv[:, cols] = top_dv
            headg[:, cols] = top_dg
            for k in range(ntap + 1):
                dwv_ref[k:k + 1, cols] += _colsum(acc_v[k])
                dwg_ref[k:k + 1, cols] += _colsum(acc_g[k])

        if ns:
            @pl.when((pl.program_id(0) == nct - 1) & (i == nb - 1))
            def _():
                _scatter_finish(s_refs, got_refs, sems)

    def blk(shift):
        return pl.BlockSpec((tb, FFN_TN), lambda c, i: (nb - 1 - i, c + shift))

    tapv = pl.BlockSpec((FFN_TAP_ROWS, FFN_TN), lambda c, i: (0, c))
    tapg = pl.BlockSpec((FFN_TAP_ROWS, FFN_TN), lambda c, i: (0, c + nct))
    return pl.pallas_call(
        body, name=name, grid=(nct, nb),
        in_specs=[blk(0), blk(0), blk(nct), pl.BlockSpec((3, tb, FFN_TN), lambda c, i: (0, nb - 1 - i, c)), tapv, tapg]
        + [ANY] * ns,
        out_specs=[pl.BlockSpec((2, tb, FFN_TN), lambda c, i: (0, nb - 1 - i, c)), tapv, tapv] + [ANY] * ns,
        out_shape=[jax.ShapeDtypeStruct((2, s, D_FF), BF16), jax.ShapeDtypeStruct((FFN_TAP_ROWS, D_FF), F32),
                   jax.ShapeDtypeStruct((FFN_TAP_ROWS, D_FF), F32)] + _scatter_shapes(scatter),
        scratch_shapes=[pltpu.VMEM((SUBLANES, FFN_TN), F32), pltpu.VMEM((SUBLANES, FFN_TN), F32)]
        + (_scatter_sems(ns) if ns else []),
        compiler_params=_cparams(2),
    )(dhf, u, u, kept, cw, cw, *scatter)


def _place():
    x, y, c = lax.axis_index("x"), lax.axis_index("y"), lax.axis_index("c")
    return x, y, c, 2 * x + y


def _chip_peer(x, y, d):
    return jnp.bitwise_xor(x, d >> 1), jnp.bitwise_xor(y, d & 1)


def _my_half(ref_rows, c):
    half = ref_rows // 2
    return pl.ds(c * half, half)


def _gather_copy(ref, part, c, sems, k, d, to):
    rows = _my_half(ref.shape[1], c)
    return pltpu.make_async_remote_copy(src_ref=ref.at[part, rows], dst_ref=ref.at[part, rows], send_sem=sems[0].at[k, d - 1],
                                        recv_sem=sems[1].at[k, d - 1], device_id=to, device_id_type=MESH)


def _gather_start(slabs, ici_sems):
    x, y, c, j = _place()
    for k, ref in enumerate(slabs):
        for d in (1, 2, 3):
            px, py = _chip_peer(x, y, d)
            _gather_copy(ref, j, c, ici_sems, k, d, (px, py, c)).start()


def _gather_finish(slabs, ici_sems, d2d_sems):
    x, y, c, j = _place()
    sib = (x, y, 1 - c)
    passed = []
    for d in (1, 2, 3):
        jd = jnp.bitwise_xor(j, d)
        for k, ref in enumerate(slabs):
            _gather_copy(ref, jd, c, ici_sems, k, d, sib).wait_recv()
            cp = _gather_copy(ref, jd, c, d2d_sems, k, d, sib)
            cp.start()
            passed.append(cp)
    for cp in passed:
        cp.wait_recv()
        cp.wait_send()
    for k, ref in enumerate(slabs):
        for d in (1, 2, 3):
            _gather_copy(ref, j, c, ici_sems, k, d, sib).wait_send()


def _gather_sems(n):
    return [pltpu.SemaphoreType.DMA((n, 3)) for _ in range(4)]


def gather_weights(slabs):
    n = len(slabs)

    def body(*refs):
        outs = refs[n:2 * n]
        sems = refs[2 * n:]
        _gather_start(outs, sems[0:2])
        _gather_finish(outs, sems[0:2], sems[2:4])

    return pl.pallas_call(
        body, name="gather_weights", in_specs=[ANY] * n, out_specs=[ANY] * n,
        out_shape=[jax.ShapeDtypeStruct(a.shape, a.dtype) for a in slabs],
        input_output_aliases={w: w for w in range(n)}, scratch_shapes=_gather_sems(n),
    )(*slabs)


def _own_slab(part, jidx):
    slab = lax.empty((N_CHIPS,) + part.shape, part.dtype)
    return lax.dynamic_update_slice_in_dim(slab, part[None], jidx, axis=0)


def _exchange_copy(g_ref, got_ref, k, sems):
    x, y, c, j = _place()
    half = g_ref.shape[1] // 2
    return pltpu.make_async_remote_copy(
        src_ref=g_ref.at[:, pl.ds((1 - c) * half, half)], dst_ref=got_ref, send_sem=sems[0].at[k],
        recv_sem=sems[1].at[k], device_id=(x, y, 1 - c), device_id_type=MESH)


def _exchange_start(g_refs, got_refs, sems):
    for k in range(len(g_refs)):
        _exchange_copy(g_refs[k], got_refs[k], k, sems).start()


def _exchange_finish(g_refs, got_refs, sems):
    for k in range(len(g_refs)):
        _exchange_copy(g_refs[k], got_refs[k], k, sems).wait()


def _exchange_sems(n):
    return [pltpu.SemaphoreType.DMA((n,)), pltpu.SemaphoreType.DMA((n,))]


def _exchange_shapes(gs):
    return [jax.ShapeDtypeStruct((g.shape[0], g.shape[1] // 2, g.shape[2]), F32) for g in gs]


def exchange_cores(gbig, name, small_all=None):
    n = len(gbig)
    with_small = small_all is not None

    def body(*refs):
        g_refs = refs[:n]
        got_refs = refs[n + with_small:2 * n + with_small]
        dsem, esem, ssem, rsem, fsem, hsem = refs[2 * (n + with_small):]
        x, y, c, j = _place()
        sib = (x, y, 1 - c)
        big = []
        for k in range(n):
            half = g_refs[k].shape[1] // 2
            cp = pltpu.make_async_remote_copy(
                src_ref=g_refs[k].at[:, pl.ds((1 - c) * half, half)], dst_ref=got_refs[k], send_sem=dsem.at[k],
                recv_sem=esem.at[k], device_id=sib, device_id_type=MESH)
            cp.start()
            big.append(cp)
        if with_small:
            all_ref = refs[2 * n + 1]
            me = 4 * x + 2 * y + c

            def small_copy(k, block, to, sems):
                return pltpu.make_async_remote_copy(
                    src_ref=all_ref.at[block], dst_ref=all_ref.at[block],
                    send_sem=sems[0].at[k], recv_sem=sems[1].at[k], device_id=to, device_id_type=MESH)

            first = [small_copy(0, me, sib, (ssem, rsem))]
            for d in (1, 2, 3):
                px, py = _chip_peer(x, y, d)
                first.append(small_copy(d, me, (px, py, c), (ssem, rsem)))
            for cp in first:
                cp.start()
            passed = []
            for d in (1, 2, 3):
                px, py = _chip_peer(x, y, d)
                src_block = 4 * px + 2 * py + c
                small_copy(d, src_block, sib, (ssem, rsem)).wait_recv()
                cp = small_copy(d - 1, src_block, sib, (fsem, hsem))
                cp.start()
                passed.append(cp)
            small_copy(0, me, sib, (ssem, rsem)).wait_recv()
            for cp in passed:
                cp.wait_recv()
            for cp in first + passed:
                cp.wait_send()
        for cp in big:
            cp.wait()

    ops = list(gbig) + ([small_all] if with_small else [])
    out_shape = [jax.ShapeDtypeStruct((g.shape[0], g.shape[1] // 2, g.shape[2]), F32) for g in gbig]
    aliases = {}
    if with_small:
        out_shape.append(jax.ShapeDtypeStruct(small_all.shape, F32))
        aliases = {n: n}
    return pl.pallas_call(
        body, name=name, in_specs=[ANY] * len(ops), out_specs=[ANY] * len(out_shape), out_shape=out_shape,
        input_output_aliases=aliases,
        scratch_shapes=[pltpu.SemaphoreType.DMA((n,)), pltpu.SemaphoreType.DMA((n,)),
                        pltpu.SemaphoreType.DMA((4,)), pltpu.SemaphoreType.DMA((4,)),
                        pltpu.SemaphoreType.DMA((3,)), pltpu.SemaphoreType.DMA((3,))],
    )(*ops)


def _scatter_copy(s_ref, got_ref, k, d, sems):
    x, y, c, j = _place()
    px, py = _chip_peer(x, y, d)
    return pltpu.make_async_remote_copy(
        src_ref=s_ref.at[jnp.bitwise_xor(j, d)], dst_ref=got_ref.at[d - 1], send_sem=sems[0].at[k, d - 1],
        recv_sem=sems[1].at[k, d - 1], device_id=(px, py, c), device_id_type=MESH)


def _scatter_start(s_refs, got_refs, sems):
    for d in (1, 2, 3):
        for k in range(len(s_refs)):
            _scatter_copy(s_refs[k], got_refs[k], k, d, sems).start()


def _scatter_finish(s_refs, got_refs, sems):
    for d in (1, 2, 3):
        for k in range(len(s_refs)):
            _scatter_copy(s_refs[k], got_refs[k], k, d, sems).wait()


def _scatter_sems(n):
    return [pltpu.SemaphoreType.DMA((n, 3)), pltpu.SemaphoreType.DMA((n, 3))]


def _scatter_shapes(s1):
    return [jax.ShapeDtypeStruct((3,) + a.shape[1:], a.dtype) for a in s1]


def scatter_shards(s1, name):
    n = len(s1)

    def body(*refs):
        _scatter_start(refs[:n], refs[n:2 * n], refs[2 * n:])
        _scatter_finish(refs[:n], refs[n:2 * n], refs[2 * n:])

    return pl.pallas_call(
        body, name=name, in_specs=[ANY] * n, out_specs=[ANY] * n, out_shape=_scatter_shapes(s1),
        scratch_shapes=_scatter_sems(n),
    )(*s1)


def share_with_sibling(parts):
    n = len(parts)

    def body(*refs):
        out_refs = refs[n:2 * n]
        ssem, rsem = refs[2 * n:]
        x, y, c, j = _place()
        cps = []
        for k in range(n):
            rows = _my_half(out_refs[k].shape[1], c)
            for l in range(DEPTH):
                cp = pltpu.make_async_remote_copy(
                    src_ref=out_refs[k].at[l, rows], dst_ref=out_refs[k].at[l, rows], send_sem=ssem.at[k, l],
                    recv_sem=rsem.at[k, l], device_id=(x, y, 1 - c), device_id_type=MESH)
                cp.start()
                cps.append(cp)
        for cp in cps:
            cp.wait()

    return pl.pallas_call(
        body, name="share_with_sibling", in_specs=[ANY] * n, out_specs=[ANY] * n,
        out_shape=[jax.ShapeDtypeStruct(a.shape, F32) for a in parts],
        input_output_aliases={k: k for k in range(n)},
        scratch_shapes=[pltpu.SemaphoreType.DMA((n, DEPTH)), pltpu.SemaphoreType.DMA((n, DEPTH))],
    )(*parts)


def add_core_halves(g, got, cidx, name):
    nch, r, cdim = g.shape
    half = r // 2
    tr = _row_tile(half, cdim, mult=16)
    per = half // tr

    def body(c_ref, a_ref, b_ref, o_ref):
        o_ref[...] = _bf(a_ref[...] + b_ref[...])

    grid_spec = pltpu.PrefetchScalarGridSpec(
        num_scalar_prefetch=1, grid=(nch, per),
        in_specs=[pl.BlockSpec((None, tr, cdim), lambda jj, i, c_ref: (jj, c_ref[0] * per + i, 0)),
                  pl.BlockSpec((None, tr, cdim), lambda jj, i, c_ref: (jj, i, 0))],
        out_specs=pl.BlockSpec((None, tr, cdim), lambda jj, i, c_ref: (jj, i, 0)))
    return pl.pallas_call(
        body, name=name, grid_spec=grid_spec,
        out_shape=jax.ShapeDtypeStruct((nch, half, cdim), BF16), compiler_params=_cparams(2),
    )(cidx, g, got)


def add_chip_parts(s1, got, jc, layer, into, name):
    _, half, cdim = s1.shape
    tr = _row_tile(half, cdim, mult=16)
    per = half // tr

    def body(jc_ref, a_ref, g0_ref, g1_ref, g2_ref, *rest):
        rest[-1][...] = ((a_ref[...].astype(F32) + g0_ref[...].astype(F32)) + g1_ref[...].astype(F32)) + g2_ref[...].astype(F32)

    def slot(k):
        return pl.BlockSpec((None, tr, cdim), lambda i, jc_ref: (k, i, 0))

    in_specs = [pl.BlockSpec((None, tr, cdim), lambda i, jc_ref: (jc_ref[0], i, 0)), slot(0), slot(1), slot(2)]
    ops = [jc, s1, got, got, got]
    aliases = {}
    if into is not None:
        in_specs.append(ANY)
        ops.append(into)
        aliases = {5: 0}
    grid_spec = pltpu.PrefetchScalarGridSpec(
        num_scalar_prefetch=1, grid=(per,), in_specs=in_specs,
        out_specs=pl.BlockSpec((None, tr, cdim), lambda i, jc_ref: (layer, jc_ref[1] * per + i, 0)))
    return pl.pallas_call(
        body, name=name, grid_spec=grid_spec, input_output_aliases=aliases,
        out_shape=jax.ShapeDtypeStruct((DEPTH, 2 * half, cdim), F32), compiler_params=_cparams(1),
    )(*ops)


def sum_devices(allp):
    _, r, _ = allp.shape

    def body(a_ref, o_ref):
        tot = a_ref[0]
        for k in range(1, 8):
            tot = tot + a_ref[k]
        o_ref[...] = tot

    tr = r // 2 if r % 16 == 0 else r
    return pl.pallas_call(
        body, name="sum_devices", grid=(r // tr,),
        in_specs=[pl.BlockSpec((8, tr, LANES), lambda i: (0, i, 0))],
        out_specs=pl.BlockSpec((tr, LANES), lambda i: (i, 0)),
        out_shape=jax.ShapeDtypeStruct((r, LANES), F32), compiler_params=_cparams(1),
    )(allp)


def _row_tile(r, cdim, limit_bytes=1 << 20, mult=8):
    best = None
    for tr in range(mult, r + 1, mult):
        if r % tr == 0 and tr * cdim * 4 <= limit_bytes:
            best = tr
    return best if best is not None else r


def adamw(w, g, m, v, name):
    r, cdim = w.shape
    tr = _row_tile(r, cdim)
    bc1 = 1.0 - ADAM_B1 ** ADAM_STEP
    bc2 = 1.0 - ADAM_B2 ** ADAM_STEP

    def body(w_ref, g_ref, m_ref, v_ref, d_ref, nm_ref, nv_ref, go_ref):
        gv = g_ref[...]
        nm = ADAM_B1 * m_ref[...] + (1.0 - ADAM_B1) * gv
        nv = ADAM_B2 * v_ref[...] + (1.0 - ADAM_B2) * (gv * gv)
        d_ref[...] = -ADAM_LR * ((nm / bc1) / (jnp.sqrt(nv / bc2) + ADAM_EPS) + ADAM_WD * w_ref[...])
        nm_ref[...] = nm
        nv_ref[...] = nv
        go_ref[...] = gv

    blk = pl.BlockSpec((tr, cdim), lambda i: (i, 0))
    shape = jax.ShapeDtypeStruct((r, cdim), F32)
    return pl.pallas_call(
        body, name=name, grid=(r // tr,), in_specs=[blk] * 4, out_specs=[blk] * 4, out_shape=[shape] * 4,
        compiler_params=_cparams(1),
    )(w, g, m, v)


def _s5_prepare(lam_re, lam_im, log_dt, b_re, b_im, c_re, c_im):
    groups, ch = 16, 16
    dt = jnp.exp(log_dt)[:, None]
    mag = jnp.exp(lam_re * dt)
    a_r, a_i = mag * jnp.cos(lam_im * dt), mag * jnp.sin(lam_im * dt)
    den = lam_re * lam_re + lam_im * lam_im
    q_r = ((a_r - 1.0) * lam_re + a_i * lam_im) / den
    q_i = (a_i * lam_re - (a_r - 1.0) * lam_im) / den
    bb_r = q_r[..., None] * b_re - q_i[..., None] * b_im
    bb_i = q_r[..., None] * b_im + q_i[..., None] * b_re
    eye = jnp.eye(groups, dtype=F32)

    def expand_b(bb):
        return jnp.einsum("gpc,gh->gchp", bb, eye).reshape(groups * ch, N_STATE)

    def expand_c(cc):
        return jnp.einsum("gcp,gh->hpgc", cc, eye).reshape(N_STATE, groups * ch)

    a2 = jnp.concatenate([a_r.reshape(1, N_STATE), a_i.reshape(1, N_STATE)], axis=1)
    bexp = jnp.concatenate([expand_b(bb_r), expand_b(bb_i)], axis=1)
    cexp = jnp.concatenate([expand_c(c_re), -expand_c(c_im)], axis=0)
    return a2, bexp, cexp


def _lru_prepare(w_r, w_i, lam):
    heads = 4
    eye = jnp.eye(heads, dtype=F32)

    def expand(w):
        return jnp.einsum("hij,hk->hikj", w, eye).reshape(D_GROUP, D_GROUP)

    return expand(w_r), expand(w_i), jax.nn.softplus(-lam).reshape(1, D_GROUP)


def _pad_rows(a, rows):
    return jnp.pad(a, ((0, rows - a.shape[0]), (0, 0)))


def _group_mean_matrix():
    gidx = jnp.arange(D_GROUP) // 64
    return (gidx[:, None] == gidx[None, :]).astype(BF16) * jnp.asarray(1.0 / 64.0, BF16)


def _pack_rows(arrs, width):
    parts = []
    for a in arrs:
        flat = a.reshape(-1)
        pad = (-flat.shape[0]) % width
        parts.append(jnp.pad(flat, (0, pad)) if pad else flat)
    flat = jnp.concatenate(parts)
    rows = flat.shape[0] // width
    pad_rows = (-rows) % 16
    if pad_rows:
        flat = jnp.pad(flat, (0, pad_rows * width))
    return flat.reshape(-1, width)


def _unpack_rows(packed, shapes, width):
    flat = packed.reshape(-1)
    out, off = [], 0
    for shp in shapes:
        size = math.prod(shp)
        out.append(flat[off:off + size].reshape(shp))
        off += size + ((-size) % width)
    return out


def kernel(x, mem, ln_in_g, ln_in_b, w_in, b_in, s5_lam_re, s5_lam_im, s5_log_dt, s5_b_re, s5_b_im, s5_c_re, s5_c_im, s5_d, s5_w_glu, s5_b_glu, cv_w, cv_b, cv_gn_g, cv_gn_b, cv_w_pw, cv_b_pw, lru_conv_w, lru_conv_b, lru_w_r, lru_b_r, lru_w_i, lru_b_i, lru_lam, attn_w_kv, w_out, b_out, ln1_g, ln1_b, ffn_w_up, ffn_conv_w, ffn_conv_b, ffn_w_down, ln2_g, ln2_b, loss_target, m_ln_in_g, m_ln_in_b, m_w_in, m_b_in, m_s5_lam_re, m_s5_lam_im, m_s5_log_dt, m_s5_b_re, m_s5_b_im, m_s5_c_re, m_s5_c_im, m_s5_d, m_s5_w_glu, m_s5_b_glu, m_cv_w, m_cv_b, m_cv_gn_g, m_cv_gn_b, m_cv_w_pw, m_cv_b_pw, m_lru_conv_w, m_lru_conv_b, m_lru_w_r, m_lru_b_r, m_lru_w_i, m_lru_b_i, m_lru_lam, m_attn_w_kv, m_w_out, m_b_out, m_ln1_g, m_ln1_b, m_ffn_w_up, m_ffn_conv_w, m_ffn_conv_b, m_ffn_w_down, m_ln2_g, m_ln2_b, v_ln_in_g, v_ln_in_b, v_w_in, v_b_in, v_s5_lam_re, v_s5_lam_im, v_s5_log_dt, v_s5_b_re, v_s5_b_im, v_s5_c_re, v_s5_c_im, v_s5_d, v_s5_w_glu, v_s5_b_glu, v_cv_w, v_cv_b, v_cv_gn_g, v_cv_gn_b, v_cv_w_pw, v_cv_b_pw, v_lru_conv_w, v_lru_conv_b, v_lru_w_r, v_lru_b_r, v_lru_w_i, v_lru_b_i, v_lru_lam, v_attn_w_kv, v_w_out, v_b_out, v_ln1_g, v_ln1_b, v_ffn_w_up, v_ffn_conv_w, v_ffn_conv_b, v_ffn_w_down, v_ln2_g, v_ln2_b):
    p = dict(locals())
    xs = x[0]
    mems = mem[0]
    target = loss_target[0]
    s = xs.shape[0]
    cidx = lax.axis_index("c")
    jidx = 2 * lax.axis_index("x") + lax.axis_index("y")
    tb_scan = min(256, s)
    tb_s5 = min(512, s)
    tb_attn = min(512, s)
    tb_ffn = min(512, s)

    small_sh_names = list(SMALL_SHARDED)
    small_sh_shapes = [p[nm].shape[1:] for nm in small_sh_names]
    slabs = [[_own_slab(_bf(p[nm][l]), jidx) for nm in BIG]
             + [_own_slab(_pack_rows([p[nm][l] for nm in small_sh_names], LANES), jidx)] for l in range(DEPTH)]
    n_slabs = len(BIG) + 1
    first_needed = FIRST_NEEDED + [len(BIG)]
    arrive_later = [k for k in range(n_slabs) if k not in first_needed]
    gathered = [[None] * n_slabs for _ in range(DEPTH)]
    for k, slab in zip(first_needed, gather_weights([slabs[0][k] for k in first_needed])):
        gathered[0][k] = slab

    def weight_views(gw, which):
        shapes = {0: (1, N_CHIPS, D_MODEL, N_IN // N_CHIPS), 1: (1, 1, D_MODEL, 2 * D_GROUP), 2: (1, 1, D_MODEL, D_MODEL),
                  3: (1, N_CHIPS, D_MODEL, 2 * D_FF // N_CHIPS), 4: (1, 1, D_FF, D_MODEL),
                  5: (1, D_GROUP, D_GROUP), 6: (1, D_GROUP, D_GROUP)}
        views = {BIG[k]: gw[k].reshape(shapes[k]) for k in which if k < len(BIG)}
        if len(BIG) in which:
            per_chip = [_unpack_rows(gw[len(BIG)][jj], small_sh_shapes, LANES) for jj in range(N_CHIPS)]
            for k, nm in enumerate(small_sh_names):
                views[nm] = jnp.concatenate([per_chip[jj][k] for jj in range(N_CHIPS)], axis=SMALL_SHARDED[nm] - 1)
        return views

    pmat = _group_mean_matrix()

    def vec(a):
        return a.reshape(1, -1)

    xh0, rs0, xb0 = ln_fwd(xs, vec(ln_in_g), vec(ln_in_b), "ln_in")
    saved = []
    prev = dict(xh=xh0, rs=rs0, xb=xb0, g=vec(ln_in_g), b=vec(ln_in_b))
    for l in range(DEPTH):
        sv = dict(prev=prev)
        (a2, bexp, cexp), sv['s5_vjp'] = jax.vjp(_s5_prepare, s5_lam_re[l], s5_lam_im[l], s5_log_dt[l],
                                                 s5_b_re[l], s5_b_im[l], s5_c_re[l], s5_c_im[l])
        (wr, wi, sp), sv['lru_vjp'] = jax.vjp(_lru_prepare, lru_w_r[l], lru_w_i[l], lru_lam[l])
        sv.update(a2=a2, bexp=_bf(bexp), cexp=_bf(cexp), wr=_bf(wr), wi=_bf(wi), sp=sp)
        late = arrive_later if l == 0 else []
        gw = sv['gw'] = weight_views(gathered[l], [k for k in range(n_slabs) if k not in late])
        sv['cvw'] = _pad_rows(gw['cv_w'], CV_HALO)
        sv['lcw'] = _pad_rows(gw['lru_conv_w'], LRU_HALO)
        sv['fcw'] = _pad_rows(gw['ffn_conv_w'], FFN_TAP_ROWS)
        sv['w_glu'], sv['w_pw'] = gw['s5_w_glu'], gw['cv_w_pw']
        h_in = mm_nn(prev['xb'], gw['w_in'], 0, vec(b_in[l]), F32, 2048, f"in_proj{l}")
        kv = mm_nn(mems, gw['attn_w_kv'], 0, jnp.zeros((1, 2 * D_GROUP), F32), F32, 256, f"kv_proj{l}")
        y_s5, hst, y0, *got = s5_fwd(h_in, a2, sv['bexp'], sv['cexp'], vec(s5_d[l]), sv['w_glu'], 0, vec(s5_b_glu[l]),
                                     tb_s5, f"s5_fwd{l}", gather=[slabs[l][k] for k in late])
        for k, slab in zip(late, got):
            gathered[l][k] = slab
        gw.update(weight_views(gathered[l], late))
        y_cv, hc = cv_fwd(h_in, sv['cvw'], vec(cv_b[l]), vec(cv_gn_g[l]), vec(cv_gn_b[l]), pmat, sv['w_pw'], 0,
                          vec(cv_b_pw[l]), tb_scan, f"cv_fwd{l}")
        y_lru, xcs, hls = lru_fwd(h_in, sv['lcw'], vec(lru_conv_b[l]), sv['wr'], vec(lru_b_r[l]), sv['wi'],
                                  vec(lru_b_i[l]), sp, tb_scan, f"lru_fwd{l}")
        y_mem = attn_fwd(h_in, kv, tb_attn, f"attn_fwd{l}")
        mix_in = jnp.concatenate([y_s5, y_cv, y_lru, y_mem], axis=1)
        xh1, rs1, xb1 = proj_ln(mix_in, gw['w_out'].reshape(1, D_MODEL, D_MODEL), 0, vec(b_out[l]),
                                prev['xh'], prev['g'], prev['b'], vec(ln1_g[l]), vec(ln1_b[l]), f"out_proj_ln{l}")
        u = mm_nn(xb1, gw['ffn_w_up'], 0, jnp.zeros((1, 2 * D_FF), F32), F32, 1024, f"ffn_up{l}")
        nxt = slabs[l + 1] if l + 1 < DEPTH else ()
        hf, sv['ffn_kept'], *got = ffn_act_fwd(u, sv['fcw'], vec(ffn_conv_b[l]), tb_ffn, f"ffn_act{l}", gather=nxt)
        if nxt:
            gathered[l + 1] = got
        xh2, rs2, xb2 = proj_ln(hf, gw['ffn_w_down'].reshape(1, D_FF, D_MODEL), 0, jnp.zeros((1, D_MODEL), F32),
                                xh1, vec(ln1_g[l]), vec(ln1_b[l]), vec(ln2_g[l]), vec(ln2_b[l]), f"ffn_down_ln{l}")
        sv.update(h_in=h_in, kv=kv, hst=hst, y0=y0, hc=hc, xcs=xcs, hls=hls, mix_in=mix_in,
                  xh1=xh1, rs1=rs1, xb1=xb1, u=u, hf=hf, xh2=xh2, rs2=rs2)
        saved.append(sv)
        prev = dict(xh=xh2, rs=rs2, xb=xb2, g=vec(ln2_g[l]), b=vec(ln2_b[l]))

    grads = {}
    per_layer = {nm: [None] * DEPTH for nm in WEIGHTS if nm not in ('ln_in_g', 'ln_in_b')}
    c1 = cidx.reshape(1).astype(jnp.int32)
    jc = jnp.stack([jidx, cidx]).astype(jnp.int32)
    red_big = [None] * len(BIG)
    handoff = None
    pending = None
    below = None

    def per_chip(gl, which):
        return [gl[k].reshape((N_CHIPS,) + p[BIG[k]].shape[1:]) for k in which]

    def core_sums(gs, got, which, tag):
        return [add_core_halves(g, ga, c1, f"add_cores_{BIG[k]}{tag}") for g, ga, k in zip(gs, got, which)]

    def chip_parts(gl, which, tag, small_all=None):
        gs = per_chip(gl, which)
        got = exchange_cores(gs, f"exchange_cores{tag}", small_all)
        return core_sums(gs, got, which, tag), (got[len(which)] if small_all is not None else None)

    def own_sum(which, parts, got, l):
        for k, part, gk in zip(which, parts, got):
            red_big[k] = add_chip_parts(part, gk, jc, l, red_big[k], f"add_chips_{BIG[k]}{l}")

    everything = list(range(len(BIG)))
    ready_early = [2, 3, 4]
    ready_last = [k for k in everything if k not in ready_early]

    for l in reversed(range(DEPTH)):
        sv = saved[l]
        pv = sv['prev']
        gw = sv['gw']
        gl = [None] * len(BIG)
        if l == DEPTH - 1:
            dr2, dg2, db2, sqerr = loss_ln_bwd(target, sv['xh2'], sv['rs2'], vec(ln2_g[l]), vec(ln2_b[l]), "loss_ln2_bwd")
            loss_local = 0.5 / D_MODEL * jnp.sum(sqerr)
        else:
            dr2, dg2, db2 = below
        per_layer['ln2_g'][l], per_layer['ln2_b'][l] = dg2[0], db2[0]
        tm_nt = min(512, s)
        ts_big = min(2048, s)
        whole = lambda a_ref, j: a_ref[...]
        dhf = mm_nt(dr2, (tm_nt, D_MODEL), lambda i: (i, 0), whole, gw['ffn_w_down'], 0, None, F32, 512, s, f"ffn_down_dx{l}")
        above = per_chip(*handoff) if handoff is not None else []
        res = mm_tn(sv['hf'], dr2, (min(1024, s), D_MODEL), lambda kt, j, st: (st, 0), 1, D_MODEL, FFN_TN, 1024, s,
                    f"ffn_down_dw{l}", exchange=above)
        gl[4] = res[0] if above else res
        if handoff is not None:
            pending = (l + 1, handoff[1], core_sums(above, res[1:], handoff[1], str(l + 1)))
            handoff = None
        waiting = pending[2] if pending is not None else ()
        du, dcwv, dcwg, *got = ffn_act_bwd(dhf, sv['u'], sv['ffn_kept'], sv['fcw'], tb_ffn, f"ffn_act_bwd{l}",
                                           scatter=waiting)
        if pending is not None:
            own_sum(pending[1], pending[2], got, pending[0])
            pending = None
        dcw = jnp.concatenate([dcwv, dcwg], axis=1)
        per_layer['ffn_conv_w'][l] = dcw[0:FFN_CONV_WIDTH]
        per_layer['ffn_conv_b'][l] = dcw[FFN_CONV_WIDTH]
        dr1, dg1, db1, cs1 = mm_nt(du, (2, tm_nt, D_FF), lambda i: (0, i, 0),
                                   lambda a_ref, j: a_ref[j // 2, :, (j % 2) * FFN_TN:(j % 2 + 1) * FFN_TN], gw['ffn_w_up'], 0, dr2,
                                   F32, 512, s, f"ffn_up_dx_ln1_bwd{l}", ln=(sv['xh1'], sv['rs1'], vec(ln1_g[l])))
        gl[3] = mm_tn(sv['xb1'], du, (None, ts_big, FFN_TN), lambda kt, j, st: (j // 2, st, j % 2), N_CHIPS, FFN_TN,
                      D_MODEL, 2048, s, f"ffn_up_dw{l}")
        per_layer['ln1_g'][l], per_layer['ln1_b'][l], per_layer['b_out'][l] = dg1[0], db1[0], cs1[0]
        dmix = mm_nt(dr1, (tm_nt, D_MODEL), lambda i: (i, 0), whole, gw['w_out'], 0, None, F32, 512, s, f"out_proj_dx{l}")
        gl[2] = mm_tn(sv['mix_in'], dr1, (min(1024, s), D_MODEL), lambda kt, j, st: (st, 0), 1, D_MODEL, D_MODEL, 1024, s,
                      f"out_proj_dw{l}")
        h_in = sv['h_in']
        early_gs = per_chip(gl, ready_early) if l == 0 else []
        d_q, cs_q, d_kv, *got = attn_bwd(dmix, h_in, sv['kv'], tb_attn, f"attn_bwd{l}", exchange=early_gs)
        early = core_sums(early_gs, got, ready_early, f"{l}a") if l == 0 else ()
        (d_u, cs_u, d_bexp, d_cexp, d_dd, d_wglu, d_bglu, d_a2) = s5_bwd(
            dmix, h_in, sv['y0'], sv['hst'], sv['a2'], sv['bexp'], sv['cexp'], vec(s5_d[l]), sv['w_glu'], 0,
            vec(s5_b_glu[l]), tb_s5, f"s5_bwd{l}")
        (d_vg, cs_vg, d_cvw, d_cvb, d_gg, d_gb, d_wpw, d_bpw, *got) = cv_bwd(
            dmix, h_in, sv['hc'], sv['cvw'], vec(cv_gn_g[l]), vec(cv_gn_b[l]), pmat, sv['w_pw'], 0, tb_scan, f"cv_bwd{l}",
            scatter=early)
        if l == 0:
            own_sum(ready_early, early, got, l)
        (d_lx, cs_lx, d_lcw, d_lcb, d_wr, d_br, d_wi, d_bi, d_sp) = lru_bwd(
            dmix, h_in, sv['xcs'], sv['hls'], sv['lcw'], sv['wr'], vec(lru_b_r[l]), sv['wi'], vec(lru_b_i[l]),
            sv['sp'], tb_scan, f"lru_bwd{l}")
        g_s5 = sv['s5_vjp']((d_a2, d_bexp, d_cexp))
        for nm, gval in zip(['s5_lam_re', 's5_lam_im', 's5_log_dt', 's5_b_re', 's5_b_im', 's5_c_re', 's5_c_im'], g_s5):
            per_layer[nm][l] = gval
        g_lru = sv['lru_vjp']((d_wr, d_wi, d_sp))
        for nm, gval in zip(['lru_w_r', 'lru_w_i', 'lru_lam'], g_lru):
            per_layer[nm][l] = gval
        per_layer['s5_d'][l], per_layer['s5_b_glu'][l] = d_dd[0], d_bglu[0]
        per_layer['cv_w'][l], per_layer['cv_b'][l] = d_cvw[0:CONV_WIDTH], d_cvb[0]
        per_layer['cv_gn_g'][l], per_layer['cv_gn_b'][l] = d_gg[0], d_gb[0]
        per_layer['cv_b_pw'][l] = d_bpw[0]
        gl[5], gl[6] = d_wglu, d_wpw
        per_layer['lru_conv_w'][l], per_layer['lru_conv_b'][l] = d_lcw[0:LRU_CONV_WIDTH], d_lcb[0]
        per_layer['lru_b_r'][l], per_layer['lru_b_i'][l] = d_br[0], d_bi[0]
        per_layer['b_in'][l] = jnp.concatenate([cs_u, cs_vg, cs_lx, cs_q], axis=1)[0]
        gl[1] = mm_tn(mems, d_kv, (MEM_ROWS, 2 * D_GROUP), lambda kt, j, st: (st, 0), 1, 2 * D_GROUP, D_MODEL, MEM_ROWS,
                      MEM_ROWS, f"kv_proj_dw{l}")
        dh_in = jnp.concatenate([d_u, d_vg, d_lx, d_q], axis=1)
        n_sh = N_IN // N_CHIPS
        below = mm_nt(dh_in, (tm_nt, N_IN), lambda i: (i, 0), lambda a_ref, j: a_ref[:, j * n_sh:(j + 1) * n_sh],
                      gw['w_in'], 0, dr1, F32, 512, s, f"in_proj_dx_ln_bwd{l}", ln=(pv['xh'], pv['rs'], pv['g']))[:3]
        gl[0] = mm_tn(pv['xb'], dh_in, (ts_big, n_sh), lambda kt, j, st: (st, j), N_CHIPS, n_sh, D_MODEL, 2048, s,
                      f"in_proj_dw{l}")
        if l > 0:
            handoff = (gl, everything)
    grad_x, dg_in, db_in = below
    grads['ln_in_g'], grads['ln_in_b'] = dg_in[0], db_in[0]
    for nm, vals in per_layer.items():
        if nm not in BIG:
            grads[nm] = jnp.stack(vals)

    small_names = [nm for nm in WEIGHTS if nm not in BIG]
    small_local = _pack_rows([grads[nm] for nm in small_names], LANES)
    me = 4 * lax.axis_index("x") + 2 * lax.axis_index("y") + cidx
    small_slab = lax.dynamic_update_slice_in_dim(lax.empty((8,) + small_local.shape, F32), small_local[None], me, axis=0)
    parts, small_all = chip_parts(gl, ready_last, "0b", small_slab)
    own_sum(ready_last, parts, scatter_shards(parts, "scatter_shards0"), 0)
    red_big = share_with_sibling(red_big)
    small_red = sum_devices(small_all)
    small_grads = dict(zip(small_names, _unpack_rows(small_red, [grads[nm].shape for nm in small_names], LANES)))

    out_g, out_d, out_m, out_v = {}, {}, {}, {}
    for k, nm in enumerate(BIG):
        gk = red_big[k]
        two_d = (-1, p[nm].shape[-1])
        res = adamw(p[nm].reshape(two_d), gk.reshape(two_d), p['m_' + nm].reshape(two_d),
                    p['v_' + nm].reshape(two_d), f"adamw_{nm}")
        out_d[nm], out_m[nm], out_v[nm], out_g[nm] = (t.reshape(p[nm].shape) for t in res)
    own = {}
    for nm in small_names:
        gfull = small_grads[nm]
        if nm in SMALL_SHARDED:
            ax = SMALL_SHARDED[nm]
            width = p[nm].shape[ax]
            gfull = lax.dynamic_slice_in_dim(gfull, jidx * width, width, axis=ax)
        own[nm] = gfull
    packs = [_pack_rows([src[nm] for nm in small_names], LANES)
             for src in (dict((nm, p[nm]) for nm in small_names), own,
                         dict((nm, p['m_' + nm]) for nm in small_names), dict((nm, p['v_' + nm]) for nm in small_names))]
    dlt, nm_, nv_, _ = adamw(packs[0], packs[1], packs[2], packs[3], "adamw_small")
    shapes = [p[nm].shape for nm in small_names]
    for dst, packed in ((out_d, dlt), (out_m, nm_), (out_v, nv_)):
        dst.update(zip(small_names, _unpack_rows(packed, shapes, LANES)))
    out_g.update(own)

    loss = lax.psum(loss_local, ("x", "y", "c"))
    return (loss, grad_x[None], *[out_g[nm] for nm in WEIGHTS], *[out_d[nm] for nm in WEIGHTS],
            *[out_m[nm] for nm in WEIGHTS], *[out_v[nm] for nm in WEIGHTS])
```

```python
import functools
import math

import jax
import jax.numpy as jnp
from jax import lax
from jax.experimental import pallas as pl
from jax.experimental.pallas import tpu as pltpu

F32 = jnp.float32
BF16 = jnp.bfloat16
MESH = pl.DeviceIdType.MESH
ANY = pl.BlockSpec(memory_space=pl.ANY)

DEPTH = 2
D_MODEL = 1024
D_GROUP = 256
N_IN = 6 * D_GROUP
D_FF = 2816
N_STATE = 1024
CONV_WIDTH = 31
LRU_CONV_WIDTH = 4
FFN_CONV_WIDTH = 3
LRU_C = 8.0
ALPHA = (2 * DEPTH) ** 0.25
LN_EPS = 1e-5
N_CHIPS = 4
MEM_ROWS = 256
LANES = 128
SUBLANES = 8
VMEM_LIMIT = 56 * 1024 * 1024

ADAM_LR, ADAM_B1, ADAM_B2, ADAM_EPS, ADAM_WD, ADAM_STEP = 0.001, 0.9, 0.999, 1e-08, 0.01, 10

WEIGHTS = ['ln_in_g', 'ln_in_b', 'w_in', 'b_in', 's5_lam_re', 's5_lam_im', 's5_log_dt', 's5_b_re', 's5_b_im',
           's5_c_re', 's5_c_im', 's5_d', 's5_w_glu', 's5_b_glu', 'cv_w', 'cv_b', 'cv_gn_g', 'cv_gn_b', 'cv_w_pw',
           'cv_b_pw', 'lru_conv_w', 'lru_conv_b', 'lru_w_r', 'lru_b_r', 'lru_w_i', 'lru_b_i', 'lru_lam',
           'attn_w_kv', 'w_out', 'b_out', 'ln1_g', 'ln1_b', 'ffn_w_up', 'ffn_conv_w', 'ffn_conv_b', 'ffn_w_down',
           'ln2_g', 'ln2_b']
BIG = ['w_in', 'attn_w_kv', 'w_out', 'ffn_w_up', 'ffn_w_down', 's5_w_glu', 'cv_w_pw']
FIRST_NEEDED = [0, 1, 5, 6]
SMALL_SHARDED = {'cv_w': 2, 'lru_conv_w': 2, 'ffn_conv_w': 2}


def _cparams(n_axes):
    return pltpu.CompilerParams(dimension_semantics=("arbitrary",) * n_axes, vmem_limit_bytes=VMEM_LIMIT)


def _dot(a, b):
    return jnp.dot(a, b, preferred_element_type=F32)


def _dot_nt(a, b):
    return lax.dot_general(a, b, (((1,), (1,)), ((), ())), preferred_element_type=F32)


def _dot_tn(a, b):
    return lax.dot_general(a, b, (((0,), (0,)), ((), ())), preferred_element_type=F32)


def _bf(v):
    return v.astype(BF16)


def _colsum(v):
    return jnp.sum(v, axis=0, keepdims=True)


def _dot3(v, p):
    hi = _bf(v)
    r1 = v - hi.astype(F32)
    mid = _bf(r1)
    lo = _bf(r1 - mid.astype(F32))
    return _dot(hi, p) + _dot(mid, p) + _dot(lo, p)


_GELU_C = math.sqrt(2.0 / math.pi)


_GELU_C3 = _GELU_C * 0.044715


def _gelu_parts(v):
    t = jnp.tanh(v * (_GELU_C + _GELU_C3 * (v * v)))
    hv = 0.5 * v
    return hv + hv * t, t


def _gelu(v):
    return _gelu_parts(v)[0]


def _gelu_grad(v, t):
    return (0.5 + 0.5 * t) + (0.5 * v) * (1.0 - t * t) * (_GELU_C + (3.0 * _GELU_C3) * (v * v))


def _sigmoid(v):
    return 1.0 / (1.0 + jnp.exp(-v))


def _acc(ref, val, first):
    @pl.when(first)
    def _():
        ref[...] = val

    @pl.when(jnp.logical_not(first))
    def _():
        ref[...] += val


def _rows(shape):
    return lax.broadcasted_iota(jnp.int32, shape, 0)


def _ln_rows(r):
    mu = jnp.mean(r, -1, keepdims=True)
    rc = r - mu
    var = jnp.mean(rc * rc, -1, keepdims=True)
    rs = lax.rsqrt(var + LN_EPS)
    return rc * rs, rs


def ln_fwd(x, g, b, name):
    s = x.shape[0]
    tm = min(512, s)

    def body(x_ref, g_ref, b_ref, xh_ref, rs_ref, xb_ref):
        xh, rs = _ln_rows(x_ref[...])
        xh_ref[...] = xh
        rs_ref[...] = rs
        xb_ref[...] = _bf(xh * g_ref[...] + b_ref[...])

    row = pl.BlockSpec((tm, D_MODEL), lambda i: (i, 0))
    vec = pl.BlockSpec((1, D_MODEL), lambda i: (0, 0))
    return pl.pallas_call(
        body, name=name, grid=(s // tm,),
        in_specs=[row, vec, vec],
        out_specs=[row, pl.BlockSpec((tm, 1), lambda i: (i, 0)), row],
        out_shape=[jax.ShapeDtypeStruct((s, D_MODEL), F32), jax.ShapeDtypeStruct((s, 1), F32),
                   jax.ShapeDtypeStruct((s, D_MODEL), BF16)],
        compiler_params=_cparams(1),
    )(x, g, b)


def proj_ln(a, w, layer, bias, xh_prev, g_prev, b_prev, g, b, name):
    s, k = a.shape
    tm = min(512, s)

    def body(a_ref, w_ref, bias_ref, xp_ref, gp_ref, bp_ref, g_ref, b_ref, xh_ref, rs_ref, xb_ref):
        acc = _dot(a_ref[...], w_ref[...]) + bias_ref[...]
        r = ALPHA * (xp_ref[...] * gp_ref[...] + bp_ref[...]) + acc
        xh, rs = _ln_rows(r)
        xh_ref[...] = xh
        rs_ref[...] = rs
        xb_ref[...] = _bf(xh * g_ref[...] + b_ref[...])

    row = pl.BlockSpec((tm, D_MODEL), lambda i: (i, 0))
    vec = pl.BlockSpec((1, D_MODEL), lambda i: (0, 0))
    return pl.pallas_call(
        body, name=name, grid=(s // tm,),
        in_specs=[pl.BlockSpec((tm, k), lambda i: (i, 0)),
                  pl.BlockSpec((None, k, D_MODEL), lambda i: (layer, 0, 0)),
                  vec, row, vec, vec, vec, vec],
        out_specs=[row, pl.BlockSpec((tm, 1), lambda i: (i, 0)), row],
        out_shape=[jax.ShapeDtypeStruct((s, D_MODEL), F32), jax.ShapeDtypeStruct((s, 1), F32),
                   jax.ShapeDtypeStruct((s, D_MODEL), BF16)],
        compiler_params=_cparams(1),
    )(a, w, bias, xh_prev, g_prev, b_prev, g, b)


def loss_ln_bwd(target, xh, rs, g, b, name):
    s = xh.shape[0]
    tm = min(512, s)

    def body(t_ref, xh_ref, rs_ref, g_ref, b_ref, dr_ref, dg_ref, db_ref, sq_ref):
        first = pl.program_id(0) == 0
        xhv = xh_ref[...]
        err = xhv * g_ref[...] + b_ref[...] - t_ref[...]
        dyv = err * (1.0 / D_MODEL)
        dxh = dyv * g_ref[...]
        dr = rs_ref[...] * (dxh - jnp.mean(dxh, -1, keepdims=True) - xhv * jnp.mean(dxh * xhv, -1, keepdims=True))
        dr_ref[...] = dr
        _acc(dg_ref, _colsum(dyv * xhv), first)
        _acc(db_ref, _colsum(dyv), first)
        _acc(sq_ref, _colsum(err * err), first)

    row = pl.BlockSpec((tm, D_MODEL), lambda i: (i, 0))
    vec = pl.BlockSpec((1, D_MODEL), lambda i: (0, 0))
    vshape = jax.ShapeDtypeStruct((1, D_MODEL), F32)
    return pl.pallas_call(
        body, name=name, grid=(s // tm,),
        in_specs=[row, row, pl.BlockSpec((tm, 1), lambda i: (i, 0)), vec, vec],
        out_specs=[row, vec, vec, vec],
        out_shape=[jax.ShapeDtypeStruct((s, D_MODEL), F32), vshape, vshape, vshape],
        compiler_params=_cparams(1),
    )(target, xh, rs, g, b)


def mm_nn(a, w, layer, bias, out_dtype, tm, name):
    m, k = a.shape
    _, nj, _, n = w.shape
    tm = min(tm, m)

    def body(a_ref, w_ref, b_ref, o_ref):
        o_ref[...] = (_dot(_bf(a_ref[...]), w_ref[...]) + b_ref[...]).astype(out_dtype)

    return pl.pallas_call(
        body, name=name, grid=(nj, m // tm),
        in_specs=[pl.BlockSpec((tm, k), lambda j, i: (i, 0)),
                  pl.BlockSpec((None, None, k, n), lambda j, i: (layer, j, 0, 0)),
                  pl.BlockSpec((1, n), lambda j, i: (0, j))],
        out_specs=pl.BlockSpec((tm, n), lambda j, i: (i, j)),
        out_shape=jax.ShapeDtypeStruct((m, nj * n), out_dtype),
        compiler_params=_cparams(2),
    )(a, w, bias)


def mm_nt(a, a_block, a_map, pick, w, layer, add, out_dtype, tm, m, name, ln=None):
    _, nj, r, n = w.shape
    tm = min(tm, m)
    has_add = add is not None
    n_in = 2 + has_add + (3 if ln is not None else 0)

    def body(*refs):
        a_ref, w_ref = refs[0], refs[1]
        res = _dot_nt(_bf(pick(a_ref, 0)), w_ref[0])
        for j in range(1, nj):
            res = res + _dot_nt(_bf(pick(a_ref, j)), w_ref[j])
        if has_add:
            res = res + ALPHA * refs[2][...]
        if ln is None:
            refs[n_in][...] = res.astype(out_dtype)
            return
        xh_ref, rs_ref, g_ref = refs[n_in - 3:n_in]
        dr_ref, dg_ref, db_ref, cs_ref = refs[n_in:]
        first = pl.program_id(0) == 0
        xhv = xh_ref[...]
        dxh = res * g_ref[...]
        dr = rs_ref[...] * (dxh - jnp.mean(dxh, -1, keepdims=True) - xhv * jnp.mean(dxh * xhv, -1, keepdims=True))
        dr_ref[...] = dr
        _acc(dg_ref, _colsum(res * xhv), first)
        _acc(db_ref, _colsum(res), first)
        _acc(cs_ref, _colsum(dr), first)

    row = pl.BlockSpec((tm, r), lambda i: (i, 0))
    in_specs = [pl.BlockSpec(a_block, a_map),
                pl.BlockSpec((None, nj, r, n), lambda i: (layer, 0, 0, 0))]
    ops = [a, w]
    if has_add:
        in_specs.append(row)
        ops.append(add)
    if ln is None:
        out_specs, out_shape = row, jax.ShapeDtypeStruct((m, r), out_dtype)
    else:
        vec = pl.BlockSpec((1, r), lambda i: (0, 0))
        vshape = jax.ShapeDtypeStruct((1, r), F32)
        in_specs += [row, pl.BlockSpec((tm, 1), lambda i: (i, 0)), vec]
        ops += list(ln)
        out_specs, out_shape = [row, vec, vec, vec], [jax.ShapeDtypeStruct((m, r), F32), vshape, vshape, vshape]
    return pl.pallas_call(
        body, name=name, grid=(m // tm,),
        in_specs=in_specs, out_specs=out_specs, out_shape=out_shape,
        compiler_params=_cparams(1),
    )(*ops)


def mm_tn(a, b, b_block, b_map, nj, n, tk, ts, s, name, exchange=()):
    kx = a.shape[1]
    ts = min(ts, s)
    ne = len(exchange)
    grid = (kx // tk, nj, s // ts)

    def body(a_ref, b_ref, *rest):
        g_refs, o_ref, got_refs, sems = rest[:ne], rest[ne], rest[ne + 1:2 * ne + 1], rest[2 * ne + 1:]
        pids = [pl.program_id(ax) for ax in range(3)]
        if ne:
            @pl.when((pids[0] == 0) & (pids[1] == 0) & (pids[2] == 0))
            def _():
                _exchange_start(g_refs, got_refs, sems)

        part = _dot_tn(_bf(a_ref[...]), _bf(b_ref[...]))
        _acc(o_ref, part, pids[2] == 0)
        if ne:
            @pl.when((pids[0] == grid[0] - 1) & (pids[1] == grid[1] - 1) & (pids[2] == grid[2] - 1))
            def _():
                _exchange_finish(g_refs, got_refs, sems)

    res = pl.pallas_call(
        body, name=name, grid=grid,
        in_specs=[pl.BlockSpec((ts, tk), lambda kt, j, st: (st, kt)), pl.BlockSpec(b_block, b_map)] + [ANY] * ne,
        out_specs=[pl.BlockSpec((None, tk, n), lambda kt, j, st: (j, kt, 0))] + [ANY] * ne,
        out_shape=[jax.ShapeDtypeStruct((nj, kx, n), F32)] + _exchange_shapes(exchange),
        scratch_shapes=_exchange_sems(ne) if ne else [],
        compiler_params=_cparams(3),
    )(a, b, *exchange)
    return res if ne else res[0]


S5_TAB_ROWS = 8 * SUBLANES


def _s5_scan_table(tab_ref, ar, ai, reverse):
    n = N_STATE
    row = _rows((SUBLANES, n))
    edge = SUBLANES - 1 if reverse else 0
    tab_ref[0:8, :] = jnp.where(row == edge, ar, 0.0)
    tab_ref[8:16, :] = jnp.where(row == edge, ai, 0.0)
    pr, pi = ar, ai
    for step, k in enumerate((1, 2, 4)):
        mask = row < SUBLANES - k if reverse else row >= k
        tab_ref[16 + 16 * step:24 + 16 * step, :] = jnp.where(mask, pr, 0.0)
        tab_ref[24 + 16 * step:32 + 16 * step, :] = jnp.where(mask, pi, 0.0)
        pr, pi = pr * pr - pi * pi, 2.0 * pr * pi


def _s5_scan(src_ref, dst_ref, tab_ref, edge_ref, tb, reverse, per_tile=None):
    n = N_STATE
    ng = tb // SUBLANES
    nq = n // LANES
    link = SUBLANES - 1 if reverse else 1

    def tile(ii, carry):
        g = ng - 1 - ii if reverse else ii
        rows = pl.ds(pl.multiple_of(g * SUBLANES, SUBLANES), SUBLANES)
        out = []
        for q in range(nq):
            cre = slice(q * LANES, (q + 1) * LANES)
            cim = slice(n + q * LANES, n + (q + 1) * LANES)
            lr, li = src_ref[rows, cre], src_ref[rows, cim]
            tr, ti = pltpu.roll(carry[2 * q], link, 0), pltpu.roll(carry[2 * q + 1], link, 0)
            kr, ki = tab_ref[0:8, cre], tab_ref[8:16, cre]
            lr, li = lr + kr * tr - ki * ti, li + kr * ti + ki * tr
            for step, k in enumerate((1, 2, 4)):
                amt = SUBLANES - k if reverse else k
                kr, ki = tab_ref[16 + 16 * step:24 + 16 * step, cre], tab_ref[24 + 16 * step:32 + 16 * step, cre]
                sr, si = pltpu.roll(lr, amt, 0), pltpu.roll(li, amt, 0)
                lr, li = lr + kr * sr - ki * si, li + kr * si + ki * sr
            dst_ref[rows, cre] = lr
            dst_ref[rows, cim] = li
            if per_tile is not None:
                per_tile(g, q, (cre, cim), lr, li)
            out += [lr, li]
        return tuple(out)

    init = []
    for q in range(nq):
        init += [edge_ref[:, q * LANES:(q + 1) * LANES], edge_ref[:, n + q * LANES:n + (q + 1) * LANES]]
    fin = lax.fori_loop(0, ng, tile, tuple(init))
    for q in range(nq):
        edge_ref[:, q * LANES:(q + 1) * LANES] = fin[2 * q]
        edge_ref[:, n + q * LANES:n + (q + 1) * LANES] = fin[2 * q + 1]


def s5_fwd(h_in, a2, bexp, cexp, dskip, wglu, layer, bglu, tb, name, gather=()):
    s = h_in.shape[0]
    n = N_STATE
    ng = len(gather)

    def body(u_ref, a_ref, b_ref, c_ref, d_ref, w_ref, bg_ref, *rest):
        y_ref, h_ref, y0_ref = rest[ng:ng + 3]
        slabs = rest[ng + 3:2 * ng + 3]
        edge, tab, bu_ref = rest[2 * ng + 3:2 * ng + 6]
        sems = rest[2 * ng + 6:]

        @pl.when(pl.program_id(0) == 0)
        def _():
            if ng:
                _gather_start(slabs, sems[0:2])
            edge[...] = jnp.zeros_like(edge)
            _s5_scan_table(tab, a_ref[0:1, 0:n], a_ref[0:1, n:2 * n], False)

        u = u_ref[...]
        bu_ref[...] = _dot(_bf(u), b_ref[...])
        _s5_scan(bu_ref, h_ref, tab, edge, tb, False)
        y0 = _dot(_bf(h_ref[:, 0:n]), c_ref[0:n, :]) + _dot(_bf(h_ref[:, n:2 * n]), c_ref[n:2 * n, :]) + d_ref[...] * u
        y0_ref[...] = y0
        yg = _gelu(y0)
        z = _dot(_bf(yg), w_ref[...]) + bg_ref[...]
        y_ref[...] = _bf(yg * _sigmoid(z))
        if ng:
            @pl.when(pl.program_id(0) == s // tb - 1)
            def _():
                _gather_finish(slabs, sems[0:2], sems[2:4])

    vec = pl.BlockSpec((1, D_GROUP), lambda i: (0, 0))
    return pl.pallas_call(
        body, name=name, grid=(s // tb,),
        in_specs=[pl.BlockSpec((tb, D_GROUP), lambda i: (i, 0)),
                  pl.BlockSpec((1, 2 * n), lambda i: (0, 0)),
                  pl.BlockSpec((D_GROUP, 2 * n), lambda i: (0, 0)),
                  pl.BlockSpec((2 * n, D_GROUP), lambda i: (0, 0)),
                  vec,
                  pl.BlockSpec((None, D_GROUP, D_GROUP), lambda i: (layer, 0, 0)),
                  vec] + [ANY] * ng,
        out_specs=[pl.BlockSpec((tb, D_GROUP), lambda i: (i, 0)),
                   pl.BlockSpec((tb, 2 * n), lambda i: (i, 0)),
                   pl.BlockSpec((tb, D_GROUP), lambda i: (i, 0))] + [ANY] * ng,
        out_shape=[jax.ShapeDtypeStruct((s, D_GROUP), BF16), jax.ShapeDtypeStruct((s, 2 * n), F32),
                   jax.ShapeDtypeStruct((s, D_GROUP), F32)] + [jax.ShapeDtypeStruct(a.shape, a.dtype) for a in gather],
        input_output_aliases={7 + k: 3 + k for k in range(ng)},
        scratch_shapes=[pltpu.VMEM((SUBLANES, 2 * n), F32), pltpu.VMEM((S5_TAB_ROWS, n), F32),
                        pltpu.VMEM((tb, 2 * n), F32)] + (_gather_sems(ng) if ng else []),
        compiler_params=_cparams(1),
    )(h_in, a2, bexp, cexp, dskip, wglu, bglu, *gather)


def s5_bwd(dmix, h_in, y0, hst, a2, bexp, cexp, dskip, wglu, layer, bglu, tb, name):
    s = h_in.shape[0]
    n = N_STATE
    nb = s // tb
    halo = tb // 8

    def body(dy_ref, u_ref, y0_ref, h_ref, hp_ref, a_ref, b_ref, c_ref, d_ref, w_ref, bg_ref,
             du_ref, cs_ref, db_ref, dc_ref, dd_ref, dw_ref, dbg_ref, da_ref, edge, tab, g_ref, da_acc):
        i = pl.program_id(0)
        first = i == 0

        @pl.when(first)
        def _():
            edge[...] = jnp.zeros_like(edge)
            da_acc[...] = jnp.zeros_like(da_acc)
            _s5_scan_table(tab, a_ref[0:1, 0:n], -a_ref[0:1, n:2 * n], True)

        dy = dy_ref[...]
        u = u_ref[...]
        y0v = y0_ref[...]
        yg, t = _gelu_parts(y0v)
        z = _dot(_bf(yg), w_ref[...]) + bg_ref[...]
        sg = _sigmoid(z)
        dz = dy * yg * sg * (1.0 - sg)
        dyg = dy * sg + _dot_nt(_bf(dz), w_ref[...])
        _acc(dw_ref, _dot_tn(_bf(yg), _bf(dz)), first)
        _acc(dbg_ref, _colsum(dz), first)
        dy0 = dyg * _gelu_grad(y0v, t)
        _acc(dd_ref, _colsum(dy0 * u), first)
        dy0b = _bf(dy0)
        _acc(dc_ref.at[0:n, :], _dot_tn(_bf(h_ref[:, 0:n]), dy0b), first)
        _acc(dc_ref.at[n:2 * n, :], _dot_tn(_bf(h_ref[:, n:2 * n]), dy0b), first)
        g_ref[...] = _dot_nt(dy0b, c_ref[...])
        keep = jnp.where(i == nb - 1, 0.0, 1.0)
        row0 = _rows((SUBLANES, LANES)) == 0

        def grad_a(g, q, cols, gr, gi):
            cre, cim = cols
            rows = pl.ds(pl.multiple_of(g * SUBLANES, SUBLANES), SUBLANES)
            before = pl.ds(pl.multiple_of(jnp.maximum(g - 1, 0) * SUBLANES, SUBLANES), SUBLANES)
            pre = jnp.where(g == 0, hp_ref[:, cre] * keep, h_ref[before, cre])
            pim = jnp.where(g == 0, hp_ref[:, cim] * keep, h_ref[before, cim])
            pr = jnp.where(row0, pltpu.roll(pre, 1, 0), pltpu.roll(h_ref[rows, cre], 1, 0))
            pi = jnp.where(row0, pltpu.roll(pim, 1, 0), pltpu.roll(h_ref[rows, cim], 1, 0))
            da_acc[:, cre] += gr * pr + gi * pi
            da_acc[:, cim] += gi * pr - gr * pi

        _s5_scan(g_ref, g_ref, tab, edge, tb, True, grad_a)
        da_ref[...] = _colsum(da_acc[...])
        gr, gi = g_ref[:, 0:n], g_ref[:, n:2 * n]
        grb, gib = _bf(gr), _bf(gi)
        du = d_ref[...] * dy0 + _dot_nt(grb, b_ref[:, 0:n]) + _dot_nt(gib, b_ref[:, n:2 * n])
        ub = _bf(u)
        _acc(db_ref.at[:, 0:n], _dot_tn(ub, grb), first)
        _acc(db_ref.at[:, n:2 * n], _dot_tn(ub, gib), first)
        du_ref[...] = _bf(du)
        _acc(cs_ref, _colsum(du), first)

    rev = lambda i: (nb - 1 - i, 0)
    vec = pl.BlockSpec((1, D_GROUP), lambda i: (0, 0))
    vshape = jax.ShapeDtypeStruct((1, D_GROUP), F32)
    return pl.pallas_call(
        body, name=name, grid=(nb,),
        in_specs=[pl.BlockSpec((tb, D_GROUP), rev),
                  pl.BlockSpec((tb, D_GROUP), rev),
                  pl.BlockSpec((tb, D_GROUP), rev),
                  pl.BlockSpec((tb, 2 * n), rev),
                  pl.BlockSpec((8, 2 * n), lambda i: (jnp.maximum((nb - 1 - i) * halo - 1, 0), 0)),
                  pl.BlockSpec((1, 2 * n), lambda i: (0, 0)),
                  pl.BlockSpec((D_GROUP, 2 * n), lambda i: (0, 0)),
                  pl.BlockSpec((2 * n, D_GROUP), lambda i: (0, 0)),
                  vec,
                  pl.BlockSpec((None, D_GROUP, D_GROUP), lambda i: (layer, 0, 0)),
                  vec],
        out_specs=[pl.BlockSpec((tb, D_GROUP), rev), vec,
                   pl.BlockSpec((D_GROUP, 2 * n), lambda i: (0, 0)),
                   pl.BlockSpec((2 * n, D_GROUP), lambda i: (0, 0)),
                   vec,
                   pl.BlockSpec((D_GROUP, D_GROUP), lambda i: (0, 0)),
                   vec,
                   pl.BlockSpec((1, 2 * n), lambda i: (0, 0))],
        out_shape=[jax.ShapeDtypeStruct((s, D_GROUP), BF16), vshape,
                   jax.ShapeDtypeStruct((D_GROUP, 2 * n), F32), jax.ShapeDtypeStruct((2 * n, D_GROUP), F32),
                   vshape, jax.ShapeDtypeStruct((D_GROUP, D_GROUP), F32), vshape,
                   jax.ShapeDtypeStruct((1, 2 * n), F32)],
        scratch_shapes=[pltpu.VMEM((SUBLANES, 2 * n), F32), pltpu.VMEM((S5_TAB_ROWS, n), F32),
                        pltpu.VMEM((tb, 2 * n), F32), pltpu.VMEM((SUBLANES, 2 * n), F32)],
        compiler_params=_cparams(1),
    )(dmix, h_in, y0, hst, hst, a2, bexp, cexp, dskip, wglu, bglu)


CV_HALO = 32


def _gn_stats(hc, pmat):
    mu = _dot3(hc, pmat)
    xc = hc - mu
    var = _dot3(xc * xc, pmat)
    rstd = lax.rsqrt(var + LN_EPS)
    return xc * rstd, rstd


def cv_fwd(h_in, cw, cb, gg, gb, pmat, wpw, layer, bpw, tb, name):
    s = h_in.shape[0]
    hl = CV_HALO

    def body(v_ref, g_ref, cw_ref, cb_ref, gg_ref, gb_ref, p_ref, w_ref, bw_ref, y_ref, hc_ref, ext):
        @pl.when(pl.program_id(0) == 0)
        def _():
            ext[0:hl, :] = jnp.zeros((hl, D_GROUP), F32)

        ext[hl:hl + tb, :] = v_ref[...] * _sigmoid(g_ref[...])
        acc = jnp.zeros((tb, D_GROUP), F32) + cb_ref[...]
        for k in range(CONV_WIDTH):
            off = hl - (CONV_WIDTH - 1) + k
            acc = acc + cw_ref[k:k + 1, :] * ext[off:off + tb, :]
        hc_ref[...] = acc
        ext[0:hl, :] = ext[tb:tb + hl, :]
        xn, _ = _gn_stats(acc, p_ref[...])
        hn = xn * gg_ref[...] + gb_ref[...]
        hs = hn * _sigmoid(hn)
        y_ref[...] = _bf(_dot(_bf(hs), w_ref[...]) + bw_ref[...])

    vec = pl.BlockSpec((1, D_GROUP), lambda i: (0, 0))
    sq = pl.BlockSpec((D_GROUP, D_GROUP), lambda i: (0, 0))
    return pl.pallas_call(
        body, name=name, grid=(s // tb,),
        in_specs=[pl.BlockSpec((tb, D_GROUP), lambda i: (i, 1)),
                  pl.BlockSpec((tb, D_GROUP), lambda i: (i, 2)),
                  pl.BlockSpec((hl, D_GROUP), lambda i: (0, 0)),
                  vec, vec, vec, sq,
                  pl.BlockSpec((None, D_GROUP, D_GROUP), lambda i: (layer, 0, 0)),
                  vec],
        out_specs=[pl.BlockSpec((tb, D_GROUP), lambda i: (i, 0)), pl.BlockSpec((tb, D_GROUP), lambda i: (i, 0))],
        out_shape=[jax.ShapeDtypeStruct((s, D_GROUP), BF16), jax.ShapeDtypeStruct((s, D_GROUP), F32)],
        scratch_shapes=[pltpu.VMEM((hl + tb, D_GROUP), F32)],
        compiler_params=_cparams(1),
    )(h_in, h_in, cw, cb, gg, gb, pmat, wpw, bpw)


def cv_bwd(dmix, h_in, hc, cw, gg, gb, pmat, wpw, layer, tb, name, scatter=()):
    s = h_in.shape[0]
    hl = CV_HALO
    nb = s // tb
    per = tb // hl
    ns = len(scatter)

    def body(dy_ref, v_ref, g_ref, vh_ref, gh_ref, hc_ref, cw_ref, gg_ref, gb_ref, p_ref, w_ref, *rest):
        s_refs = rest[:ns]
        dvg_ref, cs_ref, dcw_ref, dcb_ref, dgg_ref, dgb_ref, dw_ref, dbw_ref = rest[ns:ns + 8]
        got_refs = rest[ns + 8:2 * ns + 8]
        ext, dext, head = rest[2 * ns + 8:2 * ns + 11]
        sems = rest[2 * ns + 11:]
        i = pl.program_id(0)
        first = i == 0

        @pl.when(first)
        def _():
            if ns:
                _scatter_start(s_refs, got_refs, sems)
            head[...] = jnp.zeros_like(head)

        dy = dy_ref[...]
        pm = p_ref[...]
        xn, rstd = _gn_stats(hc_ref[...], pm)
        hn = xn * gg_ref[...] + gb_ref[...]
        sg = _sigmoid(hn)
        hs = hn * sg
        dyb = _bf(dy)
        _acc(dbw_ref, _colsum(dy), first)
        _acc(dw_ref, _dot_tn(_bf(hs), dyb), first)
        dhs = _dot_nt(dyb, w_ref[...])
        dhn = dhs * sg * (1.0 + hn * (1.0 - sg))
        _acc(dgg_ref, _colsum(dhn * xn), first)
        _acc(dgb_ref, _colsum(dhn), first)
        dxn = dhn * gg_ref[...]
        dhc = rstd * (dxn - _dot3(dxn, pm) - xn * _dot3(dxn * xn, pm))
        _acc(dcb_ref, _colsum(dhc), first)
        v = v_ref[...]
        sgg = _sigmoid(g_ref[...])
        keep = jnp.where(i == nb - 1, 0.0, 1.0)
        ext[0:hl, :] = vh_ref[...] * _sigmoid(gh_ref[...]) * keep
        ext[hl:hl + tb, :] = v * sgg
        dext[0:tb, :] = dhc
        dext[tb:tb + hl, :] = head[...]
        head[...] = dhc[0:hl]
        dhg = jnp.zeros((tb, D_GROUP), F32)
        for k in range(CONV_WIDTH):
            off = hl - (CONV_WIDTH - 1) + k
            wk = _colsum(dhc * ext[off:off + tb, :])
            _acc(dcw_ref.at[k:k + 1, :], wk, first)
            back = CONV_WIDTH - 1 - k
            dhg = dhg + cw_ref[k:k + 1, :] * dext[back:back + tb, :]

        @pl.when(first)
        def _():
            dcw_ref[CONV_WIDTH:hl, :] = jnp.zeros((hl - CONV_WIDTH, D_GROUP), F32)

        dv = dhg * sgg
        dg = dhg * v * sgg * (1.0 - sgg)
        dvg_ref[:, 0:D_GROUP] = _bf(dv)
        dvg_ref[:, D_GROUP:2 * D_GROUP] = _bf(dg)
        _acc(cs_ref.at[:, 0:D_GROUP], _colsum(dv), first)
        _acc(cs_ref.at[:, D_GROUP:2 * D_GROUP], _colsum(dg), first)
        if ns:
            @pl.when(i == nb - 1)
            def _():
                _scatter_finish(s_refs, got_refs, sems)

    vec = pl.BlockSpec((1, D_GROUP), lambda i: (0, 0))
    sq = pl.BlockSpec((D_GROUP, D_GROUP), lambda i: (0, 0))
    tap = pl.BlockSpec((hl, D_GROUP), lambda i: (0, 0))
    vshape = jax.ShapeDtypeStruct((1, D_GROUP), F32)

    def blk(col):
        return pl.BlockSpec((tb, D_GROUP), lambda i: (nb - 1 - i, col))

    def halo_blk(col):
        return pl.BlockSpec((hl, D_GROUP), lambda i: (jnp.maximum((nb - 1 - i) * per - 1, 0), col))

    return pl.pallas_call(
        body, name=name, grid=(nb,),
        in_specs=[blk(1), blk(1), blk(2), halo_blk(1), halo_blk(2),
                  pl.BlockSpec((tb, D_GROUP), lambda i: (nb - 1 - i, 0)),
                  tap, vec, vec, sq,
                  pl.BlockSpec((None, D_GROUP, D_GROUP), lambda i: (layer, 0, 0))] + [ANY] * ns,
        out_specs=[pl.BlockSpec((tb, 2 * D_GROUP), lambda i: (nb - 1 - i, 0)),
                   pl.BlockSpec((1, 2 * D_GROUP), lambda i: (0, 0)),
                   tap, vec, vec, vec, sq, vec] + [ANY] * ns,
        out_shape=[jax.ShapeDtypeStruct((s, 2 * D_GROUP), BF16), jax.ShapeDtypeStruct((1, 2 * D_GROUP), F32),
                   jax.ShapeDtypeStruct((hl, D_GROUP), F32), vshape, vshape, vshape,
                   jax.ShapeDtypeStruct((D_GROUP, D_GROUP), F32), vshape] + _scatter_shapes(scatter),
        scratch_shapes=[pltpu.VMEM((hl + tb, D_GROUP), F32), pltpu.VMEM((tb + hl, D_GROUP), F32),
                        pltpu.VMEM((hl, D_GROUP), F32)] + (_scatter_sems(ns) if ns else []),
        compiler_params=_cparams(1),
    )(dmix, h_in, h_in, h_in, h_in, hc, cw, gg, gb, pmat, wpw, *scatter)


LRU_HALO = 8


def _lru_gates(xc, wr_ref, br_ref, wi_ref, bi_ref, sp_ref):
    xcb = _bf(xc)
    r = _sigmoid(_dot(xcb, wr_ref[...]) + br_ref[...])
    gi = _sigmoid(_dot(xcb, wi_ref[...]) + bi_ref[...])
    la = -LRU_C * r * sp_ref[...]
    a = jnp.exp(la)
    e2 = a * a
    sq = jnp.sqrt(-jnp.tanh(la) * (e2 + 1.0))
    return r, gi, a, e2, sq


def _rscan(a, b, tb, reverse):
    row = _rows(a.shape)
    sh = 1
    while sh < tb:
        if reverse:
            amt, mask = tb - sh, row < tb - sh
        else:
            amt, mask = sh, row >= sh
        a_s = jnp.where(mask, pltpu.roll(a, amt, 0), 1.0)
        b_s = jnp.where(mask, pltpu.roll(b, amt, 0), 0.0)
        b = b + a * b_s
        a = a * a_s
        sh *= 2
    return a, b


def lru_fwd(h_in, cw, cb, wr, br, wi, bi, sp, tb, name):
    s = h_in.shape[0]
    hl = LRU_HALO

    def body(xg_ref, xr_ref, cw_ref, cb_ref, wr_ref, br_ref, wi_ref, bi_ref, sp_ref, y_ref, xc_ref, h_ref, ext, carry):
        @pl.when(pl.program_id(0) == 0)
        def _():
            ext[0:hl, :] = jnp.zeros((hl, D_GROUP), F32)
            carry[...] = jnp.zeros_like(carry)

        ext[hl:hl + tb, :] = xr_ref[...]
        xc = jnp.zeros((tb, D_GROUP), F32) + cb_ref[...]
        for k in range(LRU_CONV_WIDTH):
            off = hl - (LRU_CONV_WIDTH - 1) + k
            xc = xc + cw_ref[k:k + 1, :] * ext[off:off + tb, :]
        xc_ref[...] = xc
        ext[0:hl, :] = ext[tb:tb + hl, :]
        r, gi, a, e2, sq = _lru_gates(xc, wr_ref, br_ref, wi_ref, bi_ref, sp_ref)
        pa, hloc = _rscan(a, sq * (gi * xc), tb, False)
        h = hloc + pa * carry[7:8, :]
        h_ref[...] = h
        carry[...] = h[tb - 8:tb]
        y_ref[...] = _bf(h * _gelu(xg_ref[...]))

    vec = pl.BlockSpec((1, D_GROUP), lambda i: (0, 0))
    sq_spec = pl.BlockSpec((D_GROUP, D_GROUP), lambda i: (0, 0))
    blk = pl.BlockSpec((tb, D_GROUP), lambda i: (i, 0))
    return pl.pallas_call(
        body, name=name, grid=(s // tb,),
        in_specs=[pl.BlockSpec((tb, D_GROUP), lambda i: (i, 3)),
                  pl.BlockSpec((tb, D_GROUP), lambda i: (i, 4)),
                  pl.BlockSpec((hl, D_GROUP), lambda i: (0, 0)),
                  vec, sq_spec, vec, sq_spec, vec, vec],
        out_specs=[blk, blk, blk],
        out_shape=[jax.ShapeDtypeStruct((s, D_GROUP), BF16), jax.ShapeDtypeStruct((s, D_GROUP), F32),
                   jax.ShapeDtypeStruct((s, D_GROUP), F32)],
        scratch_shapes=[pltpu.VMEM((hl + tb, D_GROUP), F32), pltpu.VMEM((8, D_GROUP), F32)],
        compiler_params=_cparams(1),
    )(h_in, h_in, cw, cb, wr, br, wi, bi, sp)


def lru_bwd(dmix, h_in, xcs, hs, cw, wr, br, wi, bi, sp, tb, name):
    s = h_in.shape[0]
    hl = LRU_HALO
    nb = s // tb
    per = tb // hl

    def body(dy_ref, xg_ref, xr_ref, xrh_ref, xc_ref, h_ref, hp_ref, cw_ref, wr_ref, br_ref, wi_ref, bi_ref, sp_ref,
             dx_ref, cs_ref, dcw_ref, dcb_ref, dwr_ref, dbr_ref, dwi_ref, dbi_ref, dsp_ref,
             ext, dext, head, anext, gnext):
        i = pl.program_id(0)
        first = i == 0

        @pl.when(first)
        def _():
            head[...] = jnp.zeros_like(head)
            anext[...] = jnp.zeros_like(anext)
            gnext[...] = jnp.zeros_like(gnext)

        dy = dy_ref[...]
        xg = xg_ref[...]
        xc = xc_ref[...]
        h = h_ref[...]
        r, gi, a, e2, sq = _lru_gates(xc, wr_ref, br_ref, wi_ref, bi_ref, sp_ref)
        gate, t = _gelu_parts(xg)
        dh = dy * gate
        dxg = dy * h * _gelu_grad(xg, t)
        row = _rows((tb, D_GROUP))
        coef = jnp.where(row == tb - 1, anext[0:1, :], pltpu.roll(a, tb - 1, 0))
        pc, gloc = _rscan(coef, dh, tb, True)
        gfull = gloc + pc * gnext[0:1, :]
        anext[...] = a[0:8]
        gnext[...] = gfull[0:8]
        keep = jnp.where(i == nb - 1, 0.0, 1.0)
        hprev = jnp.where(row == 0, hp_ref[7:8, :] * keep, pltpu.roll(h, 1, 0))
        da = gfull * hprev
        uu = gi * xc
        dsq = gfull * uu
        duu = gfull * sq
        dla = da * a - dsq * e2 / sq
        sp = sp_ref[...]
        dr = dla * (-LRU_C) * sp
        _acc(dsp_ref, _colsum(dla * (-LRU_C) * r), first)
        dzr = dr * r * (1.0 - r)
        dzi = duu * xc * gi * (1.0 - gi)
        dzrb, dzib = _bf(dzr), _bf(dzi)
        dxc = duu * gi + _dot_nt(dzrb, wr_ref[...]) + _dot_nt(dzib, wi_ref[...])
        xcb = _bf(xc)
        _acc(dwr_ref, _dot_tn(xcb, dzrb), first)
        _acc(dwi_ref, _dot_tn(xcb, dzib), first)
        _acc(dbr_ref, _colsum(dzr), first)
        _acc(dbi_ref, _colsum(dzi), first)
        _acc(dcb_ref, _colsum(dxc), first)
        ext[0:hl, :] = xrh_ref[...] * keep
        ext[hl:hl + tb, :] = xr_ref[...]
        dext[0:tb, :] = dxc
        dext[tb:tb + hl, :] = head[...]
        head[...] = dxc[0:hl]
        dxr = jnp.zeros((tb, D_GROUP), F32)
        for k in range(LRU_CONV_WIDTH):
            off = hl - (LRU_CONV_WIDTH - 1) + k
            _acc(dcw_ref.at[k:k + 1, :], _colsum(dxc * ext[off:off + tb, :]), first)
            back = LRU_CONV_WIDTH - 1 - k
            dxr = dxr + cw_ref[k:k + 1, :] * dext[back:back + tb, :]

        @pl.when(first)
        def _():
            dcw_ref[LRU_CONV_WIDTH:hl, :] = jnp.zeros((hl - LRU_CONV_WIDTH, D_GROUP), F32)

        dx_ref[:, 0:D_GROUP] = _bf(dxg)
        dx_ref[:, D_GROUP:2 * D_GROUP] = _bf(dxr)
        _acc(cs_ref.at[:, 0:D_GROUP], _colsum(dxg), first)
        _acc(cs_ref.at[:, D_GROUP:2 * D_GROUP], _colsum(dxr), first)

    vec = pl.BlockSpec((1, D_GROUP), lambda i: (0, 0))
    sq_spec = pl.BlockSpec((D_GROUP, D_GROUP), lambda i: (0, 0))
    tap = pl.BlockSpec((hl, D_GROUP), lambda i: (0, 0))
    vshape = jax.ShapeDtypeStruct((1, D_GROUP), F32)
    sshape = jax.ShapeDtypeStruct((D_GROUP, D_GROUP), F32)

    def blk(col):
        return pl.BlockSpec((tb, D_GROUP), lambda i: (nb - 1 - i, col))

    def halo_blk(col):
        return pl.BlockSpec((hl, D_GROUP), lambda i: (jnp.maximum((nb - 1 - i) * per - 1, 0), col))

    return pl.pallas_call(
        body, name=name, grid=(nb,),
        in_specs=[blk(2), blk(3), blk(4), halo_blk(4), blk(0), blk(0), halo_blk(0),
                  tap, sq_spec, vec, sq_spec, vec, vec],
        out_specs=[pl.BlockSpec((tb, 2 * D_GROUP), lambda i: (nb - 1 - i, 0)),
                   pl.BlockSpec((1, 2 * D_GROUP), lambda i: (0, 0)),
                   tap, vec, sq_spec, vec, sq_spec, vec, vec],
        out_shape=[jax.ShapeDtypeStruct((s, 2 * D_GROUP), BF16), jax.ShapeDtypeStruct((1, 2 * D_GROUP), F32),
                   jax.ShapeDtypeStruct((hl, D_GROUP), F32), vshape, sshape, vshape, sshape, vshape, vshape],
        scratch_shapes=[pltpu.VMEM((hl + tb, D_GROUP), F32), pltpu.VMEM((tb + hl, D_GROUP), F32),
                        pltpu.VMEM((hl, D_GROUP), F32), pltpu.VMEM((8, D_GROUP), F32), pltpu.VMEM((8, D_GROUP), F32)],
        compiler_params=_cparams(1),
    )(dmix, h_in, h_in, h_in, xcs, hs, hs, cw, wr, br, wi, bi, sp)


ATTN_HEADS = 4
ATTN_HEAD_DIM = 64
ATTN_SCALE = ATTN_HEAD_DIM ** -0.5


def _head_mask(h):
    lane = lax.broadcasted_iota(jnp.int32, (1, D_GROUP), 1)
    return jnp.where((lane >= h * ATTN_HEAD_DIM) & (lane < (h + 1) * ATTN_HEAD_DIM), 1.0, 0.0)


def _softmax_rows(sc):
    e = jnp.exp(sc - jnp.max(sc, -1, keepdims=True))
    return e / jnp.sum(e, -1, keepdims=True)


def attn_fwd(h_in, kv, tb, name):
    s = h_in.shape[0]

    def body(q_ref, kv_ref, y_ref):
        q = q_ref[...]
        kb = _bf(kv_ref[:, 0:D_GROUP])
        vb = _bf(kv_ref[:, D_GROUP:2 * D_GROUP])
        out = jnp.zeros((tb, D_GROUP), F32)
        for h in range(ATTN_HEADS):
            mask = _head_mask(h)
            p = _softmax_rows(_dot_nt(_bf(q * mask), kb) * ATTN_SCALE)
            out = out + _dot(_bf(p), vb) * mask
        y_ref[...] = _bf(out)

    return pl.pallas_call(
        body, name=name, grid=(s // tb,),
        in_specs=[pl.BlockSpec((tb, D_GROUP), lambda i: (i, 5)),
                  pl.BlockSpec((D_GROUP, 2 * D_GROUP), lambda i: (0, 0))],
        out_specs=pl.BlockSpec((tb, D_GROUP), lambda i: (i, 0)),
        out_shape=jax.ShapeDtypeStruct((s, D_GROUP), BF16),
        compiler_params=_cparams(1),
    )(h_in, kv)


def attn_bwd(dmix, h_in, kv, tb, name, exchange=()):
    s = h_in.shape[0]
    ne = len(exchange)

    def body(do_ref, q_ref, kv_ref, *rest):
        g_refs = rest[:ne]
        dq_ref, cs_ref, dkv_ref = rest[ne:ne + 3]
        got_refs, sems = rest[ne + 3:2 * ne + 3], rest[2 * ne + 3:]
        first = pl.program_id(0) == 0
        if ne:
            @pl.when(first)
            def _():
                _exchange_start(g_refs, got_refs, sems)

        q = q_ref[...]
        do = do_ref[...]
        kb = _bf(kv_ref[:, 0:D_GROUP])
        vb = _bf(kv_ref[:, D_GROUP:2 * D_GROUP])
        dq = jnp.zeros((tb, D_GROUP), F32)
        dk = jnp.zeros((D_GROUP, D_GROUP), F32)
        dv = jnp.zeros((D_GROUP, D_GROUP), F32)
        for h in range(ATTN_HEADS):
            mask = _head_mask(h)
            qm = _bf(q * mask)
            p = _softmax_rows(_dot_nt(qm, kb) * ATTN_SCALE)
            dom = _bf(do * mask)
            dp = _dot_nt(dom, vb)
            dv = dv + _dot_tn(_bf(p), dom)
            ds = _bf(p * (dp - jnp.sum(dp * p, -1, keepdims=True)) * ATTN_SCALE)
            dq = dq + _dot(ds, kb) * mask
            dk = dk + _dot_tn(ds, qm)
        dq_ref[...] = _bf(dq)
        _acc(cs_ref, _colsum(dq), first)
        _acc(dkv_ref.at[:, 0:D_GROUP], dk, first)
        _acc(dkv_ref.at[:, D_GROUP:2 * D_GROUP], dv, first)
        if ne:
            @pl.when(pl.program_id(0) == s // tb - 1)
            def _():
                _exchange_finish(g_refs, got_refs, sems)

    return pl.pallas_call(
        body, name=name, grid=(s // tb,),
        in_specs=[pl.BlockSpec((tb, D_GROUP), lambda i: (i, 3)),
                  pl.BlockSpec((tb, D_GROUP), lambda i: (i, 5)),
                  pl.BlockSpec((D_GROUP, 2 * D_GROUP), lambda i: (0, 0))] + [ANY] * ne,
        out_specs=[pl.BlockSpec((tb, D_GROUP), lambda i: (i, 0)),
                   pl.BlockSpec((1, D_GROUP), lambda i: (0, 0)),
                   pl.BlockSpec((D_GROUP, 2 * D_GROUP), lambda i: (0, 0))] + [ANY] * ne,
        out_shape=[jax.ShapeDtypeStruct((s, D_GROUP), BF16), jax.ShapeDtypeStruct((1, D_GROUP), F32),
                   jax.ShapeDtypeStruct((D_GROUP, 2 * D_GROUP), F32)] + _exchange_shapes(exchange),
        scratch_shapes=_exchange_sems(ne) if ne else [],
        compiler_params=_cparams(1),
    )(dmix, h_in, kv, *exchange)


FFN_RB = 16
FFN_UNROLL_FWD = 4
FFN_UNROLL_BWD = 2
FFN_TAP_ROWS = 8
FFN_TN = D_FF // 2


def _shift_down(cur, tail, k):
    return pltpu.roll(jnp.concatenate([tail, cur], axis=0), k, 0)[SUBLANES:]


def _shift_up(cur, head, k):
    rb = cur.shape[0]
    return pltpu.roll(jnp.concatenate([cur, head], axis=0), rb + SUBLANES - k, 0)[:rb]


def _fold8(v):
    tot = v[0:SUBLANES]
    for t in range(1, v.shape[0] // SUBLANES):
        tot = tot + v[t * SUBLANES:(t + 1) * SUBLANES]
    return tot


def _strip(r):
    return pl.ds(pl.multiple_of(r * FFN_RB, FFN_RB), FFN_RB)


def ffn_act_fwd(u, cw, cb, tb, name, gather=()):
    s = u.shape[0]
    rb = FFN_RB
    nct = D_FF // FFN_TN
    nstrip = tb // rb
    ng = len(gather)

    def body(uv_ref, ug_ref, wv_ref, wg_ref, bv_ref, bg_ref, *rest):
        hf_ref, keep_ref = rest[ng], rest[ng + 1]
        slabs = rest[ng + 2:2 * ng + 2]
        tailv, tailg = rest[2 * ng + 2], rest[2 * ng + 3]
        sems = rest[2 * ng + 4:]
        if ng:
            @pl.when((pl.program_id(0) == 0) & (pl.program_id(1) == 0))
            def _():
                _gather_start(slabs, sems[0:2])

        @pl.when(pl.program_id(1) == 0)
        def _():
            tailv[...] = jnp.zeros_like(tailv)
            tailg[...] = jnp.zeros_like(tailg)

        for cc in range(FFN_TN // LANES):
            cols = slice(cc * LANES, (cc + 1) * LANES)
            wv = [wv_ref[k:k + 1, cols] for k in range(FFN_CONV_WIDTH)]
            wg = [wg_ref[k:k + 1, cols] for k in range(FFN_CONV_WIDTH)]
            bv, bg = bv_ref[:, cols], bg_ref[:, cols]

            def strip(r, carry):
                tail_v, tail_g = carry
                cur_v, cur_g = uv_ref[_strip(r), cols], ug_ref[_strip(r), cols]
                vc = wv[0] * _shift_down(cur_v, tail_v, 2) + wv[1] * _shift_down(cur_v, tail_v, 1) + wv[2] * cur_v + bv
                gc = wg[0] * _shift_down(cur_g, tail_g, 2) + wg[1] * _shift_down(cur_g, tail_g, 1) + wg[2] * cur_g + bg
                ge, t = _gelu_parts(gc)
                hf_ref[_strip(r), cols] = _bf(vc * ge)
                keep_ref[0, _strip(r), cols] = _bf(vc)
                keep_ref[1, _strip(r), cols] = _bf(ge)
                keep_ref[2, _strip(r), cols] = _bf(_gelu_grad(gc, t))
                return cur_v[rb - SUBLANES:], cur_g[rb - SUBLANES:]

            def strips(q, carry):
                for k in range(FFN_UNROLL_FWD):
                    carry = strip(q * FFN_UNROLL_FWD + k, carry)
                return carry

            last_v, last_g = lax.fori_loop(0, nstrip // FFN_UNROLL_FWD, strips, (tailv[:, cols], tailg[:, cols]))
            tailv[:, cols] = last_v
            tailg[:, cols] = last_g

        if ng:
            @pl.when((pl.program_id(0) == nct - 1) & (pl.program_id(1) == s // tb - 1))
            def _():
                _gather_finish(slabs, sems[0:2], sems[2:4])

    return pl.pallas_call(
        body, name=name, grid=(nct, s // tb),
        in_specs=[pl.BlockSpec((tb, FFN_TN), lambda c, i: (i, c)),
                  pl.BlockSpec((tb, FFN_TN), lambda c, i: (i, c + nct)),
                  pl.BlockSpec((FFN_TAP_ROWS, FFN_TN), lambda c, i: (0, c)),
                  pl.BlockSpec((FFN_TAP_ROWS, FFN_TN), lambda c, i: (0, c + nct)),
                  pl.BlockSpec((1, FFN_TN), lambda c, i: (0, c)),
                  pl.BlockSpec((1, FFN_TN), lambda c, i: (0, c + nct))] + [ANY] * ng,
        out_specs=[pl.BlockSpec((tb, FFN_TN), lambda c, i: (i, c)),
                   pl.BlockSpec((3, tb, FFN_TN), lambda c, i: (0, i, c))] + [ANY] * ng,
        out_shape=[jax.ShapeDtypeStruct((s, D_FF), BF16), jax.ShapeDtypeStruct((3, s, D_FF), BF16)]
        + [jax.ShapeDtypeStruct(a.shape, a.dtype) for a in gather],
        input_output_aliases={6 + k: 2 + k for k in range(ng)},
        scratch_shapes=[pltpu.VMEM((SUBLANES, FFN_TN), F32), pltpu.VMEM((SUBLANES, FFN_TN), F32)]
        + (_gather_sems(ng) if ng else []),
        compiler_params=_cparams(2),
    )(u, u, cw, cw, cb, cb, *gather)


def ffn_act_bwd(dhf, u, kept, cw, tb, name, scatter=()):
    s = u.shape[0]
    rb = FFN_RB
    nct = D_FF // FFN_TN
    nb = s // tb
    nstrip = tb // rb
    ntap = FFN_CONV_WIDTH
    ns = len(scatter)

    def body(dh_ref, uv_ref, ug_ref, kept_ref, wv_ref, wg_ref, *rest):
        s_refs = rest[:ns]
        du_ref, dwv_ref, dwg_ref = rest[ns:ns + 3]
        got_refs = rest[ns + 3:2 * ns + 3]
        headv, headg = rest[2 * ns + 3], rest[2 * ns + 4]
        sems = rest[2 * ns + 5:]
        i = pl.program_id(1)
        first = i == 0
        if ns:
            @pl.when((pl.program_id(0) == 0) & first)
            def _():
                _scatter_start(s_refs, got_refs, sems)

        @pl.when(first)
        def _():
            headv[...] = jnp.zeros_like(headv)
            headg[...] = jnp.zeros_like(headg)
            dwv_ref[...] = jnp.zeros_like(dwv_ref)
            dwg_ref[...] = jnp.zeros_like(dwg_ref)

        zero = jnp.zeros((SUBLANES, LANES), F32)
        for cc in range(FFN_TN // LANES):
            cols = slice(cc * LANES, (cc + 1) * LANES)
            wv = [wv_ref[k:k + 1, cols] for k in range(ntap)]
            wg = [wg_ref[k:k + 1, cols] for k in range(ntap)]

            def strip(ii, carry):
                head_dv, head_dg, acc_v, acc_g = carry
                r = nstrip - 1 - ii
                dh = dh_ref[_strip(r), cols]
                dvc = dh * kept_ref[1, _strip(r), cols].astype(F32)
                dgc = dh * kept_ref[0, _strip(r), cols].astype(F32) * kept_ref[2, _strip(r), cols].astype(F32)
                sdv = [_shift_up(dvc, head_dv, 2), _shift_up(dvc, head_dv, 1), dvc]
                sdg = [_shift_up(dgc, head_dg, 2), _shift_up(dgc, head_dg, 1), dgc]
                cur_v, cur_g = uv_ref[_strip(r), cols], ug_ref[_strip(r), cols]
                acc_v = tuple(acc_v[k] + _fold8(cur_v * sdv[k]) for k in range(ntap)) + (acc_v[ntap] + _fold8(dvc),)
                acc_g = tuple(acc_g[k] + _fold8(cur_g * sdg[k]) for k in range(ntap)) + (acc_g[ntap] + _fold8(dgc),)
                du_v = wv[0] * sdv[0] + wv[1] * sdv[1] + wv[2] * sdv[2]
                du_g = wg[0] * sdg[0] + wg[1] * sdg[1] + wg[2] * sdg[2]
                du_ref[0, _strip(r), cols] = _bf(du_v)
                du_ref[1, _strip(r), cols] = _bf(du_g)
                return dvc[0:SUBLANES], dgc[0:SUBLANES], acc_v, acc_g

            init = (headv[:, cols], headg[:, cols], (zero,) * (ntap + 1), (zero,) * (ntap + 1))
            def strips(q, carry):
                for k in range(FFN_UNROLL_BWD):
                    carry = strip(q * FFN_UNROLL_BWD + k, carry)
                return carry

            top_dv, top_dg, acc_v, acc_g = lax.fori_loop(0, nstrip // FFN_UNROLL_BWD, strips, init)
            headv[:, cols] = top_dv
            headg[:, cols] = top_dg
            for k in range(ntap + 1):
                dwv_ref[k:k + 1, cols] += _colsum(acc_v[k])
                dwg_ref[k:k + 1, cols] += _colsum(acc_g[k])

        if ns:
            @pl.when((pl.program_id(0) == nct - 1) & (i == nb - 1))
            def _():
                _scatter_finish(s_refs, got_refs, sems)

    def blk(shift):
        return pl.BlockSpec((tb, FFN_TN), lambda c, i: (nb - 1 - i, c + shift))

    tapv = pl.BlockSpec((FFN_TAP_ROWS, FFN_TN), lambda c, i: (0, c))
    tapg = pl.BlockSpec((FFN_TAP_ROWS, FFN_TN), lambda c, i: (0, c + nct))
    return pl.pallas_call(
        body, name=name, grid=(nct, nb),
        in_specs=[blk(0), blk(0), blk(nct), pl.BlockSpec((3, tb, FFN_TN), lambda c, i: (0, nb - 1 - i, c)), tapv, tapg]
        + [ANY] * ns,
        out_specs=[pl.BlockSpec((2, tb, FFN_TN), lambda c, i: (0, nb - 1 - i, c)), tapv, tapv] + [ANY] * ns,
        out_shape=[jax.ShapeDtypeStruct((2, s, D_FF), BF16), jax.ShapeDtypeStruct((FFN_TAP_ROWS, D_FF), F32),
                   jax.ShapeDtypeStruct((FFN_TAP_ROWS, D_FF), F32)] + _scatter_shapes(scatter),
        scratch_shapes=[pltpu.VMEM((SUBLANES, FFN_TN), F32), pltpu.VMEM((SUBLANES, FFN_TN), F32)]
        + (_scatter_sems(ns) if ns else []),
        compiler_params=_cparams(2),
    )(dhf, u, u, kept, cw, cw, *scatter)


def _place():
    x, y, c = lax.axis_index("x"), lax.axis_index("y"), lax.axis_index("c")
    return x, y, c, 2 * x + y


def _chip_peer(x, y, d):
    return jnp.bitwise_xor(x, d >> 1), jnp.bitwise_xor(y, d & 1)


def _my_half(ref_rows, c):
    half = ref_rows // 2
    return pl.ds(c * half, half)


def _gather_copy(ref, part, c, sems, k, d, to):
    rows = _my_half(ref.shape[1], c)
    return pltpu.make_async_remote_copy(src_ref=ref.at[part, rows], dst_ref=ref.at[part, rows], send_sem=sems[0].at[k, d - 1],
                                        recv_sem=sems[1].at[k, d - 1], device_id=to, device_id_type=MESH)


def _gather_start(slabs, ici_sems):
    x, y, c, j = _place()
    for k, ref in enumerate(slabs):
        for d in (1, 2, 3):
            px, py = _chip_peer(x, y, d)
            _gather_copy(ref, j, c, ici_sems, k, d, (px, py, c)).start()


def _gather_finish(slabs, ici_sems, d2d_sems):
    x, y, c, j = _place()
    sib = (x, y, 1 - c)
    passed = []
    for d in (1, 2, 3):
        jd = jnp.bitwise_xor(j, d)
        for k, ref in enumerate(slabs):
            _gather_copy(ref, jd, c, ici_sems, k, d, sib).wait_recv()
            cp = _gather_copy(ref, jd, c, d2d_sems, k, d, sib)
            cp.start()
            passed.append(cp)
    for cp in passed:
        cp.wait_recv()
        cp.wait_send()
    for k, ref in enumerate(slabs):
        for d in (1, 2, 3):
            _gather_copy(ref, j, c, ici_sems, k, d, sib).wait_send()


def _gather_sems(n):
    return [pltpu.SemaphoreType.DMA((n, 3)) for _ in range(4)]


def gather_weights(slabs):
    n = len(slabs)

    def body(*refs):
        outs = refs[n:2 * n]
        sems = refs[2 * n:]
        _gather_start(outs, sems[0:2])
        _gather_finish(outs, sems[0:2], sems[2:4])

    return pl.pallas_call(
        body, name="gather_weights", in_specs=[ANY] * n, out_specs=[ANY] * n,
        out_shape=[jax.ShapeDtypeStruct(a.shape, a.dtype) for a in slabs],
        input_output_aliases={w: w for w in range(n)}, scratch_shapes=_gather_sems(n),
    )(*slabs)


def _own_slab(part, jidx):
    slab = lax.empty((N_CHIPS,) + part.shape, part.dtype)
    return lax.dynamic_update_slice_in_dim(slab, part[None], jidx, axis=0)


def _exchange_copy(g_ref, got_ref, k, sems):
    x, y, c, j = _place()
    half = g_ref.shape[1] // 2
    return pltpu.make_async_remote_copy(
        src_ref=g_ref.at[:, pl.ds((1 - c) * half, half)], dst_ref=got_ref, send_sem=sems[0].at[k],
        recv_sem=sems[1].at[k], device_id=(x, y, 1 - c), device_id_type=MESH)


def _exchange_start(g_refs, got_refs, sems):
    for k in range(len(g_refs)):
        _exchange_copy(g_refs[k], got_refs[k], k, sems).start()


def _exchange_finish(g_refs, got_refs, sems):
    for k in range(len(g_refs)):
        _exchange_copy(g_refs[k], got_refs[k], k, sems).wait()


def _exchange_sems(n):
    return [pltpu.SemaphoreType.DMA((n,)), pltpu.SemaphoreType.DMA((n,))]


def _exchange_shapes(gs):
    return [jax.ShapeDtypeStruct((g.shape[0], g.shape[1] // 2, g.shape[2]), F32) for g in gs]


def exchange_cores(gbig, name, small_all=None):
    n = len(gbig)
    with_small = small_all is not None

    def body(*refs):
        g_refs = refs[:n]
        got_refs = refs[n + with_small:2 * n + with_small]
        dsem, esem, ssem, rsem, fsem, hsem = refs[2 * (n + with_small):]
        x, y, c, j = _place()
        sib = (x, y, 1 - c)
        big = []
        for k in range(n):
            half = g_refs[k].shape[1] // 2
            cp = pltpu.make_async_remote_copy(
                src_ref=g_refs[k].at[:, pl.ds((1 - c) * half, half)], dst_ref=got_refs[k], send_sem=dsem.at[k],
                recv_sem=esem.at[k], device_id=sib, device_id_type=MESH)
            cp.start()
            big.append(cp)
        if with_small:
            all_ref = refs[2 * n + 1]
            me = 4 * x + 2 * y + c

            def small_copy(k, block, to, sems):
                return pltpu.make_async_remote_copy(
                    src_ref=all_ref.at[block], dst_ref=all_ref.at[block],
                    send_sem=sems[0].at[k], recv_sem=sems[1].at[k], device_id=to, device_id_type=MESH)

            first = [small_copy(0, me, sib, (ssem, rsem))]
            for d in (1, 2, 3):
                px, py = _chip_peer(x, y, d)
                first.append(small_copy(d, me, (px, py, c), (ssem, rsem)))
            for cp in first:
                cp.start()
            passed = []
            for d in (1, 2, 3):
                px, py = _chip_peer(x, y, d)
                src_block = 4 * px + 2 * py + c
                small_copy(d, src_block, sib, (ssem, rsem)).wait_recv()
                cp = small_copy(d - 1, src_block, sib, (fsem, hsem))
                cp.start()
                passed.append(cp)
            small_copy(0, me, sib, (ssem, rsem)).wait_recv()
            for cp in passed:
                cp.wait_recv()
            for cp in first + passed:
                cp.wait_send()
        for cp in big:
            cp.wait()

    ops = list(gbig) + ([small_all] if with_small else [])
    out_shape = [jax.ShapeDtypeStruct((g.shape[0], g.shape[1] // 2, g.shape[2]), F32) for g in gbig]
    aliases = {}
    if with_small:
        out_shape.append(jax.ShapeDtypeStruct(small_all.shape, F32))
        aliases = {n: n}
    return pl.pallas_call(
        body, name=name, in_specs=[ANY] * len(ops), out_specs=[ANY] * len(out_shape), out_shape=out_shape,
        input_output_aliases=aliases,
        scratch_shapes=[pltpu.SemaphoreType.DMA((n,)), pltpu.SemaphoreType.DMA((n,)),
                        pltpu.SemaphoreType.DMA((4,)), pltpu.SemaphoreType.DMA((4,)),
                        pltpu.SemaphoreType.DMA((3,)), pltpu.SemaphoreType.DMA((3,))],
    )(*ops)


def _scatter_copy(s_ref, got_ref, k, d, sems):
    x, y, c, j = _place()
    px, py = _chip_peer(x, y, d)
    return pltpu.make_async_remote_copy(
        src_ref=s_ref.at[jnp.bitwise_xor(j, d)], dst_ref=got_ref.at[d - 1], send_sem=sems[0].at[k, d - 1],
        recv_sem=sems[1].at[k, d - 1], device_id=(px, py, c), device_id_type=MESH)


def _scatter_start(s_refs, got_refs, sems):
    for d in (1, 2, 3):
        for k in range(len(s_refs)):
            _scatter_copy(s_refs[k], got_refs[k], k, d, sems).start()


def _scatter_finish(s_refs, got_refs, sems):
    for d in (1, 2, 3):
        for k in range(len(s_refs)):
            _scatter_copy(s_refs[k], got_refs[k], k, d, sems).wait()


def _scatter_sems(n):
    return [pltpu.SemaphoreType.DMA((n, 3)), pltpu.SemaphoreType.DMA((n, 3))]


def _scatter_shapes(s1):
    return [jax.ShapeDtypeStruct((3,) + a.shape[1:], a.dtype) for a in s1]


def scatter_shards(s1, name):
    n = len(s1)

    def body(*refs):
        _scatter_start(refs[:n], refs[n:2 * n], refs[2 * n:])
        _scatter_finish(refs[:n], refs[n:2 * n], refs[2 * n:])

    return pl.pallas_call(
        body, name=name, in_specs=[ANY] * n, out_specs=[ANY] * n, out_shape=_scatter_shapes(s1),
        scratch_shapes=_scatter_sems(n),
    )(*s1)


def share_with_sibling(parts):
    n = len(parts)

    def body(*refs):
        out_refs = refs[n:2 * n]
        ssem, rsem = refs[2 * n:]
        x, y, c, j = _place()
        cps = []
        for k in range(n):
            rows = _my_half(out_refs[k].shape[1], c)
            for l in range(DEPTH):
                cp = pltpu.make_async_remote_copy(
                    src_ref=out_refs[k].at[l, rows], dst_ref=out_refs[k].at[l, rows], send_sem=ssem.at[k, l],
                    recv_sem=rsem.at[k, l], device_id=(x, y, 1 - c), device_id_type=MESH)
                cp.start()
                cps.append(cp)
        for cp in cps:
            cp.wait()

    return pl.pallas_call(
        body, name="share_with_sibling", in_specs=[ANY] * n, out_specs=[ANY] * n,
        out_shape=[jax.ShapeDtypeStruct(a.shape, F32) for a in parts],
        input_output_aliases={k: k for k in range(n)},
        scratch_shapes=[pltpu.SemaphoreType.DMA((n, DEPTH)), pltpu.SemaphoreType.DMA((n, DEPTH))],
    )(*parts)


def add_core_halves(g, got, cidx, name):
    nch, r, cdim = g.shape
    half = r // 2
    tr = _row_tile(half, cdim, mult=16)
    per = half // tr

    def body(c_ref, a_ref, b_ref, o_ref):
        o_ref[...] = _bf(a_ref[...] + b_ref[...])

    grid_spec = pltpu.PrefetchScalarGridSpec(
        num_scalar_prefetch=1, grid=(nch, per),
        in_specs=[pl.BlockSpec((None, tr, cdim), lambda jj, i, c_ref: (jj, c_ref[0] * per + i, 0)),
                  pl.BlockSpec((None, tr, cdim), lambda jj, i, c_ref: (jj, i, 0))],
        out_specs=pl.BlockSpec((None, tr, cdim), lambda jj, i, c_ref: (jj, i, 0)))
    return pl.pallas_call(
        body, name=name, grid_spec=grid_spec,
        out_shape=jax.ShapeDtypeStruct((nch, half, cdim), BF16), compiler_params=_cparams(2),
    )(cidx, g, got)


def add_chip_parts(s1, got, jc, layer, into, name):
    _, half, cdim = s1.shape
    tr = _row_tile(half, cdim, mult=16)
    per = half // tr

    def body(jc_ref, a_ref, g0_ref, g1_ref, g2_ref, *rest):
        rest[-1][...] = ((a_ref[...].astype(F32) + g0_ref[...].astype(F32)) + g1_ref[...].astype(F32)) + g2_ref[...].astype(F32)

    def slot(k):
        return pl.BlockSpec((None, tr, cdim), lambda i, jc_ref: (k, i, 0))

    in_specs = [pl.BlockSpec((None, tr, cdim), lambda i, jc_ref: (jc_ref[0], i, 0)), slot(0), slot(1), slot(2)]
    ops = [jc, s1, got, got, got]
    aliases = {}
    if into is not None:
        in_specs.append(ANY)
        ops.append(into)
        aliases = {5: 0}
    grid_spec = pltpu.PrefetchScalarGridSpec(
        num_scalar_prefetch=1, grid=(per,), in_specs=in_specs,
        out_specs=pl.BlockSpec((None, tr, cdim), lambda i, jc_ref: (layer, jc_ref[1] * per + i, 0)))
    return pl.pallas_call(
        body, name=name, grid_spec=grid_spec, input_output_aliases=aliases,
        out_shape=jax.ShapeDtypeStruct((DEPTH, 2 * half, cdim), F32), compiler_params=_cparams(1),
    )(*ops)


def sum_devices(allp):
    _, r, _ = allp.shape

    def body(a_ref, o_ref):
        tot = a_ref[0]
        for k in range(1, 8):
            tot = tot + a_ref[k]
        o_ref[...] = tot

    tr = r // 2 if r % 16 == 0 else r
    return pl.pallas_call(
        body, name="sum_devices", grid=(r // tr,),
        in_specs=[pl.BlockSpec((8, tr, LANES), lambda i: (0, i, 0))],
        out_specs=pl.BlockSpec((tr, LANES), lambda i: (i, 0)),
        out_shape=jax.ShapeDtypeStruct((r, LANES), F32), compiler_params=_cparams(1),
    )(allp)


def _row_tile(r, cdim, limit_bytes=1 << 20, mult=8):
    best = None
    for tr in range(mult, r + 1, mult):
        if r % tr == 0 and tr * cdim * 4 <= limit_bytes:
            best = tr
    return best if best is not None else r


def adamw(w, g, m, v, name):
    r, cdim = w.shape
    tr = _row_tile(r, cdim)
    bc1 = 1.0 - ADAM_B1 ** ADAM_STEP
    bc2 = 1.0 - ADAM_B2 ** ADAM_STEP

    def body(w_ref, g_ref, m_ref, v_ref, d_ref, nm_ref, nv_ref, go_ref):
        gv = g_ref[...]
        nm = ADAM_B1 * m_ref[...] + (1.0 - ADAM_B1) * gv
        nv = ADAM_B2 * v_ref[...] + (1.0 - ADAM_B2) * (gv * gv)
        d_ref[...] = -ADAM_LR * ((nm / bc1) / (jnp.sqrt(nv / bc2) + ADAM_EPS) + ADAM_WD * w_ref[...])
        nm_ref[...] = nm
        nv_ref[...] = nv
        go_ref[...] = gv

    blk = pl.BlockSpec((tr, cdim), lambda i: (i, 0))
    shape = jax.ShapeDtypeStruct((r, cdim), F32)
    return pl.pallas_call(
        body, name=name, grid=(r // tr,), in_specs=[blk] * 4, out_specs=[blk] * 4, out_shape=[shape] * 4,
        compiler_params=_cparams(1),
    )(w, g, m, v)


def _s5_prepare(lam_re, lam_im, log_dt, b_re, b_im, c_re, c_im):
    groups, ch = 16, 16
    dt = jnp.exp(log_dt)[:, None]
    mag = jnp.exp(lam_re * dt)
    a_r, a_i = mag * jnp.cos(lam_im * dt), mag * jnp.sin(lam_im * dt)
    den = lam_re * lam_re + lam_im * lam_im
    q_r = ((a_r - 1.0) * lam_re + a_i * lam_im) / den
    q_i = (a_i * lam_re - (a_r - 1.0) * lam_im) / den
    bb_r = q_r[..., None] * b_re - q_i[..., None] * b_im
    bb_i = q_r[..., None] * b_im + q_i[..., None] * b_re
    eye = jnp.eye(groups, dtype=F32)

    def expand_b(bb):
        return jnp.einsum("gpc,gh->gchp", bb, eye).reshape(groups * ch, N_STATE)

    def expand_c(cc):
        return jnp.einsum("gcp,gh->hpgc", cc, eye).reshape(N_STATE, groups * ch)

    a2 = jnp.concatenate([a_r.reshape(1, N_STATE), a_i.reshape(1, N_STATE)], axis=1)
    bexp = jnp.concatenate([expand_b(bb_r), expand_b(bb_i)], axis=1)
    cexp = jnp.concatenate([expand_c(c_re), -expand_c(c_im)], axis=0)
    return a2, bexp, cexp


def _lru_prepare(w_r, w_i, lam):
    heads = 4
    eye = jnp.eye(heads, dtype=F32)

    def expand(w):
        return jnp.einsum("hij,hk->hikj", w, eye).reshape(D_GROUP, D_GROUP)

    return expand(w_r), expand(w_i), jax.nn.softplus(-lam).reshape(1, D_GROUP)


def _pad_rows(a, rows):
    return jnp.pad(a, ((0, rows - a.shape[0]), (0, 0)))


def _group_mean_matrix():
    gidx = jnp.arange(D_GROUP) // 64
    return (gidx[:, None] == gidx[None, :]).astype(BF16) * jnp.asarray(1.0 / 64.0, BF16)


def _pack_rows(arrs, width):
    parts = []
    for a in arrs:
        flat = a.reshape(-1)
        pad = (-flat.shape[0]) % width
        parts.append(jnp.pad(flat, (0, pad)) if pad else flat)
    flat = jnp.concatenate(parts)
    rows = flat.shape[0] // width
    pad_rows = (-rows) % 16
    if pad_rows:
        flat = jnp.pad(flat, (0, pad_rows * width))
    return flat.reshape(-1, width)


def _unpack_rows(packed, shapes, width):
    flat = packed.reshape(-1)
    out, off = [], 0
    for shp in shapes:
        size = math.prod(shp)
        out.append(flat[off:off + size].reshape(shp))
        off += size + ((-size) % width)
    return out


def kernel(x, mem, ln_in_g, ln_in_b, w_in, b_in, s5_lam_re, s5_lam_im, s5_log_dt, s5_b_re, s5_b_im, s5_c_re, s5_c_im, s5_d, s5_w_glu, s5_b_glu, cv_w, cv_b, cv_gn_g, cv_gn_b, cv_w_pw, cv_b_pw, lru_conv_w, lru_conv_b, lru_w_r, lru_b_r, lru_w_i, lru_b_i, lru_lam, attn_w_kv, w_out, b_out, ln1_g, ln1_b, ffn_w_up, ffn_conv_w, ffn_conv_b, ffn_w_down, ln2_g, ln2_b, loss_target, m_ln_in_g, m_ln_in_b, m_w_in, m_b_in, m_s5_lam_re, m_s5_lam_im, m_s5_log_dt, m_s5_b_re, m_s5_b_im, m_s5_c_re, m_s5_c_im, m_s5_d, m_s5_w_glu, m_s5_b_glu, m_cv_w, m_cv_b, m_cv_gn_g, m_cv_gn_b, m_cv_w_pw, m_cv_b_pw, m_lru_conv_w, m_lru_conv_b, m_lru_w_r, m_lru_b_r, m_lru_w_i, m_lru_b_i, m_lru_lam, m_attn_w_kv, m_w_out, m_b_out, m_ln1_g, m_ln1_b, m_ffn_w_up, m_ffn_conv_w, m_ffn_conv_b, m_ffn_w_down, m_ln2_g, m_ln2_b, v_ln_in_g, v_ln_in_b, v_w_in, v_b_in, v_s5_lam_re, v_s5_lam_im, v_s5_log_dt, v_s5_b_re, v_s5_b_im, v_s5_c_re, v_s5_c_im, v_s5_d, v_s5_w_glu, v_s5_b_glu, v_cv_w, v_cv_b, v_cv_gn_g, v_cv_gn_b, v_cv_w_pw, v_cv_b_pw, v_lru_conv_w, v_lru_conv_b, v_lru_w_r, v_lru_b_r, v_lru_w_i, v_lru_b_i, v_lru_lam, v_attn_w_kv, v_w_out, v_b_out, v_ln1_g, v_ln1_b, v_ffn_w_up, v_ffn_conv_w, v_ffn_conv_b, v_ffn_w_down, v_ln2_g, v_ln2_b):
    p = dict(locals())
    xs = x[0]
    mems = mem[0]
    target = loss_target[0]
    s = xs.shape[0]
    cidx = lax.axis_index("c")
    jidx = 2 * lax.axis_index("x") + lax.axis_index("y")
    tb_scan = min(512, s)
    tb_s5 = min(512, s)
    tb_attn = min(512, s)
    tb_ffn = min(512, s)

    small_sh_names = list(SMALL_SHARDED)
    small_sh_shapes = [p[nm].shape[1:] for nm in small_sh_names]
    slabs = [[_own_slab(_bf(p[nm][l]), jidx) for nm in BIG]
             + [_own_slab(_pack_rows([p[nm][l] for nm in small_sh_names], LANES), jidx)] for l in range(DEPTH)]
    n_slabs = len(BIG) + 1
    first_needed = FIRST_NEEDED + [len(BIG)]
    arrive_later = [k for k in range(n_slabs) if k not in first_needed]
    gathered = [[None] * n_slabs for _ in range(DEPTH)]
    for k, slab in zip(first_needed, gather_weights([slabs[0][k] for k in first_needed])):
        gathered[0][k] = slab

    def weight_views(gw, which):
        shapes = {0: (1, N_CHIPS, D_MODEL, N_IN // N_CHIPS), 1: (1, 1, D_MODEL, 2 * D_GROUP), 2: (1, 1, D_MODEL, D_MODEL),
                  3: (1, N_CHIPS, D_MODEL, 2 * D_FF // N_CHIPS), 4: (1, 1, D_FF, D_MODEL),
                  5: (1, D_GROUP, D_GROUP), 6: (1, D_GROUP, D_GROUP)}
        views = {BIG[k]: gw[k].reshape(shapes[k]) for k in which if k < len(BIG)}
        if len(BIG) in which:
            per_chip = [_unpack_rows(gw[len(BIG)][jj], small_sh_shapes, LANES) for jj in range(N_CHIPS)]
            for k, nm in enumerate(small_sh_names):
                views[nm] = jnp.concatenate([per_chip[jj][k] for jj in range(N_CHIPS)], axis=SMALL_SHARDED[nm] - 1)
        return views

    pmat = _group_mean_matrix()

    def vec(a):
        return a.reshape(1, -1)

    xh0, rs0, xb0 = ln_fwd(xs, vec(ln_in_g), vec(ln_in_b), "ln_in")
    saved = []
    prev = dict(xh=xh0, rs=rs0, xb=xb0, g=vec(ln_in_g), b=vec(ln_in_b))
    for l in range(DEPTH):
        sv = dict(prev=prev)
        (a2, bexp, cexp), sv['s5_vjp'] = jax.vjp(_s5_prepare, s5_lam_re[l], s5_lam_im[l], s5_log_dt[l],
                                                 s5_b_re[l], s5_b_im[l], s5_c_re[l], s5_c_im[l])
        (wr, wi, sp), sv['lru_vjp'] = jax.vjp(_lru_prepare, lru_w_r[l], lru_w_i[l], lru_lam[l])
        sv.update(a2=a2, bexp=_bf(bexp), cexp=_bf(cexp), wr=_bf(wr), wi=_bf(wi), sp=sp)
        late = arrive_later if l == 0 else []
        gw = sv['gw'] = weight_views(gathered[l], [k for k in range(n_slabs) if k not in late])
        sv['cvw'] = _pad_rows(gw['cv_w'], CV_HALO)
        sv['lcw'] = _pad_rows(gw['lru_conv_w'], LRU_HALO)
        sv['fcw'] = _pad_rows(gw['ffn_conv_w'], FFN_TAP_ROWS)
        sv['w_glu'], sv['w_pw'] = gw['s5_w_glu'], gw['cv_w_pw']
        h_in = mm_nn(prev['xb'], gw['w_in'], 0, vec(b_in[l]), F32, 2048, f"in_proj{l}")
        kv = mm_nn(mems, gw['attn_w_kv'], 0, jnp.zeros((1, 2 * D_GROUP), F32), F32, 256, f"kv_proj{l}")
        y_s5, hst, y0, *got = s5_fwd(h_in, a2, sv['bexp'], sv['cexp'], vec(s5_d[l]), sv['w_glu'], 0, vec(s5_b_glu[l]),
                                     tb_s5, f"s5_fwd{l}", gather=[slabs[l][k] for k in late])
        for k, slab in zip(late, got):
            gathered[l][k] = slab
        gw.update(weight_views(gathered[l], late))
        y_cv, hc = cv_fwd(h_in, sv['cvw'], vec(cv_b[l]), vec(cv_gn_g[l]), vec(cv_gn_b[l]), pmat, sv['w_pw'], 0,
                          vec(cv_b_pw[l]), tb_scan, f"cv_fwd{l}")
        y_lru, xcs, hls = lru_fwd(h_in, sv['lcw'], vec(lru_conv_b[l]), sv['wr'], vec(lru_b_r[l]), sv['wi'],
                                  vec(lru_b_i[l]), sp, tb_scan, f"lru_fwd{l}")
        y_mem = attn_fwd(h_in, kv, tb_attn, f"attn_fwd{l}")
        mix_in = jnp.concatenate([y_s5, y_cv, y_lru, y_mem], axis=1)
        xh1, rs1, xb1 = proj_ln(mix_in, gw['w_out'].reshape(1, D_MODEL, D_MODEL), 0, vec(b_out[l]),
                                prev['xh'], prev['g'], prev['b'], vec(ln1_g[l]), vec(ln1_b[l]), f"out_proj_ln{l}")
        u = mm_nn(xb1, gw['ffn_w_up'], 0, jnp.zeros((1, 2 * D_FF), F32), F32, 1024, f"ffn_up{l}")
        nxt = slabs[l + 1] if l + 1 < DEPTH else ()
        hf, sv['ffn_kept'], *got = ffn_act_fwd(u, sv['fcw'], vec(ffn_conv_b[l]), tb_ffn, f"ffn_act{l}", gather=nxt)
        if nxt:
            gathered[l + 1] = got
        xh2, rs2, xb2 = proj_ln(hf, gw['ffn_w_down'].reshape(1, D_FF, D_MODEL), 0, jnp.zeros((1, D_MODEL), F32),
                                xh1, vec(ln1_g[l]), vec(ln1_b[l]), vec(ln2_g[l]), vec(ln2_b[l]), f"ffn_down_ln{l}")
        sv.update(h_in=h_in, kv=kv, hst=hst, y0=y0, hc=hc, xcs=xcs, hls=hls, mix_in=mix_in,
                  xh1=xh1, rs1=rs1, xb1=xb1, u=u, hf=hf, xh2=xh2, rs2=rs2)
        saved.append(sv)
        prev = dict(xh=xh2, rs=rs2, xb=xb2, g=vec(ln2_g[l]), b=vec(ln2_b[l]))

    grads = {}
    per_layer = {nm: [None] * DEPTH for nm in WEIGHTS if nm not in ('ln_in_g', 'ln_in_b')}
    c1 = cidx.reshape(1).astype(jnp.int32)
    jc = jnp.stack([jidx, cidx]).astype(jnp.int32)
    red_big = [None] * len(BIG)
    handoff = None
    pending = None
    below = None

    def per_chip(gl, which):
        return [gl[k].reshape((N_CHIPS,) + p[BIG[k]].shape[1:]) for k in which]

    def core_sums(gs, got, which, tag):
        return [add_core_halves(g, ga, c1, f"add_cores_{BIG[k]}{tag}") for g, ga, k in zip(gs, got, which)]

    def chip_parts(gl, which, tag, small_all=None):
        gs = per_chip(gl, which)
        got = exchange_cores(gs, f"exchange_cores{tag}", small_all)
        return core_sums(gs, got, which, tag), (got[len(which)] if small_all is not None else None)

    def own_sum(which, parts, got, l):
        for k, part, gk in zip(which, parts, got):
            red_big[k] = add_chip_parts(part, gk, jc, l, red_big[k], f"add_chips_{BIG[k]}{l}")

    everything = list(range(len(BIG)))
    ready_early = [2, 3, 4]
    ready_last = [k for k in everything if k not in ready_early]

    for l in reversed(range(DEPTH)):
        sv = saved[l]
        pv = sv['prev']
        gw = sv['gw']
        gl = [None] * len(BIG)
        if l == DEPTH - 1:
            dr2, dg2, db2, sqerr = loss_ln_bwd(target, sv['xh2'], sv['rs2'], vec(ln2_g[l]), vec(ln2_b[l]), "loss_ln2_bwd")
            loss_local = 0.5 / D_MODEL * jnp.sum(sqerr)
        else:
            dr2, dg2, db2 = below
        per_layer['ln2_g'][l], per_layer['ln2_b'][l] = dg2[0], db2[0]
        tm_nt = min(512, s)
        ts_big = min(2048, s)
        whole = lambda a_ref, j: a_ref[...]
        dhf = mm_nt(dr2, (tm_nt, D_MODEL), lambda i: (i, 0), whole, gw['ffn_w_down'], 0, None, F32, 512, s, f"ffn_down_dx{l}")
        above = per_chip(*handoff) if handoff is not None else []
        res = mm_tn(sv['hf'], dr2, (min(1024, s), D_MODEL), lambda kt, j, st: (st, 0), 1, D_MODEL, FFN_TN, 1024, s,
                    f"ffn_down_dw{l}", exchange=above)
        gl[4] = res[0] if above else res
        if handoff is not None:
            pending = (l + 1, handoff[1], core_sums(above, res[1:], handoff[1], str(l + 1)))
            handoff = None
        waiting = pending[2] if pending is not None else ()
        du, dcwv, dcwg, *got = ffn_act_bwd(dhf, sv['u'], sv['ffn_kept'], sv['fcw'], tb_ffn, f"ffn_act_bwd{l}",
                                           scatter=waiting)
        if pending is not None:
            own_sum(pending[1], pending[2], got, pending[0])
            pending = None
        dcw = jnp.concatenate([dcwv, dcwg], axis=1)
        per_layer['ffn_conv_w'][l] = dcw[0:FFN_CONV_WIDTH]
        per_layer['ffn_conv_b'][l] = dcw[FFN_CONV_WIDTH]
        dr1, dg1, db1, cs1 = mm_nt(du, (2, tm_nt, D_FF), lambda i: (0, i, 0),
                                   lambda a_ref, j: a_ref[j // 2, :, (j % 2) * FFN_TN:(j % 2 + 1) * FFN_TN], gw['ffn_w_up'], 0, dr2,
                                   F32, 512, s, f"ffn_up_dx_ln1_bwd{l}", ln=(sv['xh1'], sv['rs1'], vec(ln1_g[l])))
        gl[3] = mm_tn(sv['xb1'], du, (None, ts_big, FFN_TN), lambda kt, j, st: (j // 2, st, j % 2), N_CHIPS, FFN_TN,
                      D_MODEL, 2048, s, f"ffn_up_dw{l}")
        per_layer['ln1_g'][l], per_layer['ln1_b'][l], per_layer['b_out'][l] = dg1[0], db1[0], cs1[0]
        dmix = mm_nt(dr1, (tm_nt, D_MODEL), lambda i: (i, 0), whole, gw['w_out'], 0, None, F32, 512, s, f"out_proj_dx{l}")
        gl[2] = mm_tn(sv['mix_in'], dr1, (min(1024, s), D_MODEL), lambda kt, j, st: (st, 0), 1, D_MODEL, D_MODEL, 1024, s,
                      f"out_proj_dw{l}")
        h_in = sv['h_in']
        early_gs = per_chip(gl, ready_early) if l == 0 else []
        d_q, cs_q, d_kv, *got = attn_bwd(dmix, h_in, sv['kv'], tb_attn, f"attn_bwd{l}", exchange=early_gs)
        early = core_sums(early_gs, got, ready_early, f"{l}a") if l == 0 else ()
        (d_u, cs_u, d_bexp, d_cexp, d_dd, d_wglu, d_bglu, d_a2) = s5_bwd(
            dmix, h_in, sv['y0'], sv['hst'], sv['a2'], sv['bexp'], sv['cexp'], vec(s5_d[l]), sv['w_glu'], 0,
            vec(s5_b_glu[l]), tb_s5, f"s5_bwd{l}")
        (d_vg, cs_vg, d_cvw, d_cvb, d_gg, d_gb, d_wpw, d_bpw, *got) = cv_bwd(
            dmix, h_in, sv['hc'], sv['cvw'], vec(cv_gn_g[l]), vec(cv_gn_b[l]), pmat, sv['w_pw'], 0, tb_scan, f"cv_bwd{l}",
            scatter=early)
        if l == 0:
            own_sum(ready_early, early, got, l)
        (d_lx, cs_lx, d_lcw, d_lcb, d_wr, d_br, d_wi, d_bi, d_sp) = lru_bwd(
            dmix, h_in, sv['xcs'], sv['hls'], sv['lcw'], sv['wr'], vec(lru_b_r[l]), sv['wi'], vec(lru_b_i[l]),
            sv['sp'], tb_scan, f"lru_bwd{l}")
        g_s5 = sv['s5_vjp']((d_a2, d_bexp, d_cexp))
        for nm, gval in zip(['s5_lam_re', 's5_lam_im', 's5_log_dt', 's5_b_re', 's5_b_im', 's5_c_re', 's5_c_im'], g_s5):
            per_layer[nm][l] = gval
        g_lru = sv['lru_vjp']((d_wr, d_wi, d_sp))
        for nm, gval in zip(['lru_w_r', 'lru_w_i', 'lru_lam'], g_lru):
            per_layer[nm][l] = gval
        per_layer['s5_d'][l], per_layer['s5_b_glu'][l] = d_dd[0], d_bglu[0]
        per_layer['cv_w'][l], per_layer['cv_b'][l] = d_cvw[0:CONV_WIDTH], d_cvb[0]
        per_layer['cv_gn_g'][l], per_layer['cv_gn_b'][l] = d_gg[0], d_gb[0]
        per_layer['cv_b_pw'][l] = d_bpw[0]
        gl[5], gl[6] = d_wglu, d_wpw
        per_layer['lru_conv_w'][l], per_layer['lru_conv_b'][l] = d_lcw[0:LRU_CONV_WIDTH], d_lcb[0]
        per_layer['lru_b_r'][l], per_layer['lru_b_i'][l] = d_br[0], d_bi[0]
        per_layer['b_in'][l] = jnp.concatenate([cs_u, cs_vg, cs_lx, cs_q], axis=1)[0]
        gl[1] = mm_tn(mems, d_kv, (MEM_ROWS, 2 * D_GROUP), lambda kt, j, st: (st, 0), 1, 2 * D_GROUP, D_MODEL, MEM_ROWS,
                      MEM_ROWS, f"kv_proj_dw{l}")
        dh_in = jnp.concatenate([d_u, d_vg, d_lx, d_q], axis=1)
        n_sh = N_IN // N_CHIPS
        below = mm_nt(dh_in, (tm_nt, N_IN), lambda i: (i, 0), lambda a_ref, j: a_ref[:, j * n_sh:(j + 1) * n_sh],
                      gw['w_in'], 0, dr1, F32, 512, s, f"in_proj_dx_ln_bwd{l}", ln=(pv['xh'], pv['rs'], pv['g']))[:3]
        gl[0] = mm_tn(pv['xb'], dh_in, (ts_big, n_sh), lambda kt, j, st: (st, j), N_CHIPS, n_sh, D_MODEL, 2048, s,
                      f"in_proj_dw{l}")
        if l > 0:
            handoff = (gl, everything)
    grad_x, dg_in, db_in = below
    grads['ln_in_g'], grads['ln_in_b'] = dg_in[0], db_in[0]
    for nm, vals in per_layer.items():
        if nm not in BIG:
            grads[nm] = jnp.stack(vals)

    small_names = [nm for nm in WEIGHTS if nm not in BIG]
    small_local = _pack_rows([grads[nm] for nm in small_names], LANES)
    me = 4 * lax.axis_index("x") + 2 * lax.axis_index("y") + cidx
    small_slab = lax.dynamic_update_slice_in_dim(lax.empty((8,) + small_local.shape, F32), small_local[None], me, axis=0)
    parts, small_all = chip_parts(gl, ready_last, "0b", small_slab)
    own_sum(ready_last, parts, scatter_shards(parts, "scatter_shards0"), 0)
    red_big = share_with_sibling(red_big)
    small_red = sum_devices(small_all)
    small_grads = dict(zip(small_names, _unpack_rows(small_red, [grads[nm].shape for nm in small_names], LANES)))

    out_g, out_d, out_m, out_v = {}, {}, {}, {}
    for k, nm in enumerate(BIG):
        gk = red_big[k]
        two_d = (-1, p[nm].shape[-1])
        res = adamw(p[nm].reshape(two_d), gk.reshape(two_d), p['m_' + nm].reshape(two_d),
                    p['v_' + nm].reshape(two_d), f"adamw_{nm}")
        out_d[nm], out_m[nm], out_v[nm], out_g[nm] = (t.reshape(p[nm].shape) for t in res)
    own = {}
    for nm in small_names:
        gfull = small_grads[nm]
        if nm in SMALL_SHARDED:
            ax = SMALL_SHARDED[nm]
            width = p[nm].shape[ax]
            gfull = lax.dynamic_slice_in_dim(gfull, jidx * width, width, axis=ax)
        own[nm] = gfull
    packs = [_pack_rows([src[nm] for nm in small_names], LANES)
             for src in (dict((nm, p[nm]) for nm in small_names), own,
                         dict((nm, p['m_' + nm]) for nm in small_names), dict((nm, p['v_' + nm]) for nm in small_names))]
    dlt, nm_, nv_, _ = adamw(packs[0], packs[1], packs[2], packs[3], "adamw_small")
    shapes = [p[nm].shape for nm in small_names]
    for dst, packed in ((out_d, dlt), (out_m, nm_), (out_v, nv_)):
        dst.update(zip(small_names, _unpack_rows(packed, shapes, LANES)))
    out_g.update(own)

    loss = lax.psum(loss_local, ("x", "y", "c"))
    return (loss, grad_x[None], *[out_g[nm] for nm in WEIGHTS], *[out_d[nm] for nm in WEIGHTS],
            *[out_m[nm] for nm in WEIGHTS], *[out_v[nm] for nm in WEIGHTS])
```

```python
import functools
import math

import jax
import jax.numpy as jnp
from jax import lax
from jax.experimental import pallas as pl
from jax.experimental.pallas import tpu as pltpu

F32 = jnp.float32
BF16 = jnp.bfloat16
MESH = pl.DeviceIdType.MESH
ANY = pl.BlockSpec(memory_space=pl.ANY)

DEPTH = 2
D_MODEL = 1024
D_GROUP = 256
N_IN = 6 * D_GROUP
D_FF = 2816
N_STATE = 1024
CONV_WIDTH = 31
LRU_CONV_WIDTH = 4
FFN_CONV_WIDTH = 3
LRU_C = 8.0
ALPHA = (2 * DEPTH) ** 0.25
LN_EPS = 1e-5
N_CHIPS = 4
MEM_ROWS = 256
LANES = 128
SUBLANES = 8
VMEM_LIMIT = 56 * 1024 * 1024

ADAM_LR, ADAM_B1, ADAM_B2, ADAM_EPS, ADAM_WD, ADAM_STEP = 0.001, 0.9, 0.999, 1e-08, 0.01, 10

WEIGHTS = ['ln_in_g', 'ln_in_b', 'w_in', 'b_in', 's5_lam_re', 's5_lam_im', 's5_log_dt', 's5_b_re', 's5_b_im',
           's5_c_re', 's5_c_im', 's5_d', 's5_w_glu', 's5_b_glu', 'cv_w', 'cv_b', 'cv_gn_g', 'cv_gn_b', 'cv_w_pw',
           'cv_b_pw', 'lru_conv_w', 'lru_conv_b', 'lru_w_r', 'lru_b_r', 'lru_w_i', 'lru_b_i', 'lru_lam',
           'attn_w_kv', 'w_out', 'b_out', 'ln1_g', 'ln1_b', 'ffn_w_up', 'ffn_conv_w', 'ffn_conv_b', 'ffn_w_down',
           'ln2_g', 'ln2_b']
BIG = ['w_in', 'attn_w_kv', 'w_out', 'ffn_w_up', 'ffn_w_down', 's5_w_glu', 'cv_w_pw']
FIRST_NEEDED = [0, 1, 5, 6]
SMALL_SHARDED = {'cv_w': 2, 'lru_conv_w': 2, 'ffn_conv_w': 2}


def _cparams(n_axes):
    return pltpu.CompilerParams(dimension_semantics=("arbitrary",) * n_axes, vmem_limit_bytes=VMEM_LIMIT)


def _dot(a, b):
    return jnp.dot(a, b, preferred_element_type=F32)


def _dot_nt(a, b):
    return lax.dot_general(a, b, (((1,), (1,)), ((), ())), preferred_element_type=F32)


def _dot_tn(a, b):
    return lax.dot_general(a, b, (((0,), (0,)), ((), ())), preferred_element_type=F32)


def _bf(v):
    return v.astype(BF16)


def _colsum(v):
    return jnp.sum(v, axis=0, keepdims=True)


def _dot3(v, p):
    hi = _bf(v)
    r1 = v - hi.astype(F32)
    mid = _bf(r1)
    lo = _bf(r1 - mid.astype(F32))
    return _dot(hi, p) + _dot(mid, p) + _dot(lo, p)


_GELU_C = math.sqrt(2.0 / math.pi)


_GELU_C3 = _GELU_C * 0.044715


def _gelu_parts(v):
    t = jnp.tanh(v * (_GELU_C + _GELU_C3 * (v * v)))
    hv = 0.5 * v
    return hv + hv * t, t


def _gelu(v):
    return _gelu_parts(v)[0]


def _gelu_grad(v, t):
    return (0.5 + 0.5 * t) + (0.5 * v) * (1.0 - t * t) * (_GELU_C + (3.0 * _GELU_C3) * (v * v))


def _sigmoid(v):
    return 1.0 / (1.0 + jnp.exp(-v))


def _acc(ref, val, first):
    @pl.when(first)
    def _():
        ref[...] = val

    @pl.when(jnp.logical_not(first))
    def _():
        ref[...] += val


def _rows(shape):
    return lax.broadcasted_iota(jnp.int32, shape, 0)


def _ln_rows(r):
    mu = jnp.mean(r, -1, keepdims=True)
    rc = r - mu
    var = jnp.mean(rc * rc, -1, keepdims=True)
    rs = lax.rsqrt(var + LN_EPS)
    return rc * rs, rs


def ln_fwd(x, g, b, name):
    s = x.shape[0]
    tm = min(512, s)

    def body(x_ref, g_ref, b_ref, xh_ref, rs_ref, xb_ref):
        xh, rs = _ln_rows(x_ref[...])
        xh_ref[...] = xh
        rs_ref[...] = rs
        xb_ref[...] = _bf(xh * g_ref[...] + b_ref[...])

    row = pl.BlockSpec((tm, D_MODEL), lambda i: (i, 0))
    vec = pl.BlockSpec((1, D_MODEL), lambda i: (0, 0))
    return pl.pallas_call(
        body, name=name, grid=(s // tm,),
        in_specs=[row, vec, vec],
        out_specs=[row, pl.BlockSpec((tm, 1), lambda i: (i, 0)), row],
        out_shape=[jax.ShapeDtypeStruct((s, D_MODEL), F32), jax.ShapeDtypeStruct((s, 1), F32),
                   jax.ShapeDtypeStruct((s, D_MODEL), BF16)],
        compiler_params=_cparams(1),
    )(x, g, b)


def proj_ln(a, w, layer, bias, xh_prev, g_prev, b_prev, g, b, name):
    s, k = a.shape
    tm = min(512, s)

    def body(a_ref, w_ref, bias_ref, xp_ref, gp_ref, bp_ref, g_ref, b_ref, xh_ref, rs_ref, xb_ref):
        acc = _dot(a_ref[...], w_ref[...]) + bias_ref[...]
        r = ALPHA * (xp_ref[...] * gp_ref[...] + bp_ref[...]) + acc
        xh, rs = _ln_rows(r)
        xh_ref[...] = xh
        rs_ref[...] = rs
        xb_ref[...] = _bf(xh * g_ref[...] + b_ref[...])

    row = pl.BlockSpec((tm, D_MODEL), lambda i: (i, 0))
    vec = pl.BlockSpec((1, D_MODEL), lambda i: (0, 0))
    return pl.pallas_call(
        body, name=name, grid=(s // tm,),
        in_specs=[pl.BlockSpec((tm, k), lambda i: (i, 0)),
                  pl.BlockSpec((None, k, D_MODEL), lambda i: (layer, 0, 0)),
                  vec, row, vec, vec, vec, vec],
        out_specs=[row, pl.BlockSpec((tm, 1), lambda i: (i, 0)), row],
        out_shape=[jax.ShapeDtypeStruct((s, D_MODEL), F32), jax.ShapeDtypeStruct((s, 1), F32),
                   jax.ShapeDtypeStruct((s, D_MODEL), BF16)],
        compiler_params=_cparams(1),
    )(a, w, bias, xh_prev, g_prev, b_prev, g, b)


def loss_ln_bwd(target, xh, rs, g, b, name):
    s = xh.shape[0]
    tm = min(512, s)

    def body(t_ref, xh_ref, rs_ref, g_ref, b_ref, dr_ref, dg_ref, db_ref, sq_ref):
        first = pl.program_id(0) == 0
        xhv = xh_ref[...]
        err = xhv * g_ref[...] + b_ref[...] - t_ref[...]
        dyv = err * (1.0 / D_MODEL)
        dxh = dyv * g_ref[...]
        dr = rs_ref[...] * (dxh - jnp.mean(dxh, -1, keepdims=True) - xhv * jnp.mean(dxh * xhv, -1, keepdims=True))
        dr_ref[...] = dr
        _acc(dg_ref, _colsum(dyv * xhv), first)
        _acc(db_ref, _colsum(dyv), first)
        _acc(sq_ref, _colsum(err * err), first)

    row = pl.BlockSpec((tm, D_MODEL), lambda i: (i, 0))
    vec = pl.BlockSpec((1, D_MODEL), lambda i: (0, 0))
    vshape = jax.ShapeDtypeStruct((1, D_MODEL), F32)
    return pl.pallas_call(
        body, name=name, grid=(s // tm,),
        in_specs=[row, row, pl.BlockSpec((tm, 1), lambda i: (i, 0)), vec, vec],
        out_specs=[row, vec, vec, vec],
        out_shape=[jax.ShapeDtypeStruct((s, D_MODEL), F32), vshape, vshape, vshape],
        compiler_params=_cparams(1),
    )(target, xh, rs, g, b)


def mm_nn(a, w, layer, bias, out_dtype, tm, name):
    m, k = a.shape
    _, nj, _, n = w.shape
    tm = min(tm, m)

    def body(a_ref, w_ref, b_ref, o_ref):
        o_ref[...] = (_dot(_bf(a_ref[...]), w_ref[...]) + b_ref[...]).astype(out_dtype)

    return pl.pallas_call(
        body, name=name, grid=(nj, m // tm),
        in_specs=[pl.BlockSpec((tm, k), lambda j, i: (i, 0)),
                  pl.BlockSpec((None, None, k, n), lambda j, i: (layer, j, 0, 0)),
                  pl.BlockSpec((1, n), lambda j, i: (0, j))],
        out_specs=pl.BlockSpec((tm, n), lambda j, i: (i, j)),
        out_shape=jax.ShapeDtypeStruct((m, nj * n), out_dtype),
        compiler_params=_cparams(2),
    )(a, w, bias)


def mm_nt(a, a_block, a_map, pick, w, layer, add, out_dtype, tm, m, name, ln=None):
    _, nj, r, n = w.shape
    tm = min(tm, m)
    has_add = add is not None
    n_in = 2 + has_add + (3 if ln is not None else 0)

    def body(*refs):
        a_ref, w_ref = refs[0], refs[1]
        res = _dot_nt(_bf(pick(a_ref, 0)), w_ref[0])
        for j in range(1, nj):
            res = res + _dot_nt(_bf(pick(a_ref, j)), w_ref[j])
        if has_add:
            res = res + ALPHA * refs[2][...]
        if ln is None:
            refs[n_in][...] = res.astype(out_dtype)
            return
        xh_ref, rs_ref, g_ref = refs[n_in - 3:n_in]
        dr_ref, dg_ref, db_ref, cs_ref = refs[n_in:]
        first = pl.program_id(0) == 0
        xhv = xh_ref[...]
        dxh = res * g_ref[...]
        dr = rs_ref[...] * (dxh - jnp.mean(dxh, -1, keepdims=True) - xhv * jnp.mean(dxh * xhv, -1, keepdims=True))
        dr_ref[...] = dr
        _acc(dg_ref, _colsum(res * xhv), first)
        _acc(db_ref, _colsum(res), first)
        _acc(cs_ref, _colsum(dr), first)

    row = pl.BlockSpec((tm, r), lambda i: (i, 0))
    in_specs = [pl.BlockSpec(a_block, a_map),
                pl.BlockSpec((None, nj, r, n), lambda i: (layer, 0, 0, 0))]
    ops = [a, w]
    if has_add:
        in_specs.append(row)
        ops.append(add)
    if ln is None:
        out_specs, out_shape = row, jax.ShapeDtypeStruct((m, r), out_dtype)
    else:
        vec = pl.BlockSpec((1, r), lambda i: (0, 0))
        vshape = jax.ShapeDtypeStruct((1, r), F32)
        in_specs += [row, pl.BlockSpec((tm, 1), lambda i: (i, 0)), vec]
        ops += list(ln)
        out_specs, out_shape = [row, vec, vec, vec], [jax.ShapeDtypeStruct((m, r), F32), vshape, vshape, vshape]
    return pl.pallas_call(
        body, name=name, grid=(m // tm,),
        in_specs=in_specs, out_specs=out_specs, out_shape=out_shape,
        compiler_params=_cparams(1),
    )(*ops)


def mm_tn(a, b, b_block, b_map, nj, n, tk, ts, s, name, exchange=()):
    kx = a.shape[1]
    ts = min(ts, s)
    ne = len(exchange)
    grid = (kx // tk, nj, s // ts)

    def body(a_ref, b_ref, *rest):
        g_refs, o_ref, got_refs, sems = rest[:ne], rest[ne], rest[ne + 1:2 * ne + 1], rest[2 * ne + 1:]
        pids = [pl.program_id(ax) for ax in range(3)]
        if ne:
            @pl.when((pids[0] == 0) & (pids[1] == 0) & (pids[2] == 0))
            def _():
                _exchange_start(g_refs, got_refs, sems)

        part = _dot_tn(_bf(a_ref[...]), _bf(b_ref[...]))
        _acc(o_ref, part, pids[2] == 0)
        if ne:
            @pl.when((pids[0] == grid[0] - 1) & (pids[1] == grid[1] - 1) & (pids[2] == grid[2] - 1))
            def _():
                _exchange_finish(g_refs, got_refs, sems)

    res = pl.pallas_call(
        body, name=name, grid=grid,
        in_specs=[pl.BlockSpec((ts, tk), lambda kt, j, st: (st, kt)), pl.BlockSpec(b_block, b_map)] + [ANY] * ne,
        out_specs=[pl.BlockSpec((None, tk, n), lambda kt, j, st: (j, kt, 0))] + [ANY] * ne,
        out_shape=[jax.ShapeDtypeStruct((nj, kx, n), F32)] + _exchange_shapes(exchange),
        scratch_shapes=_exchange_sems(ne) if ne else [],
        compiler_params=_cparams(3),
    )(a, b, *exchange)
    return res if ne else res[0]


S5_TAB_ROWS = 8 * SUBLANES


def _s5_scan_table(tab_ref, ar, ai, reverse):
    n = N_STATE
    row = _rows((SUBLANES, n))
    edge = SUBLANES - 1 if reverse else 0
    tab_ref[0:8, :] = jnp.where(row == edge, ar, 0.0)
    tab_ref[8:16, :] = jnp.where(row == edge, ai, 0.0)
    pr, pi = ar, ai
    for step, k in enumerate((1, 2, 4)):
        mask = row < SUBLANES - k if reverse else row >= k
        tab_ref[16 + 16 * step:24 + 16 * step, :] = jnp.where(mask, pr, 0.0)
        tab_ref[24 + 16 * step:32 + 16 * step, :] = jnp.where(mask, pi, 0.0)
        pr, pi = pr * pr - pi * pi, 2.0 * pr * pi


def _s5_scan(src_ref, dst_ref, tab_ref, edge_ref, tb, reverse, per_tile=None):
    n = N_STATE
    ng = tb // SUBLANES
    nq = n // LANES
    link = SUBLANES - 1 if reverse else 1

    def tile(ii, carry):
        g = ng - 1 - ii if reverse else ii
        rows = pl.ds(pl.multiple_of(g * SUBLANES, SUBLANES), SUBLANES)
        out = []
        for q in range(nq):
            cre = slice(q * LANES, (q + 1) * LANES)
            cim = slice(n + q * LANES, n + (q + 1) * LANES)
            lr, li = src_ref[rows, cre], src_ref[rows, cim]
            tr, ti = pltpu.roll(carry[2 * q], link, 0), pltpu.roll(carry[2 * q + 1], link, 0)
            kr, ki = tab_ref[0:8, cre], tab_ref[8:16, cre]
            lr, li = lr + kr * tr - ki * ti, li + kr * ti + ki * tr
            for step, k in enumerate((1, 2, 4)):
                amt = SUBLANES - k if reverse else k
                kr, ki = tab_ref[16 + 16 * step:24 + 16 * step, cre], tab_ref[24 + 16 * step:32 + 16 * step, cre]
                sr, si = pltpu.roll(lr, amt, 0), pltpu.roll(li, amt, 0)
                lr, li = lr + kr * sr - ki * si, li + kr * si + ki * sr
            dst_ref[rows, cre] = lr
            dst_ref[rows, cim] = li
            if per_tile is not None:
                per_tile(g, q, (cre, cim), lr, li)
            out += [lr, li]
        return tuple(out)

    init = []
    for q in range(nq):
        init += [edge_ref[:, q * LANES:(q + 1) * LANES], edge_ref[:, n + q * LANES:n + (q + 1) * LANES]]
    fin = lax.fori_loop(0, ng, tile, tuple(init))
    for q in range(nq):
        edge_ref[:, q * LANES:(q + 1) * LANES] = fin[2 * q]
        edge_ref[:, n + q * LANES:n + (q + 1) * LANES] = fin[2 * q + 1]


def s5_fwd(h_in, a2, bexp, cexp, dskip, wglu, layer, bglu, tb, name, gather=()):
    s = h_in.shape[0]
    n = N_STATE
    ng = len(gather)

    def body(u_ref, a_ref, b_ref, c_ref, d_ref, w_ref, bg_ref, *rest):
        y_ref, h_ref, y0_ref = rest[ng:ng + 3]
        slabs = rest[ng + 3:2 * ng + 3]
        edge, tab, bu_ref = rest[2 * ng + 3:2 * ng + 6]
        sems = rest[2 * ng + 6:]

        @pl.when(pl.program_id(0) == 0)
        def _():
            if ng:
                _gather_start(slabs, sems[0:2])
            edge[...] = jnp.zeros_like(edge)
            _s5_scan_table(tab, a_ref[0:1, 0:n], a_ref[0:1, n:2 * n], False)

        u = u_ref[...]
        bu_ref[...] = _dot(_bf(u), b_ref[...])
        _s5_scan(bu_ref, h_ref, tab, edge, tb, False)
        y0 = _dot(_bf(h_ref[:, 0:n]), c_ref[0:n, :]) + _dot(_bf(h_ref[:, n:2 * n]), c_ref[n:2 * n, :]) + d_ref[...] * u
        y0_ref[...] = y0
        yg = _gelu(y0)
        z = _dot(_bf(yg), w_ref[...]) + bg_ref[...]
        y_ref[...] = _bf(yg * _sigmoid(z))
        if ng:
            @pl.when(pl.program_id(0) == s // tb - 1)
            def _():
                _gather_finish(slabs, sems[0:2], sems[2:4])

    vec = pl.BlockSpec((1, D_GROUP), lambda i: (0, 0))
    return pl.pallas_call(
        body, name=name, grid=(s // tb,),
        in_specs=[pl.BlockSpec((tb, D_GROUP), lambda i: (i, 0)),
                  pl.BlockSpec((1, 2 * n), lambda i: (0, 0)),
                  pl.BlockSpec((D_GROUP, 2 * n), lambda i: (0, 0)),
                  pl.BlockSpec((2 * n, D_GROUP), lambda i: (0, 0)),
                  vec,
                  pl.BlockSpec((None, D_GROUP, D_GROUP), lambda i: (layer, 0, 0)),
                  vec] + [ANY] * ng,
        out_specs=[pl.BlockSpec((tb, D_GROUP), lambda i: (i, 0)),
                   pl.BlockSpec((tb, 2 * n), lambda i: (i, 0)),
                   pl.BlockSpec((tb, D_GROUP), lambda i: (i, 0))] + [ANY] * ng,
        out_shape=[jax.ShapeDtypeStruct((s, D_GROUP), BF16), jax.ShapeDtypeStruct((s, 2 * n), F32),
                   jax.ShapeDtypeStruct((s, D_GROUP), F32)] + [jax.ShapeDtypeStruct(a.shape, a.dtype) for a in gather],
        input_output_aliases={7 + k: 3 + k for k in range(ng)},
        scratch_shapes=[pltpu.VMEM((SUBLANES, 2 * n), F32), pltpu.VMEM((S5_TAB_ROWS, n), F32),
                        pltpu.VMEM((tb, 2 * n), F32)] + (_gather_sems(ng) if ng else []),
        compiler_params=_cparams(1),
    )(h_in, a2, bexp, cexp, dskip, wglu, bglu, *gather)


def s5_bwd(dmix, h_in, y0, hst, a2, bexp, cexp, dskip, wglu, layer, bglu, tb, name):
    s = h_in.shape[0]
    n = N_STATE
    nb = s // tb
    halo = tb // 8

    def body(dy_ref, u_ref, y0_ref, h_ref, hp_ref, a_ref, b_ref, c_ref, d_ref, w_ref, bg_ref,
             du_ref, cs_ref, db_ref, dc_ref, dd_ref, dw_ref, dbg_ref, da_ref, edge, tab, g_ref, da_acc):
        i = pl.program_id(0)
        first = i == 0

        @pl.when(first)
        def _():
            edge[...] = jnp.zeros_like(edge)
            da_acc[...] = jnp.zeros_like(da_acc)
            _s5_scan_table(tab, a_ref[0:1, 0:n], -a_ref[0:1, n:2 * n], True)

        dy = dy_ref[...]
        u = u_ref[...]
        y0v = y0_ref[...]
        yg, t = _gelu_parts(y0v)
        z = _dot(_bf(yg), w_ref[...]) + bg_ref[...]
        sg = _sigmoid(z)
        dz = dy * yg * sg * (1.0 - sg)
        dyg = dy * sg + _dot_nt(_bf(dz), w_ref[...])
        _acc(dw_ref, _dot_tn(_bf(yg), _bf(dz)), first)
        _acc(dbg_ref, _colsum(dz), first)
        dy0 = dyg * _gelu_grad(y0v, t)
        _acc(dd_ref, _colsum(dy0 * u), first)
        dy0b = _bf(dy0)
        _acc(dc_ref.at[0:n, :], _dot_tn(_bf(h_ref[:, 0:n]), dy0b), first)
        _acc(dc_ref.at[n:2 * n, :], _dot_tn(_bf(h_ref[:, n:2 * n]), dy0b), first)
        g_ref[...] = _dot_nt(dy0b, c_ref[...])
        keep = jnp.where(i == nb - 1, 0.0, 1.0)
        row0 = _rows((SUBLANES, LANES)) == 0

        def grad_a(g, q, cols, gr, gi):
            cre, cim = cols
            rows = pl.ds(pl.multiple_of(g * SUBLANES, SUBLANES), SUBLANES)
            before = pl.ds(pl.multiple_of(jnp.maximum(g - 1, 0) * SUBLANES, SUBLANES), SUBLANES)
            pre = jnp.where(g == 0, hp_ref[:, cre] * keep, h_ref[before, cre])
            pim = jnp.where(g == 0, hp_ref[:, cim] * keep, h_ref[before, cim])
            pr = jnp.where(row0, pltpu.roll(pre, 1, 0), pltpu.roll(h_ref[rows, cre], 1, 0))
            pi = jnp.where(row0, pltpu.roll(pim, 1, 0), pltpu.roll(h_ref[rows, cim], 1, 0))
            da_acc[:, cre] += gr * pr + gi * pi
            da_acc[:, cim] += gi * pr - gr * pi

        _s5_scan(g_ref, g_ref, tab, edge, tb, True, grad_a)
        da_ref[...] = _colsum(da_acc[...])
        gr, gi = g_ref[:, 0:n], g_ref[:, n:2 * n]
        grb, gib = _bf(gr), _bf(gi)
        du = d_ref[...] * dy0 + _dot_nt(grb, b_ref[:, 0:n]) + _dot_nt(gib, b_ref[:, n:2 * n])
        ub = _bf(u)
        _acc(db_ref.at[:, 0:n], _dot_tn(ub, grb), first)
        _acc(db_ref.at[:, n:2 * n], _dot_tn(ub, gib), first)
        du_ref[...] = _bf(du)
        _acc(cs_ref, _colsum(du), first)

    rev = lambda i: (nb - 1 - i, 0)
    vec = pl.BlockSpec((1, D_GROUP), lambda i: (0, 0))
    vshape = jax.ShapeDtypeStruct((1, D_GROUP), F32)
    return pl.pallas_call(
        body, name=name, grid=(nb,),
        in_specs=[pl.BlockSpec((tb, D_GROUP), rev),
                  pl.BlockSpec((tb, D_GROUP), rev),
                  pl.BlockSpec((tb, D_GROUP), rev),
                  pl.BlockSpec((tb, 2 * n), rev),
                  pl.BlockSpec((8, 2 * n), lambda i: (jnp.maximum((nb - 1 - i) * halo - 1, 0), 0)),
                  pl.BlockSpec((1, 2 * n), lambda i: (0, 0)),
                  pl.BlockSpec((D_GROUP, 2 * n), lambda i: (0, 0)),
                  pl.BlockSpec((2 * n, D_GROUP), lambda i: (0, 0)),
                  vec,
                  pl.BlockSpec((None, D_GROUP, D_GROUP), lambda i: (layer, 0, 0)),
                  vec],
        out_specs=[pl.BlockSpec((tb, D_GROUP), rev), vec,
                   pl.BlockSpec((D_GROUP, 2 * n), lambda i: (0, 0)),
                   pl.BlockSpec((2 * n, D_GROUP), lambda i: (0, 0)),
                   vec,
                   pl.BlockSpec((D_GROUP, D_GROUP), lambda i: (0, 0)),
                   vec,
                   pl.BlockSpec((1, 2 * n), lambda i: (0, 0))],
        out_shape=[jax.ShapeDtypeStruct((s, D_GROUP), BF16), vshape,
                   jax.ShapeDtypeStruct((D_GROUP, 2 * n), F32), jax.ShapeDtypeStruct((2 * n, D_GROUP), F32),
                   vshape, jax.ShapeDtypeStruct((D_GROUP, D_GROUP), F32), vshape,
                   jax.ShapeDtypeStruct((1, 2 * n), F32)],
        scratch_shapes=[pltpu.VMEM((SUBLANES, 2 * n), F32), pltpu.VMEM((S5_TAB_ROWS, n), F32),
                        pltpu.VMEM((tb, 2 * n), F32), pltpu.VMEM((SUBLANES, 2 * n), F32)],
        compiler_params=_cparams(1),
    )(dmix, h_in, y0, hst, hst, a2, bexp, cexp, dskip, wglu, bglu)


CV_HALO = 32


def _gn_stats(hc, pmat):
    mu = _dot3(hc, pmat)
    xc = hc - mu
    var = _dot3(xc * xc, pmat)
    rstd = lax.rsqrt(var + LN_EPS)
    return xc * rstd, rstd


def cv_fwd(h_in, cw, cb, gg, gb, pmat, wpw, layer, bpw, tb, name):
    s = h_in.shape[0]
    hl = CV_HALO

    def body(v_ref, g_ref, cw_ref, cb_ref, gg_ref, gb_ref, p_ref, w_ref, bw_ref, y_ref, hc_ref, ext):
        @pl.when(pl.program_id(0) == 0)
        def _():
            ext[0:hl, :] = jnp.zeros((hl, D_GROUP), F32)

        ext[hl:hl + tb, :] = v_ref[...] * _sigmoid(g_ref[...])
        acc = jnp.zeros((tb, D_GROUP), F32) + cb_ref[...]
        for k in range(CONV_WIDTH):
            off = hl - (CONV_WIDTH - 1) + k
            acc = acc + cw_ref[k:k + 1, :] * ext[off:off + tb, :]
        hc_ref[...] = acc
        ext[0:hl, :] = ext[tb:tb + hl, :]
        xn, _ = _gn_stats(acc, p_ref[...])
        hn = xn * gg_ref[...] + gb_ref[...]
        hs = hn * _sigmoid(hn)
        y_ref[...] = _bf(_dot(_bf(hs), w_ref[...]) + bw_ref[...])

    vec = pl.BlockSpec((1, D_GROUP), lambda i: (0, 0))
    sq = pl.BlockSpec((D_GROUP, D_GROUP), lambda i: (0, 0))
    return pl.pallas_call(
        body, name=name, grid=(s // tb,),
        in_specs=[pl.BlockSpec((tb, D_GROUP), lambda i: (i, 1)),
                  pl.BlockSpec((tb, D_GROUP), lambda i: (i, 2)),
                  pl.BlockSpec((hl, D_GROUP), lambda i: (0, 0)),
                  vec, vec, vec, sq,
                  pl.BlockSpec((None, D_GROUP, D_GROUP), lambda i: (layer, 0, 0)),
                  vec],
        out_specs=[pl.BlockSpec((tb, D_GROUP), lambda i: (i, 0)), pl.BlockSpec((tb, D_GROUP), lambda i: (i, 0))],
        out_shape=[jax.ShapeDtypeStruct((s, D_GROUP), BF16), jax.ShapeDtypeStruct((s, D_GROUP), F32)],
        scratch_shapes=[pltpu.VMEM((hl + tb, D_GROUP), F32)],
        compiler_params=_cparams(1),
    )(h_in, h_in, cw, cb, gg, gb, pmat, wpw, bpw)


def cv_bwd(dmix, h_in, hc, cw, gg, gb, pmat, wpw, layer, tb, name, scatter=()):
    s = h_in.shape[0]
    hl = CV_HALO
    nb = s // tb
    per = tb // hl
    ns = len(scatter)

    def body(dy_ref, v_ref, g_ref, vh_ref, gh_ref, hc_ref, cw_ref, gg_ref, gb_ref, p_ref, w_ref, *rest):
        s_refs = rest[:ns]
        dvg_ref, cs_ref, dcw_ref, dcb_ref, dgg_ref, dgb_ref, dw_ref, dbw_ref = rest[ns:ns + 8]
        got_refs = rest[ns + 8:2 * ns + 8]
        ext, dext, head = rest[2 * ns + 8:2 * ns + 11]
        sems = rest[2 * ns + 11:]
        i = pl.program_id(0)
        first = i == 0

        @pl.when(first)
        def _():
            if ns:
                _scatter_start(s_refs, got_refs, sems)
            head[...] = jnp.zeros_like(head)

        dy = dy_ref[...]
        pm = p_ref[...]
        xn, rstd = _gn_stats(hc_ref[...], pm)
        hn = xn * gg_ref[...] + gb_ref[...]
        sg = _sigmoid(hn)
        hs = hn * sg
        dyb = _bf(dy)
        _acc(dbw_ref, _colsum(dy), first)
        _acc(dw_ref, _dot_tn(_bf(hs), dyb), first)
        dhs = _dot_nt(dyb, w_ref[...])
        dhn = dhs * sg * (1.0 + hn * (1.0 - sg))
        _acc(dgg_ref, _colsum(dhn * xn), first)
        _acc(dgb_ref, _colsum(dhn), first)
        dxn = dhn * gg_ref[...]
        dhc = rstd * (dxn - _dot3(dxn, pm) - xn * _dot3(dxn * xn, pm))
        _acc(dcb_ref, _colsum(dhc), first)
        v = v_ref[...]
        sgg = _sigmoid(g_ref[...])
        keep = jnp.where(i == nb - 1, 0.0, 1.0)
        ext[0:hl, :] = vh_ref[...] * _sigmoid(gh_ref[...]) * keep
        ext[hl:hl + tb, :] = v * sgg
        dext[0:tb, :] = dhc
        dext[tb:tb + hl, :] = head[...]
        head[...] = dhc[0:hl]
        dhg = jnp.zeros((tb, D_GROUP), F32)
        for k in range(CONV_WIDTH):
            off = hl - (CONV_WIDTH - 1) + k
            wk = _colsum(dhc * ext[off:off + tb, :])
            _acc(dcw_ref.at[k:k + 1, :], wk, first)
            back = CONV_WIDTH - 1 - k
            dhg = dhg + cw_ref[k:k + 1, :] * dext[back:back + tb, :]

        @pl.when(first)
        def _():
            dcw_ref[CONV_WIDTH:hl, :] = jnp.zeros((hl - CONV_WIDTH, D_GROUP), F32)

        dv = dhg * sgg
        dg = dhg * v * sgg * (1.0 - sgg)
        dvg_ref[:, 0:D_GROUP] = _bf(dv)
        dvg_ref[:, D_GROUP:2 * D_GROUP] = _bf(dg)
        _acc(cs_ref.at[:, 0:D_GROUP], _colsum(dv), first)
        _acc(cs_ref.at[:, D_GROUP:2 * D_GROUP], _colsum(dg), first)
        if ns:
            @pl.when(i == nb - 1)
            def _():
                _scatter_finish(s_refs, got_refs, sems)

    vec = pl.BlockSpec((1, D_GROUP), lambda i: (0, 0))
    sq = pl.BlockSpec((D_GROUP, D_GROUP), lambda i: (0, 0))
    tap = pl.BlockSpec((hl, D_GROUP), lambda i: (0, 0))
    vshape = jax.ShapeDtypeStruct((1, D_GROUP), F32)

    def blk(col):
        return pl.BlockSpec((tb, D_GROUP), lambda i: (nb - 1 - i, col))

    def halo_blk(col):
        return pl.BlockSpec((hl, D_GROUP), lambda i: (jnp.maximum((nb - 1 - i) * per - 1, 0), col))

    return pl.pallas_call(
        body, name=name, grid=(nb,),
        in_specs=[blk(1), blk(1), blk(2), halo_blk(1), halo_blk(2),
                  pl.BlockSpec((tb, D_GROUP), lambda i: (nb - 1 - i, 0)),
                  tap, vec, vec, sq,
                  pl.BlockSpec((None, D_GROUP, D_GROUP), lambda i: (layer, 0, 0))] + [ANY] * ns,
        out_specs=[pl.BlockSpec((tb, 2 * D_GROUP), lambda i: (nb - 1 - i, 0)),
                   pl.BlockSpec((1, 2 * D_GROUP), lambda i: (0, 0)),
                   tap, vec, vec, vec, sq, vec] + [ANY] * ns,
        out_shape=[jax.ShapeDtypeStruct((s, 2 * D_GROUP), BF16), jax.ShapeDtypeStruct((1, 2 * D_GROUP), F32),
                   jax.ShapeDtypeStruct((hl, D_GROUP), F32), vshape, vshape, vshape,
                   jax.ShapeDtypeStruct((D_GROUP, D_GROUP), F32), vshape] + _scatter_shapes(scatter),
        scratch_shapes=[pltpu.VMEM((hl + tb, D_GROUP), F32), pltpu.VMEM((tb + hl, D_GROUP), F32),
                        pltpu.VMEM((hl, D_GROUP), F32)] + (_scatter_sems(ns) if ns else []),
        compiler_params=_cparams(1),
    )(dmix, h_in, h_in, h_in, h_in, hc, cw, gg, gb, pmat, wpw, *scatter)


LRU_HALO = 8


def _lru_gates(xc, wr_ref, br_ref, wi_ref, bi_ref, sp_ref):
    xcb = _bf(xc)
    r = _sigmoid(_dot(xcb, wr_ref[...]) + br_ref[...])
    gi = _sigmoid(_dot(xcb, wi_ref[...]) + bi_ref[...])
    la = -LRU_C * r * sp_ref[...]
    a = jnp.exp(la)
    e2 = a * a
    sq = jnp.sqrt(-jnp.tanh(la) * (e2 + 1.0))
    return r, gi, a, e2, sq


def _rscan(a, b, tb, reverse):
    row = _rows(a.shape)
    sh = 1
    while sh < tb:
        if reverse:
            amt, mask = tb - sh, row < tb - sh
        else:
            amt, mask = sh, row >= sh
        a_s = jnp.where(mask, pltpu.roll(a, amt, 0), 1.0)
        b_s = jnp.where(mask, pltpu.roll(b, amt, 0), 0.0)
        b = b + a * b_s
        a = a * a_s
        sh *= 2
    return a, b


def lru_fwd(h_in, cw, cb, wr, br, wi, bi, sp, tb, name):
    s = h_in.shape[0]
    hl = LRU_HALO

    def body(xg_ref, xr_ref, cw_ref, cb_ref, wr_ref, br_ref, wi_ref, bi_ref, sp_ref, y_ref, xc_ref, h_ref, ext, carry):
        @pl.when(pl.program_id(0) == 0)
        def _():
            ext[0:hl, :] = jnp.zeros((hl, D_GROUP), F32)
            carry[...] = jnp.zeros_like(carry)

        ext[hl:hl + tb, :] = xr_ref[...]
        xc = jnp.zeros((tb, D_GROUP), F32) + cb_ref[...]
        for k in range(LRU_CONV_WIDTH):
            off = hl - (LRU_CONV_WIDTH - 1) + k
            xc = xc + cw_ref[k:k + 1, :] * ext[off:off + tb, :]
        xc_ref[...] = xc
        ext[0:hl, :] = ext[tb:tb + hl, :]
        r, gi, a, e2, sq = _lru_gates(xc, wr_ref, br_ref, wi_ref, bi_ref, sp_ref)
        pa, hloc = _rscan(a, sq * (gi * xc), tb, False)
        h = hloc + pa * carry[7:8, :]
        h_ref[...] = h
        carry[...] = h[tb - 8:tb]
        y_ref[...] = _bf(h * _gelu(xg_ref[...]))

    vec = pl.BlockSpec((1, D_GROUP), lambda i: (0, 0))
    sq_spec = pl.BlockSpec((D_GROUP, D_GROUP), lambda i: (0, 0))
    blk = pl.BlockSpec((tb, D_GROUP), lambda i: (i, 0))
    return pl.pallas_call(
        body, name=name, grid=(s // tb,),
        in_specs=[pl.BlockSpec((tb, D_GROUP), lambda i: (i, 3)),
                  pl.BlockSpec((tb, D_GROUP), lambda i: (i, 4)),
                  pl.BlockSpec((hl, D_GROUP), lambda i: (0, 0)),
                  vec, sq_spec, vec, sq_spec, vec, vec],
        out_specs=[blk, blk, blk],
        out_shape=[jax.ShapeDtypeStruct((s, D_GROUP), BF16), jax.ShapeDtypeStruct((s, D_GROUP), F32),
                   jax.ShapeDtypeStruct((s, D_GROUP), F32)],
        scratch_shapes=[pltpu.VMEM((hl + tb, D_GROUP), F32), pltpu.VMEM((8, D_GROUP), F32)],
        compiler_params=_cparams(1),
    )(h_in, h_in, cw, cb, wr, br, wi, bi, sp)


def lru_bwd(dmix, h_in, xcs, hs, cw, wr, br, wi, bi, sp, tb, name):
    s = h_in.shape[0]
    hl = LRU_HALO
    nb = s // tb
    per = tb // hl

    def body(dy_ref, xg_ref, xr_ref, xrh_ref, xc_ref, h_ref, hp_ref, cw_ref, wr_ref, br_ref, wi_ref, bi_ref, sp_ref,
             dx_ref, cs_ref, dcw_ref, dcb_ref, dwr_ref, dbr_ref, dwi_ref, dbi_ref, dsp_ref,
             ext, dext, head, anext, gnext):
        i = pl.program_id(0)
        first = i == 0

        @pl.when(first)
        def _():
            head[...] = jnp.zeros_like(head)
            anext[...] = jnp.zeros_like(anext)
            gnext[...] = jnp.zeros_like(gnext)

        dy = dy_ref[...]
        xg = xg_ref[...]
        xc = xc_ref[...]
        h = h_ref[...]
        r, gi, a, e2, sq = _lru_gates(xc, wr_ref, br_ref, wi_ref, bi_ref, sp_ref)
        gate, t = _gelu_parts(xg)
        dh = dy * gate
        dxg = dy * h * _gelu_grad(xg, t)
        row = _rows((tb, D_GROUP))
        coef = jnp.where(row == tb - 1, anext[0:1, :], pltpu.roll(a, tb - 1, 0))
        pc, gloc = _rscan(coef, dh, tb, True)
        gfull = gloc + pc * gnext[0:1, :]
        anext[...] = a[0:8]
        gnext[...] = gfull[0:8]
        keep = jnp.where(i == nb - 1, 0.0, 1.0)
        hprev = jnp.where(row == 0, hp_ref[7:8, :] * keep, pltpu.roll(h, 1, 0))
        da = gfull * hprev
        uu = gi * xc
        dsq = gfull * uu
        duu = gfull * sq
        dla = da * a - dsq * e2 / sq
        sp = sp_ref[...]
        dr = dla * (-LRU_C) * sp
        _acc(dsp_ref, _colsum(dla * (-LRU_C) * r), first)
        dzr = dr * r * (1.0 - r)
        dzi = duu * xc * gi * (1.0 - gi)
        dzrb, dzib = _bf(dzr), _bf(dzi)
        dxc = duu * gi + _dot_nt(dzrb, wr_ref[...]) + _dot_nt(dzib, wi_ref[...])
        xcb = _bf(xc)
        _acc(dwr_ref, _dot_tn(xcb, dzrb), first)
        _acc(dwi_ref, _dot_tn(xcb, dzib), first)
        _acc(dbr_ref, _colsum(dzr), first)
        _acc(dbi_ref, _colsum(dzi), first)
        _acc(dcb_ref, _colsum(dxc), first)
        ext[0:hl, :] = xrh_ref[...] * keep
        ext[hl:hl + tb, :] = xr_ref[...]
        dext[0:tb, :] = dxc
        dext[tb:tb + hl, :] = head[...]
        head[...] = dxc[0:hl]
        dxr = jnp.zeros((tb, D_GROUP), F32)
        for k in range(LRU_CONV_WIDTH):
            off = hl - (LRU_CONV_WIDTH - 1) + k
            _acc(dcw_ref.at[k:k + 1, :], _colsum(dxc * ext[off:off + tb, :]), first)
            back = LRU_CONV_WIDTH - 1 - k
            dxr = dxr + cw_ref[k:k + 1, :] * dext[back:back + tb, :]

        @pl.when(first)
        def _():
            dcw_ref[LRU_CONV_WIDTH:hl, :] = jnp.zeros((hl - LRU_CONV_WIDTH, D_GROUP), F32)

        dx_ref[:, 0:D_GROUP] = _bf(dxg)
        dx_ref[:, D_GROUP:2 * D_GROUP] = _bf(dxr)
        _acc(cs_ref.at[:, 0:D_GROUP], _colsum(dxg), first)
        _acc(cs_ref.at[:, D_GROUP:2 * D_GROUP], _colsum(dxr), first)

    vec = pl.BlockSpec((1, D_GROUP), lambda i: (0, 0))
    sq_spec = pl.BlockSpec((D_GROUP, D_GROUP), lambda i: (0, 0))
    tap = pl.BlockSpec((hl, D_GROUP), lambda i: (0, 0))
    vshape = jax.ShapeDtypeStruct((1, D_GROUP), F32)
    sshape = jax.ShapeDtypeStruct((D_GROUP, D_GROUP), F32)

    def blk(col):
        return pl.BlockSpec((tb, D_GROUP), lambda i: (nb - 1 - i, col))

    def halo_blk(col):
        return pl.BlockSpec((hl, D_GROUP), lambda i: (jnp.maximum((nb - 1 - i) * per - 1, 0), col))

    return pl.pallas_call(
        body, name=name, grid=(nb,),
        in_specs=[blk(2), blk(3), blk(4), halo_blk(4), blk(0), blk(0), halo_blk(0),
                  tap, sq_spec, vec, sq_spec, vec, vec],
        out_specs=[pl.BlockSpec((tb, 2 * D_GROUP), lambda i: (nb - 1 - i, 0)),
                   pl.BlockSpec((1, 2 * D_GROUP), lambda i: (0, 0)),
                   tap, vec, sq_spec, vec, sq_spec, vec, vec],
        out_shape=[jax.ShapeDtypeStruct((s, 2 * D_GROUP), BF16), jax.ShapeDtypeStruct((1, 2 * D_GROUP), F32),
                   jax.ShapeDtypeStruct((hl, D_GROUP), F32), vshape, sshape, vshape, sshape, vshape, vshape],
        scratch_shapes=[pltpu.VMEM((hl + tb, D_GROUP), F32), pltpu.VMEM((tb + hl, D_GROUP), F32),
                        pltpu.VMEM((hl, D_GROUP), F32), pltpu.VMEM((8, D_GROUP), F32), pltpu.VMEM((8, D_GROUP), F32)],
        compiler_params=_cparams(1),
    )(dmix, h_in, h_in, h_in, xcs, hs, hs, cw, wr, br, wi, bi, sp)


ATTN_HEADS = 4
ATTN_HEAD_DIM = 64
ATTN_SCALE = ATTN_HEAD_DIM ** -0.5


def _head_mask(h):
    lane = lax.broadcasted_iota(jnp.int32, (1, D_GROUP), 1)
    return jnp.where((lane >= h * ATTN_HEAD_DIM) & (lane < (h + 1) * ATTN_HEAD_DIM), 1.0, 0.0)


def _softmax_rows(sc):
    e = jnp.exp(sc - jnp.max(sc, -1, keepdims=True))
    return e / jnp.sum(e, -1, keepdims=True)


def attn_fwd(h_in, kv, tb, name):
    s = h_in.shape[0]

    def body(q_ref, kv_ref, y_ref):
        q = q_ref[...]
        kb = _bf(kv_ref[:, 0:D_GROUP])
        vb = _bf(kv_ref[:, D_GROUP:2 * D_GROUP])
        out = jnp.zeros((tb, D_GROUP), F32)
        for h in range(ATTN_HEADS):
            mask = _head_mask(h)
            p = _softmax_rows(_dot_nt(_bf(q * mask), kb) * ATTN_SCALE)
            out = out + _dot(_bf(p), vb) * mask
        y_ref[...] = _bf(out)

    return pl.pallas_call(
        body, name=name, grid=(s // tb,),
        in_specs=[pl.BlockSpec((tb, D_GROUP), lambda i: (i, 5)),
                  pl.BlockSpec((D_GROUP, 2 * D_GROUP), lambda i: (0, 0))],
        out_specs=pl.BlockSpec((tb, D_GROUP), lambda i: (i, 0)),
        out_shape=jax.ShapeDtypeStruct((s, D_GROUP), BF16),
        compiler_params=_cparams(1),
    )(h_in, kv)


def attn_bwd(dmix, h_in, kv, tb, name, exchange=()):
    s = h_in.shape[0]
    ne = len(exchange)

    def body(do_ref, q_ref, kv_ref, *rest):
        g_refs = rest[:ne]
        dq_ref, cs_ref, dkv_ref = rest[ne:ne + 3]
        got_refs, sems = rest[ne + 3:2 * ne + 3], rest[2 * ne + 3:]
        first = pl.program_id(0) == 0
        if ne:
            @pl.when(first)
            def _():
                _exchange_start(g_refs, got_refs, sems)

        q = q_ref[...]
        do = do_ref[...]
        kb = _bf(kv_ref[:, 0:D_GROUP])
        vb = _bf(kv_ref[:, D_GROUP:2 * D_GROUP])
        dq = jnp.zeros((tb, D_GROUP), F32)
        dk = jnp.zeros((D_GROUP, D_GROUP), F32)
        dv = jnp.zeros((D_GROUP, D_GROUP), F32)
        for h in range(ATTN_HEADS):
            mask = _head_mask(h)
            qm = _bf(q * mask)
            p = _softmax_rows(_dot_nt(qm, kb) * ATTN_SCALE)
            dom = _bf(do * mask)
            dp = _dot_nt(dom, vb)
            dv = dv + _dot_tn(_bf(p), dom)
            ds = _bf(p * (dp - jnp.sum(dp * p, -1, keepdims=True)) * ATTN_SCALE)
            dq = dq + _dot(ds, kb) * mask
            dk = dk + _dot_tn(ds, qm)
        dq_ref[...] = _bf(dq)
        _acc(cs_ref, _colsum(dq), first)
        _acc(dkv_ref.at[:, 0:D_GROUP], dk, first)
        _acc(dkv_ref.at[:, D_GROUP:2 * D_GROUP], dv, first)
        if ne:
            @pl.when(pl.program_id(0) == s // tb - 1)
            def _():
                _exchange_finish(g_refs, got_refs, sems)

    return pl.pallas_call(
        body, name=name, grid=(s // tb,),
        in_specs=[pl.BlockSpec((tb, D_GROUP), lambda i: (i, 3)),
                  pl.BlockSpec((tb, D_GROUP), lambda i: (i, 5)),
                  pl.BlockSpec((D_GROUP, 2 * D_GROUP), lambda i: (0, 0))] + [ANY] * ne,
        out_specs=[pl.BlockSpec((tb, D_GROUP), lambda i: (i, 0)),
                   pl.BlockSpec((1, D_GROUP), lambda i: (0, 0)),
                   pl.BlockSpec((D_GROUP, 2 * D_GROUP), lambda i: (0, 0))] + [ANY] * ne,
        out_shape=[jax.ShapeDtypeStruct((s, D_GROUP), BF16), jax.ShapeDtypeStruct((1, D_GROUP), F32),
                   jax.ShapeDtypeStruct((D_GROUP, 2 * D_GROUP), F32)] + _exchange_shapes(exchange),
        scratch_shapes=_exchange_sems(ne) if ne else [],
        compiler_params=_cparams(1),
    )(dmix, h_in, kv, *exchange)


FFN_RB = 16
FFN_UNROLL_FWD = 4
FFN_UNROLL_BWD = 2
FFN_TAP_ROWS = 8
FFN_TN = D_FF // 2


def _shift_down(cur, tail, k):
    return pltpu.roll(jnp.concatenate([tail, cur], axis=0), k, 0)[SUBLANES:]


def _shift_up(cur, head, k):
    rb = cur.shape[0]
    return pltpu.roll(jnp.concatenate([cur, head], axis=0), rb + SUBLANES - k, 0)[:rb]


def _fold8(v):
    tot = v[0:SUBLANES]
    for t in range(1, v.shape[0] // SUBLANES):
        tot = tot + v[t * SUBLANES:(t + 1) * SUBLANES]
    return tot


def _strip(r):
    return pl.ds(pl.multiple_of(r * FFN_RB, FFN_RB), FFN_RB)


def ffn_act_fwd(u, cw, cb, tb, name, gather=()):
    s = u.shape[0]
    rb = FFN_RB
    nct = D_FF // FFN_TN
    nstrip = tb // rb
    ng = len(gather)

    def body(uv_ref, ug_ref, wv_ref, wg_ref, bv_ref, bg_ref, *rest):
        hf_ref, keep_ref = rest[ng], rest[ng + 1]
        slabs = rest[ng + 2:2 * ng + 2]
        tailv, tailg = rest[2 * ng + 2], rest[2 * ng + 3]
        sems = rest[2 * ng + 4:]
        if ng:
            @pl.when((pl.program_id(0) == 0) & (pl.program_id(1) == 0))
            def _():
                _gather_start(slabs, sems[0:2])

        @pl.when(pl.program_id(1) == 0)
        def _():
            tailv[...] = jnp.zeros_like(tailv)
            tailg[...] = jnp.zeros_like(tailg)

        for cc in range(FFN_TN // LANES):
            cols = slice(cc * LANES, (cc + 1) * LANES)
            wv = [wv_ref[k:k + 1, cols] for k in range(FFN_CONV_WIDTH)]
            wg = [wg_ref[k:k + 1, cols] for k in range(FFN_CONV_WIDTH)]
            bv, bg = bv_ref[:, cols], bg_ref[:, cols]

            def strip(r, carry):
                tail_v, tail_g = carry
                cur_v, cur_g = uv_ref[_strip(r), cols], ug_ref[_strip(r), cols]
                vc = wv[0] * _shift_down(cur_v, tail_v, 2) + wv[1] * _shift_down(cur_v, tail_v, 1) + wv[2] * cur_v + bv
                gc = wg[0] * _shift_down(cur_g, tail_g, 2) + wg[1] * _shift_down(cur_g, tail_g, 1) + wg[2] * cur_g + bg
                ge, t = _gelu_parts(gc)
                hf_ref[_strip(r), cols] = _bf(vc * ge)
                keep_ref[0, _strip(r), cols] = _bf(vc)
                keep_ref[1, _strip(r), cols] = _bf(ge)
                keep_ref[2, _strip(r), cols] = _bf(_gelu_grad(gc, t))
                return cur_v[rb - SUBLANES:], cur_g[rb - SUBLANES:]

            def strips(q, carry):
                for k in range(FFN_UNROLL_FWD):
                    carry = strip(q * FFN_UNROLL_FWD + k, carry)
                return carry

            last_v, last_g = lax.fori_loop(0, nstrip // FFN_UNROLL_FWD, strips, (tailv[:, cols], tailg[:, cols]))
            tailv[:, cols] = last_v
            tailg[:, cols] = last_g

        if ng:
            @pl.when((pl.program_id(0) == nct - 1) & (pl.program_id(1) == s // tb - 1))
            def _():
                _gather_finish(slabs, sems[0:2], sems[2:4])

    return pl.pallas_call(
        body, name=name, grid=(nct, s // tb),
        in_specs=[pl.BlockSpec((tb, FFN_TN), lambda c, i: (i, c)),
                  pl.BlockSpec((tb, FFN_TN), lambda c, i: (i, c + nct)),
                  pl.BlockSpec((FFN_TAP_ROWS, FFN_TN), lambda c, i: (0, c)),
                  pl.BlockSpec((FFN_TAP_ROWS, FFN_TN), lambda c, i: (0, c + nct)),
                  pl.BlockSpec((1, FFN_TN), lambda c, i: (0, c)),
                  pl.BlockSpec((1, FFN_TN), lambda c, i: (0, c + nct))] + [ANY] * ng,
        out_specs=[pl.BlockSpec((tb, FFN_TN), lambda c, i: (i, c)),
                   pl.BlockSpec((3, tb, FFN_TN), lambda c, i: (0, i, c))] + [ANY] * ng,
        out_shape=[jax.ShapeDtypeStruct((s, D_FF), BF16), jax.ShapeDtypeStruct((3, s, D_FF), BF16)]
        + [jax.ShapeDtypeStruct(a.shape, a.dtype) for a in gather],
        input_output_aliases={6 + k: 2 + k for k in range(ng)},
        scratch_shapes=[pltpu.VMEM((SUBLANES, FFN_TN), F32), pltpu.VMEM((SUBLANES, FFN_TN), F32)]
        + (_gather_sems(ng) if ng else []),
        compiler_params=_cparams(2),
    )(u, u, cw, cw, cb, cb, *gather)


def ffn_act_bwd(dhf, u, kept, cw, tb, name, scatter=()):
    s = u.shape[0]
    rb = FFN_RB
    nct = D_FF // FFN_TN
    nb = s // tb
    nstrip = tb // rb
    ntap = FFN_CONV_WIDTH
    ns = len(scatter)

    def body(dh_ref, uv_ref, ug_ref, kept_ref, wv_ref, wg_ref, *rest):
        s_refs = rest[:ns]
        du_ref, dwv_ref, dwg_ref = rest[ns:ns + 3]
        got_refs = rest[ns + 3:2 * ns + 3]
        headv, headg = rest[2 * ns + 3], rest[2 * ns + 4]
        sems = rest[2 * ns + 5:]
        i = pl.program_id(1)
        first = i == 0
        if ns:
            @pl.when((pl.program_id(0) == 0) & first)
            def _():
                _scatter_start(s_refs, got_refs, sems)

        @pl.when(first)
        def _():
            headv[...] = jnp.zeros_like(headv)
            headg[...] = jnp.zeros_like(headg)
            dwv_ref[...] = jnp.zeros_like(dwv_ref)
            dwg_ref[...] = jnp.zeros_like(dwg_ref)

        zero = jnp.zeros((SUBLANES, LANES), F32)
        for cc in range(FFN_TN // LANES):
            cols = slice(cc * LANES, (cc + 1) * LANES)
            wv = [wv_ref[k:k + 1, cols] for k in range(ntap)]
            wg = [wg_ref[k:k + 1, cols] for k in range(ntap)]

            def strip(ii, carry):
                head_dv, head_dg, acc_v, acc_g = carry
                r = nstrip - 1 - ii
                dh = dh_ref[_strip(r), cols]
                dvc = dh * kept_ref[1, _strip(r), cols].astype(F32)
                dgc = dh * kept_ref[0, _strip(r), cols].astype(F32) * kept_ref[2, _strip(r), cols].astype(F32)
                sdv = [_shift_up(dvc, head_dv, 2), _shift_up(dvc, head_dv, 1), dvc]
                sdg = [_shift_up(dgc, head_dg, 2), _shift_up(dgc, head_dg, 1), dgc]
                cur_v, cur_g = uv_ref[_strip(r), cols], ug_ref[_strip(r), cols]
                acc_v = tuple(acc_v[k] + _fold8(cur_v * sdv[k]) for k in range(ntap)) + (acc_v[ntap] + _fold8(dvc),)
                acc_g = tuple(acc_g[k] + _fold8(cur_g * sdg[k]) for k in range(ntap)) + (acc_g[ntap] + _fold8(dgc),)
                du_v = wv[0] * sdv[0] + wv[1] * sdv[1] + wv[2] * sdv[2]
                du_g = wg[0] * sdg[0] + wg[1] * sdg[1] + wg[2] * sdg[2]
                du_ref[0, _strip(r), cols] = _bf(du_v)
                du_ref[1, _strip(r), cols] = _bf(du_g)
                return dvc[0:SUBLANES], dgc[0:SUBLANES], acc_v, acc_g

            init = (headv[:, cols], headg[:, cols], (zero,) * (ntap + 1), (zero,) * (ntap + 1))
            def strips(q, carry):
                for k in range(FFN_UNROLL_BWD):
                    carry = strip(q * FFN_UNROLL_BWD + k, carry)
                return carry

            top_dv, top_dg, acc_v, acc_g = lax.fori_loop(0, nstrip // FFN_UNROLL_BWD, strips, init)
            headv[:, cols] = top_dv
            headg[:, cols] = top_dg
            for k in range(ntap + 1):
                dwv_ref[k:k + 1, cols] += _colsum(acc_v[k])
                dwg_ref[k:k + 1, cols] += _colsum(acc_g[k])

        if ns:
            @pl.when((pl.program_id(0) == nct - 1) & (i == nb - 1))
            def _():
                _scatter_finish(s_refs, got_refs, sems)

    def blk(shift):
        return pl.BlockSpec((tb, FFN_TN), lambda c, i: (nb - 1 - i, c + shift))

    tapv = pl.BlockSpec((FFN_TAP_ROWS, FFN_TN), lambda c, i: (0, c))
    tapg = pl.BlockSpec((FFN_TAP_ROWS, FFN_TN), lambda c, i: (0, c + nct))
    return pl.pallas_call(
        body, name=name, grid=(nct, nb),
        in_specs=[blk(0), blk(0), blk(nct), pl.BlockSpec((3, tb, FFN_TN), lambda c, i: (0, nb - 1 - i, c)), tapv, tapg]
        + [ANY] * ns,
        out_specs=[pl.BlockSpec((2, tb, FFN_TN), lambda c, i: (0, nb - 1 - i, c)), tapv, tapv] + [ANY] * ns,
        out_shape=[jax.ShapeDtypeStruct((2, s, D_FF), BF16), jax.ShapeDtypeStruct((FFN_TAP_ROWS, D_FF), F32),
                   jax.ShapeDtypeStruct((FFN_TAP_ROWS, D_FF), F32)] + _scatter_shapes(scatter),
        scratch_shapes=[pltpu.VMEM((SUBLANES, FFN_TN), F32), pltpu.VMEM((SUBLANES, FFN_TN), F32)]
        + (_scatter_sems(ns) if ns else []),
        compiler_params=_cparams(2),
    )(dhf, u, u, kept, cw, cw, *scatter)


def _place():
    x, y, c = lax.axis_index("x"), lax.axis_index("y"), lax.axis_index("c")
    return x, y, c, 2 * x + y


def _chip_peer(x, y, d):
    return jnp.bitwise_xor(x, d >> 1), jnp.bitwise_xor(y, d & 1)


def _my_half(ref_rows, c):
    half = ref_rows // 2
    return pl.ds(c * half, half)


def _gather_copy(ref, part, c, sems, k, d, to):
    rows = _my_half(ref.shape[1], c)
    return pltpu.make_async_remote_copy(src_ref=ref.at[part, rows], dst_ref=ref.at[part, rows], send_sem=sems[0].at[k, d - 1],
                                        recv_sem=sems[1].at[k, d - 1], device_id=to, device_id_type=MESH)


def _gather_start(slabs, ici_sems):
    x, y, c, j = _place()
    for k, ref in enumerate(slabs):
        for d in (1, 2, 3):
            px, py = _chip_peer(x, y, d)
            _gather_copy(ref, j, c, ici_sems, k, d, (px, py, c)).start()


def _gather_finish(slabs, ici_sems, d2d_sems):
    x, y, c, j = _place()
    sib = (x, y, 1 - c)
    passed = []
    for d in (1, 2, 3):
        jd = jnp.bitwise_xor(j, d)
        for k, ref in enumerate(slabs):
            _gather_copy(ref, jd, c, ici_sems, k, d, sib).wait_recv()
            cp = _gather_copy(ref, jd, c, d2d_sems, k, d, sib)
            cp.start()
            passed.append(cp)
    for cp in passed:
        cp.wait_recv()
        cp.wait_send()
    for k, ref in enumerate(slabs):
        for d in (1, 2, 3):
            _gather_copy(ref, j, c, ici_sems, k, d, sib).wait_send()


def _gather_sems(n):
    return [pltpu.SemaphoreType.DMA((n, 3)) for _ in range(4)]


def gather_weights(slabs):
    n = len(slabs)

    def body(*refs):
        outs = refs[n:2 * n]
        sems = refs[2 * n:]
        _gather_start(outs, sems[0:2])
        _gather_finish(outs, sems[0:2], sems[2:4])

    return pl.pallas_call(
        body, name="gather_weights", in_specs=[ANY] * n, out_specs=[ANY] * n,
        out_shape=[jax.ShapeDtypeStruct(a.shape, a.dtype) for a in slabs],
        input_output_aliases={w: w for w in range(n)}, scratch_shapes=_gather_sems(n),
    )(*slabs)


def _own_slab(part, jidx):
    slab = lax.empty((N_CHIPS,) + part.shape, part.dtype)
    return lax.dynamic_update_slice_in_dim(slab, part[None], jidx, axis=0)


def _exchange_copy(g_ref, got_ref, k, sems):
    x, y, c, j = _place()
    half = g_ref.shape[1] // 2
    return pltpu.make_async_remote_copy(
        src_ref=g_ref.at[:, pl.ds((1 - c) * half, half)], dst_ref=got_ref, send_sem=sems[0].at[k],
        recv_sem=sems[1].at[k], device_id=(x, y, 1 - c), device_id_type=MESH)


def _exchange_start(g_refs, got_refs, sems):
    for k in range(len(g_refs)):
        _exchange_copy(g_refs[k], got_refs[k], k, sems).start()


def _exchange_finish(g_refs, got_refs, sems):
    for k in range(len(g_refs)):
        _exchange_copy(g_refs[k], got_refs[k], k, sems).wait()


def _exchange_sems(n):
    return [pltpu.SemaphoreType.DMA((n,)), pltpu.SemaphoreType.DMA((n,))]


def _exchange_shapes(gs):
    return [jax.ShapeDtypeStruct((g.shape[0], g.shape[1] // 2, g.shape[2]), F32) for g in gs]


def exchange_cores(gbig, name, small_all=None):
    n = len(gbig)
    with_small = small_all is not None

    def body(*refs):
        g_refs = refs[:n]
        got_refs = refs[n + with_small:2 * n + with_small]
        dsem, esem, ssem, rsem, fsem, hsem = refs[2 * (n + with_small):]
        x, y, c, j = _place()
        sib = (x, y, 1 - c)
        big = []
        for k in range(n):
            half = g_refs[k].shape[1] // 2
            cp = pltpu.make_async_remote_copy(
                src_ref=g_refs[k].at[:, pl.ds((1 - c) * half, half)], dst_ref=got_refs[k], send_sem=dsem.at[k],
                recv_sem=esem.at[k], device_id=sib, device_id_type=MESH)
            cp.start()
            big.append(cp)
        if with_small:
            all_ref = refs[2 * n + 1]
            me = 4 * x + 2 * y + c

            def small_copy(k, block, to, sems):
                return pltpu.make_async_remote_copy(
                    src_ref=all_ref.at[block], dst_ref=all_ref.at[block],
                    send_sem=sems[0].at[k], recv_sem=sems[1].at[k], device_id=to, device_id_type=MESH)

            first = [small_copy(0, me, sib, (ssem, rsem))]
            for d in (1, 2, 3):
                px, py = _chip_peer(x, y, d)
                first.append(small_copy(d, me, (px, py, c), (ssem, rsem)))
            for cp in first:
                cp.start()
            passed = []
            for d in (1, 2, 3):
                px, py = _chip_peer(x, y, d)
                src_block = 4 * px + 2 * py + c
                small_copy(d, src_block, sib, (ssem, rsem)).wait_recv()
                cp = small_copy(d - 1, src_block, sib, (fsem, hsem))
                cp.start()
                passed.append(cp)
            small_copy(0, me, sib, (ssem, rsem)).wait_recv()
            for cp in passed:
                cp.wait_recv()
            for cp in first + passed:
                cp.wait_send()
        for cp in big:
            cp.wait()

    ops = list(gbig) + ([small_all] if with_small else [])
    out_shape = [jax.ShapeDtypeStruct((g.shape[0], g.shape[1] // 2, g.shape[2]), F32) for g in gbig]
    aliases = {}
    if with_small:
        out_shape.append(jax.ShapeDtypeStruct(small_all.shape, F32))
        aliases = {n: n}
    return pl.pallas_call(
        body, name=name, in_specs=[ANY] * len(ops), out_specs=[ANY] * len(out_shape), out_shape=out_shape,
        input_output_aliases=aliases,
        scratch_shapes=[pltpu.SemaphoreType.DMA((n,)), pltpu.SemaphoreType.DMA((n,)),
                        pltpu.SemaphoreType.DMA((4,)), pltpu.SemaphoreType.DMA((4,)),
                        pltpu.SemaphoreType.DMA((3,)), pltpu.SemaphoreType.DMA((3,))],
    )(*ops)


def _scatter_copy(s_ref, got_ref, k, d, sems):
    x, y, c, j = _place()
    px, py = _chip_peer(x, y, d)
    return pltpu.make_async_remote_copy(
        src_ref=s_ref.at[jnp.bitwise_xor(j, d)], dst_ref=got_ref.at[d - 1], send_sem=sems[0].at[k, d - 1],
        recv_sem=sems[1].at[k, d - 1], device_id=(px, py, c), device_id_type=MESH)


def _scatter_start(s_refs, got_refs, sems):
    for d in (1, 2, 3):
        for k in range(len(s_refs)):
            _scatter_copy(s_refs[k], got_refs[k], k, d, sems).start()


def _scatter_finish(s_refs, got_refs, sems):
    for d in (1, 2, 3):
        for k in range(len(s_refs)):
            _scatter_copy(s_refs[k], got_refs[k], k, d, sems).wait()


def _scatter_sems(n):
    return [pltpu.SemaphoreType.DMA((n, 3)), pltpu.SemaphoreType.DMA((n, 3))]


def _scatter_shapes(s1):
    return [jax.ShapeDtypeStruct((3,) + a.shape[1:], a.dtype) for a in s1]


def scatter_shards(s1, name):
    n = len(s1)

    def body(*refs):
        _scatter_start(refs[:n], refs[n:2 * n], refs[2 * n:])
        _scatter_finish(refs[:n], refs[n:2 * n], refs[2 * n:])

    return pl.pallas_call(
        body, name=name, in_specs=[ANY] * n, out_specs=[ANY] * n, out_shape=_scatter_shapes(s1),
        scratch_shapes=_scatter_sems(n),
    )(*s1)


def share_with_sibling(parts):
    n = len(parts)

    def body(*refs):
        out_refs = refs[n:2 * n]
        ssem, rsem = refs[2 * n:]
        x, y, c, j = _place()
        cps = []
        for k in range(n):
            rows = _my_half(out_refs[k].shape[1], c)
            for l in range(DEPTH):
                cp = pltpu.make_async_remote_copy(
                    src_ref=out_refs[k].at[l, rows], dst_ref=out_refs[k].at[l, rows], send_sem=ssem.at[k, l],
                    recv_sem=rsem.at[k, l], device_id=(x, y, 1 - c), device_id_type=MESH)
                cp.start()
                cps.append(cp)
        for cp in cps:
            cp.wait()

    return pl.pallas_call(
        body, name="share_with_sibling", in_specs=[ANY] * n, out_specs=[ANY] * n,
        out_shape=[jax.ShapeDtypeStruct(a.shape, F32) for a in parts],
        input_output_aliases={k: k for k in range(n)},
        scratch_shapes=[pltpu.SemaphoreType.DMA((n, DEPTH)), pltpu.SemaphoreType.DMA((n, DEPTH))],
    )(*parts)


def add_core_halves(g, got, cidx, name):
    nch, r, cdim = g.shape
    half = r // 2
    tr = _row_tile(half, cdim, mult=16)
    per = half // tr

    def body(c_ref, a_ref, b_ref, o_ref):
        o_ref[...] = _bf(a_ref[...] + b_ref[...])

    grid_spec = pltpu.PrefetchScalarGridSpec(
        num_scalar_prefetch=1, grid=(nch, per),
        in_specs=[pl.BlockSpec((None, tr, cdim), lambda jj, i, c_ref: (jj, c_ref[0] * per + i, 0)),
                  pl.BlockSpec((None, tr, cdim), lambda jj, i, c_ref: (jj, i, 0))],
        out_specs=pl.BlockSpec((None, tr, cdim), lambda jj, i, c_ref: (jj, i, 0)))
    return pl.pallas_call(
        body, name=name, grid_spec=grid_spec,
        out_shape=jax.ShapeDtypeStruct((nch, half, cdim), BF16), compiler_params=_cparams(2),
    )(cidx, g, got)


def add_chip_parts(s1, got, jc, layer, into, name):
    _, half, cdim = s1.shape
    tr = _row_tile(half, cdim, mult=16)
    per = half // tr

    def body(jc_ref, a_ref, g0_ref, g1_ref, g2_ref, *rest):
        rest[-1][...] = ((a_ref[...].astype(F32) + g0_ref[...].astype(F32)) + g1_ref[...].astype(F32)) + g2_ref[...].astype(F32)

    def slot(k):
        return pl.BlockSpec((None, tr, cdim), lambda i, jc_ref: (k, i, 0))

    in_specs = [pl.BlockSpec((None, tr, cdim), lambda i, jc_ref: (jc_ref[0], i, 0)), slot(0), slot(1), slot(2)]
    ops = [jc, s1, got, got, got]
    aliases = {}
    if into is not None:
        in_specs.append(ANY)
        ops.append(into)
        aliases = {5: 0}
    grid_spec = pltpu.PrefetchScalarGridSpec(
        num_scalar_prefetch=1, grid=(per,), in_specs=in_specs,
        out_specs=pl.BlockSpec((None, tr, cdim), lambda i, jc_ref: (layer, jc_ref[1] * per + i, 0)))
    return pl.pallas_call(
        body, name=name, grid_spec=grid_spec, input_output_aliases=aliases,
        out_shape=jax.ShapeDtypeStruct((DEPTH, 2 * half, cdim), F32), compiler_params=_cparams(1),
    )(*ops)


def sum_devices(allp):
    _, r, _ = allp.shape

    def body(a_ref, o_ref):
        tot = a_ref[0]
        for k in range(1, 8):
            tot = tot + a_ref[k]
        o_ref[...] = tot

    tr = r // 2 if r % 16 == 0 else r
    return pl.pallas_call(
        body, name="sum_devices", grid=(r // tr,),
        in_specs=[pl.BlockSpec((8, tr, LANES), lambda i: (0, i, 0))],
        out_specs=pl.BlockSpec((tr, LANES), lambda i: (i, 0)),
        out_shape=jax.ShapeDtypeStruct((r, LANES), F32), compiler_params=_cparams(1),
    )(allp)


def _row_tile(r, cdim, limit_bytes=1 << 20, mult=8):
    best = None
    for tr in range(mult, r + 1, mult):
        if r % tr == 0 and tr * cdim * 4 <= limit_bytes:
            best = tr
    return best if best is not None else r


def adamw(w, g, m, v, name):
    r, cdim = w.shape
    tr = _row_tile(r, cdim)
    bc1 = 1.0 - ADAM_B1 ** ADAM_STEP
    bc2 = 1.0 - ADAM_B2 ** ADAM_STEP

    def body(w_ref, g_ref, m_ref, v_ref, d_ref, nm_ref, nv_ref, go_ref):
        gv = g_ref[...]
        nm = ADAM_B1 * m_ref[...] + (1.0 - ADAM_B1) * gv
        nv = ADAM_B2 * v_ref[...] + (1.0 - ADAM_B2) * (gv * gv)
        d_ref[...] = -ADAM_LR * ((nm / bc1) / (jnp.sqrt(nv / bc2) + ADAM_EPS) + ADAM_WD * w_ref[...])
        nm_ref[...] = nm
        nv_ref[...] = nv
        go_ref[...] = gv

    blk = pl.BlockSpec((tr, cdim), lambda i: (i, 0))
    shape = jax.ShapeDtypeStruct((r, cdim), F32)
    return pl.pallas_call(
        body, name=name, grid=(r // tr,), in_specs=[blk] * 4, out_specs=[blk] * 4, out_shape=[shape] * 4,
        compiler_params=_cparams(1),
    )(w, g, m, v)


def _s5_prepare(lam_re, lam_im, log_dt, b_re, b_im, c_re, c_im):
    groups, ch = 16, 16
    dt = jnp.exp(log_dt)[:, None]
    mag = jnp.exp(lam_re * dt)
    a_r, a_i = mag * jnp.cos(lam_im * dt), mag * jnp.sin(lam_im * dt)
    den = lam_re * lam_re + lam_im * lam_im
    q_r = ((a_r - 1.0) * lam_re + a_i * lam_im) / den
    q_i = (a_i * lam_re - (a_r - 1.0) * lam_im) / den
    bb_r = q_r[..., None] * b_re - q_i[..., None] * b_im
    bb_i = q_r[..., None] * b_im + q_i[..., None] * b_re
    eye = jnp.eye(groups, dtype=F32)

    def expand_b(bb):
        return jnp.einsum("gpc,gh->gchp", bb, eye).reshape(groups * ch, N_STATE)

    def expand_c(cc):
        return jnp.einsum("gcp,gh->hpgc", cc, eye).reshape(N_STATE, groups * ch)

    a2 = jnp.concatenate([a_r.reshape(1, N_STATE), a_i.reshape(1, N_STATE)], axis=1)
    bexp = jnp.concatenate([expand_b(bb_r), expand_b(bb_i)], axis=1)
    cexp = jnp.concatenate([expand_c(c_re), -expand_c(c_im)], axis=0)
    return a2, bexp, cexp


def _lru_prepare(w_r, w_i, lam):
    heads = 4
    eye = jnp.eye(heads, dtype=F32)

    def expand(w):
        return jnp.einsum("hij,hk->hikj", w, eye).reshape(D_GROUP, D_GROUP)

    return expand(w_r), expand(w_i), jax.nn.softplus(-lam).reshape(1, D_GROUP)


def _pad_rows(a, rows):
    return jnp.pad(a, ((0, rows - a.shape[0]), (0, 0)))


def _group_mean_matrix():
    gidx = jnp.arange(D_GROUP) // 64
    return (gidx[:, None] == gidx[None, :]).astype(BF16) * jnp.asarray(1.0 / 64.0, BF16)


def _pack_rows(arrs, width):
    parts = []
    for a in arrs:
        flat = a.reshape(-1)
        pad = (-flat.shape[0]) % width
        parts.append(jnp.pad(flat, (0, pad)) if pad else flat)
    flat = jnp.concatenate(parts)
    rows = flat.shape[0] // width
    pad_rows = (-rows) % 16
    if pad_rows:
        flat = jnp.pad(flat, (0, pad_rows * width))
    return flat.reshape(-1, width)


def _unpack_rows(packed, shapes, width):
    flat = packed.reshape(-1)
    out, off = [], 0
    for shp in shapes:
        size = math.prod(shp)
        out.append(flat[off:off + size].reshape(shp))
        off += size + ((-size) % width)
    return out


def kernel(x, mem, ln_in_g, ln_in_b, w_in, b_in, s5_lam_re, s5_lam_im, s5_log_dt, s5_b_re, s5_b_im, s5_c_re, s5_c_im, s5_d, s5_w_glu, s5_b_glu, cv_w, cv_b, cv_gn_g, cv_gn_b, cv_w_pw, cv_b_pw, lru_conv_w, lru_conv_b, lru_w_r, lru_b_r, lru_w_i, lru_b_i, lru_lam, attn_w_kv, w_out, b_out, ln1_g, ln1_b, ffn_w_up, ffn_conv_w, ffn_conv_b, ffn_w_down, ln2_g, ln2_b, loss_target, m_ln_in_g, m_ln_in_b, m_w_in, m_b_in, m_s5_lam_re, m_s5_lam_im, m_s5_log_dt, m_s5_b_re, m_s5_b_im, m_s5_c_re, m_s5_c_im, m_s5_d, m_s5_w_glu, m_s5_b_glu, m_cv_w, m_cv_b, m_cv_gn_g, m_cv_gn_b, m_cv_w_pw, m_cv_b_pw, m_lru_conv_w, m_lru_conv_b, m_lru_w_r, m_lru_b_r, m_lru_w_i, m_lru_b_i, m_lru_lam, m_attn_w_kv, m_w_out, m_b_out, m_ln1_g, m_ln1_b, m_ffn_w_up, m_ffn_conv_w, m_ffn_conv_b, m_ffn_w_down, m_ln2_g, m_ln2_b, v_ln_in_g, v_ln_in_b, v_w_in, v_b_in, v_s5_lam_re, v_s5_lam_im, v_s5_log_dt, v_s5_b_re, v_s5_b_im, v_s5_c_re, v_s5_c_im, v_s5_d, v_s5_w_glu, v_s5_b_glu, v_cv_w, v_cv_b, v_cv_gn_g, v_cv_gn_b, v_cv_w_pw, v_cv_b_pw, v_lru_conv_w, v_lru_conv_b, v_lru_w_r, v_lru_b_r, v_lru_w_i, v_lru_b_i, v_lru_lam, v_attn_w_kv, v_w_out, v_b_out, v_ln1_g, v_ln1_b, v_ffn_w_up, v_ffn_conv_w, v_ffn_conv_b, v_ffn_w_down, v_ln2_g, v_ln2_b):
    p = dict(locals())
    xs = x[0]
    mems = mem[0]
    target = loss_target[0]
    s = xs.shape[0]
    cidx = lax.axis_index("c")
    jidx = 2 * lax.axis_index("x") + lax.axis_index("y")
    tb_scan = min(512, s)
    tb_s5 = min(512, s)
    tb_attn = min(1024, s)
    tb_ffn = min(512, s)

    small_sh_names = list(SMALL_SHARDED)
    small_sh_shapes = [p[nm].shape[1:] for nm in small_sh_names]
    slabs = [[_own_slab(_bf(p[nm][l]), jidx) for nm in BIG]
             + [_own_slab(_pack_rows([p[nm][l] for nm in small_sh_names], LANES), jidx)] for l in range(DEPTH)]
    n_slabs = len(BIG) + 1
    first_needed = FIRST_NEEDED + [len(BIG)]
    arrive_later = [k for k in range(n_slabs) if k not in first_needed]
    gathered = [[None] * n_slabs for _ in range(DEPTH)]
    for k, slab in zip(first_needed, gather_weights([slabs[0][k] for k in first_needed])):
        gathered[0][k] = slab

    def weight_views(gw, which):
        shapes = {0: (1, N_CHIPS, D_MODEL, N_IN // N_CHIPS), 1: (1, 1, D_MODEL, 2 * D_GROUP), 2: (1, 1, D_MODEL, D_MODEL),
                  3: (1, N_CHIPS, D_MODEL, 2 * D_FF // N_CHIPS), 4: (1, 1, D_FF, D_MODEL),
                  5: (1, D_GROUP, D_GROUP), 6: (1, D_GROUP, D_GROUP)}
        views = {BIG[k]: gw[k].reshape(shapes[k]) for k in which if k < len(BIG)}
        if len(BIG) in which:
            per_chip = [_unpack_rows(gw[len(BIG)][jj], small_sh_shapes, LANES) for jj in range(N_CHIPS)]
            for k, nm in enumerate(small_sh_names):
                views[nm] = jnp.concatenate([per_chip[jj][k] for jj in range(N_CHIPS)], axis=SMALL_SHARDED[nm] - 1)
        return views

    pmat = _group_mean_matrix()

    def vec(a):
        return a.reshape(1, -1)

    xh0, rs0, xb0 = ln_fwd(xs, vec(ln_in_g), vec(ln_in_b), "ln_in")
    saved = []
    prev = dict(xh=xh0, rs=rs0, xb=xb0, g=vec(ln_in_g), b=vec(ln_in_b))
    for l in range(DEPTH):
        sv = dict(prev=prev)
        (a2, bexp, cexp), sv['s5_vjp'] = jax.vjp(_s5_prepare, s5_lam_re[l], s5_lam_im[l], s5_log_dt[l],
                                                 s5_b_re[l], s5_b_im[l], s5_c_re[l], s5_c_im[l])
        (wr, wi, sp), sv['lru_vjp'] = jax.vjp(_lru_prepare, lru_w_r[l], lru_w_i[l], lru_lam[l])
        sv.update(a2=a2, bexp=_bf(bexp), cexp=_bf(cexp), wr=_bf(wr), wi=_bf(wi), sp=sp)
        late = arrive_later if l == 0 else []
        gw = sv['gw'] = weight_views(gathered[l], [k for k in range(n_slabs) if k not in late])
        sv['cvw'] = _pad_rows(gw['cv_w'], CV_HALO)
        sv['lcw'] = _pad_rows(gw['lru_conv_w'], LRU_HALO)
        sv['fcw'] = _pad_rows(gw['ffn_conv_w'], FFN_TAP_ROWS)
        sv['w_glu'], sv['w_pw'] = gw['s5_w_glu'], gw['cv_w_pw']
        h_in = mm_nn(prev['xb'], gw['w_in'], 0, vec(b_in[l]), F32, 2048, f"in_proj{l}")
        kv = mm_nn(mems, gw['attn_w_kv'], 0, jnp.zeros((1, 2 * D_GROUP), F32), F32, 256, f"kv_proj{l}")
        y_s5, hst, y0, *got = s5_fwd(h_in, a2, sv['bexp'], sv['cexp'], vec(s5_d[l]), sv['w_glu'], 0, vec(s5_b_glu[l]),
                                     tb_s5, f"s5_fwd{l}", gather=[slabs[l][k] for k in late])
        for k, slab in zip(late, got):
            gathered[l][k] = slab
        gw.update(weight_views(gathered[l], late))
        y_cv, hc = cv_fwd(h_in, sv['cvw'], vec(cv_b[l]), vec(cv_gn_g[l]), vec(cv_gn_b[l]), pmat, sv['w_pw'], 0,
                          vec(cv_b_pw[l]), tb_scan, f"cv_fwd{l}")
        y_lru, xcs, hls = lru_fwd(h_in, sv['lcw'], vec(lru_conv_b[l]), sv['wr'], vec(lru_b_r[l]), sv['wi'],
                                  vec(lru_b_i[l]), sp, tb_scan, f"lru_fwd{l}")
        y_mem = attn_fwd(h_in, kv, tb_attn, f"attn_fwd{l}")
        mix_in = jnp.concatenate([y_s5, y_cv, y_lru, y_mem], axis=1)
        xh1, rs1, xb1 = proj_ln(mix_in, gw['w_out'].reshape(1, D_MODEL, D_MODEL), 0, vec(b_out[l]),
                                prev['xh'], prev['g'], prev['b'], vec(ln1_g[l]), vec(ln1_b[l]), f"out_proj_ln{l}")
        u = mm_nn(xb1, gw['ffn_w_up'], 0, jnp.zeros((1, 2 * D_FF), F32), F32, 1024, f"ffn_up{l}")
        nxt = slabs[l + 1] if l + 1 < DEPTH else ()
        hf, sv['ffn_kept'], *got = ffn_act_fwd(u, sv['fcw'], vec(ffn_conv_b[l]), tb_ffn, f"ffn_act{l}", gather=nxt)
        if nxt:
            gathered[l + 1] = got
        xh2, rs2, xb2 = proj_ln(hf, gw['ffn_w_down'].reshape(1, D_FF, D_MODEL), 0, jnp.zeros((1, D_MODEL), F32),
                                xh1, vec(ln1_g[l]), vec(ln1_b[l]), vec(ln2_g[l]), vec(ln2_b[l]), f"ffn_down_ln{l}")
        sv.update(h_in=h_in, kv=kv, hst=hst, y0=y0, hc=hc, xcs=xcs, hls=hls, mix_in=mix_in,
                  xh1=xh1, rs1=rs1, xb1=xb1, u=u, hf=hf, xh2=xh2, rs2=rs2)
        saved.append(sv)
        prev = dict(xh=xh2, rs=rs2, xb=xb2, g=vec(ln2_g[l]), b=vec(ln2_b[l]))

    grads = {}
    per_layer = {nm: [None] * DEPTH for nm in WEIGHTS if nm not in ('ln_in_g', 'ln_in_b')}
    c1 = cidx.reshape(1).astype(jnp.int32)
    jc = jnp.stack([jidx, cidx]).astype(jnp.int32)
    red_big = [None] * len(BIG)
    handoff = None
    pending = None
    below = None

    def per_chip(gl, which):
        return [gl[k].reshape((N_CHIPS,) + p[BIG[k]].shape[1:]) for k in which]

    def core_sums(gs, got, which, tag):
        return [add_core_halves(g, ga, c1, f"add_cores_{BIG[k]}{tag}") for g, ga, k in zip(gs, got, which)]

    def chip_parts(gl, which, tag, small_all=None):
        gs = per_chip(gl, which)
        got = exchange_cores(gs, f"exchange_cores{tag}", small_all)
        return core_sums(gs, got, which, tag), (got[len(which)] if small_all is not None else None)

    def own_sum(which, parts, got, l):
        for k, part, gk in zip(which, parts, got):
            red_big[k] = add_chip_parts(part, gk, jc, l, red_big[k], f"add_chips_{BIG[k]}{l}")

    everything = list(range(len(BIG)))
    ready_early = [2, 3, 4]
    ready_last = [k for k in everything if k not in ready_early]

    for l in reversed(range(DEPTH)):
        sv = saved[l]
        pv = sv['prev']
        gw = sv['gw']
        gl = [None] * len(BIG)
        if l == DEPTH - 1:
            dr2, dg2, db2, sqerr = loss_ln_bwd(target, sv['xh2'], sv['rs2'], vec(ln2_g[l]), vec(ln2_b[l]), "loss_ln2_bwd")
            loss_local = 0.5 / D_MODEL * jnp.sum(sqerr)
        else:
            dr2, dg2, db2 = below
        per_layer['ln2_g'][l], per_layer['ln2_b'][l] = dg2[0], db2[0]
        tm_nt = min(512, s)
        ts_big = min(2048, s)
        whole = lambda a_ref, j: a_ref[...]
        dhf = mm_nt(dr2, (tm_nt, D_MODEL), lambda i: (i, 0), whole, gw['ffn_w_down'], 0, None, F32, 512, s, f"ffn_down_dx{l}")
        above = per_chip(*handoff) if handoff is not None else []
        res = mm_tn(sv['hf'], dr2, (min(1024, s), D_MODEL), lambda kt, j, st: (st, 0), 1, D_MODEL, FFN_TN, 1024, s,
                    f"ffn_down_dw{l}", exchange=above)
        gl[4] = res[0] if above else res
        if handoff is not None:
            pending = (l + 1, handoff[1], core_sums(above, res[1:], handoff[1], str(l + 1)))
            handoff = None
        waiting = pending[2] if pending is not None else ()
        du, dcwv, dcwg, *got = ffn_act_bwd(dhf, sv['u'], sv['ffn_kept'], sv['fcw'], tb_ffn, f"ffn_act_bwd{l}",
                                           scatter=waiting)
        if pending is not None:
            own_sum(pending[1], pending[2], got, pending[0])
            pending = None
        dcw = jnp.concatenate([dcwv, dcwg], axis=1)
        per_layer['ffn_conv_w'][l] = dcw[0:FFN_CONV_WIDTH]
        per_layer['ffn_conv_b'][l] = dcw[FFN_CONV_WIDTH]
        dr1, dg1, db1, cs1 = mm_nt(du, (2, tm_nt, D_FF), lambda i: (0, i, 0),
                                   lambda a_ref, j: a_ref[j // 2, :, (j % 2) * FFN_TN:(j % 2 + 1) * FFN_TN], gw['ffn_w_up'], 0, dr2,
                                   F32, 512, s, f"ffn_up_dx_ln1_bwd{l}", ln=(sv['xh1'], sv['rs1'], vec(ln1_g[l])))
        gl[3] = mm_tn(sv['xb1'], du, (None, ts_big, FFN_TN), lambda kt, j, st: (j // 2, st, j % 2), N_CHIPS, FFN_TN,
                      D_MODEL, 2048, s, f"ffn_up_dw{l}")
        per_layer['ln1_g'][l], per_layer['ln1_b'][l], per_layer['b_out'][l] = dg1[0], db1[0], cs1[0]
        dmix = mm_nt(dr1, (tm_nt, D_MODEL), lambda i: (i, 0), whole, gw['w_out'], 0, None, F32, 512, s, f"out_proj_dx{l}")
        gl[2] = mm_tn(sv['mix_in'], dr1, (min(1024, s), D_MODEL), lambda kt, j, st: (st, 0), 1, D_MODEL, D_MODEL, 1024, s,
                      f"out_proj_dw{l}")
        h_in = sv['h_in']
        early_gs = per_chip(gl, ready_early) if l == 0 else []
        d_q, cs_q, d_kv, *got = attn_bwd(dmix, h_in, sv['kv'], tb_attn, f"attn_bwd{l}", exchange=early_gs)
        early = core_sums(early_gs, got, ready_early, f"{l}a") if l == 0 else ()
        (d_u, cs_u, d_bexp, d_cexp, d_dd, d_wglu, d_bglu, d_a2) = s5_bwd(
            dmix, h_in, sv['y0'], sv['hst'], sv['a2'], sv['bexp'], sv['cexp'], vec(s5_d[l]), sv['w_glu'], 0,
            vec(s5_b_glu[l]), tb_s5, f"s5_bwd{l}")
        (d_vg, cs_vg, d_cvw, d_cvb, d_gg, d_gb, d_wpw, d_bpw, *got) = cv_bwd(
            dmix, h_in, sv['hc'], sv['cvw'], vec(cv_gn_g[l]), vec(cv_gn_b[l]), pmat, sv['w_pw'], 0, tb_scan, f"cv_bwd{l}",
            scatter=early)
        if l == 0:
            own_sum(ready_early, early, got, l)
        (d_lx, cs_lx, d_lcw, d_lcb, d_wr, d_br, d_wi, d_bi, d_sp) = lru_bwd(
            dmix, h_in, sv['xcs'], sv['hls'], sv['lcw'], sv['wr'], vec(lru_b_r[l]), sv['wi'], vec(lru_b_i[l]),
            sv['sp'], tb_scan, f"lru_bwd{l}")
        g_s5 = sv['s5_vjp']((d_a2, d_bexp, d_cexp))
        for nm, gval in zip(['s5_lam_re', 's5_lam_im', 's5_log_dt', 's5_b_re', 's5_b_im', 's5_c_re', 's5_c_im'], g_s5):
            per_layer[nm][l] = gval
        g_lru = sv['lru_vjp']((d_wr, d_wi, d_sp))
        for nm, gval in zip(['lru_w_r', 'lru_w_i', 'lru_lam'], g_lru):
            per_layer[nm][l] = gval
        per_layer['s5_d'][l], per_layer['s5_b_glu'][l] = d_dd[0], d_bglu[0]
        per_layer['cv_w'][l], per_layer['cv_b'][l] = d_cvw[0:CONV_WIDTH], d_cvb[0]
        per_layer['cv_gn_g'][l], per_layer['cv_gn_b'][l] = d_gg[0], d_gb[0]
        per_layer['cv_b_pw'][l] = d_bpw[0]
        gl[5], gl[6] = d_wglu, d_wpw
        per_layer['lru_conv_w'][l], per_layer['lru_conv_b'][l] = d_lcw[0:LRU_CONV_WIDTH], d_lcb[0]
        per_layer['lru_b_r'][l], per_layer['lru_b_i'][l] = d_br[0], d_bi[0]
        per_layer['b_in'][l] = jnp.concatenate([cs_u, cs_vg, cs_lx, cs_q], axis=1)[0]
        gl[1] = mm_tn(mems, d_kv, (MEM_ROWS, 2 * D_GROUP), lambda kt, j, st: (st, 0), 1, 2 * D_GROUP, D_MODEL, MEM_ROWS,
                      MEM_ROWS, f"kv_proj_dw{l}")
        dh_in = jnp.concatenate([d_u, d_vg, d_lx, d_q], axis=1)
        n_sh = N_IN // N_CHIPS
        below = mm_nt(dh_in, (tm_nt, N_IN), lambda i: (i, 0), lambda a_ref, j: a_ref[:, j * n_sh:(j + 1) * n_sh],
                      gw['w_in'], 0, dr1, F32, 512, s, f"in_proj_dx_ln_bwd{l}", ln=(pv['xh'], pv['rs'], pv['g']))[:3]
        gl[0] = mm_tn(pv['xb'], dh_in, (ts_big, n_sh), lambda kt, j, st: (st, j), N_CHIPS, n_sh, D_MODEL, 2048, s,
                      f"in_proj_dw{l}")
        if l > 0:
            handoff = (gl, everything)
    grad_x, dg_in, db_in = below
    grads['ln_in_g'], grads['ln_in_b'] = dg_in[0], db_in[0]
    for nm, vals in per_layer.items():
        if nm not in BIG:
            grads[nm] = jnp.stack(vals)

    small_names = [nm for nm in WEIGHTS if nm not in BIG]
    small_local = _pack_rows([grads[nm] for nm in small_names], LANES)
    me = 4 * lax.axis_index("x") + 2 * lax.axis_index("y") + cidx
    small_slab = lax.dynamic_update_slice_in_dim(lax.empty((8,) + small_local.shape, F32), small_local[None], me, axis=0)
    parts, small_all = chip_parts(gl, ready_last, "0b", small_slab)
    own_sum(ready_last, parts, scatter_shards(parts, "scatter_shards0"), 0)
    red_big = share_with_sibling(red_big)
    small_red = sum_devices(small_all)
    small_grads = dict(zip(small_names, _unpack_rows(small_red, [grads[nm].shape for nm in small_names], LANES)))

    out_g, out_d, out_m, out_v = {}, {}, {}, {}
    for k, nm in enumerate(BIG):
        gk = red_big[k]
        two_d = (-1, p[nm].shape[-1])
        res = adamw(p[nm].reshape(two_d), gk.reshape(two_d), p['m_' + nm].reshape(two_d),
                    p['v_' + nm].reshape(two_d), f"adamw_{nm}")
        out_d[nm], out_m[nm], out_v[nm], out_g[nm] = (t.reshape(p[nm].shape) for t in res)
    own = {}
    for nm in small_names:
        gfull = small_grads[nm]
        if nm in SMALL_SHARDED:
            ax = SMALL_SHARDED[nm]
            width = p[nm].shape[ax]
            gfull = lax.dynamic_slice_in_dim(gfull, jidx * width, width, axis=ax)
        own[nm] = gfull
    packs = [_pack_rows([src[nm] for nm in small_names], LANES)
             for src in (dict((nm, p[nm]) for nm in small_names), own,
                         dict((nm, p['m_' + nm]) for nm in small_names), dict((nm, p['v_' + nm]) for nm in small_names))]
    dlt, nm_, nv_, _ = adamw(packs[0], packs[1], packs[2], packs[3], "adamw_small")
    shapes = [p[nm].shape for nm in small_names]
    for dst, packed in ((out_d, dlt), (out_m, nm_), (out_v, nv_)):
        dst.update(zip(small_names, _unpack_rows(packed, shapes, LANES)))
    out_g.update(own)

    loss = lax.psum(loss_local, ("x", "y", "c"))
    return (loss, grad_x[None], *[out_g[nm] for nm in WEIGHTS], *[out_d[nm] for nm in WEIGHTS],
            *[out_m[nm] for nm in WEIGHTS], *[out_v[nm] for nm in WEIGHTS])
```

```python
import functools
import math

import jax
import jax.numpy as jnp
from jax import lax
from jax.experimental import pallas as pl
from jax.experimental.pallas import tpu as pltpu

F32 = jnp.float32
BF16 = jnp.bfloat16
MESH = pl.DeviceIdType.MESH
ANY = pl.BlockSpec(memory_space=pl.ANY)

DEPTH = 2
D_MODEL = 1024
D_GROUP = 256
N_IN = 6 * D_GROUP
D_FF = 2816
N_STATE = 1024
CONV_WIDTH = 31
LRU_CONV_WIDTH = 4
FFN_CONV_WIDTH = 3
LRU_C = 8.0
ALPHA = (2 * DEPTH) ** 0.25
LN_EPS = 1e-5
N_CHIPS = 4
MEM_ROWS = 256
LANES = 128
SUBLANES = 8
VMEM_LIMIT = 56 * 1024 * 1024

ADAM_LR, ADAM_B1, ADAM_B2, ADAM_EPS, ADAM_WD, ADAM_STEP = 0.001, 0.9, 0.999, 1e-08, 0.01, 10

WEIGHTS = ['ln_in_g', 'ln_in_b', 'w_in', 'b_in', 's5_lam_re', 's5_lam_im', 's5_log_dt', 's5_b_re', 's5_b_im',
           's5_c_re', 's5_c_im', 's5_d', 's5_w_glu', 's5_b_glu', 'cv_w', 'cv_b', 'cv_gn_g', 'cv_gn_b', 'cv_w_pw',
           'cv_b_pw', 'lru_conv_w', 'lru_conv_b', 'lru_w_r', 'lru_b_r', 'lru_w_i', 'lru_b_i', 'lru_lam',
           'attn_w_kv', 'w_out', 'b_out', 'ln1_g', 'ln1_b', 'ffn_w_up', 'ffn_conv_w', 'ffn_conv_b', 'ffn_w_down',
           'ln2_g', 'ln2_b']
BIG = ['w_in', 'attn_w_kv', 'w_out', 'ffn_w_up', 'ffn_w_down', 's5_w_glu', 'cv_w_pw']
FIRST_NEEDED = [0, 1, 5, 6]
SMALL_SHARDED = {'cv_w': 2, 'lru_conv_w': 2, 'ffn_conv_w': 2}


def _cparams(n_axes):
    return pltpu.CompilerParams(dimension_semantics=("arbitrary",) * n_axes, vmem_limit_bytes=VMEM_LIMIT)


def _dot(a, b):
    return jnp.dot(a, b, preferred_element_type=F32)


def _dot_nt(a, b):
    return lax.dot_general(a, b, (((1,), (1,)), ((), ())), preferred_element_type=F32)


def _dot_tn(a, b):
    return lax.dot_general(a, b, (((0,), (0,)), ((), ())), preferred_element_type=F32)


def _bf(v):
    return v.astype(BF16)


def _colsum(v):
    return jnp.sum(v, axis=0, keepdims=True)


def _dot3(v, p):
    hi = _bf(v)
    r1 = v - hi.astype(F32)
    mid = _bf(r1)
    lo = _bf(r1 - mid.astype(F32))
    return _dot(hi, p) + _dot(mid, p) + _dot(lo, p)


_GELU_C = math.sqrt(2.0 / math.pi)


_GELU_C3 = _GELU_C * 0.044715


def _gelu_parts(v):
    t = jnp.tanh(v * (_GELU_C + _GELU_C3 * (v * v)))
    hv = 0.5 * v
    return hv + hv * t, t


def _gelu(v):
    return _gelu_parts(v)[0]


def _gelu_grad(v, t):
    return (0.5 + 0.5 * t) + (0.5 * v) * (1.0 - t * t) * (_GELU_C + (3.0 * _GELU_C3) * (v * v))


def _sigmoid(v):
    return 1.0 / (1.0 + jnp.exp(-v))


def _acc(ref, val, first):
    @pl.when(first)
    def _():
        ref[...] = val

    @pl.when(jnp.logical_not(first))
    def _():
        ref[...] += val


def _rows(shape):
    return lax.broadcasted_iota(jnp.int32, shape, 0)


def _ln_rows(r):
    mu = jnp.mean(r, -1, keepdims=True)
    rc = r - mu
    var = jnp.mean(rc * rc, -1, keepdims=True)
    rs = lax.rsqrt(var + LN_EPS)
    return rc * rs, rs


def ln_fwd(x, g, b, name):
    s = x.shape[0]
    tm = min(512, s)

    def body(x_ref, g_ref, b_ref, xh_ref, rs_ref, xb_ref):
        xh, rs = _ln_rows(x_ref[...])
        xh_ref[...] = xh
        rs_ref[...] = rs
        xb_ref[...] = _bf(xh * g_ref[...] + b_ref[...])

    row = pl.BlockSpec((tm, D_MODEL), lambda i: (i, 0))
    vec = pl.BlockSpec((1, D_MODEL), lambda i: (0, 0))
    return pl.pallas_call(
        body, name=name, grid=(s // tm,),
        in_specs=[row, vec, vec],
        out_specs=[row, pl.BlockSpec((tm, 1), lambda i: (i, 0)), row],
        out_shape=[jax.ShapeDtypeStruct((s, D_MODEL), F32), jax.ShapeDtypeStruct((s, 1), F32),
                   jax.ShapeDtypeStruct((s, D_MODEL), BF16)],
        compiler_params=_cparams(1),
    )(x, g, b)


def proj_ln(a, w, layer, bias, xh_prev, g_prev, b_prev, g, b, name):
    s, k = a.shape
    tm = min(512, s)

    def body(a_ref, w_ref, bias_ref, xp_ref, gp_ref, bp_ref, g_ref, b_ref, xh_ref, rs_ref, xb_ref):
        acc = _dot(a_ref[...], w_ref[...]) + bias_ref[...]
        r = ALPHA * (xp_ref[...] * gp_ref[...] + bp_ref[...]) + acc
        xh, rs = _ln_rows(r)
        xh_ref[...] = xh
        rs_ref[...] = rs
        xb_ref[...] = _bf(xh * g_ref[...] + b_ref[...])

    row = pl.BlockSpec((tm, D_MODEL), lambda i: (i, 0))
    vec = pl.BlockSpec((1, D_MODEL), lambda i: (0, 0))
    return pl.pallas_call(
        body, name=name, grid=(s // tm,),
        in_specs=[pl.BlockSpec((tm, k), lambda i: (i, 0)),
                  pl.BlockSpec((None, k, D_MODEL), lambda i: (layer, 0, 0)),
                  vec, row, vec, vec, vec, vec],
        out_specs=[row, pl.BlockSpec((tm, 1), lambda i: (i, 0)), row],
        out_shape=[jax.ShapeDtypeStruct((s, D_MODEL), F32), jax.ShapeDtypeStruct((s, 1), F32),
                   jax.ShapeDtypeStruct((s, D_MODEL), BF16)],
        compiler_params=_cparams(1),
    )(a, w, bias, xh_prev, g_prev, b_prev, g, b)


def loss_ln_bwd(target, xh, rs, g, b, name):
    s = xh.shape[0]
    tm = min(512, s)

    def body(t_ref, xh_ref, rs_ref, g_ref, b_ref, dr_ref, dg_ref, db_ref, sq_ref, drb_ref):
        first = pl.program_id(0) == 0
        xhv = xh_ref[...]
        err = xhv * g_ref[...] + b_ref[...] - t_ref[...]
        dyv = err * (1.0 / D_MODEL)
        dxh = dyv * g_ref[...]
        dr = rs_ref[...] * (dxh - jnp.mean(dxh, -1, keepdims=True) - xhv * jnp.mean(dxh * xhv, -1, keepdims=True))
        dr_ref[...] = dr
        drb_ref[...] = _bf(dr)
        _acc(dg_ref, _colsum(dyv * xhv), first)
        _acc(db_ref, _colsum(dyv), first)
        _acc(sq_ref, _colsum(err * err), first)

    row = pl.BlockSpec((tm, D_MODEL), lambda i: (i, 0))
    vec = pl.BlockSpec((1, D_MODEL), lambda i: (0, 0))
    vshape = jax.ShapeDtypeStruct((1, D_MODEL), F32)
    return pl.pallas_call(
        body, name=name, grid=(s // tm,),
        in_specs=[row, row, pl.BlockSpec((tm, 1), lambda i: (i, 0)), vec, vec],
        out_specs=[row, vec, vec, vec, row],
        out_shape=[jax.ShapeDtypeStruct((s, D_MODEL), F32), vshape, vshape, vshape,
                   jax.ShapeDtypeStruct((s, D_MODEL), BF16)],
        compiler_params=_cparams(1),
    )(target, xh, rs, g, b)


def mm_nn(a, w, layer, bias, out_dtype, tm, name):
    m, k = a.shape
    _, nj, _, n = w.shape
    tm = min(tm, m)

    def body(a_ref, w_ref, b_ref, o_ref):
        o_ref[...] = (_dot(_bf(a_ref[...]), w_ref[...]) + b_ref[...]).astype(out_dtype)

    return pl.pallas_call(
        body, name=name, grid=(nj, m // tm),
        in_specs=[pl.BlockSpec((tm, k), lambda j, i: (i, 0)),
                  pl.BlockSpec((None, None, k, n), lambda j, i: (layer, j, 0, 0)),
                  pl.BlockSpec((1, n), lambda j, i: (0, j))],
        out_specs=pl.BlockSpec((tm, n), lambda j, i: (i, j)),
        out_shape=jax.ShapeDtypeStruct((m, nj * n), out_dtype),
        compiler_params=_cparams(2),
    )(a, w, bias)


def mm_nt(a, a_block, a_map, pick, w, layer, add, out_dtype, tm, m, name, ln=None):
    _, nj, r, n = w.shape
    tm = min(tm, m)
    has_add = add is not None
    n_in = 2 + has_add + (3 if ln is not None else 0)

    def body(*refs):
        a_ref, w_ref = refs[0], refs[1]
        res = _dot_nt(_bf(pick(a_ref, 0)), w_ref[0])
        for j in range(1, nj):
            res = res + _dot_nt(_bf(pick(a_ref, j)), w_ref[j])
        if has_add:
            res = res + ALPHA * refs[2][...]
        if ln is None:
            refs[n_in][...] = res.astype(out_dtype)
            return
        xh_ref, rs_ref, g_ref = refs[n_in - 3:n_in]
        dr_ref, dg_ref, db_ref, cs_ref, drb_ref = refs[n_in:]
        first = pl.program_id(0) == 0
        xhv = xh_ref[...]
        dxh = res * g_ref[...]
        dr = rs_ref[...] * (dxh - jnp.mean(dxh, -1, keepdims=True) - xhv * jnp.mean(dxh * xhv, -1, keepdims=True))
        dr_ref[...] = dr
        drb_ref[...] = _bf(dr)
        _acc(dg_ref, _colsum(res * xhv), first)
        _acc(db_ref, _colsum(res), first)
        _acc(cs_ref, _colsum(dr), first)

    row = pl.BlockSpec((tm, r), lambda i: (i, 0))
    in_specs = [pl.BlockSpec(a_block, a_map),
                pl.BlockSpec((None, nj, r, n), lambda i: (layer, 0, 0, 0))]
    ops = [a, w]
    if has_add:
        in_specs.append(row)
        ops.append(add)
    if ln is None:
        out_specs, out_shape = row, jax.ShapeDtypeStruct((m, r), out_dtype)
    else:
        vec = pl.BlockSpec((1, r), lambda i: (0, 0))
        vshape = jax.ShapeDtypeStruct((1, r), F32)
        in_specs += [row, pl.BlockSpec((tm, 1), lambda i: (i, 0)), vec]
        ops += list(ln)
        out_specs = [row, vec, vec, vec, row]
        out_shape = [jax.ShapeDtypeStruct((m, r), F32), vshape, vshape, vshape, jax.ShapeDtypeStruct((m, r), BF16)]
    return pl.pallas_call(
        body, name=name, grid=(m // tm,),
        in_specs=in_specs, out_specs=out_specs, out_shape=out_shape,
        compiler_params=_cparams(1),
    )(*ops)


def mm_tn(a, b, b_block, b_map, nj, n, tk, ts, s, name, exchange=()):
    kx = a.shape[1]
    ts = min(ts, s)
    ne = len(exchange)
    grid = (kx // tk, nj, s // ts)

    def body(a_ref, b_ref, *rest):
        g_refs, o_ref, got_refs, sems = rest[:ne], rest[ne], rest[ne + 1:2 * ne + 1], rest[2 * ne + 1:]
        pids = [pl.program_id(ax) for ax in range(3)]
        if ne:
            @pl.when((pids[0] == 0) & (pids[1] == 0) & (pids[2] == 0))
            def _():
                _exchange_start(g_refs, got_refs, sems)

        part = _dot_tn(_bf(a_ref[...]), _bf(b_ref[...]))
        _acc(o_ref, part, pids[2] == 0)
        if ne:
            @pl.when((pids[0] == grid[0] - 1) & (pids[1] == grid[1] - 1) & (pids[2] == grid[2] - 1))
            def _():
                _exchange_finish(g_refs, got_refs, sems)

    res = pl.pallas_call(
        body, name=name, grid=grid,
        in_specs=[pl.BlockSpec((ts, tk), lambda kt, j, st: (st, kt)), pl.BlockSpec(b_block, b_map)] + [ANY] * ne,
        out_specs=[pl.BlockSpec((None, tk, n), lambda kt, j, st: (j, kt, 0))] + [ANY] * ne,
        out_shape=[jax.ShapeDtypeStruct((nj, kx, n), F32)] + _exchange_shapes(exchange),
        scratch_shapes=_exchange_sems(ne) if ne else [],
        compiler_params=_cparams(3),
    )(a, b, *exchange)
    return res if ne else res[0]


S5_TAB_ROWS = 8 * SUBLANES


def _s5_scan_table(tab_ref, ar, ai, reverse):
    n = N_STATE
    row = _rows((SUBLANES, n))
    edge = SUBLANES - 1 if reverse else 0
    tab_ref[0:8, :] = jnp.where(row == edge, ar, 0.0)
    tab_ref[8:16, :] = jnp.where(row == edge, ai, 0.0)
    pr, pi = ar, ai
    for step, k in enumerate((1, 2, 4)):
        mask = row < SUBLANES - k if reverse else row >= k
        tab_ref[16 + 16 * step:24 + 16 * step, :] = jnp.where(mask, pr, 0.0)
        tab_ref[24 + 16 * step:32 + 16 * step, :] = jnp.where(mask, pi, 0.0)
        pr, pi = pr * pr - pi * pi, 2.0 * pr * pi


def _s5_scan(src_ref, dst_ref, tab_ref, edge_ref, tb, reverse, per_tile=None):
    n = N_STATE
    ng = tb // SUBLANES
    nq = n // LANES
    link = SUBLANES - 1 if reverse else 1

    def tile(ii, carry):
        g = ng - 1 - ii if reverse else ii
        rows = pl.ds(pl.multiple_of(g * SUBLANES, SUBLANES), SUBLANES)
        out = []
        for q in range(nq):
            cre = slice(q * LANES, (q + 1) * LANES)
            cim = slice(n + q * LANES, n + (q + 1) * LANES)
            lr, li = src_ref[rows, cre], src_ref[rows, cim]
            tr, ti = pltpu.roll(carry[2 * q], link, 0), pltpu.roll(carry[2 * q + 1], link, 0)
            kr, ki = tab_ref[0:8, cre], tab_ref[8:16, cre]
            lr, li = lr + kr * tr - ki * ti, li + kr * ti + ki * tr
            for step, k in enumerate((1, 2, 4)):
                amt = SUBLANES - k if reverse else k
                kr, ki = tab_ref[16 + 16 * step:24 + 16 * step, cre], tab_ref[24 + 16 * step:32 + 16 * step, cre]
                sr, si = pltpu.roll(lr, amt, 0), pltpu.roll(li, amt, 0)
                lr, li = lr + kr * sr - ki * si, li + kr * si + ki * sr
            dst_ref[rows, cre] = lr
            dst_ref[rows, cim] = li
            if per_tile is not None:
                per_tile(g, q, (cre, cim), lr, li)
            out += [lr, li]
        return tuple(out)

    init = []
    for q in range(nq):
        init += [edge_ref[:, q * LANES:(q + 1) * LANES], edge_ref[:, n + q * LANES:n + (q + 1) * LANES]]
    fin = lax.fori_loop(0, ng, tile, tuple(init))
    for q in range(nq):
        edge_ref[:, q * LANES:(q + 1) * LANES] = fin[2 * q]
        edge_ref[:, n + q * LANES:n + (q + 1) * LANES] = fin[2 * q + 1]


def s5_fwd(h_in, a2, bexp, cexp, dskip, wglu, layer, bglu, tb, name, gather=()):
    s = h_in.shape[0]
    n = N_STATE
    ng = len(gather)

    def body(u_ref, a_ref, b_ref, c_ref, d_ref, w_ref, bg_ref, *rest):
        y_ref, h_ref, y0_ref = rest[ng:ng + 3]
        slabs = rest[ng + 3:2 * ng + 3]
        edge, tab, bu_ref = rest[2 * ng + 3:2 * ng + 6]
        sems = rest[2 * ng + 6:]

        @pl.when(pl.program_id(0) == 0)
        def _():
            if ng:
                _gather_start(slabs, sems[0:2])
            edge[...] = jnp.zeros_like(edge)
            _s5_scan_table(tab, a_ref[0:1, 0:n], a_ref[0:1, n:2 * n], False)

        u = u_ref[...]
        bu_ref[...] = _dot(_bf(u), b_ref[...])
        _s5_scan(bu_ref, h_ref, tab, edge, tb, False)
        y0 = _dot(_bf(h_ref[:, 0:n]), c_ref[0:n, :]) + _dot(_bf(h_ref[:, n:2 * n]), c_ref[n:2 * n, :]) + d_ref[...] * u
        y0_ref[...] = y0
        yg = _gelu(y0)
        z = _dot(_bf(yg), w_ref[...]) + bg_ref[...]
        y_ref[...] = _bf(yg * _sigmoid(z))
        if ng:
            @pl.when(pl.program_id(0) == s // tb - 1)
            def _():
                _gather_finish(slabs, sems[0:2], sems[2:4])

    vec = pl.BlockSpec((1, D_GROUP), lambda i: (0, 0))
    return pl.pallas_call(
        body, name=name, grid=(s // tb,),
        in_specs=[pl.BlockSpec((tb, D_GROUP), lambda i: (i, 0)),
                  pl.BlockSpec((1, 2 * n), lambda i: (0, 0)),
                  pl.BlockSpec((D_GROUP, 2 * n), lambda i: (0, 0)),
                  pl.BlockSpec((2 * n, D_GROUP), lambda i: (0, 0)),
                  vec,
                  pl.BlockSpec((None, D_GROUP, D_GROUP), lambda i: (layer, 0, 0)),
                  vec] + [ANY] * ng,
        out_specs=[pl.BlockSpec((tb, D_GROUP), lambda i: (i, 0)),
                   pl.BlockSpec((tb, 2 * n), lambda i: (i, 0)),
                   pl.BlockSpec((tb, D_GROUP), lambda i: (i, 0))] + [ANY] * ng,
        out_shape=[jax.ShapeDtypeStruct((s, D_GROUP), BF16), jax.ShapeDtypeStruct((s, 2 * n), F32),
                   jax.ShapeDtypeStruct((s, D_GROUP), F32)] + [jax.ShapeDtypeStruct(a.shape, a.dtype) for a in gather],
        input_output_aliases={7 + k: 3 + k for k in range(ng)},
        scratch_shapes=[pltpu.VMEM((SUBLANES, 2 * n), F32), pltpu.VMEM((S5_TAB_ROWS, n), F32),
                        pltpu.VMEM((tb, 2 * n), F32)] + (_gather_sems(ng) if ng else []),
        compiler_params=_cparams(1),
    )(h_in, a2, bexp, cexp, dskip, wglu, bglu, *gather)


def s5_bwd(dmix, h_in, y0, hst, a2, bexp, cexp, dskip, wglu, layer, bglu, tb, name):
    s = h_in.shape[0]
    n = N_STATE
    nb = s // tb
    halo = tb // 8

    def body(dy_ref, u_ref, y0_ref, h_ref, hp_ref, a_ref, b_ref, c_ref, d_ref, w_ref, bg_ref,
             du_ref, cs_ref, db_ref, dc_ref, dd_ref, dw_ref, dbg_ref, da_ref, edge, tab, g_ref, da_acc):
        i = pl.program_id(0)
        first = i == 0

        @pl.when(first)
        def _():
            edge[...] = jnp.zeros_like(edge)
            da_acc[...] = jnp.zeros_like(da_acc)
            _s5_scan_table(tab, a_ref[0:1, 0:n], -a_ref[0:1, n:2 * n], True)

        dy = dy_ref[...]
        u = u_ref[...]
        y0v = y0_ref[...]
        yg, t = _gelu_parts(y0v)
        z = _dot(_bf(yg), w_ref[...]) + bg_ref[...]
        sg = _sigmoid(z)
        dz = dy * yg * sg * (1.0 - sg)
        dyg = dy * sg + _dot_nt(_bf(dz), w_ref[...])
        _acc(dw_ref, _dot_tn(_bf(yg), _bf(dz)), first)
        _acc(dbg_ref, _colsum(dz), first)
        dy0 = dyg * _gelu_grad(y0v, t)
        _acc(dd_ref, _colsum(dy0 * u), first)
        dy0b = _bf(dy0)
        _acc(dc_ref.at[0:n, :], _dot_tn(_bf(h_ref[:, 0:n]), dy0b), first)
        _acc(dc_ref.at[n:2 * n, :], _dot_tn(_bf(h_ref[:, n:2 * n]), dy0b), first)
        g_ref[...] = _dot_nt(dy0b, c_ref[...])
        keep = jnp.where(i == nb - 1, 0.0, 1.0)
        row0 = _rows((SUBLANES, LANES)) == 0

        def grad_a(g, q, cols, gr, gi):
            cre, cim = cols
            rows = pl.ds(pl.multiple_of(g * SUBLANES, SUBLANES), SUBLANES)
            before = pl.ds(pl.multiple_of(jnp.maximum(g - 1, 0) * SUBLANES, SUBLANES), SUBLANES)
            pre = jnp.where(g == 0, hp_ref[:, cre] * keep, h_ref[before, cre])
            pim = jnp.where(g == 0, hp_ref[:, cim] * keep, h_ref[before, cim])
            pr = jnp.where(row0, pltpu.roll(pre, 1, 0), pltpu.roll(h_ref[rows, cre], 1, 0))
            pi = jnp.where(row0, pltpu.roll(pim, 1, 0), pltpu.roll(h_ref[rows, cim], 1, 0))
            da_acc[:, cre] += gr * pr + gi * pi
            da_acc[:, cim] += gi * pr - gr * pi

        _s5_scan(g_ref, g_ref, tab, edge, tb, True, grad_a)
        da_ref[...] = _colsum(da_acc[...])
        gr, gi = g_ref[:, 0:n], g_ref[:, n:2 * n]
        grb, gib = _bf(gr), _bf(gi)
        du = d_ref[...] * dy0 + _dot_nt(grb, b_ref[:, 0:n]) + _dot_nt(gib, b_ref[:, n:2 * n])
        ub = _bf(u)
        _acc(db_ref.at[:, 0:n], _dot_tn(ub, grb), first)
        _acc(db_ref.at[:, n:2 * n], _dot_tn(ub, gib), first)
        du_ref[...] = _bf(du)
        _acc(cs_ref, _colsum(du), first)

    rev = lambda i: (nb - 1 - i, 0)
    vec = pl.BlockSpec((1, D_GROUP), lambda i: (0, 0))
    vshape = jax.ShapeDtypeStruct((1, D_GROUP), F32)
    return pl.pallas_call(
        body, name=name, grid=(nb,),
        in_specs=[pl.BlockSpec((tb, D_GROUP), rev),
                  pl.BlockSpec((tb, D_GROUP), rev),
                  pl.BlockSpec((tb, D_GROUP), rev),
                  pl.BlockSpec((tb, 2 * n), rev),
                  pl.BlockSpec((8, 2 * n), lambda i: (jnp.maximum((nb - 1 - i) * halo - 1, 0), 0)),
                  pl.BlockSpec((1, 2 * n), lambda i: (0, 0)),
                  pl.BlockSpec((D_GROUP, 2 * n), lambda i: (0, 0)),
                  pl.BlockSpec((2 * n, D_GROUP), lambda i: (0, 0)),
                  vec,
                  pl.BlockSpec((None, D_GROUP, D_GROUP), lambda i: (layer, 0, 0)),
                  vec],
        out_specs=[pl.BlockSpec((tb, D_GROUP), rev), vec,
                   pl.BlockSpec((D_GROUP, 2 * n), lambda i: (0, 0)),
                   pl.BlockSpec((2 * n, D_GROUP), lambda i: (0, 0)),
                   vec,
                   pl.BlockSpec((D_GROUP, D_GROUP), lambda i: (0, 0)),
                   vec,
                   pl.BlockSpec((1, 2 * n), lambda i: (0, 0))],
        out_shape=[jax.ShapeDtypeStruct((s, D_GROUP), BF16), vshape,
                   jax.ShapeDtypeStruct((D_GROUP, 2 * n), F32), jax.ShapeDtypeStruct((2 * n, D_GROUP), F32),
                   vshape, jax.ShapeDtypeStruct((D_GROUP, D_GROUP), F32), vshape,
                   jax.ShapeDtypeStruct((1, 2 * n), F32)],
        scratch_shapes=[pltpu.VMEM((SUBLANES, 2 * n), F32), pltpu.VMEM((S5_TAB_ROWS, n), F32),
                        pltpu.VMEM((tb, 2 * n), F32), pltpu.VMEM((SUBLANES, 2 * n), F32)],
        compiler_params=_cparams(1),
    )(dmix, h_in, y0, hst, hst, a2, bexp, cexp, dskip, wglu, bglu)


CV_HALO = 32


def _gn_stats(hc, pmat):
    mu = _dot3(hc, pmat)
    xc = hc - mu
    var = _dot3(xc * xc, pmat)
    rstd = lax.rsqrt(var + LN_EPS)
    return xc * rstd, rstd


def cv_fwd(h_in, cw, cb, gg, gb, pmat, wpw, layer, bpw, tb, name):
    s = h_in.shape[0]
    hl = CV_HALO

    def body(v_ref, g_ref, cw_ref, cb_ref, gg_ref, gb_ref, p_ref, w_ref, bw_ref, y_ref, hc_ref, ext):
        @pl.when(pl.program_id(0) == 0)
        def _():
            ext[0:hl, :] = jnp.zeros((hl, D_GROUP), F32)

        ext[hl:hl + tb, :] = v_ref[...] * _sigmoid(g_ref[...])
        acc = jnp.zeros((tb, D_GROUP), F32) + cb_ref[...]
        for k in range(CONV_WIDTH):
            off = hl - (CONV_WIDTH - 1) + k
            acc = acc + cw_ref[k:k + 1, :] * ext[off:off + tb, :]
        hc_ref[...] = acc
        ext[0:hl, :] = ext[tb:tb + hl, :]
        xn, _ = _gn_stats(acc, p_ref[...])
        hn = xn * gg_ref[...] + gb_ref[...]
        hs = hn * _sigmoid(hn)
        y_ref[...] = _bf(_dot(_bf(hs), w_ref[...]) + bw_ref[...])

    vec = pl.BlockSpec((1, D_GROUP), lambda i: (0, 0))
    sq = pl.BlockSpec((D_GROUP, D_GROUP), lambda i: (0, 0))
    return pl.pallas_call(
        body, name=name, grid=(s // tb,),
        in_specs=[pl.BlockSpec((tb, D_GROUP), lambda i: (i, 1)),
                  pl.BlockSpec((tb, D_GROUP), lambda i: (i, 2)),
                  pl.BlockSpec((hl, D_GROUP), lambda i: (0, 0)),
                  vec, vec, vec, sq,
                  pl.BlockSpec((None, D_GROUP, D_GROUP), lambda i: (layer, 0, 0)),
                  vec],
        out_specs=[pl.BlockSpec((tb, D_GROUP), lambda i: (i, 0)), pl.BlockSpec((tb, D_GROUP), lambda i: (i, 0))],
        out_shape=[jax.ShapeDtypeStruct((s, D_GROUP), BF16), jax.ShapeDtypeStruct((s, D_GROUP), F32)],
        scratch_shapes=[pltpu.VMEM((hl + tb, D_GROUP), F32)],
        compiler_params=_cparams(1),
    )(h_in, h_in, cw, cb, gg, gb, pmat, wpw, bpw)


def cv_bwd(dmix, h_in, hc, cw, gg, gb, pmat, wpw, layer, tb, name, scatter=()):
    s = h_in.shape[0]
    hl = CV_HALO
    nb = s // tb
    per = tb // hl
    ns = len(scatter)

    def body(dy_ref, v_ref, g_ref, vh_ref, gh_ref, hc_ref, cw_ref, gg_ref, gb_ref, p_ref, w_ref, *rest):
        s_refs = rest[:ns]
        dvg_ref, cs_ref, dcw_ref, dcb_ref, dgg_ref, dgb_ref, dw_ref, dbw_ref = rest[ns:ns + 8]
        got_refs = rest[ns + 8:2 * ns + 8]
        ext, dext, head = rest[2 * ns + 8:2 * ns + 11]
        sems = rest[2 * ns + 11:]
        i = pl.program_id(0)
        first = i == 0

        @pl.when(first)
        def _():
            if ns:
                _scatter_start(s_refs, got_refs, sems)
            head[...] = jnp.zeros_like(head)

        dy = dy_ref[...]
        pm = p_ref[...]
        xn, rstd = _gn_stats(hc_ref[...], pm)
        hn = xn * gg_ref[...] + gb_ref[...]
        sg = _sigmoid(hn)
        hs = hn * sg
        dyb = _bf(dy)
        _acc(dbw_ref, _colsum(dy), first)
        _acc(dw_ref, _dot_tn(_bf(hs), dyb), first)
        dhs = _dot_nt(dyb, w_ref[...])
        dhn = dhs * sg * (1.0 + hn * (1.0 - sg))
        _acc(dgg_ref, _colsum(dhn * xn), first)
        _acc(dgb_ref, _colsum(dhn), first)
        dxn = dhn * gg_ref[...]
        dhc = rstd * (dxn - _dot3(dxn, pm) - xn * _dot3(dxn * xn, pm))
        _acc(dcb_ref, _colsum(dhc), first)
        v = v_ref[...]
        sgg = _sigmoid(g_ref[...])
        keep = jnp.where(i == nb - 1, 0.0, 1.0)
        ext[0:hl, :] = vh_ref[...] * _sigmoid(gh_ref[...]) * keep
        ext[hl:hl + tb, :] = v * sgg
        dext[0:tb, :] = dhc
        dext[tb:tb + hl, :] = head[...]
        head[...] = dhc[0:hl]
        dhg = jnp.zeros((tb, D_GROUP), F32)
        for k in range(CONV_WIDTH):
            off = hl - (CONV_WIDTH - 1) + k
            wk = _colsum(dhc * ext[off:off + tb, :])
            _acc(dcw_ref.at[k:k + 1, :], wk, first)
            back = CONV_WIDTH - 1 - k
            dhg = dhg + cw_ref[k:k + 1, :] * dext[back:back + tb, :]

        @pl.when(first)
        def _():
            dcw_ref[CONV_WIDTH:hl, :] = jnp.zeros((hl - CONV_WIDTH, D_GROUP), F32)

        dv = dhg * sgg
        dg = dhg * v * sgg * (1.0 - sgg)
        dvg_ref[:, 0:D_GROUP] = _bf(dv)
        dvg_ref[:, D_GROUP:2 * D_GROUP] = _bf(dg)
        _acc(cs_ref.at[:, 0:D_GROUP], _colsum(dv), first)
        _acc(cs_ref.at[:, D_GROUP:2 * D_GROUP], _colsum(dg), first)
        if ns:
            @pl.when(i == nb - 1)
            def _():
                _scatter_finish(s_refs, got_refs, sems)

    vec = pl.BlockSpec((1, D_GROUP), lambda i: (0, 0))
    sq = pl.BlockSpec((D_GROUP, D_GROUP), lambda i: (0, 0))
    tap = pl.BlockSpec((hl, D_GROUP), lambda i: (0, 0))
    vshape = jax.ShapeDtypeStruct((1, D_GROUP), F32)

    def blk(col):
        return pl.BlockSpec((tb, D_GROUP), lambda i: (nb - 1 - i, col))

    def halo_blk(col):
        return pl.BlockSpec((hl, D_GROUP), lambda i: (jnp.maximum((nb - 1 - i) * per - 1, 0), col))

    return pl.pallas_call(
        body, name=name, grid=(nb,),
        in_specs=[blk(1), blk(1), blk(2), halo_blk(1), halo_blk(2),
                  pl.BlockSpec((tb, D_GROUP), lambda i: (nb - 1 - i, 0)),
                  tap, vec, vec, sq,
                  pl.BlockSpec((None, D_GROUP, D_GROUP), lambda i: (layer, 0, 0))] + [ANY] * ns,
        out_specs=[pl.BlockSpec((tb, 2 * D_GROUP), lambda i: (nb - 1 - i, 0)),
                   pl.BlockSpec((1, 2 * D_GROUP), lambda i: (0, 0)),
                   tap, vec, vec, vec, sq, vec] + [ANY] * ns,
        out_shape=[jax.ShapeDtypeStruct((s, 2 * D_GROUP), BF16), jax.ShapeDtypeStruct((1, 2 * D_GROUP), F32),
                   jax.ShapeDtypeStruct((hl, D_GROUP), F32), vshape, vshape, vshape,
                   jax.ShapeDtypeStruct((D_GROUP, D_GROUP), F32), vshape] + _scatter_shapes(scatter),
        scratch_shapes=[pltpu.VMEM((hl + tb, D_GROUP), F32), pltpu.VMEM((tb + hl, D_GROUP), F32),
                        pltpu.VMEM((hl, D_GROUP), F32)] + (_scatter_sems(ns) if ns else []),
        compiler_params=_cparams(1),
    )(dmix, h_in, h_in, h_in, h_in, hc, cw, gg, gb, pmat, wpw, *scatter)


LRU_HALO = 8


def _lru_gates(xc, wr_ref, br_ref, wi_ref, bi_ref, sp_ref):
    xcb = _bf(xc)
    r = _sigmoid(_dot(xcb, wr_ref[...]) + br_ref[...])
    gi = _sigmoid(_dot(xcb, wi_ref[...]) + bi_ref[...])
    la = -LRU_C * r * sp_ref[...]
    a = jnp.exp(la)
    e2 = a * a
    sq = jnp.sqrt(-jnp.tanh(la) * (e2 + 1.0))
    return r, gi, a, e2, sq


def _rscan(a, b, tb, reverse):
    row = _rows(a.shape)
    sh = 1
    while sh < tb:
        if reverse:
            amt, mask = tb - sh, row < tb - sh
        else:
            amt, mask = sh, row >= sh
        a_s = jnp.where(mask, pltpu.roll(a, amt, 0), 1.0)
        b_s = jnp.where(mask, pltpu.roll(b, amt, 0), 0.0)
        b = b + a * b_s
        a = a * a_s
        sh *= 2
    return a, b


def lru_fwd(h_in, cw, cb, wr, br, wi, bi, sp, tb, name):
    s = h_in.shape[0]
    hl = LRU_HALO

    def body(xg_ref, xr_ref, cw_ref, cb_ref, wr_ref, br_ref, wi_ref, bi_ref, sp_ref, y_ref, xc_ref, h_ref, ext, carry):
        @pl.when(pl.program_id(0) == 0)
        def _():
            ext[0:hl, :] = jnp.zeros((hl, D_GROUP), F32)
            carry[...] = jnp.zeros_like(carry)

        ext[hl:hl + tb, :] = xr_ref[...]
        xc = jnp.zeros((tb, D_GROUP), F32) + cb_ref[...]
        for k in range(LRU_CONV_WIDTH):
            off = hl - (LRU_CONV_WIDTH - 1) + k
            xc = xc + cw_ref[k:k + 1, :] * ext[off:off + tb, :]
        xc_ref[...] = xc
        ext[0:hl, :] = ext[tb:tb + hl, :]
        r, gi, a, e2, sq = _lru_gates(xc, wr_ref, br_ref, wi_ref, bi_ref, sp_ref)
        pa, hloc = _rscan(a, sq * (gi * xc), tb, False)
        h = hloc + pa * carry[7:8, :]
        h_ref[...] = h
        carry[...] = h[tb - 8:tb]
        y_ref[...] = _bf(h * _gelu(xg_ref[...]))

    vec = pl.BlockSpec((1, D_GROUP), lambda i: (0, 0))
    sq_spec = pl.BlockSpec((D_GROUP, D_GROUP), lambda i: (0, 0))
    blk = pl.BlockSpec((tb, D_GROUP), lambda i: (i, 0))
    return pl.pallas_call(
        body, name=name, grid=(s // tb,),
        in_specs=[pl.BlockSpec((tb, D_GROUP), lambda i: (i, 3)),
                  pl.BlockSpec((tb, D_GROUP), lambda i: (i, 4)),
                  pl.BlockSpec((hl, D_GROUP), lambda i: (0, 0)),
                  vec, sq_spec, vec, sq_spec, vec, vec],
        out_specs=[blk, blk, blk],
        out_shape=[jax.ShapeDtypeStruct((s, D_GROUP), BF16), jax.ShapeDtypeStruct((s, D_GROUP), F32),
                   jax.ShapeDtypeStruct((s, D_GROUP), F32)],
        scratch_shapes=[pltpu.VMEM((hl + tb, D_GROUP), F32), pltpu.VMEM((8, D_GROUP), F32)],
        compiler_params=_cparams(1),
    )(h_in, h_in, cw, cb, wr, br, wi, bi, sp)


def lru_bwd(dmix, h_in, xcs, hs, cw, wr, br, wi, bi, sp, tb, name):
    s = h_in.shape[0]
    hl = LRU_HALO
    nb = s // tb
    per = tb // hl

    def body(dy_ref, xg_ref, xr_ref, xrh_ref, xc_ref, h_ref, hp_ref, cw_ref, wr_ref, br_ref, wi_ref, bi_ref, sp_ref,
             dx_ref, cs_ref, dcw_ref, dcb_ref, dwr_ref, dbr_ref, dwi_ref, dbi_ref, dsp_ref,
             ext, dext, head, anext, gnext):
        i = pl.program_id(0)
        first = i == 0

        @pl.when(first)
        def _():
            head[...] = jnp.zeros_like(head)
            anext[...] = jnp.zeros_like(anext)
            gnext[...] = jnp.zeros_like(gnext)

        dy = dy_ref[...]
        xg = xg_ref[...]
        xc = xc_ref[...]
        h = h_ref[...]
        r, gi, a, e2, sq = _lru_gates(xc, wr_ref, br_ref, wi_ref, bi_ref, sp_ref)
        gate, t = _gelu_parts(xg)
        dh = dy * gate
        dxg = dy * h * _gelu_grad(xg, t)
        row = _rows((tb, D_GROUP))
        coef = jnp.where(row == tb - 1, anext[0:1, :], pltpu.roll(a, tb - 1, 0))
        pc, gloc = _rscan(coef, dh, tb, True)
        gfull = gloc + pc * gnext[0:1, :]
        anext[...] = a[0:8]
        gnext[...] = gfull[0:8]
        keep = jnp.where(i == nb - 1, 0.0, 1.0)
        hprev = jnp.where(row == 0, hp_ref[7:8, :] * keep, pltpu.roll(h, 1, 0))
        da = gfull * hprev
        uu = gi * xc
        dsq = gfull * uu
        duu = gfull * sq
        dla = da * a - dsq * e2 / sq
        sp = sp_ref[...]
        dr = dla * (-LRU_C) * sp
        _acc(dsp_ref, _colsum(dla * (-LRU_C) * r), first)
        dzr = dr * r * (1.0 - r)
        dzi = duu * xc * gi * (1.0 - gi)
        dzrb, dzib = _bf(dzr), _bf(dzi)
        dxc = duu * gi + _dot_nt(dzrb, wr_ref[...]) + _dot_nt(dzib, wi_ref[...])
        xcb = _bf(xc)
        _acc(dwr_ref, _dot_tn(xcb, dzrb), first)
        _acc(dwi_ref, _dot_tn(xcb, dzib), first)
        _acc(dbr_ref, _colsum(dzr), first)
        _acc(dbi_ref, _colsum(dzi), first)
        _acc(dcb_ref, _colsum(dxc), first)
        ext[0:hl, :] = xrh_ref[...] * keep
        ext[hl:hl + tb, :] = xr_ref[...]
        dext[0:tb, :] = dxc
        dext[tb:tb + hl, :] = head[...]
        head[...] = dxc[0:hl]
        dxr = jnp.zeros((tb, D_GROUP), F32)
        for k in range(LRU_CONV_WIDTH):
            off = hl - (LRU_CONV_WIDTH - 1) + k
            _acc(dcw_ref.at[k:k + 1, :], _colsum(dxc * ext[off:off + tb, :]), first)
            back = LRU_CONV_WIDTH - 1 - k
            dxr = dxr + cw_ref[k:k + 1, :] * dext[back:back + tb, :]

        @pl.when(first)
        def _():
            dcw_ref[LRU_CONV_WIDTH:hl, :] = jnp.zeros((hl - LRU_CONV_WIDTH, D_GROUP), F32)

        dx_ref[:, 0:D_GROUP] = _bf(dxg)
        dx_ref[:, D_GROUP:2 * D_GROUP] = _bf(dxr)
        _acc(cs_ref.at[:, 0:D_GROUP], _colsum(dxg), first)
        _acc(cs_ref.at[:, D_GROUP:2 * D_GROUP], _colsum(dxr), first)

    vec = pl.BlockSpec((1, D_GROUP), lambda i: (0, 0))
    sq_spec = pl.BlockSpec((D_GROUP, D_GROUP), lambda i: (0, 0))
    tap = pl.BlockSpec((hl, D_GROUP), lambda i: (0, 0))
    vshape = jax.ShapeDtypeStruct((1, D_GROUP), F32)
    sshape = jax.ShapeDtypeStruct((D_GROUP, D_GROUP), F32)

    def blk(col):
        return pl.BlockSpec((tb, D_GROUP), lambda i: (nb - 1 - i, col))

    def halo_blk(col):
        return pl.BlockSpec((hl, D_GROUP), lambda i: (jnp.maximum((nb - 1 - i) * per - 1, 0), col))

    return pl.pallas_call(
        body, name=name, grid=(nb,),
        in_specs=[blk(2), blk(3), blk(4), halo_blk(4), blk(0), blk(0), halo_blk(0),
                  tap, sq_spec, vec, sq_spec, vec, vec],
        out_specs=[pl.BlockSpec((tb, 2 * D_GROUP), lambda i: (nb - 1 - i, 0)),
                   pl.BlockSpec((1, 2 * D_GROUP), lambda i: (0, 0)),
                   tap, vec, sq_spec, vec, sq_spec, vec, vec],
        out_shape=[jax.ShapeDtypeStruct((s, 2 * D_GROUP), BF16), jax.ShapeDtypeStruct((1, 2 * D_GROUP), F32),
                   jax.ShapeDtypeStruct((hl, D_GROUP), F32), vshape, sshape, vshape, sshape, vshape, vshape],
        scratch_shapes=[pltpu.VMEM((hl + tb, D_GROUP), F32), pltpu.VMEM((tb + hl, D_GROUP), F32),
                        pltpu.VMEM((hl, D_GROUP), F32), pltpu.VMEM((8, D_GROUP), F32), pltpu.VMEM((8, D_GROUP), F32)],
        compiler_params=_cparams(1),
    )(dmix, h_in, h_in, h_in, xcs, hs, hs, cw, wr, br, wi, bi, sp)


ATTN_HEADS = 4
ATTN_HEAD_DIM = 64
ATTN_SCALE = ATTN_HEAD_DIM ** -0.5


def _head_mask(h):
    lane = lax.broadcasted_iota(jnp.int32, (1, D_GROUP), 1)
    return jnp.where((lane >= h * ATTN_HEAD_DIM) & (lane < (h + 1) * ATTN_HEAD_DIM), 1.0, 0.0)


def _softmax_rows(sc):
    e = jnp.exp(sc - jnp.max(sc, -1, keepdims=True))
    return e / jnp.sum(e, -1, keepdims=True)


def attn_fwd(h_in, kv, tb, name):
    s = h_in.shape[0]

    def body(q_ref, kv_ref, y_ref):
        q = q_ref[...]
        kb = _bf(kv_ref[:, 0:D_GROUP])
        vb = _bf(kv_ref[:, D_GROUP:2 * D_GROUP])
        out = jnp.zeros((tb, D_GROUP), F32)
        for h in range(ATTN_HEADS):
            mask = _head_mask(h)
            p = _softmax_rows(_dot_nt(_bf(q * mask), kb) * ATTN_SCALE)
            out = out + _dot(_bf(p), vb) * mask
        y_ref[...] = _bf(out)

    return pl.pallas_call(
        body, name=name, grid=(s // tb,),
        in_specs=[pl.BlockSpec((tb, D_GROUP), lambda i: (i, 5)),
                  pl.BlockSpec((D_GROUP, 2 * D_GROUP), lambda i: (0, 0))],
        out_specs=pl.BlockSpec((tb, D_GROUP), lambda i: (i, 0)),
        out_shape=jax.ShapeDtypeStruct((s, D_GROUP), BF16),
        compiler_params=_cparams(1),
    )(h_in, kv)


def attn_bwd(dmix, h_in, kv, tb, name, exchange=()):
    s = h_in.shape[0]
    ne = len(exchange)

    def body(do_ref, q_ref, kv_ref, *rest):
        g_refs = rest[:ne]
        dq_ref, cs_ref, dkv_ref = rest[ne:ne + 3]
        got_refs, sems = rest[ne + 3:2 * ne + 3], rest[2 * ne + 3:]
        first = pl.program_id(0) == 0
        if ne:
            @pl.when(first)
            def _():
                _exchange_start(g_refs, got_refs, sems)

        q = q_ref[...]
        do = do_ref[...]
        kb = _bf(kv_ref[:, 0:D_GROUP])
        vb = _bf(kv_ref[:, D_GROUP:2 * D_GROUP])
        dq = jnp.zeros((tb, D_GROUP), F32)
        dk = jnp.zeros((D_GROUP, D_GROUP), F32)
        dv = jnp.zeros((D_GROUP, D_GROUP), F32)
        for h in range(ATTN_HEADS):
            mask = _head_mask(h)
            qm = _bf(q * mask)
            p = _softmax_rows(_dot_nt(qm, kb) * ATTN_SCALE)
            dom = _bf(do * mask)
            dp = _dot_nt(dom, vb)
            dv = dv + _dot_tn(_bf(p), dom)
            ds = _bf(p * (dp - jnp.sum(dp * p, -1, keepdims=True)) * ATTN_SCALE)
            dq = dq + _dot(ds, kb) * mask
            dk = dk + _dot_tn(ds, qm)
        dq_ref[...] = _bf(dq)
        _acc(cs_ref, _colsum(dq), first)
        _acc(dkv_ref.at[:, 0:D_GROUP], dk, first)
        _acc(dkv_ref.at[:, D_GROUP:2 * D_GROUP], dv, first)
        if ne:
            @pl.when(pl.program_id(0) == s // tb - 1)
            def _():
                _exchange_finish(g_refs, got_refs, sems)

    return pl.pallas_call(
        body, name=name, grid=(s // tb,),
        in_specs=[pl.BlockSpec((tb, D_GROUP), lambda i: (i, 3)),
                  pl.BlockSpec((tb, D_GROUP), lambda i: (i, 5)),
                  pl.BlockSpec((D_GROUP, 2 * D_GROUP), lambda i: (0, 0))] + [ANY] * ne,
        out_specs=[pl.BlockSpec((tb, D_GROUP), lambda i: (i, 0)),
                   pl.BlockSpec((1, D_GROUP), lambda i: (0, 0)),
                   pl.BlockSpec((D_GROUP, 2 * D_GROUP), lambda i: (0, 0))] + [ANY] * ne,
        out_shape=[jax.ShapeDtypeStruct((s, D_GROUP), BF16), jax.ShapeDtypeStruct((1, D_GROUP), F32),
                   jax.ShapeDtypeStruct((D_GROUP, 2 * D_GROUP), F32)] + _exchange_shapes(exchange),
        scratch_shapes=_exchange_sems(ne) if ne else [],
        compiler_params=_cparams(1),
    )(dmix, h_in, kv, *exchange)


FFN_RB = 16
FFN_UNROLL_FWD = 4
FFN_UNROLL_BWD = 2
FFN_TAP_ROWS = 8
FFN_TN = D_FF // 2


def _shift_down(cur, tail, k):
    return pltpu.roll(jnp.concatenate([tail, cur], axis=0), k, 0)[SUBLANES:]


def _shift_up(cur, head, k):
    rb = cur.shape[0]
    return pltpu.roll(jnp.concatenate([cur, head], axis=0), rb + SUBLANES - k, 0)[:rb]


def _fold8(v):
    tot = v[0:SUBLANES]
    for t in range(1, v.shape[0] // SUBLANES):
        tot = tot + v[t * SUBLANES:(t + 1) * SUBLANES]
    return tot


def _strip(r):
    return pl.ds(pl.multiple_of(r * FFN_RB, FFN_RB), FFN_RB)


def ffn_act_fwd(u, cw, cb, tb, name, gather=()):
    s = u.shape[0]
    rb = FFN_RB
    nct = D_FF // FFN_TN
    nstrip = tb // rb
    ng = len(gather)

    def body(uv_ref, ug_ref, wv_ref, wg_ref, bv_ref, bg_ref, *rest):
        hf_ref, keep_ref = rest[ng], rest[ng + 1]
        slabs = rest[ng + 2:2 * ng + 2]
        tailv, tailg = rest[2 * ng + 2], rest[2 * ng + 3]
        sems = rest[2 * ng + 4:]
        if ng:
            @pl.when((pl.program_id(0) == 0) & (pl.program_id(1) == 0))
            def _():
                _gather_start(slabs, sems[0:2])

        @pl.when(pl.program_id(1) == 0)
        def _():
            tailv[...] = jnp.zeros_like(tailv)
            tailg[...] = jnp.zeros_like(tailg)

        for cc in range(FFN_TN // LANES):
            cols = slice(cc * LANES, (cc + 1) * LANES)
            wv = [wv_ref[k:k + 1, cols] for k in range(FFN_CONV_WIDTH)]
            wg = [wg_ref[k:k + 1, cols] for k in range(FFN_CONV_WIDTH)]
            bv, bg = bv_ref[:, cols], bg_ref[:, cols]

            def strip(r, carry):
                tail_v, tail_g = carry
                cur_v, cur_g = uv_ref[_strip(r), cols], ug_ref[_strip(r), cols]
                vc = wv[0] * _shift_down(cur_v, tail_v, 2) + wv[1] * _shift_down(cur_v, tail_v, 1) + wv[2] * cur_v + bv
                gc = wg[0] * _shift_down(cur_g, tail_g, 2) + wg[1] * _shift_down(cur_g, tail_g, 1) + wg[2] * cur_g + bg
                ge, t = _gelu_parts(gc)
                hf_ref[_strip(r), cols] = _bf(vc * ge)
                keep_ref[0, _strip(r), cols] = _bf(vc)
                keep_ref[1, _strip(r), cols] = _bf(ge)
                keep_ref[2, _strip(r), cols] = _bf(_gelu_grad(gc, t))
                return cur_v[rb - SUBLANES:], cur_g[rb - SUBLANES:]

            def strips(q, carry):
                for k in range(FFN_UNROLL_FWD):
                    carry = strip(q * FFN_UNROLL_FWD + k, carry)
                return carry

            last_v, last_g = lax.fori_loop(0, nstrip // FFN_UNROLL_FWD, strips, (tailv[:, cols], tailg[:, cols]))
            tailv[:, cols] = last_v
            tailg[:, cols] = last_g

        if ng:
            @pl.when((pl.program_id(0) == nct - 1) & (pl.program_id(1) == s // tb - 1))
            def _():
                _gather_finish(slabs, sems[0:2], sems[2:4])

    return pl.pallas_call(
        body, name=name, grid=(nct, s // tb),
        in_specs=[pl.BlockSpec((tb, FFN_TN), lambda c, i: (i, c)),
                  pl.BlockSpec((tb, FFN_TN), lambda c, i: (i, c + nct)),
                  pl.BlockSpec((FFN_TAP_ROWS, FFN_TN), lambda c, i: (0, c)),
                  pl.BlockSpec((FFN_TAP_ROWS, FFN_TN), lambda c, i: (0, c + nct)),
                  pl.BlockSpec((1, FFN_TN), lambda c, i: (0, c)),
                  pl.BlockSpec((1, FFN_TN), lambda c, i: (0, c + nct))] + [ANY] * ng,
        out_specs=[pl.BlockSpec((tb, FFN_TN), lambda c, i: (i, c)),
                   pl.BlockSpec((3, tb, FFN_TN), lambda c, i: (0, i, c))] + [ANY] * ng,
        out_shape=[jax.ShapeDtypeStruct((s, D_FF), BF16), jax.ShapeDtypeStruct((3, s, D_FF), BF16)]
        + [jax.ShapeDtypeStruct(a.shape, a.dtype) for a in gather],
        input_output_aliases={6 + k: 2 + k for k in range(ng)},
        scratch_shapes=[pltpu.VMEM((SUBLANES, FFN_TN), F32), pltpu.VMEM((SUBLANES, FFN_TN), F32)]
        + (_gather_sems(ng) if ng else []),
        compiler_params=_cparams(2),
    )(u, u, cw, cw, cb, cb, *gather)


def ffn_act_bwd(dhf, u, kept, cw, tb, name, scatter=()):
    s = u.shape[0]
    rb = FFN_RB
    nct = D_FF // FFN_TN
    nb = s // tb
    nstrip = tb // rb
    ntap = FFN_CONV_WIDTH
    ns = len(scatter)

    def body(dh_ref, uv_ref, ug_ref, kept_ref, wv_ref, wg_ref, *rest):
        s_refs = rest[:ns]
        du_ref, dwv_ref, dwg_ref = rest[ns:ns + 3]
        got_refs = rest[ns + 3:2 * ns + 3]
        headv, headg = rest[2 * ns + 3], rest[2 * ns + 4]
        sems = rest[2 * ns + 5:]
        i = pl.program_id(1)
        first = i == 0
        if ns:
            @pl.when((pl.program_id(0) == 0) & first)
            def _():
                _scatter_start(s_refs, got_refs, sems)

        @pl.when(first)
        def _():
            headv[...] = jnp.zeros_like(headv)
            headg[...] = jnp.zeros_like(headg)
            dwv_ref[...] = jnp.zeros_like(dwv_ref)
            dwg_ref[...] = jnp.zeros_like(dwg_ref)

        zero = jnp.zeros((SUBLANES, LANES), F32)
        for cc in range(FFN_TN // LANES):
            cols = slice(cc * LANES, (cc + 1) * LANES)
            wv = [wv_ref[k:k + 1, cols] for k in range(ntap)]
            wg = [wg_ref[k:k + 1, cols] for k in range(ntap)]

            def strip(ii, carry):
                head_dv, head_dg, acc_v, acc_g = carry
                r = nstrip - 1 - ii
                dh = dh_ref[_strip(r), cols]
                dvc = dh * kept_ref[1, _strip(r), cols].astype(F32)
                dgc = dh * kept_ref[0, _strip(r), cols].astype(F32) * kept_ref[2, _strip(r), cols].astype(F32)
                sdv = [_shift_up(dvc, head_dv, 2), _shift_up(dvc, head_dv, 1), dvc]
                sdg = [_shift_up(dgc, head_dg, 2), _shift_up(dgc, head_dg, 1), dgc]
                cur_v, cur_g = uv_ref[_strip(r), cols], ug_ref[_strip(r), cols]
                acc_v = tuple(acc_v[k] + _fold8(cur_v * sdv[k]) for k in range(ntap)) + (acc_v[ntap] + _fold8(dvc),)
                acc_g = tuple(acc_g[k] + _fold8(cur_g * sdg[k]) for k in range(ntap)) + (acc_g[ntap] + _fold8(dgc),)
                du_v = wv[0] * sdv[0] + wv[1] * sdv[1] + wv[2] * sdv[2]
                du_g = wg[0] * sdg[0] + wg[1] * sdg[1] + wg[2] * sdg[2]
                du_ref[0, _strip(r), cols] = _bf(du_v)
                du_ref[1, _strip(r), cols] = _bf(du_g)
                return dvc[0:SUBLANES], dgc[0:SUBLANES], acc_v, acc_g

            init = (headv[:, cols], headg[:, cols], (zero,) * (ntap + 1), (zero,) * (ntap + 1))
            def strips(q, carry):
                for k in range(FFN_UNROLL_BWD):
                    carry = strip(q * FFN_UNROLL_BWD + k, carry)
                return carry

            top_dv, top_dg, acc_v, acc_g = lax.fori_loop(0, nstrip // FFN_UNROLL_BWD, strips, init)
            headv[:, cols] = top_dv
            headg[:, cols] = top_dg
            for k in range(ntap + 1):
                dwv_ref[k:k + 1, cols] += _colsum(acc_v[k])
                dwg_ref[k:k + 1, cols] += _colsum(acc_g[k])

        if ns:
            @pl.when((pl.program_id(0) == nct - 1) & (i == nb - 1))
            def _():
                _scatter_finish(s_refs, got_refs, sems)

    def blk(shift):
        return pl.BlockSpec((tb, FFN_TN), lambda c, i: (nb - 1 - i, c + shift))

    tapv = pl.BlockSpec((FFN_TAP_ROWS, FFN_TN), lambda c, i: (0, c))
    tapg = pl.BlockSpec((FFN_TAP_ROWS, FFN_TN), lambda c, i: (0, c + nct))
    return pl.pallas_call(
        body, name=name, grid=(nct, nb),
        in_specs=[blk(0), blk(0), blk(nct), pl.BlockSpec((3, tb, FFN_TN), lambda c, i: (0, nb - 1 - i, c)), tapv, tapg]
        + [ANY] * ns,
        out_specs=[pl.BlockSpec((2, tb, FFN_TN), lambda c, i: (0, nb - 1 - i, c)), tapv, tapv] + [ANY] * ns,
        out_shape=[jax.ShapeDtypeStruct((2, s, D_FF), BF16), jax.ShapeDtypeStruct((FFN_TAP_ROWS, D_FF), F32),
                   jax.ShapeDtypeStruct((FFN_TAP_ROWS, D_FF), F32)] + _scatter_shapes(scatter),
        scratch_shapes=[pltpu.VMEM((SUBLANES, FFN_TN), F32), pltpu.VMEM((SUBLANES, FFN_TN), F32)]
        + (_scatter_sems(ns) if ns else []),
        compiler_params=_cparams(2),
    )(dhf, u, u, kept, cw, cw, *scatter)


def _place():
    x, y, c = lax.axis_index("x"), lax.axis_index("y"), lax.axis_index("c")
    return x, y, c, 2 * x + y


def _chip_peer(x, y, d):
    return jnp.bitwise_xor(x, d >> 1), jnp.bitwise_xor(y, d & 1)


def _my_half(ref_rows, c):
    half = ref_rows // 2
    return pl.ds(c * half, half)


def _gather_copy(ref, part, c, sems, k, d, to):
    rows = _my_half(ref.shape[1], c)
    return pltpu.make_async_remote_copy(src_ref=ref.at[part, rows], dst_ref=ref.at[part, rows], send_sem=sems[0].at[k, d - 1],
                                        recv_sem=sems[1].at[k, d - 1], device_id=to, device_id_type=MESH)


def _gather_start(slabs, ici_sems):
    x, y, c, j = _place()
    for k, ref in enumerate(slabs):
        for d in (1, 2, 3):
            px, py = _chip_peer(x, y, d)
            _gather_copy(ref, j, c, ici_sems, k, d, (px, py, c)).start()


def _gather_finish(slabs, ici_sems, d2d_sems):
    x, y, c, j = _place()
    sib = (x, y, 1 - c)
    passed = []
    for d in (1, 2, 3):
        jd = jnp.bitwise_xor(j, d)
        for k, ref in enumerate(slabs):
            _gather_copy(ref, jd, c, ici_sems, k, d, sib).wait_recv()
            cp = _gather_copy(ref, jd, c, d2d_sems, k, d, sib)
            cp.start()
            passed.append(cp)
    for cp in passed:
        cp.wait_recv()
        cp.wait_send()
    for k, ref in enumerate(slabs):
        for d in (1, 2, 3):
            _gather_copy(ref, j, c, ici_sems, k, d, sib).wait_send()


def _gather_sems(n):
    return [pltpu.SemaphoreType.DMA((n, 3)) for _ in range(4)]


def gather_weights(slabs):
    n = len(slabs)

    def body(*refs):
        outs = refs[n:2 * n]
        sems = refs[2 * n:]
        _gather_start(outs, sems[0:2])
        _gather_finish(outs, sems[0:2], sems[2:4])

    return pl.pallas_call(
        body, name="gather_weights", in_specs=[ANY] * n, out_specs=[ANY] * n,
        out_shape=[jax.ShapeDtypeStruct(a.shape, a.dtype) for a in slabs],
        input_output_aliases={w: w for w in range(n)}, scratch_shapes=_gather_sems(n),
    )(*slabs)


def _own_slab(part, jidx):
    slab = lax.empty((N_CHIPS,) + part.shape, part.dtype)
    return lax.dynamic_update_slice_in_dim(slab, part[None], jidx, axis=0)


def _exchange_copy(g_ref, got_ref, k, sems):
    x, y, c, j = _place()
    half = g_ref.shape[1] // 2
    return pltpu.make_async_remote_copy(
        src_ref=g_ref.at[:, pl.ds((1 - c) * half, half)], dst_ref=got_ref, send_sem=sems[0].at[k],
        recv_sem=sems[1].at[k], device_id=(x, y, 1 - c), device_id_type=MESH)


def _exchange_start(g_refs, got_refs, sems):
    for k in range(len(g_refs)):
        _exchange_copy(g_refs[k], got_refs[k], k, sems).start()


def _exchange_finish(g_refs, got_refs, sems):
    for k in range(len(g_refs)):
        _exchange_copy(g_refs[k], got_refs[k], k, sems).wait()


def _exchange_sems(n):
    return [pltpu.SemaphoreType.DMA((n,)), pltpu.SemaphoreType.DMA((n,))]


def _exchange_shapes(gs):
    return [jax.ShapeDtypeStruct((g.shape[0], g.shape[1] // 2, g.shape[2]), F32) for g in gs]


def exchange_cores(gbig, name, small_all=None):
    n = len(gbig)
    with_small = small_all is not None

    def body(*refs):
        g_refs = refs[:n]
        got_refs = refs[n + with_small:2 * n + with_small]
        dsem, esem, ssem, rsem, fsem, hsem = refs[2 * (n + with_small):]
        x, y, c, j = _place()
        sib = (x, y, 1 - c)
        big = []
        for k in range(n):
            half = g_refs[k].shape[1] // 2
            cp = pltpu.make_async_remote_copy(
                src_ref=g_refs[k].at[:, pl.ds((1 - c) * half, half)], dst_ref=got_refs[k], send_sem=dsem.at[k],
                recv_sem=esem.at[k], device_id=sib, device_id_type=MESH)
            cp.start()
            big.append(cp)
        if with_small:
            all_ref = refs[2 * n + 1]
            me = 4 * x + 2 * y + c

            def small_copy(k, block, to, sems):
                return pltpu.make_async_remote_copy(
                    src_ref=all_ref.at[block], dst_ref=all_ref.at[block],
                    send_sem=sems[0].at[k], recv_sem=sems[1].at[k], device_id=to, device_id_type=MESH)

            first = [small_copy(0, me, sib, (ssem, rsem))]
            for d in (1, 2, 3):
                px, py = _chip_peer(x, y, d)
                first.append(small_copy(d, me, (px, py, c), (ssem, rsem)))
            for cp in first:
                cp.start()
            passed = []
            for d in (1, 2, 3):
                px, py = _chip_peer(x, y, d)
                src_block = 4 * px + 2 * py + c
                small_copy(d, src_block, sib, (ssem, rsem)).wait_recv()
                cp = small_copy(d - 1, src_block, sib, (fsem, hsem))
                cp.start()
                passed.append(cp)
            small_copy(0, me, sib, (ssem, rsem)).wait_recv()
            for cp in passed:
                cp.wait_recv()
            for cp in first + passed:
                cp.wait_send()
        for cp in big:
            cp.wait()

    ops = list(gbig) + ([small_all] if with_small else [])
    out_shape = [jax.ShapeDtypeStruct((g.shape[0], g.shape[1] // 2, g.shape[2]), F32) for g in gbig]
    aliases = {}
    if with_small:
        out_shape.append(jax.ShapeDtypeStruct(small_all.shape, F32))
        aliases = {n: n}
    return pl.pallas_call(
        body, name=name, in_specs=[ANY] * len(ops), out_specs=[ANY] * len(out_shape), out_shape=out_shape,
        input_output_aliases=aliases,
        scratch_shapes=[pltpu.SemaphoreType.DMA((n,)), pltpu.SemaphoreType.DMA((n,)),
                        pltpu.SemaphoreType.DMA((4,)), pltpu.SemaphoreType.DMA((4,)),
                        pltpu.SemaphoreType.DMA((3,)), pltpu.SemaphoreType.DMA((3,))],
    )(*ops)


def _scatter_copy(s_ref, got_ref, k, d, sems):
    x, y, c, j = _place()
    px, py = _chip_peer(x, y, d)
    return pltpu.make_async_remote_copy(
        src_ref=s_ref.at[jnp.bitwise_xor(j, d)], dst_ref=got_ref.at[d - 1], send_sem=sems[0].at[k, d - 1],
        recv_sem=sems[1].at[k, d - 1], device_id=(px, py, c), device_id_type=MESH)


def _scatter_start(s_refs, got_refs, sems):
    for d in (1, 2, 3):
        for k in range(len(s_refs)):
            _scatter_copy(s_refs[k], got_refs[k], k, d, sems).start()


def _scatter_finish(s_refs, got_refs, sems):
    for d in (1, 2, 3):
        for k in range(len(s_refs)):
            _scatter_copy(s_refs[k], got_refs[k], k, d, sems).wait()


def _scatter_sems(n):
    return [pltpu.SemaphoreType.DMA((n, 3)), pltpu.SemaphoreType.DMA((n, 3))]


def _scatter_shapes(s1):
    return [jax.ShapeDtypeStruct((3,) + a.shape[1:], a.dtype) for a in s1]


def scatter_shards(s1, name):
    n = len(s1)

    def body(*refs):
        _scatter_start(refs[:n], refs[n:2 * n], refs[2 * n:])
        _scatter_finish(refs[:n], refs[n:2 * n], refs[2 * n:])

    return pl.pallas_call(
        body, name=name, in_specs=[ANY] * n, out_specs=[ANY] * n, out_shape=_scatter_shapes(s1),
        scratch_shapes=_scatter_sems(n),
    )(*s1)


def share_with_sibling(parts):
    n = len(parts)

    def body(*refs):
        out_refs = refs[n:2 * n]
        ssem, rsem = refs[2 * n:]
        x, y, c, j = _place()
        cps = []
        for k in range(n):
            rows = _my_half(out_refs[k].shape[1], c)
            for l in range(DEPTH):
                cp = pltpu.make_async_remote_copy(
                    src_ref=out_refs[k].at[l, rows], dst_ref=out_refs[k].at[l, rows], send_sem=ssem.at[k, l],
                    recv_sem=rsem.at[k, l], device_id=(x, y, 1 - c), device_id_type=MESH)
                cp.start()
                cps.append(cp)
        for cp in cps:
            cp.wait()

    return pl.pallas_call(
        body, name="share_with_sibling", in_specs=[ANY] * n, out_specs=[ANY] * n,
        out_shape=[jax.ShapeDtypeStruct(a.shape, F32) for a in parts],
        input_output_aliases={k: k for k in range(n)},
        scratch_shapes=[pltpu.SemaphoreType.DMA((n, DEPTH)), pltpu.SemaphoreType.DMA((n, DEPTH))],
    )(*parts)


def add_core_halves(g, got, cidx, name):
    nch, r, cdim = g.shape
    half = r // 2
    tr = _row_tile(half, cdim, mult=16)
    per = half // tr

    def body(c_ref, a_ref, b_ref, o_ref):
        o_ref[...] = _bf(a_ref[...] + b_ref[...])

    grid_spec = pltpu.PrefetchScalarGridSpec(
        num_scalar_prefetch=1, grid=(nch, per),
        in_specs=[pl.BlockSpec((None, tr, cdim), lambda jj, i, c_ref: (jj, c_ref[0] * per + i, 0)),
                  pl.BlockSpec((None, tr, cdim), lambda jj, i, c_ref: (jj, i, 0))],
        out_specs=pl.BlockSpec((None, tr, cdim), lambda jj, i, c_ref: (jj, i, 0)))
    return pl.pallas_call(
        body, name=name, grid_spec=grid_spec,
        out_shape=jax.ShapeDtypeStruct((nch, half, cdim), BF16), compiler_params=_cparams(2),
    )(cidx, g, got)


def add_chip_parts(s1, got, jc, layer, into, name):
    _, half, cdim = s1.shape
    tr = _row_tile(half, cdim, mult=16)
    per = half // tr

    def body(jc_ref, a_ref, g0_ref, g1_ref, g2_ref, *rest):
        rest[-1][...] = ((a_ref[...].astype(F32) + g0_ref[...].astype(F32)) + g1_ref[...].astype(F32)) + g2_ref[...].astype(F32)

    def slot(k):
        return pl.BlockSpec((None, tr, cdim), lambda i, jc_ref: (k, i, 0))

    in_specs = [pl.BlockSpec((None, tr, cdim), lambda i, jc_ref: (jc_ref[0], i, 0)), slot(0), slot(1), slot(2)]
    ops = [jc, s1, got, got, got]
    aliases = {}
    if into is not None:
        in_specs.append(ANY)
        ops.append(into)
        aliases = {5: 0}
    grid_spec = pltpu.PrefetchScalarGridSpec(
        num_scalar_prefetch=1, grid=(per,), in_specs=in_specs,
        out_specs=pl.BlockSpec((None, tr, cdim), lambda i, jc_ref: (layer, jc_ref[1] * per + i, 0)))
    return pl.pallas_call(
        body, name=name, grid_spec=grid_spec, input_output_aliases=aliases,
        out_shape=jax.ShapeDtypeStruct((DEPTH, 2 * half, cdim), F32), compiler_params=_cparams(1),
    )(*ops)


def sum_devices(allp):
    _, r, _ = allp.shape

    def body(a_ref, o_ref):
        tot = a_ref[0]
        for k in range(1, 8):
            tot = tot + a_ref[k]
        o_ref[...] = tot

    tr = r // 2 if r % 16 == 0 else r
    return pl.pallas_call(
        body, name="sum_devices", grid=(r // tr,),
        in_specs=[pl.BlockSpec((8, tr, LANES), lambda i: (0, i, 0))],
        out_specs=pl.BlockSpec((tr, LANES), lambda i: (i, 0)),
        out_shape=jax.ShapeDtypeStruct((r, LANES), F32), compiler_params=_cparams(1),
    )(allp)


def _row_tile(r, cdim, limit_bytes=1 << 20, mult=8):
    best = None
    for tr in range(mult, r + 1, mult):
        if r % tr == 0 and tr * cdim * 4 <= limit_bytes:
            best = tr
    return best if best is not None else r


def adamw(w, g, m, v, name):
    r, cdim = w.shape
    tr = _row_tile(r, cdim)
    bc1 = 1.0 - ADAM_B1 ** ADAM_STEP
    bc2 = 1.0 - ADAM_B2 ** ADAM_STEP

    def body(w_ref, g_ref, m_ref, v_ref, d_ref, nm_ref, nv_ref, go_ref):
        gv = g_ref[...]
        nm = ADAM_B1 * m_ref[...] + (1.0 - ADAM_B1) * gv
        nv = ADAM_B2 * v_ref[...] + (1.0 - ADAM_B2) * (gv * gv)
        d_ref[...] = -ADAM_LR * ((nm / bc1) / (jnp.sqrt(nv / bc2) + ADAM_EPS) + ADAM_WD * w_ref[...])
        nm_ref[...] = nm
        nv_ref[...] = nv
        go_ref[...] = gv

    blk = pl.BlockSpec((tr, cdim), lambda i: (i, 0))
    shape = jax.ShapeDtypeStruct((r, cdim), F32)
    return pl.pallas_call(
        body, name=name, grid=(r // tr,), in_specs=[blk] * 4, out_specs=[blk] * 4, out_shape=[shape] * 4,
        compiler_params=_cparams(1),
    )(w, g, m, v)


def _s5_prepare(lam_re, lam_im, log_dt, b_re, b_im, c_re, c_im):
    groups, ch = 16, 16
    dt = jnp.exp(log_dt)[:, None]
    mag = jnp.exp(lam_re * dt)
    a_r, a_i = mag * jnp.cos(lam_im * dt), mag * jnp.sin(lam_im * dt)
    den = lam_re * lam_re + lam_im * lam_im
    q_r = ((a_r - 1.0) * lam_re + a_i * lam_im) / den
    q_i = (a_i * lam_re - (a_r - 1.0) * lam_im) / den
    bb_r = q_r[..., None] * b_re - q_i[..., None] * b_im
    bb_i = q_r[..., None] * b_im + q_i[..., None] * b_re
    eye = jnp.eye(groups, dtype=F32)

    def expand_b(bb):
        return jnp.einsum("gpc,gh->gchp", bb, eye).reshape(groups * ch, N_STATE)

    def expand_c(cc):
        return jnp.einsum("gcp,gh->hpgc", cc, eye).reshape(N_STATE, groups * ch)

    a2 = jnp.concatenate([a_r.reshape(1, N_STATE), a_i.reshape(1, N_STATE)], axis=1)
    bexp = jnp.concatenate([expand_b(bb_r), expand_b(bb_i)], axis=1)
    cexp = jnp.concatenate([expand_c(c_re), -expand_c(c_im)], axis=0)
    return a2, bexp, cexp


def _lru_prepare(w_r, w_i, lam):
    heads = 4
    eye = jnp.eye(heads, dtype=F32)

    def expand(w):
        return jnp.einsum("hij,hk->hikj", w, eye).reshape(D_GROUP, D_GROUP)

    return expand(w_r), expand(w_i), jax.nn.softplus(-lam).reshape(1, D_GROUP)


def _pad_rows(a, rows):
    return jnp.pad(a, ((0, rows - a.shape[0]), (0, 0)))


def _group_mean_matrix():
    gidx = jnp.arange(D_GROUP) // 64
    return (gidx[:, None] == gidx[None, :]).astype(BF16) * jnp.asarray(1.0 / 64.0, BF16)


def _pack_rows(arrs, width):
    parts = []
    for a in arrs:
        flat = a.reshape(-1)
        pad = (-flat.shape[0]) % width
        parts.append(jnp.pad(flat, (0, pad)) if pad else flat)
    flat = jnp.concatenate(parts)
    rows = flat.shape[0] // width
    pad_rows = (-rows) % 16
    if pad_rows:
        flat = jnp.pad(flat, (0, pad_rows * width))
    return flat.reshape(-1, width)


def _unpack_rows(packed, shapes, width):
    flat = packed.reshape(-1)
    out, off = [], 0
    for shp in shapes:
        size = math.prod(shp)
        out.append(flat[off:off + size].reshape(shp))
        off += size + ((-size) % width)
    return out


def kernel(x, mem, ln_in_g, ln_in_b, w_in, b_in, s5_lam_re, s5_lam_im, s5_log_dt, s5_b_re, s5_b_im, s5_c_re, s5_c_im, s5_d, s5_w_glu, s5_b_glu, cv_w, cv_b, cv_gn_g, cv_gn_b, cv_w_pw, cv_b_pw, lru_conv_w, lru_conv_b, lru_w_r, lru_b_r, lru_w_i, lru_b_i, lru_lam, attn_w_kv, w_out, b_out, ln1_g, ln1_b, ffn_w_up, ffn_conv_w, ffn_conv_b, ffn_w_down, ln2_g, ln2_b, loss_target, m_ln_in_g, m_ln_in_b, m_w_in, m_b_in, m_s5_lam_re, m_s5_lam_im, m_s5_log_dt, m_s5_b_re, m_s5_b_im, m_s5_c_re, m_s5_c_im, m_s5_d, m_s5_w_glu, m_s5_b_glu, m_cv_w, m_cv_b, m_cv_gn_g, m_cv_gn_b, m_cv_w_pw, m_cv_b_pw, m_lru_conv_w, m_lru_conv_b, m_lru_w_r, m_lru_b_r, m_lru_w_i, m_lru_b_i, m_lru_lam, m_attn_w_kv, m_w_out, m_b_out, m_ln1_g, m_ln1_b, m_ffn_w_up, m_ffn_conv_w, m_ffn_conv_b, m_ffn_w_down, m_ln2_g, m_ln2_b, v_ln_in_g, v_ln_in_b, v_w_in, v_b_in, v_s5_lam_re, v_s5_lam_im, v_s5_log_dt, v_s5_b_re, v_s5_b_im, v_s5_c_re, v_s5_c_im, v_s5_d, v_s5_w_glu, v_s5_b_glu, v_cv_w, v_cv_b, v_cv_gn_g, v_cv_gn_b, v_cv_w_pw, v_cv_b_pw, v_lru_conv_w, v_lru_conv_b, v_lru_w_r, v_lru_b_r, v_lru_w_i, v_lru_b_i, v_lru_lam, v_attn_w_kv, v_w_out, v_b_out, v_ln1_g, v_ln1_b, v_ffn_w_up, v_ffn_conv_w, v_ffn_conv_b, v_ffn_w_down, v_ln2_g, v_ln2_b):
    p = dict(locals())
    xs = x[0]
    mems = mem[0]
    target = loss_target[0]
    s = xs.shape[0]
    cidx = lax.axis_index("c")
    jidx = 2 * lax.axis_index("x") + lax.axis_index("y")
    tb_scan = min(512, s)
    tb_s5 = min(512, s)
    tb_attn = min(1024, s)
    tb_ffn = min(512, s)

    small_sh_names = list(SMALL_SHARDED)
    small_sh_shapes = [p[nm].shape[1:] for nm in small_sh_names]
    slabs = [[_own_slab(_bf(p[nm][l]), jidx) for nm in BIG]
             + [_own_slab(_pack_rows([p[nm][l] for nm in small_sh_names], LANES), jidx)] for l in range(DEPTH)]
    n_slabs = len(BIG) + 1
    first_needed = FIRST_NEEDED + [len(BIG)]
    arrive_later = [k for k in range(n_slabs) if k not in first_needed]
    gathered = [[None] * n_slabs for _ in range(DEPTH)]
    for k, slab in zip(first_needed, gather_weights([slabs[0][k] for k in first_needed])):
        gathered[0][k] = slab

    def weight_views(gw, which):
        shapes = {0: (1, N_CHIPS, D_MODEL, N_IN // N_CHIPS), 1: (1, 1, D_MODEL, 2 * D_GROUP), 2: (1, 1, D_MODEL, D_MODEL),
                  3: (1, N_CHIPS, D_MODEL, 2 * D_FF // N_CHIPS), 4: (1, 1, D_FF, D_MODEL),
                  5: (1, D_GROUP, D_GROUP), 6: (1, D_GROUP, D_GROUP)}
        views = {BIG[k]: gw[k].reshape(shapes[k]) for k in which if k < len(BIG)}
        if len(BIG) in which:
            per_chip = [_unpack_rows(gw[len(BIG)][jj], small_sh_shapes, LANES) for jj in range(N_CHIPS)]
            for k, nm in enumerate(small_sh_names):
                views[nm] = jnp.concatenate([per_chip[jj][k] for jj in range(N_CHIPS)], axis=SMALL_SHARDED[nm] - 1)
        return views

    pmat = _group_mean_matrix()

    def vec(a):
        return a.reshape(1, -1)

    xh0, rs0, xb0 = ln_fwd(xs, vec(ln_in_g), vec(ln_in_b), "ln_in")
    saved = []
    prev = dict(xh=xh0, rs=rs0, xb=xb0, g=vec(ln_in_g), b=vec(ln_in_b))
    for l in range(DEPTH):
        sv = dict(prev=prev)
        (a2, bexp, cexp), sv['s5_vjp'] = jax.vjp(_s5_prepare, s5_lam_re[l], s5_lam_im[l], s5_log_dt[l],
                                                 s5_b_re[l], s5_b_im[l], s5_c_re[l], s5_c_im[l])
        (wr, wi, sp), sv['lru_vjp'] = jax.vjp(_lru_prepare, lru_w_r[l], lru_w_i[l], lru_lam[l])
        sv.update(a2=a2, bexp=_bf(bexp), cexp=_bf(cexp), wr=_bf(wr), wi=_bf(wi), sp=sp)
        late = arrive_later if l == 0 else []
        gw = sv['gw'] = weight_views(gathered[l], [k for k in range(n_slabs) if k not in late])
        sv['cvw'] = _pad_rows(gw['cv_w'], CV_HALO)
        sv['lcw'] = _pad_rows(gw['lru_conv_w'], LRU_HALO)
        sv['fcw'] = _pad_rows(gw['ffn_conv_w'], FFN_TAP_ROWS)
        sv['w_glu'], sv['w_pw'] = gw['s5_w_glu'], gw['cv_w_pw']
        h_in = mm_nn(prev['xb'], gw['w_in'], 0, vec(b_in[l]), F32, 2048, f"in_proj{l}")
        kv = mm_nn(mems, gw['attn_w_kv'], 0, jnp.zeros((1, 2 * D_GROUP), F32), F32, 256, f"kv_proj{l}")
        y_s5, hst, y0, *got = s5_fwd(h_in, a2, sv['bexp'], sv['cexp'], vec(s5_d[l]), sv['w_glu'], 0, vec(s5_b_glu[l]),
                                     tb_s5, f"s5_fwd{l}", gather=[slabs[l][k] for k in late])
        for k, slab in zip(late, got):
            gathered[l][k] = slab
        gw.update(weight_views(gathered[l], late))
        y_cv, hc = cv_fwd(h_in, sv['cvw'], vec(cv_b[l]), vec(cv_gn_g[l]), vec(cv_gn_b[l]), pmat, sv['w_pw'], 0,
                          vec(cv_b_pw[l]), tb_scan, f"cv_fwd{l}")
        y_lru, xcs, hls = lru_fwd(h_in, sv['lcw'], vec(lru_conv_b[l]), sv['wr'], vec(lru_b_r[l]), sv['wi'],
                                  vec(lru_b_i[l]), sp, tb_scan, f"lru_fwd{l}")
        y_mem = attn_fwd(h_in, kv, tb_attn, f"attn_fwd{l}")
        mix_in = jnp.concatenate([y_s5, y_cv, y_lru, y_mem], axis=1)
        xh1, rs1, xb1 = proj_ln(mix_in, gw['w_out'].reshape(1, D_MODEL, D_MODEL), 0, vec(b_out[l]),
                                prev['xh'], prev['g'], prev['b'], vec(ln1_g[l]), vec(ln1_b[l]), f"out_proj_ln{l}")
        u = mm_nn(xb1, gw['ffn_w_up'], 0, jnp.zeros((1, 2 * D_FF), F32), F32, 1024, f"ffn_up{l}")
        nxt = slabs[l + 1] if l + 1 < DEPTH else ()
        hf, sv['ffn_kept'], *got = ffn_act_fwd(u, sv['fcw'], vec(ffn_conv_b[l]), tb_ffn, f"ffn_act{l}", gather=nxt)
        if nxt:
            gathered[l + 1] = got
        xh2, rs2, xb2 = proj_ln(hf, gw['ffn_w_down'].reshape(1, D_FF, D_MODEL), 0, jnp.zeros((1, D_MODEL), F32),
                                xh1, vec(ln1_g[l]), vec(ln1_b[l]), vec(ln2_g[l]), vec(ln2_b[l]), f"ffn_down_ln{l}")
        sv.update(h_in=h_in, kv=kv, hst=hst, y0=y0, hc=hc, xcs=xcs, hls=hls, mix_in=mix_in,
                  xh1=xh1, rs1=rs1, xb1=xb1, u=u, hf=hf, xh2=xh2, rs2=rs2)
        saved.append(sv)
        prev = dict(xh=xh2, rs=rs2, xb=xb2, g=vec(ln2_g[l]), b=vec(ln2_b[l]))

    grads = {}
    per_layer = {nm: [None] * DEPTH for nm in WEIGHTS if nm not in ('ln_in_g', 'ln_in_b')}
    c1 = cidx.reshape(1).astype(jnp.int32)
    jc = jnp.stack([jidx, cidx]).astype(jnp.int32)
    red_big = [None] * len(BIG)
    handoff = None
    pending = None
    below = None

    def per_chip(gl, which):
        return [gl[k].reshape((N_CHIPS,) + p[BIG[k]].shape[1:]) for k in which]

    def core_sums(gs, got, which, tag):
        return [add_core_halves(g, ga, c1, f"add_cores_{BIG[k]}{tag}") for g, ga, k in zip(gs, got, which)]

    def chip_parts(gl, which, tag, small_all=None):
        gs = per_chip(gl, which)
        got = exchange_cores(gs, f"exchange_cores{tag}", small_all)
        return core_sums(gs, got, which, tag), (got[len(which)] if small_all is not None else None)

    def own_sum(which, parts, got, l):
        for k, part, gk in zip(which, parts, got):
            red_big[k] = add_chip_parts(part, gk, jc, l, red_big[k], f"add_chips_{BIG[k]}{l}")

    everything = list(range(len(BIG)))
    ready_early = [2, 3, 4]
    ready_last = [k for k in everything if k not in ready_early]

    for l in reversed(range(DEPTH)):
        sv = saved[l]
        pv = sv['prev']
        gw = sv['gw']
        gl = [None] * len(BIG)
        if l == DEPTH - 1:
            dr2, dg2, db2, sqerr, dr2b = loss_ln_bwd(target, sv['xh2'], sv['rs2'], vec(ln2_g[l]), vec(ln2_b[l]), "loss_ln2_bwd")
            loss_local = 0.5 / D_MODEL * jnp.sum(sqerr)
        else:
            dr2, dg2, db2, dr2b = below
        per_layer['ln2_g'][l], per_layer['ln2_b'][l] = dg2[0], db2[0]
        tm_nt = min(512, s)
        ts_big = min(2048, s)
        whole = lambda a_ref, j: a_ref[...]
        dhf = mm_nt(dr2b, (tm_nt, D_MODEL), lambda i: (i, 0), whole, gw['ffn_w_down'], 0, None, F32, 512, s, f"ffn_down_dx{l}")
        above = per_chip(*handoff) if handoff is not None else []
        res = mm_tn(sv['hf'], dr2b, (min(1024, s), D_MODEL), lambda kt, j, st: (st, 0), 1, D_MODEL, FFN_TN, 1024, s,
                    f"ffn_down_dw{l}", exchange=above)
        gl[4] = res[0] if above else res
        if handoff is not None:
            pending = (l + 1, handoff[1], core_sums(above, res[1:], handoff[1], str(l + 1)))
            handoff = None
        waiting = pending[2] if pending is not None else ()
        du, dcwv, dcwg, *got = ffn_act_bwd(dhf, sv['u'], sv['ffn_kept'], sv['fcw'], tb_ffn, f"ffn_act_bwd{l}",
                                           scatter=waiting)
        if pending is not None:
            own_sum(pending[1], pending[2], got, pending[0])
            pending = None
        dcw = jnp.concatenate([dcwv, dcwg], axis=1)
        per_layer['ffn_conv_w'][l] = dcw[0:FFN_CONV_WIDTH]
        per_layer['ffn_conv_b'][l] = dcw[FFN_CONV_WIDTH]
        dr1, dg1, db1, cs1, dr1b = mm_nt(du, (2, tm_nt, D_FF), lambda i: (0, i, 0),
                                   lambda a_ref, j: a_ref[j // 2, :, (j % 2) * FFN_TN:(j % 2 + 1) * FFN_TN], gw['ffn_w_up'], 0, dr2,
                                   F32, 512, s, f"ffn_up_dx_ln1_bwd{l}", ln=(sv['xh1'], sv['rs1'], vec(ln1_g[l])))
        gl[3] = mm_tn(sv['xb1'], du, (None, ts_big, FFN_TN), lambda kt, j, st: (j // 2, st, j % 2), N_CHIPS, FFN_TN,
                      D_MODEL, 2048, s, f"ffn_up_dw{l}")
        per_layer['ln1_g'][l], per_layer['ln1_b'][l], per_layer['b_out'][l] = dg1[0], db1[0], cs1[0]
        dmix = mm_nt(dr1b, (tm_nt, D_MODEL), lambda i: (i, 0), whole, gw['w_out'], 0, None, F32, 512, s, f"out_proj_dx{l}")
        gl[2] = mm_tn(sv['mix_in'], dr1b, (min(1024, s), D_MODEL), lambda kt, j, st: (st, 0), 1, D_MODEL, D_MODEL, 1024, s,
                      f"out_proj_dw{l}")
        h_in = sv['h_in']
        early_gs = per_chip(gl, ready_early) if l == 0 else []
        d_q, cs_q, d_kv, *got = attn_bwd(dmix, h_in, sv['kv'], tb_attn, f"attn_bwd{l}", exchange=early_gs)
        early = core_sums(early_gs, got, ready_early, f"{l}a") if l == 0 else ()
        (d_u, cs_u, d_bexp, d_cexp, d_dd, d_wglu, d_bglu, d_a2) = s5_bwd(
            dmix, h_in, sv['y0'], sv['hst'], sv['a2'], sv['bexp'], sv['cexp'], vec(s5_d[l]), sv['w_glu'], 0,
            vec(s5_b_glu[l]), tb_s5, f"s5_bwd{l}")
        (d_vg, cs_vg, d_cvw, d_cvb, d_gg, d_gb, d_wpw, d_bpw, *got) = cv_bwd(
            dmix, h_in, sv['hc'], sv['cvw'], vec(cv_gn_g[l]), vec(cv_gn_b[l]), pmat, sv['w_pw'], 0, tb_scan, f"cv_bwd{l}",
            scatter=early)
        if l == 0:
            own_sum(ready_early, early, got, l)
        (d_lx, cs_lx, d_lcw, d_lcb, d_wr, d_br, d_wi, d_bi, d_sp) = lru_bwd(
            dmix, h_in, sv['xcs'], sv['hls'], sv['lcw'], sv['wr'], vec(lru_b_r[l]), sv['wi'], vec(lru_b_i[l]),
            sv['sp'], tb_scan, f"lru_bwd{l}")
        g_s5 = sv['s5_vjp']((d_a2, d_bexp, d_cexp))
        for nm, gval in zip(['s5_lam_re', 's5_lam_im', 's5_log_dt', 's5_b_re', 's5_b_im', 's5_c_re', 's5_c_im'], g_s5):
            per_layer[nm][l] = gval
        g_lru = sv['lru_vjp']((d_wr, d_wi, d_sp))
        for nm, gval in zip(['lru_w_r', 'lru_w_i', 'lru_lam'], g_lru):
            per_layer[nm][l] = gval
        per_layer['s5_d'][l], per_layer['s5_b_glu'][l] = d_dd[0], d_bglu[0]
        per_layer['cv_w'][l], per_layer['cv_b'][l] = d_cvw[0:CONV_WIDTH], d_cvb[0]
        per_layer['cv_gn_g'][l], per_layer['cv_gn_b'][l] = d_gg[0], d_gb[0]
        per_layer['cv_b_pw'][l] = d_bpw[0]
        gl[5], gl[6] = d_wglu, d_wpw
        per_layer['lru_conv_w'][l], per_layer['lru_conv_b'][l] = d_lcw[0:LRU_CONV_WIDTH], d_lcb[0]
        per_layer['lru_b_r'][l], per_layer['lru_b_i'][l] = d_br[0], d_bi[0]
        per_layer['b_in'][l] = jnp.concatenate([cs_u, cs_vg, cs_lx, cs_q], axis=1)[0]
        gl[1] = mm_tn(mems, d_kv, (MEM_ROWS, 2 * D_GROUP), lambda kt, j, st: (st, 0), 1, 2 * D_GROUP, D_MODEL, MEM_ROWS,
                      MEM_ROWS, f"kv_proj_dw{l}")
        dh_in = jnp.concatenate([d_u, d_vg, d_lx, d_q], axis=1)
        n_sh = N_IN // N_CHIPS
        under = mm_nt(dh_in, (tm_nt, N_IN), lambda i: (i, 0), lambda a_ref, j: a_ref[:, j * n_sh:(j + 1) * n_sh],
                      gw['w_in'], 0, dr1, F32, 512, s, f"in_proj_dx_ln_bwd{l}", ln=(pv['xh'], pv['rs'], pv['g']))
        below = (under[0], under[1], under[2], under[4])
        gl[0] = mm_tn(pv['xb'], dh_in, (ts_big, n_sh), lambda kt, j, st: (st, j), N_CHIPS, n_sh, D_MODEL, 2048, s,
                      f"in_proj_dw{l}")
        if l > 0:
            handoff = (gl, everything)
    grad_x, dg_in, db_in = below[:3]
    grads['ln_in_g'], grads['ln_in_b'] = dg_in[0], db_in[0]
    for nm, vals in per_layer.items():
        if nm not in BIG:
            grads[nm] = jnp.stack(vals)

    small_names = [nm for nm in WEIGHTS if nm not in BIG]
    small_local = _pack_rows([grads[nm] for nm in small_names], LANES)
    me = 4 * lax.axis_index("x") + 2 * lax.axis_index("y") + cidx
    small_slab = lax.dynamic_update_slice_in_dim(lax.empty((8,) + small_local.shape, F32), small_local[None], me, axis=0)
    parts, small_all = chip_parts(gl, ready_last, "0b", small_slab)
    own_sum(ready_last, parts, scatter_shards(parts, "scatter_shards0"), 0)
    red_big = share_with_sibling(red_big)
    small_red = sum_devices(small_all)
    small_grads = dict(zip(small_names, _unpack_rows(small_red, [grads[nm].shape for nm in small_names], LANES)))

    out_g, out_d, out_m, out_v = {}, {}, {}, {}
    for k, nm in enumerate(BIG):
        gk = red_big[k]
        two_d = (-1, p[nm].shape[-1])
        res = adamw(p[nm].reshape(two_d), gk.reshape(two_d), p['m_' + nm].reshape(two_d),
                    p['v_' + nm].reshape(two_d), f"adamw_{nm}")
        out_d[nm], out_m[nm], out_v[nm], out_g[nm] = (t.reshape(p[nm].shape) for t in res)
    own = {}
    for nm in small_names:
        gfull = small_grads[nm]
        if nm in SMALL_SHARDED:
            ax = SMALL_SHARDED[nm]
            width = p[nm].shape[ax]
            gfull = lax.dynamic_slice_in_dim(gfull, jidx * width, width, axis=ax)
        own[nm] = gfull
    packs = [_pack_rows([src[nm] for nm in small_names], LANES)
             for src in (dict((nm, p[nm]) for nm in small_names), own,
                         dict((nm, p['m_' + nm]) for nm in small_names), dict((nm, p['v_' + nm]) for nm in small_names))]
    dlt, nm_, nv_, _ = adamw(packs[0], packs[1], packs[2], packs[3], "adamw_small")
    shapes = [p[nm].shape for nm in small_names]
    for dst, packed in ((out_d, dlt), (out_m, nm_), (out_v, nv_)):
        dst.update(zip(small_names, _unpack_rows(packed, shapes, LANES)))
    out_g.update(own)

    loss = lax.psum(loss_local, ("x", "y", "c"))
    return (loss, grad_x[None], *[out_g[nm] for nm in WEIGHTS], *[out_d[nm] for nm in WEIGHTS],
            *[out_m[nm] for nm in WEIGHTS], *[out_v[nm] for nm in WEIGHTS])
```
